```python
import math
import jax
import jax.numpy as jnp
from jax import lax
import numpy as np

D_MODEL = 1024
BATCH = 8
SEQ = 8192
DEPTH = 2

HEAD_DIM = 64
MIX_HALF = D_MODEL // 2
GLA_HEADS = MIX_HALF // HEAD_DIM
FOX_HEADS = MIX_HALF // HEAD_DIM
GLA_RANK = 16
GLA_TAU = 16.0
GLA_CHUNK = 64
FOX_BLOCK = 128
S5_GROUP_WIDTH = 16
S5_GROUPS = MIX_HALF // S5_GROUP_WIDTH
S5_STATE = 64
S5_CHUNK = 128
SGU_GROUPS = 8
SGU_GROUP_WIDTH = MIX_HALF // SGU_GROUPS
SGU_CHUNK = 128
D_FF = 4 * D_MODEL
N_EVEN = (DEPTH + 1) // 2
N_ODD = DEPTH // 2
EPS = 1e-6
EVEN_SIZES = (MIX_HALF, MIX_HALF, MIX_HALF, MIX_HALF, GLA_RANK, MIX_HALF, MIX_HALF, MIX_HALF, FOX_HEADS)
EVEN_SPLITS = tuple(int(v) for v in np.cumsum(EVEN_SIZES)[:-1])
EVEN_WIDTH = int(sum(EVEN_SIZES))
ODD_WIDTH = 3 * MIX_HALF

kernel_name = "hybrid_gla_fox_s5_sgu_adaln_trunk"


def _rms(x):
    x = x.astype(jnp.float32)
    return x * lax.rsqrt(jnp.mean(x * x, axis=-1, keepdims=True) + EPS)


def _gla(q, k, v, log_a):
    B, S, H, Dh = q.shape
    n = S // GLA_CHUNK

    def chunks(t):
        return t.astype(jnp.float32).reshape(B, n, GLA_CHUNK, H, Dh).transpose(1, 0, 3, 2, 4)

    q = q.astype(jnp.float32) * (Dh ** -0.5)
    mask = jnp.tril(jnp.ones((GLA_CHUNK, GLA_CHUNK), dtype=bool))

    def step(state, inp):
        qc, kc, vc, lc = inp
        bc = jnp.cumsum(lc, axis=2)
        o_inter = jnp.einsum('bhtk,bhkv->bhtv', qc * jnp.exp(bc), state)
        diff = bc[:, :, :, None, :] - bc[:, :, None, :, :]
        decay = jnp.exp(jnp.where(mask[:, :, None], diff, -jnp.inf))
        scores = jnp.einsum('bhtk,bhtsk,bhsk->bhts', qc, decay, kc)
        o = o_inter + jnp.einsum('bhts,bhsv->bhtv', scores, vc)
        b_last = bc[:, :, -1:, :]
        state = (jnp.exp(b_last[:, :, 0, :])[..., None] * state
                 + jnp.einsum('bhsk,bhsv->bhkv', kc * jnp.exp(b_last - bc), vc))
        return state, o

    state0 = jnp.zeros((B, H, Dh, Dh), jnp.float32)
    _, o = lax.scan(step, state0, (chunks(q), chunks(k), chunks(v), chunks(log_a)))
    return o.transpose(1, 0, 3, 2, 4).reshape(B, S, H, Dh)


def _fox(q, k, v, f_logit, q_gain, k_gain):
    B, S, H, Dh = q.shape
    q = (_rms(q) * q_gain).transpose(0, 2, 1, 3)
    k = (_rms(k) * k_gain).transpose(0, 2, 1, 3)
    v = v.astype(jnp.float32).transpose(0, 2, 1, 3)
    cum = jnp.cumsum(jax.nn.log_sigmoid(f_logit.astype(jnp.float32)), axis=1).transpose(0, 2, 1)
    nb = S // FOX_BLOCK
    qb = q.reshape(B, H, nb, FOX_BLOCK, Dh).transpose(2, 0, 1, 3, 4)
    cb = cum.reshape(B, H, nb, FOX_BLOCK).transpose(2, 0, 1, 3)
    pos = jnp.arange(S, dtype=jnp.int32)
    pb = pos.reshape(nb, FOX_BLOCK)
    scale = Dh ** -0.5

    def block(args):
        qi, ci, pi = args
        logits = (jnp.einsum('bhqd,bhkd->bhqk', qi, k) * scale
                  + ci[..., None] - cum[:, :, None, :])
        logits = jnp.where(pi[:, None] >= pos[None, :], logits, -jnp.inf)
        return jnp.einsum('bhqk,bhkd->bhqd', jax.nn.softmax(logits, axis=-1), v)

    out = lax.map(block, (qb, cb, pb))
    return out.transpose(1, 0, 3, 2, 4).reshape(B, S, H, Dh)


def _ssm_combine(e1, e2):
    a1r, a1i, b1r, b1i = e1
    a2r, a2i, b2r, b2i = e2
    return (a2r * a1r - a2i * a1i,
            a2r * a1i + a2i * a1r,
            a2r * b1r - a2i * b1i + b2r,
            a2r * b1i + a2i * b1r + b2i)


def _s5(u, lam_re, lam_im, log_dt, b_re, b_im, c_re, c_im, d_skip):
    B, S, _ = u.shape
    f32 = jnp.float32
    lam_re, lam_im, b_re, b_im, c_re, c_im, d_skip = (
        t.astype(f32) for t in (lam_re, lam_im, b_re, b_im, c_re, c_im, d_skip))
    dt = jnp.exp(log_dt.astype(f32))[:, None]
    mag = jnp.exp(lam_re * dt)
    ang = lam_im * dt
    abar_re = mag * jnp.cos(ang)
    abar_im = mag * jnp.sin(ang)
    den = lam_re * lam_re + lam_im * lam_im
    coef_re = ((abar_re - 1.0) * lam_re + abar_im * lam_im) / den
    coef_im = (abar_im * lam_re - (abar_re - 1.0) * lam_im) / den
    bbar_re = coef_re[..., None] * b_re - coef_im[..., None] * b_im
    bbar_im = coef_re[..., None] * b_im + coef_im[..., None] * b_re
    n = S // S5_CHUNK
    uc = u.astype(f32).reshape(B, n, S5_CHUNK, S5_GROUPS, S5_GROUP_WIDTH).transpose(1, 0, 2, 3, 4)

    def step(carry, u_chunk):
        x_re0, x_im0 = carry
        bu_re = jnp.einsum('bcgi,gpi->bcgp', u_chunk, bbar_re)
        bu_im = jnp.einsum('bcgi,gpi->bcgp', u_chunk, bbar_im)
        a_re = jnp.broadcast_to(abar_re, bu_re.shape)
        a_im = jnp.broadcast_to(abar_im, bu_re.shape)
        acc_re, acc_im, x_re, x_im = lax.associative_scan(
            _ssm_combine, (a_re, a_im, bu_re, bu_im), axis=1)
        x_re = x_re + acc_re * x_re0[:, None] - acc_im * x_im0[:, None]
        x_im = x_im + acc_re * x_im0[:, None] + acc_im * x_re0[:, None]
        y = (jnp.einsum('bcgp,gip->bcgi', x_re, c_re)
             - jnp.einsum('bcgp,gip->bcgi', x_im, c_im)
             + d_skip * u_chunk)
        return (x_re[:, -1], x_im[:, -1]), y

    zeros = jnp.zeros((B, S5_GROUPS, S5_STATE), f32)
    _, y = lax.scan(step, (zeros, zeros), uc)
    return y.transpose(1, 0, 2, 3, 4).reshape(B, S, MIX_HALF)


def _sgu(z, ln_gain, ln_bias, w_s, b_s):
    B, S, _ = z.shape
    z = jax.nn.gelu(z.astype(jnp.float32))
    u, v = z[..., :MIX_HALF], z[..., MIX_HALF:]
    mu = jnp.mean(v, axis=-1, keepdims=True)
    var = jnp.mean(jnp.square(v - mu), axis=-1, keepdims=True)
    v = (v - mu) * lax.rsqrt(var + EPS) * ln_gain + ln_bias
    n = S // SGU_CHUNK
    v = v.reshape(B, n, SGU_CHUNK, SGU_GROUPS, SGU_GROUP_WIDTH)
    mask = jnp.tril(jnp.ones((SGU_CHUNK, SGU_CHUNK), dtype=bool))
    w = jnp.where(mask[None], w_s.astype(jnp.float32), 0.0)
    mixed = jnp.einsum('gts,bnsgc->bntgc', w, v) + b_s.astype(jnp.float32).T[None, None, :, :, None]
    return u * mixed.reshape(B, S, MIX_HALF)


def _even_mixer(h, w_in, w_out, w_lr, b_lr, gla_gain, b_f, q_gain, k_gain):
    B, S, _ = h.shape
    proj = jnp.einsum('bsd,de->bse', h, w_in)
    gq, gk, gv, gg, glr, fq, fk, fv, ff = jnp.split(proj, EVEN_SPLITS, axis=-1)

    def heads(t):
        return t.reshape(B, S, -1, HEAD_DIM)

    log_a = jax.nn.log_sigmoid((jnp.einsum('bsr,re->bse', glr, w_lr) + b_lr).astype(jnp.float32)) / GLA_TAU
    o_gla = _gla(heads(gq), heads(gk), heads(gv), heads(log_a))
    o_gla = _rms(o_gla) * gla_gain * jax.nn.silu(heads(gg).astype(jnp.float32))
    o_fox = _fox(heads(fq), heads(fk), heads(fv), ff + b_f, q_gain, k_gain)
    mixed = jnp.concatenate([o_gla.reshape(B, S, -1), o_fox.reshape(B, S, -1)], axis=-1)
    return jnp.einsum('bse,ed->bsd', mixed.astype(h.dtype), w_out)


def _odd_mixer(h, w_in, w_out, lam_re, lam_im, log_dt, b_re, b_im, c_re, c_im, d_skip,
               w_glu, b_glu, ln_gain, ln_bias, w_s, b_s):
    proj = jnp.einsum('bsd,de->bse', h, w_in)
    s5_in, sgu_z = proj[..., :MIX_HALF], proj[..., MIX_HALF:]
    y = jax.nn.gelu(_s5(s5_in, lam_re, lam_im, log_dt, b_re, b_im, c_re, c_im, d_skip))
    y = y * jax.nn.sigmoid(jnp.einsum('bse,ef->bsf', y, w_glu.astype(jnp.float32)) + b_glu)
    y_sgu = _sgu(sgu_z, ln_gain, ln_bias, w_s, b_s)
    mixed = jnp.concatenate([y, y_sgu], axis=-1)
    return jnp.einsum('bse,ed->bsd', mixed.astype(h.dtype), w_out)


def _fwd_setup_inputs(seed: int = 0) -> dict:
    key = jax.random.key(seed)
    ks = jax.random.split(key, 30)
    f32 = jnp.float32

    def nrm(k, shape, scale):
        return jax.random.normal(k, shape, f32) * scale

    n_idx = jnp.arange(S5_STATE, dtype=f32)
    return {
        "x": nrm(ks[0], (BATCH, SEQ, D_MODEL), 1.0),
        "c": nrm(ks[1], (BATCH, D_MODEL), 1.0),
        "ada_w": nrm(ks[2], (DEPTH, D_MODEL, 6 * D_MODEL), D_MODEL ** -0.5),
        "ada_b": nrm(ks[3], (DEPTH, 6 * D_MODEL), 0.01),
        "even_w_in": nrm(ks[4], (N_EVEN, D_MODEL, EVEN_WIDTH), D_MODEL ** -0.5),
        "even_w_out": nrm(ks[5], (N_EVEN, D_MODEL, D_MODEL), D_MODEL ** -0.5),
        "gla_w_lr": nrm(ks[6], (N_EVEN, GLA_RANK, MIX_HALF), GLA_RANK ** -0.5),
        "gla_b_lr": nrm(ks[7], (N_EVEN, MIX_HALF), 0.01),
        "gla_gain": 1.0 + nrm(ks[8], (N_EVEN, GLA_HEADS, HEAD_DIM), 0.01),
        "fox_b_f": nrm(ks[9], (N_EVEN, FOX_HEADS), 0.01),
        "fox_q_gain": 1.0 + nrm(ks[10], (N_EVEN, FOX_HEADS, HEAD_DIM), 0.01),
        "fox_k_gain": 1.0 + nrm(ks[11], (N_EVEN, FOX_HEADS, HEAD_DIM), 0.01),
        "odd_w_in": nrm(ks[12], (N_ODD, D_MODEL, ODD_WIDTH), D_MODEL ** -0.5),
        "odd_w_out": nrm(ks[13], (N_ODD, D_MODEL, D_MODEL), D_MODEL ** -0.5),
        "s5_lam_re": -0.5 + nrm(ks[14], (N_ODD, S5_GROUPS, S5_STATE), 0.01),
        "s5_lam_im": jnp.pi * n_idx + nrm(ks[15], (N_ODD, S5_GROUPS, S5_STATE), 0.01),
        "s5_log_dt": jax.random.uniform(ks[16], (N_ODD, S5_GROUPS), f32,
                                        minval=math.log(1e-3), maxval=math.log(1e-1)),
        "s5_b_re": nrm(ks[17], (N_ODD, S5_GROUPS, S5_STATE, S5_GROUP_WIDTH), (2 * S5_GROUP_WIDTH) ** -0.5),
        "s5_b_im": nrm(ks[18], (N_ODD, S5_GROUPS, S5_STATE, S5_GROUP_WIDTH), (2 * S5_GROUP_WIDTH) ** -0.5),
        "s5_c_re": nrm(ks[19], (N_ODD, S5_GROUPS, S5_GROUP_WIDTH, S5_STATE), (2 * S5_STATE) ** -0.5),
        "s5_c_im": nrm(ks[20], (N_ODD, S5_GROUPS, S5_GROUP_WIDTH, S5_STATE), (2 * S5_STATE) ** -0.5),
        "s5_d": nrm(ks[21], (N_ODD, S5_GROUPS, S5_GROUP_WIDTH), 1.0),
        "s5_w_glu": nrm(ks[22], (N_ODD, MIX_HALF, MIX_HALF), MIX_HALF ** -0.5),
        "s5_b_glu": nrm(ks[23], (N_ODD, MIX_HALF), 0.01),
        "sgu_ln_gain": 1.0 + nrm(ks[24], (N_ODD, MIX_HALF), 0.01),
        "sgu_ln_bias": nrm(ks[25], (N_ODD, MIX_HALF), 0.01),
        "sgu_w_s": nrm(ks[26], (N_ODD, SGU_GROUPS, SGU_CHUNK, SGU_CHUNK), SGU_CHUNK ** -0.5),
        "sgu_b_s": 1.0 + nrm(ks[27], (N_ODD, SGU_GROUPS, SGU_CHUNK), 0.01),
        "mlp_w1": nrm(ks[28], (DEPTH, D_MODEL, D_FF), D_MODEL ** -0.5),
        "mlp_w2": nrm(ks[29], (DEPTH, D_FF, D_MODEL), D_FF ** -0.5),
    }


def _fwd_reference(x, c, ada_w, ada_b, even_w_in, even_w_out, gla_w_lr, gla_b_lr, gla_gain,
              fox_b_f, fox_q_gain, fox_k_gain, odd_w_in, odd_w_out, s5_lam_re, s5_lam_im,
              s5_log_dt, s5_b_re, s5_b_im, s5_c_re, s5_c_im, s5_d, s5_w_glu, s5_b_glu,
              sgu_ln_gain, sgu_ln_bias, sgu_w_s, sgu_b_s, mlp_w1, mlp_w2):
    c_act = jax.nn.silu(c)
    for layer in range(DEPTH):
        mod = jnp.einsum('bd,de->be', c_act, ada_w[layer]) + ada_b[layer]
        sh1, sc1, g1, sh2, sc2, g2 = jnp.split(mod.astype(jnp.float32), 6, axis=-1)
        h = (_rms(x) * (1.0 + sc1[:, None]) + sh1[:, None]).astype(x.dtype)
        i = layer // 2
        if layer % 2 == 0:
            y = _even_mixer(h, even_w_in[i], even_w_out[i], gla_w_lr[i], gla_b_lr[i], gla_gain[i],
                            fox_b_f[i], fox_q_gain[i], fox_k_gain[i])
        else:
            y = _odd_mixer(h, odd_w_in[i], odd_w_out[i], s5_lam_re[i], s5_lam_im[i], s5_log_dt[i],
                           s5_b_re[i], s5_b_im[i], s5_c_re[i], s5_c_im[i], s5_d[i], s5_w_glu[i],
                           s5_b_glu[i], sgu_ln_gain[i], sgu_ln_bias[i], sgu_w_s[i], sgu_b_s[i])
        x = x + (g1[:, None] * y).astype(x.dtype)
        h = (_rms(x) * (1.0 + sc2[:, None]) + sh2[:, None]).astype(x.dtype)
        hid = jnp.square(jax.nn.relu(jnp.einsum('bsd,df->bsf', h, mlp_w1[layer])))
        x = x + (g2[:, None] * jnp.einsum('bsf,fd->bsd', hid, mlp_w2[layer])).astype(x.dtype)
    return x


import jax as _jax
import jax.numpy as _jnp

TWIN_FORMAT = 'train_step'
FWD_PARAMS = ['x', 'c', 'ada_w', 'ada_b', 'even_w_in', 'even_w_out', 'gla_w_lr', 'gla_b_lr', 'gla_gain', 'fox_b_f', 'fox_q_gain', 'fox_k_gain', 'odd_w_in', 'odd_w_out', 's5_lam_re', 's5_lam_im', 's5_log_dt', 's5_b_re', 's5_b_im', 's5_c_re', 's5_c_im', 's5_d', 's5_w_glu', 's5_b_glu', 'sgu_ln_gain', 'sgu_ln_bias', 'sgu_w_s', 'sgu_b_s', 'mlp_w1', 'mlp_w2']
TWIN_WEIGHTS = ['ada_w', 'ada_b', 'even_w_in', 'even_w_out', 'gla_w_lr', 'gla_b_lr', 'gla_gain', 'fox_b_f', 'fox_q_gain', 'fox_k_gain', 'odd_w_in', 'odd_w_out', 's5_lam_re', 's5_lam_im', 's5_log_dt', 's5_b_re', 's5_b_im', 's5_c_re', 's5_c_im', 's5_d', 's5_w_glu', 's5_b_glu', 'sgu_ln_gain', 'sgu_ln_bias', 'sgu_w_s', 'sgu_b_s', 'mlp_w1', 'mlp_w2']
TWIN_DIFF_INPUT = 'x'
TWIN_INPUTS = ['x', 'c', 'ada_w', 'ada_b', 'even_w_in', 'even_w_out', 'gla_w_lr', 'gla_b_lr', 'gla_gain', 'fox_b_f', 'fox_q_gain', 'fox_k_gain', 'odd_w_in', 'odd_w_out', 's5_lam_re', 's5_lam_im', 's5_log_dt', 's5_b_re', 's5_b_im', 's5_c_re', 's5_c_im', 's5_d', 's5_w_glu', 's5_b_glu', 'sgu_ln_gain', 'sgu_ln_bias', 'sgu_w_s', 'sgu_b_s', 'mlp_w1', 'mlp_w2', 'loss_target', 'm_ada_w', 'm_ada_b', 'm_even_w_in', 'm_even_w_out', 'm_gla_w_lr', 'm_gla_b_lr', 'm_gla_gain', 'm_fox_b_f', 'm_fox_q_gain', 'm_fox_k_gain', 'm_odd_w_in', 'm_odd_w_out', 'm_s5_lam_re', 'm_s5_lam_im', 'm_s5_log_dt', 'm_s5_b_re', 'm_s5_b_im', 'm_s5_c_re', 'm_s5_c_im', 'm_s5_d', 'm_s5_w_glu', 'm_s5_b_glu', 'm_sgu_ln_gain', 'm_sgu_ln_bias', 'm_sgu_w_s', 'm_sgu_b_s', 'm_mlp_w1', 'm_mlp_w2', 'v_ada_w', 'v_ada_b', 'v_even_w_in', 'v_even_w_out', 'v_gla_w_lr', 'v_gla_b_lr', 'v_gla_gain', 'v_fox_b_f', 'v_fox_q_gain', 'v_fox_k_gain', 'v_odd_w_in', 'v_odd_w_out', 'v_s5_lam_re', 'v_s5_lam_im', 'v_s5_log_dt', 'v_s5_b_re', 'v_s5_b_im', 'v_s5_c_re', 'v_s5_c_im', 'v_s5_d', 'v_s5_w_glu', 'v_s5_b_glu', 'v_sgu_ln_gain', 'v_sgu_ln_bias', 'v_sgu_w_s', 'v_sgu_b_s', 'v_mlp_w1', 'v_mlp_w2']
TWIN_OUTPUTS = ['loss', 'grad_x', 'grad_ada_w', 'grad_ada_b', 'grad_even_w_in', 'grad_even_w_out', 'grad_gla_w_lr', 'grad_gla_b_lr', 'grad_gla_gain', 'grad_fox_b_f', 'grad_fox_q_gain', 'grad_fox_k_gain', 'grad_odd_w_in', 'grad_odd_w_out', 'grad_s5_lam_re', 'grad_s5_lam_im', 'grad_s5_log_dt', 'grad_s5_b_re', 'grad_s5_b_im', 'grad_s5_c_re', 'grad_s5_c_im', 'grad_s5_d', 'grad_s5_w_glu', 'grad_s5_b_glu', 'grad_sgu_ln_gain', 'grad_sgu_ln_bias', 'grad_sgu_w_s', 'grad_sgu_b_s', 'grad_mlp_w1', 'grad_mlp_w2', 'delta_ada_w', 'delta_ada_b', 'delta_even_w_in', 'delta_even_w_out', 'delta_gla_w_lr', 'delta_gla_b_lr', 'delta_gla_gain', 'delta_fox_b_f', 'delta_fox_q_gain', 'delta_fox_k_gain', 'delta_odd_w_in', 'delta_odd_w_out', 'delta_s5_lam_re', 'delta_s5_lam_im', 'delta_s5_log_dt', 'delta_s5_b_re', 'delta_s5_b_im', 'delta_s5_c_re', 'delta_s5_c_im', 'delta_s5_d', 'delta_s5_w_glu', 'delta_s5_b_glu', 'delta_sgu_ln_gain', 'delta_sgu_ln_bias', 'delta_sgu_w_s', 'delta_sgu_b_s', 'delta_mlp_w1', 'delta_mlp_w2', 'new_m_ada_w', 'new_m_ada_b', 'new_m_even_w_in', 'new_m_even_w_out', 'new_m_gla_w_lr', 'new_m_gla_b_lr', 'new_m_gla_gain', 'new_m_fox_b_f', 'new_m_fox_q_gain', 'new_m_fox_k_gain', 'new_m_odd_w_in', 'new_m_odd_w_out', 'new_m_s5_lam_re', 'new_m_s5_lam_im', 'new_m_s5_log_dt', 'new_m_s5_b_re', 'new_m_s5_b_im', 'new_m_s5_c_re', 'new_m_s5_c_im', 'new_m_s5_d', 'new_m_s5_w_glu', 'new_m_s5_b_glu', 'new_m_sgu_ln_gain', 'new_m_sgu_ln_bias', 'new_m_sgu_w_s', 'new_m_sgu_b_s', 'new_m_mlp_w1', 'new_m_mlp_w2', 'new_v_ada_w', 'new_v_ada_b', 'new_v_even_w_in', 'new_v_even_w_out', 'new_v_gla_w_lr', 'new_v_gla_b_lr', 'new_v_gla_gain', 'new_v_fox_b_f', 'new_v_fox_q_gain', 'new_v_fox_k_gain', 'new_v_odd_w_in', 'new_v_odd_w_out', 'new_v_s5_lam_re', 'new_v_s5_lam_im', 'new_v_s5_log_dt', 'new_v_s5_b_re', 'new_v_s5_b_im', 'new_v_s5_c_re', 'new_v_s5_c_im', 'new_v_s5_d', 'new_v_s5_w_glu', 'new_v_s5_b_glu', 'new_v_sgu_ln_gain', 'new_v_sgu_ln_bias', 'new_v_sgu_w_s', 'new_v_sgu_b_s', 'new_v_mlp_w1', 'new_v_mlp_w2']
TWIN_LEAF_KINDS = {'loss': 'loss', 'grad_x': 'grad_x', 'grad_ada_w': 'grad_w', 'grad_ada_b': 'grad_w', 'grad_even_w_in': 'grad_w', 'grad_even_w_out': 'grad_w', 'grad_gla_w_lr': 'grad_w', 'grad_gla_b_lr': 'grad_w', 'grad_gla_gain': 'grad_w', 'grad_fox_b_f': 'grad_w', 'grad_fox_q_gain': 'grad_w', 'grad_fox_k_gain': 'grad_w', 'grad_odd_w_in': 'grad_w', 'grad_odd_w_out': 'grad_w', 'grad_s5_lam_re': 'grad_w', 'grad_s5_lam_im': 'grad_w', 'grad_s5_log_dt': 'grad_w', 'grad_s5_b_re': 'grad_w', 'grad_s5_b_im': 'grad_w', 'grad_s5_c_re': 'grad_w', 'grad_s5_c_im': 'grad_w', 'grad_s5_d': 'grad_w', 'grad_s5_w_glu': 'grad_w', 'grad_s5_b_glu': 'grad_w', 'grad_sgu_ln_gain': 'grad_w', 'grad_sgu_ln_bias': 'grad_w', 'grad_sgu_w_s': 'grad_w', 'grad_sgu_b_s': 'grad_w', 'grad_mlp_w1': 'grad_w', 'grad_mlp_w2': 'grad_w', 'delta_ada_w': 'delta_w', 'delta_ada_b': 'delta_w', 'delta_even_w_in': 'delta_w', 'delta_even_w_out': 'delta_w', 'delta_gla_w_lr': 'delta_w', 'delta_gla_b_lr': 'delta_w', 'delta_gla_gain': 'delta_w', 'delta_fox_b_f': 'delta_w', 'delta_fox_q_gain': 'delta_w', 'delta_fox_k_gain': 'delta_w', 'delta_odd_w_in': 'delta_w', 'delta_odd_w_out': 'delta_w', 'delta_s5_lam_re': 'delta_w', 'delta_s5_lam_im': 'delta_w', 'delta_s5_log_dt': 'delta_w', 'delta_s5_b_re': 'delta_w', 'delta_s5_b_im': 'delta_w', 'delta_s5_c_re': 'delta_w', 'delta_s5_c_im': 'delta_w', 'delta_s5_d': 'delta_w', 'delta_s5_w_glu': 'delta_w', 'delta_s5_b_glu': 'delta_w', 'delta_sgu_ln_gain': 'delta_w', 'delta_sgu_ln_bias': 'delta_w', 'delta_sgu_w_s': 'delta_w', 'delta_sgu_b_s': 'delta_w', 'delta_mlp_w1': 'delta_w', 'delta_mlp_w2': 'delta_w', 'new_m_ada_w': 'new_m', 'new_m_ada_b': 'new_m', 'new_m_even_w_in': 'new_m', 'new_m_even_w_out': 'new_m', 'new_m_gla_w_lr': 'new_m', 'new_m_gla_b_lr': 'new_m', 'new_m_gla_gain': 'new_m', 'new_m_fox_b_f': 'new_m', 'new_m_fox_q_gain': 'new_m', 'new_m_fox_k_gain': 'new_m', 'new_m_odd_w_in': 'new_m', 'new_m_odd_w_out': 'new_m', 'new_m_s5_lam_re': 'new_m', 'new_m_s5_lam_im': 'new_m', 'new_m_s5_log_dt': 'new_m', 'new_m_s5_b_re': 'new_m', 'new_m_s5_b_im': 'new_m', 'new_m_s5_c_re': 'new_m', 'new_m_s5_c_im': 'new_m', 'new_m_s5_d': 'new_m', 'new_m_s5_w_glu': 'new_m', 'new_m_s5_b_glu': 'new_m', 'new_m_sgu_ln_gain': 'new_m', 'new_m_sgu_ln_bias': 'new_m', 'new_m_sgu_w_s': 'new_m', 'new_m_sgu_b_s': 'new_m', 'new_m_mlp_w1': 'new_m', 'new_m_mlp_w2': 'new_m', 'new_v_ada_w': 'new_v', 'new_v_ada_b': 'new_v', 'new_v_even_w_in': 'new_v', 'new_v_even_w_out': 'new_v', 'new_v_gla_w_lr': 'new_v', 'new_v_gla_b_lr': 'new_v', 'new_v_gla_gain': 'new_v', 'new_v_fox_b_f': 'new_v', 'new_v_fox_q_gain': 'new_v', 'new_v_fox_k_gain': 'new_v', 'new_v_odd_w_in': 'new_v', 'new_v_odd_w_out': 'new_v', 'new_v_s5_lam_re': 'new_v', 'new_v_s5_lam_im': 'new_v', 'new_v_s5_log_dt': 'new_v', 'new_v_s5_b_re': 'new_v', 'new_v_s5_b_im': 'new_v', 'new_v_s5_c_re': 'new_v', 'new_v_s5_c_im': 'new_v', 'new_v_s5_d': 'new_v', 'new_v_s5_w_glu': 'new_v', 'new_v_s5_b_glu': 'new_v', 'new_v_sgu_ln_gain': 'new_v', 'new_v_sgu_ln_bias': 'new_v', 'new_v_sgu_w_s': 'new_v', 'new_v_sgu_b_s': 'new_v', 'new_v_mlp_w1': 'new_v', 'new_v_mlp_w2': 'new_v'}


def _forward(args):
    return _fwd_reference(*[args[k] for k in FWD_PARAMS])


def _output_shape():
    out = _jax.eval_shape(lambda: _forward(_fwd_setup_inputs(0)))
    return out.shape, out.dtype

N_MICROBATCH = 1
ADAM_LR = 0.001
ADAM_B1 = 0.9
ADAM_B2 = 0.999
ADAM_EPS = 1e-08
ADAM_WD = 0.01
ADAM_STEP = 10
PER_EXAMPLE_BATCH_AXIS = {'x': 0, 'c': 0, 'loss_target': 0}
SHARED_INPUTS = []
_WEIGHT_DTYPES = {'ada_w': _jnp.float32, 'ada_b': _jnp.float32, 'even_w_in': _jnp.float32, 'even_w_out': _jnp.float32, 'gla_w_lr': _jnp.float32, 'gla_b_lr': _jnp.float32, 'gla_gain': _jnp.float32, 'fox_b_f': _jnp.float32, 'fox_q_gain': _jnp.float32, 'fox_k_gain': _jnp.float32, 'odd_w_in': _jnp.float32, 'odd_w_out': _jnp.float32, 's5_lam_re': _jnp.float32, 's5_lam_im': _jnp.float32, 's5_log_dt': _jnp.float32, 's5_b_re': _jnp.float32, 's5_b_im': _jnp.float32, 's5_c_re': _jnp.float32, 's5_c_im': _jnp.float32, 's5_d': _jnp.float32, 's5_w_glu': _jnp.float32, 's5_b_glu': _jnp.float32, 'sgu_ln_gain': _jnp.float32, 'sgu_ln_bias': _jnp.float32, 'sgu_w_s': _jnp.float32, 'sgu_b_s': _jnp.float32, 'mlp_w1': _jnp.float32, 'mlp_w2': _jnp.float32}
MOMENT_SCALE = {'ada_w': 5.122183e+01, 'ada_b': 1.008871e+02, 'even_w_in': 1.759029e+01, 'even_w_out': 3.189961e+01, 'gla_w_lr': 3.111947e+00, 'gla_b_lr': 4.736677e+00, 'gla_gain': 2.289477e+01, 'fox_b_f': 1.382245e+02, 'fox_q_gain': 1.158856e+00, 'fox_k_gain': 1.157369e+00, 'odd_w_in': 1.476113e+01, 'odd_w_out': 2.157372e+01, 's5_lam_re': 7.255675e-01, 's5_lam_im': 7.047496e-01, 's5_log_dt': 6.858142e+00, 's5_b_re': 6.529230e-01, 's5_b_im': 5.901924e-01, 's5_c_re': 1.258722e+00, 's5_c_im': 1.130104e+00, 's5_d': 1.807802e+01, 's5_w_glu': 4.400272e+00, 's5_b_glu': 7.489265e+00, 'sgu_ln_gain': 1.218811e+01, 'sgu_ln_bias': 3.561951e+00, 'sgu_w_s': 1.086301e+00, 'sgu_b_s': 9.607688e+00, 'mlp_w1': 2.521119e+01, 'mlp_w2': 5.506274e+01}


def _to_microbatches(a, axis):
    t = _jnp.moveaxis(a, axis, 0)
    t = t.reshape((N_MICROBATCH, t.shape[0] // N_MICROBATCH) + t.shape[1:])
    return _jnp.moveaxis(t, 1, axis + 1)


def setup_inputs(seed: int = 0) -> dict:
    inp = _fwd_setup_inputs(seed)
    key = _jax.random.fold_in(_jax.random.key(seed), 7919)
    shape, _ = _output_shape()
    out = dict(inp)
    out["loss_target"] = _jax.random.normal(_jax.random.fold_in(key, 0), shape, _jnp.float32)
    for i, name in enumerate(TWIN_WEIGHTS):
        w = inp[name].astype(_jnp.float32)
        if MOMENT_SCALE is None:
            s = _jnp.sqrt(_jnp.mean(_jnp.square(w)) + 1e-30)
        else:
            s = MOMENT_SCALE[name]
        km, kv = _jax.random.split(_jax.random.fold_in(key, i + 1))
        out[name] = w
        out["m_" + name] = s * _jax.random.normal(km, w.shape, _jnp.float32)
        out["v_" + name] = (s * s) * _jax.random.uniform(kv, w.shape, _jnp.float32, 0.5, 1.5)
    if N_MICROBATCH > 1:
        for name, axis in PER_EXAMPLE_BATCH_AXIS.items():
            out[name] = _to_microbatches(out[name], axis)
    return {'x': out['x'], 'c': out['c'], 'ada_w': out['ada_w'], 'ada_b': out['ada_b'], 'even_w_in': out['even_w_in'], 'even_w_out': out['even_w_out'], 'gla_w_lr': out['gla_w_lr'], 'gla_b_lr': out['gla_b_lr'], 'gla_gain': out['gla_gain'], 'fox_b_f': out['fox_b_f'], 'fox_q_gain': out['fox_q_gain'], 'fox_k_gain': out['fox_k_gain'], 'odd_w_in': out['odd_w_in'], 'odd_w_out': out['odd_w_out'], 's5_lam_re': out['s5_lam_re'], 's5_lam_im': out['s5_lam_im'], 's5_log_dt': out['s5_log_dt'], 's5_b_re': out['s5_b_re'], 's5_b_im': out['s5_b_im'], 's5_c_re': out['s5_c_re'], 's5_c_im': out['s5_c_im'], 's5_d': out['s5_d'], 's5_w_glu': out['s5_w_glu'], 's5_b_glu': out['s5_b_glu'], 'sgu_ln_gain': out['sgu_ln_gain'], 'sgu_ln_bias': out['sgu_ln_bias'], 'sgu_w_s': out['sgu_w_s'], 'sgu_b_s': out['sgu_b_s'], 'mlp_w1': out['mlp_w1'], 'mlp_w2': out['mlp_w2'], 'loss_target': out['loss_target'], 'm_ada_w': out['m_ada_w'], 'm_ada_b': out['m_ada_b'], 'm_even_w_in': out['m_even_w_in'], 'm_even_w_out': out['m_even_w_out'], 'm_gla_w_lr': out['m_gla_w_lr'], 'm_gla_b_lr': out['m_gla_b_lr'], 'm_gla_gain': out['m_gla_gain'], 'm_fox_b_f': out['m_fox_b_f'], 'm_fox_q_gain': out['m_fox_q_gain'], 'm_fox_k_gain': out['m_fox_k_gain'], 'm_odd_w_in': out['m_odd_w_in'], 'm_odd_w_out': out['m_odd_w_out'], 'm_s5_lam_re': out['m_s5_lam_re'], 'm_s5_lam_im': out['m_s5_lam_im'], 'm_s5_log_dt': out['m_s5_log_dt'], 'm_s5_b_re': out['m_s5_b_re'], 'm_s5_b_im': out['m_s5_b_im'], 'm_s5_c_re': out['m_s5_c_re'], 'm_s5_c_im': out['m_s5_c_im'], 'm_s5_d': out['m_s5_d'], 'm_s5_w_glu': out['m_s5_w_glu'], 'm_s5_b_glu': out['m_s5_b_glu'], 'm_sgu_ln_gain': out['m_sgu_ln_gain'], 'm_sgu_ln_bias': out['m_sgu_ln_bias'], 'm_sgu_w_s': out['m_sgu_w_s'], 'm_sgu_b_s': out['m_sgu_b_s'], 'm_mlp_w1': out['m_mlp_w1'], 'm_mlp_w2': out['m_mlp_w2'], 'v_ada_w': out['v_ada_w'], 'v_ada_b': out['v_ada_b'], 'v_even_w_in': out['v_even_w_in'], 'v_even_w_out': out['v_even_w_out'], 'v_gla_w_lr': out['v_gla_w_lr'], 'v_gla_b_lr': out['v_gla_b_lr'], 'v_gla_gain': out['v_gla_gain'], 'v_fox_b_f': out['v_fox_b_f'], 'v_fox_q_gain': out['v_fox_q_gain'], 'v_fox_k_gain': out['v_fox_k_gain'], 'v_odd_w_in': out['v_odd_w_in'], 'v_odd_w_out': out['v_odd_w_out'], 'v_s5_lam_re': out['v_s5_lam_re'], 'v_s5_lam_im': out['v_s5_lam_im'], 'v_s5_log_dt': out['v_s5_log_dt'], 'v_s5_b_re': out['v_s5_b_re'], 'v_s5_b_im': out['v_s5_b_im'], 'v_s5_c_re': out['v_s5_c_re'], 'v_s5_c_im': out['v_s5_c_im'], 'v_s5_d': out['v_s5_d'], 'v_s5_w_glu': out['v_s5_w_glu'], 'v_s5_b_glu': out['v_s5_b_glu'], 'v_sgu_ln_gain': out['v_sgu_ln_gain'], 'v_sgu_ln_bias': out['v_sgu_ln_bias'], 'v_sgu_w_s': out['v_sgu_w_s'], 'v_sgu_b_s': out['v_sgu_b_s'], 'v_mlp_w1': out['v_mlp_w1'], 'v_mlp_w2': out['v_mlp_w2']}


def _loss(weights, diff, rest, loss_target):
    with _jax.named_scope("forward"):
        args = {**rest, TWIN_DIFF_INPUT: diff, **{k: w.astype(_WEIGHT_DTYPES[k]) for k, w in weights.items()}}
        y = _forward(args)
    with _jax.named_scope("loss_head"):
        err = _jnp.square(y.astype(_jnp.float32) - loss_target)
        return 0.5 * _jnp.sum(_jnp.mean(err, axis=-1)) if err.ndim else 0.5 * err


def _adamw(w, g, m, v):
    m = ADAM_B1 * m + (1.0 - ADAM_B1) * g
    v = ADAM_B2 * v + (1.0 - ADAM_B2) * _jnp.square(g)
    m_hat = m / (1.0 - ADAM_B1 ** ADAM_STEP)
    v_hat = v / (1.0 - ADAM_B2 ** ADAM_STEP)
    delta = -ADAM_LR * (m_hat / (_jnp.sqrt(v_hat) + ADAM_EPS) + ADAM_WD * w)
    return delta, m, v


def reference(x, c, ada_w, ada_b, even_w_in, even_w_out, gla_w_lr, gla_b_lr, gla_gain, fox_b_f, fox_q_gain, fox_k_gain, odd_w_in, odd_w_out, s5_lam_re, s5_lam_im, s5_log_dt, s5_b_re, s5_b_im, s5_c_re, s5_c_im, s5_d, s5_w_glu, s5_b_glu, sgu_ln_gain, sgu_ln_bias, sgu_w_s, sgu_b_s, mlp_w1, mlp_w2, loss_target, m_ada_w, m_ada_b, m_even_w_in, m_even_w_out, m_gla_w_lr, m_gla_b_lr, m_gla_gain, m_fox_b_f, m_fox_q_gain, m_fox_k_gain, m_odd_w_in, m_odd_w_out, m_s5_lam_re, m_s5_lam_im, m_s5_log_dt, m_s5_b_re, m_s5_b_im, m_s5_c_re, m_s5_c_im, m_s5_d, m_s5_w_glu, m_s5_b_glu, m_sgu_ln_gain, m_sgu_ln_bias, m_sgu_w_s, m_sgu_b_s, m_mlp_w1, m_mlp_w2, v_ada_w, v_ada_b, v_even_w_in, v_even_w_out, v_gla_w_lr, v_gla_b_lr, v_gla_gain, v_fox_b_f, v_fox_q_gain, v_fox_k_gain, v_odd_w_in, v_odd_w_out, v_s5_lam_re, v_s5_lam_im, v_s5_log_dt, v_s5_b_re, v_s5_b_im, v_s5_c_re, v_s5_c_im, v_s5_d, v_s5_w_glu, v_s5_b_glu, v_sgu_ln_gain, v_sgu_ln_bias, v_sgu_w_s, v_sgu_b_s, v_mlp_w1, v_mlp_w2):
    given = dict(x=x, c=c, ada_w=ada_w, ada_b=ada_b, even_w_in=even_w_in, even_w_out=even_w_out, gla_w_lr=gla_w_lr, gla_b_lr=gla_b_lr, gla_gain=gla_gain, fox_b_f=fox_b_f, fox_q_gain=fox_q_gain, fox_k_gain=fox_k_gain, odd_w_in=odd_w_in, odd_w_out=odd_w_out, s5_lam_re=s5_lam_re, s5_lam_im=s5_lam_im, s5_log_dt=s5_log_dt, s5_b_re=s5_b_re, s5_b_im=s5_b_im, s5_c_re=s5_c_re, s5_c_im=s5_c_im, s5_d=s5_d, s5_w_glu=s5_w_glu, s5_b_glu=s5_b_glu, sgu_ln_gain=sgu_ln_gain, sgu_ln_bias=sgu_ln_bias, sgu_w_s=sgu_w_s, sgu_b_s=sgu_b_s, mlp_w1=mlp_w1, mlp_w2=mlp_w2, loss_target=loss_target, m_ada_w=m_ada_w, m_ada_b=m_ada_b, m_even_w_in=m_even_w_in, m_even_w_out=m_even_w_out, m_gla_w_lr=m_gla_w_lr, m_gla_b_lr=m_gla_b_lr, m_gla_gain=m_gla_gain, m_fox_b_f=m_fox_b_f, m_fox_q_gain=m_fox_q_gain, m_fox_k_gain=m_fox_k_gain, m_odd_w_in=m_odd_w_in, m_odd_w_out=m_odd_w_out, m_s5_lam_re=m_s5_lam_re, m_s5_lam_im=m_s5_lam_im, m_s5_log_dt=m_s5_log_dt, m_s5_b_re=m_s5_b_re, m_s5_b_im=m_s5_b_im, m_s5_c_re=m_s5_c_re, m_s5_c_im=m_s5_c_im, m_s5_d=m_s5_d, m_s5_w_glu=m_s5_w_glu, m_s5_b_glu=m_s5_b_glu, m_sgu_ln_gain=m_sgu_ln_gain, m_sgu_ln_bias=m_sgu_ln_bias, m_sgu_w_s=m_sgu_w_s, m_sgu_b_s=m_sgu_b_s, m_mlp_w1=m_mlp_w1, m_mlp_w2=m_mlp_w2, v_ada_w=v_ada_w, v_ada_b=v_ada_b, v_even_w_in=v_even_w_in, v_even_w_out=v_even_w_out, v_gla_w_lr=v_gla_w_lr, v_gla_b_lr=v_gla_b_lr, v_gla_gain=v_gla_gain, v_fox_b_f=v_fox_b_f, v_fox_q_gain=v_fox_q_gain, v_fox_k_gain=v_fox_k_gain, v_odd_w_in=v_odd_w_in, v_odd_w_out=v_odd_w_out, v_s5_lam_re=v_s5_lam_re, v_s5_lam_im=v_s5_lam_im, v_s5_log_dt=v_s5_log_dt, v_s5_b_re=v_s5_b_re, v_s5_b_im=v_s5_b_im, v_s5_c_re=v_s5_c_re, v_s5_c_im=v_s5_c_im, v_s5_d=v_s5_d, v_s5_w_glu=v_s5_w_glu, v_s5_b_glu=v_s5_b_glu, v_sgu_ln_gain=v_sgu_ln_gain, v_sgu_ln_bias=v_sgu_ln_bias, v_sgu_w_s=v_sgu_w_s, v_sgu_b_s=v_sgu_b_s, v_mlp_w1=v_mlp_w1, v_mlp_w2=v_mlp_w2)
    weights = {n: given[n] for n in TWIN_WEIGHTS}
    shared = {n: given[n] for n in SHARED_INPUTS}
    per_example = {n: given[n] for n in ['x', 'c']}
    grad_fn = _jax.value_and_grad(_loss, argnums=(0, 1))

    def one_microbatch(ex, loss_target):
        ex = dict(ex)
        diff = ex.pop(TWIN_DIFF_INPUT)
        return grad_fn(weights, diff, {**shared, **ex}, loss_target)

    if N_MICROBATCH == 1:
        loss, (grad_w, grad_x) = one_microbatch(per_example, given["loss_target"])
    else:
        def body(carry, xs):
            loss_sum, grad_sum = carry
            l_k, (gw_k, gx_k) = one_microbatch(xs[0], xs[1])
            with _jax.named_scope("update"):
                return (loss_sum + l_k, _jax.tree.map(_jnp.add, grad_sum, gw_k)), gx_k

        init = (_jnp.zeros((), _jnp.float32), _jax.tree.map(_jnp.zeros_like, weights))
        (loss, grad_w), grad_x = _jax.lax.scan(body, init, (per_example, given["loss_target"]))
    with _jax.named_scope("update"):
        delta_w, new_m, new_v = {}, {}, {}
        for n in TWIN_WEIGHTS:
            delta_w[n], new_m[n], new_v[n] = _adamw(weights[n], grad_w[n], given["m_" + n], given["v_" + n])
    return (loss, grad_x, *[grad_w[n] for n in TWIN_WEIGHTS], *[delta_w[n] for n in TWIN_WEIGHTS],
            *[new_m[n] for n in TWIN_WEIGHTS], *[new_v[n] for n in TWIN_WEIGHTS])
```

```python
import functools
import math

import jax
import jax.numpy as jnp
import numpy as np
from jax import lax
from jax.experimental import pallas as pl
from jax.experimental.pallas import tpu as pltpu

F32 = jnp.float32
BF16 = jnp.bfloat16
MESH = pl.DeviceIdType.MESH
ANY = pl.BlockSpec(memory_space=pl.ANY)
DMA_SEM = pltpu.SemaphoreType.DMA

D_MODEL = 1024
HEAD_DIM = 64
MIX_HALF = 512
GLA_RANK = 16
GLA_TAU = 16.0
GLA_CHUNK = 64
S5_GROUPS = 32
S5_GROUP_WIDTH = 16
S5_STATE = 64
S5_N = S5_GROUPS * S5_STATE
SGU_GROUPS = 8
SGU_CHUNK = 128
D_FF = 4096
EPS = 1e-6
N_CHIPS = 4
LANES = 128
VMEM_LIMIT = 48 * 1024 * 1024

ADAM_LR = 0.001
ADAM_B1 = 0.9
ADAM_B2 = 0.999
ADAM_EPS = 1e-08
ADAM_WD = 0.01
ADAM_STEP = 10


def _cparams(*sem):
    return pltpu.CompilerParams(dimension_semantics=sem, vmem_limit_bytes=VMEM_LIMIT)


def _pair_gather(x, name):
    def body(x_ref, o_ref, send_sem, recv_sem, loc_sem):
        mx, my, mc = lax.axis_index("x"), lax.axis_index("y"), lax.axis_index("c")
        sib = (mx, my, 1 - mc)
        loc = pltpu.make_async_copy(x_ref, o_ref.at[mc], loc_sem)
        loc.start()
        out = pltpu.make_async_remote_copy(src_ref=x_ref, dst_ref=o_ref.at[mc], send_sem=send_sem, recv_sem=recv_sem,
                                           device_id=sib, device_id_type=MESH)
        out.start()
        pltpu.make_async_remote_copy(src_ref=x_ref, dst_ref=o_ref.at[1 - mc], send_sem=send_sem, recv_sem=recv_sem,
                                     device_id=sib, device_id_type=MESH).wait_recv()
        out.wait_send()
        loc.wait()

    return pl.pallas_call(
        body, name=name, out_shape=jax.ShapeDtypeStruct((2,) + x.shape, x.dtype), in_specs=[ANY], out_specs=ANY,
        scratch_shapes=[DMA_SEM, DMA_SEM, DMA_SEM])(x)


def _chip_exchange(x, name, bcast):
    blk = x.shape if bcast else x.shape[1:]

    def body(x_ref, o_ref, send_sems, recv_sems, loc_sem):
        mx, my, mc = lax.axis_index("x"), lax.axis_index("y"), lax.axis_index("c")
        me = 2 * mx + my
        peers = [(1 - mx, my), (mx, 1 - my), (1 - mx, 1 - my)]

        def src(k):
            return x_ref if bcast else x_ref.at[k]

        loc = pltpu.make_async_copy(src(me), o_ref.at[me], loc_sem)
        loc.start()
        sends = []
        for j, (px, py) in enumerate(peers):
            cp = pltpu.make_async_remote_copy(src_ref=src(2 * px + py), dst_ref=o_ref.at[me], send_sem=send_sems.at[j],
                                              recv_sem=recv_sems.at[j], device_id=(px, py, mc), device_id_type=MESH)
            cp.start()
            sends.append(cp)
        for j, (px, py) in enumerate(peers):
            pltpu.make_async_remote_copy(src_ref=src(me), dst_ref=o_ref.at[2 * px + py], send_sem=send_sems.at[j],
                                         recv_sem=recv_sems.at[j], device_id=(px, py, mc), device_id_type=MESH).wait_recv()
        for cp in sends:
            cp.wait_send()
        loc.wait()

    return pl.pallas_call(
        body, name=name, out_shape=jax.ShapeDtypeStruct((N_CHIPS,) + tuple(blk), x.dtype), in_specs=[ANY], out_specs=ANY,
        scratch_shapes=[DMA_SEM((3,)), DMA_SEM((3,)), DMA_SEM])(x)


def _gather8(x, name):
    return _chip_exchange(_pair_gather(x, name + "_pair"), name + "_chips", True)


def _tile(n, want):
    if n <= want:
        return n
    t = (want // LANES) * LANES
    while t >= LANES:
        if n % t == 0:
            return t
        t -= LANES
    raise ValueError(f"no lane-aligned tile for {n}")


_DIMS = {"nn": (((1,), (0,)), ((), ())), "nt": (((1,), (1,)), ((), ())), "tn": (((0,), (0,)), ((), ()))}


def _mm(a, b, mode, name, *, a_pro=None, epi=None, extras=(), out_dtype=F32, tm=512, tn=512, tk=512, a_cols=None):
    c0, csize = a_cols if a_cols is not None else (0, a.shape[1])
    if mode == "tn":
        K, M = a.shape[0], csize
    else:
        M, K = a.shape[0], csize
    N = b.shape[0] if mode == "nt" else b.shape[1]
    assert (b.shape[1] if mode == "nt" else b.shape[0]) == K, (a.shape, b.shape, mode)
    tm, tn, tk = _tile(M, tm), _tile(N, tn), _tile(K, tk)
    nk = K // tk
    if mode == "tn":
        assert c0 % tm == 0
        a_spec = pl.BlockSpec((tk, tm), lambda i, j, k: (k, i + c0 // tm))
    else:
        assert c0 % tk == 0
        a_spec = pl.BlockSpec((tm, tk), lambda i, j, k: (i, k + c0 // tk))
    b_spec = pl.BlockSpec((tn, tk), lambda i, j, k: (j, k)) if mode == "nt" else pl.BlockSpec((tk, tn), lambda i, j, k: (k, j))
    ex_specs = []
    for arr, kind in extras:
        if kind == "mn":
            assert arr.shape == (M, N)
            ex_specs.append(pl.BlockSpec((tm, tn), lambda i, j, k: (i, j)))
        else:
            assert arr.shape == (1, N)
            ex_specs.append(pl.BlockSpec((1, tn), lambda i, j, k: (0, j)))
    n_ex = len(extras)

    def body(*refs):
        a_ref, b_ref = refs[:2]
        ex_refs = refs[2:2 + n_ex]
        o_ref = refs[2 + n_ex]
        acc_ref = refs[3 + n_ex]
        k = pl.program_id(2)
        av = a_ref[...]
        if a_pro is not None:
            av = a_pro(av)
        part = lax.dot_general(av.astype(BF16), b_ref[...].astype(BF16), _DIMS[mode], preferred_element_type=F32)

        @pl.when(k == 0)
        def _():
            acc_ref[...] = part

        @pl.when(k > 0)
        def _():
            acc_ref[...] += part

        @pl.when(k == nk - 1)
        def _():
            acc = acc_ref[...]
            if epi is not None:
                acc = epi(acc, *[r[...] for r in ex_refs])
            o_ref[...] = acc.astype(o_ref.dtype)

    return pl.pallas_call(
        body, name=name, grid=(M // tm, N // tn, nk),
        in_specs=[a_spec, b_spec] + ex_specs,
        out_specs=pl.BlockSpec((tm, tn), lambda i, j, k: (i, j)),
        out_shape=jax.ShapeDtypeStruct((M, N), out_dtype),
        scratch_shapes=[pltpu.VMEM((tm, tn), F32)],
        compiler_params=_cparams("parallel", "parallel", "arbitrary"))(a, b, *[e[0] for e in extras])


ROWS = 256


def _row_spec(w, ts=ROWS):
    return pl.BlockSpec((ts, w), lambda i: (i, 0))


def _vec_spec(w):
    return pl.BlockSpec((1, w), lambda i: (0, 0))


def _res_rms(x, sc, sh, name, y=None, g=None):
    S, D = x.shape
    has_res = y is not None

    def body(*refs):
        if has_res:
            x_ref, y_ref, g_ref, sc_ref, sh_ref, xo_ref, h_ref = refs
            xv = x_ref[...] + g_ref[...] * y_ref[...]
            xo_ref[...] = xv
        else:
            x_ref, sc_ref, sh_ref, h_ref = refs
            xv = x_ref[...]
        r = lax.rsqrt(jnp.mean(xv * xv, axis=-1, keepdims=True) + EPS)
        h_ref[...] = (xv * r * (1.0 + sc_ref[...]) + sh_ref[...]).astype(BF16)

    row, vec = _row_spec(D), _vec_spec(D)
    if has_res:
        return pl.pallas_call(body, name=name, grid=(S // ROWS,), in_specs=[row, row, vec, vec, vec], out_specs=[row, row],
                              out_shape=[jax.ShapeDtypeStruct((S, D), F32), jax.ShapeDtypeStruct((S, D), BF16)],
                              compiler_params=_cparams("parallel"))(x, y, g, sc, sh)
    h = pl.pallas_call(body, name=name, grid=(S // ROWS,), in_specs=[row, vec, vec], out_specs=row,
                       out_shape=jax.ShapeDtypeStruct((S, D), BF16), compiler_params=_cparams("parallel"))(x, sc, sh)
    return x, h


def _res_rms_bwd(x, dh, sc, dres, name, y=None, g=None):
    S, D = x.shape
    has_res = y is not None

    def body(*refs):
        if has_res:
            x_ref, dh_ref, sc_ref, dres_ref, y_ref, g_ref, dx_ref, dy_ref, dg_ref, dsc_ref, dsh_ref = refs
        else:
            x_ref, dh_ref, sc_ref, dres_ref, dx_ref, dsc_ref, dsh_ref = refs
        first = pl.program_id(0) == 0
        xv = x_ref[...]
        dh = dh_ref[...]
        r = lax.rsqrt(jnp.mean(xv * xv, axis=-1, keepdims=True) + EPS)
        xn = xv * r
        dxn = dh * (1.0 + sc_ref[...])
        dx = dres_ref[...] + r * (dxn - xn * jnp.mean(dxn * xn, axis=-1, keepdims=True))
        dx_ref[...] = dx
        parts = [(dsc_ref, jnp.sum(dh * xn, axis=0, keepdims=True)), (dsh_ref, jnp.sum(dh, axis=0, keepdims=True))]
        if has_res:
            dy_ref[...] = dx * g_ref[...]
            parts.append((dg_ref, jnp.sum(dx * y_ref[...], axis=0, keepdims=True)))
        for ref, val in parts:
            @pl.when(first)
            def _(ref=ref, val=val):
                ref[...] = val

            @pl.when(jnp.logical_not(first))
            def _(ref=ref, val=val):
                ref[...] += val

    row, vec = _row_spec(D), _vec_spec(D)
    full = jax.ShapeDtypeStruct((S, D), F32)
    v = jax.ShapeDtypeStruct((1, D), F32)
    if has_res:
        return pl.pallas_call(body, name=name, grid=(S // ROWS,), in_specs=[row, row, vec, row, row, vec],
                              out_specs=[row, row, vec, vec, vec], out_shape=[full, full, v, v, v],
                              compiler_params=_cparams("arbitrary"))(x, dh, sc, dres, y, g)
    return pl.pallas_call(body, name=name, grid=(S // ROWS,), in_specs=[row, row, vec, row],
                          out_specs=[row, vec, vec], out_shape=[full, v, v],
                          compiler_params=_cparams("arbitrary"))(x, dh, sc, dres)


def _res_loss(x, m, g, target, name):
    S, D = x.shape

    def body(x_ref, m_ref, g_ref, t_ref, loss_ref, dx_ref, dm_ref, dg_ref):
        first = pl.program_id(0) == 0
        mv = m_ref[...]
        err = x_ref[...] + g_ref[...] * mv - t_ref[...]
        dx = err * (1.0 / D)
        dx_ref[...] = dx
        dm_ref[...] = dx * g_ref[...]
        part = 0.5 * jnp.sum(jnp.mean(err * err, axis=-1, keepdims=True), axis=0, keepdims=True)
        dg = jnp.sum(dx * mv, axis=0, keepdims=True)

        @pl.when(first)
        def _():
            loss_ref[...] = jnp.broadcast_to(part, loss_ref.shape)
            dg_ref[...] = dg

        @pl.when(jnp.logical_not(first))
        def _():
            loss_ref[...] += jnp.broadcast_to(part, loss_ref.shape)
            dg_ref[...] += dg

    row, vec = _row_spec(D), _vec_spec(D)
    full = jax.ShapeDtypeStruct((S, D), F32)
    return pl.pallas_call(body, name=name, grid=(S // ROWS,), in_specs=[row, row, vec, row],
                          out_specs=[pl.BlockSpec((8, LANES), lambda i: (0, 0)), row, row, vec],
                          out_shape=[jax.ShapeDtypeStruct((8, LANES), F32), full, full, jax.ShapeDtypeStruct((1, D), F32)],
                          compiler_params=_cparams("arbitrary"))(x, m, g, target)


def _adamw(w, g, m, v, name):
    R, C = w.shape
    tr = R if R <= 256 else 256
    assert R % tr == 0

    def body(w_ref, g_ref, m_ref, v_ref, d_ref, nm_ref, nv_ref):
        gv = g_ref[...]
        nm = ADAM_B1 * m_ref[...] + (1.0 - ADAM_B1) * gv
        nv = ADAM_B2 * v_ref[...] + (1.0 - ADAM_B2) * jnp.square(gv)
        m_hat = nm / (1.0 - ADAM_B1 ** ADAM_STEP)
        v_hat = nv / (1.0 - ADAM_B2 ** ADAM_STEP)
        d_ref[...] = -ADAM_LR * (m_hat / (jnp.sqrt(v_hat) + ADAM_EPS) + ADAM_WD * w_ref[...])
        nm_ref[...] = nm
        nv_ref[...] = nv

    spec = pl.BlockSpec((tr, C), lambda i: (i, 0))
    out = jax.ShapeDtypeStruct((R, C), F32)
    return pl.pallas_call(body, name=name, grid=(R // tr,), in_specs=[spec] * 4, out_specs=[spec] * 3,
                          out_shape=[out, out, out], compiler_params=_cparams("parallel"))(w, g, m, v)


def _sum_slots(x, name):
    n, R, C = x.shape
    tr = R if R <= 256 else 256
    assert R % tr == 0

    def body(x_ref, o_ref):
        acc = x_ref[0]
        for j in range(1, n):
            acc = acc + x_ref[j]
        o_ref[...] = acc

    return pl.pallas_call(body, name=name, grid=(R // tr,), in_specs=[pl.BlockSpec((n, tr, C), lambda i: (0, i, 0))],
                          out_specs=pl.BlockSpec((tr, C), lambda i: (i, 0)), out_shape=jax.ShapeDtypeStruct((R, C), x.dtype),
                          compiler_params=_cparams("parallel"))(x)


def _ew(fn, name, tiled, consts=(), outs=(), sums=(), ts=ROWS):
    tiled = [t if isinstance(t, tuple) else (t, t.shape[1], 0) for t in tiled]
    S = tiled[0][0].shape[0]
    n_t, n_c, n_o, n_s = len(tiled), len(consts), len(outs), len(sums)

    def body(*refs):
        ins = [r[...] for r in refs[:n_t + n_c]]
        res = fn(*ins)
        res = res if isinstance(res, (tuple, list)) else (res,)
        assert len(res) == n_o + n_s
        o_refs = refs[n_t + n_c:]
        for r, val in zip(o_refs[:n_o], res[:n_o]):
            r[...] = val.astype(r.dtype)
        first = pl.program_id(0) == 0
        for r, val in zip(o_refs[n_o:], res[n_o:]):
            @pl.when(first)
            def _(r=r, val=val):
                r[...] = val

            @pl.when(jnp.logical_not(first))
            def _(r=r, val=val):
                r[...] += val

    in_specs = [pl.BlockSpec((ts, w), lambda i, cb=cb: (i, cb)) for _, w, cb in tiled]
    in_specs += [pl.BlockSpec(c.shape, lambda i, nd=c.ndim: (0,) * nd) for c in consts]
    out_specs = [_row_spec(w, ts) for w, _ in outs] + [_vec_spec(w) for w in sums]
    out_shape = [jax.ShapeDtypeStruct((S, w), dt) for w, dt in outs] + [jax.ShapeDtypeStruct((1, w), F32) for w in sums]
    res = pl.pallas_call(body, name=name, grid=(S // ts,), in_specs=in_specs, out_specs=out_specs, out_shape=out_shape,
                         compiler_params=_cparams("arbitrary" if sums else "parallel"))(*[t[0] for t in tiled], *consts)
    return res


_GELU_C = math.sqrt(2.0 / math.pi)


def _gelu(x):
    return 0.5 * x * (1.0 + jnp.tanh(_GELU_C * (x + 0.044715 * x * x * x)))


def _dgelu(x):
    t = jnp.tanh(_GELU_C * (x + 0.044715 * x * x * x))
    return 0.5 * (1.0 + t) + 0.5 * x * (1.0 - t * t) * _GELU_C * (1.0 + 3.0 * 0.044715 * x * x)


def _sigmoid(x):
    return 1.0 / (1.0 + jnp.exp(-x))


def _log_sigmoid(x):
    return jnp.minimum(x, 0.0) - jnp.log(1.0 + jnp.exp(-jnp.abs(x)))


S5_TN = 64
S5_TB = 512


def _cmul(ar, ai, br, bi):
    return ar * br - ai * bi, ar * bi + ai * br


def _s5_discretise(lam_re, lam_im, log_dt, b_re, b_im):
    dt = jnp.exp(log_dt)[:, None]
    mag = jnp.exp(lam_re * dt)
    ang = lam_im * dt
    abar_re = mag * jnp.cos(ang)
    abar_im = mag * jnp.sin(ang)
    den = lam_re * lam_re + lam_im * lam_im
    coef_re = ((abar_re - 1.0) * lam_re + abar_im * lam_im) / den
    coef_im = (abar_im * lam_re - (abar_re - 1.0) * lam_im) / den
    bbar_re = coef_re[..., None] * b_re - coef_im[..., None] * b_im
    bbar_im = coef_re[..., None] * b_im + coef_im[..., None] * b_re
    return abar_re, abar_im, bbar_re, bbar_im


def _s5_scan_tables(a_re, a_im, reverse):
    pr, pi = [a_re], [a_im]
    for _ in range(6):
        r, i = _cmul(pr[-1], pi[-1], pr[-1], pi[-1])
        pr.append(r)
        pi.append(i)
    apow = jnp.stack([jnp.stack(pr), jnp.stack(pi)])[..., None]
    n = np.arange(1, LANES + 1)
    if reverse:
        n = n[::-1]
    tr = jnp.ones((a_re.shape[0], LANES), F32)
    ti = jnp.zeros((a_re.shape[0], LANES), F32)
    for k in range(8):
        bit = jnp.asarray(((n >> k) & 1).astype(np.float32))[None, :]
        if k < 7:
            fr, fi = pr[k][:, None], pi[k][:, None]
        else:
            fr, fi = _cmul(pr[6], pi[6], pr[6], pi[6])
            fr, fi = fr[:, None], fi[:, None]
        mr = bit * fr + (1.0 - bit)
        mi = bit * fi
        tr, ti = _cmul(tr, ti, mr, mi)
    return apow, jnp.stack([tr, ti])


def _s5_scan(bu, apow, ptab, name, reverse):
    _, N, S = bu.shape
    nt = S // S5_TB
    nsub = S5_TB // LANES

    def tmap(i, t):
        return (0, i, nt - 1 - t) if reverse else (0, i, t)

    def body(bu_ref, ap_ref, pt_ref, x_ref, carry_ref):
        @pl.when(pl.program_id(1) == 0)
        def _():
            carry_ref[...] = jnp.zeros_like(carry_ref)

        lane = lax.broadcasted_iota(jnp.int32, (S5_TN, LANES), 1)
        pr, pi = pt_ref[0], pt_ref[1]
        cr, ci = carry_ref[0], carry_ref[1]
        edge = 0 if reverse else LANES - 1
        for sb in (range(nsub - 1, -1, -1) if reverse else range(nsub)):
            sl = pl.ds(sb * LANES, LANES)
            xr, xi = bu_ref[0, :, sl], bu_ref[1, :, sl]
            for k in range(7):
                sh = 1 << k
                if reverse:
                    rr, ri = pltpu.roll(xr, LANES - sh, 1), pltpu.roll(xi, LANES - sh, 1)
                    valid = lane < LANES - sh
                else:
                    rr, ri = pltpu.roll(xr, sh, 1), pltpu.roll(xi, sh, 1)
                    valid = lane >= sh
                rr, ri = jnp.where(valid, rr, 0.0), jnp.where(valid, ri, 0.0)
                ar, ai = ap_ref[0, k], ap_ref[1, k]
                xr, xi = xr + ar * rr - ai * ri, xi + ar * ri + ai * rr
            xr, xi = xr + pr * cr - pi * ci, xi + pr * ci + pi * cr
            x_ref[0, :, sl] = xr
            x_ref[1, :, sl] = xi
            at_edge = lane == edge
            cr = jnp.broadcast_to(jnp.sum(jnp.where(at_edge, xr, 0.0), axis=1, keepdims=True), (S5_TN, LANES))
            ci = jnp.broadcast_to(jnp.sum(jnp.where(at_edge, xi, 0.0), axis=1, keepdims=True), (S5_TN, LANES))
        carry_ref[0] = cr
        carry_ref[1] = ci

    return pl.pallas_call(
        body, name=name, grid=(N // S5_TN, nt),
        in_specs=[pl.BlockSpec((2, S5_TN, S5_TB), tmap), pl.BlockSpec((2, 7, S5_TN, 1), lambda i, t: (0, 0, i, 0)),
                  pl.BlockSpec((2, S5_TN, LANES), lambda i, t: (0, i, 0))],
        out_specs=pl.BlockSpec((2, S5_TN, S5_TB), tmap), out_shape=jax.ShapeDtypeStruct((2, N, S), F32),
        scratch_shapes=[pltpu.VMEM((2, S5_TN, LANES), F32)], compiler_params=_cparams("parallel", "arbitrary"))(bu, apow, ptab)


def _s5_da(lam, x, name):
    _, N, S = x.shape
    nt = S // S5_TB
    nsub = S5_TB // LANES

    def body(l_ref, x_ref, o_ref, acc_ref, carry_ref):
        t = pl.program_id(1)

        @pl.when(t == 0)
        def _():
            acc_ref[...] = jnp.zeros_like(acc_ref)
            carry_ref[...] = jnp.zeros_like(carry_ref)

        lane = lax.broadcasted_iota(jnp.int32, (S5_TN, LANES), 1)
        cr, ci = carry_ref[0], carry_ref[1]
        ar, ai = acc_ref[0], acc_ref[1]
        for sb in range(nsub):
            sl = pl.ds(sb * LANES, LANES)
            xr, xi = x_ref[0, :, sl], x_ref[1, :, sl]
            pr = jnp.where(lane == 0, cr, pltpu.roll(xr, 1, 1))
            pi = jnp.where(lane == 0, ci, pltpu.roll(xi, 1, 1))
            lr, li = l_ref[0, :, sl], l_ref[1, :, sl]
            ar = ar + lr * pr + li * pi
            ai = ai + li * pr - lr * pi
            last = lane == LANES - 1
            cr = jnp.broadcast_to(jnp.sum(jnp.where(last, xr, 0.0), axis=1, keepdims=True), (S5_TN, LANES))
            ci = jnp.broadcast_to(jnp.sum(jnp.where(last, xi, 0.0), axis=1, keepdims=True), (S5_TN, LANES))
        acc_ref[0] = ar
        acc_ref[1] = ai
        carry_ref[0] = cr
        carry_ref[1] = ci

        @pl.when(t == nt - 1)
        def _():
            o_ref[0] = jnp.sum(ar, axis=1, keepdims=True)
            o_ref[1] = jnp.sum(ai, axis=1, keepdims=True)

    blk = pl.BlockSpec((2, S5_TN, S5_TB), lambda i, t: (0, i, t))
    return pl.pallas_call(
        body, name=name, grid=(N // S5_TN, nt), in_specs=[blk, blk],
        out_specs=pl.BlockSpec((2, S5_TN, 1), lambda i, t: (0, i, 0)), out_shape=jax.ShapeDtypeStruct((2, N, 1), F32),
        scratch_shapes=[pltpu.VMEM((2, S5_TN, LANES), F32), pltpu.VMEM((2, S5_TN, LANES), F32)],
        compiler_params=_cparams("parallel", "arbitrary"))(lam, x)


def _block_diag(t):
    G, a, b = t.shape
    return (t[:, :, None, :] * jnp.eye(G, dtype=t.dtype)[:, None, :, None]).reshape(G * a, G * b)


def _block_diag_take(m, G):
    a, b = m.shape[0] // G, m.shape[1] // G
    m4 = m.reshape(G, a, G, b)
    return jnp.stack([m4[g, :, g, :] for g in range(G)])


def _s5_block_fwd(u, w, pfx):
    a_re, a_im, bb_re, bb_im = _s5_discretise(w["lam_re"], w["lam_im"], w["log_dt"], w["b_re"], w["b_im"])
    bcat = jnp.concatenate([_block_diag(bb_re), _block_diag(bb_im)], axis=0).astype(BF16)
    ccat = jnp.concatenate([_block_diag(jnp.swapaxes(w["c_re"], 1, 2)),
                            -_block_diag(jnp.swapaxes(w["c_im"], 1, 2))], axis=0).astype(BF16)
    S = u.shape[0]
    af_re, af_im = a_re.reshape(-1), a_im.reshape(-1)
    apow, ptab = _s5_scan_tables(af_re, af_im, False)
    bu = _mm(bcat, u, "nt", pfx + "_bu").reshape(2, S5_N, S)
    x = _s5_scan(bu, apow, ptab, pfx + "_scan", False)
    d_row = w["d"].reshape(1, MIX_HALF)
    ys = _mm(x.reshape(2 * S5_N, S), ccat, "tn", pfx + "_y", epi=lambda acc, ut, dr: acc + dr * ut, extras=[(u, "mn"), (d_row, "n")])
    z = _mm(ys, w["w_glu"], "nn", pfx + "_glu", a_pro=_gelu, epi=lambda acc, b: acc + b, extras=[(w["b_glu"].reshape(1, -1), "n")])
    y2, = _ew(lambda ysv, zv: _gelu(ysv) * _sigmoid(zv), pfx + "_gate", [ys, z], outs=[(MIX_HALF, F32)])
    return y2, dict(u=u, x=x, ys=ys, z=z, bcat=bcat, ccat=ccat, a=(af_re, af_im), d_row=d_row)


def _s5_block_bwd(dy2, w, res, pfx):
    u, x, ys, z, bcat, ccat = res["u"], res["x"], res["ys"], res["z"], res["bcat"], res["ccat"]
    S = u.shape[0]

    def gate_bwd(dy, ysv, zv):
        sg = _sigmoid(zv)
        dz = dy * _gelu(ysv) * sg * (1.0 - sg)
        return dz, jnp.sum(dz, axis=0, keepdims=True)

    dz, db_glu = _ew(gate_bwd, pfx + "_gate_bwd", [dy2, ys, z], outs=[(MIX_HALF, F32)], sums=[MIX_HALF])
    dw_glu = _mm(ys, dz, "tn", pfx + "_dwglu", a_pro=_gelu)
    dys = _mm(dz, w["w_glu"], "nt", pfx + "_dys", epi=lambda acc, dy, zv, ysv: (acc + dy * _sigmoid(zv)) * _dgelu(ysv),
              extras=[(dy2, "mn"), (z, "mn"), (ys, "mn")])
    dd, = _ew(lambda a, b: jnp.sum(a * b, axis=0, keepdims=True), pfx + "_dd", [dys, u], sums=[MIX_HALF])
    xcat = x.reshape(2 * S5_N, S)
    dccat = _mm(xcat, dys, "nn", pfx + "_dc")
    dx = _mm(ccat, dys, "nt", pfx + "_dx").reshape(2, S5_N, S)
    af_re, af_im = res["a"]
    apow, ptab = _s5_scan_tables(af_re, -af_im, True)
    lam = _s5_scan(dx, apow, ptab, pfx + "_scan_bwd", True)
    lcat = lam.reshape(2 * S5_N, S)
    dbcat = _mm(lcat, u, "nn", pfx + "_db")
    du = _mm(lcat, bcat, "tn", pfx + "_du", epi=lambda acc, dyv, dr: acc + dyv * dr, extras=[(dys, "mn"), (res["d_row"], "n")])
    da = _s5_da(lam, x, pfx + "_da")
    G = S5_GROUPS
    d_abar_re, d_abar_im = da[0].reshape(G, S5_STATE), da[1].reshape(G, S5_STATE)
    d_bb_re, d_bb_im = _block_diag_take(dbcat[:S5_N], G), _block_diag_take(dbcat[S5_N:], G)
    _, vjp = jax.vjp(_s5_discretise, w["lam_re"], w["lam_im"], w["log_dt"], w["b_re"], w["b_im"])
    g_lam_re, g_lam_im, g_log_dt, g_b_re, g_b_im = vjp((d_abar_re, d_abar_im, d_bb_re, d_bb_im))
    g_c_re = jnp.swapaxes(_block_diag_take(dccat[:S5_N], G), 1, 2)
    g_c_im = -jnp.swapaxes(_block_diag_take(dccat[S5_N:], G), 1, 2)
    grads = dict(lam_re=g_lam_re, lam_im=g_lam_im, log_dt=g_log_dt, b_re=g_b_re, b_im=g_b_im, c_re=g_c_re, c_im=g_c_im,
                 d=dd.reshape(G, S5_GROUP_WIDTH), w_glu=dw_glu, b_glu=db_glu.reshape(-1))
    return du, grads


SGU_TS = 512
N_PAIRS = MIX_HALF // LANES


def _half_masks(rows):
    lane = lax.broadcasted_iota(jnp.int32, (rows, LANES), 1)
    left = (lane < HEAD_DIM).astype(F32)
    return left, 1.0 - left


def _sgu_norm(zv, gain, bias):
    v = _gelu(zv)
    mu = jnp.mean(v, axis=-1, keepdims=True)
    vc = v - mu
    rstd = lax.rsqrt(jnp.mean(vc * vc, axis=-1, keepdims=True) + EPS)
    vhat = vc * rstd
    return vhat, rstd, vhat * gain + bias


def _sgu_tables(w_s, b_s):
    mask = jnp.tril(jnp.ones((SGU_CHUNK, SGU_CHUNK), dtype=bool))
    wm = jnp.where(mask[None], w_s, 0.0).astype(BF16)
    bias_tab = jnp.repeat(b_s.T, MIX_HALF // SGU_GROUPS, axis=1)
    return wm, bias_tab


def _sgu_fwd(proj, ln_gain, ln_bias, wm, bias_tab, name):
    S = proj.shape[0]
    nch = SGU_TS // SGU_CHUNK

    def body(zu_ref, zv_ref, g_ref, b_ref, w_ref, bt_ref, o_ref):
        left, right = _half_masks(SGU_CHUNK)
        _, _, vn = _sgu_norm(zv_ref[...], g_ref[...], b_ref[...])
        for ch in range(nch):
            rows = pl.ds(ch * SGU_CHUNK, SGU_CHUNK)
            for p in range(N_PAIRS):
                cols = pl.ds(p * LANES, LANES)
                vp = vn[ch * SGU_CHUNK:(ch + 1) * SGU_CHUNK, p * LANES:(p + 1) * LANES]
                mixed = (jnp.dot(w_ref[2 * p], (vp * left).astype(BF16), preferred_element_type=F32)
                         + jnp.dot(w_ref[2 * p + 1], (vp * right).astype(BF16), preferred_element_type=F32) + bt_ref[:, cols])
                o_ref[rows, cols] = _gelu(zu_ref[rows, cols]) * mixed

    vec = _vec_spec(MIX_HALF)
    return pl.pallas_call(
        body, name=name, grid=(S // SGU_TS,),
        in_specs=[pl.BlockSpec((SGU_TS, MIX_HALF), lambda i: (i, 1)), pl.BlockSpec((SGU_TS, MIX_HALF), lambda i: (i, 2)), vec, vec,
                  pl.BlockSpec((SGU_GROUPS, SGU_CHUNK, SGU_CHUNK), lambda i: (0, 0, 0)), pl.BlockSpec((SGU_CHUNK, MIX_HALF), lambda i: (0, 0))],
        out_specs=_row_spec(MIX_HALF, SGU_TS), out_shape=jax.ShapeDtypeStruct((S, MIX_HALF), F32),
        compiler_params=_cparams("parallel"))(proj, proj, ln_gain, ln_bias, wm, bias_tab)


def _sgu_bwd(dout, proj, ln_gain, ln_bias, wm, bias_tab, name):
    S = proj.shape[0]
    nch = SGU_TS // SGU_CHUNK
    nt_dims = (((1,), (1,)), ((), ()))
    tn_dims = (((0,), (0,)), ((), ()))

    def body(do_ref, zu_ref, zv_ref, g_ref, b_ref, w_ref, bt_ref, dzu_ref, dzv_ref, dw_ref, dbt_ref, dg_ref, db_ref, dvn_ref):
        first = pl.program_id(0) == 0

        @pl.when(first)
        def _():
            dw_ref[...] = jnp.zeros_like(dw_ref)
            dbt_ref[...] = jnp.zeros_like(dbt_ref)
            dg_ref[...] = jnp.zeros_like(dg_ref)
            db_ref[...] = jnp.zeros_like(db_ref)

        left, right = _half_masks(SGU_CHUNK)
        zv = zv_ref[...]
        vhat, rstd, vn = _sgu_norm(zv, g_ref[...], b_ref[...])
        for ch in range(nch):
            rows = pl.ds(ch * SGU_CHUNK, SGU_CHUNK)
            for p in range(N_PAIRS):
                cols = pl.ds(p * LANES, LANES)
                vp = vn[ch * SGU_CHUNK:(ch + 1) * SGU_CHUNK, p * LANES:(p + 1) * LANES]
                vl, vr = (vp * left).astype(BF16), (vp * right).astype(BF16)
                mixed = (jnp.dot(w_ref[2 * p], vl, preferred_element_type=F32)
                         + jnp.dot(w_ref[2 * p + 1], vr, preferred_element_type=F32) + bt_ref[:, cols])
                zu = zu_ref[rows, cols]
                do = do_ref[rows, cols]
                dzu_ref[rows, cols] = do * mixed * _dgelu(zu)
                dmix = do * _gelu(zu)
                dbt_ref[:, cols] += dmix
                dl, dr = (dmix * left).astype(BF16), (dmix * right).astype(BF16)
                dw_ref[2 * p] += lax.dot_general(dl, vl, nt_dims, preferred_element_type=F32)
                dw_ref[2 * p + 1] += lax.dot_general(dr, vr, nt_dims, preferred_element_type=F32)
                dvn_ref[rows, cols] = (lax.dot_general(w_ref[2 * p], dl, tn_dims, preferred_element_type=F32)
                                       + lax.dot_general(w_ref[2 * p + 1], dr, tn_dims, preferred_element_type=F32))
        dvn = dvn_ref[...]
        dg_ref[...] += jnp.sum(dvn * vhat, axis=0, keepdims=True)
        db_ref[...] += jnp.sum(dvn, axis=0, keepdims=True)
        dvh = dvn * g_ref[...]
        dv = rstd * (dvh - jnp.mean(dvh, axis=-1, keepdims=True) - vhat * jnp.mean(dvh * vhat, axis=-1, keepdims=True))
        dzv_ref[...] = dv * _dgelu(zv)

    vec = _vec_spec(MIX_HALF)
    row = _row_spec(MIX_HALF, SGU_TS)
    wspec = pl.BlockSpec((SGU_GROUPS, SGU_CHUNK, SGU_CHUNK), lambda i: (0, 0, 0))
    tspec = pl.BlockSpec((SGU_CHUNK, MIX_HALF), lambda i: (0, 0))
    full = jax.ShapeDtypeStruct((S, MIX_HALF), F32)
    v = jax.ShapeDtypeStruct((1, MIX_HALF), F32)
    return pl.pallas_call(
        body, name=name, grid=(S // SGU_TS,),
        in_specs=[row, pl.BlockSpec((SGU_TS, MIX_HALF), lambda i: (i, 1)), pl.BlockSpec((SGU_TS, MIX_HALF), lambda i: (i, 2)), vec, vec,
                  wspec, tspec],
        out_specs=[row, row, wspec, tspec, vec, vec],
        out_shape=[full, full, jax.ShapeDtypeStruct((SGU_GROUPS, SGU_CHUNK, SGU_CHUNK), F32),
                   jax.ShapeDtypeStruct((SGU_CHUNK, MIX_HALF), F32), v, v],
        scratch_shapes=[pltpu.VMEM((SGU_TS, MIX_HALF), F32)],
        compiler_params=_cparams("arbitrary"))(dout, proj, proj, ln_gain, ln_bias, wm, bias_tab)


def _sgu_grads(dw, dbias_tab):
    mask = jnp.tril(jnp.ones((SGU_CHUNK, SGU_CHUNK), dtype=bool))
    g_w = jnp.where(mask[None], dw, 0.0)
    g_b = dbias_tab.reshape(SGU_CHUNK, SGU_GROUPS, MIX_HALF // SGU_GROUPS).sum(axis=-1).T
    return g_w, g_b


def _head_avg_matrix(w):
    idx = np.arange(w) // HEAD_DIM
    return jnp.asarray((idx[:, None] == idx[None, :]).astype(np.float32) / HEAD_DIM, dtype=BF16)


def _head_mean(t, bavg):
    hi = t.astype(BF16)
    lo = (t - hi.astype(F32)).astype(BF16)
    return jnp.dot(hi, bavg, preferred_element_type=F32) + jnp.dot(lo, bavg, preferred_element_type=F32)


def _head_rms(t, bavg):
    r = lax.rsqrt(_head_mean(t * t, bavg) + EPS)
    return t * r, r


def _head_rms_bwd(dn, n, r, bavg):
    return r * (dn - n * _head_mean(dn * n, bavg))


GLA_TS = 512
C = GLA_CHUNK
NT_DIMS = (((1,), (1,)), ((), ()))
TN_DIMS = (((0,), (0,)), ((), ()))
HI = lax.Precision.HIGHEST


def _bdot(a, b, dims=(((1,), (0,)), ((), ()))):
    return lax.dot_general(a.astype(BF16), b.astype(BF16), dims, preferred_element_type=F32)


def _gla_chunk_terms(q, k, z):
    row = lax.broadcasted_iota(jnp.int32, (C, C), 0)
    col = lax.broadcasted_iota(jnp.int32, (C, C), 1)
    lc = _log_sigmoid(z) * (1.0 / GLA_TAU)
    b = lax.dot_general((row >= col).astype(F32), lc, (((1,), (0,)), ((), ())), precision=HI, preferred_element_type=F32)
    b_last = jnp.sum(lc, axis=0, keepdims=True)
    b_mid = b[C // 2:C // 2 + 1, :]
    scale = HEAD_DIM ** -0.5
    e_b, e_q, e_k, e_l = jnp.exp(b), jnp.exp(b - b_mid), jnp.exp(b_mid - b), jnp.exp(b_last - b)
    qs = q * (scale * e_b)
    qe = q * (scale * e_q)
    ke = k * e_k
    kl = k * e_l
    return dict(e_b=e_b, e_q=e_q, e_k=e_k, e_l=e_l, qs=qs, qe=qe, ke=ke, kl=kl, dec=jnp.exp(b_last), causal=row >= col, scale=scale)


def _pair_block_diag():
    r = lax.broadcasted_iota(jnp.int32, (LANES, LANES), 0) // HEAD_DIM
    c = lax.broadcasted_iota(jnp.int32, (LANES, LANES), 1) // HEAD_DIM
    return (r == c).astype(F32)


def _gla_fwd(proj, z, name):
    S = proj.shape[0]
    nch = GLA_TS // C

    def body(q_ref, k_ref, v_ref, z_ref, o_ref, st_ref, state_ref):
        @pl.when(pl.program_id(1) == 0)
        def _():
            state_ref[...] = jnp.zeros_like(state_ref)

        left, right = _half_masks(C)
        bd = _pair_block_diag()
        for ch in range(nch):
            rows = pl.ds(ch * C, C)
            q, k, v = q_ref[rows, :], k_ref[rows, :], v_ref[rows, :]
            t = _gla_chunk_terms(q, k, z_ref[rows, :])
            st = state_ref[...]
            st_ref[ch, 0] = st
            o = _bdot(t["qs"], st, NT_DIMS)
            for m in (left, right):
                a = jnp.where(t["causal"], _bdot(t["qe"] * m, t["ke"], NT_DIMS), 0.0)
                o = o + m * _bdot(a, v)
            o_ref[rows, :] = o
            state_ref[...] = st * t["dec"] + bd * _bdot(v, t["kl"], TN_DIMS)

    def col(cb):
        return pl.BlockSpec((GLA_TS, LANES), lambda p, i: (i, cb * N_PAIRS + p))

    return pl.pallas_call(
        body, name=name, grid=(N_PAIRS, S // GLA_TS),
        in_specs=[col(0), col(1), col(2), pl.BlockSpec((GLA_TS, LANES), lambda p, i: (i, p))],
        out_specs=[pl.BlockSpec((GLA_TS, LANES), lambda p, i: (i, p)), pl.BlockSpec((nch, 1, LANES, LANES), lambda p, i: (i, p, 0, 0))],
        out_shape=[jax.ShapeDtypeStruct((S, MIX_HALF), F32), jax.ShapeDtypeStruct((S // C, N_PAIRS, LANES, LANES), F32)],
        scratch_shapes=[pltpu.VMEM((LANES, LANES), F32)], compiler_params=_cparams("parallel", "arbitrary"))(proj, proj, proj, z)


def _gla_bwd(do, proj, z, states, name):
    S = proj.shape[0]
    nch = GLA_TS // C
    nblk = S // GLA_TS

    def body(do_ref, q_ref, k_ref, v_ref, z_ref, st_ref, dq_ref, dk_ref, dv_ref, dlc_ref, dstate_ref):
        @pl.when(pl.program_id(1) == 0)
        def _():
            dstate_ref[...] = jnp.zeros_like(dstate_ref)

        left, right = _half_masks(C)
        bd = _pair_block_diag()
        rowi = lax.broadcasted_iota(jnp.int32, (C, LANES), 0)
        for ch in range(nch - 1, -1, -1):
            rows = pl.ds(ch * C, C)
            q, k, v, dov = q_ref[rows, :], k_ref[rows, :], v_ref[rows, :], do_ref[rows, :]
            t = _gla_chunk_terms(q, k, z_ref[rows, :])
            st = st_ref[ch, 0]
            dst_next = dstate_ref[...]
            g = bd * dst_next
            dqs = _bdot(dov, st)
            dv = _bdot(t["kl"], g, NT_DIMS)
            dkl = _bdot(v, g)
            dqe = jnp.zeros((C, LANES), F32)
            dke = jnp.zeros((C, LANES), F32)
            for m in (left, right):
                a = jnp.where(t["causal"], _bdot(t["qe"] * m, t["ke"], NT_DIMS), 0.0)
                da = jnp.where(t["causal"], _bdot(dov * m, v, NT_DIMS), 0.0)
                dv = dv + m * _bdot(a, dov, TN_DIMS)
                dqe = dqe + m * _bdot(da, t["ke"])
                dke = dke + m * _bdot(da, t["qe"], TN_DIMS)
            dstate_ref[...] = bd * (dst_next * t["dec"] + _bdot(dov, t["qs"], TN_DIMS))
            db_last = jnp.sum(dst_next * st, axis=0, keepdims=True) * t["dec"] + jnp.sum(dkl * t["kl"], axis=0, keepdims=True)
            db = dqs * t["qs"] + dqe * t["qe"] - dke * t["ke"] - dkl * t["kl"]
            db = db + jnp.where(rowi == C - 1, db_last, 0.0)
            dq_ref[rows, :] = (dqs * t["e_b"] + dqe * t["e_q"]) * t["scale"]
            dk_ref[rows, :] = dke * t["e_k"] + dkl * t["e_l"]
            dv_ref[rows, :] = dv
            row = lax.broadcasted_iota(jnp.int32, (C, C), 0)
            colm = lax.broadcasted_iota(jnp.int32, (C, C), 1)
            dlc_ref[rows, :] = lax.dot_general((colm >= row).astype(F32), db, (((1,), (0,)), ((), ())), precision=HI,
                                               preferred_element_type=F32)

    def col(cb):
        return pl.BlockSpec((GLA_TS, LANES), lambda p, i: (nblk - 1 - i, cb * N_PAIRS + p))

    blk = pl.BlockSpec((GLA_TS, LANES), lambda p, i: (nblk - 1 - i, p))
    full = jax.ShapeDtypeStruct((S, MIX_HALF), F32)
    return pl.pallas_call(
        body, name=name, grid=(N_PAIRS, nblk),
        in_specs=[blk, col(0), col(1), col(2), blk, pl.BlockSpec((nch, 1, LANES, LANES), lambda p, i: (nblk - 1 - i, p, 0, 0))],
        out_specs=[blk, blk, blk, blk], out_shape=[full, full, full, full],
        scratch_shapes=[pltpu.VMEM((LANES, LANES), F32)], compiler_params=_cparams("parallel", "arbitrary"))(do, proj, proj, proj, z, states)


def _gla_block_fwd(proj, w_lr_pad, b_lr, gain, bavg, pfx):
    z = _mm(proj, w_lr_pad, "nn", pfx + "_z", a_cols=(7 * MIX_HALF, MIX_HALF), epi=lambda acc, b: acc + b, extras=[(b_lr, "n")])
    o, states = _gla_fwd(proj, z, pfx + "_core")

    def out(ov, gg, ba, gn):
        n, _ = _head_rms(ov, ba)
        return n * gn * (gg * _sigmoid(gg))

    og, = _ew(out, pfx + "_out", [o, (proj, MIX_HALF, 3)], consts=[bavg, gain], outs=[(MIX_HALF, F32)])
    return og, dict(z=z, o=o, states=states)


def _gla_block_bwd(dog, proj, w_lr_pad, gain, bavg, res, pfx):
    z, o, states = res["z"], res["o"], res["states"]

    def out_bwd(dy, ov, gg, ba, gn):
        n, r = _head_rms(ov, ba)
        sg = _sigmoid(gg)
        silu = gg * sg
        dn = dy * gn * silu
        do = _head_rms_bwd(dn, n, r, ba)
        dgg = dy * n * gn * (sg * (1.0 + gg * (1.0 - sg)))
        return do, dgg, jnp.sum(dy * n * silu, axis=0, keepdims=True)

    do, dgg, dgain = _ew(out_bwd, pfx + "_out_bwd", [dog, o, (proj, MIX_HALF, 3)], consts=[bavg, gain],
                         outs=[(MIX_HALF, F32), (MIX_HALF, F32)], sums=[MIX_HALF])
    dq, dk, dv, dlc = _gla_bwd(do, proj, z, states, pfx + "_core_bwd")

    def decay_bwd(dl, zv):
        dz = dl * (1.0 / GLA_TAU) * (1.0 - _sigmoid(zv))
        return dz, jnp.sum(dz, axis=0, keepdims=True)

    dz, db_lr = _ew(decay_bwd, pfx + "_decay_bwd", [dlc, z], outs=[(MIX_HALF, F32)], sums=[MIX_HALF])
    dw_lr_pad = _mm(proj, dz, "tn", pfx + "_dwlr", a_cols=(7 * MIX_HALF, MIX_HALF))
    dsmall = _mm(dz, w_lr_pad, "nt", pfx + "_dsmall")
    return (dq, dk, dv, dgg, dsmall), dict(w_lr=dw_lr_pad[:GLA_RANK], b_lr=db_lr.reshape(-1), gain=dgain.reshape(-1, HEAD_DIM))


FOX_T = 512
FOX_HEADS = MIX_HALF // HEAD_DIM
NEG = -1e30
CUM_T = 512


def _cum_lanes(x, name, reverse, pre=None):
    R, S = x.shape
    nb = S // CUM_T

    def body(x_ref, o_ref, carry_ref):
        @pl.when(pl.program_id(0) == 0)
        def _():
            carry_ref[...] = jnp.zeros_like(carry_ref)

        xv = x_ref[...]
        if pre is not None:
            xv = pre(xv)
        i = lax.broadcasted_iota(jnp.int32, (CUM_T, CUM_T), 0)
        j = lax.broadcasted_iota(jnp.int32, (CUM_T, CUM_T), 1)
        tri = ((i >= j) if reverse else (i <= j)).astype(F32)
        c = lax.dot_general(xv, tri, (((1,), (0,)), ((), ())), precision=HI, preferred_element_type=F32)
        carry = carry_ref[...]
        o_ref[...] = c + carry[:, 0:1]
        carry_ref[...] = carry + jnp.sum(xv, axis=1, keepdims=True)

    spec = pl.BlockSpec((R, CUM_T), (lambda i: (0, nb - 1 - i)) if reverse else (lambda i: (0, i)))
    return pl.pallas_call(body, name=name, grid=(nb,), in_specs=[spec], out_specs=spec, out_shape=jax.ShapeDtypeStruct((R, S), F32),
                          scratch_shapes=[pltpu.VMEM((R, LANES), F32)], compiler_params=_cparams("arbitrary"))(x)


def _fox_scores(q, k, cqb, ck_ref, h, m, diag):
    scale = HEAD_DIM ** -0.5
    cq = cqb[:, h * HEAD_DIM:h * HEAD_DIM + 1]
    ck = ck_ref[0, h:h + 1, :]
    s = lax.dot_general(q * m.astype(q.dtype), k, NT_DIMS, preferred_element_type=F32) * scale + (cq - ck)
    row = lax.broadcasted_iota(jnp.int32, (FOX_T, FOX_T), 0)
    col = lax.broadcasted_iota(jnp.int32, (FOX_T, FOX_T), 1)
    return jnp.where(jnp.logical_and(diag, row < col), NEG, s)


def _fox_fwd(qn, kn, proj, cum_b, cum_tp, name):
    S = qn.shape[0]
    nq = S // FOX_T

    def body(q_ref, k_ref, v_ref, cq_ref, ck_ref, o_ref, lse_ref, m_scr, l_scr, acc_scr):
        qi, ki = pl.program_id(1), pl.program_id(2)

        @pl.when(ki == 0)
        def _():
            m_scr[...] = jnp.full_like(m_scr, NEG)
            l_scr[...] = jnp.zeros_like(l_scr)
            acc_scr[...] = jnp.zeros_like(acc_scr)

        left, right = _half_masks(FOX_T)

        @pl.when(ki <= qi)
        def _():
            q, k, v = q_ref[...], k_ref[...], v_ref[...].astype(BF16)
            cqb = cq_ref[...]
            for h, m in enumerate((left, right)):
                s = _fox_scores(q, k, cqb, ck_ref, h, m, ki == qi)
                m_prev = m_scr[h]
                m_new = jnp.maximum(m_prev, jnp.max(s, axis=1, keepdims=True))
                alpha = jnp.exp(m_prev - m_new)
                p = jnp.exp(s - m_new)
                l_scr[h] = alpha * l_scr[h] + jnp.sum(p, axis=1, keepdims=True)
                acc_scr[h] = alpha * acc_scr[h] + jnp.dot(p.astype(BF16), v, preferred_element_type=F32)
                m_scr[h] = m_new

        @pl.when(ki == qi)
        def _():
            o_ref[...] = left * (acc_scr[0] / l_scr[0]) + right * (acc_scr[1] / l_scr[1])
            lse_ref[...] = left * (m_scr[0] + jnp.log(l_scr[0])) + right * (m_scr[1] + jnp.log(l_scr[1]))

    qspec = pl.BlockSpec((FOX_T, LANES), lambda p, qi, ki: (qi, p))
    kspec = pl.BlockSpec((FOX_T, LANES), lambda p, qi, ki: (jnp.minimum(ki, qi), p))
    vspec = pl.BlockSpec((FOX_T, LANES), lambda p, qi, ki: (jnp.minimum(ki, qi), 6 * N_PAIRS + p))
    ckspec = pl.BlockSpec((1, 8, FOX_T), lambda p, qi, ki: (p, 0, jnp.minimum(ki, qi)))
    full = jax.ShapeDtypeStruct((S, MIX_HALF), F32)
    return pl.pallas_call(
        body, name=name, grid=(N_PAIRS, nq, nq), in_specs=[qspec, kspec, vspec, qspec, ckspec], out_specs=[qspec, qspec],
        out_shape=[full, full],
        scratch_shapes=[pltpu.VMEM((2, FOX_T, 1), F32), pltpu.VMEM((2, FOX_T, 1), F32), pltpu.VMEM((2, FOX_T, LANES), F32)],
        compiler_params=_cparams("parallel", "parallel", "arbitrary"))(qn, kn, proj, cum_b, cum_tp)


def _fox_bwd_dq(do, qn, kn, proj, cum_b, cum_tp, lse_b, delta_b, name):
    S = qn.shape[0]
    nq = S // FOX_T
    scale = HEAD_DIM ** -0.5

    def body(do_ref, q_ref, k_ref, v_ref, cq_ref, ck_ref, lse_ref, dl_ref, dq_ref, dcq_ref, acc_scr, rs_scr):
        qi, ki = pl.program_id(1), pl.program_id(2)

        @pl.when(ki == 0)
        def _():
            acc_scr[...] = jnp.zeros_like(acc_scr)
            rs_scr[...] = jnp.zeros_like(rs_scr)

        left, right = _half_masks(FOX_T)

        @pl.when(ki <= qi)
        def _():
            q, k, v, dov = q_ref[...], k_ref[...], v_ref[...].astype(BF16), do_ref[...]
            cqb, lseb, dlb = cq_ref[...], lse_ref[...], dl_ref[...]
            acc = acc_scr[...]
            for h, m in enumerate((left, right)):
                s = _fox_scores(q, k, cqb, ck_ref, h, m, ki == qi)
                p = jnp.exp(s - lseb[:, h * HEAD_DIM:h * HEAD_DIM + 1])
                dp = lax.dot_general((dov * m).astype(BF16), v, NT_DIMS, preferred_element_type=F32)
                ds = p * (dp - dlb[:, h * HEAD_DIM:h * HEAD_DIM + 1])
                acc = acc + m * jnp.dot(ds.astype(BF16), k, preferred_element_type=F32)
                rs_scr[h] = rs_scr[h] + jnp.sum(ds, axis=1, keepdims=True)
            acc_scr[...] = acc

        @pl.when(ki == qi)
        def _():
            dq_ref[...] = acc_scr[...] * scale
            dcq_ref[...] = left * rs_scr[0] + right * rs_scr[1]

    qspec = pl.BlockSpec((FOX_T, LANES), lambda p, qi, ki: (qi, p))
    kspec = pl.BlockSpec((FOX_T, LANES), lambda p, qi, ki: (jnp.minimum(ki, qi), p))
    vspec = pl.BlockSpec((FOX_T, LANES), lambda p, qi, ki: (jnp.minimum(ki, qi), 6 * N_PAIRS + p))
    ckspec = pl.BlockSpec((1, 8, FOX_T), lambda p, qi, ki: (p, 0, jnp.minimum(ki, qi)))
    return pl.pallas_call(
        body, name=name, grid=(N_PAIRS, nq, nq), in_specs=[qspec, qspec, kspec, vspec, qspec, ckspec, qspec, qspec],
        out_specs=[qspec, qspec], out_shape=[jax.ShapeDtypeStruct((S, MIX_HALF), F32)] * 2,
        scratch_shapes=[pltpu.VMEM((FOX_T, LANES), F32), pltpu.VMEM((2, FOX_T, 1), F32)],
        compiler_params=_cparams("parallel", "parallel", "arbitrary"))(do, qn, kn, proj, cum_b, cum_tp, lse_b, delta_b)


def _fox_bwd_dkv(do, qn, kn, proj, cum_b, cum_tp, lse_b, delta_b, name):
    S = qn.shape[0]
    nq = S // FOX_T
    scale = HEAD_DIM ** -0.5

    def body(do_ref, q_ref, k_ref, v_ref, cq_ref, ck_ref, lse_ref, dl_ref, dk_ref, dv_ref, dck_ref, dk_scr, dv_scr, dck_scr):
        ki, qi = pl.program_id(1), pl.program_id(2)

        @pl.when(qi == 0)
        def _():
            dk_scr[...] = jnp.zeros_like(dk_scr)
            dv_scr[...] = jnp.zeros_like(dv_scr)
            dck_scr[...] = jnp.zeros_like(dck_scr)

        left, right = _half_masks(FOX_T)

        @pl.when(qi >= ki)
        def _():
            q, k, v, dov = q_ref[...], k_ref[...], v_ref[...].astype(BF16), do_ref[...]
            cqb, lseb, dlb = cq_ref[...], lse_ref[...], dl_ref[...]
            dob = dov.astype(BF16)
            dk, dv = dk_scr[...], dv_scr[...]
            for h, m in enumerate((left, right)):
                s = _fox_scores(q, k, cqb, ck_ref, h, m, ki == qi)
                p = jnp.exp(s - lseb[:, h * HEAD_DIM:h * HEAD_DIM + 1])
                dp = lax.dot_general((dov * m).astype(BF16), v, NT_DIMS, preferred_element_type=F32)
                ds = p * (dp - dlb[:, h * HEAD_DIM:h * HEAD_DIM + 1])
                dv = dv + m * lax.dot_general(p.astype(BF16), dob, TN_DIMS, preferred_element_type=F32)
                dk = dk + m * lax.dot_general(ds.astype(BF16), q, TN_DIMS, preferred_element_type=F32)
                dck_scr[h:h + 1, :] = dck_scr[h:h + 1, :] - jnp.sum(ds, axis=0, keepdims=True)
            dk_scr[...] = dk
            dv_scr[...] = dv

        @pl.when(qi == nq - 1)
        def _():
            dk_ref[...] = dk_scr[...] * scale
            dv_ref[...] = dv_scr[...]
            dck_ref[0] = dck_scr[...]

    qspec = pl.BlockSpec((FOX_T, LANES), lambda p, ki, qi: (jnp.maximum(qi, ki), p))
    kspec = pl.BlockSpec((FOX_T, LANES), lambda p, ki, qi: (ki, p))
    vspec = pl.BlockSpec((FOX_T, LANES), lambda p, ki, qi: (ki, 6 * N_PAIRS + p))
    ckspec = pl.BlockSpec((1, 8, FOX_T), lambda p, ki, qi: (p, 0, ki))
    full = jax.ShapeDtypeStruct((S, MIX_HALF), F32)
    return pl.pallas_call(
        body, name=name, grid=(N_PAIRS, nq, nq), in_specs=[qspec, qspec, kspec, vspec, qspec, ckspec, qspec, qspec],
        out_specs=[kspec, kspec, ckspec], out_shape=[full, full, jax.ShapeDtypeStruct((N_PAIRS, 8, S), F32)],
        scratch_shapes=[pltpu.VMEM((FOX_T, LANES), F32), pltpu.VMEM((FOX_T, LANES), F32), pltpu.VMEM((8, FOX_T), F32)],
        compiler_params=_cparams("parallel", "parallel", "arbitrary"))(do, qn, kn, proj, cum_b, cum_tp, lse_b, delta_b)


def _ff_bwd(rc, f_t, name):
    def body(rc_ref, f_ref, d_ref, s_ref):
        d = rc_ref[...] * (1.0 - _sigmoid(f_ref[...]))
        d_ref[...] = d
        s_ref[...] = jnp.sum(d, axis=1, keepdims=True)

    return pl.pallas_call(body, name=name, out_shape=[jax.ShapeDtypeStruct(rc.shape, F32), jax.ShapeDtypeStruct((rc.shape[0], 1), F32)])(rc, f_t)


def _fox_block_fwd(proj, b_f, q_gain, k_gain, bavg, pfx):
    S = proj.shape[0]

    def prep(qv, kv, ba, qg, kg):
        return _head_rms(qv, ba)[0] * qg, _head_rms(kv, ba)[0] * kg

    qn, kn = _ew(prep, pfx + "_prep", [(proj, MIX_HALF, 4), (proj, MIX_HALF, 5)], consts=[bavg, q_gain, k_gain],
                 outs=[(MIX_HALF, BF16), (MIX_HALF, BF16)])
    f0 = 7 * MIX_HALF + GLA_RANK
    f_t = proj[:, f0:f0 + FOX_HEADS].T + b_f.reshape(FOX_HEADS, 1)
    cum = _cum_lanes(f_t, pfx + "_cum", False, pre=_log_sigmoid)
    cum_b = jnp.repeat(cum.T, HEAD_DIM, axis=1)
    cum_tp = jnp.pad(cum.reshape(N_PAIRS, 2, S), ((0, 0), (0, 6), (0, 0)))
    o, lse_b = _fox_fwd(qn, kn, proj, cum_b, cum_tp, pfx + "_attn")
    return o, dict(qn=qn, kn=kn, f_t=f_t, cum_b=cum_b, cum_tp=cum_tp, o=o, lse_b=lse_b)


def _fox_block_bwd(do, proj, q_gain, k_gain, bavg, res, pfx):
    qn, kn, o = res["qn"], res["kn"], res["o"]
    S = proj.shape[0]
    delta_b, = _ew(lambda a, b, ba: _head_mean(a * b, ba) * float(HEAD_DIM), pfx + "_delta", [do, o], consts=[bavg], outs=[(MIX_HALF, F32)])
    args = (do, qn, kn, proj, res["cum_b"], res["cum_tp"], res["lse_b"], delta_b)
    dqn, dcq_b = _fox_bwd_dq(*args, pfx + "_dq")
    dkn, dv, dck = _fox_bwd_dkv(*args, pfx + "_dkv")

    def prep_bwd(dq, dk, qv, kv, ba, qg, kg):
        nq, rq = _head_rms(qv, ba)
        nk, rk = _head_rms(kv, ba)
        return (_head_rms_bwd(dq * qg, nq, rq, ba), _head_rms_bwd(dk * kg, nk, rk, ba),
                jnp.sum(dq * nq, axis=0, keepdims=True), jnp.sum(dk * nk, axis=0, keepdims=True))

    dfq, dfk, dqg, dkg = _ew(prep_bwd, pfx + "_prep_bwd", [dqn, dkn, (proj, MIX_HALF, 4), (proj, MIX_HALF, 5)],
                             consts=[bavg, q_gain, k_gain], outs=[(MIX_HALF, F32), (MIX_HALF, F32)], sums=[MIX_HALF, MIX_HALF])
    dcum = dck[:, :2, :].reshape(FOX_HEADS, S) + dcq_b[:, ::HEAD_DIM].T
    rc = _cum_lanes(dcum, pfx + "_rcum", True)
    dff_t, db_f = _ff_bwd(rc, res["f_t"], pfx + "_ff_bwd")
    grads = dict(b_f=db_f.reshape(-1), q_gain=dqg.reshape(-1, HEAD_DIM), k_gain=dkg.reshape(-1, HEAD_DIM))
    return (dfq, dfk, dv, dff_t.T), grads


WEIGHTS = ['ada_w', 'ada_b', 'even_w_in', 'even_w_out', 'gla_w_lr', 'gla_b_lr', 'gla_gain', 'fox_b_f', 'fox_q_gain', 'fox_k_gain',
           'odd_w_in', 'odd_w_out', 's5_lam_re', 's5_lam_im', 's5_log_dt', 's5_b_re', 's5_b_im', 's5_c_re', 's5_c_im', 's5_d',
           's5_w_glu', 's5_b_glu', 'sgu_ln_gain', 'sgu_ln_bias', 'sgu_w_s', 'sgu_b_s', 'mlp_w1', 'mlp_w2']
ARGS = ['x', 'c'] + WEIGHTS + ['loss_target'] + ['m_' + w for w in WEIGHTS] + ['v_' + w for w in WEIGHTS]

EVEN_COLS = 3608
EVEN_PAD = 8 * MIX_HALF
MOD = 6 * D_MODEL
MOD_SHARD = MOD // N_CHIPS

SHARDED = [("even_w_in", (1, 1024, 902), 2), ("even_w_out", (1, 256, 1024), 1), ("odd_w_in", (1, 1024, 384), 2),
           ("odd_w_out", (1, 256, 1024), 1), ("mlp_w1", (2, 1024, 1024), 2), ("mlp_w2", (2, 1024, 1024), 1),
           ("gla_w_lr", (1, 16, 128), 2), ("s5_w_glu", (1, 128, 512), 1), ("s5_b_glu", (1, 128), 1),
           ("sgu_ln_gain", (1, 128), 1), ("sgu_ln_bias", (1, 128), 1)]
PACK_COLS = 512
PACK_ROWS = 12288
REPLICATED = [("gla_b_lr", (1, 512)), ("gla_gain", (1, 8, 64)), ("fox_b_f", (1, 8)), ("fox_q_gain", (1, 8, 64)),
              ("fox_k_gain", (1, 8, 64)), ("s5_lam_re", (1, 32, 64)), ("s5_lam_im", (1, 32, 64)), ("s5_log_dt", (1, 32)),
              ("s5_b_re", (1, 32, 64, 16)), ("s5_b_im", (1, 32, 64, 16)), ("s5_c_re", (1, 32, 16, 64)), ("s5_c_im", (1, 32, 16, 64)),
              ("s5_d", (1, 32, 16)), ("sgu_w_s", (1, 8, 128, 128)), ("sgu_b_s", (1, 8, 128))]
SMALL_ROWS = 768
BIG_ADAM = {"ada_w": (2048, 1536), "even_w_in": (1024, 902), "even_w_out": (256, 1024), "odd_w_in": (1024, 384),
            "odd_w_out": (256, 1024), "mlp_w1": (2048, 1024), "mlp_w2": (2048, 1024), "s5_w_glu": (128, 512)}


def _pack_rows(pieces, rows):
    flat = jnp.concatenate([p.reshape(-1) for p in pieces])
    return jnp.pad(flat, (0, rows * PACK_COLS - flat.shape[0])).reshape(rows, PACK_COLS)


def _unpack(flat, specs):
    out, off = {}, 0
    for name, shape in specs:
        n = math.prod(shape)
        out[name] = flat[off:off + n].reshape(shape)
        off += n
    return out


def _shards_to_full(flat4):
    out, off = {}, 0
    for name, shape, axis in SHARDED:
        n = math.prod(shape)
        seg = flat4[:, off:off + n].reshape((N_CHIPS,) + shape)
        out[name] = jnp.concatenate([seg[k] for k in range(N_CHIPS)], axis=axis)
        off += n
    return out


def _full_to_shards(full):
    cols = []
    for name, shape, axis in SHARDED:
        parts = jnp.split(full[name], N_CHIPS, axis=axis)
        cols.append(jnp.stack([p.reshape(-1) for p in parts]))
    flat = jnp.concatenate(cols, axis=1)
    return jnp.pad(flat, ((0, 0), (0, PACK_ROWS * PACK_COLS - flat.shape[1]))).reshape(N_CHIPS, PACK_ROWS, PACK_COLS)


def _relu2(t):
    r = jnp.maximum(t, 0.0)
    return r * r


def _silu(t):
    return t * _sigmoid(t)


def _pack_even(w):
    return jnp.concatenate([w[:, :2048], w[:, 2064:3600], w[:, 2048:2064], w[:, 3600:3608],
                            jnp.zeros((w.shape[0], EVEN_PAD - EVEN_COLS), w.dtype)], axis=1)


def _unpack_even(wp):
    return jnp.concatenate([wp[:, :2048], wp[:, 3584:3600], wp[:, 2048:3584], wp[:, 3600:3608]], axis=1)


def _mlp_fwd(h, w1, w2, pfx):
    pre = _mm(h, w1, "nn", pfx + "_up")
    return pre, _mm(pre, w2, "nn", pfx + "_down", a_pro=_relu2)


def _mlp_bwd(dm, h, pre, w1, w2, pfx):
    dpre = _mm(dm, w2, "nt", pfx + "_dpre", epi=lambda acc, p: acc * (2.0 * jnp.maximum(p, 0.0)), extras=[(pre, "mn")])
    dw2 = _mm(pre, dm, "tn", pfx + "_dw2", a_pro=_relu2)
    dw1 = _mm(h, dpre, "tn", pfx + "_dw1")
    dh = _mm(dpre, w1, "nt", pfx + "_dh")
    return dh, dw1, dw2


def _step(args):
    a = dict(zip(ARGS, args, strict=True))
    x0 = a["x"][0]
    target = a["loss_target"][0]
    mx, my, mc = lax.axis_index("x"), lax.axis_index("y"), lax.axis_index("c")
    chip = 2 * mx + my
    dev = 2 * chip + mc
    bavg = _head_avg_matrix(MIX_HALF)

    c_all = _gather8(jnp.pad(a["c"], ((0, 7), (0, 0))), "c_gather")[:, :, 0, :].reshape(2 * N_CHIPS, D_MODEL)
    ada_b_shard = lax.dynamic_slice_in_dim(a["ada_b"], chip * MOD_SHARD, MOD_SHARD, axis=1)
    mod_sh = [_mm(c_all, a["ada_w"][l], "nn", f"mod{l}", a_pro=_silu, epi=lambda acc, b: acc + b, extras=[(ada_b_shard[l:l + 1], "n")])
              for l in range(2)]
    small3 = jnp.zeros((8, MOD_SHARD), F32)
    for r, n in enumerate(("s5_b_glu", "sgu_ln_gain", "sgu_ln_bias")):
        small3 = small3.at[r, :LANES].set(a[n][0])
    mod_all = _chip_exchange(jnp.concatenate(mod_sh + [small3]), "mod_gather", True)
    mods = []
    for l in range(2):
        full = mod_all[:, 8 * l:8 * l + 8].transpose(1, 0, 2).reshape(8, MOD)
        mods.append(jnp.split(lax.dynamic_slice_in_dim(full, dev, 1, axis=0), 6, axis=1))
    b_glu, ln_gain, ln_bias = (mod_all[:, 16 + r, :LANES].reshape(1, MIX_HALF) for r in range(3))

    shard = _pack_rows([a[n] for n, _, _ in SHARDED], PACK_ROWS).astype(BF16)
    half = lax.dynamic_slice_in_dim(shard, mc * (PACK_ROWS // 2), PACK_ROWS // 2, axis=0)
    halves = _pair_gather(_chip_exchange(half, "w_chips", True), "w_pair")
    w = _shards_to_full(halves.transpose(1, 0, 2, 3).reshape(N_CHIPS, PACK_ROWS * PACK_COLS))
    w_even = _pack_even(w["even_w_in"][0])
    w_lr_pad = jnp.zeros((MIX_HALF, MIX_HALF), BF16).at[:GLA_RANK].set(w["gla_w_lr"][0])
    gla_b_lr = a["gla_b_lr"]
    gla_gain, q_gain, k_gain = (a[n].reshape(1, MIX_HALF) for n in ("gla_gain", "fox_q_gain", "fox_k_gain"))
    s5w = dict(lam_re=a["s5_lam_re"][0], lam_im=a["s5_lam_im"][0], log_dt=a["s5_log_dt"][0], b_re=a["s5_b_re"][0], b_im=a["s5_b_im"][0],
               c_re=a["s5_c_re"][0], c_im=a["s5_c_im"][0], d=a["s5_d"][0], w_glu=w["s5_w_glu"][0], b_glu=b_glu)
    sgu_wm, sgu_bt = _sgu_tables(a["sgu_w_s"][0], a["sgu_b_s"][0])

    sh1, sc1, g1, sh2, sc2, g2 = mods[0]
    _, h1_0 = _res_rms(x0, sc1, sh1, "l0_norm1")
    proj0 = _mm(h1_0, w_even, "nn", "l0_proj")
    og, gla_res = _gla_block_fwd(proj0, w_lr_pad, gla_b_lr, gla_gain, bavg, "gla")
    of, fox_res = _fox_block_fwd(proj0, a["fox_b_f"][0], q_gain, k_gain, bavg, "fox")
    mixed0 = jnp.concatenate([og, of], axis=1)
    y0 = _mm(mixed0, w["even_w_out"][0], "nn", "l0_out")
    x1, h2_0 = _res_rms(x0, sc2, sh2, "l0_norm2", y=y0, g=g1)
    pre0, m0 = _mlp_fwd(h2_0, w["mlp_w1"][0], w["mlp_w2"][0], "l0_mlp")
    sh1b, sc1b, g1b, sh2b, sc2b, g2b = mods[1]
    x2, h1_1 = _res_rms(x1, sc1b, sh1b, "l1_norm1", y=m0, g=g2)
    proj1 = _mm(h1_1, w["odd_w_in"][0], "nn", "l1_proj")
    ys5, s5_res = _s5_block_fwd(proj1[:, :MIX_HALF], s5w, "s5")
    ysgu = _sgu_fwd(proj1, ln_gain, ln_bias, sgu_wm, sgu_bt, "sgu")
    mixed1 = jnp.concatenate([ys5, ysgu], axis=1)
    y1 = _mm(mixed1, w["odd_w_out"][0], "nn", "l1_out")
    x3, h2_1 = _res_rms(x2, sc2b, sh2b, "l1_norm2", y=y1, g=g1b)
    pre1, m1 = _mlp_fwd(h2_1, w["mlp_w1"][1], w["mlp_w2"][1], "l1_mlp")
    loss_b, dx4, dm1, dg2b = _res_loss(x3, m1, g2b, target, "loss")
    loss = lax.psum(loss_b[0, 0], ("x", "y", "c"))

    full = {}
    dh2_1, dw1_1, dw2_1 = _mlp_bwd(dm1, h2_1, pre1, w["mlp_w1"][1], w["mlp_w2"][1], "l1_mlp")
    dx3, dy1, dg1b, dsc2b, dsh2b = _res_rms_bwd(x3, dh2_1, sc2b, dx4, "l1_norm2_bwd", y=y1, g=g1b)
    dmixed1 = _mm(dy1, w["odd_w_out"][0], "nt", "l1_out_dx")
    full["odd_w_out"] = _mm(mixed1, dy1, "tn", "l1_out_dw")[None]
    du, s5g = _s5_block_bwd(dmixed1[:, :MIX_HALF], s5w, s5_res, "s5")
    dzu, dzv, dws, dbt, dlg, dlb = _sgu_bwd(dmixed1[:, MIX_HALF:], proj1, ln_gain, ln_bias, sgu_wm, sgu_bt, "sgu_bwd")
    g_ws, g_bs = _sgu_grads(dws, dbt)
    dproj1 = jnp.concatenate([du, dzu, dzv], axis=1)
    full["odd_w_in"] = _mm(h1_1, dproj1, "tn", "l1_proj_dw")[None]
    dh1_1 = _mm(dproj1, w["odd_w_in"][0], "nt", "l1_proj_dx")
    dx2, dm0, dg2, dsc1b, dsh1b = _res_rms_bwd(x2, dh1_1, sc1b, dx3, "l1_norm1_bwd", y=m0, g=g2)
    dh2_0, dw1_0, dw2_0 = _mlp_bwd(dm0, h2_0, pre0, w["mlp_w1"][0], w["mlp_w2"][0], "l0_mlp")
    full["mlp_w1"] = jnp.stack([dw1_0, dw1_1])
    full["mlp_w2"] = jnp.stack([dw2_0, dw2_1])
    dx1, dy0, dg1, dsc2, dsh2 = _res_rms_bwd(x1, dh2_0, sc2, dx2, "l0_norm2_bwd", y=y0, g=g1)
    dmixed0 = _mm(dy0, w["even_w_out"][0], "nt", "l0_out_dx")
    full["even_w_out"] = _mm(mixed0, dy0, "tn", "l0_out_dw")[None]
    (dgq, dgk, dgv, dgg, dsmall), glag = _gla_block_bwd(dmixed0[:, :MIX_HALF], proj0, w_lr_pad, gla_gain, bavg, gla_res, "gla")
    (dfq, dfk, dfv, dff), foxg = _fox_block_bwd(dmixed0[:, MIX_HALF:], proj0, q_gain, k_gain, bavg, fox_res, "fox")
    dsmall = lax.dynamic_update_slice(dsmall, dff, (0, GLA_RANK))
    dproj0 = jnp.concatenate([dgq, dgk, dgv, dgg, dfq, dfk, dfv, dsmall], axis=1)
    full["even_w_in"] = _unpack_even(_mm(h1_0, dproj0, "tn", "l0_proj_dw"))[None]
    dh1_0 = _mm(dproj0, w_even, "nt", "l0_proj_dx")
    grad_x, dsc1, dsh1 = _res_rms_bwd(x0, dh1_0, sc1, dx1, "l0_norm1_bwd")
    full["gla_w_lr"] = glag["w_lr"][None]
    full["s5_w_glu"] = s5g["w_glu"][None]
    full["s5_b_glu"] = s5g["b_glu"][None]
    full["sgu_ln_gain"] = dlg
    full["sgu_ln_bias"] = dlb

    dmod = jnp.concatenate([dsh1, dsc1, dg1, dsh2, dsc2, dg2, dsh1b, dsc1b, dg1b, dsh2b, dsc2b, dg2b], axis=1)
    dmod_all = _gather8(jnp.pad(dmod, ((0, 7), (0, 0))), "dmod_gather")[:, :, 0, :].reshape(2 * N_CHIPS, 2, MOD)
    grads = {}
    grads["ada_w"] = jnp.stack([
        _mm(c_all, lax.dynamic_slice_in_dim(dmod_all[:, l], chip * MOD_SHARD, MOD_SHARD, axis=1), "tn", f"ada_dw{l}", a_pro=_silu)
        for l in range(2)])
    grads["ada_b"] = _sum_slots(dmod_all.reshape(2 * N_CHIPS, 2 * MOD // PACK_COLS, PACK_COLS), "ada_db").reshape(2, MOD)

    packed = _full_to_shards(full)
    hr = PACK_ROWS // 2
    mine = lax.dynamic_slice_in_dim(packed, mc * hr, hr, axis=1)
    other = lax.dynamic_slice_in_dim(packed, (1 - mc) * hr, hr, axis=1)
    theirs = lax.dynamic_index_in_dim(_pair_gather(other, "g_pair"), 1 - mc, axis=0, keepdims=False)
    pair_sum, = _ew(lambda p, q: p + q, "g_pair_sum", [mine.reshape(N_CHIPS * hr, PACK_COLS), theirs.reshape(N_CHIPS * hr, PACK_COLS)],
                    outs=[(PACK_COLS, F32)])
    arrived = _chip_exchange(pair_sum.reshape(N_CHIPS, hr, PACK_COLS), "g_chips", False)
    reduced = _pair_gather(_sum_slots(arrived, "g_chip_sum"), "g_pair_out").reshape(-1)
    grads.update(_unpack(reduced, [(n, s) for n, s, _ in SHARDED]))

    part = dict(gla_b_lr=glag["b_lr"], gla_gain=glag["gain"], fox_b_f=foxg["b_f"], fox_q_gain=foxg["q_gain"], fox_k_gain=foxg["k_gain"],
                s5_lam_re=s5g["lam_re"], s5_lam_im=s5g["lam_im"], s5_log_dt=s5g["log_dt"], s5_b_re=s5g["b_re"], s5_b_im=s5g["b_im"],
                s5_c_re=s5g["c_re"], s5_c_im=s5g["c_im"], s5_d=s5g["d"], sgu_w_s=g_ws, sgu_b_s=g_bs)
    parts_all = _gather8(_pack_rows([part[n] for n, _ in REPLICATED], SMALL_ROWS), "rep_gather")
    rep = _sum_slots(parts_all.reshape(2 * N_CHIPS, SMALL_ROWS, PACK_COLS), "rep_sum").reshape(-1)
    grads.update(_unpack(rep, REPLICATED))

    delta, new_m, new_v = {}, {}, {}
    for n, shape2 in BIG_ADAM.items():
        d, nm, nv = _adamw(a[n].reshape(shape2), grads[n].reshape(shape2), a["m_" + n].reshape(shape2), a["v_" + n].reshape(shape2), "adamw_" + n)
        delta[n], new_m[n], new_v[n] = (t.reshape(a[n].shape) for t in (d, nm, nv))
    small = [n for n in WEIGHTS if n not in BIG_ADAM]
    spec = [(n, a[n].shape) for n in small]
    packs = [_pack_rows([src[n] for n in small], SMALL_ROWS) for src in
             (a, grads, {n: a["m_" + n] for n in small}, {n: a["v_" + n] for n in small})]
    for tgt, res in zip((delta, new_m, new_v), _adamw(*packs, "adamw_small")):
        tgt.update(_unpack(res.reshape(-1), spec))
    outs = [loss, grad_x[None]]
    for group in (grads, delta, new_m, new_v):
        outs += [group[n].reshape(a[n].shape) for n in WEIGHTS]
    return tuple(outs)


def kernel(x, c, ada_w, ada_b, even_w_in, even_w_out, gla_w_lr, gla_b_lr, gla_gain, fox_b_f, fox_q_gain, fox_k_gain, odd_w_in,
           odd_w_out, s5_lam_re, s5_lam_im, s5_log_dt, s5_b_re, s5_b_im, s5_c_re, s5_c_im, s5_d, s5_w_glu, s5_b_glu, sgu_ln_gain,
           sgu_ln_bias, sgu_w_s, sgu_b_s, mlp_w1, mlp_w2, loss_target, m_ada_w, m_ada_b, m_even_w_in, m_even_w_out, m_gla_w_lr,
           m_gla_b_lr, m_gla_gain, m_fox_b_f, m_fox_q_gain, m_fox_k_gain, m_odd_w_in, m_odd_w_out, m_s5_lam_re, m_s5_lam_im,
           m_s5_log_dt, m_s5_b_re, m_s5_b_im, m_s5_c_re, m_s5_c_im, m_s5_d, m_s5_w_glu, m_s5_b_glu, m_sgu_ln_gain, m_sgu_ln_bias,
           m_sgu_w_s, m_sgu_b_s, m_mlp_w1, m_mlp_w2, v_ada_w, v_ada_b, v_even_w_in, v_even_w_out, v_gla_w_lr, v_gla_b_lr,
           v_gla_gain, v_fox_b_f, v_fox_q_gain, v_fox_k_gain, v_odd_w_in, v_odd_w_out, v_s5_lam_re, v_s5_lam_im, v_s5_log_dt,
           v_s5_b_re, v_s5_b_im, v_s5_c_re, v_s5_c_im, v_s5_d, v_s5_w_glu, v_s5_b_glu, v_sgu_ln_gain, v_sgu_ln_bias, v_sgu_w_s,
           v_sgu_b_s, v_mlp_w1, v_mlp_w2):
    return _step((x, c, ada_w, ada_b, even_w_in, even_w_out, gla_w_lr, gla_b_lr, gla_gain, fox_b_f, fox_q_gain, fox_k_gain,
                  odd_w_in, odd_w_out, s5_lam_re, s5_lam_im, s5_log_dt, s5_b_re, s5_b_im, s5_c_re, s5_c_im, s5_d, s5_w_glu,
                  s5_b_glu, sgu_ln_gain, sgu_ln_bias, sgu_w_s, sgu_b_s, mlp_w1, mlp_w2, loss_target, m_ada_w, m_ada_b,
                  m_even_w_in, m_even_w_out, m_gla_w_lr, m_gla_b_lr, m_gla_gain, m_fox_b_f, m_fox_q_gain, m_fox_k_gain,
                  m_odd_w_in, m_odd_w_out, m_s5_lam_re, m_s5_lam_im, m_s5_log_dt, m_s5_b_re, m_s5_b_im, m_s5_c_re, m_s5_c_im,
                  m_s5_d, m_s5_w_glu, m_s5_b_glu, m_sgu_ln_gain, m_sgu_ln_bias, m_sgu_w_s, m_sgu_b_s, m_mlp_w1, m_mlp_w2, v_ada_w,
                  v_ada_b, v_even_w_in, v_even_w_out, v_gla_w_lr, v_gla_b_lr, v_gla_gain, v_fox_b_f, v_fox_q_gain, v_fox_k_gain,
                  v_odd_w_in, v_odd_w_out, v_s5_lam_re, v_s5_lam_im, v_s5_log_dt, v_s5_b_re, v_s5_b_im, v_s5_c_re, v_s5_c_im,
                  v_s5_d, v_s5_w_glu, v_s5_b_glu, v_sgu_ln_gain, v_sgu_ln_bias, v_sgu_w_s, v_sgu_b_s, v_mlp_w1, v_mlp_w2))
```

```python
import functools
import math

import jax
import jax.numpy as jnp
import numpy as np
from jax import lax
from jax.experimental import pallas as pl
from jax.experimental.pallas import tpu as pltpu

F32 = jnp.float32
BF16 = jnp.bfloat16
MESH = pl.DeviceIdType.MESH
ANY = pl.BlockSpec(memory_space=pl.ANY)
DMA_SEM = pltpu.SemaphoreType.DMA

D_MODEL = 1024
HEAD_DIM = 64
MIX_HALF = 512
GLA_RANK = 16
GLA_TAU = 16.0
GLA_CHUNK = 64
S5_GROUPS = 32
S5_GROUP_WIDTH = 16
S5_STATE = 64
S5_N = S5_GROUPS * S5_STATE
SGU_GROUPS = 8
SGU_CHUNK = 128
D_FF = 4096
EPS = 1e-6
N_CHIPS = 4
LANES = 128
VMEM_LIMIT = 48 * 1024 * 1024
PAIR_COPIES = 16

ADAM_LR = 0.001
ADAM_B1 = 0.9
ADAM_B2 = 0.999
ADAM_EPS = 1e-08
ADAM_WD = 0.01
ADAM_STEP = 10


def _cparams(*sem):
    return pltpu.CompilerParams(dimension_semantics=sem, vmem_limit_bytes=VMEM_LIMIT)


def _pair_gather(x, name):
    lead = x.shape[:-2]
    rows = x.shape[-2]
    nsplit = max(1, PAIR_COPIES // max(1, math.prod(lead)))
    while nsplit > 1 and rows % (nsplit * 16):
        nsplit -= 1
    pieces = [idx + (pl.ds(j * (rows // nsplit), rows // nsplit),) for idx in np.ndindex(*lead) for j in range(nsplit)]

    def body(x_ref, o_ref, send_sems, recv_sems, loc_sem):
        mx, my, mc = lax.axis_index("x"), lax.axis_index("y"), lax.axis_index("c")
        sib = (mx, my, 1 - mc)
        loc = pltpu.make_async_copy(x_ref, o_ref.at[mc], loc_sem)
        loc.start()

        def piece(j, slot):
            return pltpu.make_async_remote_copy(src_ref=x_ref.at[pieces[j]], dst_ref=o_ref.at[(slot,) + pieces[j]],
                                                send_sem=send_sems.at[j], recv_sem=recv_sems.at[j], device_id=sib, device_id_type=MESH)

        sends = [piece(j, mc) for j in range(len(pieces))]
        for cp in sends:
            cp.start()
        for j in range(len(pieces)):
            piece(j, 1 - mc).wait_recv()
        for cp in sends:
            cp.wait_send()
        loc.wait()

    return pl.pallas_call(
        body, name=name, out_shape=jax.ShapeDtypeStruct((2,) + x.shape, x.dtype), in_specs=[ANY], out_specs=ANY,
        scratch_shapes=[DMA_SEM((len(pieces),)), DMA_SEM((len(pieces),)), DMA_SEM])(x)


def _chip_exchange(x, name, bcast):
    blk = x.shape if bcast else x.shape[1:]

    def body(x_ref, o_ref, send_sems, recv_sems, loc_sem):
        mx, my, mc = lax.axis_index("x"), lax.axis_index("y"), lax.axis_index("c")
        me = 2 * mx + my
        peers = [(1 - mx, my), (mx, 1 - my), (1 - mx, 1 - my)]

        def src(k):
            return x_ref if bcast else x_ref.at[k]

        loc = pltpu.make_async_copy(src(me), o_ref.at[me], loc_sem)
        loc.start()
        sends = []
        for j, (px, py) in enumerate(peers):
            cp = pltpu.make_async_remote_copy(src_ref=src(2 * px + py), dst_ref=o_ref.at[me], send_sem=send_sems.at[j],
                                              recv_sem=recv_sems.at[j], device_id=(px, py, mc), device_id_type=MESH)
            cp.start()
            sends.append(cp)
        for j, (px, py) in enumerate(peers):
            pltpu.make_async_remote_copy(src_ref=src(me), dst_ref=o_ref.at[2 * px + py], send_sem=send_sems.at[j],
                                         recv_sem=recv_sems.at[j], device_id=(px, py, mc), device_id_type=MESH).wait_recv()
        for cp in sends:
            cp.wait_send()
        loc.wait()

    return pl.pallas_call(
        body, name=name, out_shape=jax.ShapeDtypeStruct((N_CHIPS,) + tuple(blk), x.dtype), in_specs=[ANY], out_specs=ANY,
        scratch_shapes=[DMA_SEM((3,)), DMA_SEM((3,)), DMA_SEM])(x)


def _gather8(x, name):
    return _chip_exchange(_pair_gather(x, name + "_pair"), name + "_chips", True)


def _tile(n, want):
    if n <= want:
        return n
    t = (want // LANES) * LANES
    while t >= LANES:
        if n % t == 0:
            return t
        t -= LANES
    raise ValueError(f"no lane-aligned tile for {n}")


_DIMS = {"nn": (((1,), (0,)), ((), ())), "nt": (((1,), (1,)), ((), ())), "tn": (((0,), (0,)), ((), ()))}


MM_FULL_K = 4096
MM_SLAB_K = 2048


def _mm(a, b, mode, name, *, a_pro=None, epi=None, extras=(), out_dtype=F32, tm=512, tn=512, tk=None, a_cols=None):
    c0, csize = a_cols if a_cols is not None else (0, a.shape[1])
    if mode == "tn":
        K, M = a.shape[0], csize
    else:
        M, K = a.shape[0], csize
    N = b.shape[0] if mode == "nt" else b.shape[1]
    assert (b.shape[1] if mode == "nt" else b.shape[0]) == K, (a.shape, b.shape, mode)
    if tk is None:
        tk = K if (mode != "tn" and K <= MM_FULL_K) else MM_SLAB_K
    tm, tn, tk = _tile(M, tm), _tile(N, tn), _tile(K, tk)
    nk = K // tk
    if mode == "tn":
        assert c0 % tm == 0
        a_spec = pl.BlockSpec((tk, tm), lambda i, j, k: (k, i + c0 // tm))
    else:
        assert c0 % tk == 0
        a_spec = pl.BlockSpec((tm, tk), lambda i, j, k: (i, k + c0 // tk))
    b_spec = pl.BlockSpec((tn, tk), lambda i, j, k: (j, k)) if mode == "nt" else pl.BlockSpec((tk, tn), lambda i, j, k: (k, j))
    ex_specs = []
    for arr, kind in extras:
        if kind == "mn":
            assert arr.shape == (M, N)
            ex_specs.append(pl.BlockSpec((tm, tn), lambda i, j, k: (i, j)))
        else:
            assert arr.shape == (1, N)
            ex_specs.append(pl.BlockSpec((1, tn), lambda i, j, k: (0, j)))
    n_ex = len(extras)

    def body(*refs):
        a_ref, b_ref = refs[:2]
        ex_refs = refs[2:2 + n_ex]
        o_ref = refs[2 + n_ex]
        acc_ref = refs[3 + n_ex] if nk > 1 else None
        k = pl.program_id(2)
        av = a_ref[...]
        if a_pro is not None:
            av = a_pro(av)
        part = lax.dot_general(av.astype(BF16), b_ref[...].astype(BF16), _DIMS[mode], preferred_element_type=F32)
        if nk == 1:
            if epi is not None:
                part = epi(part, *[r[...] for r in ex_refs])
            o_ref[...] = part.astype(o_ref.dtype)
            return

        @pl.when(k == 0)
        def _():
            acc_ref[...] = part

        @pl.when(k > 0)
        def _():
            acc_ref[...] += part

        @pl.when(k == nk - 1)
        def _():
            acc = acc_ref[...]
            if epi is not None:
                acc = epi(acc, *[r[...] for r in ex_refs])
            o_ref[...] = acc.astype(o_ref.dtype)

    return pl.pallas_call(
        body, name=name, grid=(M // tm, N // tn, nk),
        in_specs=[a_spec, b_spec] + ex_specs,
        out_specs=pl.BlockSpec((tm, tn), lambda i, j, k: (i, j)),
        out_shape=jax.ShapeDtypeStruct((M, N), out_dtype),
        scratch_shapes=[pltpu.VMEM((tm, tn), F32)] if nk > 1 else [],
        compiler_params=_cparams("parallel", "parallel", "arbitrary"))(a, b, *[e[0] for e in extras])


ROWS = 256


def _row_spec(w, ts=ROWS):
    return pl.BlockSpec((ts, w), lambda i: (i, 0))


def _vec_spec(w):
    return pl.BlockSpec((1, w), lambda i: (0, 0))


def _res_rms(x, sc, sh, name, y=None, g=None):
    S, D = x.shape
    has_res = y is not None

    def body(*refs):
        if has_res:
            x_ref, y_ref, g_ref, sc_ref, sh_ref, xo_ref, h_ref = refs
            xv = x_ref[...] + g_ref[...] * y_ref[...]
            xo_ref[...] = xv
        else:
            x_ref, sc_ref, sh_ref, h_ref = refs
            xv = x_ref[...]
        r = lax.rsqrt(jnp.mean(xv * xv, axis=-1, keepdims=True) + EPS)
        h_ref[...] = (xv * r * (1.0 + sc_ref[...]) + sh_ref[...]).astype(BF16)

    row, vec = _row_spec(D), _vec_spec(D)
    if has_res:
        return pl.pallas_call(body, name=name, grid=(S // ROWS,), in_specs=[row, row, vec, vec, vec], out_specs=[row, row],
                              out_shape=[jax.ShapeDtypeStruct((S, D), F32), jax.ShapeDtypeStruct((S, D), BF16)],
                              compiler_params=_cparams("parallel"))(x, y, g, sc, sh)
    h = pl.pallas_call(body, name=name, grid=(S // ROWS,), in_specs=[row, vec, vec], out_specs=row,
                       out_shape=jax.ShapeDtypeStruct((S, D), BF16), compiler_params=_cparams("parallel"))(x, sc, sh)
    return x, h


def _res_rms_bwd(x, dh, sc, dres, name, y=None, g=None):
    S, D = x.shape
    has_res = y is not None

    def body(*refs):
        if has_res:
            x_ref, dh_ref, sc_ref, dres_ref, y_ref, g_ref, dx_ref, dy_ref, dg_ref, dsc_ref, dsh_ref = refs
        else:
            x_ref, dh_ref, sc_ref, dres_ref, dx_ref, dsc_ref, dsh_ref = refs
        first = pl.program_id(0) == 0
        xv = x_ref[...]
        dh = dh_ref[...]
        r = lax.rsqrt(jnp.mean(xv * xv, axis=-1, keepdims=True) + EPS)
        xn = xv * r
        dxn = dh * (1.0 + sc_ref[...])
        dx = dres_ref[...] + r * (dxn - xn * jnp.mean(dxn * xn, axis=-1, keepdims=True))
        dx_ref[...] = dx
        parts = [(dsc_ref, jnp.sum(dh * xn, axis=0, keepdims=True)), (dsh_ref, jnp.sum(dh, axis=0, keepdims=True))]
        if has_res:
            dy_ref[...] = dx * g_ref[...]
            parts.append((dg_ref, jnp.sum(dx * y_ref[...], axis=0, keepdims=True)))
        for ref, val in parts:
            @pl.when(first)
            def _(ref=ref, val=val):
                ref[...] = val

            @pl.when(jnp.logical_not(first))
            def _(ref=ref, val=val):
                ref[...] += val

    row, vec = _row_spec(D), _vec_spec(D)
    full = jax.ShapeDtypeStruct((S, D), F32)
    v = jax.ShapeDtypeStruct((1, D), F32)
    if has_res:
        return pl.pallas_call(body, name=name, grid=(S // ROWS,), in_specs=[row, row, vec, row, row, vec],
                              out_specs=[row, row, vec, vec, vec], out_shape=[full, full, v, v, v],
                              compiler_params=_cparams("arbitrary"))(x, dh, sc, dres, y, g)
    return pl.pallas_call(body, name=name, grid=(S // ROWS,), in_specs=[row, row, vec, row],
                          out_specs=[row, vec, vec], out_shape=[full, v, v],
                          compiler_params=_cparams("arbitrary"))(x, dh, sc, dres)


def _res_loss(x, m, g, target, name):
    S, D = x.shape

    def body(x_ref, m_ref, g_ref, t_ref, loss_ref, dx_ref, dm_ref, dg_ref):
        first = pl.program_id(0) == 0
        mv = m_ref[...]
        err = x_ref[...] + g_ref[...] * mv - t_ref[...]
        dx = err * (1.0 / D)
        dx_ref[...] = dx
        dm_ref[...] = dx * g_ref[...]
        part = 0.5 * jnp.sum(jnp.mean(err * err, axis=-1, keepdims=True), axis=0, keepdims=True)
        dg = jnp.sum(dx * mv, axis=0, keepdims=True)

        @pl.when(first)
        def _():
            loss_ref[...] = jnp.broadcast_to(part, loss_ref.shape)
            dg_ref[...] = dg

        @pl.when(jnp.logical_not(first))
        def _():
            loss_ref[...] += jnp.broadcast_to(part, loss_ref.shape)
            dg_ref[...] += dg

    row, vec = _row_spec(D), _vec_spec(D)
    full = jax.ShapeDtypeStruct((S, D), F32)
    return pl.pallas_call(body, name=name, grid=(S // ROWS,), in_specs=[row, row, vec, row],
                          out_specs=[pl.BlockSpec((8, LANES), lambda i: (0, 0)), row, row, vec],
                          out_shape=[jax.ShapeDtypeStruct((8, LANES), F32), full, full, jax.ShapeDtypeStruct((1, D), F32)],
                          compiler_params=_cparams("arbitrary"))(x, m, g, target)


def _adamw(w, g, m, v, name):
    R, C = w.shape
    tr = R if R <= 256 else 256
    assert R % tr == 0

    def body(w_ref, g_ref, m_ref, v_ref, d_ref, nm_ref, nv_ref):
        gv = g_ref[...]
        nm = ADAM_B1 * m_ref[...] + (1.0 - ADAM_B1) * gv
        nv = ADAM_B2 * v_ref[...] + (1.0 - ADAM_B2) * jnp.square(gv)
        m_hat = nm / (1.0 - ADAM_B1 ** ADAM_STEP)
        v_hat = nv / (1.0 - ADAM_B2 ** ADAM_STEP)
        d_ref[...] = -ADAM_LR * (m_hat / (jnp.sqrt(v_hat) + ADAM_EPS) + ADAM_WD * w_ref[...])
        nm_ref[...] = nm
        nv_ref[...] = nv

    spec = pl.BlockSpec((tr, C), lambda i: (i, 0))
    out = jax.ShapeDtypeStruct((R, C), F32)
    return pl.pallas_call(body, name=name, grid=(R // tr,), in_specs=[spec] * 4, out_specs=[spec] * 3,
                          out_shape=[out, out, out], compiler_params=_cparams("parallel"))(w, g, m, v)


def _sum_slots(x, name):
    n, R, C = x.shape
    tr = R if R <= 256 else 256
    assert R % tr == 0

    def body(x_ref, o_ref):
        acc = x_ref[0]
        for j in range(1, n):
            acc = acc + x_ref[j]
        o_ref[...] = acc

    return pl.pallas_call(body, name=name, grid=(R // tr,), in_specs=[pl.BlockSpec((n, tr, C), lambda i: (0, i, 0))],
                          out_specs=pl.BlockSpec((tr, C), lambda i: (i, 0)), out_shape=jax.ShapeDtypeStruct((R, C), x.dtype),
                          compiler_params=_cparams("parallel"))(x)


def _ew(fn, name, tiled, consts=(), outs=(), sums=(), ts=ROWS):
    tiled = [t if isinstance(t, tuple) else (t, t.shape[1], 0) for t in tiled]
    S = tiled[0][0].shape[0]
    n_t, n_c, n_o, n_s = len(tiled), len(consts), len(outs), len(sums)

    def body(*refs):
        ins = [r[...] for r in refs[:n_t + n_c]]
        res = fn(*ins)
        res = res if isinstance(res, (tuple, list)) else (res,)
        assert len(res) == n_o + n_s
        o_refs = refs[n_t + n_c:]
        for r, val in zip(o_refs[:n_o], res[:n_o]):
            r[...] = val.astype(r.dtype)
        first = pl.program_id(0) == 0
        for r, val in zip(o_refs[n_o:], res[n_o:]):
            @pl.when(first)
            def _(r=r, val=val):
                r[...] = val

            @pl.when(jnp.logical_not(first))
            def _(r=r, val=val):
                r[...] += val

    in_specs = [pl.BlockSpec((ts, w), lambda i, cb=cb: (i, cb)) for _, w, cb in tiled]
    in_specs += [pl.BlockSpec(c.shape, lambda i, nd=c.ndim: (0,) * nd) for c in consts]
    out_specs = [_row_spec(w, ts) for w, _ in outs] + [_vec_spec(w) for w in sums]
    out_shape = [jax.ShapeDtypeStruct((S, w), dt) for w, dt in outs] + [jax.ShapeDtypeStruct((1, w), F32) for w in sums]
    res = pl.pallas_call(body, name=name, grid=(S // ts,), in_specs=in_specs, out_specs=out_specs, out_shape=out_shape,
                         compiler_params=_cparams("arbitrary" if sums else "parallel"))(*[t[0] for t in tiled], *consts)
    return res


_GELU_C = math.sqrt(2.0 / math.pi)


def _gelu(x):
    return 0.5 * x * (1.0 + jnp.tanh(_GELU_C * (x + 0.044715 * x * x * x)))


def _dgelu(x):
    t = jnp.tanh(_GELU_C * (x + 0.044715 * x * x * x))
    return 0.5 * (1.0 + t) + 0.5 * x * (1.0 - t * t) * _GELU_C * (1.0 + 3.0 * 0.044715 * x * x)


def _sigmoid(x):
    return 1.0 / (1.0 + jnp.exp(-x))


def _log_sigmoid(x):
    return jnp.minimum(x, 0.0) - jnp.log(1.0 + jnp.exp(-jnp.abs(x)))


S5_TN = 64
S5_TB = 512
SCAN_TN = 32
SCAN_TB = 1024
SCAN_GROUP = 4


def _cmul(ar, ai, br, bi):
    return ar * br - ai * bi, ar * bi + ai * br


def _s5_discretise(lam_re, lam_im, log_dt, b_re, b_im):
    dt = jnp.exp(log_dt)[:, None]
    mag = jnp.exp(lam_re * dt)
    ang = lam_im * dt
    abar_re = mag * jnp.cos(ang)
    abar_im = mag * jnp.sin(ang)
    den = lam_re * lam_re + lam_im * lam_im
    coef_re = ((abar_re - 1.0) * lam_re + abar_im * lam_im) / den
    coef_im = (abar_im * lam_re - (abar_re - 1.0) * lam_im) / den
    bbar_re = coef_re[..., None] * b_re - coef_im[..., None] * b_im
    bbar_im = coef_re[..., None] * b_im + coef_im[..., None] * b_re
    return abar_re, abar_im, bbar_re, bbar_im


def _s5_scan_tables(a_re, a_im, reverse):
    pr, pi = [a_re], [a_im]
    for _ in range(7):
        r, i = _cmul(pr[-1], pi[-1], pr[-1], pi[-1])
        pr.append(r)
        pi.append(i)
    apow = jnp.broadcast_to(jnp.stack([jnp.stack(pr), jnp.stack(pi)])[..., None], (2, 8, a_re.shape[0], LANES))
    n = np.arange(1, LANES + 1)
    if reverse:
        n = n[::-1]
    tr = jnp.ones((a_re.shape[0], LANES), F32)
    ti = jnp.zeros((a_re.shape[0], LANES), F32)
    for k in range(8):
        bit = jnp.asarray(((n >> k) & 1).astype(np.float32))[None, :]
        fr, fi = pr[k][:, None], pi[k][:, None]
        mr = bit * fr + (1.0 - bit)
        mi = bit * fi
        tr, ti = _cmul(tr, ti, mr, mi)
    return apow, jnp.stack([tr, ti])


def _s5_scan(bu, apow, ptab, name, reverse):
    _, N, S = bu.shape
    tn, tb = SCAN_TN, min(SCAN_TB, S)
    nt = S // tb
    nsub = tb // LANES
    order = list(range(nsub - 1, -1, -1) if reverse else range(nsub))

    def tmap(i, t):
        return (0, i, nt - 1 - t) if reverse else (0, i, t)

    def body(bu_ref, ap_ref, pt_ref, x_ref, carry_ref):
        @pl.when(pl.program_id(1) == 0)
        def _():
            carry_ref[...] = jnp.zeros_like(carry_ref)

        lane = lax.broadcasted_iota(jnp.int32, (tn, LANES), 1)
        at_edge = lane == (0 if reverse else LANES - 1)
        edges = {}
        for g in range(0, nsub, SCAN_GROUP):
            subs = order[g:g + SCAN_GROUP]
            xs = [(bu_ref[0, :, pl.ds(sb * LANES, LANES)], bu_ref[1, :, pl.ds(sb * LANES, LANES)]) for sb in subs]
            for k in range(7):
                sh = 1 << k
                ar, ai = ap_ref[0, k], ap_ref[1, k]
                valid = (lane < LANES - sh) if reverse else (lane >= sh)
                nxt = []
                for xr, xi in xs:
                    rr = jnp.where(valid, pltpu.roll(xr, (LANES - sh) if reverse else sh, 1), 0.0)
                    ri = jnp.where(valid, pltpu.roll(xi, (LANES - sh) if reverse else sh, 1), 0.0)
                    nxt.append((xr + ar * rr - ai * ri, xi + ar * ri + ai * rr))
                xs = nxt
            for sb, (xr, xi) in zip(subs, xs):
                x_ref[0, :, pl.ds(sb * LANES, LANES)] = xr
                x_ref[1, :, pl.ds(sb * LANES, LANES)] = xi
                edges[sb] = (jnp.broadcast_to(jnp.sum(jnp.where(at_edge, xr, 0.0), axis=1, keepdims=True), (tn, LANES)),
                             jnp.broadcast_to(jnp.sum(jnp.where(at_edge, xi, 0.0), axis=1, keepdims=True), (tn, LANES)))
        pr, pi = pt_ref[0], pt_ref[1]
        br, bi = ap_ref[0, 7], ap_ref[1, 7]
        cr, ci = carry_ref[0], carry_ref[1]
        for sb in order:
            sl = pl.ds(sb * LANES, LANES)
            x_ref[0, :, sl] = x_ref[0, :, sl] + pr * cr - pi * ci
            x_ref[1, :, sl] = x_ref[1, :, sl] + pr * ci + pi * cr
            er, ei = edges[sb]
            cr, ci = er + br * cr - bi * ci, ei + br * ci + bi * cr
        carry_ref[0] = cr
        carry_ref[1] = ci

    return pl.pallas_call(
        body, name=name, grid=(N // tn, nt),
        in_specs=[pl.BlockSpec((2, tn, tb), tmap), pl.BlockSpec((2, 8, tn, LANES), lambda i, t: (0, 0, i, 0)),
                  pl.BlockSpec((2, tn, LANES), lambda i, t: (0, i, 0))],
        out_specs=pl.BlockSpec((2, tn, tb), tmap), out_shape=jax.ShapeDtypeStruct((2, N, S), F32),
        scratch_shapes=[pltpu.VMEM((2, tn, LANES), F32)], compiler_params=_cparams("parallel", "arbitrary"))(bu, apow, ptab)


def _s5_da(lam, x, name):
    _, N, S = x.shape
    nt = S // S5_TB
    nsub = S5_TB // LANES

    def body(l_ref, x_ref, o_ref, acc_ref, carry_ref):
        t = pl.program_id(1)

        @pl.when(t == 0)
        def _():
            acc_ref[...] = jnp.zeros_like(acc_ref)
            carry_ref[...] = jnp.zeros_like(carry_ref)

        lane = lax.broadcasted_iota(jnp.int32, (S5_TN, LANES), 1)
        cr, ci = carry_ref[0], carry_ref[1]
        ar, ai = acc_ref[0], acc_ref[1]
        for sb in range(nsub):
            sl = pl.ds(sb * LANES, LANES)
            xr, xi = x_ref[0, :, sl], x_ref[1, :, sl]
            pr = jnp.where(lane == 0, cr, pltpu.roll(xr, 1, 1))
            pi = jnp.where(lane == 0, ci, pltpu.roll(xi, 1, 1))
            lr, li = l_ref[0, :, sl], l_ref[1, :, sl]
            ar = ar + lr * pr + li * pi
            ai = ai + li * pr - lr * pi
            last = lane == LANES - 1
            cr = jnp.broadcast_to(jnp.sum(jnp.where(last, xr, 0.0), axis=1, keepdims=True), (S5_TN, LANES))
            ci = jnp.broadcast_to(jnp.sum(jnp.where(last, xi, 0.0), axis=1, keepdims=True), (S5_TN, LANES))
        acc_ref[0] = ar
        acc_ref[1] = ai
        carry_ref[0] = cr
        carry_ref[1] = ci

        @pl.when(t == nt - 1)
        def _():
            o_ref[0] = jnp.sum(ar, axis=1, keepdims=True)
            o_ref[1] = jnp.sum(ai, axis=1, keepdims=True)

    blk = pl.BlockSpec((2, S5_TN, S5_TB), lambda i, t: (0, i, t))
    return pl.pallas_call(
        body, name=name, grid=(N // S5_TN, nt), in_specs=[blk, blk],
        out_specs=pl.BlockSpec((2, S5_TN, 1), lambda i, t: (0, i, 0)), out_shape=jax.ShapeDtypeStruct((2, N, 1), F32),
        scratch_shapes=[pltpu.VMEM((2, S5_TN, LANES), F32), pltpu.VMEM((2, S5_TN, LANES), F32)],
        compiler_params=_cparams("parallel", "arbitrary"))(lam, x)


def _block_diag(t):
    G, a, b = t.shape
    return (t[:, :, None, :] * jnp.eye(G, dtype=t.dtype)[:, None, :, None]).reshape(G * a, G * b)


def _block_diag_take(m, G):
    a, b = m.shape[0] // G, m.shape[1] // G
    m4 = m.reshape(G, a, G, b)
    return jnp.stack([m4[g, :, g, :] for g in range(G)])


def _s5_block_fwd(u, w, pfx):
    a_re, a_im, bb_re, bb_im = _s5_discretise(w["lam_re"], w["lam_im"], w["log_dt"], w["b_re"], w["b_im"])
    bcat = jnp.concatenate([_block_diag(bb_re), _block_diag(bb_im)], axis=0).astype(BF16)
    ccat = jnp.concatenate([_block_diag(jnp.swapaxes(w["c_re"], 1, 2)),
                            -_block_diag(jnp.swapaxes(w["c_im"], 1, 2))], axis=0).astype(BF16)
    S = u.shape[0]
    af_re, af_im = a_re.reshape(-1), a_im.reshape(-1)
    apow, ptab = _s5_scan_tables(af_re, af_im, False)
    bu = _mm(bcat, u, "nt", pfx + "_bu").reshape(2, S5_N, S)
    x = _s5_scan(bu, apow, ptab, pfx + "_scan", False)
    d_row = w["d"].reshape(1, MIX_HALF)
    ys = _mm(x.reshape(2 * S5_N, S), ccat, "tn", pfx + "_y", epi=lambda acc, ut, dr: acc + dr * ut, extras=[(u, "mn"), (d_row, "n")])
    z = _mm(ys, w["w_glu"], "nn", pfx + "_glu", a_pro=_gelu, epi=lambda acc, b: acc + b, extras=[(w["b_glu"].reshape(1, -1), "n")])
    y2, = _ew(lambda ysv, zv: _gelu(ysv) * _sigmoid(zv), pfx + "_gate", [ys, z], outs=[(MIX_HALF, F32)])
    return y2, dict(u=u, x=x, ys=ys, z=z, bcat=bcat, ccat=ccat, a=(af_re, af_im), d_row=d_row)


def _s5_block_bwd(dy2, w, res, pfx):
    u, x, ys, z, bcat, ccat = res["u"], res["x"], res["ys"], res["z"], res["bcat"], res["ccat"]
    S = u.shape[0]

    def gate_bwd(dy, ysv, zv):
        sg = _sigmoid(zv)
        dz = dy * _gelu(ysv) * sg * (1.0 - sg)
        return dz, jnp.sum(dz, axis=0, keepdims=True)

    dz, db_glu = _ew(gate_bwd, pfx + "_gate_bwd", [dy2, ys, z], outs=[(MIX_HALF, F32)], sums=[MIX_HALF])
    dw_glu = _mm(ys, dz, "tn", pfx + "_dwglu", a_pro=_gelu)
    dys = _mm(dz, w["w_glu"], "nt", pfx + "_dys", epi=lambda acc, dy, zv, ysv: (acc + dy * _sigmoid(zv)) * _dgelu(ysv),
              extras=[(dy2, "mn"), (z, "mn"), (ys, "mn")])
    dd, = _ew(lambda a, b: jnp.sum(a * b, axis=0, keepdims=True), pfx + "_dd", [dys, u], sums=[MIX_HALF])
    xcat = x.reshape(2 * S5_N, S)
    dccat = _mm(xcat, dys, "nn", pfx + "_dc")
    dx = _mm(ccat, dys, "nt", pfx + "_dx").reshape(2, S5_N, S)
    af_re, af_im = res["a"]
    apow, ptab = _s5_scan_tables(af_re, -af_im, True)
    lam = _s5_scan(dx, apow, ptab, pfx + "_scan_bwd", True)
    lcat = lam.reshape(2 * S5_N, S)
    dbcat = _mm(lcat, u, "nn", pfx + "_db")
    du = _mm(lcat, bcat, "tn", pfx + "_du", epi=lambda acc, dyv, dr: acc + dyv * dr, extras=[(dys, "mn"), (res["d_row"], "n")])
    da = _s5_da(lam, x, pfx + "_da")
    G = S5_GROUPS
    d_abar_re, d_abar_im = da[0].reshape(G, S5_STATE), da[1].reshape(G, S5_STATE)
    d_bb_re, d_bb_im = _block_diag_take(dbcat[:S5_N], G), _block_diag_take(dbcat[S5_N:], G)
    _, vjp = jax.vjp(_s5_discretise, w["lam_re"], w["lam_im"], w["log_dt"], w["b_re"], w["b_im"])
    g_lam_re, g_lam_im, g_log_dt, g_b_re, g_b_im = vjp((d_abar_re, d_abar_im, d_bb_re, d_bb_im))
    g_c_re = jnp.swapaxes(_block_diag_take(dccat[:S5_N], G), 1, 2)
    g_c_im = -jnp.swapaxes(_block_diag_take(dccat[S5_N:], G), 1, 2)
    grads = dict(lam_re=g_lam_re, lam_im=g_lam_im, log_dt=g_log_dt, b_re=g_b_re, b_im=g_b_im, c_re=g_c_re, c_im=g_c_im,
                 d=dd.reshape(G, S5_GROUP_WIDTH), w_glu=dw_glu, b_glu=db_glu.reshape(-1))
    return du, grads


SGU_TS = 512
N_PAIRS = MIX_HALF // LANES


def _half_masks(rows):
    lane = lax.broadcasted_iota(jnp.int32, (rows, LANES), 1)
    left = (lane < HEAD_DIM).astype(F32)
    return left, 1.0 - left


def _sgu_norm(zv, gain, bias):
    v = _gelu(zv)
    mu = jnp.mean(v, axis=-1, keepdims=True)
    vc = v - mu
    rstd = lax.rsqrt(jnp.mean(vc * vc, axis=-1, keepdims=True) + EPS)
    vhat = vc * rstd
    return vhat, rstd, vhat * gain + bias


def _sgu_tables(w_s, b_s):
    mask = jnp.tril(jnp.ones((SGU_CHUNK, SGU_CHUNK), dtype=bool))
    wm = jnp.where(mask[None], w_s, 0.0).astype(BF16)
    bias_tab = jnp.repeat(b_s.T, MIX_HALF // SGU_GROUPS, axis=1)
    return wm, bias_tab


def _sgu_fwd(proj, ln_gain, ln_bias, wm, bias_tab, name):
    S = proj.shape[0]
    nch = SGU_TS // SGU_CHUNK

    def body(zu_ref, zv_ref, g_ref, b_ref, w_ref, bt_ref, o_ref):
        left, right = _half_masks(SGU_CHUNK)
        _, _, vn = _sgu_norm(zv_ref[...], g_ref[...], b_ref[...])
        for ch in range(nch):
            rows = pl.ds(ch * SGU_CHUNK, SGU_CHUNK)
            for p in range(N_PAIRS):
                cols = pl.ds(p * LANES, LANES)
                vp = vn[ch * SGU_CHUNK:(ch + 1) * SGU_CHUNK, p * LANES:(p + 1) * LANES]
                mixed = (jnp.dot(w_ref[2 * p], (vp * left).astype(BF16), preferred_element_type=F32)
                         + jnp.dot(w_ref[2 * p + 1], (vp * right).astype(BF16), preferred_element_type=F32) + bt_ref[:, cols])
                o_ref[rows, cols] = _gelu(zu_ref[rows, cols]) * mixed

    vec = _vec_spec(MIX_HALF)
    return pl.pallas_call(
        body, name=name, grid=(S // SGU_TS,),
        in_specs=[pl.BlockSpec((SGU_TS, MIX_HALF), lambda i: (i, 1)), pl.BlockSpec((SGU_TS, MIX_HALF), lambda i: (i, 2)), vec, vec,
                  pl.BlockSpec((SGU_GROUPS, SGU_CHUNK, SGU_CHUNK), lambda i: (0, 0, 0)), pl.BlockSpec((SGU_CHUNK, MIX_HALF), lambda i: (0, 0))],
        out_specs=_row_spec(MIX_HALF, SGU_TS), out_shape=jax.ShapeDtypeStruct((S, MIX_HALF), F32),
        compiler_params=_cparams("parallel"))(proj, proj, ln_gain, ln_bias, wm, bias_tab)


def _sgu_bwd(dout, proj, ln_gain, ln_bias, wm, bias_tab, name):
    S = proj.shape[0]
    nch = SGU_TS // SGU_CHUNK
    nt_dims = (((1,), (1,)), ((), ()))
    tn_dims = (((0,), (0,)), ((), ()))

    def body(do_ref, zu_ref, zv_ref, g_ref, b_ref, w_ref, bt_ref, dzu_ref, dzv_ref, dw_ref, dbt_ref, dg_ref, db_ref, dvn_ref):
        first = pl.program_id(0) == 0

        @pl.when(first)
        def _():
            dw_ref[...] = jnp.zeros_like(dw_ref)
            dbt_ref[...] = jnp.zeros_like(dbt_ref)
            dg_ref[...] = jnp.zeros_like(dg_ref)
            db_ref[...] = jnp.zeros_like(db_ref)

        left, right = _half_masks(SGU_CHUNK)
        zv = zv_ref[...]
        vhat, rstd, vn = _sgu_norm(zv, g_ref[...], b_ref[...])
        for ch in range(nch):
            rows = pl.ds(ch * SGU_CHUNK, SGU_CHUNK)
            for p in range(N_PAIRS):
                cols = pl.ds(p * LANES, LANES)
                vp = vn[ch * SGU_CHUNK:(ch + 1) * SGU_CHUNK, p * LANES:(p + 1) * LANES]
                vl, vr = (vp * left).astype(BF16), (vp * right).astype(BF16)
                mixed = (jnp.dot(w_ref[2 * p], vl, preferred_element_type=F32)
                         + jnp.dot(w_ref[2 * p + 1], vr, preferred_element_type=F32) + bt_ref[:, cols])
                zu = zu_ref[rows, cols]
                do = do_ref[rows, cols]
                dzu_ref[rows, cols] = do * mixed * _dgelu(zu)
                dmix = do * _gelu(zu)
                dbt_ref[:, cols] += dmix
                dl, dr = (dmix * left).astype(BF16), (dmix * right).astype(BF16)
                dw_ref[2 * p] += lax.dot_general(dl, vl, nt_dims, preferred_element_type=F32)
                dw_ref[2 * p + 1] += lax.dot_general(dr, vr, nt_dims, preferred_element_type=F32)
                dvn_ref[rows, cols] = (lax.dot_general(w_ref[2 * p], dl, tn_dims, preferred_element_type=F32)
                                       + lax.dot_general(w_ref[2 * p + 1], dr, tn_dims, preferred_element_type=F32))
        dvn = dvn_ref[...]
        dg_ref[...] += jnp.sum(dvn * vhat, axis=0, keepdims=True)
        db_ref[...] += jnp.sum(dvn, axis=0, keepdims=True)
        dvh = dvn * g_ref[...]
        dv = rstd * (dvh - jnp.mean(dvh, axis=-1, keepdims=True) - vhat * jnp.mean(dvh * vhat, axis=-1, keepdims=True))
        dzv_ref[...] = dv * _dgelu(zv)

    vec = _vec_spec(MIX_HALF)
    row = _row_spec(MIX_HALF, SGU_TS)
    wspec = pl.BlockSpec((SGU_GROUPS, SGU_CHUNK, SGU_CHUNK), lambda i: (0, 0, 0))
    tspec = pl.BlockSpec((SGU_CHUNK, MIX_HALF), lambda i: (0, 0))
    full = jax.ShapeDtypeStruct((S, MIX_HALF), F32)
    v = jax.ShapeDtypeStruct((1, MIX_HALF), F32)
    return pl.pallas_call(
        body, name=name, grid=(S // SGU_TS,),
        in_specs=[row, pl.BlockSpec((SGU_TS, MIX_HALF), lambda i: (i, 1)), pl.BlockSpec((SGU_TS, MIX_HALF), lambda i: (i, 2)), vec, vec,
                  wspec, tspec],
        out_specs=[row, row, wspec, tspec, vec, vec],
        out_shape=[full, full, jax.ShapeDtypeStruct((SGU_GROUPS, SGU_CHUNK, SGU_CHUNK), F32),
                   jax.ShapeDtypeStruct((SGU_CHUNK, MIX_HALF), F32), v, v],
        scratch_shapes=[pltpu.VMEM((SGU_TS, MIX_HALF), F32)],
        compiler_params=_cparams("arbitrary"))(dout, proj, proj, ln_gain, ln_bias, wm, bias_tab)


def _sgu_grads(dw, dbias_tab):
    mask = jnp.tril(jnp.ones((SGU_CHUNK, SGU_CHUNK), dtype=bool))
    g_w = jnp.where(mask[None], dw, 0.0)
    g_b = dbias_tab.reshape(SGU_CHUNK, SGU_GROUPS, MIX_HALF // SGU_GROUPS).sum(axis=-1).T
    return g_w, g_b


def _head_avg_matrix(w):
    idx = np.arange(w) // HEAD_DIM
    return jnp.asarray((idx[:, None] == idx[None, :]).astype(np.float32) / HEAD_DIM, dtype=BF16)


def _head_mean(t, bavg):
    hi = t.astype(BF16)
    lo = (t - hi.astype(F32)).astype(BF16)
    return jnp.dot(hi, bavg, preferred_element_type=F32) + jnp.dot(lo, bavg, preferred_element_type=F32)


def _head_rms(t, bavg):
    r = lax.rsqrt(_head_mean(t * t, bavg) + EPS)
    return t * r, r


def _head_rms_bwd(dn, n, r, bavg):
    return r * (dn - n * _head_mean(dn * n, bavg))


GLA_TS = 512
C = GLA_CHUNK
NT_DIMS = (((1,), (1,)), ((), ()))
TN_DIMS = (((0,), (0,)), ((), ()))
HI = lax.Precision.HIGHEST


def _bdot(a, b, dims=(((1,), (0,)), ((), ()))):
    return lax.dot_general(a.astype(BF16), b.astype(BF16), dims, preferred_element_type=F32)


def _gla_chunk_terms(q, k, z):
    row = lax.broadcasted_iota(jnp.int32, (C, C), 0)
    col = lax.broadcasted_iota(jnp.int32, (C, C), 1)
    lc = _log_sigmoid(z) * (1.0 / GLA_TAU)
    b = lax.dot_general((row >= col).astype(F32), lc, (((1,), (0,)), ((), ())), precision=HI, preferred_element_type=F32)
    b_last = jnp.sum(lc, axis=0, keepdims=True)
    b_mid = b[C // 2:C // 2 + 1, :]
    scale = HEAD_DIM ** -0.5
    e_b, e_q, e_k, e_l = jnp.exp(b), jnp.exp(b - b_mid), jnp.exp(b_mid - b), jnp.exp(b_last - b)
    qs = q * (scale * e_b)
    qe = q * (scale * e_q)
    ke = k * e_k
    kl = k * e_l
    return dict(e_b=e_b, e_q=e_q, e_k=e_k, e_l=e_l, qs=qs, qe=qe, ke=ke, kl=kl, dec=jnp.exp(b_last), causal=row >= col, scale=scale)


def _pair_block_diag():
    r = lax.broadcasted_iota(jnp.int32, (LANES, LANES), 0) // HEAD_DIM
    c = lax.broadcasted_iota(jnp.int32, (LANES, LANES), 1) // HEAD_DIM
    return (r == c).astype(F32)


def _gla_fwd(proj, z, name):
    S = proj.shape[0]
    nch = GLA_TS // C

    def body(q_ref, k_ref, v_ref, z_ref, o_ref, st_ref, state_ref):
        @pl.when(pl.program_id(1) == 0)
        def _():
            state_ref[...] = jnp.zeros_like(state_ref)

        left, right = _half_masks(C)
        bd = _pair_block_diag()
        for ch in range(nch):
            rows = pl.ds(ch * C, C)
            q, k, v = q_ref[rows, :], k_ref[rows, :], v_ref[rows, :]
            t = _gla_chunk_terms(q, k, z_ref[rows, :])
            st = state_ref[...]
            st_ref[ch, 0] = st
            o = _bdot(t["qs"], st, NT_DIMS)
            for m in (left, right):
                a = jnp.where(t["causal"], _bdot(t["qe"] * m, t["ke"], NT_DIMS), 0.0)
                o = o + m * _bdot(a, v)
            o_ref[rows, :] = o
            state_ref[...] = st * t["dec"] + bd * _bdot(v, t["kl"], TN_DIMS)

    def col(cb):
        return pl.BlockSpec((GLA_TS, LANES), lambda p, i: (i, cb * N_PAIRS + p))

    return pl.pallas_call(
        body, name=name, grid=(N_PAIRS, S // GLA_TS),
        in_specs=[col(0), col(1), col(2), pl.BlockSpec((GLA_TS, LANES), lambda p, i: (i, p))],
        out_specs=[pl.BlockSpec((GLA_TS, LANES), lambda p, i: (i, p)), pl.BlockSpec((nch, 1, LANES, LANES), lambda p, i: (i, p, 0, 0))],
        out_shape=[jax.ShapeDtypeStruct((S, MIX_HALF), F32), jax.ShapeDtypeStruct((S // C, N_PAIRS, LANES, LANES), F32)],
        scratch_shapes=[pltpu.VMEM((LANES, LANES), F32)], compiler_params=_cparams("parallel", "arbitrary"))(proj, proj, proj, z)


def _gla_bwd(do, proj, z, states, name):
    S = proj.shape[0]
    nch = GLA_TS // C
    nblk = S // GLA_TS

    def body(do_ref, q_ref, k_ref, v_ref, z_ref, st_ref, dq_ref, dk_ref, dv_ref, dlc_ref, dstate_ref):
        @pl.when(pl.program_id(1) == 0)
        def _():
            dstate_ref[...] = jnp.zeros_like(dstate_ref)

        left, right = _half_masks(C)
        bd = _pair_block_diag()
        rowi = lax.broadcasted_iota(jnp.int32, (C, LANES), 0)
        for ch in range(nch - 1, -1, -1):
            rows = pl.ds(ch * C, C)
            q, k, v, dov = q_ref[rows, :], k_ref[rows, :], v_ref[rows, :], do_ref[rows, :]
            t = _gla_chunk_terms(q, k, z_ref[rows, :])
            st = st_ref[ch, 0]
            dst_next = dstate_ref[...]
            g = bd * dst_next
            dqs = _bdot(dov, st)
            dv = _bdot(t["kl"], g, NT_DIMS)
            dkl = _bdot(v, g)
            dqe = jnp.zeros((C, LANES), F32)
            dke = jnp.zeros((C, LANES), F32)
            for m in (left, right):
                a = jnp.where(t["causal"], _bdot(t["qe"] * m, t["ke"], NT_DIMS), 0.0)
                da = jnp.where(t["causal"], _bdot(dov * m, v, NT_DIMS), 0.0)
                dv = dv + m * _bdot(a, dov, TN_DIMS)
                dqe = dqe + m * _bdot(da, t["ke"])
                dke = dke + m * _bdot(da, t["qe"], TN_DIMS)
            dstate_ref[...] = bd * (dst_next * t["dec"] + _bdot(dov, t["qs"], TN_DIMS))
            db_last = jnp.sum(dst_next * st, axis=0, keepdims=True) * t["dec"] + jnp.sum(dkl * t["kl"], axis=0, keepdims=True)
            db = dqs * t["qs"] + dqe * t["qe"] - dke * t["ke"] - dkl * t["kl"]
            db = db + jnp.where(rowi == C - 1, db_last, 0.0)
            dq_ref[rows, :] = (dqs * t["e_b"] + dqe * t["e_q"]) * t["scale"]
            dk_ref[rows, :] = dke * t["e_k"] + dkl * t["e_l"]
            dv_ref[rows, :] = dv
            row = lax.broadcasted_iota(jnp.int32, (C, C), 0)
            colm = lax.broadcasted_iota(jnp.int32, (C, C), 1)
            dlc_ref[rows, :] = lax.dot_general((colm >= row).astype(F32), db, (((1,), (0,)), ((), ())), precision=HI,
                                               preferred_element_type=F32)

    def col(cb):
        return pl.BlockSpec((GLA_TS, LANES), lambda p, i: (nblk - 1 - i, cb * N_PAIRS + p))

    blk = pl.BlockSpec((GLA_TS, LANES), lambda p, i: (nblk - 1 - i, p))
    full = jax.ShapeDtypeStruct((S, MIX_HALF), F32)
    return pl.pallas_call(
        body, name=name, grid=(N_PAIRS, nblk),
        in_specs=[blk, col(0), col(1), col(2), blk, pl.BlockSpec((nch, 1, LANES, LANES), lambda p, i: (nblk - 1 - i, p, 0, 0))],
        out_specs=[blk, blk, blk, blk], out_shape=[full, full, full, full],
        scratch_shapes=[pltpu.VMEM((LANES, LANES), F32)], compiler_params=_cparams("parallel", "arbitrary"))(do, proj, proj, proj, z, states)


def _gla_block_fwd(proj, w_lr_pad, b_lr, gain, bavg, pfx):
    z = _mm(proj, w_lr_pad, "nn", pfx + "_z", a_cols=(7 * MIX_HALF, MIX_HALF), epi=lambda acc, b: acc + b, extras=[(b_lr, "n")])
    o, states = _gla_fwd(proj, z, pfx + "_core")

    def out(ov, gg, ba, gn):
        n, _ = _head_rms(ov, ba)
        return n * gn * (gg * _sigmoid(gg))

    og, = _ew(out, pfx + "_out", [o, (proj, MIX_HALF, 3)], consts=[bavg, gain], outs=[(MIX_HALF, F32)])
    return og, dict(z=z, o=o, states=states)


def _gla_block_bwd(dog, proj, w_lr_pad, gain, bavg, res, pfx):
    z, o, states = res["z"], res["o"], res["states"]

    def out_bwd(dy, ov, gg, ba, gn):
        n, r = _head_rms(ov, ba)
        sg = _sigmoid(gg)
        silu = gg * sg
        dn = dy * gn * silu
        do = _head_rms_bwd(dn, n, r, ba)
        dgg = dy * n * gn * (sg * (1.0 + gg * (1.0 - sg)))
        return do, dgg, jnp.sum(dy * n * silu, axis=0, keepdims=True)

    do, dgg, dgain = _ew(out_bwd, pfx + "_out_bwd", [dog, o, (proj, MIX_HALF, 3)], consts=[bavg, gain],
                         outs=[(MIX_HALF, F32), (MIX_HALF, F32)], sums=[MIX_HALF])
    dq, dk, dv, dlc = _gla_bwd(do, proj, z, states, pfx + "_core_bwd")

    def decay_bwd(dl, zv):
        dz = dl * (1.0 / GLA_TAU) * (1.0 - _sigmoid(zv))
        return dz, jnp.sum(dz, axis=0, keepdims=True)

    dz, db_lr = _ew(decay_bwd, pfx + "_decay_bwd", [dlc, z], outs=[(MIX_HALF, F32)], sums=[MIX_HALF])
    dw_lr_pad = _mm(proj, dz, "tn", pfx + "_dwlr", a_cols=(7 * MIX_HALF, MIX_HALF))
    dsmall = _mm(dz, w_lr_pad, "nt", pfx + "_dsmall")
    return (dq, dk, dv, dgg, dsmall), dict(w_lr=dw_lr_pad[:GLA_RANK], b_lr=db_lr.reshape(-1), gain=dgain.reshape(-1, HEAD_DIM))


FOX_T = 512
FOX_HEADS = MIX_HALF // HEAD_DIM
NEG = -1e30
CUM_T = 512


def _cum_lanes(x, name, reverse, pre=None):
    R, S = x.shape
    nb = S // CUM_T

    def body(x_ref, o_ref, carry_ref):
        @pl.when(pl.program_id(0) == 0)
        def _():
            carry_ref[...] = jnp.zeros_like(carry_ref)

        xv = x_ref[...]
        if pre is not None:
            xv = pre(xv)
        i = lax.broadcasted_iota(jnp.int32, (CUM_T, CUM_T), 0)
        j = lax.broadcasted_iota(jnp.int32, (CUM_T, CUM_T), 1)
        tri = ((i >= j) if reverse else (i <= j)).astype(F32)
        c = lax.dot_general(xv, tri, (((1,), (0,)), ((), ())), precision=HI, preferred_element_type=F32)
        carry = carry_ref[...]
        o_ref[...] = c + carry[:, 0:1]
        carry_ref[...] = carry + jnp.sum(xv, axis=1, keepdims=True)

    spec = pl.BlockSpec((R, CUM_T), (lambda i: (0, nb - 1 - i)) if reverse else (lambda i: (0, i)))
    return pl.pallas_call(body, name=name, grid=(nb,), in_specs=[spec], out_specs=spec, out_shape=jax.ShapeDtypeStruct((R, S), F32),
                          scratch_shapes=[pltpu.VMEM((R, LANES), F32)], compiler_params=_cparams("arbitrary"))(x)


def _fox_scores(q, k, cqb, ck_ref, h, m, diag):
    cq = cqb[:, h * HEAD_DIM:h * HEAD_DIM + 1]
    ck = ck_ref[0, h:h + 1, :]
    s = lax.dot_general(q * m.astype(q.dtype), k, NT_DIMS, preferred_element_type=F32) + (cq - ck)
    if not diag:
        return s
    row = lax.broadcasted_iota(jnp.int32, (FOX_T, FOX_T), 0)
    col = lax.broadcasted_iota(jnp.int32, (FOX_T, FOX_T), 1)
    return jnp.where(row < col, NEG, s)


def _on_causal_blocks(q_blk, k_blk, step):
    @pl.when(k_blk < q_blk)
    def _():
        step(False)

    @pl.when(k_blk == q_blk)
    def _():
        step(True)


def _fox_fwd(qn, kn, proj, cum_b, cum_tp, name):
    S = qn.shape[0]
    nq = S // FOX_T

    def body(q_ref, k_ref, v_ref, cq_ref, ck_ref, o_ref, lse_ref, m_scr, l_scr, acc_scr):
        qi, ki = pl.program_id(1), pl.program_id(2)

        @pl.when(ki == 0)
        def _():
            m_scr[...] = jnp.full_like(m_scr, NEG)
            l_scr[...] = jnp.zeros_like(l_scr)
            acc_scr[...] = jnp.zeros_like(acc_scr)

        left, right = _half_masks(FOX_T)

        def step(diag):
            q, k, v = q_ref[...], k_ref[...], v_ref[...].astype(BF16)
            cqb = cq_ref[...]
            for h, m in enumerate((left, right)):
                s = _fox_scores(q, k, cqb, ck_ref, h, m, diag)
                m_prev = m_scr[h]
                m_new = jnp.maximum(m_prev, jnp.max(s, axis=1, keepdims=True))
                alpha = jnp.exp(m_prev - m_new)
                p = jnp.exp(s - m_new)
                l_scr[h] = alpha * l_scr[h] + jnp.sum(p, axis=1, keepdims=True)
                acc_scr[h] = alpha * acc_scr[h] + jnp.dot(p.astype(BF16), v, preferred_element_type=F32)
                m_scr[h] = m_new

        _on_causal_blocks(qi, ki, step)

        @pl.when(ki == qi)
        def _():
            o_ref[...] = left * (acc_scr[0] / l_scr[0]) + right * (acc_scr[1] / l_scr[1])
            lse_ref[...] = left * (m_scr[0] + jnp.log(l_scr[0])) + right * (m_scr[1] + jnp.log(l_scr[1]))

    qspec = pl.BlockSpec((FOX_T, LANES), lambda p, qi, ki: (qi, p))
    kspec = pl.BlockSpec((FOX_T, LANES), lambda p, qi, ki: (jnp.minimum(ki, qi), p))
    vspec = pl.BlockSpec((FOX_T, LANES), lambda p, qi, ki: (jnp.minimum(ki, qi), 6 * N_PAIRS + p))
    ckspec = pl.BlockSpec((1, 8, FOX_T), lambda p, qi, ki: (p, 0, jnp.minimum(ki, qi)))
    full = jax.ShapeDtypeStruct((S, MIX_HALF), F32)
    return pl.pallas_call(
        body, name=name, grid=(N_PAIRS, nq, nq), in_specs=[qspec, kspec, vspec, qspec, ckspec], out_specs=[qspec, qspec],
        out_shape=[full, full],
        scratch_shapes=[pltpu.VMEM((2, FOX_T, 1), F32), pltpu.VMEM((2, FOX_T, 1), F32), pltpu.VMEM((2, FOX_T, LANES), F32)],
        compiler_params=_cparams("parallel", "parallel", "arbitrary"))(qn, kn, proj, cum_b, cum_tp)


def _fox_bwd_dq(do, qn, kn, proj, cum_b, cum_tp, lse_b, delta_b, name):
    S = qn.shape[0]
    nq = S // FOX_T
    scale = HEAD_DIM ** -0.5

    def body(do_ref, q_ref, k_ref, v_ref, cq_ref, ck_ref, lse_ref, dl_ref, dq_ref, dcq_ref, acc_scr, rs_scr):
        qi, ki = pl.program_id(1), pl.program_id(2)

        @pl.when(ki == 0)
        def _():
            acc_scr[...] = jnp.zeros_like(acc_scr)
            rs_scr[...] = jnp.zeros_like(rs_scr)

        left, right = _half_masks(FOX_T)

        def step(diag):
            q, k, v, dov = q_ref[...], k_ref[...], v_ref[...].astype(BF16), do_ref[...]
            cqb, lseb, dlb = cq_ref[...], lse_ref[...], dl_ref[...]
            acc = acc_scr[...]
            for h, m in enumerate((left, right)):
                s = _fox_scores(q, k, cqb, ck_ref, h, m, diag)
                p = jnp.exp(s - lseb[:, h * HEAD_DIM:h * HEAD_DIM + 1])
                dp = lax.dot_general((dov * m).astype(BF16), v, NT_DIMS, preferred_element_type=F32)
                ds = p * (dp - dlb[:, h * HEAD_DIM:h * HEAD_DIM + 1])
                acc = acc + m * jnp.dot(ds.astype(BF16), k, preferred_element_type=F32)
                rs_scr[h] = rs_scr[h] + jnp.sum(ds, axis=1, keepdims=True)
            acc_scr[...] = acc

        _on_causal_blocks(qi, ki, step)

        @pl.when(ki == qi)
        def _():
            dq_ref[...] = acc_scr[...] * scale
            dcq_ref[...] = left * rs_scr[0] + right * rs_scr[1]

    qspec = pl.BlockSpec((FOX_T, LANES), lambda p, qi, ki: (qi, p))
    kspec = pl.BlockSpec((FOX_T, LANES), lambda p, qi, ki: (jnp.minimum(ki, qi), p))
    vspec = pl.BlockSpec((FOX_T, LANES), lambda p, qi, ki: (jnp.minimum(ki, qi), 6 * N_PAIRS + p))
    ckspec = pl.BlockSpec((1, 8, FOX_T), lambda p, qi, ki: (p, 0, jnp.minimum(ki, qi)))
    return pl.pallas_call(
        body, name=name, grid=(N_PAIRS, nq, nq), in_specs=[qspec, qspec, kspec, vspec, qspec, ckspec, qspec, qspec],
        out_specs=[qspec, qspec], out_shape=[jax.ShapeDtypeStruct((S, MIX_HALF), F32)] * 2,
        scratch_shapes=[pltpu.VMEM((FOX_T, LANES), F32), pltpu.VMEM((2, FOX_T, 1), F32)],
        compiler_params=_cparams("parallel", "parallel", "arbitrary"))(do, qn, kn, proj, cum_b, cum_tp, lse_b, delta_b)


def _fox_bwd_dkv(do, qn, kn, proj, cum_b, cum_tp, lse_b, delta_b, name):
    S = qn.shape[0]
    nq = S // FOX_T
    scale = HEAD_DIM ** -0.5

    def body(do_ref, q_ref, k_ref, v_ref, cq_ref, ck_ref, lse_ref, dl_ref, dk_ref, dv_ref, dck_ref, dk_scr, dv_scr, dck_scr):
        ki, qi = pl.program_id(1), pl.program_id(2)

        @pl.when(qi == 0)
        def _():
            dk_scr[...] = jnp.zeros_like(dk_scr)
            dv_scr[...] = jnp.zeros_like(dv_scr)
            dck_scr[...] = jnp.zeros_like(dck_scr)

        left, right = _half_masks(FOX_T)

        def step(diag):
            q, k, v, dov = q_ref[...], k_ref[...], v_ref[...].astype(BF16), do_ref[...]
            cqb, lseb, dlb = cq_ref[...], lse_ref[...], dl_ref[...]
            dob = dov.astype(BF16)
            dk, dv = dk_scr[...], dv_scr[...]
            for h, m in enumerate((left, right)):
                s = _fox_scores(q, k, cqb, ck_ref, h, m, diag)
                p = jnp.exp(s - lseb[:, h * HEAD_DIM:h * HEAD_DIM + 1])
                dp = lax.dot_general((dov * m).astype(BF16), v, NT_DIMS, preferred_element_type=F32)
                ds = p * (dp - dlb[:, h * HEAD_DIM:h * HEAD_DIM + 1])
                dv = dv + m * lax.dot_general(p.astype(BF16), dob, TN_DIMS, preferred_element_type=F32)
                dk = dk + m * lax.dot_general(ds.astype(BF16), q, TN_DIMS, preferred_element_type=F32)
                dck_scr[h:h + 1, :] = dck_scr[h:h + 1, :] - jnp.sum(ds, axis=0, keepdims=True)
            dk_scr[...] = dk
            dv_scr[...] = dv

        _on_causal_blocks(qi, ki, step)

        @pl.when(qi == nq - 1)
        def _():
            dk_ref[...] = dk_scr[...]
            dv_ref[...] = dv_scr[...]
            dck_ref[0] = dck_scr[...]

    qspec = pl.BlockSpec((FOX_T, LANES), lambda p, ki, qi: (jnp.maximum(qi, ki), p))
    kspec = pl.BlockSpec((FOX_T, LANES), lambda p, ki, qi: (ki, p))
    vspec = pl.BlockSpec((FOX_T, LANES), lambda p, ki, qi: (ki, 6 * N_PAIRS + p))
    ckspec = pl.BlockSpec((1, 8, FOX_T), lambda p, ki, qi: (p, 0, ki))
    full = jax.ShapeDtypeStruct((S, MIX_HALF), F32)
    return pl.pallas_call(
        body, name=name, grid=(N_PAIRS, nq, nq), in_specs=[qspec, qspec, kspec, vspec, qspec, ckspec, qspec, qspec],
        out_specs=[kspec, kspec, ckspec], out_shape=[full, full, jax.ShapeDtypeStruct((N_PAIRS, 8, S), F32)],
        scratch_shapes=[pltpu.VMEM((FOX_T, LANES), F32), pltpu.VMEM((FOX_T, LANES), F32), pltpu.VMEM((8, FOX_T), F32)],
        compiler_params=_cparams("parallel", "parallel", "arbitrary"))(do, qn, kn, proj, cum_b, cum_tp, lse_b, delta_b)


def _ff_bwd(rc, f_t, name):
    def body(rc_ref, f_ref, d_ref, s_ref):
        d = rc_ref[...] * (1.0 - _sigmoid(f_ref[...]))
        d_ref[...] = d
        s_ref[...] = jnp.sum(d, axis=1, keepdims=True)

    return pl.pallas_call(body, name=name, out_shape=[jax.ShapeDtypeStruct(rc.shape, F32), jax.ShapeDtypeStruct((rc.shape[0], 1), F32)])(rc, f_t)


def _fox_block_fwd(proj, b_f, q_gain, k_gain, bavg, pfx):
    S = proj.shape[0]

    def prep(qv, kv, ba, qg, kg):
        return _head_rms(qv, ba)[0] * qg * (HEAD_DIM ** -0.5), _head_rms(kv, ba)[0] * kg

    qn, kn = _ew(prep, pfx + "_prep", [(proj, MIX_HALF, 4), (proj, MIX_HALF, 5)], consts=[bavg, q_gain, k_gain],
                 outs=[(MIX_HALF, BF16), (MIX_HALF, BF16)])
    f0 = 7 * MIX_HALF + GLA_RANK
    f_t = proj[:, f0:f0 + FOX_HEADS].T + b_f.reshape(FOX_HEADS, 1)
    cum = _cum_lanes(f_t, pfx + "_cum", False, pre=_log_sigmoid)
    cum_b = jnp.repeat(cum.T, HEAD_DIM, axis=1)
    cum_tp = jnp.pad(cum.reshape(N_PAIRS, 2, S), ((0, 0), (0, 6), (0, 0)))
    o, lse_b = _fox_fwd(qn, kn, proj, cum_b, cum_tp, pfx + "_attn")
    return o, dict(qn=qn, kn=kn, f_t=f_t, cum_b=cum_b, cum_tp=cum_tp, o=o, lse_b=lse_b)


def _fox_block_bwd(do, proj, q_gain, k_gain, bavg, res, pfx):
    qn, kn, o = res["qn"], res["kn"], res["o"]
    S = proj.shape[0]
    delta_b, = _ew(lambda a, b, ba: _head_mean(a * b, ba) * float(HEAD_DIM), pfx + "_delta", [do, o], consts=[bavg], outs=[(MIX_HALF, F32)])
    args = (do, qn, kn, proj, res["cum_b"], res["cum_tp"], res["lse_b"], delta_b)
    dqn, dcq_b = _fox_bwd_dq(*args, pfx + "_dq")
    dkn, dv, dck = _fox_bwd_dkv(*args, pfx + "_dkv")

    def prep_bwd(dq, dk, qv, kv, ba, qg, kg):
        nq, rq = _head_rms(qv, ba)
        nk, rk = _head_rms(kv, ba)
        return (_head_rms_bwd(dq * qg, nq, rq, ba), _head_rms_bwd(dk * kg, nk, rk, ba),
                jnp.sum(dq * nq, axis=0, keepdims=True), jnp.sum(dk * nk, axis=0, keepdims=True))

    dfq, dfk, dqg, dkg = _ew(prep_bwd, pfx + "_prep_bwd", [dqn, dkn, (proj, MIX_HALF, 4), (proj, MIX_HALF, 5)],
                             consts=[bavg, q_gain, k_gain], outs=[(MIX_HALF, F32), (MIX_HALF, F32)], sums=[MIX_HALF, MIX_HALF])
    dcum = dck[:, :2, :].reshape(FOX_HEADS, S) + dcq_b[:, ::HEAD_DIM].T
    rc = _cum_lanes(dcum, pfx + "_rcum", True)
    dff_t, db_f = _ff_bwd(rc, res["f_t"], pfx + "_ff_bwd")
    grads = dict(b_f=db_f.reshape(-1), q_gain=dqg.reshape(-1, HEAD_DIM), k_gain=dkg.reshape(-1, HEAD_DIM))
    return (dfq, dfk, dv, dff_t.T), grads


WEIGHTS = ['ada_w', 'ada_b', 'even_w_in', 'even_w_out', 'gla_w_lr', 'gla_b_lr', 'gla_gain', 'fox_b_f', 'fox_q_gain', 'fox_k_gain',
           'odd_w_in', 'odd_w_out', 's5_lam_re', 's5_lam_im', 's5_log_dt', 's5_b_re', 's5_b_im', 's5_c_re', 's5_c_im', 's5_d',
           's5_w_glu', 's5_b_glu', 'sgu_ln_gain', 'sgu_ln_bias', 'sgu_w_s', 'sgu_b_s', 'mlp_w1', 'mlp_w2']
ARGS = ['x', 'c'] + WEIGHTS + ['loss_target'] + ['m_' + w for w in WEIGHTS] + ['v_' + w for w in WEIGHTS]

EVEN_COLS = 3608
EVEN_PAD = 8 * MIX_HALF
MOD = 6 * D_MODEL
MOD_SHARD = MOD // N_CHIPS

SHARDED = [("even_w_in", (1, 1024, 902), 2), ("even_w_out", (1, 256, 1024), 1), ("odd_w_in", (1, 1024, 384), 2),
           ("odd_w_out", (1, 256, 1024), 1), ("mlp_w1", (2, 1024, 1024), 2), ("mlp_w2", (2, 1024, 1024), 1),
           ("gla_w_lr", (1, 16, 128), 2), ("s5_w_glu", (1, 128, 512), 1), ("s5_b_glu", (1, 128), 1),
           ("sgu_ln_gain", (1, 128), 1), ("sgu_ln_bias", (1, 128), 1)]
PACK_COLS = 512
PACK_ROWS = 12288
REPLICATED = [("gla_b_lr", (1, 512)), ("gla_gain", (1, 8, 64)), ("fox_b_f", (1, 8)), ("fox_q_gain", (1, 8, 64)),
              ("fox_k_gain", (1, 8, 64)), ("s5_lam_re", (1, 32, 64)), ("s5_lam_im", (1, 32, 64)), ("s5_log_dt", (1, 32)),
              ("s5_b_re", (1, 32, 64, 16)), ("s5_b_im", (1, 32, 64, 16)), ("s5_c_re", (1, 32, 16, 64)), ("s5_c_im", (1, 32, 16, 64)),
              ("s5_d", (1, 32, 16)), ("sgu_w_s", (1, 8, 128, 128)), ("sgu_b_s", (1, 8, 128))]
SMALL_ROWS = 768
BIG_ADAM = {"ada_w": (2048, 1536), "even_w_in": (1024, 902), "even_w_out": (256, 1024), "odd_w_in": (1024, 384),
            "odd_w_out": (256, 1024), "mlp_w1": (2048, 1024), "mlp_w2": (2048, 1024), "s5_w_glu": (128, 512)}


def _pack_rows(pieces, rows):
    flat = jnp.concatenate([p.reshape(-1) for p in pieces])
    return jnp.pad(flat, (0, rows * PACK_COLS - flat.shape[0])).reshape(rows, PACK_COLS)


def _unpack(flat, specs):
    out, off = {}, 0
    for name, shape in specs:
        n = math.prod(shape)
        out[name] = flat[off:off + n].reshape(shape)
        off += n
    return out


def _shards_to_full(flat4):
    out, off = {}, 0
    for name, shape, axis in SHARDED:
        n = math.prod(shape)
        seg = flat4[:, off:off + n].reshape((N_CHIPS,) + shape)
        out[name] = jnp.concatenate([seg[k] for k in range(N_CHIPS)], axis=axis)
        off += n
    return out


def _full_to_shards(full):
    cols = []
    for name, shape, axis in SHARDED:
        parts = jnp.split(full[name], N_CHIPS, axis=axis)
        cols.append(jnp.stack([p.reshape(-1) for p in parts]))
    flat = jnp.concatenate(cols, axis=1)
    return jnp.pad(flat, ((0, 0), (0, PACK_ROWS * PACK_COLS - flat.shape[1]))).reshape(N_CHIPS, PACK_ROWS, PACK_COLS)


def _relu2(t):
    r = jnp.maximum(t, 0.0)
    return r * r


def _silu(t):
    return t * _sigmoid(t)


def _pack_even(w):
    return jnp.concatenate([w[:, :2048], w[:, 2064:3600], w[:, 2048:2064], w[:, 3600:3608],
                            jnp.zeros((w.shape[0], EVEN_PAD - EVEN_COLS), w.dtype)], axis=1)


def _unpack_even(wp):
    return jnp.concatenate([wp[:, :2048], wp[:, 3584:3600], wp[:, 2048:3584], wp[:, 3600:3608]], axis=1)


def _mlp_fwd(h, w1, w2, pfx):
    pre = _mm(h, w1, "nn", pfx + "_up")
    return pre, _mm(pre, w2, "nn", pfx + "_down", a_pro=_relu2)


def _mlp_bwd(dm, h, pre, w1, w2, pfx):
    dpre = _mm(dm, w2, "nt", pfx + "_dpre", epi=lambda acc, p: acc * (2.0 * jnp.maximum(p, 0.0)), extras=[(pre, "mn")])
    dw2 = _mm(pre, dm, "tn", pfx + "_dw2", a_pro=_relu2)
    dw1 = _mm(h, dpre, "tn", pfx + "_dw1")
    dh = _mm(dpre, w1, "nt", pfx + "_dh")
    return dh, dw1, dw2


def _step(args):
    a = dict(zip(ARGS, args, strict=True))
    x0 = a["x"][0]
    target = a["loss_target"][0]
    mx, my, mc = lax.axis_index("x"), lax.axis_index("y"), lax.axis_index("c")
    chip = 2 * mx + my
    dev = 2 * chip + mc
    bavg = _head_avg_matrix(MIX_HALF)

    c_all = _gather8(jnp.pad(a["c"], ((0, 7), (0, 0))), "c_gather")[:, :, 0, :].reshape(2 * N_CHIPS, D_MODEL)
    ada_b_shard = lax.dynamic_slice_in_dim(a["ada_b"], chip * MOD_SHARD, MOD_SHARD, axis=1)
    mod_sh = [_mm(c_all, a["ada_w"][l], "nn", f"mod{l}", a_pro=_silu, epi=lambda acc, b: acc + b, extras=[(ada_b_shard[l:l + 1], "n")])
              for l in range(2)]
    small3 = jnp.zeros((8, MOD_SHARD), F32)
    for r, n in enumerate(("s5_b_glu", "sgu_ln_gain", "sgu_ln_bias")):
        small3 = small3.at[r, :LANES].set(a[n][0])
    mod_all = _chip_exchange(jnp.concatenate(mod_sh + [small3]), "mod_gather", True)
    mods = []
    for l in range(2):
        full = mod_all[:, 8 * l:8 * l + 8].transpose(1, 0, 2).reshape(8, MOD)
        mods.append(jnp.split(lax.dynamic_slice_in_dim(full, dev, 1, axis=0), 6, axis=1))
    b_glu, ln_gain, ln_bias = (mod_all[:, 16 + r, :LANES].reshape(1, MIX_HALF) for r in range(3))

    shard = _pack_rows([a[n] for n, _, _ in SHARDED], PACK_ROWS).astype(BF16)
    half = lax.dynamic_slice_in_dim(shard, mc * (PACK_ROWS // 2), PACK_ROWS // 2, axis=0)
    halves = _pair_gather(_chip_exchange(half, "w_chips", True), "w_pair")
    w = _shards_to_full(halves.transpose(1, 0, 2, 3).reshape(N_CHIPS, PACK_ROWS * PACK_COLS))
    w_even = _pack_even(w["even_w_in"][0])
    w_lr_pad = jnp.zeros((MIX_HALF, MIX_HALF), BF16).at[:GLA_RANK].set(w["gla_w_lr"][0])
    gla_b_lr = a["gla_b_lr"]
    gla_gain, q_gain, k_gain = (a[n].reshape(1, MIX_HALF) for n in ("gla_gain", "fox_q_gain", "fox_k_gain"))
    s5w = dict(lam_re=a["s5_lam_re"][0], lam_im=a["s5_lam_im"][0], log_dt=a["s5_log_dt"][0], b_re=a["s5_b_re"][0], b_im=a["s5_b_im"][0],
               c_re=a["s5_c_re"][0], c_im=a["s5_c_im"][0], d=a["s5_d"][0], w_glu=w["s5_w_glu"][0], b_glu=b_glu)
    sgu_wm, sgu_bt = _sgu_tables(a["sgu_w_s"][0], a["sgu_b_s"][0])

    sh1, sc1, g1, sh2, sc2, g2 = mods[0]
    _, h1_0 = _res_rms(x0, sc1, sh1, "l0_norm1")
    proj0 = _mm(h1_0, w_even, "nn", "l0_proj")
    og, gla_res = _gla_block_fwd(proj0, w_lr_pad, gla_b_lr, gla_gain, bavg, "gla")
    of, fox_res = _fox_block_fwd(proj0, a["fox_b_f"][0], q_gain, k_gain, bavg, "fox")
    mixed0 = jnp.concatenate([og, of], axis=1)
    y0 = _mm(mixed0, w["even_w_out"][0], "nn", "l0_out")
    x1, h2_0 = _res_rms(x0, sc2, sh2, "l0_norm2", y=y0, g=g1)
    pre0, m0 = _mlp_fwd(h2_0, w["mlp_w1"][0], w["mlp_w2"][0], "l0_mlp")
    sh1b, sc1b, g1b, sh2b, sc2b, g2b = mods[1]
    x2, h1_1 = _res_rms(x1, sc1b, sh1b, "l1_norm1", y=m0, g=g2)
    proj1 = _mm(h1_1, w["odd_w_in"][0], "nn", "l1_proj")
    ys5, s5_res = _s5_block_fwd(proj1[:, :MIX_HALF], s5w, "s5")
    ysgu = _sgu_fwd(proj1, ln_gain, ln_bias, sgu_wm, sgu_bt, "sgu")
    mixed1 = jnp.concatenate([ys5, ysgu], axis=1)
    y1 = _mm(mixed1, w["odd_w_out"][0], "nn", "l1_out")
    x3, h2_1 = _res_rms(x2, sc2b, sh2b, "l1_norm2", y=y1, g=g1b)
    pre1, m1 = _mlp_fwd(h2_1, w["mlp_w1"][1], w["mlp_w2"][1], "l1_mlp")
    loss_b, dx4, dm1, dg2b = _res_loss(x3, m1, g2b, target, "loss")
    loss = lax.psum(loss_b[0, 0], ("x", "y", "c"))

    full = {}
    dh2_1, dw1_1, dw2_1 = _mlp_bwd(dm1, h2_1, pre1, w["mlp_w1"][1], w["mlp_w2"][1], "l1_mlp")
    dx3, dy1, dg1b, dsc2b, dsh2b = _res_rms_bwd(x3, dh2_1, sc2b, dx4, "l1_norm2_bwd", y=y1, g=g1b)
    dmixed1 = _mm(dy1, w["odd_w_out"][0], "nt", "l1_out_dx")
    full["odd_w_out"] = _mm(mixed1, dy1, "tn", "l1_out_dw")[None]
    du, s5g = _s5_block_bwd(dmixed1[:, :MIX_HALF], s5w, s5_res, "s5")
    dzu, dzv, dws, dbt, dlg, dlb = _sgu_bwd(dmixed1[:, MIX_HALF:], proj1, ln_gain, ln_bias, sgu_wm, sgu_bt, "sgu_bwd")
    g_ws, g_bs = _sgu_grads(dws, dbt)
    dproj1 = jnp.concatenate([du, dzu, dzv], axis=1)
    full["odd_w_in"] = _mm(h1_1, dproj1, "tn", "l1_proj_dw")[None]
    dh1_1 = _mm(dproj1, w["odd_w_in"][0], "nt", "l1_proj_dx")
    dx2, dm0, dg2, dsc1b, dsh1b = _res_rms_bwd(x2, dh1_1, sc1b, dx3, "l1_norm1_bwd", y=m0, g=g2)
    dh2_0, dw1_0, dw2_0 = _mlp_bwd(dm0, h2_0, pre0, w["mlp_w1"][0], w["mlp_w2"][0], "l0_mlp")
    full["mlp_w1"] = jnp.stack([dw1_0, dw1_1])
    full["mlp_w2"] = jnp.stack([dw2_0, dw2_1])
    dx1, dy0, dg1, dsc2, dsh2 = _res_rms_bwd(x1, dh2_0, sc2, dx2, "l0_norm2_bwd", y=y0, g=g1)
    dmixed0 = _mm(dy0, w["even_w_out"][0], "nt", "l0_out_dx")
    full["even_w_out"] = _mm(mixed0, dy0, "tn", "l0_out_dw")[None]
    (dgq, dgk, dgv, dgg, dsmall), glag = _gla_block_bwd(dmixed0[:, :MIX_HALF], proj0, w_lr_pad, gla_gain, bavg, gla_res, "gla")
    (dfq, dfk, dfv, dff), foxg = _fox_block_bwd(dmixed0[:, MIX_HALF:], proj0, q_gain, k_gain, bavg, fox_res, "fox")
    dsmall = lax.dynamic_update_slice(dsmall, dff, (0, GLA_RANK))
    dproj0 = jnp.concatenate([dgq, dgk, dgv, dgg, dfq, dfk, dfv, dsmall], axis=1)
    full["even_w_in"] = _unpack_even(_mm(h1_0, dproj0, "tn", "l0_proj_dw"))[None]
    dh1_0 = _mm(dproj0, w_even, "nt", "l0_proj_dx")
    grad_x, dsc1, dsh1 = _res_rms_bwd(x0, dh1_0, sc1, dx1, "l0_norm1_bwd")
    full["gla_w_lr"] = glag["w_lr"][None]
    full["s5_w_glu"] = s5g["w_glu"][None]
    full["s5_b_glu"] = s5g["b_glu"][None]
    full["sgu_ln_gain"] = dlg
    full["sgu_ln_bias"] = dlb

    dmod = jnp.concatenate([dsh1, dsc1, dg1, dsh2, dsc2, dg2, dsh1b, dsc1b, dg1b, dsh2b, dsc2b, dg2b], axis=1)
    dmod_all = _gather8(jnp.pad(dmod, ((0, 7), (0, 0))), "dmod_gather")[:, :, 0, :].reshape(2 * N_CHIPS, 2, MOD)
    grads = {}
    grads["ada_w"] = jnp.stack([
        _mm(c_all, lax.dynamic_slice_in_dim(dmod_all[:, l], chip * MOD_SHARD, MOD_SHARD, axis=1), "tn", f"ada_dw{l}", a_pro=_silu)
        for l in range(2)])
    grads["ada_b"] = _sum_slots(dmod_all.reshape(2 * N_CHIPS, 2 * MOD // PACK_COLS, PACK_COLS), "ada_db").reshape(2, MOD)

    packed = _full_to_shards(full)
    hr = PACK_ROWS // 2
    mine = lax.dynamic_slice_in_dim(packed, mc * hr, hr, axis=1)
    other = lax.dynamic_slice_in_dim(packed, (1 - mc) * hr, hr, axis=1)
    theirs = lax.dynamic_index_in_dim(_pair_gather(other, "g_pair"), 1 - mc, axis=0, keepdims=False)
    pair_sum, = _ew(lambda p, q: p + q, "g_pair_sum", [mine.reshape(N_CHIPS * hr, PACK_COLS), theirs.reshape(N_CHIPS * hr, PACK_COLS)],
                    outs=[(PACK_COLS, F32)])
    arrived = _chip_exchange(pair_sum.reshape(N_CHIPS, hr, PACK_COLS), "g_chips", False)
    reduced = _pair_gather(_sum_slots(arrived, "g_chip_sum"), "g_pair_out").reshape(-1)
    grads.update(_unpack(reduced, [(n, s) for n, s, _ in SHARDED]))

    part = dict(gla_b_lr=glag["b_lr"], gla_gain=glag["gain"], fox_b_f=foxg["b_f"], fox_q_gain=foxg["q_gain"], fox_k_gain=foxg["k_gain"],
                s5_lam_re=s5g["lam_re"], s5_lam_im=s5g["lam_im"], s5_log_dt=s5g["log_dt"], s5_b_re=s5g["b_re"], s5_b_im=s5g["b_im"],
                s5_c_re=s5g["c_re"], s5_c_im=s5g["c_im"], s5_d=s5g["d"], sgu_w_s=g_ws, sgu_b_s=g_bs)
    parts_all = _gather8(_pack_rows([part[n] for n, _ in REPLICATED], SMALL_ROWS), "rep_gather")
    rep = _sum_slots(parts_all.reshape(2 * N_CHIPS, SMALL_ROWS, PACK_COLS), "rep_sum").reshape(-1)
    grads.update(_unpack(rep, REPLICATED))

    delta, new_m, new_v = {}, {}, {}
    for n, shape2 in BIG_ADAM.items():
        d, nm, nv = _adamw(a[n].reshape(shape2), grads[n].reshape(shape2), a["m_" + n].reshape(shape2), a["v_" + n].reshape(shape2), "adamw_" + n)
        delta[n], new_m[n], new_v[n] = (t.reshape(a[n].shape) for t in (d, nm, nv))
    small = [n for n in WEIGHTS if n not in BIG_ADAM]
    spec = [(n, a[n].shape) for n in small]
    packs = [_pack_rows([src[n] for n in small], SMALL_ROWS) for src in
             (a, grads, {n: a["m_" + n] for n in small}, {n: a["v_" + n] for n in small})]
    for tgt, res in zip((delta, new_m, new_v), _adamw(*packs, "adamw_small")):
        tgt.update(_unpack(res.reshape(-1), spec))
    outs = [loss, grad_x[None]]
    for group in (grads, delta, new_m, new_v):
        outs += [group[n].reshape(a[n].shape) for n in WEIGHTS]
    return tuple(outs)


def kernel(x, c, ada_w, ada_b, even_w_in, even_w_out, gla_w_lr, gla_b_lr, gla_gain, fox_b_f, fox_q_gain, fox_k_gain, odd_w_in,
           odd_w_out, s5_lam_re, s5_lam_im, s5_log_dt, s5_b_re, s5_b_im, s5_c_re, s5_c_im, s5_d, s5_w_glu, s5_b_glu, sgu_ln_gain,
           sgu_ln_bias, sgu_w_s, sgu_b_s, mlp_w1, mlp_w2, loss_target, m_ada_w, m_ada_b, m_even_w_in, m_even_w_out, m_gla_w_lr,
           m_gla_b_lr, m_gla_gain, m_fox_b_f, m_fox_q_gain, m_fox_k_gain, m_odd_w_in, m_odd_w_out, m_s5_lam_re, m_s5_lam_im,
           m_s5_log_dt, m_s5_b_re, m_s5_b_im, m_s5_c_re, m_s5_c_im, m_s5_d, m_s5_w_glu, m_s5_b_glu, m_sgu_ln_gain, m_sgu_ln_bias,
           m_sgu_w_s, m_sgu_b_s, m_mlp_w1, m_mlp_w2, v_ada_w, v_ada_b, v_even_w_in, v_even_w_out, v_gla_w_lr, v_gla_b_lr,
           v_gla_gain, v_fox_b_f, v_fox_q_gain, v_fox_k_gain, v_odd_w_in, v_odd_w_out, v_s5_lam_re, v_s5_lam_im, v_s5_log_dt,
           v_s5_b_re, v_s5_b_im, v_s5_c_re, v_s5_c_im, v_s5_d, v_s5_w_glu, v_s5_b_glu, v_sgu_ln_gain, v_sgu_ln_bias, v_sgu_w_s,
           v_sgu_b_s, v_mlp_w1, v_mlp_w2):
    return _step((x, c, ada_w, ada_b, even_w_in, even_w_out, gla_w_lr, gla_b_lr, gla_gain, fox_b_f, fox_q_gain, fox_k_gain,
                  odd_w_in, odd_w_out, s5_lam_re, s5_lam_im, s5_log_dt, s5_b_re, s5_b_im, s5_c_re, s5_c_im, s5_d, s5_w_glu,
                  s5_b_glu, sgu_ln_gain, sgu_ln_bias, sgu_w_s, sgu_b_s, mlp_w1, mlp_w2, loss_target, m_ada_w, m_ada_b,
                  m_even_w_in, m_even_w_out, m_gla_w_lr, m_gla_b_lr, m_gla_gain, m_fox_b_f, m_fox_q_gain, m_fox_k_gain,
                  m_odd_w_in, m_odd_w_out, m_s5_lam_re, m_s5_lam_im, m_s5_log_dt, m_s5_b_re, m_s5_b_im, m_s5_c_re, m_s5_c_im,
                  m_s5_d, m_s5_w_glu, m_s5_b_glu, m_sgu_ln_gain, m_sgu_ln_bias, m_sgu_w_s, m_sgu_b_s, m_mlp_w1, m_mlp_w2, v_ada_w,
                  v_ada_b, v_even_w_in, v_even_w_out, v_gla_w_lr, v_gla_b_lr, v_gla_gain, v_fox_b_f, v_fox_q_gain, v_fox_k_gain,
                  v_odd_w_in, v_odd_w_out, v_s5_lam_re, v_s5_lam_im, v_s5_log_dt, v_s5_b_re, v_s5_b_im, v_s5_c_re, v_s5_c_im,
                  v_s5_d, v_s5_w_glu, v_s5_b_glu, v_sgu_ln_gain, v_sgu_ln_bias, v_sgu_w_s, v_sgu_b_s, v_mlp_w1, v_mlp_w2))
```

```python
import functools
import math

import jax
import jax.numpy as jnp
import numpy as np
from jax import lax
from jax.experimental import pallas as pl
from jax.experimental.pallas import tpu as pltpu

F32 = jnp.float32
BF16 = jnp.bfloat16
MESH = pl.DeviceIdType.MESH
ANY = pl.BlockSpec(memory_space=pl.ANY)
DMA_SEM = pltpu.SemaphoreType.DMA

D_MODEL = 1024
HEAD_DIM = 64
MIX_HALF = 512
GLA_RANK = 16
GLA_TAU = 16.0
GLA_CHUNK = 64
S5_GROUPS = 32
S5_GROUP_WIDTH = 16
S5_STATE = 64
S5_N = S5_GROUPS * S5_STATE
SGU_GROUPS = 8
SGU_CHUNK = 128
D_FF = 4096
EPS = 1e-6
N_CHIPS = 4
LANES = 128
VMEM_LIMIT = 48 * 1024 * 1024
PAIR_COPIES = 16

ADAM_LR = 0.001
ADAM_B1 = 0.9
ADAM_B2 = 0.999
ADAM_EPS = 1e-08
ADAM_WD = 0.01
ADAM_STEP = 10


def _cparams(*sem):
    return pltpu.CompilerParams(dimension_semantics=sem, vmem_limit_bytes=VMEM_LIMIT)


def _pair_swap(x, name):
    lead = x.shape[:-2]
    rows = x.shape[-2]
    nsplit = max(1, PAIR_COPIES // max(1, math.prod(lead)))
    while nsplit > 1 and rows % (nsplit * 16):
        nsplit -= 1
    pieces = [idx + (pl.ds(j * (rows // nsplit), rows // nsplit),) for idx in np.ndindex(*lead) for j in range(nsplit)]

    def body(x_ref, o_ref, send_sems, recv_sems):
        mx, my, mc = lax.axis_index("x"), lax.axis_index("y"), lax.axis_index("c")
        copies = [pltpu.make_async_remote_copy(src_ref=x_ref.at[p], dst_ref=o_ref.at[p], send_sem=send_sems.at[j], recv_sem=recv_sems.at[j],
                                               device_id=(mx, my, 1 - mc), device_id_type=MESH) for j, p in enumerate(pieces)]
        for cp in copies:
            cp.start()
        for cp in copies:
            cp.wait_recv()
        for cp in copies:
            cp.wait_send()

    return pl.pallas_call(
        body, name=name, out_shape=jax.ShapeDtypeStruct(x.shape, x.dtype), in_specs=[ANY], out_specs=ANY,
        scratch_shapes=[DMA_SEM((len(pieces),)), DMA_SEM((len(pieces),))])(x)


def _by_core(mine, theirs):
    first = lax.axis_index("c") == 0
    return jnp.stack([jnp.where(first, mine, theirs), jnp.where(first, theirs, mine)])


def _chip_exchange(x, name, bcast):
    blk = x.shape if bcast else x.shape[1:]

    def body(x_ref, o_ref, send_sems, recv_sems, loc_sem):
        mx, my, mc = lax.axis_index("x"), lax.axis_index("y"), lax.axis_index("c")
        me = 2 * mx + my
        peers = [(1 - mx, my), (mx, 1 - my), (1 - mx, 1 - my)]

        def src(k):
            return x_ref if bcast else x_ref.at[k]

        loc = pltpu.make_async_copy(src(me), o_ref.at[me], loc_sem)
        loc.start()
        sends = []
        for j, (px, py) in enumerate(peers):
            cp = pltpu.make_async_remote_copy(src_ref=src(2 * px + py), dst_ref=o_ref.at[me], send_sem=send_sems.at[j],
                                              recv_sem=recv_sems.at[j], device_id=(px, py, mc), device_id_type=MESH)
            cp.start()
            sends.append(cp)
        for j, (px, py) in enumerate(peers):
            pltpu.make_async_remote_copy(src_ref=src(me), dst_ref=o_ref.at[2 * px + py], send_sem=send_sems.at[j],
                                         recv_sem=recv_sems.at[j], device_id=(px, py, mc), device_id_type=MESH).wait_recv()
        for cp in sends:
            cp.wait_send()
        loc.wait()

    return pl.pallas_call(
        body, name=name, out_shape=jax.ShapeDtypeStruct((N_CHIPS,) + tuple(blk), x.dtype), in_specs=[ANY], out_specs=ANY,
        scratch_shapes=[DMA_SEM((3,)), DMA_SEM((3,)), DMA_SEM])(x)


def _gather8(x, name):
    return _chip_exchange(_by_core(x, _pair_swap(x, name + "_pair")), name + "_chips", True)


def _tile(n, want):
    if n <= want:
        return n
    t = (want // LANES) * LANES
    while t >= LANES:
        if n % t == 0:
            return t
        t -= LANES
    raise ValueError(f"no lane-aligned tile for {n}")


_DIMS = {"nn": (((1,), (0,)), ((), ())), "nt": (((1,), (1,)), ((), ())), "tn": (((0,), (0,)), ((), ()))}


MM_FULL_K = 4096
MM_SLAB_K = 2048
MM_TILES = ((1024, 1024), (512, 1024), (1024, 512), (512, 512), (256, 512), (256, 256))
MM_VMEM_BUDGET = 36 * 1024 * 1024


def _mm(a, b, mode, name, *, a_pro=None, epi=None, extras=(), out_dtype=F32, tm_max=1024, tn_max=1024, tk=None, a_cols=None):
    c0, csize = a_cols if a_cols is not None else (0, a.shape[1])
    if mode == "tn":
        K, M = a.shape[0], csize
    else:
        M, K = a.shape[0], csize
    N = b.shape[0] if mode == "nt" else b.shape[1]
    assert (b.shape[1] if mode == "nt" else b.shape[0]) == K, (a.shape, b.shape, mode)
    if tk is None:
        tk = K if (mode != "tn" and K <= MM_FULL_K) else MM_SLAB_K
    tk = _tile(K, tk)
    nk = K // tk
    n_mn = sum(1 for _, kind in extras if kind == "mn")
    for tm_want, tn_want in MM_TILES:
        tm, tn = _tile(M, min(tm_want, tm_max)), _tile(N, min(tn_want, tn_max))
        need = 2 * (tm * tk * a.dtype.itemsize + tk * tn * b.dtype.itemsize + tm * tn * 4 * (1 + n_mn)) + tm * tn * 4 * (nk > 1)
        if need <= MM_VMEM_BUDGET:
            break
    if mode == "tn":
        assert c0 % tm == 0
        a_spec = pl.BlockSpec((tk, tm), lambda i, j, k: (k, i + c0 // tm))
    else:
        assert c0 % tk == 0
        a_spec = pl.BlockSpec((tm, tk), lambda i, j, k: (i, k + c0 // tk))
    b_spec = pl.BlockSpec((tn, tk), lambda i, j, k: (j, k)) if mode == "nt" else pl.BlockSpec((tk, tn), lambda i, j, k: (k, j))
    ex_specs = []
    for arr, kind in extras:
        if kind == "mn":
            assert arr.shape == (M, N)
            ex_specs.append(pl.BlockSpec((tm, tn), lambda i, j, k: (i, j)))
        else:
            assert arr.shape == (1, N)
            ex_specs.append(pl.BlockSpec((1, tn), lambda i, j, k: (0, j)))
    n_ex = len(extras)

    def body(*refs):
        a_ref, b_ref = refs[:2]
        ex_refs = refs[2:2 + n_ex]
        o_ref = refs[2 + n_ex]
        acc_ref = refs[3 + n_ex] if nk > 1 else None
        k = pl.program_id(2)
        av = a_ref[...]
        if a_pro is not None:
            av = a_pro(av)
        part = lax.dot_general(av.astype(BF16), b_ref[...].astype(BF16), _DIMS[mode], preferred_element_type=F32)
        if nk == 1:
            if epi is not None:
                part = epi(part, *[r[...] for r in ex_refs])
            o_ref[...] = part.astype(o_ref.dtype)
            return

        @pl.when(k == 0)
        def _():
            acc_ref[...] = part

        @pl.when(k > 0)
        def _():
            acc_ref[...] += part

        @pl.when(k == nk - 1)
        def _():
            acc = acc_ref[...]
            if epi is not None:
                acc = epi(acc, *[r[...] for r in ex_refs])
            o_ref[...] = acc.astype(o_ref.dtype)

    return pl.pallas_call(
        body, name=name, grid=(M // tm, N // tn, nk),
        in_specs=[a_spec, b_spec] + ex_specs,
        out_specs=pl.BlockSpec((tm, tn), lambda i, j, k: (i, j)),
        out_shape=jax.ShapeDtypeStruct((M, N), out_dtype),
        scratch_shapes=[pltpu.VMEM((tm, tn), F32)] if nk > 1 else [],
        compiler_params=_cparams("parallel", "parallel", "arbitrary"))(a, b, *[e[0] for e in extras])


ROWS = 256


def _row_spec(w, ts=ROWS):
    return pl.BlockSpec((ts, w), lambda i: (i, 0))


def _vec_spec(w):
    return pl.BlockSpec((1, w), lambda i: (0, 0))


def _res_rms(x, sc, sh, name, y=None, g=None):
    S, D = x.shape
    has_res = y is not None

    def body(*refs):
        if has_res:
            x_ref, y_ref, g_ref, sc_ref, sh_ref, xo_ref, h_ref = refs
            xv = x_ref[...] + g_ref[...] * y_ref[...]
            xo_ref[...] = xv
        else:
            x_ref, sc_ref, sh_ref, h_ref = refs
            xv = x_ref[...]
        r = lax.rsqrt(jnp.mean(xv * xv, axis=-1, keepdims=True) + EPS)
        h_ref[...] = (xv * r * (1.0 + sc_ref[...]) + sh_ref[...]).astype(BF16)

    row, vec = _row_spec(D), _vec_spec(D)
    if has_res:
        return pl.pallas_call(body, name=name, grid=(S // ROWS,), in_specs=[row, row, vec, vec, vec], out_specs=[row, row],
                              out_shape=[jax.ShapeDtypeStruct((S, D), F32), jax.ShapeDtypeStruct((S, D), BF16)],
                              compiler_params=_cparams("parallel"))(x, y, g, sc, sh)
    h = pl.pallas_call(body, name=name, grid=(S // ROWS,), in_specs=[row, vec, vec], out_specs=row,
                       out_shape=jax.ShapeDtypeStruct((S, D), BF16), compiler_params=_cparams("parallel"))(x, sc, sh)
    return x, h


def _res_rms_bwd(x, dh, sc, dres, name, y=None, g=None):
    S, D = x.shape
    has_res = y is not None

    def body(*refs):
        if has_res:
            x_ref, dh_ref, sc_ref, dres_ref, y_ref, g_ref, dx_ref, dy_ref, dg_ref, dsc_ref, dsh_ref = refs
        else:
            x_ref, dh_ref, sc_ref, dres_ref, dx_ref, dsc_ref, dsh_ref = refs
        first = pl.program_id(0) == 0
        xv = x_ref[...]
        dh = dh_ref[...]
        r = lax.rsqrt(jnp.mean(xv * xv, axis=-1, keepdims=True) + EPS)
        xn = xv * r
        dxn = dh * (1.0 + sc_ref[...])
        dx = dres_ref[...] + r * (dxn - xn * jnp.mean(dxn * xn, axis=-1, keepdims=True))
        dx_ref[...] = dx
        parts = [(dsc_ref, jnp.sum(dh * xn, axis=0, keepdims=True)), (dsh_ref, jnp.sum(dh, axis=0, keepdims=True))]
        if has_res:
            dy_ref[...] = dx * g_ref[...]
            parts.append((dg_ref, jnp.sum(dx * y_ref[...], axis=0, keepdims=True)))
        for ref, val in parts:
            @pl.when(first)
            def _(ref=ref, val=val):
                ref[...] = val

            @pl.when(jnp.logical_not(first))
            def _(ref=ref, val=val):
                ref[...] += val

    row, vec = _row_spec(D), _vec_spec(D)
    full = jax.ShapeDtypeStruct((S, D), F32)
    v = jax.ShapeDtypeStruct((1, D), F32)
    if has_res:
        return pl.pallas_call(body, name=name, grid=(S // ROWS,), in_specs=[row, row, vec, row, row, vec],
                              out_specs=[row, row, vec, vec, vec], out_shape=[full, full, v, v, v],
                              compiler_params=_cparams("arbitrary"))(x, dh, sc, dres, y, g)
    return pl.pallas_call(body, name=name, grid=(S // ROWS,), in_specs=[row, row, vec, row],
                          out_specs=[row, vec, vec], out_shape=[full, v, v],
                          compiler_params=_cparams("arbitrary"))(x, dh, sc, dres)


def _res_loss(x, m, g, target, name):
    S, D = x.shape

    def body(x_ref, m_ref, g_ref, t_ref, loss_ref, dx_ref, dm_ref, dg_ref):
        first = pl.program_id(0) == 0
        mv = m_ref[...]
        err = x_ref[...] + g_ref[...] * mv - t_ref[...]
        dx = err * (1.0 / D)
        dx_ref[...] = dx
        dm_ref[...] = dx * g_ref[...]
        part = 0.5 * jnp.sum(jnp.mean(err * err, axis=-1, keepdims=True), axis=0, keepdims=True)
        dg = jnp.sum(dx * mv, axis=0, keepdims=True)

        @pl.when(first)
        def _():
            loss_ref[...] = jnp.broadcast_to(part, loss_ref.shape)
            dg_ref[...] = dg

        @pl.when(jnp.logical_not(first))
        def _():
            loss_ref[...] += jnp.broadcast_to(part, loss_ref.shape)
            dg_ref[...] += dg

    row, vec = _row_spec(D), _vec_spec(D)
    full = jax.ShapeDtypeStruct((S, D), F32)
    return pl.pallas_call(body, name=name, grid=(S // ROWS,), in_specs=[row, row, vec, row],
                          out_specs=[pl.BlockSpec((8, LANES), lambda i: (0, 0)), row, row, vec],
                          out_shape=[jax.ShapeDtypeStruct((8, LANES), F32), full, full, jax.ShapeDtypeStruct((1, D), F32)],
                          compiler_params=_cparams("arbitrary"))(x, m, g, target)


def _adamw(w, g, m, v, name):
    R, C = w.shape
    tr = R if R <= 256 else 256
    assert R % tr == 0

    def body(w_ref, g_ref, m_ref, v_ref, d_ref, nm_ref, nv_ref):
        gv = g_ref[...]
        nm = ADAM_B1 * m_ref[...] + (1.0 - ADAM_B1) * gv
        nv = ADAM_B2 * v_ref[...] + (1.0 - ADAM_B2) * jnp.square(gv)
        m_hat = nm / (1.0 - ADAM_B1 ** ADAM_STEP)
        v_hat = nv / (1.0 - ADAM_B2 ** ADAM_STEP)
        d_ref[...] = -ADAM_LR * (m_hat / (jnp.sqrt(v_hat) + ADAM_EPS) + ADAM_WD * w_ref[...])
        nm_ref[...] = nm
        nv_ref[...] = nv

    spec = pl.BlockSpec((tr, C), lambda i: (i, 0))
    out = jax.ShapeDtypeStruct((R, C), F32)
    return pl.pallas_call(body, name=name, grid=(R // tr,), in_specs=[spec] * 4, out_specs=[spec] * 3,
                          out_shape=[out, out, out], compiler_params=_cparams("parallel"))(w, g, m, v)


def _sum_slots(x, name):
    n, R, C = x.shape
    tr = R if R <= 256 else 256
    assert R % tr == 0

    def body(x_ref, o_ref):
        acc = x_ref[0]
        for j in range(1, n):
            acc = acc + x_ref[j]
        o_ref[...] = acc

    return pl.pallas_call(body, name=name, grid=(R // tr,), in_specs=[pl.BlockSpec((n, tr, C), lambda i: (0, i, 0))],
                          out_specs=pl.BlockSpec((tr, C), lambda i: (i, 0)), out_shape=jax.ShapeDtypeStruct((R, C), x.dtype),
                          compiler_params=_cparams("parallel"))(x)


def _ew(fn, name, tiled, consts=(), outs=(), sums=(), ts=ROWS):
    tiled = [t if isinstance(t, tuple) else (t, t.shape[1], 0) for t in tiled]
    S = tiled[0][0].shape[0]
    n_t, n_c, n_o, n_s = len(tiled), len(consts), len(outs), len(sums)

    def body(*refs):
        ins = [r[...] for r in refs[:n_t + n_c]]
        res = fn(*ins)
        res = res if isinstance(res, (tuple, list)) else (res,)
        assert len(res) == n_o + n_s
        o_refs = refs[n_t + n_c:]
        for r, val in zip(o_refs[:n_o], res[:n_o]):
            r[...] = val.astype(r.dtype)
        first = pl.program_id(0) == 0
        for r, val in zip(o_refs[n_o:], res[n_o:]):
            @pl.when(first)
            def _(r=r, val=val):
                r[...] = val

            @pl.when(jnp.logical_not(first))
            def _(r=r, val=val):
                r[...] += val

    in_specs = [pl.BlockSpec((ts, w), lambda i, cb=cb: (i, cb)) for _, w, cb in tiled]
    in_specs += [pl.BlockSpec(c.shape, lambda i, nd=c.ndim: (0,) * nd) for c in consts]
    out_specs = [_row_spec(w, ts) for w, _ in outs] + [_vec_spec(w) for w in sums]
    out_shape = [jax.ShapeDtypeStruct((S, w), dt) for w, dt in outs] + [jax.ShapeDtypeStruct((1, w), F32) for w in sums]
    res = pl.pallas_call(body, name=name, grid=(S // ts,), in_specs=in_specs, out_specs=out_specs, out_shape=out_shape,
                         compiler_params=_cparams("arbitrary" if sums else "parallel"))(*[t[0] for t in tiled], *consts)
    return res


_GELU_C = math.sqrt(2.0 / math.pi)


def _gelu(x):
    return 0.5 * x * (1.0 + jnp.tanh(_GELU_C * (x + 0.044715 * x * x * x)))


def _dgelu(x):
    t = jnp.tanh(_GELU_C * (x + 0.044715 * x * x * x))
    return 0.5 * (1.0 + t) + 0.5 * x * (1.0 - t * t) * _GELU_C * (1.0 + 3.0 * 0.044715 * x * x)


def _sigmoid(x):
    return 1.0 / (1.0 + jnp.exp(-x))


def _log_sigmoid(x):
    return jnp.minimum(x, 0.0) - jnp.log(1.0 + jnp.exp(-jnp.abs(x)))


S5_TN = 64
S5_TB = 512
SCAN_TN = 32
SCAN_TB = 1024
SCAN_GROUP = 4


def _cmul(ar, ai, br, bi):
    return ar * br - ai * bi, ar * bi + ai * br


def _s5_discretise(lam_re, lam_im, log_dt, b_re, b_im):
    dt = jnp.exp(log_dt)[:, None]
    mag = jnp.exp(lam_re * dt)
    ang = lam_im * dt
    abar_re = mag * jnp.cos(ang)
    abar_im = mag * jnp.sin(ang)
    den = lam_re * lam_re + lam_im * lam_im
    coef_re = ((abar_re - 1.0) * lam_re + abar_im * lam_im) / den
    coef_im = (abar_im * lam_re - (abar_re - 1.0) * lam_im) / den
    bbar_re = coef_re[..., None] * b_re - coef_im[..., None] * b_im
    bbar_im = coef_re[..., None] * b_im + coef_im[..., None] * b_re
    return abar_re, abar_im, bbar_re, bbar_im


def _s5_scan_tables(a_re, a_im, reverse):
    pr, pi = [a_re], [a_im]
    for _ in range(7):
        r, i = _cmul(pr[-1], pi[-1], pr[-1], pi[-1])
        pr.append(r)
        pi.append(i)
    apow = jnp.broadcast_to(jnp.stack([jnp.stack(pr), jnp.stack(pi)])[..., None], (2, 8, a_re.shape[0], LANES))
    n = np.arange(1, LANES + 1)
    if reverse:
        n = n[::-1]
    tr = jnp.ones((a_re.shape[0], LANES), F32)
    ti = jnp.zeros((a_re.shape[0], LANES), F32)
    for k in range(8):
        bit = jnp.asarray(((n >> k) & 1).astype(np.float32))[None, :]
        fr, fi = pr[k][:, None], pi[k][:, None]
        mr = bit * fr + (1.0 - bit)
        mi = bit * fi
        tr, ti = _cmul(tr, ti, mr, mi)
    return apow, jnp.stack([tr, ti])


def _s5_scan(bu, apow, ptab, name, reverse):
    _, N, S = bu.shape
    tn, tb = SCAN_TN, min(SCAN_TB, S)
    nt = S // tb
    nsub = tb // LANES
    order = list(range(nsub - 1, -1, -1) if reverse else range(nsub))

    def tmap(i, t):
        return (0, i, nt - 1 - t) if reverse else (0, i, t)

    def body(bu_ref, ap_ref, pt_ref, x_ref, carry_ref):
        @pl.when(pl.program_id(1) == 0)
        def _():
            carry_ref[...] = jnp.zeros_like(carry_ref)

        lane = lax.broadcasted_iota(jnp.int32, (tn, LANES), 1)
        at_edge = lane == (0 if reverse else LANES - 1)
        edges = {}
        for g in range(0, nsub, SCAN_GROUP):
            subs = order[g:g + SCAN_GROUP]
            xs = [(bu_ref[0, :, pl.ds(sb * LANES, LANES)], bu_ref[1, :, pl.ds(sb * LANES, LANES)]) for sb in subs]
            for k in range(7):
                sh = 1 << k
                ar, ai = ap_ref[0, k], ap_ref[1, k]
                valid = (lane < LANES - sh) if reverse else (lane >= sh)
                nxt = []
                for xr, xi in xs:
                    rr = jnp.where(valid, pltpu.roll(xr, (LANES - sh) if reverse else sh, 1), 0.0)
                    ri = jnp.where(valid, pltpu.roll(xi, (LANES - sh) if reverse else sh, 1), 0.0)
                    nxt.append((xr + ar * rr - ai * ri, xi + ar * ri + ai * rr))
                xs = nxt
            for sb, (xr, xi) in zip(subs, xs):
                x_ref[0, :, pl.ds(sb * LANES, LANES)] = xr
                x_ref[1, :, pl.ds(sb * LANES, LANES)] = xi
                edges[sb] = (jnp.broadcast_to(jnp.sum(jnp.where(at_edge, xr, 0.0), axis=1, keepdims=True), (tn, LANES)),
                             jnp.broadcast_to(jnp.sum(jnp.where(at_edge, xi, 0.0), axis=1, keepdims=True), (tn, LANES)))
        pr, pi = pt_ref[0], pt_ref[1]
        br, bi = ap_ref[0, 7], ap_ref[1, 7]
        cr, ci = carry_ref[0], carry_ref[1]
        for sb in order:
            sl = pl.ds(sb * LANES, LANES)
            x_ref[0, :, sl] = x_ref[0, :, sl] + pr * cr - pi * ci
            x_ref[1, :, sl] = x_ref[1, :, sl] + pr * ci + pi * cr
            er, ei = edges[sb]
            cr, ci = er + br * cr - bi * ci, ei + br * ci + bi * cr
        carry_ref[0] = cr
        carry_ref[1] = ci

    return pl.pallas_call(
        body, name=name, grid=(N // tn, nt),
        in_specs=[pl.BlockSpec((2, tn, tb), tmap), pl.BlockSpec((2, 8, tn, LANES), lambda i, t: (0, 0, i, 0)),
                  pl.BlockSpec((2, tn, LANES), lambda i, t: (0, i, 0))],
        out_specs=pl.BlockSpec((2, tn, tb), tmap), out_shape=jax.ShapeDtypeStruct((2, N, S), F32),
        scratch_shapes=[pltpu.VMEM((2, tn, LANES), F32)], compiler_params=_cparams("parallel", "arbitrary"))(bu, apow, ptab)


def _s5_da(lam, x, name):
    _, N, S = x.shape
    nt = S // S5_TB
    nsub = S5_TB // LANES

    def body(l_ref, x_ref, o_ref, acc_ref, carry_ref):
        t = pl.program_id(1)

        @pl.when(t == 0)
        def _():
            acc_ref[...] = jnp.zeros_like(acc_ref)
            carry_ref[...] = jnp.zeros_like(carry_ref)

        lane = lax.broadcasted_iota(jnp.int32, (S5_TN, LANES), 1)
        cr, ci = carry_ref[0], carry_ref[1]
        ar, ai = acc_ref[0], acc_ref[1]
        for sb in range(nsub):
            sl = pl.ds(sb * LANES, LANES)
            xr, xi = x_ref[0, :, sl], x_ref[1, :, sl]
            pr = jnp.where(lane == 0, pltpu.roll(cr, 1, 1), pltpu.roll(xr, 1, 1))
            pi = jnp.where(lane == 0, pltpu.roll(ci, 1, 1), pltpu.roll(xi, 1, 1))
            lr, li = l_ref[0, :, sl], l_ref[1, :, sl]
            ar = ar + lr * pr + li * pi
            ai = ai + li * pr - lr * pi
            cr, ci = xr, xi
        acc_ref[0] = ar
        acc_ref[1] = ai
        carry_ref[0] = cr
        carry_ref[1] = ci

        @pl.when(t == nt - 1)
        def _():
            o_ref[0] = jnp.sum(ar, axis=1, keepdims=True)
            o_ref[1] = jnp.sum(ai, axis=1, keepdims=True)

    blk = pl.BlockSpec((2, S5_TN, S5_TB), lambda i, t: (0, i, t))
    return pl.pallas_call(
        body, name=name, grid=(N // S5_TN, nt), in_specs=[blk, blk],
        out_specs=pl.BlockSpec((2, S5_TN, 1), lambda i, t: (0, i, 0)), out_shape=jax.ShapeDtypeStruct((2, N, 1), F32),
        scratch_shapes=[pltpu.VMEM((2, S5_TN, LANES), F32), pltpu.VMEM((2, S5_TN, LANES), F32)],
        compiler_params=_cparams("parallel", "arbitrary"))(lam, x)


def _block_diag(t):
    G, a, b = t.shape
    return (t[:, :, None, :] * jnp.eye(G, dtype=t.dtype)[:, None, :, None]).reshape(G * a, G * b)


def _block_diag_take(m, G):
    a, b = m.shape[0] // G, m.shape[1] // G
    m4 = m.reshape(G, a, G, b)
    return jnp.sum(m4 * jnp.eye(G, dtype=m.dtype)[:, None, :, None], axis=2)


def _s5_block_fwd(u, w, pfx):
    a_re, a_im, bb_re, bb_im = _s5_discretise(w["lam_re"], w["lam_im"], w["log_dt"], w["b_re"], w["b_im"])
    bcat = jnp.concatenate([_block_diag(bb_re), _block_diag(bb_im)], axis=0).astype(BF16)
    ccat = jnp.concatenate([_block_diag(jnp.swapaxes(w["c_re"], 1, 2)),
                            -_block_diag(jnp.swapaxes(w["c_im"], 1, 2))], axis=0).astype(BF16)
    S = u.shape[0]
    af_re, af_im = a_re.reshape(-1), a_im.reshape(-1)
    apow, ptab = _s5_scan_tables(af_re, af_im, False)
    bu = _mm(bcat, u, "nt", pfx + "_bu").reshape(2, S5_N, S)
    x = _s5_scan(bu, apow, ptab, pfx + "_scan", False)
    d_row = w["d"].reshape(1, MIX_HALF)
    ys = _mm(x.reshape(2 * S5_N, S), ccat, "tn", pfx + "_y", epi=lambda acc, ut, dr: acc + dr * ut, extras=[(u, "mn"), (d_row, "n")])
    z = _mm(ys, w["w_glu"], "nn", pfx + "_glu", a_pro=_gelu, epi=lambda acc, b: acc + b, extras=[(w["b_glu"].reshape(1, -1), "n")])
    y2, = _ew(lambda ysv, zv: _gelu(ysv) * _sigmoid(zv), pfx + "_gate", [ys, z], outs=[(MIX_HALF, F32)])
    return y2, dict(u=u, x=x, ys=ys, z=z, bcat=bcat, ccat=ccat, a=(af_re, af_im), d_row=d_row)


def _s5_block_bwd(dy2, w, res, pfx):
    u, x, ys, z, bcat, ccat = res["u"], res["x"], res["ys"], res["z"], res["bcat"], res["ccat"]
    S = u.shape[0]

    def gate_bwd(dy, ysv, zv):
        sg = _sigmoid(zv)
        dz = dy * _gelu(ysv) * sg * (1.0 - sg)
        return dz, jnp.sum(dz, axis=0, keepdims=True)

    dz, db_glu = _ew(gate_bwd, pfx + "_gate_bwd", [dy2, ys, z], outs=[(MIX_HALF, F32)], sums=[MIX_HALF])
    dw_glu = _mm(ys, dz, "tn", pfx + "_dwglu", a_pro=_gelu)
    dys = _mm(dz, w["w_glu"], "nt", pfx + "_dys", epi=lambda acc, dy, zv, ysv: (acc + dy * _sigmoid(zv)) * _dgelu(ysv),
              extras=[(dy2, "mn"), (z, "mn"), (ys, "mn")])
    dd, = _ew(lambda a, b: jnp.sum(a * b, axis=0, keepdims=True), pfx + "_dd", [dys, u], sums=[MIX_HALF])
    xcat = x.reshape(2 * S5_N, S)
    dccat = _mm(xcat, dys, "nn", pfx + "_dc")
    dx = _mm(ccat, dys, "nt", pfx + "_dx").reshape(2, S5_N, S)
    af_re, af_im = res["a"]
    apow, ptab = _s5_scan_tables(af_re, -af_im, True)
    lam = _s5_scan(dx, apow, ptab, pfx + "_scan_bwd", True)
    lcat = lam.reshape(2 * S5_N, S)
    dbcat = _mm(lcat, u, "nn", pfx + "_db")
    du = _mm(lcat, bcat, "tn", pfx + "_du", epi=lambda acc, dyv, dr: acc + dyv * dr, extras=[(dys, "mn"), (res["d_row"], "n")])
    da = _s5_da(lam, x, pfx + "_da")
    G = S5_GROUPS
    d_abar_re, d_abar_im = da[0].reshape(G, S5_STATE), da[1].reshape(G, S5_STATE)
    d_bb_re, d_bb_im = _block_diag_take(dbcat[:S5_N], G), _block_diag_take(dbcat[S5_N:], G)
    _, vjp = jax.vjp(_s5_discretise, w["lam_re"], w["lam_im"], w["log_dt"], w["b_re"], w["b_im"])
    g_lam_re, g_lam_im, g_log_dt, g_b_re, g_b_im = vjp((d_abar_re, d_abar_im, d_bb_re, d_bb_im))
    g_c_re = jnp.swapaxes(_block_diag_take(dccat[:S5_N], G), 1, 2)
    g_c_im = -jnp.swapaxes(_block_diag_take(dccat[S5_N:], G), 1, 2)
    grads = dict(lam_re=g_lam_re, lam_im=g_lam_im, log_dt=g_log_dt, b_re=g_b_re, b_im=g_b_im, c_re=g_c_re, c_im=g_c_im,
                 d=dd.reshape(G, S5_GROUP_WIDTH), w_glu=dw_glu, b_glu=db_glu.reshape(-1))
    return du, grads


SGU_TS = 512
N_PAIRS = MIX_HALF // LANES


def _half_masks(rows):
    lane = lax.broadcasted_iota(jnp.int32, (rows, LANES), 1)
    left = (lane < HEAD_DIM).astype(F32)
    return left, 1.0 - left


def _sgu_norm(zv, gain, bias):
    v = _gelu(zv)
    mu = jnp.mean(v, axis=-1, keepdims=True)
    vc = v - mu
    rstd = lax.rsqrt(jnp.mean(vc * vc, axis=-1, keepdims=True) + EPS)
    vhat = vc * rstd
    return vhat, rstd, vhat * gain + bias


def _sgu_tables(w_s, b_s):
    mask = jnp.tril(jnp.ones((SGU_CHUNK, SGU_CHUNK), dtype=bool))
    wm = jnp.where(mask[None], w_s, 0.0).astype(BF16)
    bias_tab = jnp.repeat(b_s.T, MIX_HALF // SGU_GROUPS, axis=1)
    return wm, bias_tab


def _sgu_fwd(proj, ln_gain, ln_bias, wm, bias_tab, name):
    S = proj.shape[0]
    nch = SGU_TS // SGU_CHUNK

    def body(zu_ref, zv_ref, g_ref, b_ref, w_ref, bt_ref, o_ref):
        left, right = _half_masks(SGU_CHUNK)
        _, _, vn = _sgu_norm(zv_ref[...], g_ref[...], b_ref[...])
        for ch in range(nch):
            rows = pl.ds(ch * SGU_CHUNK, SGU_CHUNK)
            for p in range(N_PAIRS):
                cols = pl.ds(p * LANES, LANES)
                vp = vn[ch * SGU_CHUNK:(ch + 1) * SGU_CHUNK, p * LANES:(p + 1) * LANES]
                mixed = (jnp.dot(w_ref[2 * p], (vp * left).astype(BF16), preferred_element_type=F32)
                         + jnp.dot(w_ref[2 * p + 1], (vp * right).astype(BF16), preferred_element_type=F32) + bt_ref[:, cols])
                o_ref[rows, cols] = _gelu(zu_ref[rows, cols]) * mixed

    vec = _vec_spec(MIX_HALF)
    return pl.pallas_call(
        body, name=name, grid=(S // SGU_TS,),
        in_specs=[pl.BlockSpec((SGU_TS, MIX_HALF), lambda i: (i, 1)), pl.BlockSpec((SGU_TS, MIX_HALF), lambda i: (i, 2)), vec, vec,
                  pl.BlockSpec((SGU_GROUPS, SGU_CHUNK, SGU_CHUNK), lambda i: (0, 0, 0)), pl.BlockSpec((SGU_CHUNK, MIX_HALF), lambda i: (0, 0))],
        out_specs=_row_spec(MIX_HALF, SGU_TS), out_shape=jax.ShapeDtypeStruct((S, MIX_HALF), F32),
        compiler_params=_cparams("parallel"))(proj, proj, ln_gain, ln_bias, wm, bias_tab)


def _sgu_bwd(dout, proj, ln_gain, ln_bias, wm, bias_tab, name):
    S = proj.shape[0]
    nch = SGU_TS // SGU_CHUNK
    nt_dims = (((1,), (1,)), ((), ()))
    tn_dims = (((0,), (0,)), ((), ()))

    def body(do_ref, zu_ref, zv_ref, g_ref, b_ref, w_ref, bt_ref, dzu_ref, dzv_ref, dw_ref, dbt_ref, dg_ref, db_ref, dvn_ref):
        first = pl.program_id(0) == 0

        @pl.when(first)
        def _():
            dw_ref[...] = jnp.zeros_like(dw_ref)
            dbt_ref[...] = jnp.zeros_like(dbt_ref)
            dg_ref[...] = jnp.zeros_like(dg_ref)
            db_ref[...] = jnp.zeros_like(db_ref)

        left, right = _half_masks(SGU_CHUNK)
        zv = zv_ref[...]
        vhat, rstd, vn = _sgu_norm(zv, g_ref[...], b_ref[...])
        for ch in range(nch):
            rows = pl.ds(ch * SGU_CHUNK, SGU_CHUNK)
            for p in range(N_PAIRS):
                cols = pl.ds(p * LANES, LANES)
                vp = vn[ch * SGU_CHUNK:(ch + 1) * SGU_CHUNK, p * LANES:(p + 1) * LANES]
                vl, vr = (vp * left).astype(BF16), (vp * right).astype(BF16)
                mixed = (jnp.dot(w_ref[2 * p], vl, preferred_element_type=F32)
                         + jnp.dot(w_ref[2 * p + 1], vr, preferred_element_type=F32) + bt_ref[:, cols])
                zu = zu_ref[rows, cols]
                do = do_ref[rows, cols]
                dzu_ref[rows, cols] = do * mixed * _dgelu(zu)
                dmix = do * _gelu(zu)
                dbt_ref[:, cols] += dmix
                dl, dr = (dmix * left).astype(BF16), (dmix * right).astype(BF16)
                dw_ref[2 * p] += lax.dot_general(dl, vl, nt_dims, preferred_element_type=F32)
                dw_ref[2 * p + 1] += lax.dot_general(dr, vr, nt_dims, preferred_element_type=F32)
                dvn_ref[rows, cols] = (lax.dot_general(w_ref[2 * p], dl, tn_dims, preferred_element_type=F32)
                                       + lax.dot_general(w_ref[2 * p + 1], dr, tn_dims, preferred_element_type=F32))
        dvn = dvn_ref[...]
        dg_ref[...] += jnp.sum(dvn * vhat, axis=0, keepdims=True)
        db_ref[...] += jnp.sum(dvn, axis=0, keepdims=True)
        dvh = dvn * g_ref[...]
        dv = rstd * (dvh - jnp.mean(dvh, axis=-1, keepdims=True) - vhat * jnp.mean(dvh * vhat, axis=-1, keepdims=True))
        dzv_ref[...] = dv * _dgelu(zv)

    vec = _vec_spec(MIX_HALF)
    row = _row_spec(MIX_HALF, SGU_TS)
    wspec = pl.BlockSpec((SGU_GROUPS, SGU_CHUNK, SGU_CHUNK), lambda i: (0, 0, 0))
    tspec = pl.BlockSpec((SGU_CHUNK, MIX_HALF), lambda i: (0, 0))
    full = jax.ShapeDtypeStruct((S, MIX_HALF), F32)
    v = jax.ShapeDtypeStruct((1, MIX_HALF), F32)
    return pl.pallas_call(
        body, name=name, grid=(S // SGU_TS,),
        in_specs=[row, pl.BlockSpec((SGU_TS, MIX_HALF), lambda i: (i, 1)), pl.BlockSpec((SGU_TS, MIX_HALF), lambda i: (i, 2)), vec, vec,
                  wspec, tspec],
        out_specs=[row, row, wspec, tspec, vec, vec],
        out_shape=[full, full, jax.ShapeDtypeStruct((SGU_GROUPS, SGU_CHUNK, SGU_CHUNK), F32),
                   jax.ShapeDtypeStruct((SGU_CHUNK, MIX_HALF), F32), v, v],
        scratch_shapes=[pltpu.VMEM((SGU_TS, MIX_HALF), F32)],
        compiler_params=_cparams("arbitrary"))(dout, proj, proj, ln_gain, ln_bias, wm, bias_tab)


def _sgu_grads(dw, dbias_tab):
    mask = jnp.tril(jnp.ones((SGU_CHUNK, SGU_CHUNK), dtype=bool))
    g_w = jnp.where(mask[None], dw, 0.0)
    g_b = dbias_tab.reshape(SGU_CHUNK, SGU_GROUPS, MIX_HALF // SGU_GROUPS).sum(axis=-1).T
    return g_w, g_b


def _head_avg_matrix(w):
    idx = np.arange(w) // HEAD_DIM
    return jnp.asarray((idx[:, None] == idx[None, :]).astype(np.float32) / HEAD_DIM, dtype=BF16)


def _head_mean(t, bavg):
    hi = t.astype(BF16)
    lo = (t - hi.astype(F32)).astype(BF16)
    return jnp.dot(hi, bavg, preferred_element_type=F32) + jnp.dot(lo, bavg, preferred_element_type=F32)


def _head_rms(t, bavg):
    r = lax.rsqrt(_head_mean(t * t, bavg) + EPS)
    return t * r, r


def _head_rms_bwd(dn, n, r, bavg):
    return r * (dn - n * _head_mean(dn * n, bavg))


GLA_TS = 512
C = GLA_CHUNK
NT_DIMS = (((1,), (1,)), ((), ()))
TN_DIMS = (((0,), (0,)), ((), ()))
HI = lax.Precision.HIGHEST


def _bdot(a, b, dims=(((1,), (0,)), ((), ()))):
    return lax.dot_general(a.astype(BF16), b.astype(BF16), dims, preferred_element_type=F32)


def _gla_chunk_terms(q, k, z):
    row = lax.broadcasted_iota(jnp.int32, (C, C), 0)
    col = lax.broadcasted_iota(jnp.int32, (C, C), 1)
    lc = _log_sigmoid(z) * (1.0 / GLA_TAU)
    b = lax.dot_general((row >= col).astype(F32), lc, (((1,), (0,)), ((), ())), precision=HI, preferred_element_type=F32)
    b_last = jnp.sum(lc, axis=0, keepdims=True)
    b_mid = b[C // 2:C // 2 + 1, :]
    scale = HEAD_DIM ** -0.5
    e_b, e_q, e_k, e_l = jnp.exp(b), jnp.exp(b - b_mid), jnp.exp(b_mid - b), jnp.exp(b_last - b)
    qs = q * (scale * e_b)
    qe = q * (scale * e_q)
    ke = k * e_k
    kl = k * e_l
    return dict(e_b=e_b, e_q=e_q, e_k=e_k, e_l=e_l, qs=qs, qe=qe, ke=ke, kl=kl, dec=jnp.exp(b_last), causal=row >= col, scale=scale)


def _pair_block_diag():
    r = lax.broadcasted_iota(jnp.int32, (LANES, LANES), 0) // HEAD_DIM
    c = lax.broadcasted_iota(jnp.int32, (LANES, LANES), 1) // HEAD_DIM
    return (r == c).astype(F32)


def _gla_fwd(proj, z, name):
    S = proj.shape[0]
    nch = GLA_TS // C

    def body(q_ref, k_ref, v_ref, z_ref, o_ref, st_ref, state_ref):
        @pl.when(pl.program_id(1) == 0)
        def _():
            state_ref[...] = jnp.zeros_like(state_ref)

        left, right = _half_masks(C)
        bd = _pair_block_diag()
        for ch in range(nch):
            rows = pl.ds(ch * C, C)
            q, k, v = q_ref[rows, :], k_ref[rows, :], v_ref[rows, :]
            t = _gla_chunk_terms(q, k, z_ref[rows, :])
            st = state_ref[...]
            st_ref[ch, 0] = st
            o = _bdot(t["qs"], st, NT_DIMS)
            for m in (left, right):
                a = jnp.where(t["causal"], _bdot(t["qe"] * m, t["ke"], NT_DIMS), 0.0)
                o = o + m * _bdot(a, v)
            o_ref[rows, :] = o
            state_ref[...] = st * t["dec"] + bd * _bdot(v, t["kl"], TN_DIMS)

    def col(cb):
        return pl.BlockSpec((GLA_TS, LANES), lambda p, i: (i, cb * N_PAIRS + p))

    return pl.pallas_call(
        body, name=name, grid=(N_PAIRS, S // GLA_TS),
        in_specs=[col(0), col(1), col(2), pl.BlockSpec((GLA_TS, LANES), lambda p, i: (i, p))],
        out_specs=[pl.BlockSpec((GLA_TS, LANES), lambda p, i: (i, p)), pl.BlockSpec((nch, 1, LANES, LANES), lambda p, i: (i, p, 0, 0))],
        out_shape=[jax.ShapeDtypeStruct((S, MIX_HALF), F32), jax.ShapeDtypeStruct((S // C, N_PAIRS, LANES, LANES), F32)],
        scratch_shapes=[pltpu.VMEM((LANES, LANES), F32)], compiler_params=_cparams("parallel", "arbitrary"))(proj, proj, proj, z)


def _gla_bwd(do, proj, z, states, name):
    S = proj.shape[0]
    nch = GLA_TS // C
    nblk = S // GLA_TS

    def body(do_ref, q_ref, k_ref, v_ref, z_ref, st_ref, dq_ref, dk_ref, dv_ref, dlc_ref, dstate_ref):
        @pl.when(pl.program_id(1) == 0)
        def _():
            dstate_ref[...] = jnp.zeros_like(dstate_ref)

        left, right = _half_masks(C)
        bd = _pair_block_diag()
        rowi = lax.broadcasted_iota(jnp.int32, (C, LANES), 0)
        for ch in range(nch - 1, -1, -1):
            rows = pl.ds(ch * C, C)
            q, k, v, dov = q_ref[rows, :], k_ref[rows, :], v_ref[rows, :], do_ref[rows, :]
            t = _gla_chunk_terms(q, k, z_ref[rows, :])
            st = st_ref[ch, 0]
            dst_next = dstate_ref[...]
            g = bd * dst_next
            dqs = _bdot(dov, st)
            dv = _bdot(t["kl"], g, NT_DIMS)
            dkl = _bdot(v, g)
            dqe = jnp.zeros((C, LANES), F32)
            dke = jnp.zeros((C, LANES), F32)
            for m in (left, right):
                a = jnp.where(t["causal"], _bdot(t["qe"] * m, t["ke"], NT_DIMS), 0.0)
                da = jnp.where(t["causal"], _bdot(dov * m, v, NT_DIMS), 0.0)
                dv = dv + m * _bdot(a, dov, TN_DIMS)
                dqe = dqe + m * _bdot(da, t["ke"])
                dke = dke + m * _bdot(da, t["qe"], TN_DIMS)
            dstate_ref[...] = bd * (dst_next * t["dec"] + _bdot(dov, t["qs"], TN_DIMS))
            db_last = jnp.sum(dst_next * st, axis=0, keepdims=True) * t["dec"] + jnp.sum(dkl * t["kl"], axis=0, keepdims=True)
            db = dqs * t["qs"] + dqe * t["qe"] - dke * t["ke"] - dkl * t["kl"]
            db = db + jnp.where(rowi == C - 1, db_last, 0.0)
            dq_ref[rows, :] = (dqs * t["e_b"] + dqe * t["e_q"]) * t["scale"]
            dk_ref[rows, :] = dke * t["e_k"] + dkl * t["e_l"]
            dv_ref[rows, :] = dv
            row = lax.broadcasted_iota(jnp.int32, (C, C), 0)
            colm = lax.broadcasted_iota(jnp.int32, (C, C), 1)
            dlc_ref[rows, :] = lax.dot_general((colm >= row).astype(F32), db, (((1,), (0,)), ((), ())), precision=HI,
                                               preferred_element_type=F32)

    def col(cb):
        return pl.BlockSpec((GLA_TS, LANES), lambda p, i: (nblk - 1 - i, cb * N_PAIRS + p))

    blk = pl.BlockSpec((GLA_TS, LANES), lambda p, i: (nblk - 1 - i, p))
    full = jax.ShapeDtypeStruct((S, MIX_HALF), F32)
    return pl.pallas_call(
        body, name=name, grid=(N_PAIRS, nblk),
        in_specs=[blk, col(0), col(1), col(2), blk, pl.BlockSpec((nch, 1, LANES, LANES), lambda p, i: (nblk - 1 - i, p, 0, 0))],
        out_specs=[blk, blk, blk, blk], out_shape=[full, full, full, full],
        scratch_shapes=[pltpu.VMEM((LANES, LANES), F32)], compiler_params=_cparams("parallel", "arbitrary"))(do, proj, proj, proj, z, states)


def _gla_block_fwd(proj, w_lr_pad, b_lr, gain, bavg, pfx):
    z = _mm(proj, w_lr_pad, "nn", pfx + "_z", a_cols=(7 * MIX_HALF, MIX_HALF), epi=lambda acc, b: acc + b, extras=[(b_lr, "n")])
    o, states = _gla_fwd(proj, z, pfx + "_core")

    def out(ov, gg, ba, gn):
        n, _ = _head_rms(ov, ba)
        return n * gn * (gg * _sigmoid(gg))

    og, = _ew(out, pfx + "_out", [o, (proj, MIX_HALF, 3)], consts=[bavg, gain], outs=[(MIX_HALF, F32)])
    return og, dict(z=z, o=o, states=states)


def _gla_block_bwd(dog, proj, w_lr_pad, gain, bavg, res, pfx):
    z, o, states = res["z"], res["o"], res["states"]

    def out_bwd(dy, ov, gg, ba, gn):
        n, r = _head_rms(ov, ba)
        sg = _sigmoid(gg)
        silu = gg * sg
        dn = dy * gn * silu
        do = _head_rms_bwd(dn, n, r, ba)
        dgg = dy * n * gn * (sg * (1.0 + gg * (1.0 - sg)))
        return do, dgg, jnp.sum(dy * n * silu, axis=0, keepdims=True)

    do, dgg, dgain = _ew(out_bwd, pfx + "_out_bwd", [dog, o, (proj, MIX_HALF, 3)], consts=[bavg, gain],
                         outs=[(MIX_HALF, F32), (MIX_HALF, F32)], sums=[MIX_HALF])
    dq, dk, dv, dlc = _gla_bwd(do, proj, z, states, pfx + "_core_bwd")

    def decay_bwd(dl, zv):
        dz = dl * (1.0 / GLA_TAU) * (1.0 - _sigmoid(zv))
        return dz, jnp.sum(dz, axis=0, keepdims=True)

    dz, db_lr = _ew(decay_bwd, pfx + "_decay_bwd", [dlc, z], outs=[(MIX_HALF, F32)], sums=[MIX_HALF])
    dw_lr_pad = _mm(proj, dz, "tn", pfx + "_dwlr", a_cols=(7 * MIX_HALF, MIX_HALF))
    dsmall = _mm(dz, w_lr_pad, "nt", pfx + "_dsmall")
    return (dq, dk, dv, dgg, dsmall), dict(w_lr=dw_lr_pad[:GLA_RANK], b_lr=db_lr.reshape(-1), gain=dgain.reshape(-1, HEAD_DIM))


FOX_T = 512
FOX_HEADS = MIX_HALF // HEAD_DIM
NEG = -1e30
CUM_T = 512


def _cum_lanes(x, name, reverse, pre=None):
    R, S = x.shape
    nb = S // CUM_T

    def body(x_ref, o_ref, carry_ref):
        @pl.when(pl.program_id(0) == 0)
        def _():
            carry_ref[...] = jnp.zeros_like(carry_ref)

        xv = x_ref[...]
        if pre is not None:
            xv = pre(xv)
        i = lax.broadcasted_iota(jnp.int32, (CUM_T, CUM_T), 0)
        j = lax.broadcasted_iota(jnp.int32, (CUM_T, CUM_T), 1)
        tri = ((i >= j) if reverse else (i <= j)).astype(F32)
        c = lax.dot_general(xv, tri, (((1,), (0,)), ((), ())), precision=HI, preferred_element_type=F32)
        carry = carry_ref[...]
        o_ref[...] = c + carry[:, 0:1]
        carry_ref[...] = carry + jnp.sum(xv, axis=1, keepdims=True)

    spec = pl.BlockSpec((R, CUM_T), (lambda i: (0, nb - 1 - i)) if reverse else (lambda i: (0, i)))
    return pl.pallas_call(body, name=name, grid=(nb,), in_specs=[spec], out_specs=spec, out_shape=jax.ShapeDtypeStruct((R, S), F32),
                          scratch_shapes=[pltpu.VMEM((R, LANES), F32)], compiler_params=_cparams("arbitrary"))(x)


def _fox_scores(q, k, cqb, ck_ref, h, m, diag):
    cq = cqb[:, h * HEAD_DIM:h * HEAD_DIM + 1]
    ck = ck_ref[0, h:h + 1, :]
    s = lax.dot_general(q * m.astype(q.dtype), k, NT_DIMS, preferred_element_type=F32) + (cq - ck)
    if not diag:
        return s
    row = lax.broadcasted_iota(jnp.int32, (FOX_T, FOX_T), 0)
    col = lax.broadcasted_iota(jnp.int32, (FOX_T, FOX_T), 1)
    return jnp.where(row < col, NEG, s)


def _on_causal_blocks(q_blk, k_blk, step):
    @pl.when(k_blk < q_blk)
    def _():
        step(False)

    @pl.when(k_blk == q_blk)
    def _():
        step(True)


def _fox_fwd(qn, kn, proj, cum_b, cum_tp, name):
    S = qn.shape[0]
    nq = S // FOX_T

    def body(q_ref, k_ref, v_ref, cq_ref, ck_ref, o_ref, lse_ref, m_scr, l_scr, acc_scr):
        qi, ki = pl.program_id(1), pl.program_id(2)

        @pl.when(ki == 0)
        def _():
            m_scr[...] = jnp.full_like(m_scr, NEG)
            l_scr[...] = jnp.zeros_like(l_scr)
            acc_scr[...] = jnp.zeros_like(acc_scr)

        left, right = _half_masks(FOX_T)

        def step(diag):
            q, k, v = q_ref[...], k_ref[...], v_ref[...].astype(BF16)
            cqb = cq_ref[...]
            for h, m in enumerate((left, right)):
                s = _fox_scores(q, k, cqb, ck_ref, h, m, diag)
                m_prev = m_scr[h]
                m_new = jnp.maximum(m_prev, jnp.max(s, axis=1, keepdims=True))
                alpha = jnp.exp(m_prev - m_new)
                p = jnp.exp(s - m_new)
                l_scr[h] = alpha * l_scr[h] + jnp.sum(p, axis=1, keepdims=True)
                acc_scr[h] = alpha * acc_scr[h] + jnp.dot(p.astype(BF16), v, preferred_element_type=F32)
                m_scr[h] = m_new

        _on_causal_blocks(qi, ki, step)

        @pl.when(ki == qi)
        def _():
            o_ref[...] = left * (acc_scr[0] / l_scr[0]) + right * (acc_scr[1] / l_scr[1])
            lse_ref[...] = left * (m_scr[0] + jnp.log(l_scr[0])) + right * (m_scr[1] + jnp.log(l_scr[1]))

    qspec = pl.BlockSpec((FOX_T, LANES), lambda p, qi, ki: (qi, p))
    kspec = pl.BlockSpec((FOX_T, LANES), lambda p, qi, ki: (jnp.minimum(ki, qi), p))
    vspec = pl.BlockSpec((FOX_T, LANES), lambda p, qi, ki: (jnp.minimum(ki, qi), 6 * N_PAIRS + p))
    ckspec = pl.BlockSpec((1, 8, FOX_T), lambda p, qi, ki: (p, 0, jnp.minimum(ki, qi)))
    full = jax.ShapeDtypeStruct((S, MIX_HALF), F32)
    return pl.pallas_call(
        body, name=name, grid=(N_PAIRS, nq, nq), in_specs=[qspec, kspec, vspec, qspec, ckspec], out_specs=[qspec, qspec],
        out_shape=[full, full],
        scratch_shapes=[pltpu.VMEM((2, FOX_T, 1), F32), pltpu.VMEM((2, FOX_T, 1), F32), pltpu.VMEM((2, FOX_T, LANES), F32)],
        compiler_params=_cparams("parallel", "parallel", "arbitrary"))(qn, kn, proj, cum_b, cum_tp)


def _fox_bwd_dq(do, qn, kn, proj, cum_b, cum_tp, lse_b, delta_b, name):
    S = qn.shape[0]
    nq = S // FOX_T
    scale = HEAD_DIM ** -0.5

    def body(do_ref, q_ref, k_ref, v_ref, cq_ref, ck_ref, lse_ref, dl_ref, dq_ref, dcq_ref, acc_scr, rs_scr):
        qi, ki = pl.program_id(1), pl.program_id(2)

        @pl.when(ki == 0)
        def _():
            acc_scr[...] = jnp.zeros_like(acc_scr)
            rs_scr[...] = jnp.zeros_like(rs_scr)

        left, right = _half_masks(FOX_T)

        def step(diag):
            q, k, v, dov = q_ref[...], k_ref[...], v_ref[...].astype(BF16), do_ref[...]
            cqb, lseb, dlb = cq_ref[...], lse_ref[...], dl_ref[...]
            acc = acc_scr[...]
            for h, m in enumerate((left, right)):
                s = _fox_scores(q, k, cqb, ck_ref, h, m, diag)
                p = jnp.exp(s - lseb[:, h * HEAD_DIM:h * HEAD_DIM + 1])
                dp = lax.dot_general((dov * m).astype(BF16), v, NT_DIMS, preferred_element_type=F32)
                ds = p * (dp - dlb[:, h * HEAD_DIM:h * HEAD_DIM + 1])
                acc = acc + m * jnp.dot(ds.astype(BF16), k, preferred_element_type=F32)
                rs_scr[h] = rs_scr[h] + jnp.sum(ds, axis=1, keepdims=True)
            acc_scr[...] = acc

        _on_causal_blocks(qi, ki, step)

        @pl.when(ki == qi)
        def _():
            dq_ref[...] = acc_scr[...] * scale
            dcq_ref[...] = left * rs_scr[0] + right * rs_scr[1]

    qspec = pl.BlockSpec((FOX_T, LANES), lambda p, qi, ki: (qi, p))
    kspec = pl.BlockSpec((FOX_T, LANES), lambda p, qi, ki: (jnp.minimum(ki, qi), p))
    vspec = pl.BlockSpec((FOX_T, LANES), lambda p, qi, ki: (jnp.minimum(ki, qi), 6 * N_PAIRS + p))
    ckspec = pl.BlockSpec((1, 8, FOX_T), lambda p, qi, ki: (p, 0, jnp.minimum(ki, qi)))
    return pl.pallas_call(
        body, name=name, grid=(N_PAIRS, nq, nq), in_specs=[qspec, qspec, kspec, vspec, qspec, ckspec, qspec, qspec],
        out_specs=[qspec, qspec], out_shape=[jax.ShapeDtypeStruct((S, MIX_HALF), F32)] * 2,
        scratch_shapes=[pltpu.VMEM((FOX_T, LANES), F32), pltpu.VMEM((2, FOX_T, 1), F32)],
        compiler_params=_cparams("parallel", "parallel", "arbitrary"))(do, qn, kn, proj, cum_b, cum_tp, lse_b, delta_b)


def _fox_bwd_dkv(do, qn, kn, proj, cum_b, cum_tp, lse_b, delta_b, name):
    S = qn.shape[0]
    nq = S // FOX_T
    scale = HEAD_DIM ** -0.5

    def body(do_ref, q_ref, k_ref, v_ref, cq_ref, ck_ref, lse_ref, dl_ref, dk_ref, dv_ref, dck_ref, dk_scr, dv_scr, dck_scr):
        ki, qi = pl.program_id(1), pl.program_id(2)

        @pl.when(qi == 0)
        def _():
            dk_scr[...] = jnp.zeros_like(dk_scr)
            dv_scr[...] = jnp.zeros_like(dv_scr)
            dck_scr[...] = jnp.zeros_like(dck_scr)

        left, right = _half_masks(FOX_T)

        def step(diag):
            q, k, v, dov = q_ref[...], k_ref[...], v_ref[...].astype(BF16), do_ref[...]
            cqb, lseb, dlb = cq_ref[...], lse_ref[...], dl_ref[...]
            dob = dov.astype(BF16)
            dk, dv = dk_scr[...], dv_scr[...]
            for h, m in enumerate((left, right)):
                s = _fox_scores(q, k, cqb, ck_ref, h, m, diag)
                p = jnp.exp(s - lseb[:, h * HEAD_DIM:h * HEAD_DIM + 1])
                dp = lax.dot_general((dov * m).astype(BF16), v, NT_DIMS, preferred_element_type=F32)
                ds = p * (dp - dlb[:, h * HEAD_DIM:h * HEAD_DIM + 1])
                dv = dv + m * lax.dot_general(p.astype(BF16), dob, TN_DIMS, preferred_element_type=F32)
                dk = dk + m * lax.dot_general(ds.astype(BF16), q, TN_DIMS, preferred_element_type=F32)
                dck_scr[h:h + 1, :] = dck_scr[h:h + 1, :] - jnp.sum(ds, axis=0, keepdims=True)
            dk_scr[...] = dk
            dv_scr[...] = dv

        _on_causal_blocks(qi, ki, step)

        @pl.when(qi == nq - 1)
        def _():
            dk_ref[...] = dk_scr[...]
            dv_ref[...] = dv_scr[...]
            dck_ref[0] = dck_scr[...]

    qspec = pl.BlockSpec((FOX_T, LANES), lambda p, ki, qi: (jnp.maximum(qi, ki), p))
    kspec = pl.BlockSpec((FOX_T, LANES), lambda p, ki, qi: (ki, p))
    vspec = pl.BlockSpec((FOX_T, LANES), lambda p, ki, qi: (ki, 6 * N_PAIRS + p))
    ckspec = pl.BlockSpec((1, 8, FOX_T), lambda p, ki, qi: (p, 0, ki))
    full = jax.ShapeDtypeStruct((S, MIX_HALF), F32)
    return pl.pallas_call(
        body, name=name, grid=(N_PAIRS, nq, nq), in_specs=[qspec, qspec, kspec, vspec, qspec, ckspec, qspec, qspec],
        out_specs=[kspec, kspec, ckspec], out_shape=[full, full, jax.ShapeDtypeStruct((N_PAIRS, 8, S), F32)],
        scratch_shapes=[pltpu.VMEM((FOX_T, LANES), F32), pltpu.VMEM((FOX_T, LANES), F32), pltpu.VMEM((8, FOX_T), F32)],
        compiler_params=_cparams("parallel", "parallel", "arbitrary"))(do, qn, kn, proj, cum_b, cum_tp, lse_b, delta_b)


def _ff_bwd(rc, f_t, name):
    def body(rc_ref, f_ref, d_ref, s_ref):
        d = rc_ref[...] * (1.0 - _sigmoid(f_ref[...]))
        d_ref[...] = d
        s_ref[...] = jnp.sum(d, axis=1, keepdims=True)

    return pl.pallas_call(body, name=name, out_shape=[jax.ShapeDtypeStruct(rc.shape, F32), jax.ShapeDtypeStruct((rc.shape[0], 1), F32)])(rc, f_t)


def _fox_block_fwd(proj, b_f, q_gain, k_gain, bavg, pfx):
    S = proj.shape[0]

    def prep(qv, kv, ba, qg, kg):
        return _head_rms(qv, ba)[0] * qg * (HEAD_DIM ** -0.5), _head_rms(kv, ba)[0] * kg

    qn, kn = _ew(prep, pfx + "_prep", [(proj, MIX_HALF, 4), (proj, MIX_HALF, 5)], consts=[bavg, q_gain, k_gain],
                 outs=[(MIX_HALF, BF16), (MIX_HALF, BF16)])
    f0 = 7 * MIX_HALF + GLA_RANK
    f_t = proj[:, f0:f0 + FOX_HEADS].T + b_f.reshape(FOX_HEADS, 1)
    cum = _cum_lanes(f_t, pfx + "_cum", False, pre=_log_sigmoid)
    cum_b = jnp.repeat(cum.T, HEAD_DIM, axis=1)
    cum_tp = jnp.pad(cum.reshape(N_PAIRS, 2, S), ((0, 0), (0, 6), (0, 0)))
    o, lse_b = _fox_fwd(qn, kn, proj, cum_b, cum_tp, pfx + "_attn")
    return o, dict(qn=qn, kn=kn, f_t=f_t, cum_b=cum_b, cum_tp=cum_tp, o=o, lse_b=lse_b)


def _fox_block_bwd(do, proj, q_gain, k_gain, bavg, res, pfx):
    qn, kn, o = res["qn"], res["kn"], res["o"]
    S = proj.shape[0]
    delta_b, = _ew(lambda a, b, ba: _head_mean(a * b, ba) * float(HEAD_DIM), pfx + "_delta", [do, o], consts=[bavg], outs=[(MIX_HALF, F32)])
    args = (do, qn, kn, proj, res["cum_b"], res["cum_tp"], res["lse_b"], delta_b)
    dqn, dcq_b = _fox_bwd_dq(*args, pfx + "_dq")
    dkn, dv, dck = _fox_bwd_dkv(*args, pfx + "_dkv")

    def prep_bwd(dq, dk, qv, kv, ba, qg, kg):
        nq, rq = _head_rms(qv, ba)
        nk, rk = _head_rms(kv, ba)
        return (_head_rms_bwd(dq * qg, nq, rq, ba), _head_rms_bwd(dk * kg, nk, rk, ba),
                jnp.sum(dq * nq, axis=0, keepdims=True), jnp.sum(dk * nk, axis=0, keepdims=True))

    dfq, dfk, dqg, dkg = _ew(prep_bwd, pfx + "_prep_bwd", [dqn, dkn, (proj, MIX_HALF, 4), (proj, MIX_HALF, 5)],
                             consts=[bavg, q_gain, k_gain], outs=[(MIX_HALF, F32), (MIX_HALF, F32)], sums=[MIX_HALF, MIX_HALF])
    dcum = dck[:, :2, :].reshape(FOX_HEADS, S) + dcq_b[:, ::HEAD_DIM].T
    rc = _cum_lanes(dcum, pfx + "_rcum", True)
    dff_t, db_f = _ff_bwd(rc, res["f_t"], pfx + "_ff_bwd")
    grads = dict(b_f=db_f.reshape(-1), q_gain=dqg.reshape(-1, HEAD_DIM), k_gain=dkg.reshape(-1, HEAD_DIM))
    return (dfq, dfk, dv, dff_t.T), grads


WEIGHTS = ['ada_w', 'ada_b', 'even_w_in', 'even_w_out', 'gla_w_lr', 'gla_b_lr', 'gla_gain', 'fox_b_f', 'fox_q_gain', 'fox_k_gain',
           'odd_w_in', 'odd_w_out', 's5_lam_re', 's5_lam_im', 's5_log_dt', 's5_b_re', 's5_b_im', 's5_c_re', 's5_c_im', 's5_d',
           's5_w_glu', 's5_b_glu', 'sgu_ln_gain', 'sgu_ln_bias', 'sgu_w_s', 'sgu_b_s', 'mlp_w1', 'mlp_w2']
ARGS = ['x', 'c'] + WEIGHTS + ['loss_target'] + ['m_' + w for w in WEIGHTS] + ['v_' + w for w in WEIGHTS]

EVEN_COLS = 3608
EVEN_PAD = 8 * MIX_HALF
MOD = 6 * D_MODEL
MOD_SHARD = MOD // N_CHIPS

SHARDED = [("even_w_in", (1, 1024, 902), 2), ("even_w_out", (1, 256, 1024), 1), ("odd_w_in", (1, 1024, 384), 2),
           ("odd_w_out", (1, 256, 1024), 1), ("mlp_w1", (2, 1024, 1024), 2), ("mlp_w2", (2, 1024, 1024), 1),
           ("gla_w_lr", (1, 16, 128), 2), ("s5_w_glu", (1, 128, 512), 1), ("s5_b_glu", (1, 128), 1),
           ("sgu_ln_gain", (1, 128), 1), ("sgu_ln_bias", (1, 128), 1)]
PACK_COLS = 512
PACK_ROWS = 12288
REPLICATED = [("gla_b_lr", (1, 512)), ("gla_gain", (1, 8, 64)), ("fox_b_f", (1, 8)), ("fox_q_gain", (1, 8, 64)),
              ("fox_k_gain", (1, 8, 64)), ("s5_lam_re", (1, 32, 64)), ("s5_lam_im", (1, 32, 64)), ("s5_log_dt", (1, 32)),
              ("s5_b_re", (1, 32, 64, 16)), ("s5_b_im", (1, 32, 64, 16)), ("s5_c_re", (1, 32, 16, 64)), ("s5_c_im", (1, 32, 16, 64)),
              ("s5_d", (1, 32, 16)), ("sgu_w_s", (1, 8, 128, 128)), ("sgu_b_s", (1, 8, 128))]
SMALL_ROWS = 768
BIG_ADAM = {"ada_w": (2048, 1536), "even_w_in": (1024, 902), "even_w_out": (256, 1024), "odd_w_in": (1024, 384),
            "odd_w_out": (256, 1024), "mlp_w1": (2048, 1024), "mlp_w2": (2048, 1024), "s5_w_glu": (128, 512)}


def _pack_rows(pieces, rows):
    flat = jnp.concatenate([p.reshape(-1) for p in pieces])
    return jnp.pad(flat, (0, rows * PACK_COLS - flat.shape[0])).reshape(rows, PACK_COLS)


def _unpack(flat, specs):
    out, off = {}, 0
    for name, shape in specs:
        n = math.prod(shape)
        out[name] = flat[off:off + n].reshape(shape)
        off += n
    return out


def _shards_to_full(flat4):
    out, off = {}, 0
    for name, shape, axis in SHARDED:
        n = math.prod(shape)
        seg = flat4[:, off:off + n].reshape((N_CHIPS,) + shape)
        out[name] = jnp.concatenate([seg[k] for k in range(N_CHIPS)], axis=axis)
        off += n
    return out


def _full_to_shards(full):
    cols = []
    for name, shape, axis in SHARDED:
        parts = jnp.split(full[name], N_CHIPS, axis=axis)
        cols.append(jnp.stack([p.reshape(-1) for p in parts]))
    flat = jnp.concatenate(cols, axis=1)
    return jnp.pad(flat, ((0, 0), (0, PACK_ROWS * PACK_COLS - flat.shape[1]))).reshape(N_CHIPS, PACK_ROWS, PACK_COLS)


def _relu2(t):
    r = jnp.maximum(t, 0.0)
    return r * r


def _silu(t):
    return t * _sigmoid(t)


def _pack_even(w):
    return jnp.concatenate([w[:, :2048], w[:, 2064:3600], w[:, 2048:2064], w[:, 3600:3608],
                            jnp.zeros((w.shape[0], EVEN_PAD - EVEN_COLS), w.dtype)], axis=1)


def _unpack_even(wp):
    return jnp.concatenate([wp[:, :2048], wp[:, 3584:3600], wp[:, 2048:3584], wp[:, 3600:3608]], axis=1)


def _mlp_fwd(h, w1, w2, pfx):
    pre = _mm(h, w1, "nn", pfx + "_up")
    return pre, _mm(pre, w2, "nn", pfx + "_down", a_pro=_relu2)


def _mlp_bwd(dm, h, pre, w1, w2, pfx):
    dpre = _mm(dm, w2, "nt", pfx + "_dpre", epi=lambda acc, p: acc * (2.0 * jnp.maximum(p, 0.0)), extras=[(pre, "mn")])
    dw2 = _mm(pre, dm, "tn", pfx + "_dw2", a_pro=_relu2)
    dw1 = _mm(h, dpre, "tn", pfx + "_dw1")
    dh = _mm(dpre, w1, "nt", pfx + "_dh")
    return dh, dw1, dw2


def _step(args):
    a = dict(zip(ARGS, args, strict=True))
    x0 = a["x"][0]
    target = a["loss_target"][0]
    mx, my, mc = lax.axis_index("x"), lax.axis_index("y"), lax.axis_index("c")
    chip = 2 * mx + my
    dev = 2 * chip + mc
    bavg = _head_avg_matrix(MIX_HALF)

    c_all = _gather8(jnp.pad(a["c"], ((0, 7), (0, 0))), "c_gather")[:, :, 0, :].reshape(2 * N_CHIPS, D_MODEL)
    ada_b_shard = lax.dynamic_slice_in_dim(a["ada_b"], chip * MOD_SHARD, MOD_SHARD, axis=1)
    mod_sh = [_mm(c_all, a["ada_w"][l], "nn", f"mod{l}", a_pro=_silu, epi=lambda acc, b: acc + b, extras=[(ada_b_shard[l:l + 1], "n")])
              for l in range(2)]
    small3 = jnp.zeros((8, MOD_SHARD), F32)
    for r, n in enumerate(("s5_b_glu", "sgu_ln_gain", "sgu_ln_bias")):
        small3 = small3.at[r, :LANES].set(a[n][0])
    mod_all = _chip_exchange(jnp.concatenate(mod_sh + [small3]), "mod_gather", True)
    mods = []
    for l in range(2):
        full = mod_all[:, 8 * l:8 * l + 8].transpose(1, 0, 2).reshape(8, MOD)
        mods.append(jnp.split(lax.dynamic_slice_in_dim(full, dev, 1, axis=0), 6, axis=1))
    b_glu, ln_gain, ln_bias = (mod_all[:, 16 + r, :LANES].reshape(1, MIX_HALF) for r in range(3))

    shard = _pack_rows([a[n] for n, _, _ in SHARDED], PACK_ROWS).astype(BF16)
    half = lax.dynamic_slice_in_dim(shard, mc * (PACK_ROWS // 2), PACK_ROWS // 2, axis=0)
    collected = _chip_exchange(half, "w_chips", True)
    halves = _by_core(collected, _pair_swap(collected, "w_pair"))
    w = _shards_to_full(halves.transpose(1, 0, 2, 3).reshape(N_CHIPS, PACK_ROWS * PACK_COLS))
    w_even = _pack_even(w["even_w_in"][0])
    w_lr_pad = jnp.zeros((MIX_HALF, MIX_HALF), BF16).at[:GLA_RANK].set(w["gla_w_lr"][0])
    gla_b_lr = a["gla_b_lr"]
    gla_gain, q_gain, k_gain = (a[n].reshape(1, MIX_HALF) for n in ("gla_gain", "fox_q_gain", "fox_k_gain"))
    s5w = dict(lam_re=a["s5_lam_re"][0], lam_im=a["s5_lam_im"][0], log_dt=a["s5_log_dt"][0], b_re=a["s5_b_re"][0], b_im=a["s5_b_im"][0],
               c_re=a["s5_c_re"][0], c_im=a["s5_c_im"][0], d=a["s5_d"][0], w_glu=w["s5_w_glu"][0], b_glu=b_glu)
    sgu_wm, sgu_bt = _sgu_tables(a["sgu_w_s"][0], a["sgu_b_s"][0])

    sh1, sc1, g1, sh2, sc2, g2 = mods[0]
    _, h1_0 = _res_rms(x0, sc1, sh1, "l0_norm1")
    proj0 = _mm(h1_0, w_even, "nn", "l0_proj")
    og, gla_res = _gla_block_fwd(proj0, w_lr_pad, gla_b_lr, gla_gain, bavg, "gla")
    of, fox_res = _fox_block_fwd(proj0, a["fox_b_f"][0], q_gain, k_gain, bavg, "fox")
    mixed0 = jnp.concatenate([og, of], axis=1)
    y0 = _mm(mixed0, w["even_w_out"][0], "nn", "l0_out")
    x1, h2_0 = _res_rms(x0, sc2, sh2, "l0_norm2", y=y0, g=g1)
    pre0, m0 = _mlp_fwd(h2_0, w["mlp_w1"][0], w["mlp_w2"][0], "l0_mlp")
    sh1b, sc1b, g1b, sh2b, sc2b, g2b = mods[1]
    x2, h1_1 = _res_rms(x1, sc1b, sh1b, "l1_norm1", y=m0, g=g2)
    proj1 = _mm(h1_1, w["odd_w_in"][0], "nn", "l1_proj")
    ys5, s5_res = _s5_block_fwd(proj1[:, :MIX_HALF], s5w, "s5")
    ysgu = _sgu_fwd(proj1, ln_gain, ln_bias, sgu_wm, sgu_bt, "sgu")
    mixed1 = jnp.concatenate([ys5, ysgu], axis=1)
    y1 = _mm(mixed1, w["odd_w_out"][0], "nn", "l1_out")
    x3, h2_1 = _res_rms(x2, sc2b, sh2b, "l1_norm2", y=y1, g=g1b)
    pre1, m1 = _mlp_fwd(h2_1, w["mlp_w1"][1], w["mlp_w2"][1], "l1_mlp")
    loss_b, dx4, dm1, dg2b = _res_loss(x3, m1, g2b, target, "loss")
    loss = lax.psum(loss_b[0, 0], ("x", "y", "c"))

    full = {}
    dh2_1, dw1_1, dw2_1 = _mlp_bwd(dm1, h2_1, pre1, w["mlp_w1"][1], w["mlp_w2"][1], "l1_mlp")
    dx3, dy1, dg1b, dsc2b, dsh2b = _res_rms_bwd(x3, dh2_1, sc2b, dx4, "l1_norm2_bwd", y=y1, g=g1b)
    dmixed1 = _mm(dy1, w["odd_w_out"][0], "nt", "l1_out_dx")
    full["odd_w_out"] = _mm(mixed1, dy1, "tn", "l1_out_dw")[None]
    du, s5g = _s5_block_bwd(dmixed1[:, :MIX_HALF], s5w, s5_res, "s5")
    dzu, dzv, dws, dbt, dlg, dlb = _sgu_bwd(dmixed1[:, MIX_HALF:], proj1, ln_gain, ln_bias, sgu_wm, sgu_bt, "sgu_bwd")
    g_ws, g_bs = _sgu_grads(dws, dbt)
    dproj1 = jnp.concatenate([du, dzu, dzv], axis=1)
    full["odd_w_in"] = _mm(h1_1, dproj1, "tn", "l1_proj_dw")[None]
    dh1_1 = _mm(dproj1, w["odd_w_in"][0], "nt", "l1_proj_dx")
    dx2, dm0, dg2, dsc1b, dsh1b = _res_rms_bwd(x2, dh1_1, sc1b, dx3, "l1_norm1_bwd", y=m0, g=g2)
    dh2_0, dw1_0, dw2_0 = _mlp_bwd(dm0, h2_0, pre0, w["mlp_w1"][0], w["mlp_w2"][0], "l0_mlp")
    full["mlp_w1"] = jnp.stack([dw1_0, dw1_1])
    full["mlp_w2"] = jnp.stack([dw2_0, dw2_1])
    dx1, dy0, dg1, dsc2, dsh2 = _res_rms_bwd(x1, dh2_0, sc2, dx2, "l0_norm2_bwd", y=y0, g=g1)
    dmixed0 = _mm(dy0, w["even_w_out"][0], "nt", "l0_out_dx")
    full["even_w_out"] = _mm(mixed0, dy0, "tn", "l0_out_dw")[None]
    (dgq, dgk, dgv, dgg, dsmall), glag = _gla_block_bwd(dmixed0[:, :MIX_HALF], proj0, w_lr_pad, gla_gain, bavg, gla_res, "gla")
    (dfq, dfk, dfv, dff), foxg = _fox_block_bwd(dmixed0[:, MIX_HALF:], proj0, q_gain, k_gain, bavg, fox_res, "fox")
    dsmall = lax.dynamic_update_slice(dsmall, dff, (0, GLA_RANK))
    dproj0 = jnp.concatenate([dgq, dgk, dgv, dgg, dfq, dfk, dfv, dsmall], axis=1)
    full["even_w_in"] = _unpack_even(_mm(h1_0, dproj0, "tn", "l0_proj_dw"))[None]
    dh1_0 = _mm(dproj0, w_even, "nt", "l0_proj_dx")
    grad_x, dsc1, dsh1 = _res_rms_bwd(x0, dh1_0, sc1, dx1, "l0_norm1_bwd")
    full["gla_w_lr"] = glag["w_lr"][None]
    full["s5_w_glu"] = s5g["w_glu"][None]
    full["s5_b_glu"] = s5g["b_glu"][None]
    full["sgu_ln_gain"] = dlg
    full["sgu_ln_bias"] = dlb

    dmod = jnp.concatenate([dsh1, dsc1, dg1, dsh2, dsc2, dg2, dsh1b, dsc1b, dg1b, dsh2b, dsc2b, dg2b], axis=1)
    dmod_all = _gather8(jnp.pad(dmod, ((0, 7), (0, 0))), "dmod_gather")[:, :, 0, :].reshape(2 * N_CHIPS, 2, MOD)
    grads = {}
    grads["ada_w"] = jnp.stack([
        _mm(c_all, lax.dynamic_slice_in_dim(dmod_all[:, l], chip * MOD_SHARD, MOD_SHARD, axis=1), "tn", f"ada_dw{l}", a_pro=_silu)
        for l in range(2)])
    grads["ada_b"] = _sum_slots(dmod_all.reshape(2 * N_CHIPS, 2 * MOD // PACK_COLS, PACK_COLS), "ada_db").reshape(2, MOD)

    packed = _full_to_shards(full)
    hr = PACK_ROWS // 2
    mine = lax.dynamic_slice_in_dim(packed, mc * hr, hr, axis=1)
    other = lax.dynamic_slice_in_dim(packed, (1 - mc) * hr, hr, axis=1)
    theirs = _pair_swap(other, "g_pair")
    pair_sum, = _ew(lambda p, q: p + q, "g_pair_sum", [mine.reshape(N_CHIPS * hr, PACK_COLS), theirs.reshape(N_CHIPS * hr, PACK_COLS)],
                    outs=[(PACK_COLS, F32)])
    arrived = _chip_exchange(pair_sum.reshape(N_CHIPS, hr, PACK_COLS), "g_chips", False)
    red_half = _sum_slots(arrived, "g_chip_sum")
    reduced = _by_core(red_half, _pair_swap(red_half, "g_pair_out")).reshape(-1)
    grads.update(_unpack(reduced, [(n, s) for n, s, _ in SHARDED]))

    part = dict(gla_b_lr=glag["b_lr"], gla_gain=glag["gain"], fox_b_f=foxg["b_f"], fox_q_gain=foxg["q_gain"], fox_k_gain=foxg["k_gain"],
                s5_lam_re=s5g["lam_re"], s5_lam_im=s5g["lam_im"], s5_log_dt=s5g["log_dt"], s5_b_re=s5g["b_re"], s5_b_im=s5g["b_im"],
                s5_c_re=s5g["c_re"], s5_c_im=s5g["c_im"], s5_d=s5g["d"], sgu_w_s=g_ws, sgu_b_s=g_bs)
    parts_all = _gather8(_pack_rows([part[n] for n, _ in REPLICATED], SMALL_ROWS), "rep_gather")
    rep = _sum_slots(parts_all.reshape(2 * N_CHIPS, SMALL_ROWS, PACK_COLS), "rep_sum").reshape(-1)
    grads.update(_unpack(rep, REPLICATED))

    delta, new_m, new_v = {}, {}, {}
    for n, shape2 in BIG_ADAM.items():
        d, nm, nv = _adamw(a[n].reshape(shape2), grads[n].reshape(shape2), a["m_" + n].reshape(shape2), a["v_" + n].reshape(shape2), "adamw_" + n)
        delta[n], new_m[n], new_v[n] = (t.reshape(a[n].shape) for t in (d, nm, nv))
    small = [n for n in WEIGHTS if n not in BIG_ADAM]
    spec = [(n, a[n].shape) for n in small]
    packs = [_pack_rows([src[n] for n in small], SMALL_ROWS) for src in
             (a, grads, {n: a["m_" + n] for n in small}, {n: a["v_" + n] for n in small})]
    for tgt, res in zip((delta, new_m, new_v), _adamw(*packs, "adamw_small")):
        tgt.update(_unpack(res.reshape(-1), spec))
    outs = [loss, grad_x[None]]
    for group in (grads, delta, new_m, new_v):
        outs += [group[n].reshape(a[n].shape) for n in WEIGHTS]
    return tuple(outs)


def kernel(x, c, ada_w, ada_b, even_w_in, even_w_out, gla_w_lr, gla_b_lr, gla_gain, fox_b_f, fox_q_gain, fox_k_gain, odd_w_in,
           odd_w_out, s5_lam_re, s5_lam_im, s5_log_dt, s5_b_re, s5_b_im, s5_c_re, s5_c_im, s5_d, s5_w_glu, s5_b_glu, sgu_ln_gain,
           sgu_ln_bias, sgu_w_s, sgu_b_s, mlp_w1, mlp_w2, loss_target, m_ada_w, m_ada_b, m_even_w_in, m_even_w_out, m_gla_w_lr,
           m_gla_b_lr, m_gla_gain, m_fox_b_f, m_fox_q_gain, m_fox_k_gain, m_odd_w_in, m_odd_w_out, m_s5_lam_re, m_s5_lam_im,
           m_s5_log_dt, m_s5_b_re, m_s5_b_im, m_s5_c_re, m_s5_c_im, m_s5_d, m_s5_w_glu, m_s5_b_glu, m_sgu_ln_gain, m_sgu_ln_bias,
           m_sgu_w_s, m_sgu_b_s, m_mlp_w1, m_mlp_w2, v_ada_w, v_ada_b, v_even_w_in, v_even_w_out, v_gla_w_lr, v_gla_b_lr,
           v_gla_gain, v_fox_b_f, v_fox_q_gain, v_fox_k_gain, v_odd_w_in, v_odd_w_out, v_s5_lam_re, v_s5_lam_im, v_s5_log_dt,
           v_s5_b_re, v_s5_b_im, v_s5_c_re, v_s5_c_im, v_s5_d, v_s5_w_glu, v_s5_b_glu, v_sgu_ln_gain, v_sgu_ln_bias, v_sgu_w_s,
           v_sgu_b_s, v_mlp_w1, v_mlp_w2):
    return _step((x, c, ada_w, ada_b, even_w_in, even_w_out, gla_w_lr, gla_b_lr, gla_gain, fox_b_f, fox_q_gain, fox_k_gain,
                  odd_w_in, odd_w_out, s5_lam_re, s5_lam_im, s5_log_dt, s5_b_re, s5_b_im, s5_c_re, s5_c_im, s5_d, s5_w_glu,
                  s5_b_glu, sgu_ln_gain, sgu_ln_bias, sgu_w_s, sgu_b_s, mlp_w1, mlp_w2, loss_target, m_ada_w, m_ada_b,
                  m_even_w_in, m_even_w_out, m_gla_w_lr, m_gla_b_lr, m_gla_gain, m_fox_b_f, m_fox_q_gain, m_fox_k_gain,
                  m_odd_w_in, m_odd_w_out, m_s5_lam_re, m_s5_lam_im, m_s5_log_dt, m_s5_b_re, m_s5_b_im, m_s5_c_re, m_s5_c_im,
                  m_s5_d, m_s5_w_glu, m_s5_b_glu, m_sgu_ln_gain, m_sgu_ln_bias, m_sgu_w_s, m_sgu_b_s, m_mlp_w1, m_mlp_w2, v_ada_w,
                  v_ada_b, v_even_w_in, v_even_w_out, v_gla_w_lr, v_gla_b_lr, v_gla_gain, v_fox_b_f, v_fox_q_gain, v_fox_k_gain,
                  v_odd_w_in, v_odd_w_out, v_s5_lam_re, v_s5_lam_im, v_s5_log_dt, v_s5_b_re, v_s5_b_im, v_s5_c_re, v_s5_c_im,
                  v_s5_d, v_s5_w_glu, v_s5_b_glu, v_sgu_ln_gain, v_sgu_ln_bias, v_sgu_w_s, v_sgu_b_s, v_mlp_w1, v_mlp_w2))
```

```python
import functools
import math

import jax
import jax.numpy as jnp
import numpy as np
from jax import lax
from jax.experimental import pallas as pl
from jax.experimental.pallas import tpu as pltpu

F32 = jnp.float32
BF16 = jnp.bfloat16
MESH = pl.DeviceIdType.MESH
ANY = pl.BlockSpec(memory_space=pl.ANY)
DMA_SEM = pltpu.SemaphoreType.DMA

D_MODEL = 1024
HEAD_DIM = 64
MIX_HALF = 512
GLA_RANK = 16
GLA_TAU = 16.0
GLA_CHUNK = 64
S5_GROUPS = 32
S5_GROUP_WIDTH = 16
S5_STATE = 64
S5_N = S5_GROUPS * S5_STATE
SGU_GROUPS = 8
SGU_CHUNK = 128
D_FF = 4096
EPS = 1e-6
N_CHIPS = 4
LANES = 128
VMEM_LIMIT = 48 * 1024 * 1024
PAIR_COPIES = 16

ADAM_LR = 0.001
ADAM_B1 = 0.9
ADAM_B2 = 0.999
ADAM_EPS = 1e-08
ADAM_WD = 0.01
ADAM_STEP = 10


def _cparams(*sem):
    return pltpu.CompilerParams(dimension_semantics=sem, vmem_limit_bytes=VMEM_LIMIT)


def _pair_swap(x, name):
    lead = x.shape[:-2]
    rows = x.shape[-2]
    nsplit = max(1, PAIR_COPIES // max(1, math.prod(lead)))
    while nsplit > 1 and rows % (nsplit * 16):
        nsplit -= 1
    pieces = [idx + (pl.ds(j * (rows // nsplit), rows // nsplit),) for idx in np.ndindex(*lead) for j in range(nsplit)]

    def body(x_ref, o_ref, send_sems, recv_sems):
        mx, my, mc = lax.axis_index("x"), lax.axis_index("y"), lax.axis_index("c")
        copies = [pltpu.make_async_remote_copy(src_ref=x_ref.at[p], dst_ref=o_ref.at[p], send_sem=send_sems.at[j], recv_sem=recv_sems.at[j],
                                               device_id=(mx, my, 1 - mc), device_id_type=MESH) for j, p in enumerate(pieces)]
        for cp in copies:
            cp.start()
        for cp in copies:
            cp.wait_recv()
        for cp in copies:
            cp.wait_send()

    return pl.pallas_call(
        body, name=name, out_shape=jax.ShapeDtypeStruct(x.shape, x.dtype), in_specs=[ANY], out_specs=ANY,
        scratch_shapes=[DMA_SEM((len(pieces),)), DMA_SEM((len(pieces),))])(x)


def _by_core(mine, theirs):
    first = lax.axis_index("c") == 0
    return jnp.stack([jnp.where(first, mine, theirs), jnp.where(first, theirs, mine)])


def _chip_exchange(x, name, bcast):
    blk = x.shape if bcast else x.shape[1:]

    def body(x_ref, o_ref, send_sems, recv_sems, loc_sem):
        mx, my, mc = lax.axis_index("x"), lax.axis_index("y"), lax.axis_index("c")
        me = 2 * mx + my
        peers = [(1 - mx, my), (mx, 1 - my), (1 - mx, 1 - my)]

        def src(k):
            return x_ref if bcast else x_ref.at[k]

        loc = pltpu.make_async_copy(src(me), o_ref.at[me], loc_sem)
        loc.start()
        sends = []
        for j, (px, py) in enumerate(peers):
            cp = pltpu.make_async_remote_copy(src_ref=src(2 * px + py), dst_ref=o_ref.at[me], send_sem=send_sems.at[j],
                                              recv_sem=recv_sems.at[j], device_id=(px, py, mc), device_id_type=MESH)
            cp.start()
            sends.append(cp)
        for j, (px, py) in enumerate(peers):
            pltpu.make_async_remote_copy(src_ref=src(me), dst_ref=o_ref.at[2 * px + py], send_sem=send_sems.at[j],
                                         recv_sem=recv_sems.at[j], device_id=(px, py, mc), device_id_type=MESH).wait_recv()
        for cp in sends:
            cp.wait_send()
        loc.wait()

    return pl.pallas_call(
        body, name=name, out_shape=jax.ShapeDtypeStruct((N_CHIPS,) + tuple(blk), x.dtype), in_specs=[ANY], out_specs=ANY,
        scratch_shapes=[DMA_SEM((3,)), DMA_SEM((3,)), DMA_SEM])(x)


def _gather8(x, name):
    return _chip_exchange(_by_core(x, _pair_swap(x, name + "_pair")), name + "_chips", True)


def _tile(n, want):
    if n <= want:
        return n
    t = (want // LANES) * LANES
    while t >= LANES:
        if n % t == 0:
            return t
        t -= LANES
    raise ValueError(f"no lane-aligned tile for {n}")


_DIMS = {"nn": (((1,), (0,)), ((), ())), "nt": (((1,), (1,)), ((), ())), "tn": (((0,), (0,)), ((), ()))}


MM_FULL_K = 4096
MM_SLAB_K = 2048
MM_TILES = ((1024, 1024), (512, 1024), (1024, 512), (512, 512), (256, 512), (256, 256))
MM_VMEM_BUDGET = 36 * 1024 * 1024


def _mm(a, b, mode, name, *, a_pro=None, epi=None, extras=(), out_dtype=F32, tm_max=1024, tn_max=1024, tk=None, a_cols=None):
    c0, csize = a_cols if a_cols is not None else (0, a.shape[1])
    if mode == "tn":
        K, M = a.shape[0], csize
    else:
        M, K = a.shape[0], csize
    N = b.shape[0] if mode == "nt" else b.shape[1]
    assert (b.shape[1] if mode == "nt" else b.shape[0]) == K, (a.shape, b.shape, mode)
    if tk is None:
        tk = K if (mode != "tn" and K <= MM_FULL_K) else MM_SLAB_K
    tk = _tile(K, tk)
    nk = K // tk
    n_mn = sum(1 for _, kind in extras if kind == "mn")
    for tm_want, tn_want in MM_TILES:
        tm, tn = _tile(M, min(tm_want, tm_max)), _tile(N, min(tn_want, tn_max))
        need = 2 * (tm * tk * a.dtype.itemsize + tk * tn * b.dtype.itemsize + tm * tn * 4 * (1 + n_mn)) + tm * tn * 4 * (nk > 1)
        if need <= MM_VMEM_BUDGET:
            break
    if mode == "tn":
        assert c0 % tm == 0
        a_spec = pl.BlockSpec((tk, tm), lambda i, j, k: (k, i + c0 // tm))
    else:
        assert c0 % tk == 0
        a_spec = pl.BlockSpec((tm, tk), lambda i, j, k: (i, k + c0 // tk))
    b_spec = pl.BlockSpec((tn, tk), lambda i, j, k: (j, k)) if mode == "nt" else pl.BlockSpec((tk, tn), lambda i, j, k: (k, j))
    ex_specs = []
    for arr, kind in extras:
        if kind == "mn":
            assert arr.shape == (M, N)
            ex_specs.append(pl.BlockSpec((tm, tn), lambda i, j, k: (i, j)))
        else:
            assert arr.shape == (1, N)
            ex_specs.append(pl.BlockSpec((1, tn), lambda i, j, k: (0, j)))
    n_ex = len(extras)

    def body(*refs):
        a_ref, b_ref = refs[:2]
        ex_refs = refs[2:2 + n_ex]
        o_ref = refs[2 + n_ex]
        acc_ref = refs[3 + n_ex] if nk > 1 else None
        k = pl.program_id(2)
        av = a_ref[...]
        if a_pro is not None:
            av = a_pro(av)
        part = lax.dot_general(av.astype(BF16), b_ref[...].astype(BF16), _DIMS[mode], preferred_element_type=F32)
        if nk == 1:
            if epi is not None:
                part = epi(part, *[r[...] for r in ex_refs])
            o_ref[...] = part.astype(o_ref.dtype)
            return

        @pl.when(k == 0)
        def _():
            acc_ref[...] = part

        @pl.when(k > 0)
        def _():
            acc_ref[...] += part

        @pl.when(k == nk - 1)
        def _():
            acc = acc_ref[...]
            if epi is not None:
                acc = epi(acc, *[r[...] for r in ex_refs])
            o_ref[...] = acc.astype(o_ref.dtype)

    return pl.pallas_call(
        body, name=name, grid=(M // tm, N // tn, nk),
        in_specs=[a_spec, b_spec] + ex_specs,
        out_specs=pl.BlockSpec((tm, tn), lambda i, j, k: (i, j)),
        out_shape=jax.ShapeDtypeStruct((M, N), out_dtype),
        scratch_shapes=[pltpu.VMEM((tm, tn), F32)] if nk > 1 else [],
        compiler_params=_cparams("parallel", "parallel", "arbitrary"))(a, b, *[e[0] for e in extras])


ROWS = 256


def _row_spec(w, ts=ROWS):
    return pl.BlockSpec((ts, w), lambda i: (i, 0))


def _vec_spec(w):
    return pl.BlockSpec((1, w), lambda i: (0, 0))


def _res_rms(x, sc, sh, name, y=None, g=None):
    S, D = x.shape
    has_res = y is not None

    def body(*refs):
        if has_res:
            x_ref, y_ref, g_ref, sc_ref, sh_ref, xo_ref, h_ref = refs
            xv = x_ref[...] + g_ref[...] * y_ref[...]
            xo_ref[...] = xv
        else:
            x_ref, sc_ref, sh_ref, h_ref = refs
            xv = x_ref[...]
        r = lax.rsqrt(jnp.mean(xv * xv, axis=-1, keepdims=True) + EPS)
        h_ref[...] = (xv * r * (1.0 + sc_ref[...]) + sh_ref[...]).astype(BF16)

    row, vec = _row_spec(D), _vec_spec(D)
    if has_res:
        return pl.pallas_call(body, name=name, grid=(S // ROWS,), in_specs=[row, row, vec, vec, vec], out_specs=[row, row],
                              out_shape=[jax.ShapeDtypeStruct((S, D), F32), jax.ShapeDtypeStruct((S, D), BF16)],
                              compiler_params=_cparams("parallel"))(x, y, g, sc, sh)
    h = pl.pallas_call(body, name=name, grid=(S // ROWS,), in_specs=[row, vec, vec], out_specs=row,
                       out_shape=jax.ShapeDtypeStruct((S, D), BF16), compiler_params=_cparams("parallel"))(x, sc, sh)
    return x, h


def _res_rms_bwd(x, dh, sc, dres, name, y=None, g=None):
    S, D = x.shape
    has_res = y is not None

    def body(*refs):
        if has_res:
            x_ref, dh_ref, sc_ref, dres_ref, y_ref, g_ref, dx_ref, dy_ref, dg_ref, dsc_ref, dsh_ref = refs
        else:
            x_ref, dh_ref, sc_ref, dres_ref, dx_ref, dsc_ref, dsh_ref = refs
        first = pl.program_id(0) == 0
        xv = x_ref[...]
        dh = dh_ref[...]
        r = lax.rsqrt(jnp.mean(xv * xv, axis=-1, keepdims=True) + EPS)
        xn = xv * r
        dxn = dh * (1.0 + sc_ref[...])
        dx = dres_ref[...] + r * (dxn - xn * jnp.mean(dxn * xn, axis=-1, keepdims=True))
        dx_ref[...] = dx
        parts = [(dsc_ref, jnp.sum(dh * xn, axis=0, keepdims=True)), (dsh_ref, jnp.sum(dh, axis=0, keepdims=True))]
        if has_res:
            dy_ref[...] = (dx * g_ref[...]).astype(BF16)
            parts.append((dg_ref, jnp.sum(dx * y_ref[...], axis=0, keepdims=True)))
        for ref, val in parts:
            @pl.when(first)
            def _(ref=ref, val=val):
                ref[...] = val

            @pl.when(jnp.logical_not(first))
            def _(ref=ref, val=val):
                ref[...] += val

    row, vec = _row_spec(D), _vec_spec(D)
    full = jax.ShapeDtypeStruct((S, D), F32)
    v = jax.ShapeDtypeStruct((1, D), F32)
    if has_res:
        return pl.pallas_call(body, name=name, grid=(S // ROWS,), in_specs=[row, row, vec, row, row, vec],
                              out_specs=[row, row, vec, vec, vec], out_shape=[full, jax.ShapeDtypeStruct((S, D), BF16), v, v, v],
                              compiler_params=_cparams("arbitrary"))(x, dh, sc, dres, y, g)
    return pl.pallas_call(body, name=name, grid=(S // ROWS,), in_specs=[row, row, vec, row],
                          out_specs=[row, vec, vec], out_shape=[full, v, v],
                          compiler_params=_cparams("arbitrary"))(x, dh, sc, dres)


def _res_loss(x, m, g, target, name):
    S, D = x.shape

    def body(x_ref, m_ref, g_ref, t_ref, loss_ref, dx_ref, dm_ref, dg_ref):
        first = pl.program_id(0) == 0
        mv = m_ref[...]
        err = x_ref[...] + g_ref[...] * mv - t_ref[...]
        dx = err * (1.0 / D)
        dx_ref[...] = dx
        dm_ref[...] = (dx * g_ref[...]).astype(BF16)
        part = 0.5 * jnp.sum(jnp.mean(err * err, axis=-1, keepdims=True), axis=0, keepdims=True)
        dg = jnp.sum(dx * mv, axis=0, keepdims=True)

        @pl.when(first)
        def _():
            loss_ref[...] = jnp.broadcast_to(part, loss_ref.shape)
            dg_ref[...] = dg

        @pl.when(jnp.logical_not(first))
        def _():
            loss_ref[...] += jnp.broadcast_to(part, loss_ref.shape)
            dg_ref[...] += dg

    row, vec = _row_spec(D), _vec_spec(D)
    full = jax.ShapeDtypeStruct((S, D), F32)
    return pl.pallas_call(body, name=name, grid=(S // ROWS,), in_specs=[row, row, vec, row],
                          out_specs=[pl.BlockSpec((8, LANES), lambda i: (0, 0)), row, row, vec],
                          out_shape=[jax.ShapeDtypeStruct((8, LANES), F32), full, jax.ShapeDtypeStruct((S, D), BF16), jax.ShapeDtypeStruct((1, D), F32)],
                          compiler_params=_cparams("arbitrary"))(x, m, g, target)


def _adamw(w, g, m, v, name):
    R, C = w.shape
    tr = R if R <= 256 else 256
    assert R % tr == 0

    def body(w_ref, g_ref, m_ref, v_ref, d_ref, nm_ref, nv_ref):
        gv = g_ref[...]
        nm = ADAM_B1 * m_ref[...] + (1.0 - ADAM_B1) * gv
        nv = ADAM_B2 * v_ref[...] + (1.0 - ADAM_B2) * jnp.square(gv)
        m_hat = nm / (1.0 - ADAM_B1 ** ADAM_STEP)
        v_hat = nv / (1.0 - ADAM_B2 ** ADAM_STEP)
        d_ref[...] = -ADAM_LR * (m_hat / (jnp.sqrt(v_hat) + ADAM_EPS) + ADAM_WD * w_ref[...])
        nm_ref[...] = nm
        nv_ref[...] = nv

    spec = pl.BlockSpec((tr, C), lambda i: (i, 0))
    out = jax.ShapeDtypeStruct((R, C), F32)
    return pl.pallas_call(body, name=name, grid=(R // tr,), in_specs=[spec] * 4, out_specs=[spec] * 3,
                          out_shape=[out, out, out], compiler_params=_cparams("parallel"))(w, g, m, v)


def _sum_slots(x, name):
    n, R, C = x.shape
    tr = R if R <= 256 else 256
    assert R % tr == 0

    def body(x_ref, o_ref):
        acc = x_ref[0].astype(F32)
        for j in range(1, n):
            acc = acc + x_ref[j].astype(F32)
        o_ref[...] = acc

    return pl.pallas_call(body, name=name, grid=(R // tr,), in_specs=[pl.BlockSpec((n, tr, C), lambda i: (0, i, 0))],
                          out_specs=pl.BlockSpec((tr, C), lambda i: (i, 0)), out_shape=jax.ShapeDtypeStruct((R, C), F32),
                          compiler_params=_cparams("parallel"))(x)


def _ew(fn, name, tiled, consts=(), outs=(), sums=(), ts=ROWS):
    tiled = [t if isinstance(t, tuple) else (t, t.shape[1], 0) for t in tiled]
    S = tiled[0][0].shape[0]
    n_t, n_c, n_o, n_s = len(tiled), len(consts), len(outs), len(sums)

    def body(*refs):
        ins = [r[...] for r in refs[:n_t + n_c]]
        res = fn(*ins)
        res = res if isinstance(res, (tuple, list)) else (res,)
        assert len(res) == n_o + n_s
        o_refs = refs[n_t + n_c:]
        for r, val in zip(o_refs[:n_o], res[:n_o]):
            r[...] = val.astype(r.dtype)
        first = pl.program_id(0) == 0
        for r, val in zip(o_refs[n_o:], res[n_o:]):
            @pl.when(first)
            def _(r=r, val=val):
                r[...] = val

            @pl.when(jnp.logical_not(first))
            def _(r=r, val=val):
                r[...] += val

    in_specs = [pl.BlockSpec((ts, w), lambda i, cb=cb: (i, cb)) for _, w, cb in tiled]
    in_specs += [pl.BlockSpec(c.shape, lambda i, nd=c.ndim: (0,) * nd) for c in consts]
    out_specs = [_row_spec(w, ts) for w, _ in outs] + [_vec_spec(w) for w in sums]
    out_shape = [jax.ShapeDtypeStruct((S, w), dt) for w, dt in outs] + [jax.ShapeDtypeStruct((1, w), F32) for w in sums]
    res = pl.pallas_call(body, name=name, grid=(S // ts,), in_specs=in_specs, out_specs=out_specs, out_shape=out_shape,
                         compiler_params=_cparams("arbitrary" if sums else "parallel"))(*[t[0] for t in tiled], *consts)
    return res


_GELU_C = math.sqrt(2.0 / math.pi)


def _gelu(x):
    return 0.5 * x * (1.0 + jnp.tanh(_GELU_C * (x + 0.044715 * x * x * x)))


def _dgelu(x):
    t = jnp.tanh(_GELU_C * (x + 0.044715 * x * x * x))
    return 0.5 * (1.0 + t) + 0.5 * x * (1.0 - t * t) * _GELU_C * (1.0 + 3.0 * 0.044715 * x * x)


def _sigmoid(x):
    return 1.0 / (1.0 + jnp.exp(-x))


def _log_sigmoid(x):
    return jnp.minimum(x, 0.0) - jnp.log(1.0 + jnp.exp(-jnp.abs(x)))


S5_TN = 64
S5_TB = 512
SCAN_TN = 32
SCAN_TB = 1024
SCAN_GROUP = 4


def _cmul(ar, ai, br, bi):
    return ar * br - ai * bi, ar * bi + ai * br


def _s5_discretise(lam_re, lam_im, log_dt, b_re, b_im):
    dt = jnp.exp(log_dt)[:, None]
    mag = jnp.exp(lam_re * dt)
    ang = lam_im * dt
    abar_re = mag * jnp.cos(ang)
    abar_im = mag * jnp.sin(ang)
    den = lam_re * lam_re + lam_im * lam_im
    coef_re = ((abar_re - 1.0) * lam_re + abar_im * lam_im) / den
    coef_im = (abar_im * lam_re - (abar_re - 1.0) * lam_im) / den
    bbar_re = coef_re[..., None] * b_re - coef_im[..., None] * b_im
    bbar_im = coef_re[..., None] * b_im + coef_im[..., None] * b_re
    return abar_re, abar_im, bbar_re, bbar_im


def _s5_scan_tables(a_re, a_im, reverse):
    pr, pi = [a_re], [a_im]
    for _ in range(7):
        r, i = _cmul(pr[-1], pi[-1], pr[-1], pi[-1])
        pr.append(r)
        pi.append(i)
    lane = np.arange(LANES)
    live = np.stack([(lane < LANES - (1 << k)) if reverse else (lane >= (1 << k)) for k in range(7)] + [np.ones(LANES, bool)])
    apow = jnp.stack([jnp.stack(pr), jnp.stack(pi)])[..., None] * jnp.asarray(live.astype(np.float32))[None, :, None, :]
    n = np.arange(1, LANES + 1)
    if reverse:
        n = n[::-1]
    tr = jnp.ones((a_re.shape[0], LANES), F32)
    ti = jnp.zeros((a_re.shape[0], LANES), F32)
    for k in range(8):
        bit = jnp.asarray(((n >> k) & 1).astype(np.float32))[None, :]
        fr, fi = pr[k][:, None], pi[k][:, None]
        mr = bit * fr + (1.0 - bit)
        mi = bit * fi
        tr, ti = _cmul(tr, ti, mr, mi)
    return apow, jnp.stack([tr, ti])


def _s5_scan(bu, apow, ptab, name, reverse):
    _, N, S = bu.shape
    tn, tb = SCAN_TN, min(SCAN_TB, S)
    nt = S // tb
    nsub = tb // LANES
    order = list(range(nsub - 1, -1, -1) if reverse else range(nsub))

    def tmap(i, t):
        return (0, i, nt - 1 - t) if reverse else (0, i, t)

    def body(bu_ref, ap_ref, pt_ref, x_ref, carry_ref):
        @pl.when(pl.program_id(1) == 0)
        def _():
            carry_ref[...] = jnp.zeros_like(carry_ref)

        lane = lax.broadcasted_iota(jnp.int32, (tn, LANES), 1)
        at_edge = lane == (0 if reverse else LANES - 1)
        edges = {}
        for g in range(0, nsub, SCAN_GROUP):
            subs = order[g:g + SCAN_GROUP]
            xs = [(bu_ref[0, :, pl.ds(sb * LANES, LANES)], bu_ref[1, :, pl.ds(sb * LANES, LANES)]) for sb in subs]
            for k in range(7):
                sh = 1 << k
                ar, ai = ap_ref[0, k], ap_ref[1, k]
                nxt = []
                for xr, xi in xs:
                    rr = pltpu.roll(xr, (LANES - sh) if reverse else sh, 1)
                    ri = pltpu.roll(xi, (LANES - sh) if reverse else sh, 1)
                    nxt.append((xr + ar * rr - ai * ri, xi + ar * ri + ai * rr))
                xs = nxt
            for sb, (xr, xi) in zip(subs, xs):
                x_ref[0, :, pl.ds(sb * LANES, LANES)] = xr
                x_ref[1, :, pl.ds(sb * LANES, LANES)] = xi
                edges[sb] = (jnp.broadcast_to(jnp.sum(jnp.where(at_edge, xr, 0.0), axis=1, keepdims=True), (tn, LANES)),
                             jnp.broadcast_to(jnp.sum(jnp.where(at_edge, xi, 0.0), axis=1, keepdims=True), (tn, LANES)))
        pr, pi = pt_ref[0], pt_ref[1]
        br, bi = ap_ref[0, 7], ap_ref[1, 7]
        cr, ci = carry_ref[0], carry_ref[1]
        for sb in order:
            sl = pl.ds(sb * LANES, LANES)
            x_ref[0, :, sl] = x_ref[0, :, sl] + pr * cr - pi * ci
            x_ref[1, :, sl] = x_ref[1, :, sl] + pr * ci + pi * cr
            er, ei = edges[sb]
            cr, ci = er + br * cr - bi * ci, ei + br * ci + bi * cr
        carry_ref[0] = cr
        carry_ref[1] = ci

    return pl.pallas_call(
        body, name=name, grid=(N // tn, nt),
        in_specs=[pl.BlockSpec((2, tn, tb), tmap), pl.BlockSpec((2, 8, tn, LANES), lambda i, t: (0, 0, i, 0)),
                  pl.BlockSpec((2, tn, LANES), lambda i, t: (0, i, 0))],
        out_specs=pl.BlockSpec((2, tn, tb), tmap), out_shape=jax.ShapeDtypeStruct((2, N, S), F32),
        scratch_shapes=[pltpu.VMEM((2, tn, LANES), F32)], compiler_params=_cparams("parallel", "arbitrary"))(bu, apow, ptab)


def _s5_da(lam, x, name):
    _, N, S = x.shape
    nt = S // S5_TB
    nsub = S5_TB // LANES

    def body(l_ref, x_ref, o_ref, acc_ref, carry_ref):
        t = pl.program_id(1)

        @pl.when(t == 0)
        def _():
            acc_ref[...] = jnp.zeros_like(acc_ref)
            carry_ref[...] = jnp.zeros_like(carry_ref)

        lane = lax.broadcasted_iota(jnp.int32, (S5_TN, LANES), 1)
        cr, ci = carry_ref[0], carry_ref[1]
        ar, ai = acc_ref[0], acc_ref[1]
        for sb in range(nsub):
            sl = pl.ds(sb * LANES, LANES)
            xr, xi = x_ref[0, :, sl], x_ref[1, :, sl]
            pr = jnp.where(lane == 0, pltpu.roll(cr, 1, 1), pltpu.roll(xr, 1, 1))
            pi = jnp.where(lane == 0, pltpu.roll(ci, 1, 1), pltpu.roll(xi, 1, 1))
            lr, li = l_ref[0, :, sl], l_ref[1, :, sl]
            ar = ar + lr * pr + li * pi
            ai = ai + li * pr - lr * pi
            cr, ci = xr, xi
        acc_ref[0] = ar
        acc_ref[1] = ai
        carry_ref[0] = cr
        carry_ref[1] = ci

        @pl.when(t == nt - 1)
        def _():
            o_ref[0] = jnp.sum(ar, axis=1, keepdims=True)
            o_ref[1] = jnp.sum(ai, axis=1, keepdims=True)

    blk = pl.BlockSpec((2, S5_TN, S5_TB), lambda i, t: (0, i, t))
    return pl.pallas_call(
        body, name=name, grid=(N // S5_TN, nt), in_specs=[blk, blk],
        out_specs=pl.BlockSpec((2, S5_TN, 1), lambda i, t: (0, i, 0)), out_shape=jax.ShapeDtypeStruct((2, N, 1), F32),
        scratch_shapes=[pltpu.VMEM((2, S5_TN, LANES), F32), pltpu.VMEM((2, S5_TN, LANES), F32)],
        compiler_params=_cparams("parallel", "arbitrary"))(lam, x)


def _block_diag(t):
    G, a, b = t.shape
    return (t[:, :, None, :] * jnp.eye(G, dtype=t.dtype)[:, None, :, None]).reshape(G * a, G * b)


def _block_diag_take(m, G):
    a, b = m.shape[0] // G, m.shape[1] // G
    m4 = m.reshape(G, a, G, b)
    return jnp.sum(m4 * jnp.eye(G, dtype=m.dtype)[:, None, :, None], axis=2)


def _s5_block_fwd(u, w, pfx):
    a_re, a_im, bb_re, bb_im = _s5_discretise(w["lam_re"], w["lam_im"], w["log_dt"], w["b_re"], w["b_im"])
    bcat = jnp.concatenate([_block_diag(bb_re), _block_diag(bb_im)], axis=0).astype(BF16)
    ccat = jnp.concatenate([_block_diag(jnp.swapaxes(w["c_re"], 1, 2)),
                            -_block_diag(jnp.swapaxes(w["c_im"], 1, 2))], axis=0).astype(BF16)
    S = u.shape[0]
    af_re, af_im = a_re.reshape(-1), a_im.reshape(-1)
    apow, ptab = _s5_scan_tables(af_re, af_im, False)
    bu = _mm(bcat, u, "nt", pfx + "_bu").reshape(2, S5_N, S)
    x = _s5_scan(bu, apow, ptab, pfx + "_scan", False)
    d_row = w["d"].reshape(1, MIX_HALF)
    ys = _mm(x.reshape(2 * S5_N, S), ccat, "tn", pfx + "_y", epi=lambda acc, ut, dr: acc + dr * ut, extras=[(u, "mn"), (d_row, "n")])
    z = _mm(ys, w["w_glu"], "nn", pfx + "_glu", a_pro=_gelu, epi=lambda acc, b: acc + b, extras=[(w["b_glu"].reshape(1, -1), "n")])
    y2, = _ew(lambda ysv, zv: _gelu(ysv) * _sigmoid(zv), pfx + "_gate", [ys, z], outs=[(MIX_HALF, F32)])
    return y2, dict(u=u, x=x, ys=ys, z=z, bcat=bcat, ccat=ccat, a=(af_re, af_im), d_row=d_row)


def _s5_block_bwd(dy2, w, res, pfx):
    u, x, ys, z, bcat, ccat = res["u"], res["x"], res["ys"], res["z"], res["bcat"], res["ccat"]
    S = u.shape[0]

    def gate_bwd(dy, ysv, zv):
        sg = _sigmoid(zv)
        dz = dy * _gelu(ysv) * sg * (1.0 - sg)
        return dz, jnp.sum(dz, axis=0, keepdims=True)

    dz, db_glu = _ew(gate_bwd, pfx + "_gate_bwd", [dy2, ys, z], outs=[(MIX_HALF, F32)], sums=[MIX_HALF])
    dw_glu = _mm(ys, dz, "tn", pfx + "_dwglu", a_pro=_gelu)
    dys = _mm(dz, w["w_glu"], "nt", pfx + "_dys", epi=lambda acc, dy, zv, ysv: (acc + dy * _sigmoid(zv)) * _dgelu(ysv),
              extras=[(dy2, "mn"), (z, "mn"), (ys, "mn")])
    dd, = _ew(lambda a, b: jnp.sum(a * b, axis=0, keepdims=True), pfx + "_dd", [dys, u], sums=[MIX_HALF])
    xcat = x.reshape(2 * S5_N, S)
    dccat = _mm(xcat, dys, "nn", pfx + "_dc")
    dx = _mm(ccat, dys, "nt", pfx + "_dx").reshape(2, S5_N, S)
    af_re, af_im = res["a"]
    apow, ptab = _s5_scan_tables(af_re, -af_im, True)
    lam = _s5_scan(dx, apow, ptab, pfx + "_scan_bwd", True)
    lcat = lam.reshape(2 * S5_N, S)
    dbcat = _mm(lcat, u, "nn", pfx + "_db")
    du = _mm(lcat, bcat, "tn", pfx + "_du", epi=lambda acc, dyv, dr: acc + dyv * dr, extras=[(dys, "mn"), (res["d_row"], "n")])
    da = _s5_da(lam, x, pfx + "_da")
    G = S5_GROUPS
    d_abar_re, d_abar_im = da[0].reshape(G, S5_STATE), da[1].reshape(G, S5_STATE)
    d_bb_re, d_bb_im = _block_diag_take(dbcat[:S5_N], G), _block_diag_take(dbcat[S5_N:], G)
    _, vjp = jax.vjp(_s5_discretise, w["lam_re"], w["lam_im"], w["log_dt"], w["b_re"], w["b_im"])
    g_lam_re, g_lam_im, g_log_dt, g_b_re, g_b_im = vjp((d_abar_re, d_abar_im, d_bb_re, d_bb_im))
    g_c_re = jnp.swapaxes(_block_diag_take(dccat[:S5_N], G), 1, 2)
    g_c_im = -jnp.swapaxes(_block_diag_take(dccat[S5_N:], G), 1, 2)
    grads = dict(lam_re=g_lam_re, lam_im=g_lam_im, log_dt=g_log_dt, b_re=g_b_re, b_im=g_b_im, c_re=g_c_re, c_im=g_c_im,
                 d=dd.reshape(G, S5_GROUP_WIDTH), w_glu=dw_glu, b_glu=db_glu.reshape(-1))
    return du, grads


SGU_TS = 512
N_PAIRS = MIX_HALF // LANES


def _half_masks(rows):
    lane = lax.broadcasted_iota(jnp.int32, (rows, LANES), 1)
    left = (lane < HEAD_DIM).astype(F32)
    return left, 1.0 - left


def _sgu_norm(zv, gain, bias):
    v = _gelu(zv)
    mu = jnp.mean(v, axis=-1, keepdims=True)
    vc = v - mu
    rstd = lax.rsqrt(jnp.mean(vc * vc, axis=-1, keepdims=True) + EPS)
    vhat = vc * rstd
    return vhat, rstd, vhat * gain + bias


def _sgu_tables(w_s, b_s):
    mask = jnp.tril(jnp.ones((SGU_CHUNK, SGU_CHUNK), dtype=bool))
    wm = jnp.where(mask[None], w_s, 0.0).astype(BF16)
    bias_tab = jnp.repeat(b_s.T, MIX_HALF // SGU_GROUPS, axis=1)
    return wm, bias_tab


def _sgu_fwd(proj, ln_gain, ln_bias, wm, bias_tab, name):
    S = proj.shape[0]
    nch = SGU_TS // SGU_CHUNK

    def body(zu_ref, zv_ref, g_ref, b_ref, w_ref, bt_ref, o_ref):
        left, right = _half_masks(SGU_CHUNK)
        _, _, vn = _sgu_norm(zv_ref[...], g_ref[...], b_ref[...])
        for ch in range(nch):
            rows = pl.ds(ch * SGU_CHUNK, SGU_CHUNK)
            for p in range(N_PAIRS):
                cols = pl.ds(p * LANES, LANES)
                vp = vn[ch * SGU_CHUNK:(ch + 1) * SGU_CHUNK, p * LANES:(p + 1) * LANES]
                mixed = (jnp.dot(w_ref[2 * p], (vp * left).astype(BF16), preferred_element_type=F32)
                         + jnp.dot(w_ref[2 * p + 1], (vp * right).astype(BF16), preferred_element_type=F32) + bt_ref[:, cols])
                o_ref[rows, cols] = _gelu(zu_ref[rows, cols]) * mixed

    vec = _vec_spec(MIX_HALF)
    return pl.pallas_call(
        body, name=name, grid=(S // SGU_TS,),
        in_specs=[pl.BlockSpec((SGU_TS, MIX_HALF), lambda i: (i, 1)), pl.BlockSpec((SGU_TS, MIX_HALF), lambda i: (i, 2)), vec, vec,
                  pl.BlockSpec((SGU_GROUPS, SGU_CHUNK, SGU_CHUNK), lambda i: (0, 0, 0)), pl.BlockSpec((SGU_CHUNK, MIX_HALF), lambda i: (0, 0))],
        out_specs=_row_spec(MIX_HALF, SGU_TS), out_shape=jax.ShapeDtypeStruct((S, MIX_HALF), F32),
        compiler_params=_cparams("parallel"))(proj, proj, ln_gain, ln_bias, wm, bias_tab)


def _sgu_bwd(dout, proj, ln_gain, ln_bias, wm, bias_tab, name):
    S = proj.shape[0]
    nch = SGU_TS // SGU_CHUNK
    nt_dims = (((1,), (1,)), ((), ()))
    tn_dims = (((0,), (0,)), ((), ()))

    def body(do_ref, zu_ref, zv_ref, g_ref, b_ref, w_ref, bt_ref, dzu_ref, dzv_ref, dw_ref, dbt_ref, dg_ref, db_ref, dvn_ref):
        first = pl.program_id(0) == 0

        @pl.when(first)
        def _():
            dw_ref[...] = jnp.zeros_like(dw_ref)
            dbt_ref[...] = jnp.zeros_like(dbt_ref)
            dg_ref[...] = jnp.zeros_like(dg_ref)
            db_ref[...] = jnp.zeros_like(db_ref)

        left, right = _half_masks(SGU_CHUNK)
        zv = zv_ref[...]
        vhat, rstd, vn = _sgu_norm(zv, g_ref[...], b_ref[...])
        for ch in range(nch):
            rows = pl.ds(ch * SGU_CHUNK, SGU_CHUNK)
            for p in range(N_PAIRS):
                cols = pl.ds(p * LANES, LANES)
                vp = vn[ch * SGU_CHUNK:(ch + 1) * SGU_CHUNK, p * LANES:(p + 1) * LANES]
                vl, vr = (vp * left).astype(BF16), (vp * right).astype(BF16)
                mixed = (jnp.dot(w_ref[2 * p], vl, preferred_element_type=F32)
                         + jnp.dot(w_ref[2 * p + 1], vr, preferred_element_type=F32) + bt_ref[:, cols])
                zu = zu_ref[rows, cols]
                do = do_ref[rows, cols]
                dzu_ref[rows, cols] = do * mixed * _dgelu(zu)
                dmix = do * _gelu(zu)
                dbt_ref[:, cols] += dmix
                dl, dr = (dmix * left).astype(BF16), (dmix * right).astype(BF16)
                dw_ref[2 * p] += lax.dot_general(dl, vl, nt_dims, preferred_element_type=F32)
                dw_ref[2 * p + 1] += lax.dot_general(dr, vr, nt_dims, preferred_element_type=F32)
                dvn_ref[rows, cols] = (lax.dot_general(w_ref[2 * p], dl, tn_dims, preferred_element_type=F32)
                                       + lax.dot_general(w_ref[2 * p + 1], dr, tn_dims, preferred_element_type=F32))
        dvn = dvn_ref[...]
        dg_ref[...] += jnp.sum(dvn * vhat, axis=0, keepdims=True)
        db_ref[...] += jnp.sum(dvn, axis=0, keepdims=True)
        dvh = dvn * g_ref[...]
        dv = rstd * (dvh - jnp.mean(dvh, axis=-1, keepdims=True) - vhat * jnp.mean(dvh * vhat, axis=-1, keepdims=True))
        dzv_ref[...] = dv * _dgelu(zv)

    vec = _vec_spec(MIX_HALF)
    row = _row_spec(MIX_HALF, SGU_TS)
    wspec = pl.BlockSpec((SGU_GROUPS, SGU_CHUNK, SGU_CHUNK), lambda i: (0, 0, 0))
    tspec = pl.BlockSpec((SGU_CHUNK, MIX_HALF), lambda i: (0, 0))
    full = jax.ShapeDtypeStruct((S, MIX_HALF), F32)
    v = jax.ShapeDtypeStruct((1, MIX_HALF), F32)
    return pl.pallas_call(
        body, name=name, grid=(S // SGU_TS,),
        in_specs=[row, pl.BlockSpec((SGU_TS, MIX_HALF), lambda i: (i, 1)), pl.BlockSpec((SGU_TS, MIX_HALF), lambda i: (i, 2)), vec, vec,
                  wspec, tspec],
        out_specs=[row, row, wspec, tspec, vec, vec],
        out_shape=[full, full, jax.ShapeDtypeStruct((SGU_GROUPS, SGU_CHUNK, SGU_CHUNK), F32),
                   jax.ShapeDtypeStruct((SGU_CHUNK, MIX_HALF), F32), v, v],
        scratch_shapes=[pltpu.VMEM((SGU_TS, MIX_HALF), F32)],
        compiler_params=_cparams("arbitrary"))(dout, proj, proj, ln_gain, ln_bias, wm, bias_tab)


def _sgu_grads(dw, dbias_tab):
    mask = jnp.tril(jnp.ones((SGU_CHUNK, SGU_CHUNK), dtype=bool))
    g_w = jnp.where(mask[None], dw, 0.0)
    g_b = dbias_tab.reshape(SGU_CHUNK, SGU_GROUPS, MIX_HALF // SGU_GROUPS).sum(axis=-1).T
    return g_w, g_b


def _head_avg_matrix(w):
    idx = np.arange(w) // HEAD_DIM
    return jnp.asarray((idx[:, None] == idx[None, :]).astype(np.float32) / HEAD_DIM, dtype=BF16)


def _head_mean(t, bavg):
    hi = t.astype(BF16)
    lo = (t - hi.astype(F32)).astype(BF16)
    return jnp.dot(hi, bavg, preferred_element_type=F32) + jnp.dot(lo, bavg, preferred_element_type=F32)


def _head_rms(t, bavg):
    r = lax.rsqrt(_head_mean(t * t, bavg) + EPS)
    return t * r, r


def _head_rms_bwd(dn, n, r, bavg):
    return r * (dn - n * _head_mean(dn * n, bavg))


GLA_TS = 512
C = GLA_CHUNK
NT_DIMS = (((1,), (1,)), ((), ()))
TN_DIMS = (((0,), (0,)), ((), ()))
HI = lax.Precision.HIGHEST


def _bdot(a, b, dims=(((1,), (0,)), ((), ()))):
    return lax.dot_general(a.astype(BF16), b.astype(BF16), dims, preferred_element_type=F32)


def _gla_chunk_terms(q, k, z):
    row = lax.broadcasted_iota(jnp.int32, (C, C), 0)
    col = lax.broadcasted_iota(jnp.int32, (C, C), 1)
    lc = _log_sigmoid(z) * (1.0 / GLA_TAU)
    b = lax.dot_general((row >= col).astype(F32), lc, (((1,), (0,)), ((), ())), precision=HI, preferred_element_type=F32)
    b_last = jnp.sum(lc, axis=0, keepdims=True)
    b_mid = b[C // 2:C // 2 + 1, :]
    scale = HEAD_DIM ** -0.5
    e_b, e_q, e_k, e_l = jnp.exp(b), jnp.exp(b - b_mid), jnp.exp(b_mid - b), jnp.exp(b_last - b)
    qs = q * (scale * e_b)
    qe = q * (scale * e_q)
    ke = k * e_k
    kl = k * e_l
    return dict(e_b=e_b, e_q=e_q, e_k=e_k, e_l=e_l, qs=qs, qe=qe, ke=ke, kl=kl, dec=jnp.exp(b_last), causal=row >= col, scale=scale)


def _pair_block_diag():
    r = lax.broadcasted_iota(jnp.int32, (LANES, LANES), 0) // HEAD_DIM
    c = lax.broadcasted_iota(jnp.int32, (LANES, LANES), 1) // HEAD_DIM
    return (r == c).astype(F32)


def _gla_fwd(proj, z, name):
    S = proj.shape[0]
    nch = GLA_TS // C

    def body(q_ref, k_ref, v_ref, z_ref, o_ref, st_ref, state_ref):
        @pl.when(pl.program_id(1) == 0)
        def _():
            state_ref[...] = jnp.zeros_like(state_ref)

        left, right = _half_masks(C)
        bd = _pair_block_diag()
        for ch in range(nch):
            rows = pl.ds(ch * C, C)
            q, k, v = q_ref[rows, :], k_ref[rows, :], v_ref[rows, :]
            t = _gla_chunk_terms(q, k, z_ref[rows, :])
            st = state_ref[...]
            st_ref[ch, 0] = st
            o = _bdot(t["qs"], st, NT_DIMS)
            for m in (left, right):
                a = jnp.where(t["causal"], _bdot(t["qe"] * m, t["ke"], NT_DIMS), 0.0)
                o = o + m * _bdot(a, v)
            o_ref[rows, :] = o
            state_ref[...] = st * t["dec"] + bd * _bdot(v, t["kl"], TN_DIMS)

    def col(cb):
        return pl.BlockSpec((GLA_TS, LANES), lambda p, i: (i, cb * N_PAIRS + p))

    return pl.pallas_call(
        body, name=name, grid=(N_PAIRS, S // GLA_TS),
        in_specs=[col(0), col(1), col(2), pl.BlockSpec((GLA_TS, LANES), lambda p, i: (i, p))],
        out_specs=[pl.BlockSpec((GLA_TS, LANES), lambda p, i: (i, p)), pl.BlockSpec((nch, 1, LANES, LANES), lambda p, i: (i, p, 0, 0))],
        out_shape=[jax.ShapeDtypeStruct((S, MIX_HALF), F32), jax.ShapeDtypeStruct((S // C, N_PAIRS, LANES, LANES), F32)],
        scratch_shapes=[pltpu.VMEM((LANES, LANES), F32)], compiler_params=_cparams("parallel", "arbitrary"))(proj, proj, proj, z)


def _gla_bwd(do, proj, z, states, name):
    S = proj.shape[0]
    nch = GLA_TS // C
    nblk = S // GLA_TS

    def body(do_ref, q_ref, k_ref, v_ref, z_ref, st_ref, dq_ref, dk_ref, dv_ref, dlc_ref, dstate_ref):
        @pl.when(pl.program_id(1) == 0)
        def _():
            dstate_ref[...] = jnp.zeros_like(dstate_ref)

        left, right = _half_masks(C)
        bd = _pair_block_diag()
        rowi = lax.broadcasted_iota(jnp.int32, (C, LANES), 0)
        for ch in range(nch - 1, -1, -1):
            rows = pl.ds(ch * C, C)
            q, k, v, dov = q_ref[rows, :], k_ref[rows, :], v_ref[rows, :], do_ref[rows, :]
            t = _gla_chunk_terms(q, k, z_ref[rows, :])
            st = st_ref[ch, 0]
            dst_next = dstate_ref[...]
            g = bd * dst_next
            dqs = _bdot(dov, st)
            dv = _bdot(t["kl"], g, NT_DIMS)
            dkl = _bdot(v, g)
            dqe = jnp.zeros((C, LANES), F32)
            dke = jnp.zeros((C, LANES), F32)
            for m in (left, right):
                a = jnp.where(t["causal"], _bdot(t["qe"] * m, t["ke"], NT_DIMS), 0.0)
                da = jnp.where(t["causal"], _bdot(dov * m, v, NT_DIMS), 0.0)
                dv = dv + m * _bdot(a, dov, TN_DIMS)
                dqe = dqe + m * _bdot(da, t["ke"])
                dke = dke + m * _bdot(da, t["qe"], TN_DIMS)
            dstate_ref[...] = bd * (dst_next * t["dec"] + _bdot(dov, t["qs"], TN_DIMS))
            db_last = jnp.sum(dst_next * st, axis=0, keepdims=True) * t["dec"] + jnp.sum(dkl * t["kl"], axis=0, keepdims=True)
            db = dqs * t["qs"] + dqe * t["qe"] - dke * t["ke"] - dkl * t["kl"]
            db = db + jnp.where(rowi == C - 1, db_last, 0.0)
            dq_ref[rows, :] = (dqs * t["e_b"] + dqe * t["e_q"]) * t["scale"]
            dk_ref[rows, :] = dke * t["e_k"] + dkl * t["e_l"]
            dv_ref[rows, :] = dv
            row = lax.broadcasted_iota(jnp.int32, (C, C), 0)
            colm = lax.broadcasted_iota(jnp.int32, (C, C), 1)
            dlc_ref[rows, :] = lax.dot_general((colm >= row).astype(F32), db, (((1,), (0,)), ((), ())), precision=HI,
                                               preferred_element_type=F32)

    def col(cb):
        return pl.BlockSpec((GLA_TS, LANES), lambda p, i: (nblk - 1 - i, cb * N_PAIRS + p))

    blk = pl.BlockSpec((GLA_TS, LANES), lambda p, i: (nblk - 1 - i, p))
    full = jax.ShapeDtypeStruct((S, MIX_HALF), F32)
    return pl.pallas_call(
        body, name=name, grid=(N_PAIRS, nblk),
        in_specs=[blk, col(0), col(1), col(2), blk, pl.BlockSpec((nch, 1, LANES, LANES), lambda p, i: (nblk - 1 - i, p, 0, 0))],
        out_specs=[blk, blk, blk, blk], out_shape=[full, full, full, full],
        scratch_shapes=[pltpu.VMEM((LANES, LANES), F32)], compiler_params=_cparams("parallel", "arbitrary"))(do, proj, proj, proj, z, states)


def _gla_block_fwd(proj, w_lr_pad, b_lr, gain, bavg, pfx):
    z = _mm(proj, w_lr_pad, "nn", pfx + "_z", a_cols=(7 * MIX_HALF, MIX_HALF), epi=lambda acc, b: acc + b, extras=[(b_lr, "n")])
    o, states = _gla_fwd(proj, z, pfx + "_core")

    def out(ov, gg, ba, gn):
        n, _ = _head_rms(ov, ba)
        return n * gn * (gg * _sigmoid(gg))

    og, = _ew(out, pfx + "_out", [o, (proj, MIX_HALF, 3)], consts=[bavg, gain], outs=[(MIX_HALF, F32)])
    return og, dict(z=z, o=o, states=states)


def _gla_block_bwd(dog, proj, w_lr_pad, gain, bavg, res, pfx):
    z, o, states = res["z"], res["o"], res["states"]

    def out_bwd(dy, ov, gg, ba, gn):
        n, r = _head_rms(ov, ba)
        sg = _sigmoid(gg)
        silu = gg * sg
        dn = dy * gn * silu
        do = _head_rms_bwd(dn, n, r, ba)
        dgg = dy * n * gn * (sg * (1.0 + gg * (1.0 - sg)))
        return do, dgg, jnp.sum(dy * n * silu, axis=0, keepdims=True)

    do, dgg, dgain = _ew(out_bwd, pfx + "_out_bwd", [dog, o, (proj, MIX_HALF, 3)], consts=[bavg, gain],
                         outs=[(MIX_HALF, F32), (MIX_HALF, F32)], sums=[MIX_HALF])
    dq, dk, dv, dlc = _gla_bwd(do, proj, z, states, pfx + "_core_bwd")

    def decay_bwd(dl, zv):
        dz = dl * (1.0 / GLA_TAU) * (1.0 - _sigmoid(zv))
        return dz, jnp.sum(dz, axis=0, keepdims=True)

    dz, db_lr = _ew(decay_bwd, pfx + "_decay_bwd", [dlc, z], outs=[(MIX_HALF, F32)], sums=[MIX_HALF])
    dw_lr_pad = _mm(proj, dz, "tn", pfx + "_dwlr", a_cols=(7 * MIX_HALF, MIX_HALF))
    dsmall = _mm(dz, w_lr_pad, "nt", pfx + "_dsmall")
    return (dq, dk, dv, dgg, dsmall), dict(w_lr=dw_lr_pad[:GLA_RANK], b_lr=db_lr.reshape(-1), gain=dgain.reshape(-1, HEAD_DIM))


FOX_T = 512
FOX_HEADS = MIX_HALF // HEAD_DIM
NEG = -1e30
CUM_T = 512


def _cum_lanes(x, name, reverse, pre=None):
    R, S = x.shape
    nb = S // CUM_T

    def body(x_ref, o_ref, carry_ref):
        @pl.when(pl.program_id(0) == 0)
        def _():
            carry_ref[...] = jnp.zeros_like(carry_ref)

        xv = x_ref[...]
        if pre is not None:
            xv = pre(xv)
        i = lax.broadcasted_iota(jnp.int32, (CUM_T, CUM_T), 0)
        j = lax.broadcasted_iota(jnp.int32, (CUM_T, CUM_T), 1)
        tri = ((i >= j) if reverse else (i <= j)).astype(F32)
        c = lax.dot_general(xv, tri, (((1,), (0,)), ((), ())), precision=HI, preferred_element_type=F32)
        carry = carry_ref[...]
        o_ref[...] = c + carry[:, 0:1]
        carry_ref[...] = carry + jnp.sum(xv, axis=1, keepdims=True)

    spec = pl.BlockSpec((R, CUM_T), (lambda i: (0, nb - 1 - i)) if reverse else (lambda i: (0, i)))
    return pl.pallas_call(body, name=name, grid=(nb,), in_specs=[spec], out_specs=spec, out_shape=jax.ShapeDtypeStruct((R, S), F32),
                          scratch_shapes=[pltpu.VMEM((R, LANES), F32)], compiler_params=_cparams("arbitrary"))(x)


def _fox_scores(q, k, cqb, ck_ref, h, m, diag):
    cq = cqb[:, h * HEAD_DIM:h * HEAD_DIM + 1]
    ck = ck_ref[0, h:h + 1, :]
    s = lax.dot_general(q * m.astype(q.dtype), k, NT_DIMS, preferred_element_type=F32) + (cq - ck)
    if not diag:
        return s
    row = lax.broadcasted_iota(jnp.int32, (FOX_T, FOX_T), 0)
    col = lax.broadcasted_iota(jnp.int32, (FOX_T, FOX_T), 1)
    return jnp.where(row < col, NEG, s)


def _on_causal_blocks(q_blk, k_blk, step):
    @pl.when(k_blk < q_blk)
    def _():
        step(False)

    @pl.when(k_blk == q_blk)
    def _():
        step(True)


def _fox_fwd(qn, kn, proj, cum_b, cum_tp, name):
    S = qn.shape[0]
    nq = S // FOX_T

    def body(q_ref, k_ref, v_ref, cq_ref, ck_ref, o_ref, lse_ref, m_scr, acc_scr):
        qi, ki = pl.program_id(1), pl.program_id(2)

        @pl.when(ki == 0)
        def _():
            m_scr[...] = jnp.full_like(m_scr, NEG)
            acc_scr[...] = jnp.zeros_like(acc_scr)

        left, right = _half_masks(FOX_T)

        def step(diag):
            q, k, v = q_ref[...], k_ref[...], v_ref[...].astype(BF16)
            cqb = cq_ref[...]
            for h, m in enumerate((left, right)):
                s = _fox_scores(q, k, cqb, ck_ref, h, m, diag)
                m_prev = m_scr[h]
                m_new = jnp.maximum(m_prev, jnp.max(s, axis=1, keepdims=True))
                p = jnp.exp(s - m_new)
                v_h = jnp.where(m > 0, v, jnp.ones_like(v))
                acc_scr[h] = jnp.exp(m_prev - m_new) * acc_scr[h] + jnp.dot(p.astype(BF16), v_h, preferred_element_type=F32)
                m_scr[h] = m_new

        _on_causal_blocks(qi, ki, step)

        @pl.when(ki == qi)
        def _():
            a0, a1 = acc_scr[0], acc_scr[1]
            is_left = left > 0
            num = jnp.where(is_left, a0, a1)
            den = jnp.where(is_left, pltpu.roll(a0, HEAD_DIM, 1), pltpu.roll(a1, HEAD_DIM, 1))
            o_ref[...] = num / den
            lse_ref[...] = jnp.where(is_left, m_scr[0], m_scr[1]) + jnp.log(den)

    qspec = pl.BlockSpec((FOX_T, LANES), lambda p, qi, ki: (qi, p))
    kspec = pl.BlockSpec((FOX_T, LANES), lambda p, qi, ki: (jnp.minimum(ki, qi), p))
    vspec = pl.BlockSpec((FOX_T, LANES), lambda p, qi, ki: (jnp.minimum(ki, qi), 6 * N_PAIRS + p))
    ckspec = pl.BlockSpec((1, 8, FOX_T), lambda p, qi, ki: (p, 0, jnp.minimum(ki, qi)))
    full = jax.ShapeDtypeStruct((S, MIX_HALF), F32)
    return pl.pallas_call(
        body, name=name, grid=(N_PAIRS, nq, nq), in_specs=[qspec, kspec, vspec, qspec, ckspec], out_specs=[qspec, qspec],
        out_shape=[full, full],
        scratch_shapes=[pltpu.VMEM((2, FOX_T, 1), F32), pltpu.VMEM((2, FOX_T, LANES), F32)],
        compiler_params=_cparams("parallel", "parallel", "arbitrary"))(qn, kn, proj, cum_b, cum_tp)


def _fox_bwd_dq(do, qn, kn, proj, cum_b, cum_tp, lse_b, delta_b, name):
    S = qn.shape[0]
    nq = S // FOX_T
    scale = HEAD_DIM ** -0.5

    def body(do_ref, q_ref, k_ref, v_ref, cq_ref, ck_ref, lse_ref, dl_ref, dq_ref, dcq_ref, acc_scr):
        qi, ki = pl.program_id(1), pl.program_id(2)

        @pl.when(ki == 0)
        def _():
            acc_scr[...] = jnp.zeros_like(acc_scr)

        left, right = _half_masks(FOX_T)

        def step(diag):
            q, k, v, dov = q_ref[...], k_ref[...], v_ref[...].astype(BF16), do_ref[...]
            cqb, lseb, dlb = cq_ref[...], lse_ref[...], dl_ref[...]
            for h, m in enumerate((left, right)):
                s = _fox_scores(q, k, cqb, ck_ref, h, m, diag)
                p = jnp.exp(s - lseb[:, h * HEAD_DIM:h * HEAD_DIM + 1])
                dp = lax.dot_general((dov * m).astype(BF16), v, NT_DIMS, preferred_element_type=F32)
                ds = p * (dp - dlb[:, h * HEAD_DIM:h * HEAD_DIM + 1])
                k_h = jnp.where(m > 0, k, jnp.ones_like(k))
                acc_scr[h] = acc_scr[h] + jnp.dot(ds.astype(BF16), k_h, preferred_element_type=F32)

        _on_causal_blocks(qi, ki, step)

        @pl.when(ki == qi)
        def _():
            a0, a1 = acc_scr[0], acc_scr[1]
            dq_ref[...] = (left * a0 + right * a1) * scale
            dcq_ref[...] = left * pltpu.roll(a0, HEAD_DIM, 1) + right * pltpu.roll(a1, HEAD_DIM, 1)

    qspec = pl.BlockSpec((FOX_T, LANES), lambda p, qi, ki: (qi, p))
    kspec = pl.BlockSpec((FOX_T, LANES), lambda p, qi, ki: (jnp.minimum(ki, qi), p))
    vspec = pl.BlockSpec((FOX_T, LANES), lambda p, qi, ki: (jnp.minimum(ki, qi), 6 * N_PAIRS + p))
    ckspec = pl.BlockSpec((1, 8, FOX_T), lambda p, qi, ki: (p, 0, jnp.minimum(ki, qi)))
    return pl.pallas_call(
        body, name=name, grid=(N_PAIRS, nq, nq), in_specs=[qspec, qspec, kspec, vspec, qspec, ckspec, qspec, qspec],
        out_specs=[qspec, qspec], out_shape=[jax.ShapeDtypeStruct((S, MIX_HALF), F32)] * 2,
        scratch_shapes=[pltpu.VMEM((2, FOX_T, LANES), F32)],
        compiler_params=_cparams("parallel", "parallel", "arbitrary"))(do, qn, kn, proj, cum_b, cum_tp, lse_b, delta_b)


def _fox_bwd_dkv(do, qn, kn, proj, cum_b, cum_tp, lse_b, delta_b, name):
    S = qn.shape[0]
    nq = S // FOX_T

    def body(do_ref, q_ref, k_ref, v_ref, cq_ref, ck_ref, lse_ref, dl_ref, dk_ref, dv_ref, dck_ref, dk_scr, dv_scr):
        ki, qi = pl.program_id(1), pl.program_id(2)

        @pl.when(qi == 0)
        def _():
            dk_scr[...] = jnp.zeros_like(dk_scr)
            dv_scr[...] = jnp.zeros_like(dv_scr)

        left, right = _half_masks(FOX_T)

        def step(diag):
            q, k, v, dov = q_ref[...], k_ref[...], v_ref[...].astype(BF16), do_ref[...]
            cqb, lseb, dlb = cq_ref[...], lse_ref[...], dl_ref[...]
            dob = dov.astype(BF16)
            dv = dv_scr[...]
            for h, m in enumerate((left, right)):
                s = _fox_scores(q, k, cqb, ck_ref, h, m, diag)
                p = jnp.exp(s - lseb[:, h * HEAD_DIM:h * HEAD_DIM + 1])
                dp = lax.dot_general((dov * m).astype(BF16), v, NT_DIMS, preferred_element_type=F32)
                ds = p * (dp - dlb[:, h * HEAD_DIM:h * HEAD_DIM + 1])
                dv = dv + m * lax.dot_general(p.astype(BF16), dob, TN_DIMS, preferred_element_type=F32)
                q_h = jnp.where(m > 0, q, jnp.ones_like(q))
                dk_scr[h] = dk_scr[h] + lax.dot_general(ds.astype(BF16), q_h, TN_DIMS, preferred_element_type=F32)
            dv_scr[...] = dv

        _on_causal_blocks(qi, ki, step)

        @pl.when(qi == nq - 1)
        def _():
            a0, a1 = dk_scr[0], dk_scr[1]
            dk_ref[...] = left * a0 + right * a1
            dv_ref[...] = dv_scr[...]
            dck_ref[...] = left * pltpu.roll(a0, HEAD_DIM, 1) + right * pltpu.roll(a1, HEAD_DIM, 1)

    qspec = pl.BlockSpec((FOX_T, LANES), lambda p, ki, qi: (jnp.maximum(qi, ki), p))
    kspec = pl.BlockSpec((FOX_T, LANES), lambda p, ki, qi: (ki, p))
    vspec = pl.BlockSpec((FOX_T, LANES), lambda p, ki, qi: (ki, 6 * N_PAIRS + p))
    ckspec = pl.BlockSpec((1, 8, FOX_T), lambda p, ki, qi: (p, 0, ki))
    full = jax.ShapeDtypeStruct((S, MIX_HALF), F32)
    return pl.pallas_call(
        body, name=name, grid=(N_PAIRS, nq, nq), in_specs=[qspec, qspec, kspec, vspec, qspec, ckspec, qspec, qspec],
        out_specs=[kspec, kspec, kspec], out_shape=[full, full, full],
        scratch_shapes=[pltpu.VMEM((2, FOX_T, LANES), F32), pltpu.VMEM((FOX_T, LANES), F32)],
        compiler_params=_cparams("parallel", "parallel", "arbitrary"))(do, qn, kn, proj, cum_b, cum_tp, lse_b, delta_b)


def _ff_bwd(rc, f_t, name):
    def body(rc_ref, f_ref, d_ref, s_ref):
        d = rc_ref[...] * (1.0 - _sigmoid(f_ref[...]))
        d_ref[...] = d
        s_ref[...] = jnp.sum(d, axis=1, keepdims=True)

    return pl.pallas_call(body, name=name, out_shape=[jax.ShapeDtypeStruct(rc.shape, F32), jax.ShapeDtypeStruct((rc.shape[0], 1), F32)])(rc, f_t)


def _fox_block_fwd(proj, b_f, q_gain, k_gain, bavg, pfx):
    S = proj.shape[0]

    def prep(qv, kv, ba, qg, kg):
        return _head_rms(qv, ba)[0] * qg * (HEAD_DIM ** -0.5), _head_rms(kv, ba)[0] * kg

    qn, kn = _ew(prep, pfx + "_prep", [(proj, MIX_HALF, 4), (proj, MIX_HALF, 5)], consts=[bavg, q_gain, k_gain],
                 outs=[(MIX_HALF, BF16), (MIX_HALF, BF16)])
    f0 = 7 * MIX_HALF + GLA_RANK
    f_t = proj[:, f0:f0 + FOX_HEADS].T + b_f.reshape(FOX_HEADS, 1)
    cum = _cum_lanes(f_t, pfx + "_cum", False, pre=_log_sigmoid)
    cum_b = jnp.repeat(cum.T, HEAD_DIM, axis=1)
    cum_tp = jnp.pad(cum.reshape(N_PAIRS, 2, S), ((0, 0), (0, 6), (0, 0)))
    o, lse_b = _fox_fwd(qn, kn, proj, cum_b, cum_tp, pfx + "_attn")
    return o, dict(qn=qn, kn=kn, f_t=f_t, cum_b=cum_b, cum_tp=cum_tp, o=o, lse_b=lse_b)


def _fox_block_bwd(do, proj, q_gain, k_gain, bavg, res, pfx):
    qn, kn, o = res["qn"], res["kn"], res["o"]
    S = proj.shape[0]
    delta_b, = _ew(lambda a, b, ba: _head_mean(a * b, ba) * float(HEAD_DIM), pfx + "_delta", [do, o], consts=[bavg], outs=[(MIX_HALF, F32)])
    args = (do, qn, kn, proj, res["cum_b"], res["cum_tp"], res["lse_b"], delta_b)
    dqn, dcq_b = _fox_bwd_dq(*args, pfx + "_dq")
    dkn, dv, dck_b = _fox_bwd_dkv(*args, pfx + "_dkv")

    def prep_bwd(dq, dk, qv, kv, ba, qg, kg):
        nq, rq = _head_rms(qv, ba)
        nk, rk = _head_rms(kv, ba)
        return (_head_rms_bwd(dq * qg, nq, rq, ba), _head_rms_bwd(dk * kg, nk, rk, ba),
                jnp.sum(dq * nq, axis=0, keepdims=True), jnp.sum(dk * nk, axis=0, keepdims=True))

    dfq, dfk, dqg, dkg = _ew(prep_bwd, pfx + "_prep_bwd", [dqn, dkn, (proj, MIX_HALF, 4), (proj, MIX_HALF, 5)],
                             consts=[bavg, q_gain, k_gain], outs=[(MIX_HALF, F32), (MIX_HALF, F32)], sums=[MIX_HALF, MIX_HALF])
    dcum = (dcq_b - dck_b)[:, ::HEAD_DIM].T
    rc = _cum_lanes(dcum, pfx + "_rcum", True)
    dff_t, db_f = _ff_bwd(rc, res["f_t"], pfx + "_ff_bwd")
    grads = dict(b_f=db_f.reshape(-1), q_gain=dqg.reshape(-1, HEAD_DIM), k_gain=dkg.reshape(-1, HEAD_DIM))
    return (dfq, dfk, dv, dff_t.T), grads


WEIGHTS = ['ada_w', 'ada_b', 'even_w_in', 'even_w_out', 'gla_w_lr', 'gla_b_lr', 'gla_gain', 'fox_b_f', 'fox_q_gain', 'fox_k_gain',
           'odd_w_in', 'odd_w_out', 's5_lam_re', 's5_lam_im', 's5_log_dt', 's5_b_re', 's5_b_im', 's5_c_re', 's5_c_im', 's5_d',
           's5_w_glu', 's5_b_glu', 'sgu_ln_gain', 'sgu_ln_bias', 'sgu_w_s', 'sgu_b_s', 'mlp_w1', 'mlp_w2']
ARGS = ['x', 'c'] + WEIGHTS + ['loss_target'] + ['m_' + w for w in WEIGHTS] + ['v_' + w for w in WEIGHTS]

EVEN_COLS = 3608
EVEN_PAD = 8 * MIX_HALF
MOD = 6 * D_MODEL
MOD_SHARD = MOD // N_CHIPS

SHARDED = [("even_w_in", (1, 1024, 902), 2), ("even_w_out", (1, 256, 1024), 1), ("odd_w_in", (1, 1024, 384), 2),
           ("odd_w_out", (1, 256, 1024), 1), ("mlp_w1", (2, 1024, 1024), 2), ("mlp_w2", (2, 1024, 1024), 1),
           ("gla_w_lr", (1, 16, 128), 2), ("s5_w_glu", (1, 128, 512), 1), ("s5_b_glu", (1, 128), 1),
           ("sgu_ln_gain", (1, 128), 1), ("sgu_ln_bias", (1, 128), 1)]
PACK_COLS = 512
PACK_ROWS = 12288
REPLICATED = [("gla_b_lr", (1, 512)), ("gla_gain", (1, 8, 64)), ("fox_b_f", (1, 8)), ("fox_q_gain", (1, 8, 64)),
              ("fox_k_gain", (1, 8, 64)), ("s5_lam_re", (1, 32, 64)), ("s5_lam_im", (1, 32, 64)), ("s5_log_dt", (1, 32)),
              ("s5_b_re", (1, 32, 64, 16)), ("s5_b_im", (1, 32, 64, 16)), ("s5_c_re", (1, 32, 16, 64)), ("s5_c_im", (1, 32, 16, 64)),
              ("s5_d", (1, 32, 16)), ("sgu_w_s", (1, 8, 128, 128)), ("sgu_b_s", (1, 8, 128))]
SMALL_ROWS = 768
BIG_ADAM = {"ada_w": (2048, 1536), "even_w_in": (1024, 902), "even_w_out": (256, 1024), "odd_w_in": (1024, 384),
            "odd_w_out": (256, 1024), "mlp_w1": (2048, 1024), "mlp_w2": (2048, 1024), "s5_w_glu": (128, 512)}


PACK_ALIGN = 16


def _piece_rows(shape):
    rows = -(-math.prod(shape) // PACK_COLS)
    return -(-rows // PACK_ALIGN) * PACK_ALIGN


def _to_rows(p, lead=()):
    n = math.prod(p.shape[len(lead):])
    rows = _piece_rows(p.shape[len(lead):])
    flat = p.reshape(lead + (n,))
    if rows * PACK_COLS != n:
        flat = jnp.pad(flat, [(0, 0)] * len(lead) + [(0, rows * PACK_COLS - n)])
    return flat.reshape(lead + (rows, PACK_COLS))


def _from_rows(x, r0, shape, lead=()):
    n = math.prod(shape)
    seg = lax.slice_in_dim(x, r0, r0 + _piece_rows(shape), axis=len(lead)).reshape(lead + (-1,))
    return lax.slice_in_dim(seg, 0, n, axis=len(lead)).reshape(lead + tuple(shape))


def _pack_rows(pieces, rows):
    x = jnp.concatenate([_to_rows(p) for p in pieces], axis=0)
    return jnp.pad(x, ((0, rows - x.shape[0]), (0, 0)))


def _unpack(x, specs):
    out, r0 = {}, 0
    for name, shape in specs:
        out[name] = _from_rows(x, r0, shape)
        r0 += _piece_rows(shape)
    return out


def _shards_to_full(x4):
    out, r0 = {}, 0
    for name, shape, axis in SHARDED:
        seg = _from_rows(x4, r0, shape, lead=(N_CHIPS,))
        out[name] = jnp.concatenate([seg[k] for k in range(N_CHIPS)], axis=axis)
        r0 += _piece_rows(shape)
    return out


def _full_to_shards(full):
    blocks = [_to_rows(jnp.stack(jnp.split(full[name], N_CHIPS, axis=axis)), lead=(N_CHIPS,)) for name, _, axis in SHARDED]
    x = jnp.concatenate(blocks, axis=1)
    return jnp.pad(x, ((0, 0), (0, PACK_ROWS - x.shape[1]), (0, 0)))


def _relu2(t):
    r = jnp.maximum(t, 0.0)
    return r * r


def _silu(t):
    return t * _sigmoid(t)


def _pack_even(w):
    return jnp.concatenate([w[:, :2048], w[:, 2064:3600], w[:, 2048:2064], w[:, 3600:3608],
                            jnp.zeros((w.shape[0], EVEN_PAD - EVEN_COLS), w.dtype)], axis=1)


def _unpack_even(wp):
    return jnp.concatenate([wp[:, :2048], wp[:, 3584:3600], wp[:, 2048:3584], wp[:, 3600:3608]], axis=1)


def _mlp_fwd(h, w1, w2, pfx):
    pre = _mm(h, w1, "nn", pfx + "_up")
    return pre, _mm(pre, w2, "nn", pfx + "_down", a_pro=_relu2)


def _mlp_bwd(dm, h, pre, w1, w2, pfx):
    dpre = _mm(dm, w2, "nt", pfx + "_dpre", epi=lambda acc, p: acc * (2.0 * jnp.maximum(p, 0.0)), extras=[(pre, "mn")], out_dtype=BF16)
    dw2 = _mm(pre, dm, "tn", pfx + "_dw2", a_pro=_relu2)
    dw1 = _mm(h, dpre, "tn", pfx + "_dw1")
    dh = _mm(dpre, w1, "nt", pfx + "_dh")
    return dh, dw1, dw2


def _step(args):
    a = dict(zip(ARGS, args, strict=True))
    x0 = a["x"][0]
    target = a["loss_target"][0]
    mx, my, mc = lax.axis_index("x"), lax.axis_index("y"), lax.axis_index("c")
    chip = 2 * mx + my
    dev = 2 * chip + mc
    bavg = _head_avg_matrix(MIX_HALF)

    c_all = _gather8(jnp.pad(a["c"], ((0, 7), (0, 0))), "c_gather")[:, :, 0, :].reshape(2 * N_CHIPS, D_MODEL)
    ada_b_shard = lax.dynamic_slice_in_dim(a["ada_b"], chip * MOD_SHARD, MOD_SHARD, axis=1)
    mod_sh = [_mm(c_all, a["ada_w"][l], "nn", f"mod{l}", a_pro=_silu, epi=lambda acc, b: acc + b, extras=[(ada_b_shard[l:l + 1], "n")])
              for l in range(2)]
    small3 = jnp.zeros((8, MOD_SHARD), F32)
    for r, n in enumerate(("s5_b_glu", "sgu_ln_gain", "sgu_ln_bias")):
        small3 = small3.at[r, :LANES].set(a[n][0])
    mod_all = _chip_exchange(jnp.concatenate(mod_sh + [small3]), "mod_gather", True)
    mods = []
    for l in range(2):
        full = mod_all[:, 8 * l:8 * l + 8].transpose(1, 0, 2).reshape(8, MOD)
        mods.append(jnp.split(lax.dynamic_slice_in_dim(full, dev, 1, axis=0), 6, axis=1))
    b_glu, ln_gain, ln_bias = (mod_all[:, 16 + r, :LANES].reshape(1, MIX_HALF) for r in range(3))

    shard = _pack_rows([a[n] for n, _, _ in SHARDED], PACK_ROWS).astype(BF16)
    half = lax.dynamic_slice_in_dim(shard, mc * (PACK_ROWS // 2), PACK_ROWS // 2, axis=0)
    collected = _chip_exchange(half, "w_chips", True)
    halves = _by_core(collected, _pair_swap(collected, "w_pair"))
    w = _shards_to_full(halves.transpose(1, 0, 2, 3).reshape(N_CHIPS, PACK_ROWS, PACK_COLS))
    w_even = _pack_even(w["even_w_in"][0])
    w_lr_pad = jnp.zeros((MIX_HALF, MIX_HALF), BF16).at[:GLA_RANK].set(w["gla_w_lr"][0])
    gla_b_lr = a["gla_b_lr"]
    gla_gain, q_gain, k_gain = (a[n].reshape(1, MIX_HALF) for n in ("gla_gain", "fox_q_gain", "fox_k_gain"))
    s5w = dict(lam_re=a["s5_lam_re"][0], lam_im=a["s5_lam_im"][0], log_dt=a["s5_log_dt"][0], b_re=a["s5_b_re"][0], b_im=a["s5_b_im"][0],
               c_re=a["s5_c_re"][0], c_im=a["s5_c_im"][0], d=a["s5_d"][0], w_glu=w["s5_w_glu"][0], b_glu=b_glu)
    sgu_wm, sgu_bt = _sgu_tables(a["sgu_w_s"][0], a["sgu_b_s"][0])

    sh1, sc1, g1, sh2, sc2, g2 = mods[0]
    _, h1_0 = _res_rms(x0, sc1, sh1, "l0_norm1")
    proj0 = _mm(h1_0, w_even, "nn", "l0_proj")
    og, gla_res = _gla_block_fwd(proj0, w_lr_pad, gla_b_lr, gla_gain, bavg, "gla")
    of, fox_res = _fox_block_fwd(proj0, a["fox_b_f"][0], q_gain, k_gain, bavg, "fox")
    mixed0 = jnp.concatenate([og, of], axis=1).astype(BF16)
    y0 = _mm(mixed0, w["even_w_out"][0], "nn", "l0_out")
    x1, h2_0 = _res_rms(x0, sc2, sh2, "l0_norm2", y=y0, g=g1)
    pre0, m0 = _mlp_fwd(h2_0, w["mlp_w1"][0], w["mlp_w2"][0], "l0_mlp")
    sh1b, sc1b, g1b, sh2b, sc2b, g2b = mods[1]
    x2, h1_1 = _res_rms(x1, sc1b, sh1b, "l1_norm1", y=m0, g=g2)
    proj1 = _mm(h1_1, w["odd_w_in"][0], "nn", "l1_proj")
    ys5, s5_res = _s5_block_fwd(proj1[:, :MIX_HALF], s5w, "s5")
    ysgu = _sgu_fwd(proj1, ln_gain, ln_bias, sgu_wm, sgu_bt, "sgu")
    mixed1 = jnp.concatenate([ys5, ysgu], axis=1).astype(BF16)
    y1 = _mm(mixed1, w["odd_w_out"][0], "nn", "l1_out")
    x3, h2_1 = _res_rms(x2, sc2b, sh2b, "l1_norm2", y=y1, g=g1b)
    pre1, m1 = _mlp_fwd(h2_1, w["mlp_w1"][1], w["mlp_w2"][1], "l1_mlp")
    loss_b, dx4, dm1, dg2b = _res_loss(x3, m1, g2b, target, "loss")
    loss = lax.psum(loss_b[0, 0], ("x", "y", "c"))

    full = {}
    dh2_1, dw1_1, dw2_1 = _mlp_bwd(dm1, h2_1, pre1, w["mlp_w1"][1], w["mlp_w2"][1], "l1_mlp")
    dx3, dy1, dg1b, dsc2b, dsh2b = _res_rms_bwd(x3, dh2_1, sc2b, dx4, "l1_norm2_bwd", y=y1, g=g1b)
    dmixed1 = _mm(dy1, w["odd_w_out"][0], "nt", "l1_out_dx")
    full["odd_w_out"] = _mm(mixed1, dy1, "tn", "l1_out_dw")[None]
    du, s5g = _s5_block_bwd(dmixed1[:, :MIX_HALF], s5w, s5_res, "s5")
    dzu, dzv, dws, dbt, dlg, dlb = _sgu_bwd(dmixed1[:, MIX_HALF:], proj1, ln_gain, ln_bias, sgu_wm, sgu_bt, "sgu_bwd")
    g_ws, g_bs = _sgu_grads(dws, dbt)
    dproj1 = jnp.concatenate([du, dzu, dzv], axis=1).astype(BF16)
    full["odd_w_in"] = _mm(h1_1, dproj1, "tn", "l1_proj_dw")[None]
    dh1_1 = _mm(dproj1, w["odd_w_in"][0], "nt", "l1_proj_dx")
    dx2, dm0, dg2, dsc1b, dsh1b = _res_rms_bwd(x2, dh1_1, sc1b, dx3, "l1_norm1_bwd", y=m0, g=g2)
    dh2_0, dw1_0, dw2_0 = _mlp_bwd(dm0, h2_0, pre0, w["mlp_w1"][0], w["mlp_w2"][0], "l0_mlp")
    full["mlp_w1"] = jnp.stack([dw1_0, dw1_1])
    full["mlp_w2"] = jnp.stack([dw2_0, dw2_1])
    dx1, dy0, dg1, dsc2, dsh2 = _res_rms_bwd(x1, dh2_0, sc2, dx2, "l0_norm2_bwd", y=y0, g=g1)
    dmixed0 = _mm(dy0, w["even_w_out"][0], "nt", "l0_out_dx")
    full["even_w_out"] = _mm(mixed0, dy0, "tn", "l0_out_dw")[None]
    (dgq, dgk, dgv, dgg, dsmall), glag = _gla_block_bwd(dmixed0[:, :MIX_HALF], proj0, w_lr_pad, gla_gain, bavg, gla_res, "gla")
    (dfq, dfk, dfv, dff), foxg = _fox_block_bwd(dmixed0[:, MIX_HALF:], proj0, q_gain, k_gain, bavg, fox_res, "fox")
    dsmall = lax.dynamic_update_slice(dsmall, dff, (0, GLA_RANK))
    dproj0 = jnp.concatenate([dgq, dgk, dgv, dgg, dfq, dfk, dfv, dsmall], axis=1).astype(BF16)
    full["even_w_in"] = _unpack_even(_mm(h1_0, dproj0, "tn", "l0_proj_dw"))[None]
    dh1_0 = _mm(dproj0, w_even, "nt", "l0_proj_dx")
    grad_x, dsc1, dsh1 = _res_rms_bwd(x0, dh1_0, sc1, dx1, "l0_norm1_bwd")
    full["gla_w_lr"] = glag["w_lr"][None]
    full["s5_w_glu"] = s5g["w_glu"][None]
    full["s5_b_glu"] = s5g["b_glu"][None]
    full["sgu_ln_gain"] = dlg
    full["sgu_ln_bias"] = dlb

    dmod = jnp.concatenate([dsh1, dsc1, dg1, dsh2, dsc2, dg2, dsh1b, dsc1b, dg1b, dsh2b, dsc2b, dg2b], axis=1)
    dmod_all = _gather8(jnp.pad(dmod, ((0, 7), (0, 0))), "dmod_gather")[:, :, 0, :].reshape(2 * N_CHIPS, 2, MOD)
    grads = {}
    grads["ada_w"] = jnp.stack([
        _mm(c_all, lax.dynamic_slice_in_dim(dmod_all[:, l], chip * MOD_SHARD, MOD_SHARD, axis=1), "tn", f"ada_dw{l}", a_pro=_silu)
        for l in range(2)])
    grads["ada_b"] = _sum_slots(dmod_all.reshape(2 * N_CHIPS, 2 * MOD // PACK_COLS, PACK_COLS), "ada_db").reshape(2, MOD)

    packed = _full_to_shards(full)
    hr = PACK_ROWS // 2
    mine = lax.dynamic_slice_in_dim(packed, mc * hr, hr, axis=1)
    other = lax.dynamic_slice_in_dim(packed, (1 - mc) * hr, hr, axis=1)
    theirs = _pair_swap(other, "g_pair")
    pair_sum, = _ew(lambda p, q: p + q, "g_pair_sum", [mine.reshape(N_CHIPS * hr, PACK_COLS), theirs.reshape(N_CHIPS * hr, PACK_COLS)],
                    outs=[(PACK_COLS, BF16)])
    arrived = _chip_exchange(pair_sum.reshape(N_CHIPS, hr, PACK_COLS), "g_chips", False)
    red_half = _sum_slots(arrived, "g_chip_sum")
    reduced = _by_core(red_half, _pair_swap(red_half, "g_pair_out")).reshape(PACK_ROWS, PACK_COLS)
    grads.update(_unpack(reduced, [(n, s) for n, s, _ in SHARDED]))

    part = dict(gla_b_lr=glag["b_lr"], gla_gain=glag["gain"], fox_b_f=foxg["b_f"], fox_q_gain=foxg["q_gain"], fox_k_gain=foxg["k_gain"],
                s5_lam_re=s5g["lam_re"], s5_lam_im=s5g["lam_im"], s5_log_dt=s5g["log_dt"], s5_b_re=s5g["b_re"], s5_b_im=s5g["b_im"],
                s5_c_re=s5g["c_re"], s5_c_im=s5g["c_im"], s5_d=s5g["d"], sgu_w_s=g_ws, sgu_b_s=g_bs)
    parts_all = _gather8(_pack_rows([part[n] for n, _ in REPLICATED], SMALL_ROWS), "rep_gather")
    rep = _sum_slots(parts_all.reshape(2 * N_CHIPS, SMALL_ROWS, PACK_COLS), "rep_sum")
    grads.update(_unpack(rep, REPLICATED))

    delta, new_m, new_v = {}, {}, {}
    for n, shape2 in BIG_ADAM.items():
        d, nm, nv = _adamw(a[n].reshape(shape2), grads[n].reshape(shape2), a["m_" + n].reshape(shape2), a["v_" + n].reshape(shape2), "adamw_" + n)
        delta[n], new_m[n], new_v[n] = (t.reshape(a[n].shape) for t in (d, nm, nv))
    small = [n for n in WEIGHTS if n not in BIG_ADAM]
    spec = [(n, a[n].shape) for n in small]
    packs = [_pack_rows([src[n] for n in small], SMALL_ROWS) for src in
             (a, grads, {n: a["m_" + n] for n in small}, {n: a["v_" + n] for n in small})]
    for tgt, res in zip((delta, new_m, new_v), _adamw(*packs, "adamw_small")):
        tgt.update(_unpack(res, spec))
    outs = [loss, grad_x[None]]
    for group in (grads, delta, new_m, new_v):
        outs += [group[n].reshape(a[n].shape) for n in WEIGHTS]
    return tuple(outs)


def kernel(x, c, ada_w, ada_b, even_w_in, even_w_out, gla_w_lr, gla_b_lr, gla_gain, fox_b_f, fox_q_gain, fox_k_gain, odd_w_in,
           odd_w_out, s5_lam_re, s5_lam_im, s5_log_dt, s5_b_re, s5_b_im, s5_c_re, s5_c_im, s5_d, s5_w_glu, s5_b_glu, sgu_ln_gain,
           sgu_ln_bias, sgu_w_s, sgu_b_s, mlp_w1, mlp_w2, loss_target, m_ada_w, m_ada_b, m_even_w_in, m_even_w_out, m_gla_w_lr,
           m_gla_b_lr, m_gla_gain, m_fox_b_f, m_fox_q_gain, m_fox_k_gain, m_odd_w_in, m_odd_w_out, m_s5_lam_re, m_s5_lam_im,
           m_s5_log_dt, m_s5_b_re, m_s5_b_im, m_s5_c_re, m_s5_c_im, m_s5_d, m_s5_w_glu, m_s5_b_glu, m_sgu_ln_gain, m_sgu_ln_bias,
           m_sgu_w_s, m_sgu_b_s, m_mlp_w1, m_mlp_w2, v_ada_w, v_ada_b, v_even_w_in, v_even_w_out, v_gla_w_lr, v_gla_b_lr,
           v_gla_gain, v_fox_b_f, v_fox_q_gain, v_fox_k_gain, v_odd_w_in, v_odd_w_out, v_s5_lam_re, v_s5_lam_im, v_s5_log_dt,
           v_s5_b_re, v_s5_b_im, v_s5_c_re, v_s5_c_im, v_s5_d, v_s5_w_glu, v_s5_b_glu, v_sgu_ln_gain, v_sgu_ln_bias, v_sgu_w_s,
           v_sgu_b_s, v_mlp_w1, v_mlp_w2):
    return _step((x, c, ada_w, ada_b, even_w_in, even_w_out, gla_w_lr, gla_b_lr, gla_gain, fox_b_f, fox_q_gain, fox_k_gain,
                  odd_w_in, odd_w_out, s5_lam_re, s5_lam_im, s5_log_dt, s5_b_re, s5_b_im, s5_c_re, s5_c_im, s5_d, s5_w_glu,
                  s5_b_glu, sgu_ln_gain, sgu_ln_bias, sgu_w_s, sgu_b_s, mlp_w1, mlp_w2, loss_target, m_ada_w, m_ada_b,
                  m_even_w_in, m_even_w_out, m_gla_w_lr, m_gla_b_lr, m_gla_gain, m_fox_b_f, m_fox_q_gain, m_fox_k_gain,
                  m_odd_w_in, m_odd_w_out, m_s5_lam_re, m_s5_lam_im, m_s5_log_dt, m_s5_b_re, m_s5_b_im, m_s5_c_re, m_s5_c_im,
                  m_s5_d, m_s5_w_glu, m_s5_b_glu, m_sgu_ln_gain, m_sgu_ln_bias, m_sgu_w_s, m_sgu_b_s, m_mlp_w1, m_mlp_w2, v_ada_w,
                  v_ada_b, v_even_w_in, v_even_w_out, v_gla_w_lr, v_gla_b_lr, v_gla_gain, v_fox_b_f, v_fox_q_gain, v_fox_k_gain,
                  v_odd_w_in, v_odd_w_out, v_s5_lam_re, v_s5_lam_im, v_s5_log_dt, v_s5_b_re, v_s5_b_im, v_s5_c_re, v_s5_c_im,
                  v_s5_d, v_s5_w_glu, v_s5_b_glu, v_sgu_ln_gain, v_sgu_ln_bias, v_sgu_w_s, v_sgu_b_s, v_mlp_w1, v_mlp_w2))
```

```python
import functools
import math

import jax
import jax.numpy as jnp
import numpy as np
from jax import lax
from jax.experimental import pallas as pl
from jax.experimental.pallas import tpu as pltpu

F32 = jnp.float32
BF16 = jnp.bfloat16
MESH = pl.DeviceIdType.MESH
ANY = pl.BlockSpec(memory_space=pl.ANY)
DMA_SEM = pltpu.SemaphoreType.DMA

D_MODEL = 1024
HEAD_DIM = 64
MIX_HALF = 512
GLA_RANK = 16
GLA_TAU = 16.0
GLA_CHUNK = 64
S5_GROUPS = 32
S5_GROUP_WIDTH = 16
S5_STATE = 64
S5_N = S5_GROUPS * S5_STATE
SGU_GROUPS = 8
SGU_CHUNK = 128
D_FF = 4096
EPS = 1e-6
N_CHIPS = 4
LANES = 128
VMEM_LIMIT = 48 * 1024 * 1024
PAIR_COPIES = 16

ADAM_LR = 0.001
ADAM_B1 = 0.9
ADAM_B2 = 0.999
ADAM_EPS = 1e-08
ADAM_WD = 0.01
ADAM_STEP = 10


def _cparams(*sem):
    return pltpu.CompilerParams(dimension_semantics=sem, vmem_limit_bytes=VMEM_LIMIT)


def _pair_swap(x, name):
    lead = x.shape[:-2]
    rows = x.shape[-2]
    nsplit = max(1, PAIR_COPIES // max(1, math.prod(lead)))
    while nsplit > 1 and rows % (nsplit * 16):
        nsplit -= 1
    pieces = [idx + (pl.ds(j * (rows // nsplit), rows // nsplit),) for idx in np.ndindex(*lead) for j in range(nsplit)]

    def body(x_ref, o_ref, send_sems, recv_sems):
        mx, my, mc = lax.axis_index("x"), lax.axis_index("y"), lax.axis_index("c")
        copies = [pltpu.make_async_remote_copy(src_ref=x_ref.at[p], dst_ref=o_ref.at[p], send_sem=send_sems.at[j], recv_sem=recv_sems.at[j],
                                               device_id=(mx, my, 1 - mc), device_id_type=MESH) for j, p in enumerate(pieces)]
        for cp in copies:
            cp.start()
        for cp in copies:
            cp.wait_recv()
        for cp in copies:
            cp.wait_send()

    return pl.pallas_call(
        body, name=name, out_shape=jax.ShapeDtypeStruct(x.shape, x.dtype), in_specs=[ANY], out_specs=ANY,
        scratch_shapes=[DMA_SEM((len(pieces),)), DMA_SEM((len(pieces),))])(x)


def _by_core(mine, theirs):
    first = lax.axis_index("c") == 0
    return jnp.stack([jnp.where(first, mine, theirs), jnp.where(first, theirs, mine)])


def _chip_exchange(x, name, bcast):
    blk = x.shape if bcast else x.shape[1:]

    def body(x_ref, o_ref, send_sems, recv_sems, loc_sem):
        mx, my, mc = lax.axis_index("x"), lax.axis_index("y"), lax.axis_index("c")
        me = 2 * mx + my
        peers = [(1 - mx, my), (mx, 1 - my), (1 - mx, 1 - my)]

        def src(k):
            return x_ref if bcast else x_ref.at[k]

        loc = pltpu.make_async_copy(src(me), o_ref.at[me], loc_sem)
        loc.start()
        sends = []
        for j, (px, py) in enumerate(peers):
            cp = pltpu.make_async_remote_copy(src_ref=src(2 * px + py), dst_ref=o_ref.at[me], send_sem=send_sems.at[j],
                                              recv_sem=recv_sems.at[j], device_id=(px, py, mc), device_id_type=MESH)
            cp.start()
            sends.append(cp)
        for j, (px, py) in enumerate(peers):
            pltpu.make_async_remote_copy(src_ref=src(me), dst_ref=o_ref.at[2 * px + py], send_sem=send_sems.at[j],
                                         recv_sem=recv_sems.at[j], device_id=(px, py, mc), device_id_type=MESH).wait_recv()
        for cp in sends:
            cp.wait_send()
        loc.wait()

    return pl.pallas_call(
        body, name=name, out_shape=jax.ShapeDtypeStruct((N_CHIPS,) + tuple(blk), x.dtype), in_specs=[ANY], out_specs=ANY,
        scratch_shapes=[DMA_SEM((3,)), DMA_SEM((3,)), DMA_SEM])(x)


def _gather8(x, name):
    return _chip_exchange(_by_core(x, _pair_swap(x, name + "_pair")), name + "_chips", True)


def _tile(n, want):
    if n <= want:
        return n
    t = (want // LANES) * LANES
    while t >= LANES:
        if n % t == 0:
            return t
        t -= LANES
    raise ValueError(f"no lane-aligned tile for {n}")


_DIMS = {"nn": (((1,), (0,)), ((), ())), "nt": (((1,), (1,)), ((), ())), "tn": (((0,), (0,)), ((), ()))}


MM_FULL_K = 4096
MM_SLAB_K = 2048
MM_TILES = ((1024, 1024), (512, 1024), (1024, 512), (512, 512), (256, 512), (256, 256))
MM_VMEM_BUDGET = 36 * 1024 * 1024


def _mm(a, b, mode, name, *, a_pro=None, epi=None, extras=(), out_dtype=F32, tm_max=1024, tn_max=1024, tk=None, a_cols=None):
    c0, csize = a_cols if a_cols is not None else (0, a.shape[1])
    if mode == "tn":
        K, M = a.shape[0], csize
    else:
        M, K = a.shape[0], csize
    N = b.shape[0] if mode == "nt" else b.shape[1]
    assert (b.shape[1] if mode == "nt" else b.shape[0]) == K, (a.shape, b.shape, mode)
    if tk is None:
        tk = K if (mode != "tn" and K <= MM_FULL_K) else MM_SLAB_K
    tk = _tile(K, tk)
    nk = K // tk
    n_mn = sum(1 for _, kind in extras if kind == "mn")
    for tm_want, tn_want in MM_TILES:
        tm, tn = _tile(M, min(tm_want, tm_max)), _tile(N, min(tn_want, tn_max))
        need = 2 * (tm * tk * a.dtype.itemsize + tk * tn * b.dtype.itemsize + tm * tn * 4 * (1 + n_mn)) + tm * tn * 4 * (nk > 1)
        if need <= MM_VMEM_BUDGET:
            break
    if mode == "tn":
        assert c0 % tm == 0
        a_spec = pl.BlockSpec((tk, tm), lambda i, j, k: (k, i + c0 // tm))
    else:
        assert c0 % tk == 0
        a_spec = pl.BlockSpec((tm, tk), lambda i, j, k: (i, k + c0 // tk))
    b_spec = pl.BlockSpec((tn, tk), lambda i, j, k: (j, k)) if mode == "nt" else pl.BlockSpec((tk, tn), lambda i, j, k: (k, j))
    ex_specs = []
    for arr, kind in extras:
        if kind == "mn":
            assert arr.shape == (M, N)
            ex_specs.append(pl.BlockSpec((tm, tn), lambda i, j, k: (i, j)))
        else:
            assert arr.shape == (1, N)
            ex_specs.append(pl.BlockSpec((1, tn), lambda i, j, k: (0, j)))
    n_ex = len(extras)

    def body(*refs):
        a_ref, b_ref = refs[:2]
        ex_refs = refs[2:2 + n_ex]
        o_ref = refs[2 + n_ex]
        acc_ref = refs[3 + n_ex] if nk > 1 else None
        k = pl.program_id(2)
        av = a_ref[...]
        if a_pro is not None:
            av = a_pro(av)
        part = lax.dot_general(av.astype(BF16), b_ref[...].astype(BF16), _DIMS[mode], preferred_element_type=F32)
        if nk == 1:
            if epi is not None:
                part = epi(part, *[r[...] for r in ex_refs])
            o_ref[...] = part.astype(o_ref.dtype)
            return

        @pl.when(k == 0)
        def _():
            acc_ref[...] = part

        @pl.when(k > 0)
        def _():
            acc_ref[...] += part

        @pl.when(k == nk - 1)
        def _():
            acc = acc_ref[...]
            if epi is not None:
                acc = epi(acc, *[r[...] for r in ex_refs])
            o_ref[...] = acc.astype(o_ref.dtype)

    return pl.pallas_call(
        body, name=name, grid=(M // tm, N // tn, nk),
        in_specs=[a_spec, b_spec] + ex_specs,
        out_specs=pl.BlockSpec((tm, tn), lambda i, j, k: (i, j)),
        out_shape=jax.ShapeDtypeStruct((M, N), out_dtype),
        scratch_shapes=[pltpu.VMEM((tm, tn), F32)] if nk > 1 else [],
        compiler_params=_cparams("parallel", "parallel", "arbitrary"))(a, b, *[e[0] for e in extras])


ROWS = 256


def _row_spec(w, ts=ROWS):
    return pl.BlockSpec((ts, w), lambda i: (i, 0))


def _vec_spec(w):
    return pl.BlockSpec((1, w), lambda i: (0, 0))


def _res_rms(x, sc, sh, name, y=None, g=None):
    S, D = x.shape
    has_res = y is not None

    def body(*refs):
        if has_res:
            x_ref, y_ref, g_ref, sc_ref, sh_ref, xo_ref, h_ref = refs
            xv = x_ref[...] + g_ref[...] * y_ref[...]
            xo_ref[...] = xv
        else:
            x_ref, sc_ref, sh_ref, h_ref = refs
            xv = x_ref[...]
        r = lax.rsqrt(jnp.mean(xv * xv, axis=-1, keepdims=True) + EPS)
        h_ref[...] = (xv * r * (1.0 + sc_ref[...]) + sh_ref[...]).astype(BF16)

    row, vec = _row_spec(D), _vec_spec(D)
    if has_res:
        return pl.pallas_call(body, name=name, grid=(S // ROWS,), in_specs=[row, row, vec, vec, vec], out_specs=[row, row],
                              out_shape=[jax.ShapeDtypeStruct((S, D), F32), jax.ShapeDtypeStruct((S, D), BF16)],
                              compiler_params=_cparams("parallel"))(x, y, g, sc, sh)
    h = pl.pallas_call(body, name=name, grid=(S // ROWS,), in_specs=[row, vec, vec], out_specs=row,
                       out_shape=jax.ShapeDtypeStruct((S, D), BF16), compiler_params=_cparams("parallel"))(x, sc, sh)
    return x, h


def _res_rms_bwd(x, dh, sc, dres, name, y=None, g=None):
    S, D = x.shape
    has_res = y is not None

    def body(*refs):
        if has_res:
            x_ref, dh_ref, sc_ref, dres_ref, y_ref, g_ref, dx_ref, dy_ref, dg_ref, dsc_ref, dsh_ref = refs
        else:
            x_ref, dh_ref, sc_ref, dres_ref, dx_ref, dsc_ref, dsh_ref = refs
        first = pl.program_id(0) == 0
        xv = x_ref[...]
        dh = dh_ref[...]
        r = lax.rsqrt(jnp.mean(xv * xv, axis=-1, keepdims=True) + EPS)
        xn = xv * r
        dxn = dh * (1.0 + sc_ref[...])
        dx = dres_ref[...] + r * (dxn - xn * jnp.mean(dxn * xn, axis=-1, keepdims=True))
        dx_ref[...] = dx
        parts = [(dsc_ref, jnp.sum(dh * xn, axis=0, keepdims=True)), (dsh_ref, jnp.sum(dh, axis=0, keepdims=True))]
        if has_res:
            dy_ref[...] = (dx * g_ref[...]).astype(BF16)
            parts.append((dg_ref, jnp.sum(dx * y_ref[...], axis=0, keepdims=True)))
        for ref, val in parts:
            @pl.when(first)
            def _(ref=ref, val=val):
                ref[...] = val

            @pl.when(jnp.logical_not(first))
            def _(ref=ref, val=val):
                ref[...] += val

    row, vec = _row_spec(D), _vec_spec(D)
    full = jax.ShapeDtypeStruct((S, D), F32)
    v = jax.ShapeDtypeStruct((1, D), F32)
    if has_res:
        return pl.pallas_call(body, name=name, grid=(S // ROWS,), in_specs=[row, row, vec, row, row, vec],
                              out_specs=[row, row, vec, vec, vec], out_shape=[full, jax.ShapeDtypeStruct((S, D), BF16), v, v, v],
                              compiler_params=_cparams("arbitrary"))(x, dh, sc, dres, y, g)
    return pl.pallas_call(body, name=name, grid=(S // ROWS,), in_specs=[row, row, vec, row],
                          out_specs=[row, vec, vec], out_shape=[full, v, v],
                          compiler_params=_cparams("arbitrary"))(x, dh, sc, dres)


def _res_loss(x, m, g, target, name):
    S, D = x.shape

    def body(x_ref, m_ref, g_ref, t_ref, loss_ref, dx_ref, dm_ref, dg_ref):
        first = pl.program_id(0) == 0
        mv = m_ref[...]
        err = x_ref[...] + g_ref[...] * mv - t_ref[...]
        dx = err * (1.0 / D)
        dx_ref[...] = dx
        dm_ref[...] = (dx * g_ref[...]).astype(BF16)
        part = 0.5 * jnp.sum(jnp.mean(err * err, axis=-1, keepdims=True), axis=0, keepdims=True)
        dg = jnp.sum(dx * mv, axis=0, keepdims=True)

        @pl.when(first)
        def _():
            loss_ref[...] = jnp.broadcast_to(part, loss_ref.shape)
            dg_ref[...] = dg

        @pl.when(jnp.logical_not(first))
        def _():
            loss_ref[...] += jnp.broadcast_to(part, loss_ref.shape)
            dg_ref[...] += dg

    row, vec = _row_spec(D), _vec_spec(D)
    full = jax.ShapeDtypeStruct((S, D), F32)
    return pl.pallas_call(body, name=name, grid=(S // ROWS,), in_specs=[row, row, vec, row],
                          out_specs=[pl.BlockSpec((8, LANES), lambda i: (0, 0)), row, row, vec],
                          out_shape=[jax.ShapeDtypeStruct((8, LANES), F32), full, jax.ShapeDtypeStruct((S, D), BF16), jax.ShapeDtypeStruct((1, D), F32)],
                          compiler_params=_cparams("arbitrary"))(x, m, g, target)


def _adamw(w, g, m, v, name):
    R, C = w.shape
    tr = R if R <= 256 else 256
    assert R % tr == 0

    def body(w_ref, g_ref, m_ref, v_ref, d_ref, nm_ref, nv_ref):
        gv = g_ref[...]
        nm = ADAM_B1 * m_ref[...] + (1.0 - ADAM_B1) * gv
        nv = ADAM_B2 * v_ref[...] + (1.0 - ADAM_B2) * jnp.square(gv)
        m_hat = nm / (1.0 - ADAM_B1 ** ADAM_STEP)
        v_hat = nv / (1.0 - ADAM_B2 ** ADAM_STEP)
        d_ref[...] = -ADAM_LR * (m_hat / (jnp.sqrt(v_hat) + ADAM_EPS) + ADAM_WD * w_ref[...])
        nm_ref[...] = nm
        nv_ref[...] = nv

    spec = pl.BlockSpec((tr, C), lambda i: (i, 0))
    out = jax.ShapeDtypeStruct((R, C), F32)
    return pl.pallas_call(body, name=name, grid=(R // tr,), in_specs=[spec] * 4, out_specs=[spec] * 3,
                          out_shape=[out, out, out], compiler_params=_cparams("parallel"))(w, g, m, v)


def _sum_slots(x, name):
    n, R, C = x.shape
    tr = R if R <= 256 else 256
    assert R % tr == 0

    def body(x_ref, o_ref):
        acc = x_ref[0].astype(F32)
        for j in range(1, n):
            acc = acc + x_ref[j].astype(F32)
        o_ref[...] = acc

    return pl.pallas_call(body, name=name, grid=(R // tr,), in_specs=[pl.BlockSpec((n, tr, C), lambda i: (0, i, 0))],
                          out_specs=pl.BlockSpec((tr, C), lambda i: (i, 0)), out_shape=jax.ShapeDtypeStruct((R, C), F32),
                          compiler_params=_cparams("parallel"))(x)


def _ew(fn, name, tiled, consts=(), outs=(), sums=(), ts=ROWS):
    tiled = [t if isinstance(t, tuple) else (t, t.shape[1], 0) for t in tiled]
    S = tiled[0][0].shape[0]
    n_t, n_c, n_o, n_s = len(tiled), len(consts), len(outs), len(sums)

    def body(*refs):
        ins = [r[...] for r in refs[:n_t + n_c]]
        res = fn(*ins)
        res = res if isinstance(res, (tuple, list)) else (res,)
        assert len(res) == n_o + n_s
        o_refs = refs[n_t + n_c:]
        for r, val in zip(o_refs[:n_o], res[:n_o]):
            r[...] = val.astype(r.dtype)
        first = pl.program_id(0) == 0
        for r, val in zip(o_refs[n_o:], res[n_o:]):
            @pl.when(first)
            def _(r=r, val=val):
                r[...] = val

            @pl.when(jnp.logical_not(first))
            def _(r=r, val=val):
                r[...] += val

    in_specs = [pl.BlockSpec((ts, w), lambda i, cb=cb: (i, cb)) for _, w, cb in tiled]
    in_specs += [pl.BlockSpec(c.shape, lambda i, nd=c.ndim: (0,) * nd) for c in consts]
    out_specs = [_row_spec(w, ts) for w, _ in outs] + [_vec_spec(w) for w in sums]
    out_shape = [jax.ShapeDtypeStruct((S, w), dt) for w, dt in outs] + [jax.ShapeDtypeStruct((1, w), F32) for w in sums]
    res = pl.pallas_call(body, name=name, grid=(S // ts,), in_specs=in_specs, out_specs=out_specs, out_shape=out_shape,
                         compiler_params=_cparams("arbitrary" if sums else "parallel"))(*[t[0] for t in tiled], *consts)
    return res


_GELU_C = math.sqrt(2.0 / math.pi)


def _gelu(x):
    return 0.5 * x * (1.0 + jnp.tanh(_GELU_C * (x + 0.044715 * x * x * x)))


def _dgelu(x):
    t = jnp.tanh(_GELU_C * (x + 0.044715 * x * x * x))
    return 0.5 * (1.0 + t) + 0.5 * x * (1.0 - t * t) * _GELU_C * (1.0 + 3.0 * 0.044715 * x * x)


def _sigmoid(x):
    return 1.0 / (1.0 + jnp.exp(-x))


def _log_sigmoid(x):
    return jnp.minimum(x, 0.0) - jnp.log(1.0 + jnp.exp(-jnp.abs(x)))


SCAN_T = 128
SCAN_TB = 512


def _cmul(ar, ai, br, bi):
    return ar * br - ai * bi, ar * bi + ai * br


def _s5_discretise(lam_re, lam_im, log_dt, b_re, b_im):
    dt = jnp.exp(log_dt)[:, None]
    mag = jnp.exp(lam_re * dt)
    ang = lam_im * dt
    abar_re = mag * jnp.cos(ang)
    abar_im = mag * jnp.sin(ang)
    den = lam_re * lam_re + lam_im * lam_im
    coef_re = ((abar_re - 1.0) * lam_re + abar_im * lam_im) / den
    coef_im = (abar_im * lam_re - (abar_re - 1.0) * lam_im) / den
    bbar_re = coef_re[..., None] * b_re - coef_im[..., None] * b_im
    bbar_im = coef_re[..., None] * b_im + coef_im[..., None] * b_re
    return abar_re, abar_im, bbar_re, bbar_im


def _planes(re, im):
    lead = re.shape[:-1]
    return jnp.stack([re.reshape(lead + (-1, LANES)), im.reshape(lead + (-1, LANES))], axis=-2).reshape(lead + (-1,))


def _unplanes(x):
    lead = x.shape[:-1]
    x4 = x.reshape(lead + (-1, 2, LANES))
    return x4[..., 0, :].reshape(lead + (-1,)), x4[..., 1, :].reshape(lead + (-1,))


def _s5_scan_tables(a_re, a_im, reverse):
    pr, pi = [a_re], [a_im]
    for _ in range(7):
        r, i = _cmul(pr[-1], pi[-1], pr[-1], pi[-1])
        pr.append(r)
        pi.append(i)
    apow = _planes(jnp.stack(pr), jnp.stack(pi))
    n = np.arange(1, SCAN_T + 1)
    if reverse:
        n = n[::-1]
    tr = jnp.ones((SCAN_T, a_re.shape[0]), F32)
    ti = jnp.zeros((SCAN_T, a_re.shape[0]), F32)
    for k in range(8):
        bit = jnp.asarray(((n >> k) & 1).astype(np.float32))[:, None]
        mr = bit * pr[k][None, :] + (1.0 - bit)
        mi = bit * pi[k][None, :]
        tr, ti = _cmul(tr, ti, mr, mi)
    return apow, _planes(tr, ti)


def _s5_scan(bu, apow, ptab, name, reverse, x_fwd=None):
    S, N2 = bu.shape
    T, W = SCAN_T, 2 * LANES
    tb = min(SCAN_TB, S)
    nt, nsub = S // tb, tb // T
    order = list(range(nsub - 1, -1, -1) if reverse else range(nsub))
    with_da = x_fwd is not None

    def tblk(t):
        return (nt - 1 - t) if reverse else t

    def shifted(v, k, rowi):
        s = 1 << k
        if s < 8:
            if reverse:
                return jnp.where(rowi < T - s, pltpu.roll(v, T - s, 0), 0.0)
            return jnp.where(rowi >= s, pltpu.roll(v, s, 0), 0.0)
        z = jnp.zeros((s, LANES), F32)
        return jnp.concatenate([v[s:], z], axis=0) if reverse else jnp.concatenate([z, v[:T - s]], axis=0)

    def body(*refs):
        if with_da:
            bu_ref, ap_ref, pt_ref, xf_ref, xp_ref, x_ref, da_ref, carry_ref = refs
        else:
            bu_ref, ap_ref, pt_ref, x_ref, carry_ref = refs
        t = pl.program_id(1)

        @pl.when(t == 0)
        def _():
            carry_ref[...] = jnp.zeros_like(carry_ref)
            if with_da:
                da_ref[...] = jnp.zeros_like(da_ref)

        rowi = lax.broadcasted_iota(jnp.int32, (T, LANES), 0)
        pr, pi = pt_ref[:, :LANES], pt_ref[:, LANES:]
        cr, ci = carry_ref[0:1, :LANES], carry_ref[0:1, LANES:]
        for sb in order:
            rows = pl.ds(sb * T, T)
            xr, xi = bu_ref[rows, :LANES], bu_ref[rows, LANES:]
            for k in range(7):
                ar, ai = ap_ref[k:k + 1, :LANES], ap_ref[k:k + 1, LANES:]
                rr, ri = shifted(xr, k, rowi), shifted(xi, k, rowi)
                xr, xi = xr + ar * rr - ai * ri, xi + ar * ri + ai * rr
            xr, xi = xr + pr * cr - pi * ci, xi + pr * ci + pi * cr
            x_ref[rows, :LANES] = xr
            x_ref[rows, LANES:] = xi
            edge = pl.ds(sb * T + (0 if reverse else T - 1), 1)
            cr, ci = x_ref[edge, :LANES], x_ref[edge, LANES:]
            if with_da:
                if sb > 0:
                    before = pl.ds(sb * T - 1, 1)
                    b_r, b_i = xf_ref[before, :LANES], xf_ref[before, LANES:]
                else:
                    keep = (tblk(t) > 0).astype(F32)
                    b_r, b_i = xp_ref[7:8, :LANES] * keep, xp_ref[7:8, LANES:] * keep
                fr, fi = xf_ref[rows, :LANES], xf_ref[rows, LANES:]
                qr = jnp.where(rowi >= 1, pltpu.roll(fr, 1, 0), b_r)
                qi = jnp.where(rowi >= 1, pltpu.roll(fi, 1, 0), b_i)
                gr, gi = xr * qr + xi * qi, xi * qr - xr * qi
                sr, si = gr[0:8], gi[0:8]
                for j in range(1, T // 8):
                    sr, si = sr + gr[8 * j:8 * j + 8], si + gi[8 * j:8 * j + 8]
                da_ref[:, :LANES] += sr
                da_ref[:, LANES:] += si
        carry_ref[0:1, :LANES] = cr
        carry_ref[0:1, LANES:] = ci

    blk = pl.BlockSpec((tb, W), lambda j, t: (tblk(t), j))
    in_specs = [blk, pl.BlockSpec((8, W), lambda j, t: (0, j)), pl.BlockSpec((T, W), lambda j, t: (0, j))]
    out_specs, out_shape = [blk], [jax.ShapeDtypeStruct((S, N2), F32)]
    args = [bu, apow, ptab]
    if with_da:
        in_specs += [blk, pl.BlockSpec((8, W), lambda j, t: (jnp.maximum(tblk(t) * (tb // 8) - 1, 0), j))]
        out_specs.append(pl.BlockSpec((8, W), lambda j, t: (0, j)))
        out_shape.append(jax.ShapeDtypeStruct((8, N2), F32))
        args += [x_fwd, x_fwd]
    res = pl.pallas_call(body, name=name, grid=(N2 // W, nt), in_specs=in_specs, out_specs=out_specs, out_shape=out_shape,
                         scratch_shapes=[pltpu.VMEM((8, W), F32)], compiler_params=_cparams("parallel", "arbitrary"))(*args)
    return res if with_da else res[0]


def _block_diag(t):
    G, a, b = t.shape
    return (t[:, :, None, :] * jnp.eye(G, dtype=t.dtype)[:, None, :, None]).reshape(G * a, G * b)


def _block_diag_take(m, G):
    a, b = m.shape[0] // G, m.shape[1] // G
    m4 = m.reshape(G, a, G, b)
    return jnp.sum(m4 * jnp.eye(G, dtype=m.dtype)[:, None, :, None], axis=2)


def _s5_block_fwd(u, w, pfx):
    a_re, a_im, bb_re, bb_im = _s5_discretise(w["lam_re"], w["lam_im"], w["log_dt"], w["b_re"], w["b_im"])
    bcat = _planes(_block_diag(bb_re).T, _block_diag(bb_im).T).astype(BF16)
    ccat = _planes(_block_diag(jnp.swapaxes(w["c_re"], 1, 2)).T, -_block_diag(jnp.swapaxes(w["c_im"], 1, 2)).T).T.astype(BF16)
    af_re, af_im = a_re.reshape(-1), a_im.reshape(-1)
    apow, ptab = _s5_scan_tables(af_re, af_im, False)
    bu = _mm(u, bcat, "nn", pfx + "_bu")
    x = _s5_scan(bu, apow, ptab, pfx + "_scan", False)
    d_row = w["d"].reshape(1, MIX_HALF)
    ys = _mm(x, ccat, "nn", pfx + "_y", epi=lambda acc, ut, dr: acc + dr * ut, extras=[(u, "mn"), (d_row, "n")])
    z = _mm(ys, w["w_glu"], "nn", pfx + "_glu", a_pro=_gelu, epi=lambda acc, b: acc + b, extras=[(w["b_glu"].reshape(1, -1), "n")])
    y2, = _ew(lambda ysv, zv: _gelu(ysv) * _sigmoid(zv), pfx + "_gate", [ys, z], outs=[(MIX_HALF, F32)])
    return y2, dict(u=u, x=x, ys=ys, z=z, bcat=bcat, ccat=ccat, a=(af_re, af_im), d_row=d_row)


def _s5_block_bwd(dy2, w, res, pfx):
    u, x, ys, z, bcat, ccat = res["u"], res["x"], res["ys"], res["z"], res["bcat"], res["ccat"]

    def gate_bwd(dy, ysv, zv):
        sg = _sigmoid(zv)
        dz = dy * _gelu(ysv) * sg * (1.0 - sg)
        return dz, jnp.sum(dz, axis=0, keepdims=True)

    dz, db_glu = _ew(gate_bwd, pfx + "_gate_bwd", [dy2, ys, z], outs=[(MIX_HALF, F32)], sums=[MIX_HALF])
    dw_glu = _mm(ys, dz, "tn", pfx + "_dwglu", a_pro=_gelu)
    dys = _mm(dz, w["w_glu"], "nt", pfx + "_dys", epi=lambda acc, dy, zv, ysv: (acc + dy * _sigmoid(zv)) * _dgelu(ysv),
              extras=[(dy2, "mn"), (z, "mn"), (ys, "mn")])
    dd, = _ew(lambda a, b: jnp.sum(a * b, axis=0, keepdims=True), pfx + "_dd", [dys, u], sums=[MIX_HALF])
    dccat = _mm(x, dys, "tn", pfx + "_dc")
    dx = _mm(dys, ccat, "nt", pfx + "_dx")
    af_re, af_im = res["a"]
    apow, ptab = _s5_scan_tables(af_re, -af_im, True)
    lam, da8 = _s5_scan(dx, apow, ptab, pfx + "_scan_bwd", True, x_fwd=x)
    dbcat = _mm(u, lam, "tn", pfx + "_db")
    du = _mm(lam, bcat, "nt", pfx + "_du", epi=lambda acc, dyv, dr: acc + dyv * dr, extras=[(dys, "mn"), (res["d_row"], "n")])
    G = S5_GROUPS
    d_abar_re, d_abar_im = (t.reshape(G, S5_STATE) for t in _unplanes(jnp.sum(da8, axis=0)))
    d_bb_re, d_bb_im = (_block_diag_take(t.T, G) for t in _unplanes(dbcat))
    _, vjp = jax.vjp(_s5_discretise, w["lam_re"], w["lam_im"], w["log_dt"], w["b_re"], w["b_im"])
    g_lam_re, g_lam_im, g_log_dt, g_b_re, g_b_im = vjp((d_abar_re, d_abar_im, d_bb_re, d_bb_im))
    dc_re, dc_im = _unplanes(dccat.T)
    g_c_re = jnp.swapaxes(_block_diag_take(dc_re.T, G), 1, 2)
    g_c_im = -jnp.swapaxes(_block_diag_take(dc_im.T, G), 1, 2)
    grads = dict(lam_re=g_lam_re, lam_im=g_lam_im, log_dt=g_log_dt, b_re=g_b_re, b_im=g_b_im, c_re=g_c_re, c_im=g_c_im,
                 d=dd.reshape(G, S5_GROUP_WIDTH), w_glu=dw_glu, b_glu=db_glu.reshape(-1))
    return du, grads


SGU_TS = 512
N_PAIRS = MIX_HALF // LANES


def _half_masks(rows):
    lane = lax.broadcasted_iota(jnp.int32, (rows, LANES), 1)
    left = (lane < HEAD_DIM).astype(F32)
    return left, 1.0 - left


def _sgu_norm(zv, gain, bias):
    v = _gelu(zv)
    mu = jnp.mean(v, axis=-1, keepdims=True)
    vc = v - mu
    rstd = lax.rsqrt(jnp.mean(vc * vc, axis=-1, keepdims=True) + EPS)
    vhat = vc * rstd
    return vhat, rstd, vhat * gain + bias


def _sgu_tables(w_s, b_s):
    mask = jnp.tril(jnp.ones((SGU_CHUNK, SGU_CHUNK), dtype=bool))
    wm = jnp.where(mask[None], w_s, 0.0).astype(BF16)
    bias_tab = jnp.repeat(b_s.T, MIX_HALF // SGU_GROUPS, axis=1)
    return wm, bias_tab


def _sgu_fwd(proj, ln_gain, ln_bias, wm, bias_tab, name):
    S = proj.shape[0]
    nch = SGU_TS // SGU_CHUNK

    def body(zu_ref, zv_ref, g_ref, b_ref, w_ref, bt_ref, o_ref):
        left, right = _half_masks(SGU_CHUNK)
        _, _, vn = _sgu_norm(zv_ref[...], g_ref[...], b_ref[...])
        for ch in range(nch):
            rows = pl.ds(ch * SGU_CHUNK, SGU_CHUNK)
            for p in range(N_PAIRS):
                cols = pl.ds(p * LANES, LANES)
                vp = vn[ch * SGU_CHUNK:(ch + 1) * SGU_CHUNK, p * LANES:(p + 1) * LANES]
                mixed = (jnp.dot(w_ref[2 * p], (vp * left).astype(BF16), preferred_element_type=F32)
                         + jnp.dot(w_ref[2 * p + 1], (vp * right).astype(BF16), preferred_element_type=F32) + bt_ref[:, cols])
                o_ref[rows, cols] = _gelu(zu_ref[rows, cols]) * mixed

    vec = _vec_spec(MIX_HALF)
    return pl.pallas_call(
        body, name=name, grid=(S // SGU_TS,),
        in_specs=[pl.BlockSpec((SGU_TS, MIX_HALF), lambda i: (i, 1)), pl.BlockSpec((SGU_TS, MIX_HALF), lambda i: (i, 2)), vec, vec,
                  pl.BlockSpec((SGU_GROUPS, SGU_CHUNK, SGU_CHUNK), lambda i: (0, 0, 0)), pl.BlockSpec((SGU_CHUNK, MIX_HALF), lambda i: (0, 0))],
        out_specs=_row_spec(MIX_HALF, SGU_TS), out_shape=jax.ShapeDtypeStruct((S, MIX_HALF), F32),
        compiler_params=_cparams("parallel"))(proj, proj, ln_gain, ln_bias, wm, bias_tab)


def _sgu_bwd(dout, proj, ln_gain, ln_bias, wm, bias_tab, name):
    S = proj.shape[0]
    nch = SGU_TS // SGU_CHUNK
    nt_dims = (((1,), (1,)), ((), ()))
    tn_dims = (((0,), (0,)), ((), ()))

    def body(do_ref, zu_ref, zv_ref, g_ref, b_ref, w_ref, bt_ref, dzu_ref, dzv_ref, dw_ref, dbt_ref, dg_ref, db_ref, dvn_ref):
        first = pl.program_id(0) == 0

        @pl.when(first)
        def _():
            dw_ref[...] = jnp.zeros_like(dw_ref)
            dbt_ref[...] = jnp.zeros_like(dbt_ref)
            dg_ref[...] = jnp.zeros_like(dg_ref)
            db_ref[...] = jnp.zeros_like(db_ref)

        left, right = _half_masks(SGU_CHUNK)
        zv = zv_ref[...]
        vhat, rstd, vn = _sgu_norm(zv, g_ref[...], b_ref[...])
        for ch in range(nch):
            rows = pl.ds(ch * SGU_CHUNK, SGU_CHUNK)
            for p in range(N_PAIRS):
                cols = pl.ds(p * LANES, LANES)
                vp = vn[ch * SGU_CHUNK:(ch + 1) * SGU_CHUNK, p * LANES:(p + 1) * LANES]
                vl, vr = (vp * left).astype(BF16), (vp * right).astype(BF16)
                mixed = (jnp.dot(w_ref[2 * p], vl, preferred_element_type=F32)
                         + jnp.dot(w_ref[2 * p + 1], vr, preferred_element_type=F32) + bt_ref[:, cols])
                zu = zu_ref[rows, cols]
                do = do_ref[rows, cols]
                dzu_ref[rows, cols] = do * mixed * _dgelu(zu)
                dmix = do * _gelu(zu)
                dbt_ref[:, cols] += dmix
                dl, dr = (dmix * left).astype(BF16), (dmix * right).astype(BF16)
                dw_ref[2 * p] += lax.dot_general(dl, vl, nt_dims, preferred_element_type=F32)
                dw_ref[2 * p + 1] += lax.dot_general(dr, vr, nt_dims, preferred_element_type=F32)
                dvn_ref[rows, cols] = (lax.dot_general(w_ref[2 * p], dl, tn_dims, preferred_element_type=F32)
                                       + lax.dot_general(w_ref[2 * p + 1], dr, tn_dims, preferred_element_type=F32))
        dvn = dvn_ref[...]
        dg_ref[...] += jnp.sum(dvn * vhat, axis=0, keepdims=True)
        db_ref[...] += jnp.sum(dvn, axis=0, keepdims=True)
        dvh = dvn * g_ref[...]
        dv = rstd * (dvh - jnp.mean(dvh, axis=-1, keepdims=True) - vhat * jnp.mean(dvh * vhat, axis=-1, keepdims=True))
        dzv_ref[...] = dv * _dgelu(zv)

    vec = _vec_spec(MIX_HALF)
    row = _row_spec(MIX_HALF, SGU_TS)
    wspec = pl.BlockSpec((SGU_GROUPS, SGU_CHUNK, SGU_CHUNK), lambda i: (0, 0, 0))
    tspec = pl.BlockSpec((SGU_CHUNK, MIX_HALF), lambda i: (0, 0))
    full = jax.ShapeDtypeStruct((S, MIX_HALF), F32)
    v = jax.ShapeDtypeStruct((1, MIX_HALF), F32)
    return pl.pallas_call(
        body, name=name, grid=(S // SGU_TS,),
        in_specs=[row, pl.BlockSpec((SGU_TS, MIX_HALF), lambda i: (i, 1)), pl.BlockSpec((SGU_TS, MIX_HALF), lambda i: (i, 2)), vec, vec,
                  wspec, tspec],
        out_specs=[row, row, wspec, tspec, vec, vec],
        out_shape=[full, full, jax.ShapeDtypeStruct((SGU_GROUPS, SGU_CHUNK, SGU_CHUNK), F32),
                   jax.ShapeDtypeStruct((SGU_CHUNK, MIX_HALF), F32), v, v],
        scratch_shapes=[pltpu.VMEM((SGU_TS, MIX_HALF), F32)],
        compiler_params=_cparams("arbitrary"))(dout, proj, proj, ln_gain, ln_bias, wm, bias_tab)


def _sgu_grads(dw, dbias_tab):
    mask = jnp.tril(jnp.ones((SGU_CHUNK, SGU_CHUNK), dtype=bool))
    g_w = jnp.where(mask[None], dw, 0.0)
    g_b = dbias_tab.reshape(SGU_CHUNK, SGU_GROUPS, MIX_HALF // SGU_GROUPS).sum(axis=-1).T
    return g_w, g_b


def _head_avg_matrix(w):
    idx = np.arange(w) // HEAD_DIM
    return jnp.asarray((idx[:, None] == idx[None, :]).astype(np.float32) / HEAD_DIM, dtype=BF16)


def _head_mean(t, bavg):
    hi = t.astype(BF16)
    lo = (t - hi.astype(F32)).astype(BF16)
    return jnp.dot(hi, bavg, preferred_element_type=F32) + jnp.dot(lo, bavg, preferred_element_type=F32)


def _head_rms(t, bavg):
    r = lax.rsqrt(_head_mean(t * t, bavg) + EPS)
    return t * r, r


def _head_rms_bwd(dn, n, r, bavg):
    return r * (dn - n * _head_mean(dn * n, bavg))


GLA_TS = 512
C = GLA_CHUNK
NT_DIMS = (((1,), (1,)), ((), ()))
TN_DIMS = (((0,), (0,)), ((), ()))
HI = lax.Precision.HIGHEST


def _bdot(a, b, dims=(((1,), (0,)), ((), ()))):
    return lax.dot_general(a.astype(BF16), b.astype(BF16), dims, preferred_element_type=F32)


def _gla_chunk_terms(q, k, z):
    row = lax.broadcasted_iota(jnp.int32, (C, C), 0)
    col = lax.broadcasted_iota(jnp.int32, (C, C), 1)
    lc = _log_sigmoid(z) * (1.0 / GLA_TAU)
    b = lax.dot_general((row >= col).astype(F32), lc, (((1,), (0,)), ((), ())), precision=HI, preferred_element_type=F32)
    b_last = jnp.sum(lc, axis=0, keepdims=True)
    b_mid = b[C // 2:C // 2 + 1, :]
    scale = HEAD_DIM ** -0.5
    e_b, e_q, e_k, e_l = jnp.exp(b), jnp.exp(b - b_mid), jnp.exp(b_mid - b), jnp.exp(b_last - b)
    qs = q * (scale * e_b)
    qe = q * (scale * e_q)
    ke = k * e_k
    kl = k * e_l
    return dict(e_b=e_b, e_q=e_q, e_k=e_k, e_l=e_l, qs=qs, qe=qe, ke=ke, kl=kl, dec=jnp.exp(b_last), causal=row >= col, scale=scale)


def _pair_block_diag():
    r = lax.broadcasted_iota(jnp.int32, (LANES, LANES), 0) // HEAD_DIM
    c = lax.broadcasted_iota(jnp.int32, (LANES, LANES), 1) // HEAD_DIM
    return (r == c).astype(F32)


def _gla_fwd(proj, z, name):
    S = proj.shape[0]
    nch = GLA_TS // C

    def body(q_ref, k_ref, v_ref, z_ref, o_ref, st_ref, state_ref):
        @pl.when(pl.program_id(1) == 0)
        def _():
            state_ref[...] = jnp.zeros_like(state_ref)

        left, right = _half_masks(C)
        bd = _pair_block_diag()
        for ch in range(nch):
            rows = pl.ds(ch * C, C)
            q, k, v = q_ref[rows, :], k_ref[rows, :], v_ref[rows, :]
            t = _gla_chunk_terms(q, k, z_ref[rows, :])
            st = state_ref[...]
            st_ref[ch, 0] = st
            o = _bdot(t["qs"], st, NT_DIMS)
            for m in (left, right):
                a = jnp.where(t["causal"], _bdot(t["qe"] * m, t["ke"], NT_DIMS), 0.0)
                o = o + m * _bdot(a, v)
            o_ref[rows, :] = o
            state_ref[...] = st * t["dec"] + bd * _bdot(v, t["kl"], TN_DIMS)

    def col(cb):
        return pl.BlockSpec((GLA_TS, LANES), lambda p, i: (i, cb * N_PAIRS + p))

    return pl.pallas_call(
        body, name=name, grid=(N_PAIRS, S // GLA_TS),
        in_specs=[col(0), col(1), col(2), pl.BlockSpec((GLA_TS, LANES), lambda p, i: (i, p))],
        out_specs=[pl.BlockSpec((GLA_TS, LANES), lambda p, i: (i, p)), pl.BlockSpec((nch, 1, LANES, LANES), lambda p, i: (i, p, 0, 0))],
        out_shape=[jax.ShapeDtypeStruct((S, MIX_HALF), F32), jax.ShapeDtypeStruct((S // C, N_PAIRS, LANES, LANES), F32)],
        scratch_shapes=[pltpu.VMEM((LANES, LANES), F32)], compiler_params=_cparams("parallel", "arbitrary"))(proj, proj, proj, z)


def _gla_bwd(do, proj, z, states, name):
    S = proj.shape[0]
    nch = GLA_TS // C
    nblk = S // GLA_TS

    def body(do_ref, q_ref, k_ref, v_ref, z_ref, st_ref, dq_ref, dk_ref, dv_ref, dlc_ref, dstate_ref):
        @pl.when(pl.program_id(1) == 0)
        def _():
            dstate_ref[...] = jnp.zeros_like(dstate_ref)

        left, right = _half_masks(C)
        bd = _pair_block_diag()
        rowi = lax.broadcasted_iota(jnp.int32, (C, LANES), 0)
        for ch in range(nch - 1, -1, -1):
            rows = pl.ds(ch * C, C)
            q, k, v, dov = q_ref[rows, :], k_ref[rows, :], v_ref[rows, :], do_ref[rows, :]
            t = _gla_chunk_terms(q, k, z_ref[rows, :])
            st = st_ref[ch, 0]
            dst_next = dstate_ref[...]
            g = bd * dst_next
            dqs = _bdot(dov, st)
            dv = _bdot(t["kl"], g, NT_DIMS)
            dkl = _bdot(v, g)
            dqe = jnp.zeros((C, LANES), F32)
            dke = jnp.zeros((C, LANES), F32)
            for m in (left, right):
                a = jnp.where(t["causal"], _bdot(t["qe"] * m, t["ke"], NT_DIMS), 0.0)
                da = jnp.where(t["causal"], _bdot(dov * m, v, NT_DIMS), 0.0)
                dv = dv + m * _bdot(a, dov, TN_DIMS)
                dqe = dqe + m * _bdot(da, t["ke"])
                dke = dke + m * _bdot(da, t["qe"], TN_DIMS)
            dstate_ref[...] = bd * (dst_next * t["dec"] + _bdot(dov, t["qs"], TN_DIMS))
            db_last = jnp.sum(dst_next * st, axis=0, keepdims=True) * t["dec"] + jnp.sum(dkl * t["kl"], axis=0, keepdims=True)
            db = dqs * t["qs"] + dqe * t["qe"] - dke * t["ke"] - dkl * t["kl"]
            db = db + jnp.where(rowi == C - 1, db_last, 0.0)
            dq_ref[rows, :] = (dqs * t["e_b"] + dqe * t["e_q"]) * t["scale"]
            dk_ref[rows, :] = dke * t["e_k"] + dkl * t["e_l"]
            dv_ref[rows, :] = dv
            row = lax.broadcasted_iota(jnp.int32, (C, C), 0)
            colm = lax.broadcasted_iota(jnp.int32, (C, C), 1)
            dlc_ref[rows, :] = lax.dot_general((colm >= row).astype(F32), db, (((1,), (0,)), ((), ())), precision=HI,
                                               preferred_element_type=F32)

    def col(cb):
        return pl.BlockSpec((GLA_TS, LANES), lambda p, i: (nblk - 1 - i, cb * N_PAIRS + p))

    blk = pl.BlockSpec((GLA_TS, LANES), lambda p, i: (nblk - 1 - i, p))
    full = jax.ShapeDtypeStruct((S, MIX_HALF), F32)
    return pl.pallas_call(
        body, name=name, grid=(N_PAIRS, nblk),
        in_specs=[blk, col(0), col(1), col(2), blk, pl.BlockSpec((nch, 1, LANES, LANES), lambda p, i: (nblk - 1 - i, p, 0, 0))],
        out_specs=[blk, blk, blk, blk], out_shape=[full, full, full, full],
        scratch_shapes=[pltpu.VMEM((LANES, LANES), F32)], compiler_params=_cparams("parallel", "arbitrary"))(do, proj, proj, proj, z, states)


def _gla_block_fwd(proj, w_lr_pad, b_lr, gain, bavg, pfx):
    z = _mm(proj, w_lr_pad, "nn", pfx + "_z", a_cols=(7 * MIX_HALF, MIX_HALF), epi=lambda acc, b: acc + b, extras=[(b_lr, "n")])
    o, states = _gla_fwd(proj, z, pfx + "_core")

    def out(ov, gg, ba, gn):
        n, _ = _head_rms(ov, ba)
        return n * gn * (gg * _sigmoid(gg))

    og, = _ew(out, pfx + "_out", [o, (proj, MIX_HALF, 3)], consts=[bavg, gain], outs=[(MIX_HALF, F32)])
    return og, dict(z=z, o=o, states=states)


def _gla_block_bwd(dog, proj, w_lr_pad, gain, bavg, res, pfx):
    z, o, states = res["z"], res["o"], res["states"]

    def out_bwd(dy, ov, gg, ba, gn):
        n, r = _head_rms(ov, ba)
        sg = _sigmoid(gg)
        silu = gg * sg
        dn = dy * gn * silu
        do = _head_rms_bwd(dn, n, r, ba)
        dgg = dy * n * gn * (sg * (1.0 + gg * (1.0 - sg)))
        return do, dgg, jnp.sum(dy * n * silu, axis=0, keepdims=True)

    do, dgg, dgain = _ew(out_bwd, pfx + "_out_bwd", [dog, o, (proj, MIX_HALF, 3)], consts=[bavg, gain],
                         outs=[(MIX_HALF, F32), (MIX_HALF, F32)], sums=[MIX_HALF])
    dq, dk, dv, dlc = _gla_bwd(do, proj, z, states, pfx + "_core_bwd")

    def decay_bwd(dl, zv):
        dz = dl * (1.0 / GLA_TAU) * (1.0 - _sigmoid(zv))
        return dz, jnp.sum(dz, axis=0, keepdims=True)

    dz, db_lr = _ew(decay_bwd, pfx + "_decay_bwd", [dlc, z], outs=[(MIX_HALF, F32)], sums=[MIX_HALF])
    dw_lr_pad = _mm(proj, dz, "tn", pfx + "_dwlr", a_cols=(7 * MIX_HALF, MIX_HALF))
    dsmall = _mm(dz, w_lr_pad, "nt", pfx + "_dsmall")
    return (dq, dk, dv, dgg, dsmall), dict(w_lr=dw_lr_pad[:GLA_RANK], b_lr=db_lr.reshape(-1), gain=dgain.reshape(-1, HEAD_DIM))


FOX_T = 512
FOX_HEADS = MIX_HALF // HEAD_DIM
NEG = -1e30
CUM_T = 512


def _cum_lanes(x, name, reverse, pre=None):
    R, S = x.shape
    nb = S // CUM_T

    def body(x_ref, o_ref, carry_ref):
        @pl.when(pl.program_id(0) == 0)
        def _():
            carry_ref[...] = jnp.zeros_like(carry_ref)

        xv = x_ref[...]
        if pre is not None:
            xv = pre(xv)
        i = lax.broadcasted_iota(jnp.int32, (CUM_T, CUM_T), 0)
        j = lax.broadcasted_iota(jnp.int32, (CUM_T, CUM_T), 1)
        tri = ((i >= j) if reverse else (i <= j)).astype(F32)
        c = lax.dot_general(xv, tri, (((1,), (0,)), ((), ())), precision=HI, preferred_element_type=F32)
        carry = carry_ref[...]
        o_ref[...] = c + carry[:, 0:1]
        carry_ref[...] = carry + jnp.sum(xv, axis=1, keepdims=True)

    spec = pl.BlockSpec((R, CUM_T), (lambda i: (0, nb - 1 - i)) if reverse else (lambda i: (0, i)))
    return pl.pallas_call(body, name=name, grid=(nb,), in_specs=[spec], out_specs=spec, out_shape=jax.ShapeDtypeStruct((R, S), F32),
                          scratch_shapes=[pltpu.VMEM((R, LANES), F32)], compiler_params=_cparams("arbitrary"))(x)


def _fox_scores(q, k, cqb, ck_ref, h, m, diag):
    cq = cqb[:, h * HEAD_DIM:h * HEAD_DIM + 1]
    ck = ck_ref[0, h:h + 1, :]
    s = lax.dot_general(q * m.astype(q.dtype), k, NT_DIMS, preferred_element_type=F32) + (cq - ck)
    if not diag:
        return s
    row = lax.broadcasted_iota(jnp.int32, (FOX_T, FOX_T), 0)
    col = lax.broadcasted_iota(jnp.int32, (FOX_T, FOX_T), 1)
    return jnp.where(row < col, NEG, s)


def _on_causal_blocks(q_blk, k_blk, step):
    @pl.when(k_blk < q_blk)
    def _():
        step(False)

    @pl.when(k_blk == q_blk)
    def _():
        step(True)


def _causal_pairs(n, key_major):
    if key_major:
        pairs = [(q, k) for k in range(n) for q in range(k, n)]
    else:
        pairs = [(q, k) for q in range(n) for k in range(q + 1)]
    return jnp.asarray([p[0] for p in pairs], jnp.int32), jnp.asarray([p[1] for p in pairs], jnp.int32)


def _fox_fwd(qn, kn, proj, cum_b, cum_tp, name):
    S = qn.shape[0]
    nq = S // FOX_T
    qidx, kidx = _causal_pairs(nq, False)

    def body(qidx_ref, kidx_ref, q_ref, k_ref, v_ref, cq_ref, ck_ref, o_ref, lse_ref, m_scr, acc_scr):
        t = pl.program_id(1)
        qi, ki = qidx_ref[t], kidx_ref[t]

        @pl.when(ki == 0)
        def _():
            m_scr[...] = jnp.full_like(m_scr, NEG)
            acc_scr[...] = jnp.zeros_like(acc_scr)

        left, right = _half_masks(FOX_T)

        def step(diag):
            q, k, v = q_ref[...], k_ref[...], v_ref[...].astype(BF16)
            cqb = cq_ref[...]
            for h, m in enumerate((left, right)):
                s = _fox_scores(q, k, cqb, ck_ref, h, m, diag)
                m_prev = m_scr[h]
                m_new = jnp.maximum(m_prev, jnp.max(s, axis=1, keepdims=True))
                p = jnp.exp(s - m_new)
                v_h = jnp.where(m > 0, v, jnp.ones_like(v))
                acc_scr[h] = jnp.exp(m_prev - m_new) * acc_scr[h] + jnp.dot(p.astype(BF16), v_h, preferred_element_type=F32)
                m_scr[h] = m_new

        _on_causal_blocks(qi, ki, step)

        @pl.when(ki == qi)
        def _():
            a0, a1 = acc_scr[0], acc_scr[1]
            is_left = left > 0
            num = jnp.where(is_left, a0, a1)
            den = jnp.where(is_left, pltpu.roll(a0, HEAD_DIM, 1), pltpu.roll(a1, HEAD_DIM, 1))
            o_ref[...] = num / den
            lse_ref[...] = jnp.where(is_left, m_scr[0], m_scr[1]) + jnp.log(den)

    qspec = pl.BlockSpec((FOX_T, LANES), lambda p, t, qx, kx: (qx[t], p))
    kspec = pl.BlockSpec((FOX_T, LANES), lambda p, t, qx, kx: (kx[t], p))
    vspec = pl.BlockSpec((FOX_T, LANES), lambda p, t, qx, kx: (kx[t], 6 * N_PAIRS + p))
    ckspec = pl.BlockSpec((1, 8, FOX_T), lambda p, t, qx, kx: (p, 0, kx[t]))
    full = jax.ShapeDtypeStruct((S, MIX_HALF), F32)
    grid_spec = pltpu.PrefetchScalarGridSpec(
        num_scalar_prefetch=2, grid=(N_PAIRS, int(qidx.shape[0])), in_specs=[qspec, kspec, vspec, qspec, ckspec], out_specs=[qspec, qspec],
        scratch_shapes=[pltpu.VMEM((2, FOX_T, 1), F32), pltpu.VMEM((2, FOX_T, LANES), F32)])
    return pl.pallas_call(body, name=name, grid_spec=grid_spec, out_shape=[full, full],
                          compiler_params=_cparams("parallel", "arbitrary"))(qidx, kidx, qn, kn, proj, cum_b, cum_tp)


def _fox_bwd(do, qn, kn, proj, cum_b, cum_tp, lse_b, delta_b, name):
    S = qn.shape[0]
    nq = S // FOX_T
    scale = HEAD_DIM ** -0.5
    qidx, kidx = _causal_pairs(nq, True)
    ntri = int(qidx.shape[0])

    def body(qidx_ref, kidx_ref, do_ref, q_ref, k_ref, v_ref, cq_ref, ck_ref, lse_ref, dl_ref,
             dq_ref, dcq_ref, dk_ref, dv_ref, dck_ref, dq_scr, dk_scr, dv_scr):
        t = pl.program_id(1)
        qi, ki = qidx_ref[t], kidx_ref[t]

        @pl.when(t == 0)
        def _():
            dq_scr[...] = jnp.zeros_like(dq_scr)

        @pl.when(qi == ki)
        def _():
            dk_scr[...] = jnp.zeros_like(dk_scr)
            dv_scr[...] = jnp.zeros_like(dv_scr)

        left, right = _half_masks(FOX_T)
        rows = pl.ds(pl.multiple_of(qi * FOX_T, FOX_T), FOX_T)

        def step(diag):
            q, k, v, dov = q_ref[...], k_ref[...], v_ref[...].astype(BF16), do_ref[...]
            cqb, lseb, dlb = cq_ref[...], lse_ref[...], dl_ref[...]
            dob = dov.astype(BF16)
            dv = dv_scr[...]
            for h, m in enumerate((left, right)):
                s = _fox_scores(q, k, cqb, ck_ref, h, m, diag)
                p = jnp.exp(s - lseb[:, h * HEAD_DIM:h * HEAD_DIM + 1])
                dp = lax.dot_general((dov * m).astype(BF16), v, NT_DIMS, preferred_element_type=F32)
                ds = (p * (dp - dlb[:, h * HEAD_DIM:h * HEAD_DIM + 1])).astype(BF16)
                dv = dv + m * lax.dot_general(p.astype(BF16), dob, TN_DIMS, preferred_element_type=F32)
                q_h = jnp.where(m > 0, q, jnp.ones_like(q))
                k_h = jnp.where(m > 0, k, jnp.ones_like(k))
                dk_scr[h] = dk_scr[h] + lax.dot_general(ds, q_h, TN_DIMS, preferred_element_type=F32)
                dq_scr[h, rows, :] = dq_scr[h, rows, :] + jnp.dot(ds, k_h, preferred_element_type=F32)
            dv_scr[...] = dv

        _on_causal_blocks(qi, ki, step)

        @pl.when(qi == nq - 1)
        def _():
            a0, a1 = dk_scr[0], dk_scr[1]
            dk_ref[...] = left * a0 + right * a1
            dv_ref[...] = dv_scr[...]
            dck_ref[...] = left * pltpu.roll(a0, HEAD_DIM, 1) + right * pltpu.roll(a1, HEAD_DIM, 1)

        @pl.when(t == ntri - 1)
        def _():
            for r in range(nq):
                blk = pl.ds(r * FOX_T, FOX_T)
                a0, a1 = dq_scr[0, blk, :], dq_scr[1, blk, :]
                dq_ref[blk, :] = (left * a0 + right * a1) * scale
                dcq_ref[blk, :] = left * pltpu.roll(a0, HEAD_DIM, 1) + right * pltpu.roll(a1, HEAD_DIM, 1)

    qspec = pl.BlockSpec((FOX_T, LANES), lambda p, t, qx, kx: (qx[t], p))
    kspec = pl.BlockSpec((FOX_T, LANES), lambda p, t, qx, kx: (kx[t], p))
    vspec = pl.BlockSpec((FOX_T, LANES), lambda p, t, qx, kx: (kx[t], 6 * N_PAIRS + p))
    ckspec = pl.BlockSpec((1, 8, FOX_T), lambda p, t, qx, kx: (p, 0, kx[t]))
    seq = pl.BlockSpec((S, LANES), lambda p, t, qx, kx: (0, p))
    full = jax.ShapeDtypeStruct((S, MIX_HALF), F32)
    grid_spec = pltpu.PrefetchScalarGridSpec(
        num_scalar_prefetch=2, grid=(N_PAIRS, ntri), in_specs=[qspec, qspec, kspec, vspec, qspec, ckspec, qspec, qspec],
        out_specs=[seq, seq, kspec, kspec, kspec],
        scratch_shapes=[pltpu.VMEM((2, S, LANES), F32), pltpu.VMEM((2, FOX_T, LANES), F32), pltpu.VMEM((FOX_T, LANES), F32)])
    return pl.pallas_call(body, name=name, grid_spec=grid_spec, out_shape=[full] * 5,
                          compiler_params=_cparams("parallel", "arbitrary"))(qidx, kidx, do, qn, kn, proj, cum_b, cum_tp, lse_b, delta_b)


def _ff_bwd(rc, f_t, name):
    def body(rc_ref, f_ref, d_ref, s_ref):
        d = rc_ref[...] * (1.0 - _sigmoid(f_ref[...]))
        d_ref[...] = d
        s_ref[...] = jnp.sum(d, axis=1, keepdims=True)

    return pl.pallas_call(body, name=name, out_shape=[jax.ShapeDtypeStruct(rc.shape, F32), jax.ShapeDtypeStruct((rc.shape[0], 1), F32)])(rc, f_t)


def _fox_block_fwd(proj, b_f, q_gain, k_gain, bavg, pfx):
    S = proj.shape[0]

    def prep(qv, kv, ba, qg, kg):
        return _head_rms(qv, ba)[0] * qg * (HEAD_DIM ** -0.5), _head_rms(kv, ba)[0] * kg

    qn, kn = _ew(prep, pfx + "_prep", [(proj, MIX_HALF, 4), (proj, MIX_HALF, 5)], consts=[bavg, q_gain, k_gain],
                 outs=[(MIX_HALF, BF16), (MIX_HALF, BF16)])
    f0 = 7 * MIX_HALF + GLA_RANK
    f_t = proj[:, f0:f0 + FOX_HEADS].T + b_f.reshape(FOX_HEADS, 1)
    cum = _cum_lanes(f_t, pfx + "_cum", False, pre=_log_sigmoid)
    cum_b = jnp.repeat(cum.T, HEAD_DIM, axis=1)
    cum_tp = jnp.pad(cum.reshape(N_PAIRS, 2, S), ((0, 0), (0, 6), (0, 0)))
    o, lse_b = _fox_fwd(qn, kn, proj, cum_b, cum_tp, pfx + "_attn")
    return o, dict(qn=qn, kn=kn, f_t=f_t, cum_b=cum_b, cum_tp=cum_tp, o=o, lse_b=lse_b)


def _fox_block_bwd(do, proj, q_gain, k_gain, bavg, res, pfx):
    qn, kn, o = res["qn"], res["kn"], res["o"]
    S = proj.shape[0]
    delta_b, = _ew(lambda a, b, ba: _head_mean(a * b, ba) * float(HEAD_DIM), pfx + "_delta", [do, o], consts=[bavg], outs=[(MIX_HALF, F32)])
    args = (do, qn, kn, proj, res["cum_b"], res["cum_tp"], res["lse_b"], delta_b)
    dqn, dcq_b, dkn, dv, dck_b = _fox_bwd(*args, pfx + "_bwd")

    def prep_bwd(dq, dk, qv, kv, ba, qg, kg):
        nq, rq = _head_rms(qv, ba)
        nk, rk = _head_rms(kv, ba)
        return (_head_rms_bwd(dq * qg, nq, rq, ba), _head_rms_bwd(dk * kg, nk, rk, ba),
                jnp.sum(dq * nq, axis=0, keepdims=True), jnp.sum(dk * nk, axis=0, keepdims=True))

    dfq, dfk, dqg, dkg = _ew(prep_bwd, pfx + "_prep_bwd", [dqn, dkn, (proj, MIX_HALF, 4), (proj, MIX_HALF, 5)],
                             consts=[bavg, q_gain, k_gain], outs=[(MIX_HALF, F32), (MIX_HALF, F32)], sums=[MIX_HALF, MIX_HALF])
    dcum = (dcq_b - dck_b)[:, ::HEAD_DIM].T
    rc = _cum_lanes(dcum, pfx + "_rcum", True)
    dff_t, db_f = _ff_bwd(rc, res["f_t"], pfx + "_ff_bwd")
    grads = dict(b_f=db_f.reshape(-1), q_gain=dqg.reshape(-1, HEAD_DIM), k_gain=dkg.reshape(-1, HEAD_DIM))
    return (dfq, dfk, dv, dff_t.T), grads


WEIGHTS = ['ada_w', 'ada_b', 'even_w_in', 'even_w_out', 'gla_w_lr', 'gla_b_lr', 'gla_gain', 'fox_b_f', 'fox_q_gain', 'fox_k_gain',
           'odd_w_in', 'odd_w_out', 's5_lam_re', 's5_lam_im', 's5_log_dt', 's5_b_re', 's5_b_im', 's5_c_re', 's5_c_im', 's5_d',
           's5_w_glu', 's5_b_glu', 'sgu_ln_gain', 'sgu_ln_bias', 'sgu_w_s', 'sgu_b_s', 'mlp_w1', 'mlp_w2']
ARGS = ['x', 'c'] + WEIGHTS + ['loss_target'] + ['m_' + w for w in WEIGHTS] + ['v_' + w for w in WEIGHTS]

EVEN_COLS = 3608
EVEN_PAD = 8 * MIX_HALF
MOD = 6 * D_MODEL
MOD_SHARD = MOD // N_CHIPS

SHARDED = [("even_w_in", (1, 1024, 902), 2), ("even_w_out", (1, 256, 1024), 1), ("odd_w_in", (1, 1024, 384), 2),
           ("odd_w_out", (1, 256, 1024), 1), ("mlp_w1", (2, 1024, 1024), 2), ("mlp_w2", (2, 1024, 1024), 1),
           ("gla_w_lr", (1, 16, 128), 2), ("s5_w_glu", (1, 128, 512), 1), ("s5_b_glu", (1, 128), 1),
           ("sgu_ln_gain", (1, 128), 1), ("sgu_ln_bias", (1, 128), 1)]
PACK_COLS = 512
PACK_ROWS = 12288
REPLICATED = [("gla_b_lr", (1, 512)), ("gla_gain", (1, 8, 64)), ("fox_b_f", (1, 8)), ("fox_q_gain", (1, 8, 64)),
              ("fox_k_gain", (1, 8, 64)), ("s5_lam_re", (1, 32, 64)), ("s5_lam_im", (1, 32, 64)), ("s5_log_dt", (1, 32)),
              ("s5_b_re", (1, 32, 64, 16)), ("s5_b_im", (1, 32, 64, 16)), ("s5_c_re", (1, 32, 16, 64)), ("s5_c_im", (1, 32, 16, 64)),
              ("s5_d", (1, 32, 16)), ("sgu_w_s", (1, 8, 128, 128)), ("sgu_b_s", (1, 8, 128))]
SMALL_ROWS = 768
BIG_ADAM = {"ada_w": (2048, 1536), "even_w_in": (1024, 902), "even_w_out": (256, 1024), "odd_w_in": (1024, 384),
            "odd_w_out": (256, 1024), "mlp_w1": (2048, 1024), "mlp_w2": (2048, 1024), "s5_w_glu": (128, 512)}


PACK_ALIGN = 16


def _piece_rows(shape):
    rows = -(-math.prod(shape) // PACK_COLS)
    return -(-rows // PACK_ALIGN) * PACK_ALIGN


def _to_rows(p, lead=()):
    n = math.prod(p.shape[len(lead):])
    rows = _piece_rows(p.shape[len(lead):])
    flat = p.reshape(lead + (n,))
    if rows * PACK_COLS != n:
        flat = jnp.pad(flat, [(0, 0)] * len(lead) + [(0, rows * PACK_COLS - n)])
    return flat.reshape(lead + (rows, PACK_COLS))


def _from_rows(x, r0, shape, lead=()):
    n = math.prod(shape)
    seg = lax.slice_in_dim(x, r0, r0 + _piece_rows(shape), axis=len(lead)).reshape(lead + (-1,))
    return lax.slice_in_dim(seg, 0, n, axis=len(lead)).reshape(lead + tuple(shape))


def _pack_rows(pieces, rows):
    x = jnp.concatenate([_to_rows(p) for p in pieces], axis=0)
    return jnp.pad(x, ((0, rows - x.shape[0]), (0, 0)))


def _unpack(x, specs):
    out, r0 = {}, 0
    for name, shape in specs:
        out[name] = _from_rows(x, r0, shape)
        r0 += _piece_rows(shape)
    return out


def _shards_to_full(x4):
    out, r0 = {}, 0
    for name, shape, axis in SHARDED:
        seg = _from_rows(x4, r0, shape, lead=(N_CHIPS,))
        out[name] = jnp.concatenate([seg[k] for k in range(N_CHIPS)], axis=axis)
        r0 += _piece_rows(shape)
    return out


def _full_to_shards(full):
    blocks = [_to_rows(jnp.stack(jnp.split(full[name], N_CHIPS, axis=axis)), lead=(N_CHIPS,)) for name, _, axis in SHARDED]
    x = jnp.concatenate(blocks, axis=1)
    return jnp.pad(x, ((0, 0), (0, PACK_ROWS - x.shape[1]), (0, 0)))


def _relu2(t):
    r = jnp.maximum(t, 0.0)
    return r * r


def _silu(t):
    return t * _sigmoid(t)


def _pack_even(w):
    return jnp.concatenate([w[:, :2048], w[:, 2064:3600], w[:, 2048:2064], w[:, 3600:3608],
                            jnp.zeros((w.shape[0], EVEN_PAD - EVEN_COLS), w.dtype)], axis=1)


def _unpack_even(wp):
    return jnp.concatenate([wp[:, :2048], wp[:, 3584:3600], wp[:, 2048:3584], wp[:, 3600:3608]], axis=1)


def _mlp_fwd(h, w1, w2, pfx):
    pre = _mm(h, w1, "nn", pfx + "_up")
    return pre, _mm(pre, w2, "nn", pfx + "_down", a_pro=_relu2)


def _mlp_bwd(dm, h, pre, w1, w2, pfx):
    dpre = _mm(dm, w2, "nt", pfx + "_dpre", epi=lambda acc, p: acc * (2.0 * jnp.maximum(p, 0.0)), extras=[(pre, "mn")], out_dtype=BF16)
    dw2 = _mm(pre, dm, "tn", pfx + "_dw2", a_pro=_relu2)
    dw1 = _mm(h, dpre, "tn", pfx + "_dw1")
    dh = _mm(dpre, w1, "nt", pfx + "_dh")
    return dh, dw1, dw2


def _step(args):
    a = dict(zip(ARGS, args, strict=True))
    x0 = a["x"][0]
    target = a["loss_target"][0]
    mx, my, mc = lax.axis_index("x"), lax.axis_index("y"), lax.axis_index("c")
    chip = 2 * mx + my
    dev = 2 * chip + mc
    bavg = _head_avg_matrix(MIX_HALF)

    c_all = _gather8(jnp.pad(a["c"], ((0, 7), (0, 0))), "c_gather")[:, :, 0, :].reshape(2 * N_CHIPS, D_MODEL)
    ada_b_shard = lax.dynamic_slice_in_dim(a["ada_b"], chip * MOD_SHARD, MOD_SHARD, axis=1)
    mod_sh = [_mm(c_all, a["ada_w"][l], "nn", f"mod{l}", a_pro=_silu, epi=lambda acc, b: acc + b, extras=[(ada_b_shard[l:l + 1], "n")])
              for l in range(2)]
    small3 = jnp.zeros((8, MOD_SHARD), F32)
    for r, n in enumerate(("s5_b_glu", "sgu_ln_gain", "sgu_ln_bias")):
        small3 = small3.at[r, :LANES].set(a[n][0])
    mod_all = _chip_exchange(jnp.concatenate(mod_sh + [small3]), "mod_gather", True)
    mods = []
    for l in range(2):
        full = mod_all[:, 8 * l:8 * l + 8].transpose(1, 0, 2).reshape(8, MOD)
        mods.append(jnp.split(lax.dynamic_slice_in_dim(full, dev, 1, axis=0), 6, axis=1))
    b_glu, ln_gain, ln_bias = (mod_all[:, 16 + r, :LANES].reshape(1, MIX_HALF) for r in range(3))

    shard = _pack_rows([a[n] for n, _, _ in SHARDED], PACK_ROWS).astype(BF16)
    half = lax.dynamic_slice_in_dim(shard, mc * (PACK_ROWS // 2), PACK_ROWS // 2, axis=0)
    collected = _chip_exchange(half, "w_chips", True)
    halves = _by_core(collected, _pair_swap(collected, "w_pair"))
    w = _shards_to_full(halves.transpose(1, 0, 2, 3).reshape(N_CHIPS, PACK_ROWS, PACK_COLS))
    w_even = _pack_even(w["even_w_in"][0])
    w_lr_pad = jnp.zeros((MIX_HALF, MIX_HALF), BF16).at[:GLA_RANK].set(w["gla_w_lr"][0])
    gla_b_lr = a["gla_b_lr"]
    gla_gain, q_gain, k_gain = (a[n].reshape(1, MIX_HALF) for n in ("gla_gain", "fox_q_gain", "fox_k_gain"))
    s5w = dict(lam_re=a["s5_lam_re"][0], lam_im=a["s5_lam_im"][0], log_dt=a["s5_log_dt"][0], b_re=a["s5_b_re"][0], b_im=a["s5_b_im"][0],
               c_re=a["s5_c_re"][0], c_im=a["s5_c_im"][0], d=a["s5_d"][0], w_glu=w["s5_w_glu"][0], b_glu=b_glu)
    sgu_wm, sgu_bt = _sgu_tables(a["sgu_w_s"][0], a["sgu_b_s"][0])

    sh1, sc1, g1, sh2, sc2, g2 = mods[0]
    _, h1_0 = _res_rms(x0, sc1, sh1, "l0_norm1")
    proj0 = _mm(h1_0, w_even, "nn", "l0_proj")
    og, gla_res = _gla_block_fwd(proj0, w_lr_pad, gla_b_lr, gla_gain, bavg, "gla")
    of, fox_res = _fox_block_fwd(proj0, a["fox_b_f"][0], q_gain, k_gain, bavg, "fox")
    mixed0 = jnp.concatenate([og, of], axis=1).astype(BF16)
    y0 = _mm(mixed0, w["even_w_out"][0], "nn", "l0_out")
    x1, h2_0 = _res_rms(x0, sc2, sh2, "l0_norm2", y=y0, g=g1)
    pre0, m0 = _mlp_fwd(h2_0, w["mlp_w1"][0], w["mlp_w2"][0], "l0_mlp")
    sh1b, sc1b, g1b, sh2b, sc2b, g2b = mods[1]
    x2, h1_1 = _res_rms(x1, sc1b, sh1b, "l1_norm1", y=m0, g=g2)
    proj1 = _mm(h1_1, w["odd_w_in"][0], "nn", "l1_proj")
    ys5, s5_res = _s5_block_fwd(proj1[:, :MIX_HALF], s5w, "s5")
    ysgu = _sgu_fwd(proj1, ln_gain, ln_bias, sgu_wm, sgu_bt, "sgu")
    mixed1 = jnp.concatenate([ys5, ysgu], axis=1).astype(BF16)
    y1 = _mm(mixed1, w["odd_w_out"][0], "nn", "l1_out")
    x3, h2_1 = _res_rms(x2, sc2b, sh2b, "l1_norm2", y=y1, g=g1b)
    pre1, m1 = _mlp_fwd(h2_1, w["mlp_w1"][1], w["mlp_w2"][1], "l1_mlp")
    loss_b, dx4, dm1, dg2b = _res_loss(x3, m1, g2b, target, "loss")
    loss = lax.psum(loss_b[0, 0], ("x", "y", "c"))

    full = {}
    dh2_1, dw1_1, dw2_1 = _mlp_bwd(dm1, h2_1, pre1, w["mlp_w1"][1], w["mlp_w2"][1], "l1_mlp")
    dx3, dy1, dg1b, dsc2b, dsh2b = _res_rms_bwd(x3, dh2_1, sc2b, dx4, "l1_norm2_bwd", y=y1, g=g1b)
    dmixed1 = _mm(dy1, w["odd_w_out"][0], "nt", "l1_out_dx")
    full["odd_w_out"] = _mm(mixed1, dy1, "tn", "l1_out_dw")[None]
    du, s5g = _s5_block_bwd(dmixed1[:, :MIX_HALF], s5w, s5_res, "s5")
    dzu, dzv, dws, dbt, dlg, dlb = _sgu_bwd(dmixed1[:, MIX_HALF:], proj1, ln_gain, ln_bias, sgu_wm, sgu_bt, "sgu_bwd")
    g_ws, g_bs = _sgu_grads(dws, dbt)
    dproj1 = jnp.concatenate([du, dzu, dzv], axis=1).astype(BF16)
    full["odd_w_in"] = _mm(h1_1, dproj1, "tn", "l1_proj_dw")[None]
    dh1_1 = _mm(dproj1, w["odd_w_in"][0], "nt", "l1_proj_dx")
    dx2, dm0, dg2, dsc1b, dsh1b = _res_rms_bwd(x2, dh1_1, sc1b, dx3, "l1_norm1_bwd", y=m0, g=g2)
    dh2_0, dw1_0, dw2_0 = _mlp_bwd(dm0, h2_0, pre0, w["mlp_w1"][0], w["mlp_w2"][0], "l0_mlp")
    full["mlp_w1"] = jnp.stack([dw1_0, dw1_1])
    full["mlp_w2"] = jnp.stack([dw2_0, dw2_1])
    dx1, dy0, dg1, dsc2, dsh2 = _res_rms_bwd(x1, dh2_0, sc2, dx2, "l0_norm2_bwd", y=y0, g=g1)
    dmixed0 = _mm(dy0, w["even_w_out"][0], "nt", "l0_out_dx")
    full["even_w_out"] = _mm(mixed0, dy0, "tn", "l0_out_dw")[None]
    (dgq, dgk, dgv, dgg, dsmall), glag = _gla_block_bwd(dmixed0[:, :MIX_HALF], proj0, w_lr_pad, gla_gain, bavg, gla_res, "gla")
    (dfq, dfk, dfv, dff), foxg = _fox_block_bwd(dmixed0[:, MIX_HALF:], proj0, q_gain, k_gain, bavg, fox_res, "fox")
    dsmall = lax.dynamic_update_slice(dsmall, dff, (0, GLA_RANK))
    dproj0 = jnp.concatenate([dgq, dgk, dgv, dgg, dfq, dfk, dfv, dsmall], axis=1).astype(BF16)
    full["even_w_in"] = _unpack_even(_mm(h1_0, dproj0, "tn", "l0_proj_dw"))[None]
    dh1_0 = _mm(dproj0, w_even, "nt", "l0_proj_dx")
    grad_x, dsc1, dsh1 = _res_rms_bwd(x0, dh1_0, sc1, dx1, "l0_norm1_bwd")
    full["gla_w_lr"] = glag["w_lr"][None]
    full["s5_w_glu"] = s5g["w_glu"][None]
    full["s5_b_glu"] = s5g["b_glu"][None]
    full["sgu_ln_gain"] = dlg
    full["sgu_ln_bias"] = dlb

    dmod = jnp.concatenate([dsh1, dsc1, dg1, dsh2, dsc2, dg2, dsh1b, dsc1b, dg1b, dsh2b, dsc2b, dg2b], axis=1)
    dmod_all = _gather8(jnp.pad(dmod, ((0, 7), (0, 0))), "dmod_gather")[:, :, 0, :].reshape(2 * N_CHIPS, 2, MOD)
    grads = {}
    grads["ada_w"] = jnp.stack([
        _mm(c_all, lax.dynamic_slice_in_dim(dmod_all[:, l], chip * MOD_SHARD, MOD_SHARD, axis=1), "tn", f"ada_dw{l}", a_pro=_silu)
        for l in range(2)])
    grads["ada_b"] = _sum_slots(dmod_all.reshape(2 * N_CHIPS, 2 * MOD // PACK_COLS, PACK_COLS), "ada_db").reshape(2, MOD)

    packed = _full_to_shards(full)
    hr = PACK_ROWS // 2
    mine = lax.dynamic_slice_in_dim(packed, mc * hr, hr, axis=1)
    other = lax.dynamic_slice_in_dim(packed, (1 - mc) * hr, hr, axis=1)
    theirs = _pair_swap(other, "g_pair")
    pair_sum, = _ew(lambda p, q: p + q, "g_pair_sum", [mine.reshape(N_CHIPS * hr, PACK_COLS), theirs.reshape(N_CHIPS * hr, PACK_COLS)],
                    outs=[(PACK_COLS, BF16)])
    arrived = _chip_exchange(pair_sum.reshape(N_CHIPS, hr, PACK_COLS), "g_chips", False)
    red_half = _sum_slots(arrived, "g_chip_sum")
    reduced = _by_core(red_half, _pair_swap(red_half, "g_pair_out")).reshape(PACK_ROWS, PACK_COLS)
    grads.update(_unpack(reduced, [(n, s) for n, s, _ in SHARDED]))

    part = dict(gla_b_lr=glag["b_lr"], gla_gain=glag["gain"], fox_b_f=foxg["b_f"], fox_q_gain=foxg["q_gain"], fox_k_gain=foxg["k_gain"],
                s5_lam_re=s5g["lam_re"], s5_lam_im=s5g["lam_im"], s5_log_dt=s5g["log_dt"], s5_b_re=s5g["b_re"], s5_b_im=s5g["b_im"],
                s5_c_re=s5g["c_re"], s5_c_im=s5g["c_im"], s5_d=s5g["d"], sgu_w_s=g_ws, sgu_b_s=g_bs)
    parts_all = _gather8(_pack_rows([part[n] for n, _ in REPLICATED], SMALL_ROWS), "rep_gather")
    rep = _sum_slots(parts_all.reshape(2 * N_CHIPS, SMALL_ROWS, PACK_COLS), "rep_sum")
    grads.update(_unpack(rep, REPLICATED))

    delta, new_m, new_v = {}, {}, {}
    for n, shape2 in BIG_ADAM.items():
        d, nm, nv = _adamw(a[n].reshape(shape2), grads[n].reshape(shape2), a["m_" + n].reshape(shape2), a["v_" + n].reshape(shape2), "adamw_" + n)
        delta[n], new_m[n], new_v[n] = (t.reshape(a[n].shape) for t in (d, nm, nv))
    small = [n for n in WEIGHTS if n not in BIG_ADAM]
    spec = [(n, a[n].shape) for n in small]
    packs = [_pack_rows([src[n] for n in small], SMALL_ROWS) for src in
             (a, grads, {n: a["m_" + n] for n in small}, {n: a["v_" + n] for n in small})]
    for tgt, res in zip((delta, new_m, new_v), _adamw(*packs, "adamw_small")):
        tgt.update(_unpack(res, spec))
    outs = [loss, grad_x[None]]
    for group in (grads, delta, new_m, new_v):
        outs += [group[n].reshape(a[n].shape) for n in WEIGHTS]
    return tuple(outs)


def kernel(x, c, ada_w, ada_b, even_w_in, even_w_out, gla_w_lr, gla_b_lr, gla_gain, fox_b_f, fox_q_gain, fox_k_gain, odd_w_in,
           odd_w_out, s5_lam_re, s5_lam_im, s5_log_dt, s5_b_re, s5_b_im, s5_c_re, s5_c_im, s5_d, s5_w_glu, s5_b_glu, sgu_ln_gain,
           sgu_ln_bias, sgu_w_s, sgu_b_s, mlp_w1, mlp_w2, loss_target, m_ada_w, m_ada_b, m_even_w_in, m_even_w_out, m_gla_w_lr,
           m_gla_b_lr, m_gla_gain, m_fox_b_f, m_fox_q_gain, m_fox_k_gain, m_odd_w_in, m_odd_w_out, m_s5_lam_re, m_s5_lam_im,
           m_s5_log_dt, m_s5_b_re, m_s5_b_im, m_s5_c_re, m_s5_c_im, m_s5_d, m_s5_w_glu, m_s5_b_glu, m_sgu_ln_gain, m_sgu_ln_bias,
           m_sgu_w_s, m_sgu_b_s, m_mlp_w1, m_mlp_w2, v_ada_w, v_ada_b, v_even_w_in, v_even_w_out, v_gla_w_lr, v_gla_b_lr,
           v_gla_gain, v_fox_b_f, v_fox_q_gain, v_fox_k_gain, v_odd_w_in, v_odd_w_out, v_s5_lam_re, v_s5_lam_im, v_s5_log_dt,
           v_s5_b_re, v_s5_b_im, v_s5_c_re, v_s5_c_im, v_s5_d, v_s5_w_glu, v_s5_b_glu, v_sgu_ln_gain, v_sgu_ln_bias, v_sgu_w_s,
           v_sgu_b_s, v_mlp_w1, v_mlp_w2):
    return _step((x, c, ada_w, ada_b, even_w_in, even_w_out, gla_w_lr, gla_b_lr, gla_gain, fox_b_f, fox_q_gain, fox_k_gain,
                  odd_w_in, odd_w_out, s5_lam_re, s5_lam_im, s5_log_dt, s5_b_re, s5_b_im, s5_c_re, s5_c_im, s5_d, s5_w_glu,
                  s5_b_glu, sgu_ln_gain, sgu_ln_bias, sgu_w_s, sgu_b_s, mlp_w1, mlp_w2, loss_target, m_ada_w, m_ada_b,
                  m_even_w_in, m_even_w_out, m_gla_w_lr, m_gla_b_lr, m_gla_gain, m_fox_b_f, m_fox_q_gain, m_fox_k_gain,
                  m_odd_w_in, m_odd_w_out, m_s5_lam_re, m_s5_lam_im, m_s5_log_dt, m_s5_b_re, m_s5_b_im, m_s5_c_re, m_s5_c_im,
                  m_s5_d, m_s5_w_glu, m_s5_b_glu, m_sgu_ln_gain, m_sgu_ln_bias, m_sgu_w_s, m_sgu_b_s, m_mlp_w1, m_mlp_w2, v_ada_w,
                  v_ada_b, v_even_w_in, v_even_w_out, v_gla_w_lr, v_gla_b_lr, v_gla_gain, v_fox_b_f, v_fox_q_gain, v_fox_k_gain,
                  v_odd_w_in, v_odd_w_out, v_s5_lam_re, v_s5_lam_im, v_s5_log_dt, v_s5_b_re, v_s5_b_im, v_s5_c_re, v_s5_c_im,
                  v_s5_d, v_s5_w_glu, v_s5_b_glu, v_sgu_ln_gain, v_sgu_ln_bias, v_sgu_w_s, v_sgu_b_s, v_mlp_w1, v_mlp_w2))
```

```python
import functools
import math

import jax
import jax.numpy as jnp
import numpy as np
from jax import lax
from jax.experimental import pallas as pl
from jax.experimental.pallas import tpu as pltpu

F32 = jnp.float32
BF16 = jnp.bfloat16
MESH = pl.DeviceIdType.MESH
ANY = pl.BlockSpec(memory_space=pl.ANY)
DMA_SEM = pltpu.SemaphoreType.DMA

D_MODEL = 1024
HEAD_DIM = 64
MIX_HALF = 512
GLA_RANK = 16
GLA_TAU = 16.0
GLA_CHUNK = 64
S5_GROUPS = 32
S5_GROUP_WIDTH = 16
S5_STATE = 64
S5_N = S5_GROUPS * S5_STATE
SGU_GROUPS = 8
SGU_CHUNK = 128
D_FF = 4096
EPS = 1e-6
N_CHIPS = 4
LANES = 128
VMEM_LIMIT = 48 * 1024 * 1024
PAIR_COPIES = 16

ADAM_LR = 0.001
ADAM_B1 = 0.9
ADAM_B2 = 0.999
ADAM_EPS = 1e-08
ADAM_WD = 0.01
ADAM_STEP = 10


def _cparams(*sem):
    return pltpu.CompilerParams(dimension_semantics=sem, vmem_limit_bytes=VMEM_LIMIT)


def _pair_swap(x, name):
    lead = x.shape[:-2]
    rows = x.shape[-2]
    nsplit = max(1, PAIR_COPIES // max(1, math.prod(lead)))
    while nsplit > 1 and rows % (nsplit * 16):
        nsplit -= 1
    pieces = [idx + (pl.ds(j * (rows // nsplit), rows // nsplit),) for idx in np.ndindex(*lead) for j in range(nsplit)]

    def body(x_ref, o_ref, send_sems, recv_sems):
        mx, my, mc = lax.axis_index("x"), lax.axis_index("y"), lax.axis_index("c")
        copies = [pltpu.make_async_remote_copy(src_ref=x_ref.at[p], dst_ref=o_ref.at[p], send_sem=send_sems.at[j], recv_sem=recv_sems.at[j],
                                               device_id=(mx, my, 1 - mc), device_id_type=MESH) for j, p in enumerate(pieces)]
        for cp in copies:
            cp.start()
        for cp in copies:
            cp.wait_recv()
        for cp in copies:
            cp.wait_send()

    return pl.pallas_call(
        body, name=name, out_shape=jax.ShapeDtypeStruct(x.shape, x.dtype), in_specs=[ANY], out_specs=ANY,
        scratch_shapes=[DMA_SEM((len(pieces),)), DMA_SEM((len(pieces),))])(x)


def _by_core(mine, theirs):
    first = lax.axis_index("c") == 0
    return jnp.stack([jnp.where(first, mine, theirs), jnp.where(first, theirs, mine)])


def _chip_exchange(x, name, bcast):
    blk = x.shape if bcast else x.shape[1:]

    def body(x_ref, o_ref, send_sems, recv_sems, loc_sem):
        mx, my, mc = lax.axis_index("x"), lax.axis_index("y"), lax.axis_index("c")
        me = 2 * mx + my
        peers = [(1 - mx, my), (mx, 1 - my), (1 - mx, 1 - my)]

        def src(k):
            return x_ref if bcast else x_ref.at[k]

        loc = pltpu.make_async_copy(src(me), o_ref.at[me], loc_sem)
        loc.start()
        sends = []
        for j, (px, py) in enumerate(peers):
            cp = pltpu.make_async_remote_copy(src_ref=src(2 * px + py), dst_ref=o_ref.at[me], send_sem=send_sems.at[j],
                                              recv_sem=recv_sems.at[j], device_id=(px, py, mc), device_id_type=MESH)
            cp.start()
            sends.append(cp)
        for j, (px, py) in enumerate(peers):
            pltpu.make_async_remote_copy(src_ref=src(me), dst_ref=o_ref.at[2 * px + py], send_sem=send_sems.at[j],
                                         recv_sem=recv_sems.at[j], device_id=(px, py, mc), device_id_type=MESH).wait_recv()
        for cp in sends:
            cp.wait_send()
        loc.wait()

    return pl.pallas_call(
        body, name=name, out_shape=jax.ShapeDtypeStruct((N_CHIPS,) + tuple(blk), x.dtype), in_specs=[ANY], out_specs=ANY,
        scratch_shapes=[DMA_SEM((3,)), DMA_SEM((3,)), DMA_SEM])(x)


def _gather8(x, name):
    collected = _chip_exchange(x, name + "_chips", True)
    return jnp.swapaxes(_by_core(collected, _pair_swap(collected, name + "_pair")), 0, 1)


def _tile(n, want):
    if n <= want:
        return n
    t = (want // LANES) * LANES
    while t >= LANES:
        if n % t == 0:
            return t
        t -= LANES
    raise ValueError(f"no lane-aligned tile for {n}")


_DIMS = {"nn": (((1,), (0,)), ((), ())), "nt": (((1,), (1,)), ((), ())), "tn": (((0,), (0,)), ((), ()))}


MM_FULL_K = 4096
MM_SLAB_K = 2048
MM_TILES = ((1024, 1024), (512, 1024), (1024, 512), (512, 512), (256, 512), (256, 256))
MM_VMEM_BUDGET = 36 * 1024 * 1024


def _mm(a, b, mode, name, *, a_pro=None, epi=None, extras=(), out_dtype=F32, tm_max=1024, tn_max=1024, tk=None, a_cols=None):
    c0, csize = a_cols if a_cols is not None else (0, a.shape[1])
    if mode == "tn":
        K, M = a.shape[0], csize
    else:
        M, K = a.shape[0], csize
    N = b.shape[0] if mode == "nt" else b.shape[1]
    assert (b.shape[1] if mode == "nt" else b.shape[0]) == K, (a.shape, b.shape, mode)
    if tk is None:
        tk = K if (mode != "tn" and K <= MM_FULL_K) else MM_SLAB_K
    tk = _tile(K, tk)
    nk = K // tk
    n_mn = sum(1 for _, kind in extras if kind == "mn")
    for tm_want, tn_want in MM_TILES:
        tm, tn = _tile(M, min(tm_want, tm_max)), _tile(N, min(tn_want, tn_max))
        need = 2 * (tm * tk * a.dtype.itemsize + tk * tn * b.dtype.itemsize + tm * tn * 4 * (1 + n_mn)) + tm * tn * 4 * (nk > 1)
        if need <= MM_VMEM_BUDGET:
            break
    if mode == "tn":
        assert c0 % tm == 0
        a_spec = pl.BlockSpec((tk, tm), lambda i, j, k: (k, i + c0 // tm))
    else:
        assert c0 % tk == 0
        a_spec = pl.BlockSpec((tm, tk), lambda i, j, k: (i, k + c0 // tk))
    b_spec = pl.BlockSpec((tn, tk), lambda i, j, k: (j, k)) if mode == "nt" else pl.BlockSpec((tk, tn), lambda i, j, k: (k, j))
    ex_specs = []
    for arr, kind in extras:
        if kind == "mn":
            assert arr.shape == (M, N)
            ex_specs.append(pl.BlockSpec((tm, tn), lambda i, j, k: (i, j)))
        else:
            assert arr.shape == (1, N)
            ex_specs.append(pl.BlockSpec((1, tn), lambda i, j, k: (0, j)))
    n_ex = len(extras)

    def body(*refs):
        a_ref, b_ref = refs[:2]
        ex_refs = refs[2:2 + n_ex]
        o_ref = refs[2 + n_ex]
        acc_ref = refs[3 + n_ex] if nk > 1 else None
        k = pl.program_id(2)
        av = a_ref[...]
        if a_pro is not None:
            av = a_pro(av)
        part = lax.dot_general(av.astype(BF16), b_ref[...].astype(BF16), _DIMS[mode], preferred_element_type=F32)
        if nk == 1:
            if epi is not None:
                part = epi(part, *[r[...] for r in ex_refs])
            o_ref[...] = part.astype(o_ref.dtype)
            return

        @pl.when(k == 0)
        def _():
            acc_ref[...] = part

        @pl.when(k > 0)
        def _():
            acc_ref[...] += part

        @pl.when(k == nk - 1)
        def _():
            acc = acc_ref[...]
            if epi is not None:
                acc = epi(acc, *[r[...] for r in ex_refs])
            o_ref[...] = acc.astype(o_ref.dtype)

    return pl.pallas_call(
        body, name=name, grid=(M // tm, N // tn, nk),
        in_specs=[a_spec, b_spec] + ex_specs,
        out_specs=pl.BlockSpec((tm, tn), lambda i, j, k: (i, j)),
        out_shape=jax.ShapeDtypeStruct((M, N), out_dtype),
        scratch_shapes=[pltpu.VMEM((tm, tn), F32)] if nk > 1 else [],
        compiler_params=_cparams("parallel", "parallel", "arbitrary"))(a, b, *[e[0] for e in extras])


ROWS = 256


def _row_spec(w, ts=ROWS):
    return pl.BlockSpec((ts, w), lambda i: (i, 0))


def _vec_spec(w):
    return pl.BlockSpec((1, w), lambda i: (0, 0))


def _res_rms(x, sc, sh, name, y=None, g=None):
    S, D = x.shape
    has_res = y is not None

    def body(*refs):
        if has_res:
            x_ref, y_ref, g_ref, sc_ref, sh_ref, xo_ref, h_ref = refs
            xv = x_ref[...] + g_ref[...] * y_ref[...]
            xo_ref[...] = xv
        else:
            x_ref, sc_ref, sh_ref, h_ref = refs
            xv = x_ref[...]
        r = lax.rsqrt(jnp.mean(xv * xv, axis=-1, keepdims=True) + EPS)
        h_ref[...] = (xv * r * (1.0 + sc_ref[...]) + sh_ref[...]).astype(BF16)

    row, vec = _row_spec(D), _vec_spec(D)
    if has_res:
        return pl.pallas_call(body, name=name, grid=(S // ROWS,), in_specs=[row, row, vec, vec, vec], out_specs=[row, row],
                              out_shape=[jax.ShapeDtypeStruct((S, D), F32), jax.ShapeDtypeStruct((S, D), BF16)],
                              compiler_params=_cparams("parallel"))(x, y, g, sc, sh)
    h = pl.pallas_call(body, name=name, grid=(S // ROWS,), in_specs=[row, vec, vec], out_specs=row,
                       out_shape=jax.ShapeDtypeStruct((S, D), BF16), compiler_params=_cparams("parallel"))(x, sc, sh)
    return x, h


def _res_rms_bwd(x, dh, sc, dres, name, y=None, g=None):
    S, D = x.shape
    has_res = y is not None

    def body(*refs):
        if has_res:
            x_ref, dh_ref, sc_ref, dres_ref, y_ref, g_ref, dx_ref, dy_ref, dg_ref, dsc_ref, dsh_ref = refs
        else:
            x_ref, dh_ref, sc_ref, dres_ref, dx_ref, dsc_ref, dsh_ref = refs
        first = pl.program_id(0) == 0
        xv = x_ref[...]
        dh = dh_ref[...]
        r = lax.rsqrt(jnp.mean(xv * xv, axis=-1, keepdims=True) + EPS)
        xn = xv * r
        dxn = dh * (1.0 + sc_ref[...])
        dx = dres_ref[...] + r * (dxn - xn * jnp.mean(dxn * xn, axis=-1, keepdims=True))
        dx_ref[...] = dx
        parts = [(dsc_ref, jnp.sum(dh * xn, axis=0, keepdims=True)), (dsh_ref, jnp.sum(dh, axis=0, keepdims=True))]
        if has_res:
            dy_ref[...] = (dx * g_ref[...]).astype(BF16)
            parts.append((dg_ref, jnp.sum(dx * y_ref[...], axis=0, keepdims=True)))
        for ref, val in parts:
            @pl.when(first)
            def _(ref=ref, val=val):
                ref[...] = val

            @pl.when(jnp.logical_not(first))
            def _(ref=ref, val=val):
                ref[...] += val

    row, vec = _row_spec(D), _vec_spec(D)
    full = jax.ShapeDtypeStruct((S, D), F32)
    v = jax.ShapeDtypeStruct((1, D), F32)
    if has_res:
        return pl.pallas_call(body, name=name, grid=(S // ROWS,), in_specs=[row, row, vec, row, row, vec],
                              out_specs=[row, row, vec, vec, vec], out_shape=[full, jax.ShapeDtypeStruct((S, D), BF16), v, v, v],
                              compiler_params=_cparams("arbitrary"))(x, dh, sc, dres, y, g)
    return pl.pallas_call(body, name=name, grid=(S // ROWS,), in_specs=[row, row, vec, row],
                          out_specs=[row, vec, vec], out_shape=[full, v, v],
                          compiler_params=_cparams("arbitrary"))(x, dh, sc, dres)


def _res_loss(x, m, g, target, name):
    S, D = x.shape

    def body(x_ref, m_ref, g_ref, t_ref, loss_ref, dx_ref, dm_ref, dg_ref):
        first = pl.program_id(0) == 0
        mv = m_ref[...]
        err = x_ref[...] + g_ref[...] * mv - t_ref[...]
        dx = err * (1.0 / D)
        dx_ref[...] = dx
        dm_ref[...] = (dx * g_ref[...]).astype(BF16)
        part = 0.5 * jnp.sum(jnp.mean(err * err, axis=-1, keepdims=True), axis=0, keepdims=True)
        dg = jnp.sum(dx * mv, axis=0, keepdims=True)

        @pl.when(first)
        def _():
            loss_ref[...] = jnp.broadcast_to(part, loss_ref.shape)
            dg_ref[...] = dg

        @pl.when(jnp.logical_not(first))
        def _():
            loss_ref[...] += jnp.broadcast_to(part, loss_ref.shape)
            dg_ref[...] += dg

    row, vec = _row_spec(D), _vec_spec(D)
    full = jax.ShapeDtypeStruct((S, D), F32)
    return pl.pallas_call(body, name=name, grid=(S // ROWS,), in_specs=[row, row, vec, row],
                          out_specs=[pl.BlockSpec((8, LANES), lambda i: (0, 0)), row, row, vec],
                          out_shape=[jax.ShapeDtypeStruct((8, LANES), F32), full, jax.ShapeDtypeStruct((S, D), BF16), jax.ShapeDtypeStruct((1, D), F32)],
                          compiler_params=_cparams("arbitrary"))(x, m, g, target)


def _adamw(w, g, m, v, name):
    R, C = w.shape
    tr = R if R <= 256 else 256
    assert R % tr == 0

    def body(w_ref, g_ref, m_ref, v_ref, d_ref, nm_ref, nv_ref):
        gv = g_ref[...]
        nm = ADAM_B1 * m_ref[...] + (1.0 - ADAM_B1) * gv
        nv = ADAM_B2 * v_ref[...] + (1.0 - ADAM_B2) * jnp.square(gv)
        m_hat = nm / (1.0 - ADAM_B1 ** ADAM_STEP)
        v_hat = nv / (1.0 - ADAM_B2 ** ADAM_STEP)
        d_ref[...] = -ADAM_LR * (m_hat / (jnp.sqrt(v_hat) + ADAM_EPS) + ADAM_WD * w_ref[...])
        nm_ref[...] = nm
        nv_ref[...] = nv

    spec = pl.BlockSpec((tr, C), lambda i: (i, 0))
    out = jax.ShapeDtypeStruct((R, C), F32)
    return pl.pallas_call(body, name=name, grid=(R // tr,), in_specs=[spec] * 4, out_specs=[spec] * 3,
                          out_shape=[out, out, out], compiler_params=_cparams("parallel"))(w, g, m, v)


def _sum_slots(x, name):
    n, R, C = x.shape
    tr = R if R <= 256 else 256
    assert R % tr == 0

    def body(x_ref, o_ref):
        acc = x_ref[0].astype(F32)
        for j in range(1, n):
            acc = acc + x_ref[j].astype(F32)
        o_ref[...] = acc

    return pl.pallas_call(body, name=name, grid=(R // tr,), in_specs=[pl.BlockSpec((n, tr, C), lambda i: (0, i, 0))],
                          out_specs=pl.BlockSpec((tr, C), lambda i: (i, 0)), out_shape=jax.ShapeDtypeStruct((R, C), F32),
                          compiler_params=_cparams("parallel"))(x)


def _ew(fn, name, tiled, consts=(), outs=(), sums=(), ts=ROWS):
    tiled = [t if isinstance(t, tuple) else (t, t.shape[1], 0) for t in tiled]
    S = tiled[0][0].shape[0]
    n_t, n_c, n_o, n_s = len(tiled), len(consts), len(outs), len(sums)

    def body(*refs):
        ins = [r[...] for r in refs[:n_t + n_c]]
        res = fn(*ins)
        res = res if isinstance(res, (tuple, list)) else (res,)
        assert len(res) == n_o + n_s
        o_refs = refs[n_t + n_c:]
        for r, val in zip(o_refs[:n_o], res[:n_o]):
            r[...] = val.astype(r.dtype)
        first = pl.program_id(0) == 0
        for r, val in zip(o_refs[n_o:], res[n_o:]):
            @pl.when(first)
            def _(r=r, val=val):
                r[...] = val

            @pl.when(jnp.logical_not(first))
            def _(r=r, val=val):
                r[...] += val

    in_specs = [pl.BlockSpec((ts, w), lambda i, cb=cb: (i, cb)) for _, w, cb in tiled]
    in_specs += [pl.BlockSpec(c.shape, lambda i, nd=c.ndim: (0,) * nd) for c in consts]
    out_specs = [_row_spec(w, ts) for w, _ in outs] + [_vec_spec(w) for w in sums]
    out_shape = [jax.ShapeDtypeStruct((S, w), dt) for w, dt in outs] + [jax.ShapeDtypeStruct((1, w), F32) for w in sums]
    res = pl.pallas_call(body, name=name, grid=(S // ts,), in_specs=in_specs, out_specs=out_specs, out_shape=out_shape,
                         compiler_params=_cparams("arbitrary" if sums else "parallel"))(*[t[0] for t in tiled], *consts)
    return res


_GELU_C = math.sqrt(2.0 / math.pi)


def _gelu(x):
    return 0.5 * x * (1.0 + jnp.tanh(_GELU_C * (x + 0.044715 * x * x * x)))


def _dgelu(x):
    t = jnp.tanh(_GELU_C * (x + 0.044715 * x * x * x))
    return 0.5 * (1.0 + t) + 0.5 * x * (1.0 - t * t) * _GELU_C * (1.0 + 3.0 * 0.044715 * x * x)


def _sigmoid(x):
    return 1.0 / (1.0 + jnp.exp(-x))


def _log_sigmoid(x):
    return jnp.minimum(x, 0.0) - jnp.log(1.0 + jnp.exp(-jnp.abs(x)))


SCAN_T = 128
SCAN_TB = 512


def _cmul(ar, ai, br, bi):
    return ar * br - ai * bi, ar * bi + ai * br


def _s5_discretise(lam_re, lam_im, log_dt, b_re, b_im):
    dt = jnp.exp(log_dt)[:, None]
    mag = jnp.exp(lam_re * dt)
    ang = lam_im * dt
    abar_re = mag * jnp.cos(ang)
    abar_im = mag * jnp.sin(ang)
    den = lam_re * lam_re + lam_im * lam_im
    coef_re = ((abar_re - 1.0) * lam_re + abar_im * lam_im) / den
    coef_im = (abar_im * lam_re - (abar_re - 1.0) * lam_im) / den
    bbar_re = coef_re[..., None] * b_re - coef_im[..., None] * b_im
    bbar_im = coef_re[..., None] * b_im + coef_im[..., None] * b_re
    return abar_re, abar_im, bbar_re, bbar_im


def _planes(re, im):
    lead = re.shape[:-1]
    return jnp.stack([re.reshape(lead + (-1, LANES)), im.reshape(lead + (-1, LANES))], axis=-2).reshape(lead + (-1,))


def _unplanes(x):
    lead = x.shape[:-1]
    x4 = x.reshape(lead + (-1, 2, LANES))
    return x4[..., 0, :].reshape(lead + (-1,)), x4[..., 1, :].reshape(lead + (-1,))


def _s5_scan_tables(a_re, a_im, reverse):
    pr, pi = [a_re], [a_im]
    for _ in range(7):
        r, i = _cmul(pr[-1], pi[-1], pr[-1], pi[-1])
        pr.append(r)
        pi.append(i)
    apow = _planes(jnp.stack(pr), jnp.stack(pi))
    n = np.arange(1, SCAN_T + 1)
    if reverse:
        n = n[::-1]
    tr = jnp.ones((SCAN_T, a_re.shape[0]), F32)
    ti = jnp.zeros((SCAN_T, a_re.shape[0]), F32)
    for k in range(8):
        bit = jnp.asarray(((n >> k) & 1).astype(np.float32))[:, None]
        mr = bit * pr[k][None, :] + (1.0 - bit)
        mi = bit * pi[k][None, :]
        tr, ti = _cmul(tr, ti, mr, mi)
    return apow, _planes(tr, ti)


def _s5_scan(bu, apow, ptab, name, reverse, x_fwd=None):
    S, N2 = bu.shape
    T, W = SCAN_T, 2 * LANES
    tb = min(SCAN_TB, S)
    nt, nsub = S // tb, tb // T
    order = list(range(nsub - 1, -1, -1) if reverse else range(nsub))
    with_da = x_fwd is not None

    def tblk(t):
        return (nt - 1 - t) if reverse else t

    def shifted(v, k, rowi):
        s = 1 << k
        if s < 8:
            if reverse:
                return jnp.where(rowi < T - s, pltpu.roll(v, T - s, 0), 0.0)
            return jnp.where(rowi >= s, pltpu.roll(v, s, 0), 0.0)
        z = jnp.zeros((s, LANES), F32)
        return jnp.concatenate([v[s:], z], axis=0) if reverse else jnp.concatenate([z, v[:T - s]], axis=0)

    def body(*refs):
        if with_da:
            bu_ref, ap_ref, pt_ref, xf_ref, xp_ref, x_ref, da_ref, carry_ref = refs
        else:
            bu_ref, ap_ref, pt_ref, x_ref, carry_ref = refs
        t = pl.program_id(1)

        @pl.when(t == 0)
        def _():
            carry_ref[...] = jnp.zeros_like(carry_ref)
            if with_da:
                da_ref[...] = jnp.zeros_like(da_ref)

        rowi = lax.broadcasted_iota(jnp.int32, (T, LANES), 0)
        pr, pi = pt_ref[:, :LANES], pt_ref[:, LANES:]
        cr, ci = carry_ref[0:1, :LANES], carry_ref[0:1, LANES:]
        for sb in order:
            rows = pl.ds(sb * T, T)
            xr, xi = bu_ref[rows, :LANES], bu_ref[rows, LANES:]
            for k in range(7):
                ar, ai = ap_ref[k:k + 1, :LANES], ap_ref[k:k + 1, LANES:]
                rr, ri = shifted(xr, k, rowi), shifted(xi, k, rowi)
                xr, xi = xr + ar * rr - ai * ri, xi + ar * ri + ai * rr
            xr, xi = xr + pr * cr - pi * ci, xi + pr * ci + pi * cr
            x_ref[rows, :LANES] = xr
            x_ref[rows, LANES:] = xi
            edge = pl.ds(sb * T + (0 if reverse else T - 1), 1)
            cr, ci = x_ref[edge, :LANES], x_ref[edge, LANES:]
            if with_da:
                if sb > 0:
                    before = pl.ds(sb * T - 1, 1)
                    b_r, b_i = xf_ref[before, :LANES], xf_ref[before, LANES:]
                else:
                    keep = (tblk(t) > 0).astype(F32)
                    b_r, b_i = xp_ref[7:8, :LANES] * keep, xp_ref[7:8, LANES:] * keep
                fr, fi = xf_ref[rows, :LANES], xf_ref[rows, LANES:]
                qr = jnp.where(rowi >= 1, pltpu.roll(fr, 1, 0), b_r)
                qi = jnp.where(rowi >= 1, pltpu.roll(fi, 1, 0), b_i)
                gr, gi = xr * qr + xi * qi, xi * qr - xr * qi
                sr, si = gr[0:8], gi[0:8]
                for j in range(1, T // 8):
                    sr, si = sr + gr[8 * j:8 * j + 8], si + gi[8 * j:8 * j + 8]
                da_ref[:, :LANES] += sr
                da_ref[:, LANES:] += si
        carry_ref[0:1, :LANES] = cr
        carry_ref[0:1, LANES:] = ci

    blk = pl.BlockSpec((tb, W), lambda j, t: (tblk(t), j))
    in_specs = [blk, pl.BlockSpec((8, W), lambda j, t: (0, j)), pl.BlockSpec((T, W), lambda j, t: (0, j))]
    out_specs, out_shape = [blk], [jax.ShapeDtypeStruct((S, N2), F32)]
    args = [bu, apow, ptab]
    if with_da:
        in_specs += [blk, pl.BlockSpec((8, W), lambda j, t: (jnp.maximum(tblk(t) * (tb // 8) - 1, 0), j))]
        out_specs.append(pl.BlockSpec((8, W), lambda j, t: (0, j)))
        out_shape.append(jax.ShapeDtypeStruct((8, N2), F32))
        args += [x_fwd, x_fwd]
    res = pl.pallas_call(body, name=name, grid=(N2 // W, nt), in_specs=in_specs, out_specs=out_specs, out_shape=out_shape,
                         scratch_shapes=[pltpu.VMEM((8, W), F32)], compiler_params=_cparams("parallel", "arbitrary"))(*args)
    return res if with_da else res[0]


def _block_diag(t):
    G, a, b = t.shape
    return (t[:, :, None, :] * jnp.eye(G, dtype=t.dtype)[:, None, :, None]).reshape(G * a, G * b)


def _block_diag_take(m, G):
    a, b = m.shape[0] // G, m.shape[1] // G
    m4 = m.reshape(G, a, G, b)
    return jnp.sum(m4 * jnp.eye(G, dtype=m.dtype)[:, None, :, None], axis=2)


def _s5_block_fwd(u, w, pfx):
    a_re, a_im, bb_re, bb_im = _s5_discretise(w["lam_re"], w["lam_im"], w["log_dt"], w["b_re"], w["b_im"])
    bcat = _planes(_block_diag(bb_re).T, _block_diag(bb_im).T).astype(BF16)
    ccat = _planes(_block_diag(jnp.swapaxes(w["c_re"], 1, 2)).T, -_block_diag(jnp.swapaxes(w["c_im"], 1, 2)).T).T.astype(BF16)
    af_re, af_im = a_re.reshape(-1), a_im.reshape(-1)
    apow, ptab = _s5_scan_tables(af_re, af_im, False)
    bu = _mm(u, bcat, "nn", pfx + "_bu")
    x = _s5_scan(bu, apow, ptab, pfx + "_scan", False)
    d_row = w["d"].reshape(1, MIX_HALF)
    ys = _mm(x, ccat, "nn", pfx + "_y", epi=lambda acc, ut, dr: acc + dr * ut, extras=[(u, "mn"), (d_row, "n")])
    z = _mm(ys, w["w_glu"], "nn", pfx + "_glu", a_pro=_gelu, epi=lambda acc, b: acc + b, extras=[(w["b_glu"].reshape(1, -1), "n")])
    y2, = _ew(lambda ysv, zv: _gelu(ysv) * _sigmoid(zv), pfx + "_gate", [ys, z], outs=[(MIX_HALF, F32)])
    return y2, dict(u=u, x=x, ys=ys, z=z, bcat=bcat, ccat=ccat, a=(af_re, af_im), d_row=d_row)


def _s5_block_bwd(dy2, w, res, pfx):
    u, x, ys, z, bcat, ccat = res["u"], res["x"], res["ys"], res["z"], res["bcat"], res["ccat"]

    def gate_bwd(dy, ysv, zv):
        sg = _sigmoid(zv)
        dz = dy * _gelu(ysv) * sg * (1.0 - sg)
        return dz, jnp.sum(dz, axis=0, keepdims=True)

    dz, db_glu = _ew(gate_bwd, pfx + "_gate_bwd", [dy2, ys, z], outs=[(MIX_HALF, F32)], sums=[MIX_HALF])
    dw_glu = _mm(ys, dz, "tn", pfx + "_dwglu", a_pro=_gelu)
    dys = _mm(dz, w["w_glu"], "nt", pfx + "_dys", epi=lambda acc, dy, zv, ysv: (acc + dy * _sigmoid(zv)) * _dgelu(ysv),
              extras=[(dy2, "mn"), (z, "mn"), (ys, "mn")])
    dd, = _ew(lambda a, b: jnp.sum(a * b, axis=0, keepdims=True), pfx + "_dd", [dys, u], sums=[MIX_HALF])
    dccat = _mm(x, dys, "tn", pfx + "_dc")
    dx = _mm(dys, ccat, "nt", pfx + "_dx")
    af_re, af_im = res["a"]
    apow, ptab = _s5_scan_tables(af_re, -af_im, True)
    lam, da8 = _s5_scan(dx, apow, ptab, pfx + "_scan_bwd", True, x_fwd=x)
    dbcat = _mm(u, lam, "tn", pfx + "_db")
    du = _mm(lam, bcat, "nt", pfx + "_du", epi=lambda acc, dyv, dr: acc + dyv * dr, extras=[(dys, "mn"), (res["d_row"], "n")])
    G = S5_GROUPS
    d_abar_re, d_abar_im = (t.reshape(G, S5_STATE) for t in _unplanes(jnp.sum(da8, axis=0)))
    d_bb_re, d_bb_im = (_block_diag_take(t.T, G) for t in _unplanes(dbcat))
    _, vjp = jax.vjp(_s5_discretise, w["lam_re"], w["lam_im"], w["log_dt"], w["b_re"], w["b_im"])
    g_lam_re, g_lam_im, g_log_dt, g_b_re, g_b_im = vjp((d_abar_re, d_abar_im, d_bb_re, d_bb_im))
    dc_re, dc_im = _unplanes(dccat.T)
    g_c_re = jnp.swapaxes(_block_diag_take(dc_re.T, G), 1, 2)
    g_c_im = -jnp.swapaxes(_block_diag_take(dc_im.T, G), 1, 2)
    grads = dict(lam_re=g_lam_re, lam_im=g_lam_im, log_dt=g_log_dt, b_re=g_b_re, b_im=g_b_im, c_re=g_c_re, c_im=g_c_im,
                 d=dd.reshape(G, S5_GROUP_WIDTH), w_glu=dw_glu, b_glu=db_glu.reshape(-1))
    return du, grads


SGU_TS = 512
N_PAIRS = MIX_HALF // LANES


def _half_masks(rows):
    lane = lax.broadcasted_iota(jnp.int32, (rows, LANES), 1)
    left = (lane < HEAD_DIM).astype(F32)
    return left, 1.0 - left


def _sgu_norm(zv, gain, bias):
    v = _gelu(zv)
    mu = jnp.mean(v, axis=-1, keepdims=True)
    vc = v - mu
    rstd = lax.rsqrt(jnp.mean(vc * vc, axis=-1, keepdims=True) + EPS)
    vhat = vc * rstd
    return vhat, rstd, vhat * gain + bias


def _sgu_tables(w_s, b_s):
    mask = jnp.tril(jnp.ones((SGU_CHUNK, SGU_CHUNK), dtype=bool))
    wm = jnp.where(mask[None], w_s, 0.0).astype(BF16)
    bias_tab = jnp.repeat(b_s.T, MIX_HALF // SGU_GROUPS, axis=1)
    return wm, bias_tab


def _sgu_fwd(proj, ln_gain, ln_bias, wm, bias_tab, name):
    S = proj.shape[0]
    nch = SGU_TS // SGU_CHUNK

    def body(zu_ref, zv_ref, g_ref, b_ref, w_ref, bt_ref, o_ref):
        left, right = _half_masks(SGU_CHUNK)
        _, _, vn = _sgu_norm(zv_ref[...], g_ref[...], b_ref[...])
        for ch in range(nch):
            rows = pl.ds(ch * SGU_CHUNK, SGU_CHUNK)
            for p in range(N_PAIRS):
                cols = pl.ds(p * LANES, LANES)
                vp = vn[ch * SGU_CHUNK:(ch + 1) * SGU_CHUNK, p * LANES:(p + 1) * LANES]
                mixed = (jnp.dot(w_ref[2 * p], (vp * left).astype(BF16), preferred_element_type=F32)
                         + jnp.dot(w_ref[2 * p + 1], (vp * right).astype(BF16), preferred_element_type=F32) + bt_ref[:, cols])
                o_ref[rows, cols] = _gelu(zu_ref[rows, cols]) * mixed

    vec = _vec_spec(MIX_HALF)
    return pl.pallas_call(
        body, name=name, grid=(S // SGU_TS,),
        in_specs=[pl.BlockSpec((SGU_TS, MIX_HALF), lambda i: (i, 1)), pl.BlockSpec((SGU_TS, MIX_HALF), lambda i: (i, 2)), vec, vec,
                  pl.BlockSpec((SGU_GROUPS, SGU_CHUNK, SGU_CHUNK), lambda i: (0, 0, 0)), pl.BlockSpec((SGU_CHUNK, MIX_HALF), lambda i: (0, 0))],
        out_specs=_row_spec(MIX_HALF, SGU_TS), out_shape=jax.ShapeDtypeStruct((S, MIX_HALF), F32),
        compiler_params=_cparams("parallel"))(proj, proj, ln_gain, ln_bias, wm, bias_tab)


def _sgu_bwd(dout, proj, ln_gain, ln_bias, wm, bias_tab, name):
    S = proj.shape[0]
    nch = SGU_TS // SGU_CHUNK
    nt_dims = (((1,), (1,)), ((), ()))
    tn_dims = (((0,), (0,)), ((), ()))

    def body(do_ref, zu_ref, zv_ref, g_ref, b_ref, w_ref, bt_ref, dzu_ref, dzv_ref, dw_ref, dbt_ref, dg_ref, db_ref, dvn_ref):
        first = pl.program_id(0) == 0

        @pl.when(first)
        def _():
            dw_ref[...] = jnp.zeros_like(dw_ref)
            dbt_ref[...] = jnp.zeros_like(dbt_ref)
            dg_ref[...] = jnp.zeros_like(dg_ref)
            db_ref[...] = jnp.zeros_like(db_ref)

        left, right = _half_masks(SGU_CHUNK)
        zv = zv_ref[...]
        vhat, rstd, vn = _sgu_norm(zv, g_ref[...], b_ref[...])
        for ch in range(nch):
            rows = pl.ds(ch * SGU_CHUNK, SGU_CHUNK)
            for p in range(N_PAIRS):
                cols = pl.ds(p * LANES, LANES)
                vp = vn[ch * SGU_CHUNK:(ch + 1) * SGU_CHUNK, p * LANES:(p + 1) * LANES]
                vl, vr = (vp * left).astype(BF16), (vp * right).astype(BF16)
                mixed = (jnp.dot(w_ref[2 * p], vl, preferred_element_type=F32)
                         + jnp.dot(w_ref[2 * p + 1], vr, preferred_element_type=F32) + bt_ref[:, cols])
                zu = zu_ref[rows, cols]
                do = do_ref[rows, cols]
                dzu_ref[rows, cols] = do * mixed * _dgelu(zu)
                dmix = do * _gelu(zu)
                dbt_ref[:, cols] += dmix
                dl, dr = (dmix * left).astype(BF16), (dmix * right).astype(BF16)
                dw_ref[2 * p] += lax.dot_general(dl, vl, nt_dims, preferred_element_type=F32)
                dw_ref[2 * p + 1] += lax.dot_general(dr, vr, nt_dims, preferred_element_type=F32)
                dvn_ref[rows, cols] = (lax.dot_general(w_ref[2 * p], dl, tn_dims, preferred_element_type=F32)
                                       + lax.dot_general(w_ref[2 * p + 1], dr, tn_dims, preferred_element_type=F32))
        dvn = dvn_ref[...]
        dg_ref[...] += jnp.sum(dvn * vhat, axis=0, keepdims=True)
        db_ref[...] += jnp.sum(dvn, axis=0, keepdims=True)
        dvh = dvn * g_ref[...]
        dv = rstd * (dvh - jnp.mean(dvh, axis=-1, keepdims=True) - vhat * jnp.mean(dvh * vhat, axis=-1, keepdims=True))
        dzv_ref[...] = dv * _dgelu(zv)

    vec = _vec_spec(MIX_HALF)
    row = _row_spec(MIX_HALF, SGU_TS)
    wspec = pl.BlockSpec((SGU_GROUPS, SGU_CHUNK, SGU_CHUNK), lambda i: (0, 0, 0))
    tspec = pl.BlockSpec((SGU_CHUNK, MIX_HALF), lambda i: (0, 0))
    full = jax.ShapeDtypeStruct((S, MIX_HALF), F32)
    v = jax.ShapeDtypeStruct((1, MIX_HALF), F32)
    return pl.pallas_call(
        body, name=name, grid=(S // SGU_TS,),
        in_specs=[row, pl.BlockSpec((SGU_TS, MIX_HALF), lambda i: (i, 1)), pl.BlockSpec((SGU_TS, MIX_HALF), lambda i: (i, 2)), vec, vec,
                  wspec, tspec],
        out_specs=[row, row, wspec, tspec, vec, vec],
        out_shape=[full, full, jax.ShapeDtypeStruct((SGU_GROUPS, SGU_CHUNK, SGU_CHUNK), F32),
                   jax.ShapeDtypeStruct((SGU_CHUNK, MIX_HALF), F32), v, v],
        scratch_shapes=[pltpu.VMEM((SGU_TS, MIX_HALF), F32)],
        compiler_params=_cparams("arbitrary"))(dout, proj, proj, ln_gain, ln_bias, wm, bias_tab)


def _sgu_grads(dw, dbias_tab):
    mask = jnp.tril(jnp.ones((SGU_CHUNK, SGU_CHUNK), dtype=bool))
    g_w = jnp.where(mask[None], dw, 0.0)
    g_b = dbias_tab.reshape(SGU_CHUNK, SGU_GROUPS, MIX_HALF // SGU_GROUPS).sum(axis=-1).T
    return g_w, g_b


def _head_avg_matrix(w):
    idx = np.arange(w) // HEAD_DIM
    return jnp.asarray((idx[:, None] == idx[None, :]).astype(np.float32) / HEAD_DIM, dtype=BF16)


def _head_mean(t, bavg):
    hi = t.astype(BF16)
    lo = (t - hi.astype(F32)).astype(BF16)
    return jnp.dot(hi, bavg, preferred_element_type=F32) + jnp.dot(lo, bavg, preferred_element_type=F32)


def _head_rms(t, bavg):
    r = lax.rsqrt(_head_mean(t * t, bavg) + EPS)
    return t * r, r


def _head_rms_bwd(dn, n, r, bavg):
    return r * (dn - n * _head_mean(dn * n, bavg))


GLA_TS = 512
C = GLA_CHUNK
NT_DIMS = (((1,), (1,)), ((), ()))
TN_DIMS = (((0,), (0,)), ((), ()))
HI = lax.Precision.HIGHEST


def _bdot(a, b, dims=(((1,), (0,)), ((), ()))):
    return lax.dot_general(a.astype(BF16), b.astype(BF16), dims, preferred_element_type=F32)


def _gla_chunk_terms(q, k, z):
    row = lax.broadcasted_iota(jnp.int32, (C, C), 0)
    col = lax.broadcasted_iota(jnp.int32, (C, C), 1)
    lc = _log_sigmoid(z) * (1.0 / GLA_TAU)
    b = lax.dot_general((row >= col).astype(F32), lc, (((1,), (0,)), ((), ())), precision=HI, preferred_element_type=F32)
    b_last = jnp.sum(lc, axis=0, keepdims=True)
    b_mid = b[C // 2:C // 2 + 1, :]
    scale = HEAD_DIM ** -0.5
    e_b, e_q, e_k, e_l = jnp.exp(b), jnp.exp(b - b_mid), jnp.exp(b_mid - b), jnp.exp(b_last - b)
    qs = q * (scale * e_b)
    qe = q * (scale * e_q)
    ke = k * e_k
    kl = k * e_l
    return dict(e_b=e_b, e_q=e_q, e_k=e_k, e_l=e_l, qs=qs, qe=qe, ke=ke, kl=kl, dec=jnp.exp(b_last), causal=row >= col, scale=scale)


def _pair_block_diag():
    r = lax.broadcasted_iota(jnp.int32, (LANES, LANES), 0) // HEAD_DIM
    c = lax.broadcasted_iota(jnp.int32, (LANES, LANES), 1) // HEAD_DIM
    return (r == c).astype(F32)


def _gla_fwd(proj, z, name):
    S = proj.shape[0]
    nch = GLA_TS // C

    def body(q_ref, k_ref, v_ref, z_ref, o_ref, st_ref, state_ref):
        @pl.when(pl.program_id(0) == 0)
        def _():
            state_ref[...] = jnp.zeros_like(state_ref)

        left, right = _half_masks(C)
        bd = _pair_block_diag()
        for ch in range(nch):
            rows = pl.ds(ch * C, C)
            for pp in range(N_PAIRS):
                ln = pl.ds(pp * LANES, LANES)
                q, k, v = q_ref[rows, ln], k_ref[rows, ln], v_ref[rows, ln]
                t = _gla_chunk_terms(q, k, z_ref[rows, ln])
                st = state_ref[pp]
                st_ref[ch, pp] = st
                o = _bdot(t["qs"], st, NT_DIMS)
                for m in (left, right):
                    a = jnp.where(t["causal"], _bdot(t["qe"] * m, t["ke"], NT_DIMS), 0.0)
                    o = o + m * _bdot(a, v)
                o_ref[rows, ln] = o
                state_ref[pp] = st * t["dec"] + bd * _bdot(v, t["kl"], TN_DIMS)

    def col(cb):
        return pl.BlockSpec((GLA_TS, MIX_HALF), lambda i: (i, cb))

    return pl.pallas_call(
        body, name=name, grid=(S // GLA_TS,),
        in_specs=[col(0), col(1), col(2), col(0)],
        out_specs=[col(0), pl.BlockSpec((nch, N_PAIRS, LANES, LANES), lambda i: (i, 0, 0, 0))],
        out_shape=[jax.ShapeDtypeStruct((S, MIX_HALF), F32), jax.ShapeDtypeStruct((S // C, N_PAIRS, LANES, LANES), F32)],
        scratch_shapes=[pltpu.VMEM((N_PAIRS, LANES, LANES), F32)], compiler_params=_cparams("arbitrary"))(proj, proj, proj, z)


def _gla_bwd(do, proj, z, states, name):
    S = proj.shape[0]
    nch = GLA_TS // C
    nblk = S // GLA_TS

    def body(do_ref, q_ref, k_ref, v_ref, z_ref, st_ref, dq_ref, dk_ref, dv_ref, dlc_ref, dstate_ref):
        @pl.when(pl.program_id(0) == 0)
        def _():
            dstate_ref[...] = jnp.zeros_like(dstate_ref)

        left, right = _half_masks(C)
        bd = _pair_block_diag()
        rowi = lax.broadcasted_iota(jnp.int32, (C, LANES), 0)
        row = lax.broadcasted_iota(jnp.int32, (C, C), 0)
        colm = lax.broadcasted_iota(jnp.int32, (C, C), 1)
        for ch in range(nch - 1, -1, -1):
            rows = pl.ds(ch * C, C)
            for pp in range(N_PAIRS):
                ln = pl.ds(pp * LANES, LANES)
                q, k, v, dov = q_ref[rows, ln], k_ref[rows, ln], v_ref[rows, ln], do_ref[rows, ln]
                t = _gla_chunk_terms(q, k, z_ref[rows, ln])
                st = st_ref[ch, pp]
                dst_next = dstate_ref[pp]
                g = bd * dst_next
                dqs = _bdot(dov, st)
                dv = _bdot(t["kl"], g, NT_DIMS)
                dkl = _bdot(v, g)
                dqe = jnp.zeros((C, LANES), F32)
                dke = jnp.zeros((C, LANES), F32)
                for m in (left, right):
                    a = jnp.where(t["causal"], _bdot(t["qe"] * m, t["ke"], NT_DIMS), 0.0)
                    da = jnp.where(t["causal"], _bdot(dov * m, v, NT_DIMS), 0.0)
                    dv = dv + m * _bdot(a, dov, TN_DIMS)
                    dqe = dqe + m * _bdot(da, t["ke"])
                    dke = dke + m * _bdot(da, t["qe"], TN_DIMS)
                dstate_ref[pp] = bd * (dst_next * t["dec"] + _bdot(dov, t["qs"], TN_DIMS))
                db_last = jnp.sum(dst_next * st, axis=0, keepdims=True) * t["dec"] + jnp.sum(dkl * t["kl"], axis=0, keepdims=True)
                db = dqs * t["qs"] + dqe * t["qe"] - dke * t["ke"] - dkl * t["kl"]
                db = db + jnp.where(rowi == C - 1, db_last, 0.0)
                dq_ref[rows, ln] = (dqs * t["e_b"] + dqe * t["e_q"]) * t["scale"]
                dk_ref[rows, ln] = dke * t["e_k"] + dkl * t["e_l"]
                dv_ref[rows, ln] = dv
                dlc_ref[rows, ln] = lax.dot_general((colm >= row).astype(F32), db, (((1,), (0,)), ((), ())), precision=HI,
                                                    preferred_element_type=F32)

    def col(cb):
        return pl.BlockSpec((GLA_TS, MIX_HALF), lambda i: (nblk - 1 - i, cb))

    full = jax.ShapeDtypeStruct((S, MIX_HALF), F32)
    return pl.pallas_call(
        body, name=name, grid=(nblk,),
        in_specs=[col(0), col(0), col(1), col(2), col(0), pl.BlockSpec((nch, N_PAIRS, LANES, LANES), lambda i: (nblk - 1 - i, 0, 0, 0))],
        out_specs=[col(0)] * 4, out_shape=[full, full, full, full],
        scratch_shapes=[pltpu.VMEM((N_PAIRS, LANES, LANES), F32)], compiler_params=_cparams("arbitrary"))(do, proj, proj, proj, z, states)


def _gla_block_fwd(proj, w_lr_pad, b_lr, gain, bavg, pfx):
    z = _mm(proj, w_lr_pad, "nn", pfx + "_z", a_cols=(7 * MIX_HALF, MIX_HALF), epi=lambda acc, b: acc + b, extras=[(b_lr, "n")])
    o, states = _gla_fwd(proj, z, pfx + "_core")

    def out(ov, gg, ba, gn):
        n, _ = _head_rms(ov, ba)
        return n * gn * (gg * _sigmoid(gg))

    og, = _ew(out, pfx + "_out", [o, (proj, MIX_HALF, 3)], consts=[bavg, gain], outs=[(MIX_HALF, F32)])
    return og, dict(z=z, o=o, states=states)


def _gla_block_bwd(dog, proj, w_lr_pad, gain, bavg, res, pfx):
    z, o, states = res["z"], res["o"], res["states"]

    def out_bwd(dy, ov, gg, ba, gn):
        n, r = _head_rms(ov, ba)
        sg = _sigmoid(gg)
        silu = gg * sg
        dn = dy * gn * silu
        do = _head_rms_bwd(dn, n, r, ba)
        dgg = dy * n * gn * (sg * (1.0 + gg * (1.0 - sg)))
        return do, dgg, jnp.sum(dy * n * silu, axis=0, keepdims=True)

    do, dgg, dgain = _ew(out_bwd, pfx + "_out_bwd", [dog, o, (proj, MIX_HALF, 3)], consts=[bavg, gain],
                         outs=[(MIX_HALF, F32), (MIX_HALF, F32)], sums=[MIX_HALF])
    dq, dk, dv, dlc = _gla_bwd(do, proj, z, states, pfx + "_core_bwd")

    def decay_bwd(dl, zv):
        dz = dl * (1.0 / GLA_TAU) * (1.0 - _sigmoid(zv))
        return dz, jnp.sum(dz, axis=0, keepdims=True)

    dz, db_lr = _ew(decay_bwd, pfx + "_decay_bwd", [dlc, z], outs=[(MIX_HALF, F32)], sums=[MIX_HALF])
    dw_lr_pad = _mm(proj, dz, "tn", pfx + "_dwlr", a_cols=(7 * MIX_HALF, MIX_HALF))
    dsmall = _mm(dz, w_lr_pad, "nt", pfx + "_dsmall")
    return (dq, dk, dv, dgg, dsmall), dict(w_lr=dw_lr_pad[:GLA_RANK], b_lr=db_lr.reshape(-1), gain=dgain.reshape(-1, HEAD_DIM))


FOX_T = 512
FOX_HEADS = MIX_HALF // HEAD_DIM
NEG = -1e30
CUM_T = 512


def _cum_lanes(x, name, reverse, pre=None):
    R, S = x.shape
    nb = S // CUM_T

    def body(x_ref, o_ref, carry_ref):
        @pl.when(pl.program_id(0) == 0)
        def _():
            carry_ref[...] = jnp.zeros_like(carry_ref)

        xv = x_ref[...]
        if pre is not None:
            xv = pre(xv)
        i = lax.broadcasted_iota(jnp.int32, (CUM_T, CUM_T), 0)
        j = lax.broadcasted_iota(jnp.int32, (CUM_T, CUM_T), 1)
        tri = ((i >= j) if reverse else (i <= j)).astype(F32)
        c = lax.dot_general(xv, tri, (((1,), (0,)), ((), ())), precision=HI, preferred_element_type=F32)
        carry = carry_ref[...]
        o_ref[...] = c + carry[:, 0:1]
        carry_ref[...] = carry + jnp.sum(xv, axis=1, keepdims=True)

    spec = pl.BlockSpec((R, CUM_T), (lambda i: (0, nb - 1 - i)) if reverse else (lambda i: (0, i)))
    return pl.pallas_call(body, name=name, grid=(nb,), in_specs=[spec], out_specs=spec, out_shape=jax.ShapeDtypeStruct((R, S), F32),
                          scratch_shapes=[pltpu.VMEM((R, LANES), F32)], compiler_params=_cparams("arbitrary"))(x)


def _fox_scores(q, k, cqb, ck_ref, h, m, diag):
    cq = cqb[:, h * HEAD_DIM:h * HEAD_DIM + 1]
    ck = ck_ref[0, h:h + 1, :]
    s = lax.dot_general(q * m.astype(q.dtype), k, NT_DIMS, preferred_element_type=F32) + (cq - ck)
    if not diag:
        return s
    row = lax.broadcasted_iota(jnp.int32, (FOX_T, FOX_T), 0)
    col = lax.broadcasted_iota(jnp.int32, (FOX_T, FOX_T), 1)
    return jnp.where(row < col, NEG, s)


def _on_causal_blocks(q_blk, k_blk, step):
    @pl.when(k_blk < q_blk)
    def _():
        step(False)

    @pl.when(k_blk == q_blk)
    def _():
        step(True)


def _causal_pairs(n, key_major):
    if key_major:
        pairs = [(q, k) for k in range(n) for q in range(k, n)]
    else:
        pairs = [(q, k) for q in range(n) for k in range(q + 1)]
    return jnp.asarray([p[0] for p in pairs], jnp.int32), jnp.asarray([p[1] for p in pairs], jnp.int32)


def _fox_fwd(qn, kn, proj, cum_b, cum_tp, name):
    S = qn.shape[0]
    nq = S // FOX_T
    qidx, kidx = _causal_pairs(nq, False)

    def body(qidx_ref, kidx_ref, q_ref, k_ref, v_ref, cq_ref, ck_ref, o_ref, lse_ref, m_scr, acc_scr):
        t = pl.program_id(1)
        qi, ki = qidx_ref[t], kidx_ref[t]

        @pl.when(ki == 0)
        def _():
            m_scr[...] = jnp.full_like(m_scr, NEG)
            acc_scr[...] = jnp.zeros_like(acc_scr)

        left, right = _half_masks(FOX_T)

        def step(diag):
            q, k, v = q_ref[...], k_ref[...], v_ref[...].astype(BF16)
            cqb = cq_ref[...]
            for h, m in enumerate((left, right)):
                s = _fox_scores(q, k, cqb, ck_ref, h, m, diag)
                m_prev = m_scr[h]
                m_new = jnp.maximum(m_prev, jnp.max(s, axis=1, keepdims=True))
                p = jnp.exp(s - m_new)
                v_h = jnp.where(m > 0, v, jnp.ones_like(v))
                acc_scr[h] = jnp.exp(m_prev - m_new) * acc_scr[h] + jnp.dot(p.astype(BF16), v_h, preferred_element_type=F32)
                m_scr[h] = m_new

        _on_causal_blocks(qi, ki, step)

        @pl.when(ki == qi)
        def _():
            a0, a1 = acc_scr[0], acc_scr[1]
            is_left = left > 0
            num = jnp.where(is_left, a0, a1)
            den = jnp.where(is_left, pltpu.roll(a0, HEAD_DIM, 1), pltpu.roll(a1, HEAD_DIM, 1))
            o_ref[...] = num / den
            lse_ref[...] = jnp.where(is_left, m_scr[0], m_scr[1]) + jnp.log(den)

    qspec = pl.BlockSpec((FOX_T, LANES), lambda p, t, qx, kx: (qx[t], p))
    kspec = pl.BlockSpec((FOX_T, LANES), lambda p, t, qx, kx: (kx[t], p))
    vspec = pl.BlockSpec((FOX_T, LANES), lambda p, t, qx, kx: (kx[t], 6 * N_PAIRS + p))
    ckspec = pl.BlockSpec((1, 8, FOX_T), lambda p, t, qx, kx: (p, 0, kx[t]))
    full = jax.ShapeDtypeStruct((S, MIX_HALF), F32)
    grid_spec = pltpu.PrefetchScalarGridSpec(
        num_scalar_prefetch=2, grid=(N_PAIRS, int(qidx.shape[0])), in_specs=[qspec, kspec, vspec, qspec, ckspec], out_specs=[qspec, qspec],
        scratch_shapes=[pltpu.VMEM((2, FOX_T, 1), F32), pltpu.VMEM((2, FOX_T, LANES), F32)])
    return pl.pallas_call(body, name=name, grid_spec=grid_spec, out_shape=[full, full],
                          compiler_params=_cparams("parallel", "arbitrary"))(qidx, kidx, qn, kn, proj, cum_b, cum_tp)


def _fox_bwd(do, qn, kn, proj, cum_b, cum_tp, lse_b, delta_b, name):
    S = qn.shape[0]
    nq = S // FOX_T
    scale = HEAD_DIM ** -0.5
    qidx, kidx = _causal_pairs(nq, True)
    ntri = int(qidx.shape[0])

    def body(qidx_ref, kidx_ref, do_ref, q_ref, k_ref, v_ref, cq_ref, ck_ref, lse_ref, dl_ref,
             dq_ref, dcq_ref, dk_ref, dv_ref, dck_ref, dq_scr, dk_scr, dv_scr):
        t = pl.program_id(1)
        qi, ki = qidx_ref[t], kidx_ref[t]

        @pl.when(t == 0)
        def _():
            dq_scr[...] = jnp.zeros_like(dq_scr)

        @pl.when(qi == ki)
        def _():
            dk_scr[...] = jnp.zeros_like(dk_scr)
            dv_scr[...] = jnp.zeros_like(dv_scr)

        left, right = _half_masks(FOX_T)
        rows = pl.ds(pl.multiple_of(qi * FOX_T, FOX_T), FOX_T)

        def step(diag):
            q, k, v, dov = q_ref[...], k_ref[...], v_ref[...].astype(BF16), do_ref[...]
            cqb, lseb, dlb = cq_ref[...], lse_ref[...], dl_ref[...]
            dob = dov.astype(BF16)
            dv = dv_scr[...]
            for h, m in enumerate((left, right)):
                s = _fox_scores(q, k, cqb, ck_ref, h, m, diag)
                p = jnp.exp(s - lseb[:, h * HEAD_DIM:h * HEAD_DIM + 1])
                dp = lax.dot_general((dov * m).astype(BF16), v, NT_DIMS, preferred_element_type=F32)
                ds = (p * (dp - dlb[:, h * HEAD_DIM:h * HEAD_DIM + 1])).astype(BF16)
                dv = dv + m * lax.dot_general(p.astype(BF16), dob, TN_DIMS, preferred_element_type=F32)
                q_h = jnp.where(m > 0, q, jnp.ones_like(q))
                k_h = jnp.where(m > 0, k, jnp.ones_like(k))
                dk_scr[h] = dk_scr[h] + lax.dot_general(ds, q_h, TN_DIMS, preferred_element_type=F32)
                dq_scr[h, rows, :] = dq_scr[h, rows, :] + jnp.dot(ds, k_h, preferred_element_type=F32)
            dv_scr[...] = dv

        _on_causal_blocks(qi, ki, step)

        @pl.when(qi == nq - 1)
        def _():
            a0, a1 = dk_scr[0], dk_scr[1]
            dk_ref[...] = left * a0 + right * a1
            dv_ref[...] = dv_scr[...]
            dck_ref[...] = left * pltpu.roll(a0, HEAD_DIM, 1) + right * pltpu.roll(a1, HEAD_DIM, 1)

        @pl.when(t == ntri - 1)
        def _():
            for r in range(nq):
                blk = pl.ds(r * FOX_T, FOX_T)
                a0, a1 = dq_scr[0, blk, :], dq_scr[1, blk, :]
                dq_ref[blk, :] = (left * a0 + right * a1) * scale
                dcq_ref[blk, :] = left * pltpu.roll(a0, HEAD_DIM, 1) + right * pltpu.roll(a1, HEAD_DIM, 1)

    qspec = pl.BlockSpec((FOX_T, LANES), lambda p, t, qx, kx: (qx[t], p))
    kspec = pl.BlockSpec((FOX_T, LANES), lambda p, t, qx, kx: (kx[t], p))
    vspec = pl.BlockSpec((FOX_T, LANES), lambda p, t, qx, kx: (kx[t], 6 * N_PAIRS + p))
    ckspec = pl.BlockSpec((1, 8, FOX_T), lambda p, t, qx, kx: (p, 0, kx[t]))
    seq = pl.BlockSpec((S, LANES), lambda p, t, qx, kx: (0, p))
    full = jax.ShapeDtypeStruct((S, MIX_HALF), F32)
    grid_spec = pltpu.PrefetchScalarGridSpec(
        num_scalar_prefetch=2, grid=(N_PAIRS, ntri), in_specs=[qspec, qspec, kspec, vspec, qspec, ckspec, qspec, qspec],
        out_specs=[seq, seq, kspec, kspec, kspec],
        scratch_shapes=[pltpu.VMEM((2, S, LANES), F32), pltpu.VMEM((2, FOX_T, LANES), F32), pltpu.VMEM((FOX_T, LANES), F32)])
    return pl.pallas_call(body, name=name, grid_spec=grid_spec, out_shape=[full] * 5,
                          compiler_params=_cparams("parallel", "arbitrary"))(qidx, kidx, do, qn, kn, proj, cum_b, cum_tp, lse_b, delta_b)


def _ff_bwd(rc, f_t, name):
    def body(rc_ref, f_ref, d_ref, s_ref):
        d = rc_ref[...] * (1.0 - _sigmoid(f_ref[...]))
        d_ref[...] = d
        s_ref[...] = jnp.sum(d, axis=1, keepdims=True)

    return pl.pallas_call(body, name=name, out_shape=[jax.ShapeDtypeStruct(rc.shape, F32), jax.ShapeDtypeStruct((rc.shape[0], 1), F32)])(rc, f_t)


def _fox_block_fwd(proj, b_f, q_gain, k_gain, bavg, pfx):
    S = proj.shape[0]

    def prep(qv, kv, ba, qg, kg):
        return _head_rms(qv, ba)[0] * qg * (HEAD_DIM ** -0.5), _head_rms(kv, ba)[0] * kg

    qn, kn = _ew(prep, pfx + "_prep", [(proj, MIX_HALF, 4), (proj, MIX_HALF, 5)], consts=[bavg, q_gain, k_gain],
                 outs=[(MIX_HALF, BF16), (MIX_HALF, BF16)])
    f0 = 7 * MIX_HALF + GLA_RANK
    f_t = proj[:, f0:f0 + FOX_HEADS].T + b_f.reshape(FOX_HEADS, 1)
    cum = _cum_lanes(f_t, pfx + "_cum", False, pre=_log_sigmoid)
    cum_b = jnp.repeat(cum.T, HEAD_DIM, axis=1)
    cum_tp = jnp.pad(cum.reshape(N_PAIRS, 2, S), ((0, 0), (0, 6), (0, 0)))
    o, lse_b = _fox_fwd(qn, kn, proj, cum_b, cum_tp, pfx + "_attn")
    return o, dict(qn=qn, kn=kn, f_t=f_t, cum_b=cum_b, cum_tp=cum_tp, o=o, lse_b=lse_b)


def _fox_block_bwd(do, proj, q_gain, k_gain, bavg, res, pfx):
    qn, kn, o = res["qn"], res["kn"], res["o"]
    S = proj.shape[0]
    delta_b, = _ew(lambda a, b, ba: _head_mean(a * b, ba) * float(HEAD_DIM), pfx + "_delta", [do, o], consts=[bavg], outs=[(MIX_HALF, F32)])
    args = (do, qn, kn, proj, res["cum_b"], res["cum_tp"], res["lse_b"], delta_b)
    dqn, dcq_b, dkn, dv, dck_b = _fox_bwd(*args, pfx + "_bwd")

    def prep_bwd(dq, dk, qv, kv, ba, qg, kg):
        nq, rq = _head_rms(qv, ba)
        nk, rk = _head_rms(kv, ba)
        return (_head_rms_bwd(dq * qg, nq, rq, ba), _head_rms_bwd(dk * kg, nk, rk, ba),
                jnp.sum(dq * nq, axis=0, keepdims=True), jnp.sum(dk * nk, axis=0, keepdims=True))

    dfq, dfk, dqg, dkg = _ew(prep_bwd, pfx + "_prep_bwd", [dqn, dkn, (proj, MIX_HALF, 4), (proj, MIX_HALF, 5)],
                             consts=[bavg, q_gain, k_gain], outs=[(MIX_HALF, F32), (MIX_HALF, F32)], sums=[MIX_HALF, MIX_HALF])
    dcum = (dcq_b - dck_b)[:, ::HEAD_DIM].T
    rc = _cum_lanes(dcum, pfx + "_rcum", True)
    dff_t, db_f = _ff_bwd(rc, res["f_t"], pfx + "_ff_bwd")
    grads = dict(b_f=db_f.reshape(-1), q_gain=dqg.reshape(-1, HEAD_DIM), k_gain=dkg.reshape(-1, HEAD_DIM))
    return (dfq, dfk, dv, dff_t.T), grads


WEIGHTS = ['ada_w', 'ada_b', 'even_w_in', 'even_w_out', 'gla_w_lr', 'gla_b_lr', 'gla_gain', 'fox_b_f', 'fox_q_gain', 'fox_k_gain',
           'odd_w_in', 'odd_w_out', 's5_lam_re', 's5_lam_im', 's5_log_dt', 's5_b_re', 's5_b_im', 's5_c_re', 's5_c_im', 's5_d',
           's5_w_glu', 's5_b_glu', 'sgu_ln_gain', 'sgu_ln_bias', 'sgu_w_s', 'sgu_b_s', 'mlp_w1', 'mlp_w2']
ARGS = ['x', 'c'] + WEIGHTS + ['loss_target'] + ['m_' + w for w in WEIGHTS] + ['v_' + w for w in WEIGHTS]

EVEN_COLS = 3608
EVEN_PAD = 8 * MIX_HALF
MOD = 6 * D_MODEL
MOD_SHARD = MOD // N_CHIPS

SHARDED = [("even_w_in", (1, 1024, 902), 2), ("even_w_out", (1, 256, 1024), 1), ("odd_w_in", (1, 1024, 384), 2),
           ("odd_w_out", (1, 256, 1024), 1), ("mlp_w1", (2, 1024, 1024), 2), ("mlp_w2", (2, 1024, 1024), 1),
           ("gla_w_lr", (1, 16, 128), 2), ("s5_w_glu", (1, 128, 512), 1), ("s5_b_glu", (1, 128), 1),
           ("sgu_ln_gain", (1, 128), 1), ("sgu_ln_bias", (1, 128), 1)]
PACK_COLS = 1024
PACK_ROWS = 6144
REPLICATED = [("gla_b_lr", (1, 512)), ("gla_gain", (1, 8, 64)), ("fox_b_f", (1, 8)), ("fox_q_gain", (1, 8, 64)),
              ("fox_k_gain", (1, 8, 64)), ("s5_lam_re", (1, 32, 64)), ("s5_lam_im", (1, 32, 64)), ("s5_log_dt", (1, 32)),
              ("s5_b_re", (1, 32, 64, 16)), ("s5_b_im", (1, 32, 64, 16)), ("s5_c_re", (1, 32, 16, 64)), ("s5_c_im", (1, 32, 16, 64)),
              ("s5_d", (1, 32, 16)), ("sgu_w_s", (1, 8, 128, 128)), ("sgu_b_s", (1, 8, 128))]
SMALL_ROWS = 512
BIG_ADAM = {"ada_w": (2048, 1536), "even_w_in": (1024, 902), "even_w_out": (256, 1024), "odd_w_in": (1024, 384),
            "odd_w_out": (256, 1024), "mlp_w1": (2048, 1024), "mlp_w2": (2048, 1024), "s5_w_glu": (128, 512)}


PACK_ALIGN = 16


def _piece_rows(shape):
    rows = -(-math.prod(shape) // PACK_COLS)
    return -(-rows // PACK_ALIGN) * PACK_ALIGN


def _to_rows(p, lead=()):
    n = math.prod(p.shape[len(lead):])
    rows = _piece_rows(p.shape[len(lead):])
    flat = p.reshape(lead + (n,))
    if rows * PACK_COLS != n:
        flat = jnp.pad(flat, [(0, 0)] * len(lead) + [(0, rows * PACK_COLS - n)])
    return flat.reshape(lead + (rows, PACK_COLS))


def _from_rows(x, r0, shape, lead=()):
    n = math.prod(shape)
    seg = lax.slice_in_dim(x, r0, r0 + _piece_rows(shape), axis=len(lead)).reshape(lead + (-1,))
    return lax.slice_in_dim(seg, 0, n, axis=len(lead)).reshape(lead + tuple(shape))


def _pack_rows(pieces, rows):
    x = jnp.concatenate([_to_rows(p) for p in pieces], axis=0)
    return jnp.pad(x, ((0, rows - x.shape[0]), (0, 0)))


def _unpack(x, specs):
    out, r0 = {}, 0
    for name, shape in specs:
        out[name] = _from_rows(x, r0, shape)
        r0 += _piece_rows(shape)
    return out


def _shards_to_full(x4):
    out, r0 = {}, 0
    for name, shape, axis in SHARDED:
        seg = _from_rows(x4, r0, shape, lead=(N_CHIPS,))
        out[name] = jnp.concatenate([seg[k] for k in range(N_CHIPS)], axis=axis)
        r0 += _piece_rows(shape)
    return out


def _full_to_shards(full):
    blocks = [_to_rows(jnp.stack(jnp.split(full[name], N_CHIPS, axis=axis)), lead=(N_CHIPS,)) for name, _, axis in SHARDED]
    x = jnp.concatenate(blocks, axis=1)
    return jnp.pad(x, ((0, 0), (0, PACK_ROWS - x.shape[1]), (0, 0)))


def _relu2(t):
    r = jnp.maximum(t, 0.0)
    return r * r


def _silu(t):
    return t * _sigmoid(t)


def _pack_even(w):
    return jnp.concatenate([w[:, :2048], w[:, 2064:3600], w[:, 2048:2064], w[:, 3600:3608],
                            jnp.zeros((w.shape[0], EVEN_PAD - EVEN_COLS), w.dtype)], axis=1)


def _unpack_even(wp):
    return jnp.concatenate([wp[:, :2048], wp[:, 3584:3600], wp[:, 2048:3584], wp[:, 3600:3608]], axis=1)


def _mlp_fwd(h, w1, w2, pfx):
    pre = _mm(h, w1, "nn", pfx + "_up")
    return pre, _mm(pre, w2, "nn", pfx + "_down", a_pro=_relu2)


def _mlp_bwd(dm, h, pre, w1, w2, pfx):
    dpre = _mm(dm, w2, "nt", pfx + "_dpre", epi=lambda acc, p: acc * (2.0 * jnp.maximum(p, 0.0)), extras=[(pre, "mn")], out_dtype=BF16)
    dw2 = _mm(pre, dm, "tn", pfx + "_dw2", a_pro=_relu2)
    dw1 = _mm(h, dpre, "tn", pfx + "_dw1")
    dh = _mm(dpre, w1, "nt", pfx + "_dh")
    return dh, dw1, dw2


def _step(args):
    a = dict(zip(ARGS, args, strict=True))
    x0 = a["x"][0]
    target = a["loss_target"][0]
    mx, my, mc = lax.axis_index("x"), lax.axis_index("y"), lax.axis_index("c")
    chip = 2 * mx + my
    dev = 2 * chip + mc
    bavg = _head_avg_matrix(MIX_HALF)

    c_all = _gather8(jnp.pad(a["c"], ((0, 7), (0, 0))), "c_gather")[:, :, 0, :].reshape(2 * N_CHIPS, D_MODEL)
    ada_b_shard = lax.dynamic_slice_in_dim(a["ada_b"], chip * MOD_SHARD, MOD_SHARD, axis=1)
    mod_sh = [_mm(c_all, a["ada_w"][l], "nn", f"mod{l}", a_pro=_silu, epi=lambda acc, b: acc + b, extras=[(ada_b_shard[l:l + 1], "n")])
              for l in range(2)]
    small3 = jnp.zeros((8, MOD_SHARD), F32)
    for r, n in enumerate(("s5_b_glu", "sgu_ln_gain", "sgu_ln_bias")):
        small3 = small3.at[r, :LANES].set(a[n][0])
    mod_all = _chip_exchange(jnp.concatenate(mod_sh + [small3]), "mod_gather", True)
    mods = []
    for l in range(2):
        full = mod_all[:, 8 * l:8 * l + 8].transpose(1, 0, 2).reshape(8, MOD)
        mods.append(jnp.split(lax.dynamic_slice_in_dim(full, dev, 1, axis=0), 6, axis=1))
    b_glu, ln_gain, ln_bias = (mod_all[:, 16 + r, :LANES].reshape(1, MIX_HALF) for r in range(3))

    shard = _pack_rows([a[n] for n, _, _ in SHARDED], PACK_ROWS).astype(BF16)
    half = lax.dynamic_slice_in_dim(shard, mc * (PACK_ROWS // 2), PACK_ROWS // 2, axis=0)
    collected = _chip_exchange(half, "w_chips", True)
    halves = _by_core(collected, _pair_swap(collected, "w_pair"))
    w = _shards_to_full(halves.transpose(1, 0, 2, 3).reshape(N_CHIPS, PACK_ROWS, PACK_COLS))
    w_even = _pack_even(w["even_w_in"][0])
    w_lr_pad = jnp.zeros((MIX_HALF, MIX_HALF), BF16).at[:GLA_RANK].set(w["gla_w_lr"][0])
    gla_b_lr = a["gla_b_lr"]
    gla_gain, q_gain, k_gain = (a[n].reshape(1, MIX_HALF) for n in ("gla_gain", "fox_q_gain", "fox_k_gain"))
    s5w = dict(lam_re=a["s5_lam_re"][0], lam_im=a["s5_lam_im"][0], log_dt=a["s5_log_dt"][0], b_re=a["s5_b_re"][0], b_im=a["s5_b_im"][0],
               c_re=a["s5_c_re"][0], c_im=a["s5_c_im"][0], d=a["s5_d"][0], w_glu=w["s5_w_glu"][0], b_glu=b_glu)
    sgu_wm, sgu_bt = _sgu_tables(a["sgu_w_s"][0], a["sgu_b_s"][0])

    sh1, sc1, g1, sh2, sc2, g2 = mods[0]
    _, h1_0 = _res_rms(x0, sc1, sh1, "l0_norm1")
    proj0 = _mm(h1_0, w_even, "nn", "l0_proj")
    og, gla_res = _gla_block_fwd(proj0, w_lr_pad, gla_b_lr, gla_gain, bavg, "gla")
    of, fox_res = _fox_block_fwd(proj0, a["fox_b_f"][0], q_gain, k_gain, bavg, "fox")
    mixed0 = jnp.concatenate([og, of], axis=1).astype(BF16)
    y0 = _mm(mixed0, w["even_w_out"][0], "nn", "l0_out")
    x1, h2_0 = _res_rms(x0, sc2, sh2, "l0_norm2", y=y0, g=g1)
    pre0, m0 = _mlp_fwd(h2_0, w["mlp_w1"][0], w["mlp_w2"][0], "l0_mlp")
    sh1b, sc1b, g1b, sh2b, sc2b, g2b = mods[1]
    x2, h1_1 = _res_rms(x1, sc1b, sh1b, "l1_norm1", y=m0, g=g2)
    proj1 = _mm(h1_1, w["odd_w_in"][0], "nn", "l1_proj")
    ys5, s5_res = _s5_block_fwd(proj1[:, :MIX_HALF], s5w, "s5")
    ysgu = _sgu_fwd(proj1, ln_gain, ln_bias, sgu_wm, sgu_bt, "sgu")
    mixed1 = jnp.concatenate([ys5, ysgu], axis=1).astype(BF16)
    y1 = _mm(mixed1, w["odd_w_out"][0], "nn", "l1_out")
    x3, h2_1 = _res_rms(x2, sc2b, sh2b, "l1_norm2", y=y1, g=g1b)
    pre1, m1 = _mlp_fwd(h2_1, w["mlp_w1"][1], w["mlp_w2"][1], "l1_mlp")
    loss_b, dx4, dm1, dg2b = _res_loss(x3, m1, g2b, target, "loss")
    loss = lax.psum(loss_b[0, 0], ("x", "y", "c"))

    full = {}
    dh2_1, dw1_1, dw2_1 = _mlp_bwd(dm1, h2_1, pre1, w["mlp_w1"][1], w["mlp_w2"][1], "l1_mlp")
    dx3, dy1, dg1b, dsc2b, dsh2b = _res_rms_bwd(x3, dh2_1, sc2b, dx4, "l1_norm2_bwd", y=y1, g=g1b)
    dmixed1 = _mm(dy1, w["odd_w_out"][0], "nt", "l1_out_dx")
    full["odd_w_out"] = _mm(mixed1, dy1, "tn", "l1_out_dw")[None]
    du, s5g = _s5_block_bwd(dmixed1[:, :MIX_HALF], s5w, s5_res, "s5")
    dzu, dzv, dws, dbt, dlg, dlb = _sgu_bwd(dmixed1[:, MIX_HALF:], proj1, ln_gain, ln_bias, sgu_wm, sgu_bt, "sgu_bwd")
    g_ws, g_bs = _sgu_grads(dws, dbt)
    dproj1 = jnp.concatenate([du, dzu, dzv], axis=1).astype(BF16)
    full["odd_w_in"] = _mm(h1_1, dproj1, "tn", "l1_proj_dw")[None]
    dh1_1 = _mm(dproj1, w["odd_w_in"][0], "nt", "l1_proj_dx")
    dx2, dm0, dg2, dsc1b, dsh1b = _res_rms_bwd(x2, dh1_1, sc1b, dx3, "l1_norm1_bwd", y=m0, g=g2)
    dh2_0, dw1_0, dw2_0 = _mlp_bwd(dm0, h2_0, pre0, w["mlp_w1"][0], w["mlp_w2"][0], "l0_mlp")
    full["mlp_w1"] = jnp.stack([dw1_0, dw1_1])
    full["mlp_w2"] = jnp.stack([dw2_0, dw2_1])
    dx1, dy0, dg1, dsc2, dsh2 = _res_rms_bwd(x1, dh2_0, sc2, dx2, "l0_norm2_bwd", y=y0, g=g1)
    dmixed0 = _mm(dy0, w["even_w_out"][0], "nt", "l0_out_dx")
    full["even_w_out"] = _mm(mixed0, dy0, "tn", "l0_out_dw")[None]
    (dgq, dgk, dgv, dgg, dsmall), glag = _gla_block_bwd(dmixed0[:, :MIX_HALF], proj0, w_lr_pad, gla_gain, bavg, gla_res, "gla")
    (dfq, dfk, dfv, dff), foxg = _fox_block_bwd(dmixed0[:, MIX_HALF:], proj0, q_gain, k_gain, bavg, fox_res, "fox")
    dsmall = lax.dynamic_update_slice(dsmall, dff, (0, GLA_RANK))
    dproj0 = jnp.concatenate([dgq, dgk, dgv, dgg, dfq, dfk, dfv, dsmall], axis=1).astype(BF16)
    full["even_w_in"] = _unpack_even(_mm(h1_0, dproj0, "tn", "l0_proj_dw"))[None]
    dh1_0 = _mm(dproj0, w_even, "nt", "l0_proj_dx")
    grad_x, dsc1, dsh1 = _res_rms_bwd(x0, dh1_0, sc1, dx1, "l0_norm1_bwd")
    full["gla_w_lr"] = glag["w_lr"][None]
    full["s5_w_glu"] = s5g["w_glu"][None]
    full["s5_b_glu"] = s5g["b_glu"][None]
    full["sgu_ln_gain"] = dlg
    full["sgu_ln_bias"] = dlb

    dmod = jnp.concatenate([dsh1, dsc1, dg1, dsh2, dsc2, dg2, dsh1b, dsc1b, dg1b, dsh2b, dsc2b, dg2b], axis=1)
    dmod_all = _gather8(jnp.pad(dmod, ((0, 7), (0, 0))), "dmod_gather")[:, :, 0, :].reshape(2 * N_CHIPS, 2, MOD)
    grads = {}
    grads["ada_w"] = jnp.stack([
        _mm(c_all, lax.dynamic_slice_in_dim(dmod_all[:, l], chip * MOD_SHARD, MOD_SHARD, axis=1), "tn", f"ada_dw{l}", a_pro=_silu)
        for l in range(2)])
    grads["ada_b"] = _sum_slots(dmod_all.reshape(2 * N_CHIPS, 2 * MOD // MIX_HALF, MIX_HALF), "ada_db").reshape(2, MOD)

    packed = _full_to_shards(full)
    hr = PACK_ROWS // 2
    mine = lax.dynamic_slice_in_dim(packed, mc * hr, hr, axis=1)
    other = lax.dynamic_slice_in_dim(packed, (1 - mc) * hr, hr, axis=1)
    theirs = _pair_swap(other, "g_pair")
    pair_sum, = _ew(lambda p, q: p + q, "g_pair_sum", [mine.reshape(N_CHIPS * hr, PACK_COLS), theirs.reshape(N_CHIPS * hr, PACK_COLS)],
                    outs=[(PACK_COLS, BF16)])
    arrived = _chip_exchange(pair_sum.reshape(N_CHIPS, hr, PACK_COLS), "g_chips", False)
    red_half = _sum_slots(arrived, "g_chip_sum")
    reduced = _by_core(red_half, _pair_swap(red_half, "g_pair_out")).reshape(PACK_ROWS, PACK_COLS)
    grads.update(_unpack(reduced, [(n, s) for n, s, _ in SHARDED]))

    part = dict(gla_b_lr=glag["b_lr"], gla_gain=glag["gain"], fox_b_f=foxg["b_f"], fox_q_gain=foxg["q_gain"], fox_k_gain=foxg["k_gain"],
                s5_lam_re=s5g["lam_re"], s5_lam_im=s5g["lam_im"], s5_log_dt=s5g["log_dt"], s5_b_re=s5g["b_re"], s5_b_im=s5g["b_im"],
                s5_c_re=s5g["c_re"], s5_c_im=s5g["c_im"], s5_d=s5g["d"], sgu_w_s=g_ws, sgu_b_s=g_bs)
    parts_all = _gather8(_pack_rows([part[n] for n, _ in REPLICATED], SMALL_ROWS).astype(BF16), "rep_gather")
    rep = _sum_slots(parts_all.reshape(2 * N_CHIPS, SMALL_ROWS, PACK_COLS), "rep_sum")
    grads.update(_unpack(rep, REPLICATED))

    delta, new_m, new_v = {}, {}, {}
    for n, shape2 in BIG_ADAM.items():
        d, nm, nv = _adamw(a[n].reshape(shape2), grads[n].reshape(shape2), a["m_" + n].reshape(shape2), a["v_" + n].reshape(shape2), "adamw_" + n)
        delta[n], new_m[n], new_v[n] = (t.reshape(a[n].shape) for t in (d, nm, nv))
    small = [n for n in WEIGHTS if n not in BIG_ADAM]
    spec = [(n, a[n].shape) for n in small]
    packs = [_pack_rows([src[n] for n in small], SMALL_ROWS) for src in
             (a, grads, {n: a["m_" + n] for n in small}, {n: a["v_" + n] for n in small})]
    for tgt, res in zip((delta, new_m, new_v), _adamw(*packs, "adamw_small")):
        tgt.update(_unpack(res, spec))
    outs = [loss, grad_x[None]]
    for group in (grads, delta, new_m, new_v):
        outs += [group[n].reshape(a[n].shape) for n in WEIGHTS]
    return tuple(outs)


def kernel(x, c, ada_w, ada_b, even_w_in, even_w_out, gla_w_lr, gla_b_lr, gla_gain, fox_b_f, fox_q_gain, fox_k_gain, odd_w_in,
           odd_w_out, s5_lam_re, s5_lam_im, s5_log_dt, s5_b_re, s5_b_im, s5_c_re, s5_c_im, s5_d, s5_w_glu, s5_b_glu, sgu_ln_gain,
           sgu_ln_bias, sgu_w_s, sgu_b_s, mlp_w1, mlp_w2, loss_target, m_ada_w, m_ada_b, m_even_w_in, m_even_w_out, m_gla_w_lr,
           m_gla_b_lr, m_gla_gain, m_fox_b_f, m_fox_q_gain, m_fox_k_gain, m_odd_w_in, m_odd_w_out, m_s5_lam_re, m_s5_lam_im,
           m_s5_log_dt, m_s5_b_re, m_s5_b_im, m_s5_c_re, m_s5_c_im, m_s5_d, m_s5_w_glu, m_s5_b_glu, m_sgu_ln_gain, m_sgu_ln_bias,
           m_sgu_w_s, m_sgu_b_s, m_mlp_w1, m_mlp_w2, v_ada_w, v_ada_b, v_even_w_in, v_even_w_out, v_gla_w_lr, v_gla_b_lr,
           v_gla_gain, v_fox_b_f, v_fox_q_gain, v_fox_k_gain, v_odd_w_in, v_odd_w_out, v_s5_lam_re, v_s5_lam_im, v_s5_log_dt,
           v_s5_b_re, v_s5_b_im, v_s5_c_re, v_s5_c_im, v_s5_d, v_s5_w_glu, v_s5_b_glu, v_sgu_ln_gain, v_sgu_ln_bias, v_sgu_w_s,
           v_sgu_b_s, v_mlp_w1, v_mlp_w2):
    return _step((x, c, ada_w, ada_b, even_w_in, even_w_out, gla_w_lr, gla_b_lr, gla_gain, fox_b_f, fox_q_gain, fox_k_gain,
                  odd_w_in, odd_w_out, s5_lam_re, s5_lam_im, s5_log_dt, s5_b_re, s5_b_im, s5_c_re, s5_c_im, s5_d, s5_w_glu,
                  s5_b_glu, sgu_ln_gain, sgu_ln_bias, sgu_w_s, sgu_b_s, mlp_w1, mlp_w2, loss_target, m_ada_w, m_ada_b,
                  m_even_w_in, m_even_w_out, m_gla_w_lr, m_gla_b_lr, m_gla_gain, m_fox_b_f, m_fox_q_gain, m_fox_k_gain,
                  m_odd_w_in, m_odd_w_out, m_s5_lam_re, m_s5_lam_im, m_s5_log_dt, m_s5_b_re, m_s5_b_im, m_s5_c_re, m_s5_c_im,
                  m_s5_d, m_s5_w_glu, m_s5_b_glu, m_sgu_ln_gain, m_sgu_ln_bias, m_sgu_w_s, m_sgu_b_s, m_mlp_w1, m_mlp_w2, v_ada_w,
                  v_ada_b, v_even_w_in, v_even_w_out, v_gla_w_lr, v_gla_b_lr, v_gla_gain, v_fox_b_f, v_fox_q_gain, v_fox_k_gain,
                  v_odd_w_in, v_odd_w_out, v_s5_lam_re, v_s5_lam_im, v_s5_log_dt, v_s5_b_re, v_s5_b_im, v_s5_c_re, v_s5_c_im,
                  v_s5_d, v_s5_w_glu, v_s5_b_glu, v_sgu_ln_gain, v_sgu_ln_bias, v_sgu_w_s, v_sgu_b_s, v_mlp_w1, v_mlp_w2))
```

```python
import functools
import math

import jax
import jax.numpy as jnp
import numpy as np
from jax import lax
from jax.experimental import pallas as pl
from jax.experimental.pallas import tpu as pltpu

F32 = jnp.float32
BF16 = jnp.bfloat16
MESH = pl.DeviceIdType.MESH
ANY = pl.BlockSpec(memory_space=pl.ANY)
DMA_SEM = pltpu.SemaphoreType.DMA

D_MODEL = 1024
HEAD_DIM = 64
MIX_HALF = 512
GLA_RANK = 16
GLA_TAU = 16.0
GLA_CHUNK = 64
S5_GROUPS = 32
S5_GROUP_WIDTH = 16
S5_STATE = 64
S5_N = S5_GROUPS * S5_STATE
SGU_GROUPS = 8
SGU_CHUNK = 128
D_FF = 4096
EPS = 1e-6
N_CHIPS = 4
LANES = 128
VMEM_LIMIT = 48 * 1024 * 1024
PAIR_COPIES = 16

ADAM_LR = 0.001
ADAM_B1 = 0.9
ADAM_B2 = 0.999
ADAM_EPS = 1e-08
ADAM_WD = 0.01
ADAM_STEP = 10


def _cparams(*sem):
    return pltpu.CompilerParams(dimension_semantics=sem, vmem_limit_bytes=VMEM_LIMIT)


def _pair_swap(x, name):
    lead = x.shape[:-2]
    rows = x.shape[-2]
    nsplit = max(1, PAIR_COPIES // max(1, math.prod(lead)))
    while nsplit > 1 and rows % (nsplit * 16):
        nsplit -= 1
    pieces = [idx + (pl.ds(j * (rows // nsplit), rows // nsplit),) for idx in np.ndindex(*lead) for j in range(nsplit)]

    def body(x_ref, o_ref, send_sems, recv_sems):
        mx, my, mc = lax.axis_index("x"), lax.axis_index("y"), lax.axis_index("c")
        copies = [pltpu.make_async_remote_copy(src_ref=x_ref.at[p], dst_ref=o_ref.at[p], send_sem=send_sems.at[j], recv_sem=recv_sems.at[j],
                                               device_id=(mx, my, 1 - mc), device_id_type=MESH) for j, p in enumerate(pieces)]
        for cp in copies:
            cp.start()
        for cp in copies:
            cp.wait_recv()
        for cp in copies:
            cp.wait_send()

    return pl.pallas_call(
        body, name=name, out_shape=jax.ShapeDtypeStruct(x.shape, x.dtype), in_specs=[ANY], out_specs=ANY,
        scratch_shapes=[DMA_SEM((len(pieces),)), DMA_SEM((len(pieces),))])(x)


def _by_core(mine, theirs):
    first = lax.axis_index("c") == 0
    return jnp.stack([jnp.where(first, mine, theirs), jnp.where(first, theirs, mine)])


def _chip_exchange(x, name, bcast):
    blk = x.shape if bcast else x.shape[1:]

    def body(x_ref, o_ref, send_sems, recv_sems, loc_sem):
        mx, my, mc = lax.axis_index("x"), lax.axis_index("y"), lax.axis_index("c")
        me = 2 * mx + my
        peers = [(1 - mx, my), (mx, 1 - my), (1 - mx, 1 - my)]

        def src(k):
            return x_ref if bcast else x_ref.at[k]

        loc = pltpu.make_async_copy(src(me), o_ref.at[me], loc_sem)
        loc.start()
        sends = []
        for j, (px, py) in enumerate(peers):
            cp = pltpu.make_async_remote_copy(src_ref=src(2 * px + py), dst_ref=o_ref.at[me], send_sem=send_sems.at[j],
                                              recv_sem=recv_sems.at[j], device_id=(px, py, mc), device_id_type=MESH)
            cp.start()
            sends.append(cp)
        for j, (px, py) in enumerate(peers):
            pltpu.make_async_remote_copy(src_ref=src(me), dst_ref=o_ref.at[2 * px + py], send_sem=send_sems.at[j],
                                         recv_sem=recv_sems.at[j], device_id=(px, py, mc), device_id_type=MESH).wait_recv()
        for cp in sends:
            cp.wait_send()
        loc.wait()

    return pl.pallas_call(
        body, name=name, out_shape=jax.ShapeDtypeStruct((N_CHIPS,) + tuple(blk), x.dtype), in_specs=[ANY], out_specs=ANY,
        scratch_shapes=[DMA_SEM((3,)), DMA_SEM((3,)), DMA_SEM])(x)


def _gather8(x, name):
    collected = _chip_exchange(x, name + "_chips", True)
    return jnp.swapaxes(_by_core(collected, _pair_swap(collected, name + "_pair")), 0, 1)


def _tile(n, want):
    if n <= want:
        return n
    t = (want // LANES) * LANES
    while t >= LANES:
        if n % t == 0:
            return t
        t -= LANES
    raise ValueError(f"no lane-aligned tile for {n}")


_DIMS = {"nn": (((1,), (0,)), ((), ())), "nt": (((1,), (1,)), ((), ())), "tn": (((0,), (0,)), ((), ()))}


MM_FULL_K = 4096
MM_SLAB_K = 2048
MM_TILES = ((1024, 1024), (512, 1024), (1024, 512), (512, 512), (256, 512), (256, 256))
MM_VMEM_BUDGET = 36 * 1024 * 1024


def _mm(a, b, mode, name, *, a_pro=None, epi=None, extras=(), out_dtype=F32, tm_max=1024, tn_max=1024, tk=None, a_cols=None):
    c0, csize = a_cols if a_cols is not None else (0, a.shape[1])
    if mode == "tn":
        K, M = a.shape[0], csize
    else:
        M, K = a.shape[0], csize
    N = b.shape[0] if mode == "nt" else b.shape[1]
    assert (b.shape[1] if mode == "nt" else b.shape[0]) == K, (a.shape, b.shape, mode)
    if tk is None:
        tk = K if (mode != "tn" and K <= MM_FULL_K) else MM_SLAB_K
    tk = _tile(K, tk)
    nk = K // tk
    n_mn = sum(1 for _, kind in extras if kind == "mn")
    for tm_want, tn_want in MM_TILES:
        tm, tn = _tile(M, min(tm_want, tm_max)), _tile(N, min(tn_want, tn_max))
        need = 2 * (tm * tk * a.dtype.itemsize + tk * tn * b.dtype.itemsize + tm * tn * 4 * (1 + n_mn)) + tm * tn * 4 * (nk > 1)
        if need <= MM_VMEM_BUDGET:
            break
    if mode == "tn":
        assert c0 % tm == 0
        a_spec = pl.BlockSpec((tk, tm), lambda i, j, k: (k, i + c0 // tm))
    else:
        assert c0 % tk == 0
        a_spec = pl.BlockSpec((tm, tk), lambda i, j, k: (i, k + c0 // tk))
    b_spec = pl.BlockSpec((tn, tk), lambda i, j, k: (j, k)) if mode == "nt" else pl.BlockSpec((tk, tn), lambda i, j, k: (k, j))
    ex_specs = []
    for arr, kind in extras:
        if kind == "mn":
            assert arr.shape == (M, N)
            ex_specs.append(pl.BlockSpec((tm, tn), lambda i, j, k: (i, j)))
        else:
            assert arr.shape == (1, N)
            ex_specs.append(pl.BlockSpec((1, tn), lambda i, j, k: (0, j)))
    n_ex = len(extras)

    def body(*refs):
        a_ref, b_ref = refs[:2]
        ex_refs = refs[2:2 + n_ex]
        o_ref = refs[2 + n_ex]
        acc_ref = refs[3 + n_ex] if nk > 1 else None
        k = pl.program_id(2)
        av = a_ref[...]
        if a_pro is not None:
            av = a_pro(av)
        part = lax.dot_general(av.astype(BF16), b_ref[...].astype(BF16), _DIMS[mode], preferred_element_type=F32)
        if nk == 1:
            if epi is not None:
                part = epi(part, *[r[...] for r in ex_refs])
            o_ref[...] = part.astype(o_ref.dtype)
            return

        @pl.when(k == 0)
        def _():
            acc_ref[...] = part

        @pl.when(k > 0)
        def _():
            acc_ref[...] += part

        @pl.when(k == nk - 1)
        def _():
            acc = acc_ref[...]
            if epi is not None:
                acc = epi(acc, *[r[...] for r in ex_refs])
            o_ref[...] = acc.astype(o_ref.dtype)

    return pl.pallas_call(
        body, name=name, grid=(M // tm, N // tn, nk),
        in_specs=[a_spec, b_spec] + ex_specs,
        out_specs=pl.BlockSpec((tm, tn), lambda i, j, k: (i, j)),
        out_shape=jax.ShapeDtypeStruct((M, N), out_dtype),
        scratch_shapes=[pltpu.VMEM((tm, tn), F32)] if nk > 1 else [],
        compiler_params=_cparams("parallel", "parallel", "arbitrary"))(a, b, *[e[0] for e in extras])


ROWS = 256


def _row_spec(w, ts=ROWS):
    return pl.BlockSpec((ts, w), lambda i: (i, 0))


def _vec_spec(w):
    return pl.BlockSpec((1, w), lambda i: (0, 0))


def _res_rms(x, sc, sh, name, y=None, g=None):
    S, D = x.shape
    has_res = y is not None

    def body(*refs):
        if has_res:
            x_ref, y_ref, g_ref, sc_ref, sh_ref, xo_ref, h_ref = refs
            xv = x_ref[...] + g_ref[...] * y_ref[...]
            xo_ref[...] = xv
        else:
            x_ref, sc_ref, sh_ref, h_ref = refs
            xv = x_ref[...]
        r = lax.rsqrt(jnp.mean(xv * xv, axis=-1, keepdims=True) + EPS)
        h_ref[...] = (xv * r * (1.0 + sc_ref[...]) + sh_ref[...]).astype(BF16)

    row, vec = _row_spec(D), _vec_spec(D)
    if has_res:
        return pl.pallas_call(body, name=name, grid=(S // ROWS,), in_specs=[row, row, vec, vec, vec], out_specs=[row, row],
                              out_shape=[jax.ShapeDtypeStruct((S, D), F32), jax.ShapeDtypeStruct((S, D), BF16)],
                              compiler_params=_cparams("parallel"))(x, y, g, sc, sh)
    h = pl.pallas_call(body, name=name, grid=(S // ROWS,), in_specs=[row, vec, vec], out_specs=row,
                       out_shape=jax.ShapeDtypeStruct((S, D), BF16), compiler_params=_cparams("parallel"))(x, sc, sh)
    return x, h


def _res_rms_bwd(x, dh, sc, dres, name, y=None, g=None):
    S, D = x.shape
    has_res = y is not None

    def body(*refs):
        if has_res:
            x_ref, dh_ref, sc_ref, dres_ref, y_ref, g_ref, dx_ref, dy_ref, dg_ref, dsc_ref, dsh_ref = refs
        else:
            x_ref, dh_ref, sc_ref, dres_ref, dx_ref, dsc_ref, dsh_ref = refs
        first = pl.program_id(0) == 0
        xv = x_ref[...]
        dh = dh_ref[...]
        r = lax.rsqrt(jnp.mean(xv * xv, axis=-1, keepdims=True) + EPS)
        xn = xv * r
        dxn = dh * (1.0 + sc_ref[...])
        dx = dres_ref[...] + r * (dxn - xn * jnp.mean(dxn * xn, axis=-1, keepdims=True))
        dx_ref[...] = dx
        parts = [(dsc_ref, jnp.sum(dh * xn, axis=0, keepdims=True)), (dsh_ref, jnp.sum(dh, axis=0, keepdims=True))]
        if has_res:
            dy_ref[...] = (dx * g_ref[...]).astype(BF16)
            parts.append((dg_ref, jnp.sum(dx * y_ref[...], axis=0, keepdims=True)))
        for ref, val in parts:
            @pl.when(first)
            def _(ref=ref, val=val):
                ref[...] = val

            @pl.when(jnp.logical_not(first))
            def _(ref=ref, val=val):
                ref[...] += val

    row, vec = _row_spec(D), _vec_spec(D)
    full = jax.ShapeDtypeStruct((S, D), F32)
    v = jax.ShapeDtypeStruct((1, D), F32)
    if has_res:
        return pl.pallas_call(body, name=name, grid=(S // ROWS,), in_specs=[row, row, vec, row, row, vec],
                              out_specs=[row, row, vec, vec, vec], out_shape=[full, jax.ShapeDtypeStruct((S, D), BF16), v, v, v],
                              compiler_params=_cparams("arbitrary"))(x, dh, sc, dres, y, g)
    return pl.pallas_call(body, name=name, grid=(S // ROWS,), in_specs=[row, row, vec, row],
                          out_specs=[row, vec, vec], out_shape=[full, v, v],
                          compiler_params=_cparams("arbitrary"))(x, dh, sc, dres)


def _res_loss(x, m, g, target, name):
    S, D = x.shape

    def body(x_ref, m_ref, g_ref, t_ref, loss_ref, dx_ref, dm_ref, dg_ref):
        first = pl.program_id(0) == 0
        mv = m_ref[...]
        err = x_ref[...] + g_ref[...] * mv - t_ref[...]
        dx = err * (1.0 / D)
        dx_ref[...] = dx
        dm_ref[...] = (dx * g_ref[...]).astype(BF16)
        part = 0.5 * jnp.sum(jnp.mean(err * err, axis=-1, keepdims=True), axis=0, keepdims=True)
        dg = jnp.sum(dx * mv, axis=0, keepdims=True)

        @pl.when(first)
        def _():
            loss_ref[...] = jnp.broadcast_to(part, loss_ref.shape)
            dg_ref[...] = dg

        @pl.when(jnp.logical_not(first))
        def _():
            loss_ref[...] += jnp.broadcast_to(part, loss_ref.shape)
            dg_ref[...] += dg

    row, vec = _row_spec(D), _vec_spec(D)
    full = jax.ShapeDtypeStruct((S, D), F32)
    return pl.pallas_call(body, name=name, grid=(S // ROWS,), in_specs=[row, row, vec, row],
                          out_specs=[pl.BlockSpec((8, LANES), lambda i: (0, 0)), row, row, vec],
                          out_shape=[jax.ShapeDtypeStruct((8, LANES), F32), full, jax.ShapeDtypeStruct((S, D), BF16), jax.ShapeDtypeStruct((1, D), F32)],
                          compiler_params=_cparams("arbitrary"))(x, m, g, target)


def _adamw(w, g, m, v, name):
    R, C = w.shape
    tr = R if R <= 256 else 256
    assert R % tr == 0

    def body(w_ref, g_ref, m_ref, v_ref, d_ref, nm_ref, nv_ref):
        gv = g_ref[...]
        nm = ADAM_B1 * m_ref[...] + (1.0 - ADAM_B1) * gv
        nv = ADAM_B2 * v_ref[...] + (1.0 - ADAM_B2) * jnp.square(gv)
        m_hat = nm / (1.0 - ADAM_B1 ** ADAM_STEP)
        v_hat = nv / (1.0 - ADAM_B2 ** ADAM_STEP)
        d_ref[...] = -ADAM_LR * (m_hat / (jnp.sqrt(v_hat) + ADAM_EPS) + ADAM_WD * w_ref[...])
        nm_ref[...] = nm
        nv_ref[...] = nv

    spec = pl.BlockSpec((tr, C), lambda i: (i, 0))
    out = jax.ShapeDtypeStruct((R, C), F32)
    return pl.pallas_call(body, name=name, grid=(R // tr,), in_specs=[spec] * 4, out_specs=[spec] * 3,
                          out_shape=[out, out, out], compiler_params=_cparams("parallel"))(w, g, m, v)


def _sum_slots(x, name):
    n, R, C = x.shape
    tr = R if R <= 256 else 256
    assert R % tr == 0

    def body(x_ref, o_ref):
        acc = x_ref[0].astype(F32)
        for j in range(1, n):
            acc = acc + x_ref[j].astype(F32)
        o_ref[...] = acc

    return pl.pallas_call(body, name=name, grid=(R // tr,), in_specs=[pl.BlockSpec((n, tr, C), lambda i: (0, i, 0))],
                          out_specs=pl.BlockSpec((tr, C), lambda i: (i, 0)), out_shape=jax.ShapeDtypeStruct((R, C), F32),
                          compiler_params=_cparams("parallel"))(x)


def _ew(fn, name, tiled, consts=(), outs=(), sums=(), ts=ROWS):
    tiled = [t if isinstance(t, tuple) else (t, t.shape[1], 0) for t in tiled]
    S = tiled[0][0].shape[0]
    n_t, n_c, n_o, n_s = len(tiled), len(consts), len(outs), len(sums)

    def body(*refs):
        ins = [r[...] for r in refs[:n_t + n_c]]
        res = fn(*ins)
        res = res if isinstance(res, (tuple, list)) else (res,)
        assert len(res) == n_o + n_s
        o_refs = refs[n_t + n_c:]
        for r, val in zip(o_refs[:n_o], res[:n_o]):
            r[...] = val.astype(r.dtype)
        first = pl.program_id(0) == 0
        for r, val in zip(o_refs[n_o:], res[n_o:]):
            @pl.when(first)
            def _(r=r, val=val):
                r[...] = val

            @pl.when(jnp.logical_not(first))
            def _(r=r, val=val):
                r[...] += val

    in_specs = [pl.BlockSpec((ts, w), lambda i, cb=cb: (i, cb)) for _, w, cb in tiled]
    in_specs += [pl.BlockSpec(c.shape, lambda i, nd=c.ndim: (0,) * nd) for c in consts]
    out_specs = [_row_spec(w, ts) for w, _ in outs] + [_vec_spec(w) for w in sums]
    out_shape = [jax.ShapeDtypeStruct((S, w), dt) for w, dt in outs] + [jax.ShapeDtypeStruct((1, w), F32) for w in sums]
    res = pl.pallas_call(body, name=name, grid=(S // ts,), in_specs=in_specs, out_specs=out_specs, out_shape=out_shape,
                         compiler_params=_cparams("arbitrary" if sums else "parallel"))(*[t[0] for t in tiled], *consts)
    return res


_GELU_C = math.sqrt(2.0 / math.pi)


def _gelu(x):
    return 0.5 * x * (1.0 + jnp.tanh(_GELU_C * (x + 0.044715 * x * x * x)))


def _dgelu(x):
    t = jnp.tanh(_GELU_C * (x + 0.044715 * x * x * x))
    return 0.5 * (1.0 + t) + 0.5 * x * (1.0 - t * t) * _GELU_C * (1.0 + 3.0 * 0.044715 * x * x)


def _sigmoid(x):
    return 1.0 / (1.0 + jnp.exp(-x))


def _log_sigmoid(x):
    return jnp.minimum(x, 0.0) - jnp.log(1.0 + jnp.exp(-jnp.abs(x)))


SCAN_T = 128
SCAN_TB = 512


def _cmul(ar, ai, br, bi):
    return ar * br - ai * bi, ar * bi + ai * br


def _s5_discretise(lam_re, lam_im, log_dt, b_re, b_im):
    dt = jnp.exp(log_dt)[:, None]
    mag = jnp.exp(lam_re * dt)
    ang = lam_im * dt
    abar_re = mag * jnp.cos(ang)
    abar_im = mag * jnp.sin(ang)
    den = lam_re * lam_re + lam_im * lam_im
    coef_re = ((abar_re - 1.0) * lam_re + abar_im * lam_im) / den
    coef_im = (abar_im * lam_re - (abar_re - 1.0) * lam_im) / den
    bbar_re = coef_re[..., None] * b_re - coef_im[..., None] * b_im
    bbar_im = coef_re[..., None] * b_im + coef_im[..., None] * b_re
    return abar_re, abar_im, bbar_re, bbar_im


def _planes(re, im):
    lead = re.shape[:-1]
    return jnp.stack([re.reshape(lead + (-1, LANES)), im.reshape(lead + (-1, LANES))], axis=-2).reshape(lead + (-1,))


def _unplanes(x):
    lead = x.shape[:-1]
    x4 = x.reshape(lead + (-1, 2, LANES))
    return x4[..., 0, :].reshape(lead + (-1,)), x4[..., 1, :].reshape(lead + (-1,))


def _s5_scan_tables(a_re, a_im, reverse):
    pr, pi = [a_re], [a_im]
    for _ in range(7):
        r, i = _cmul(pr[-1], pi[-1], pr[-1], pi[-1])
        pr.append(r)
        pi.append(i)
    apow = _planes(jnp.stack(pr), jnp.stack(pi))
    n = np.arange(1, SCAN_T + 1)
    if reverse:
        n = n[::-1]
    tr = jnp.ones((SCAN_T, a_re.shape[0]), F32)
    ti = jnp.zeros((SCAN_T, a_re.shape[0]), F32)
    for k in range(8):
        bit = jnp.asarray(((n >> k) & 1).astype(np.float32))[:, None]
        mr = bit * pr[k][None, :] + (1.0 - bit)
        mi = bit * pi[k][None, :]
        tr, ti = _cmul(tr, ti, mr, mi)
    return apow, _planes(tr, ti)


def _s5_scan(bu, apow, ptab, name, reverse, x_fwd=None):
    S, N2 = bu.shape
    T, W = SCAN_T, 2 * LANES
    tb = min(SCAN_TB, S)
    nt, nsub = S // tb, tb // T
    order = list(range(nsub - 1, -1, -1) if reverse else range(nsub))
    with_da = x_fwd is not None

    def tblk(t):
        return (nt - 1 - t) if reverse else t

    def shifted(v, k, rowi):
        s = 1 << k
        if s < 8:
            if reverse:
                return jnp.where(rowi < T - s, pltpu.roll(v, T - s, 0), 0.0)
            return jnp.where(rowi >= s, pltpu.roll(v, s, 0), 0.0)
        z = jnp.zeros((s, LANES), F32)
        return jnp.concatenate([v[s:], z], axis=0) if reverse else jnp.concatenate([z, v[:T - s]], axis=0)

    def body(*refs):
        if with_da:
            bu_ref, ap_ref, pt_ref, xf_ref, xp_ref, x_ref, da_ref, carry_ref = refs
        else:
            bu_ref, ap_ref, pt_ref, x_ref, carry_ref = refs
        t = pl.program_id(1)

        @pl.when(t == 0)
        def _():
            carry_ref[...] = jnp.zeros_like(carry_ref)
            if with_da:
                da_ref[...] = jnp.zeros_like(da_ref)

        rowi = lax.broadcasted_iota(jnp.int32, (T, LANES), 0)
        pr, pi = pt_ref[:, :LANES], pt_ref[:, LANES:]
        cr, ci = carry_ref[0:1, :LANES], carry_ref[0:1, LANES:]
        for sb in order:
            rows = pl.ds(sb * T, T)
            xr, xi = bu_ref[rows, :LANES], bu_ref[rows, LANES:]
            for k in range(7):
                ar, ai = ap_ref[k:k + 1, :LANES], ap_ref[k:k + 1, LANES:]
                rr, ri = shifted(xr, k, rowi), shifted(xi, k, rowi)
                xr, xi = xr + ar * rr - ai * ri, xi + ar * ri + ai * rr
            xr, xi = xr + pr * cr - pi * ci, xi + pr * ci + pi * cr
            x_ref[rows, :LANES] = xr
            x_ref[rows, LANES:] = xi
            edge = pl.ds(sb * T + (0 if reverse else T - 1), 1)
            cr, ci = x_ref[edge, :LANES], x_ref[edge, LANES:]
            if with_da:
                if sb > 0:
                    before = pl.ds(sb * T - 1, 1)
                    b_r, b_i = xf_ref[before, :LANES], xf_ref[before, LANES:]
                else:
                    keep = (tblk(t) > 0).astype(F32)
                    b_r, b_i = xp_ref[7:8, :LANES] * keep, xp_ref[7:8, LANES:] * keep
                fr, fi = xf_ref[rows, :LANES], xf_ref[rows, LANES:]
                qr = jnp.where(rowi >= 1, pltpu.roll(fr, 1, 0), b_r)
                qi = jnp.where(rowi >= 1, pltpu.roll(fi, 1, 0), b_i)
                gr, gi = xr * qr + xi * qi, xi * qr - xr * qi
                sr, si = gr[0:8], gi[0:8]
                for j in range(1, T // 8):
                    sr, si = sr + gr[8 * j:8 * j + 8], si + gi[8 * j:8 * j + 8]
                da_ref[:, :LANES] += sr
                da_ref[:, LANES:] += si
        carry_ref[0:1, :LANES] = cr
        carry_ref[0:1, LANES:] = ci

    blk = pl.BlockSpec((tb, W), lambda j, t: (tblk(t), j))
    in_specs = [blk, pl.BlockSpec((8, W), lambda j, t: (0, j)), pl.BlockSpec((T, W), lambda j, t: (0, j))]
    out_specs, out_shape = [blk], [jax.ShapeDtypeStruct((S, N2), F32)]
    args = [bu, apow, ptab]
    if with_da:
        in_specs += [blk, pl.BlockSpec((8, W), lambda j, t: (jnp.maximum(tblk(t) * (tb // 8) - 1, 0), j))]
        out_specs.append(pl.BlockSpec((8, W), lambda j, t: (0, j)))
        out_shape.append(jax.ShapeDtypeStruct((8, N2), F32))
        args += [x_fwd, x_fwd]
    res = pl.pallas_call(body, name=name, grid=(N2 // W, nt), in_specs=in_specs, out_specs=out_specs, out_shape=out_shape,
                         scratch_shapes=[pltpu.VMEM((8, W), F32)], compiler_params=_cparams("parallel", "arbitrary"))(*args)
    return res if with_da else res[0]


def _block_diag(t):
    G, a, b = t.shape
    return (t[:, :, None, :] * jnp.eye(G, dtype=t.dtype)[:, None, :, None]).reshape(G * a, G * b)


def _block_diag_take(m, G):
    a, b = m.shape[0] // G, m.shape[1] // G
    m4 = m.reshape(G, a, G, b)
    return jnp.sum(m4 * jnp.eye(G, dtype=m.dtype)[:, None, :, None], axis=2)


def _s5_block_fwd(u, w, pfx):
    a_re, a_im, bb_re, bb_im = _s5_discretise(w["lam_re"], w["lam_im"], w["log_dt"], w["b_re"], w["b_im"])
    bcat = _planes(_block_diag(bb_re).T, _block_diag(bb_im).T).astype(BF16)
    ccat = _planes(_block_diag(jnp.swapaxes(w["c_re"], 1, 2)).T, -_block_diag(jnp.swapaxes(w["c_im"], 1, 2)).T).T.astype(BF16)
    af_re, af_im = a_re.reshape(-1), a_im.reshape(-1)
    apow, ptab = _s5_scan_tables(af_re, af_im, False)
    bu = _mm(u, bcat, "nn", pfx + "_bu")
    x = _s5_scan(bu, apow, ptab, pfx + "_scan", False)
    d_row = w["d"].reshape(1, MIX_HALF)
    ys = _mm(x, ccat, "nn", pfx + "_y", epi=lambda acc, ut, dr: acc + dr * ut, extras=[(u, "mn"), (d_row, "n")])
    z = _mm(ys, w["w_glu"], "nn", pfx + "_glu", a_pro=_gelu, epi=lambda acc, b: acc + b, extras=[(w["b_glu"].reshape(1, -1), "n")])
    y2, = _ew(lambda ysv, zv: _gelu(ysv) * _sigmoid(zv), pfx + "_gate", [ys, z], outs=[(MIX_HALF, F32)])
    return y2, dict(u=u, x=x, ys=ys, z=z, bcat=bcat, ccat=ccat, a=(af_re, af_im), d_row=d_row)


def _s5_block_bwd(dy2, w, res, pfx):
    u, x, ys, z, bcat, ccat = res["u"], res["x"], res["ys"], res["z"], res["bcat"], res["ccat"]

    def gate_bwd(dy, ysv, zv):
        sg = _sigmoid(zv)
        dz = dy * _gelu(ysv) * sg * (1.0 - sg)
        return dz, jnp.sum(dz, axis=0, keepdims=True)

    dz, db_glu = _ew(gate_bwd, pfx + "_gate_bwd", [dy2, ys, z], outs=[(MIX_HALF, F32)], sums=[MIX_HALF])
    dw_glu = _mm(ys, dz, "tn", pfx + "_dwglu", a_pro=_gelu)
    dys = _mm(dz, w["w_glu"], "nt", pfx + "_dys", epi=lambda acc, dy, zv, ysv: (acc + dy * _sigmoid(zv)) * _dgelu(ysv),
              extras=[(dy2, "mn"), (z, "mn"), (ys, "mn")])
    dd, = _ew(lambda a, b: jnp.sum(a * b, axis=0, keepdims=True), pfx + "_dd", [dys, u], sums=[MIX_HALF])
    dccat = _mm(x, dys, "tn", pfx + "_dc")
    dx = _mm(dys, ccat, "nt", pfx + "_dx")
    af_re, af_im = res["a"]
    apow, ptab = _s5_scan_tables(af_re, -af_im, True)
    lam, da8 = _s5_scan(dx, apow, ptab, pfx + "_scan_bwd", True, x_fwd=x)
    dbcat = _mm(u, lam, "tn", pfx + "_db")
    du = _mm(lam, bcat, "nt", pfx + "_du", epi=lambda acc, dyv, dr: acc + dyv * dr, extras=[(dys, "mn"), (res["d_row"], "n")])
    G = S5_GROUPS
    d_abar_re, d_abar_im = (t.reshape(G, S5_STATE) for t in _unplanes(jnp.sum(da8, axis=0)))
    d_bb_re, d_bb_im = (_block_diag_take(t.T, G) for t in _unplanes(dbcat))
    _, vjp = jax.vjp(_s5_discretise, w["lam_re"], w["lam_im"], w["log_dt"], w["b_re"], w["b_im"])
    g_lam_re, g_lam_im, g_log_dt, g_b_re, g_b_im = vjp((d_abar_re, d_abar_im, d_bb_re, d_bb_im))
    dc_re, dc_im = _unplanes(dccat.T)
    g_c_re = jnp.swapaxes(_block_diag_take(dc_re.T, G), 1, 2)
    g_c_im = -jnp.swapaxes(_block_diag_take(dc_im.T, G), 1, 2)
    grads = dict(lam_re=g_lam_re, lam_im=g_lam_im, log_dt=g_log_dt, b_re=g_b_re, b_im=g_b_im, c_re=g_c_re, c_im=g_c_im,
                 d=dd.reshape(G, S5_GROUP_WIDTH), w_glu=dw_glu, b_glu=db_glu.reshape(-1))
    return du, grads


SGU_TS = 512
N_PAIRS = MIX_HALF // LANES


def _half_masks(rows):
    lane = lax.broadcasted_iota(jnp.int32, (rows, LANES), 1)
    left = (lane < HEAD_DIM).astype(F32)
    return left, 1.0 - left


def _sgu_norm(zv, gain, bias):
    v = _gelu(zv)
    mu = jnp.mean(v, axis=-1, keepdims=True)
    vc = v - mu
    rstd = lax.rsqrt(jnp.mean(vc * vc, axis=-1, keepdims=True) + EPS)
    vhat = vc * rstd
    return vhat, rstd, vhat * gain + bias


def _sgu_tables(w_s, b_s):
    mask = jnp.tril(jnp.ones((SGU_CHUNK, SGU_CHUNK), dtype=bool))
    wm = jnp.where(mask[None], w_s, 0.0).astype(BF16)
    bias_tab = jnp.repeat(b_s.T, MIX_HALF // SGU_GROUPS, axis=1)
    return wm, bias_tab


def _sgu_fwd(proj, ln_gain, ln_bias, wm, bias_tab, name):
    S = proj.shape[0]
    nch = SGU_TS // SGU_CHUNK

    def body(zu_ref, zv_ref, g_ref, b_ref, w_ref, bt_ref, o_ref):
        left, right = _half_masks(SGU_CHUNK)
        _, _, vn = _sgu_norm(zv_ref[...], g_ref[...], b_ref[...])
        for ch in range(nch):
            rows = pl.ds(ch * SGU_CHUNK, SGU_CHUNK)
            for p in range(N_PAIRS):
                cols = pl.ds(p * LANES, LANES)
                vp = vn[ch * SGU_CHUNK:(ch + 1) * SGU_CHUNK, p * LANES:(p + 1) * LANES]
                mixed = (jnp.dot(w_ref[2 * p], (vp * left).astype(BF16), preferred_element_type=F32)
                         + jnp.dot(w_ref[2 * p + 1], (vp * right).astype(BF16), preferred_element_type=F32) + bt_ref[:, cols])
                o_ref[rows, cols] = _gelu(zu_ref[rows, cols]) * mixed

    vec = _vec_spec(MIX_HALF)
    return pl.pallas_call(
        body, name=name, grid=(S // SGU_TS,),
        in_specs=[pl.BlockSpec((SGU_TS, MIX_HALF), lambda i: (i, 1)), pl.BlockSpec((SGU_TS, MIX_HALF), lambda i: (i, 2)), vec, vec,
                  pl.BlockSpec((SGU_GROUPS, SGU_CHUNK, SGU_CHUNK), lambda i: (0, 0, 0)), pl.BlockSpec((SGU_CHUNK, MIX_HALF), lambda i: (0, 0))],
        out_specs=_row_spec(MIX_HALF, SGU_TS), out_shape=jax.ShapeDtypeStruct((S, MIX_HALF), F32),
        compiler_params=_cparams("parallel"))(proj, proj, ln_gain, ln_bias, wm, bias_tab)


def _sgu_bwd(dout, proj, ln_gain, ln_bias, wm, bias_tab, name):
    S = proj.shape[0]
    nch = SGU_TS // SGU_CHUNK
    nt_dims = (((1,), (1,)), ((), ()))
    tn_dims = (((0,), (0,)), ((), ()))

    def body(do_ref, zu_ref, zv_ref, g_ref, b_ref, w_ref, bt_ref, dzu_ref, dzv_ref, dw_ref, dbt_ref, dg_ref, db_ref, dvn_ref):
        first = pl.program_id(0) == 0

        @pl.when(first)
        def _():
            dw_ref[...] = jnp.zeros_like(dw_ref)
            dbt_ref[...] = jnp.zeros_like(dbt_ref)
            dg_ref[...] = jnp.zeros_like(dg_ref)
            db_ref[...] = jnp.zeros_like(db_ref)

        left, right = _half_masks(SGU_CHUNK)
        zv = zv_ref[...]
        vhat, rstd, vn = _sgu_norm(zv, g_ref[...], b_ref[...])
        for ch in range(nch):
            rows = pl.ds(ch * SGU_CHUNK, SGU_CHUNK)
            for p in range(N_PAIRS):
                cols = pl.ds(p * LANES, LANES)
                vp = vn[ch * SGU_CHUNK:(ch + 1) * SGU_CHUNK, p * LANES:(p + 1) * LANES]
                vl, vr = (vp * left).astype(BF16), (vp * right).astype(BF16)
                mixed = (jnp.dot(w_ref[2 * p], vl, preferred_element_type=F32)
                         + jnp.dot(w_ref[2 * p + 1], vr, preferred_element_type=F32) + bt_ref[:, cols])
                zu = zu_ref[rows, cols]
                do = do_ref[rows, cols]
                dzu_ref[rows, cols] = do * mixed * _dgelu(zu)
                dmix = do * _gelu(zu)
                dbt_ref[:, cols] += dmix
                dl, dr = (dmix * left).astype(BF16), (dmix * right).astype(BF16)
                dw_ref[2 * p] += lax.dot_general(dl, vl, nt_dims, preferred_element_type=F32)
                dw_ref[2 * p + 1] += lax.dot_general(dr, vr, nt_dims, preferred_element_type=F32)
                dvn_ref[rows, cols] = (lax.dot_general(w_ref[2 * p], dl, tn_dims, preferred_element_type=F32)
                                       + lax.dot_general(w_ref[2 * p + 1], dr, tn_dims, preferred_element_type=F32))
        dvn = dvn_ref[...]
        dg_ref[...] += jnp.sum(dvn * vhat, axis=0, keepdims=True)
        db_ref[...] += jnp.sum(dvn, axis=0, keepdims=True)
        dvh = dvn * g_ref[...]
        dv = rstd * (dvh - jnp.mean(dvh, axis=-1, keepdims=True) - vhat * jnp.mean(dvh * vhat, axis=-1, keepdims=True))
        dzv_ref[...] = dv * _dgelu(zv)

    vec = _vec_spec(MIX_HALF)
    row = _row_spec(MIX_HALF, SGU_TS)
    wspec = pl.BlockSpec((SGU_GROUPS, SGU_CHUNK, SGU_CHUNK), lambda i: (0, 0, 0))
    tspec = pl.BlockSpec((SGU_CHUNK, MIX_HALF), lambda i: (0, 0))
    full = jax.ShapeDtypeStruct((S, MIX_HALF), F32)
    v = jax.ShapeDtypeStruct((1, MIX_HALF), F32)
    return pl.pallas_call(
        body, name=name, grid=(S // SGU_TS,),
        in_specs=[row, pl.BlockSpec((SGU_TS, MIX_HALF), lambda i: (i, 1)), pl.BlockSpec((SGU_TS, MIX_HALF), lambda i: (i, 2)), vec, vec,
                  wspec, tspec],
        out_specs=[row, row, wspec, tspec, vec, vec],
        out_shape=[full, full, jax.ShapeDtypeStruct((SGU_GROUPS, SGU_CHUNK, SGU_CHUNK), F32),
                   jax.ShapeDtypeStruct((SGU_CHUNK, MIX_HALF), F32), v, v],
        scratch_shapes=[pltpu.VMEM((SGU_TS, MIX_HALF), F32)],
        compiler_params=_cparams("arbitrary"))(dout, proj, proj, ln_gain, ln_bias, wm, bias_tab)


def _sgu_grads(dw, dbias_tab):
    mask = jnp.tril(jnp.ones((SGU_CHUNK, SGU_CHUNK), dtype=bool))
    g_w = jnp.where(mask[None], dw, 0.0)
    g_b = dbias_tab.reshape(SGU_CHUNK, SGU_GROUPS, MIX_HALF // SGU_GROUPS).sum(axis=-1).T
    return g_w, g_b


def _head_avg_matrix(w):
    idx = np.arange(w) // HEAD_DIM
    return jnp.asarray((idx[:, None] == idx[None, :]).astype(np.float32) / HEAD_DIM, dtype=BF16)


def _head_mean(t, bavg):
    hi = t.astype(BF16)
    lo = (t - hi.astype(F32)).astype(BF16)
    return jnp.dot(hi, bavg, preferred_element_type=F32) + jnp.dot(lo, bavg, preferred_element_type=F32)


def _head_rms(t, bavg):
    r = lax.rsqrt(_head_mean(t * t, bavg) + EPS)
    return t * r, r


def _head_rms_bwd(dn, n, r, bavg):
    return r * (dn - n * _head_mean(dn * n, bavg))


GLA_TS = 512
C = GLA_CHUNK
NT_DIMS = (((1,), (1,)), ((), ()))
TN_DIMS = (((0,), (0,)), ((), ()))
HI = lax.Precision.HIGHEST


def _bdot(a, b, dims=(((1,), (0,)), ((), ()))):
    return lax.dot_general(a.astype(BF16), b.astype(BF16), dims, preferred_element_type=F32)


def _gla_chunk_terms(q, k, z):
    row = lax.broadcasted_iota(jnp.int32, (C, C), 0)
    col = lax.broadcasted_iota(jnp.int32, (C, C), 1)
    lc = _log_sigmoid(z) * (1.0 / GLA_TAU)
    b = lax.dot_general((row >= col).astype(F32), lc, (((1,), (0,)), ((), ())), precision=HI, preferred_element_type=F32)
    b_last = jnp.sum(lc, axis=0, keepdims=True)
    b_mid = b[C // 2:C // 2 + 1, :]
    scale = HEAD_DIM ** -0.5
    e_b, e_q, e_k, e_l = jnp.exp(b), jnp.exp(b - b_mid), jnp.exp(b_mid - b), jnp.exp(b_last - b)
    qs = q * (scale * e_b)
    qe = q * (scale * e_q)
    ke = k * e_k
    kl = k * e_l
    return dict(e_b=e_b, e_q=e_q, e_k=e_k, e_l=e_l, qs=qs, qe=qe, ke=ke, kl=kl, dec=jnp.exp(b_last), causal=row >= col, scale=scale)


def _pair(x, pp):
    return x[:, pp * LANES:(pp + 1) * LANES]


def _pair_block_diag():
    r = lax.broadcasted_iota(jnp.int32, (LANES, LANES), 0) // HEAD_DIM
    c = lax.broadcasted_iota(jnp.int32, (LANES, LANES), 1) // HEAD_DIM
    return (r == c).astype(F32)


def _gla_fwd(proj, z, name):
    S = proj.shape[0]
    nch = GLA_TS // C

    def body(q_ref, k_ref, v_ref, z_ref, o_ref, st_ref, state_ref):
        @pl.when(pl.program_id(0) == 0)
        def _():
            state_ref[...] = jnp.zeros_like(state_ref)

        left, right = _half_masks(C)
        bd = _pair_block_diag()
        pairs = range(N_PAIRS)
        for ch in range(nch):
            rows = pl.ds(ch * C, C)
            v = v_ref[rows, :]
            t = _gla_chunk_terms(q_ref[rows, :], k_ref[rows, :], z_ref[rows, :])
            sts = [state_ref[pp] for pp in pairs]
            for pp in pairs:
                st_ref[ch, pp] = sts[pp]
            os = [_bdot(_pair(t["qs"], pp), sts[pp], NT_DIMS) for pp in pairs]
            for m in (left, right):
                scores = [jnp.where(t["causal"], _bdot(_pair(t["qe"], pp) * m, _pair(t["ke"], pp), NT_DIMS), 0.0) for pp in pairs]
                os = [os[pp] + m * _bdot(scores[pp], _pair(v, pp)) for pp in pairs]
            o_ref[rows, :] = jnp.concatenate(os, axis=1)
            new = [sts[pp] * _pair(t["dec"], pp) + bd * _bdot(_pair(v, pp), _pair(t["kl"], pp), TN_DIMS) for pp in pairs]
            for pp in pairs:
                state_ref[pp] = new[pp]

    def col(cb):
        return pl.BlockSpec((GLA_TS, MIX_HALF), lambda i: (i, cb))

    return pl.pallas_call(
        body, name=name, grid=(S // GLA_TS,),
        in_specs=[col(0), col(1), col(2), col(0)],
        out_specs=[col(0), pl.BlockSpec((nch, N_PAIRS, LANES, LANES), lambda i: (i, 0, 0, 0))],
        out_shape=[jax.ShapeDtypeStruct((S, MIX_HALF), F32), jax.ShapeDtypeStruct((S // C, N_PAIRS, LANES, LANES), F32)],
        scratch_shapes=[pltpu.VMEM((N_PAIRS, LANES, LANES), F32)], compiler_params=_cparams("arbitrary"))(proj, proj, proj, z)


def _gla_bwd(do, proj, z, states, name):
    S = proj.shape[0]
    nch = GLA_TS // C
    nblk = S // GLA_TS

    def body(do_ref, q_ref, k_ref, v_ref, z_ref, st_ref, dq_ref, dk_ref, dv_ref, dlc_ref, dstate_ref):
        @pl.when(pl.program_id(0) == 0)
        def _():
            dstate_ref[...] = jnp.zeros_like(dstate_ref)

        left, right = _half_masks(C)
        bd = _pair_block_diag()
        rowi = lax.broadcasted_iota(jnp.int32, (C, LANES), 0)
        row = lax.broadcasted_iota(jnp.int32, (C, C), 0)
        colm = lax.broadcasted_iota(jnp.int32, (C, C), 1)
        pairs = range(N_PAIRS)
        rowi = lax.broadcasted_iota(jnp.int32, (C, MIX_HALF), 0)
        for ch in range(nch - 1, -1, -1):
            rows = pl.ds(ch * C, C)
            v, dov = v_ref[rows, :], do_ref[rows, :]
            t = _gla_chunk_terms(q_ref[rows, :], k_ref[rows, :], z_ref[rows, :])
            sts = [st_ref[ch, pp] for pp in pairs]
            nxt = [dstate_ref[pp] for pp in pairs]
            gs = [bd * nxt[pp] for pp in pairs]
            dqs = [_bdot(_pair(dov, pp), sts[pp]) for pp in pairs]
            dv = [_bdot(_pair(t["kl"], pp), gs[pp], NT_DIMS) for pp in pairs]
            dkl = [_bdot(_pair(v, pp), gs[pp]) for pp in pairs]
            dqe = [jnp.zeros((C, LANES), F32) for _ in pairs]
            dke = [jnp.zeros((C, LANES), F32) for _ in pairs]
            for m in (left, right):
                sc = [jnp.where(t["causal"], _bdot(_pair(t["qe"], pp) * m, _pair(t["ke"], pp), NT_DIMS), 0.0) for pp in pairs]
                dsc = [jnp.where(t["causal"], _bdot(_pair(dov, pp) * m, _pair(v, pp), NT_DIMS), 0.0) for pp in pairs]
                dv = [dv[pp] + m * _bdot(sc[pp], _pair(dov, pp), TN_DIMS) for pp in pairs]
                dqe = [dqe[pp] + m * _bdot(dsc[pp], _pair(t["ke"], pp)) for pp in pairs]
                dke = [dke[pp] + m * _bdot(dsc[pp], _pair(t["qe"], pp), TN_DIMS) for pp in pairs]
            for pp in pairs:
                dstate_ref[pp] = bd * (nxt[pp] * _pair(t["dec"], pp) + _bdot(_pair(dov, pp), _pair(t["qs"], pp), TN_DIMS))
            decay_sum = jnp.concatenate([jnp.sum(nxt[pp] * sts[pp], axis=0, keepdims=True) for pp in pairs], axis=1)
            dqs, dv, dkl, dqe, dke = (jnp.concatenate(parts, axis=1) for parts in (dqs, dv, dkl, dqe, dke))
            db_last = decay_sum * t["dec"] + jnp.sum(dkl * t["kl"], axis=0, keepdims=True)
            db = dqs * t["qs"] + dqe * t["qe"] - dke * t["ke"] - dkl * t["kl"]
            db = db + jnp.where(rowi == C - 1, db_last, 0.0)
            dq_ref[rows, :] = (dqs * t["e_b"] + dqe * t["e_q"]) * t["scale"]
            dk_ref[rows, :] = dke * t["e_k"] + dkl * t["e_l"]
            dv_ref[rows, :] = dv
            dlc_ref[rows, :] = lax.dot_general((colm >= row).astype(F32), db, (((1,), (0,)), ((), ())), precision=HI,
                                               preferred_element_type=F32)

    def col(cb):
        return pl.BlockSpec((GLA_TS, MIX_HALF), lambda i: (nblk - 1 - i, cb))

    full = jax.ShapeDtypeStruct((S, MIX_HALF), F32)
    return pl.pallas_call(
        body, name=name, grid=(nblk,),
        in_specs=[col(0), col(0), col(1), col(2), col(0), pl.BlockSpec((nch, N_PAIRS, LANES, LANES), lambda i: (nblk - 1 - i, 0, 0, 0))],
        out_specs=[col(0)] * 4, out_shape=[full, full, full, full],
        scratch_shapes=[pltpu.VMEM((N_PAIRS, LANES, LANES), F32)], compiler_params=_cparams("arbitrary"))(do, proj, proj, proj, z, states)


def _gla_block_fwd(proj, w_lr_pad, b_lr, gain, bavg, pfx):
    z = _mm(proj, w_lr_pad, "nn", pfx + "_z", a_cols=(7 * MIX_HALF, MIX_HALF), epi=lambda acc, b: acc + b, extras=[(b_lr, "n")])
    o, states = _gla_fwd(proj, z, pfx + "_core")

    def out(ov, gg, ba, gn):
        n, _ = _head_rms(ov, ba)
        return n * gn * (gg * _sigmoid(gg))

    og, = _ew(out, pfx + "_out", [o, (proj, MIX_HALF, 3)], consts=[bavg, gain], outs=[(MIX_HALF, F32)])
    return og, dict(z=z, o=o, states=states)


def _gla_block_bwd(dog, proj, w_lr_pad, gain, bavg, res, pfx):
    z, o, states = res["z"], res["o"], res["states"]

    def out_bwd(dy, ov, gg, ba, gn):
        n, r = _head_rms(ov, ba)
        sg = _sigmoid(gg)
        silu = gg * sg
        dn = dy * gn * silu
        do = _head_rms_bwd(dn, n, r, ba)
        dgg = dy * n * gn * (sg * (1.0 + gg * (1.0 - sg)))
        return do, dgg, jnp.sum(dy * n * silu, axis=0, keepdims=True)

    do, dgg, dgain = _ew(out_bwd, pfx + "_out_bwd", [dog, o, (proj, MIX_HALF, 3)], consts=[bavg, gain],
                         outs=[(MIX_HALF, F32), (MIX_HALF, F32)], sums=[MIX_HALF])
    dq, dk, dv, dlc = _gla_bwd(do, proj, z, states, pfx + "_core_bwd")

    def decay_bwd(dl, zv):
        dz = dl * (1.0 / GLA_TAU) * (1.0 - _sigmoid(zv))
        return dz, jnp.sum(dz, axis=0, keepdims=True)

    dz, db_lr = _ew(decay_bwd, pfx + "_decay_bwd", [dlc, z], outs=[(MIX_HALF, F32)], sums=[MIX_HALF])
    dw_lr_pad = _mm(proj, dz, "tn", pfx + "_dwlr", a_cols=(7 * MIX_HALF, MIX_HALF))
    dsmall = _mm(dz, w_lr_pad, "nt", pfx + "_dsmall")
    return (dq, dk, dv, dgg, dsmall), dict(w_lr=dw_lr_pad[:GLA_RANK], b_lr=db_lr.reshape(-1), gain=dgain.reshape(-1, HEAD_DIM))


FOX_T = 512
FOX_HEADS = MIX_HALF // HEAD_DIM
NEG = -1e30
CUM_T = 512


def _cum_lanes(x, name, reverse, pre=None):
    R, S = x.shape
    nb = S // CUM_T

    def body(x_ref, o_ref, carry_ref):
        @pl.when(pl.program_id(0) == 0)
        def _():
            carry_ref[...] = jnp.zeros_like(carry_ref)

        xv = x_ref[...]
        if pre is not None:
            xv = pre(xv)
        i = lax.broadcasted_iota(jnp.int32, (CUM_T, CUM_T), 0)
        j = lax.broadcasted_iota(jnp.int32, (CUM_T, CUM_T), 1)
        tri = ((i >= j) if reverse else (i <= j)).astype(F32)
        c = lax.dot_general(xv, tri, (((1,), (0,)), ((), ())), precision=HI, preferred_element_type=F32)
        carry = carry_ref[...]
        o_ref[...] = c + carry[:, 0:1]
        carry_ref[...] = carry + jnp.sum(xv, axis=1, keepdims=True)

    spec = pl.BlockSpec((R, CUM_T), (lambda i: (0, nb - 1 - i)) if reverse else (lambda i: (0, i)))
    return pl.pallas_call(body, name=name, grid=(nb,), in_specs=[spec], out_specs=spec, out_shape=jax.ShapeDtypeStruct((R, S), F32),
                          scratch_shapes=[pltpu.VMEM((R, LANES), F32)], compiler_params=_cparams("arbitrary"))(x)


def _fox_scores(q, k, cqb, ck_ref, h, m, diag):
    cq = cqb[:, h * HEAD_DIM:h * HEAD_DIM + 1]
    ck = ck_ref[0, h:h + 1, :]
    s = lax.dot_general(q * m.astype(q.dtype), k, NT_DIMS, preferred_element_type=F32) + (cq - ck)
    if not diag:
        return s
    row = lax.broadcasted_iota(jnp.int32, (FOX_T, FOX_T), 0)
    col = lax.broadcasted_iota(jnp.int32, (FOX_T, FOX_T), 1)
    return jnp.where(row < col, NEG, s)


def _on_causal_blocks(q_blk, k_blk, step):
    @pl.when(k_blk < q_blk)
    def _():
        step(False)

    @pl.when(k_blk == q_blk)
    def _():
        step(True)


def _causal_pairs(n, key_major):
    if key_major:
        pairs = [(q, k) for k in range(n) for q in range(k, n)]
    else:
        pairs = [(q, k) for q in range(n) for k in range(q + 1)]
    return jnp.asarray([p[0] for p in pairs], jnp.int32), jnp.asarray([p[1] for p in pairs], jnp.int32)


def _fox_fwd(qn, kn, proj, cum_b, cum_tp, name):
    S = qn.shape[0]
    nq = S // FOX_T
    qidx, kidx = _causal_pairs(nq, False)

    def body(qidx_ref, kidx_ref, q_ref, k_ref, v_ref, cq_ref, ck_ref, o_ref, lse_ref, m_scr, acc_scr):
        t = pl.program_id(1)
        qi, ki = qidx_ref[t], kidx_ref[t]

        @pl.when(ki == 0)
        def _():
            m_scr[...] = jnp.full_like(m_scr, NEG)
            acc_scr[...] = jnp.zeros_like(acc_scr)

        left, right = _half_masks(FOX_T)

        def step(diag):
            q, k, v = q_ref[...], k_ref[...], v_ref[...].astype(BF16)
            cqb = cq_ref[...]
            for h, m in enumerate((left, right)):
                s = _fox_scores(q, k, cqb, ck_ref, h, m, diag)
                m_prev = m_scr[h]
                m_new = jnp.maximum(m_prev, jnp.max(s, axis=1, keepdims=True))
                p = jnp.exp(s - m_new)
                v_h = jnp.where(m > 0, v, jnp.ones_like(v))
                acc_scr[h] = jnp.exp(m_prev - m_new) * acc_scr[h] + jnp.dot(p.astype(BF16), v_h, preferred_element_type=F32)
                m_scr[h] = m_new

        _on_causal_blocks(qi, ki, step)

        @pl.when(ki == qi)
        def _():
            a0, a1 = acc_scr[0], acc_scr[1]
            is_left = left > 0
            num = jnp.where(is_left, a0, a1)
            den = jnp.where(is_left, pltpu.roll(a0, HEAD_DIM, 1), pltpu.roll(a1, HEAD_DIM, 1))
            o_ref[...] = num / den
            lse_ref[...] = jnp.where(is_left, m_scr[0], m_scr[1]) + jnp.log(den)

    qspec = pl.BlockSpec((FOX_T, LANES), lambda p, t, qx, kx: (qx[t], p))
    kspec = pl.BlockSpec((FOX_T, LANES), lambda p, t, qx, kx: (kx[t], p))
    vspec = pl.BlockSpec((FOX_T, LANES), lambda p, t, qx, kx: (kx[t], 6 * N_PAIRS + p))
    ckspec = pl.BlockSpec((1, 8, FOX_T), lambda p, t, qx, kx: (p, 0, kx[t]))
    full = jax.ShapeDtypeStruct((S, MIX_HALF), F32)
    grid_spec = pltpu.PrefetchScalarGridSpec(
        num_scalar_prefetch=2, grid=(N_PAIRS, int(qidx.shape[0])), in_specs=[qspec, kspec, vspec, qspec, ckspec], out_specs=[qspec, qspec],
        scratch_shapes=[pltpu.VMEM((2, FOX_T, 1), F32), pltpu.VMEM((2, FOX_T, LANES), F32)])
    return pl.pallas_call(body, name=name, grid_spec=grid_spec, out_shape=[full, full],
                          compiler_params=_cparams("parallel", "arbitrary"))(qidx, kidx, qn, kn, proj, cum_b, cum_tp)


def _fox_bwd(do, qn, kn, proj, cum_b, cum_tp, lse_b, delta_b, name):
    S = qn.shape[0]
    nq = S // FOX_T
    scale = HEAD_DIM ** -0.5
    qidx, kidx = _causal_pairs(nq, True)
    ntri = int(qidx.shape[0])

    def body(qidx_ref, kidx_ref, do_ref, q_ref, k_ref, v_ref, cq_ref, ck_ref, lse_ref, dl_ref,
             dq_ref, dcq_ref, dk_ref, dv_ref, dck_ref, dq_scr, dk_scr, dv_scr):
        t = pl.program_id(1)
        qi, ki = qidx_ref[t], kidx_ref[t]

        @pl.when(t == 0)
        def _():
            dq_scr[...] = jnp.zeros_like(dq_scr)

        @pl.when(qi == ki)
        def _():
            dk_scr[...] = jnp.zeros_like(dk_scr)
            dv_scr[...] = jnp.zeros_like(dv_scr)

        left, right = _half_masks(FOX_T)
        rows = pl.ds(pl.multiple_of(qi * FOX_T, FOX_T), FOX_T)

        def step(diag):
            q, k, v, dov = q_ref[...], k_ref[...], v_ref[...].astype(BF16), do_ref[...]
            cqb, lseb, dlb = cq_ref[...], lse_ref[...], dl_ref[...]
            dob = dov.astype(BF16)
            dv = dv_scr[...]
            for h, m in enumerate((left, right)):
                s = _fox_scores(q, k, cqb, ck_ref, h, m, diag)
                p = jnp.exp(s - lseb[:, h * HEAD_DIM:h * HEAD_DIM + 1])
                dp = lax.dot_general((dov * m).astype(BF16), v, NT_DIMS, preferred_element_type=F32)
                ds = (p * (dp - dlb[:, h * HEAD_DIM:h * HEAD_DIM + 1])).astype(BF16)
                dv = dv + m * lax.dot_general(p.astype(BF16), dob, TN_DIMS, preferred_element_type=F32)
                q_h = jnp.where(m > 0, q, jnp.ones_like(q))
                k_h = jnp.where(m > 0, k, jnp.ones_like(k))
                dk_scr[h] = dk_scr[h] + lax.dot_general(ds, q_h, TN_DIMS, preferred_element_type=F32)
                dq_scr[h, rows, :] = dq_scr[h, rows, :] + jnp.dot(ds, k_h, preferred_element_type=F32)
            dv_scr[...] = dv

        _on_causal_blocks(qi, ki, step)

        @pl.when(qi == nq - 1)
        def _():
            a0, a1 = dk_scr[0], dk_scr[1]
            dk_ref[...] = left * a0 + right * a1
            dv_ref[...] = dv_scr[...]
            dck_ref[...] = left * pltpu.roll(a0, HEAD_DIM, 1) + right * pltpu.roll(a1, HEAD_DIM, 1)

        @pl.when(t == ntri - 1)
        def _():
            for r in range(nq):
                blk = pl.ds(r * FOX_T, FOX_T)
                a0, a1 = dq_scr[0, blk, :], dq_scr[1, blk, :]
                dq_ref[blk, :] = (left * a0 + right * a1) * scale
                dcq_ref[blk, :] = left * pltpu.roll(a0, HEAD_DIM, 1) + right * pltpu.roll(a1, HEAD_DIM, 1)

    qspec = pl.BlockSpec((FOX_T, LANES), lambda p, t, qx, kx: (qx[t], p))
    kspec = pl.BlockSpec((FOX_T, LANES), lambda p, t, qx, kx: (kx[t], p))
    vspec = pl.BlockSpec((FOX_T, LANES), lambda p, t, qx, kx: (kx[t], 6 * N_PAIRS + p))
    ckspec = pl.BlockSpec((1, 8, FOX_T), lambda p, t, qx, kx: (p, 0, kx[t]))
    seq = pl.BlockSpec((S, LANES), lambda p, t, qx, kx: (0, p))
    full = jax.ShapeDtypeStruct((S, MIX_HALF), F32)
    grid_spec = pltpu.PrefetchScalarGridSpec(
        num_scalar_prefetch=2, grid=(N_PAIRS, ntri), in_specs=[qspec, qspec, kspec, vspec, qspec, ckspec, qspec, qspec],
        out_specs=[seq, seq, kspec, kspec, kspec],
        scratch_shapes=[pltpu.VMEM((2, S, LANES), F32), pltpu.VMEM((2, FOX_T, LANES), F32), pltpu.VMEM((FOX_T, LANES), F32)])
    return pl.pallas_call(body, name=name, grid_spec=grid_spec, out_shape=[full] * 5,
                          compiler_params=_cparams("parallel", "arbitrary"))(qidx, kidx, do, qn, kn, proj, cum_b, cum_tp, lse_b, delta_b)


def _ff_bwd(rc, f_t, name):
    def body(rc_ref, f_ref, d_ref, s_ref):
        d = rc_ref[...] * (1.0 - _sigmoid(f_ref[...]))
        d_ref[...] = d
        s_ref[...] = jnp.sum(d, axis=1, keepdims=True)

    return pl.pallas_call(body, name=name, out_shape=[jax.ShapeDtypeStruct(rc.shape, F32), jax.ShapeDtypeStruct((rc.shape[0], 1), F32)])(rc, f_t)


def _fox_block_fwd(proj, b_f, q_gain, k_gain, bavg, pfx):
    S = proj.shape[0]

    def prep(qv, kv, ba, qg, kg):
        return _head_rms(qv, ba)[0] * qg * (HEAD_DIM ** -0.5), _head_rms(kv, ba)[0] * kg

    qn, kn = _ew(prep, pfx + "_prep", [(proj, MIX_HALF, 4), (proj, MIX_HALF, 5)], consts=[bavg, q_gain, k_gain],
                 outs=[(MIX_HALF, BF16), (MIX_HALF, BF16)])
    f0 = 7 * MIX_HALF + GLA_RANK
    f_t = proj[:, f0:f0 + FOX_HEADS].T + b_f.reshape(FOX_HEADS, 1)
    cum = _cum_lanes(f_t, pfx + "_cum", False, pre=_log_sigmoid)
    cum_b = jnp.repeat(cum.T, HEAD_DIM, axis=1)
    cum_tp = jnp.pad(cum.reshape(N_PAIRS, 2, S), ((0, 0), (0, 6), (0, 0)))
    o, lse_b = _fox_fwd(qn, kn, proj, cum_b, cum_tp, pfx + "_attn")
    return o, dict(qn=qn, kn=kn, f_t=f_t, cum_b=cum_b, cum_tp=cum_tp, o=o, lse_b=lse_b)


def _fox_block_bwd(do, proj, q_gain, k_gain, bavg, res, pfx):
    qn, kn, o = res["qn"], res["kn"], res["o"]
    S = proj.shape[0]
    delta_b, = _ew(lambda a, b, ba: _head_mean(a * b, ba) * float(HEAD_DIM), pfx + "_delta", [do, o], consts=[bavg], outs=[(MIX_HALF, F32)])
    args = (do, qn, kn, proj, res["cum_b"], res["cum_tp"], res["lse_b"], delta_b)
    dqn, dcq_b, dkn, dv, dck_b = _fox_bwd(*args, pfx + "_bwd")

    def prep_bwd(dq, dk, qv, kv, ba, qg, kg):
        nq, rq = _head_rms(qv, ba)
        nk, rk = _head_rms(kv, ba)
        return (_head_rms_bwd(dq * qg, nq, rq, ba), _head_rms_bwd(dk * kg, nk, rk, ba),
                jnp.sum(dq * nq, axis=0, keepdims=True), jnp.sum(dk * nk, axis=0, keepdims=True))

    dfq, dfk, dqg, dkg = _ew(prep_bwd, pfx + "_prep_bwd", [dqn, dkn, (proj, MIX_HALF, 4), (proj, MIX_HALF, 5)],
                             consts=[bavg, q_gain, k_gain], outs=[(MIX_HALF, F32), (MIX_HALF, F32)], sums=[MIX_HALF, MIX_HALF])
    dcum = (dcq_b - dck_b)[:, ::HEAD_DIM].T
    rc = _cum_lanes(dcum, pfx + "_rcum", True)
    dff_t, db_f = _ff_bwd(rc, res["f_t"], pfx + "_ff_bwd")
    grads = dict(b_f=db_f.reshape(-1), q_gain=dqg.reshape(-1, HEAD_DIM), k_gain=dkg.reshape(-1, HEAD_DIM))
    return (dfq, dfk, dv, dff_t.T), grads


WEIGHTS = ['ada_w', 'ada_b', 'even_w_in', 'even_w_out', 'gla_w_lr', 'gla_b_lr', 'gla_gain', 'fox_b_f', 'fox_q_gain', 'fox_k_gain',
           'odd_w_in', 'odd_w_out', 's5_lam_re', 's5_lam_im', 's5_log_dt', 's5_b_re', 's5_b_im', 's5_c_re', 's5_c_im', 's5_d',
           's5_w_glu', 's5_b_glu', 'sgu_ln_gain', 'sgu_ln_bias', 'sgu_w_s', 'sgu_b_s', 'mlp_w1', 'mlp_w2']
ARGS = ['x', 'c'] + WEIGHTS + ['loss_target'] + ['m_' + w for w in WEIGHTS] + ['v_' + w for w in WEIGHTS]

EVEN_COLS = 3608
EVEN_PAD = 8 * MIX_HALF
MOD = 6 * D_MODEL
MOD_SHARD = MOD // N_CHIPS

PACK_COLS = 1024
PACK_ROWS = 6144
EVEN_SHARD = EVEN_COLS // N_CHIPS
SHARDED = [("even_w_in", (1, 1024, PACK_COLS), 2), ("even_w_out", (1, 256, 1024), 1), ("odd_w_in", (1, 1024, 384), 2),
           ("odd_w_out", (1, 256, 1024), 1), ("mlp_w1", (2, 1024, 1024), 2), ("mlp_w2", (2, 1024, 1024), 1),
           ("gla_w_lr", (1, 16, 128), 2), ("s5_w_glu", (1, 128, 512), 1), ("s5_b_glu", (1, 128), 1),
           ("sgu_ln_gain", (1, 128), 1), ("sgu_ln_bias", (1, 128), 1)]
REPLICATED = [("gla_b_lr", (1, 512)), ("gla_gain", (1, 8, 64)), ("fox_b_f", (1, 8)), ("fox_q_gain", (1, 8, 64)),
              ("fox_k_gain", (1, 8, 64)), ("s5_lam_re", (1, 32, 64)), ("s5_lam_im", (1, 32, 64)), ("s5_log_dt", (1, 32)),
              ("s5_b_re", (1, 32, 64, 16)), ("s5_b_im", (1, 32, 64, 16)), ("s5_c_re", (1, 32, 16, 64)), ("s5_c_im", (1, 32, 16, 64)),
              ("s5_d", (1, 32, 16)), ("sgu_w_s", (1, 8, 128, 128)), ("sgu_b_s", (1, 8, 128))]
SMALL_ROWS = 512
BIG_ADAM = {"ada_w": (2048, 1536), "even_w_in": (1024, 902), "even_w_out": (256, 1024), "odd_w_in": (1024, 384),
            "odd_w_out": (256, 1024), "mlp_w1": (2048, 1024), "mlp_w2": (2048, 1024), "s5_w_glu": (128, 512)}


PACK_ALIGN = 16


def _piece_rows(shape):
    rows = -(-math.prod(shape) // PACK_COLS)
    return -(-rows // PACK_ALIGN) * PACK_ALIGN


def _to_rows(p, lead=()):
    n = math.prod(p.shape[len(lead):])
    rows = _piece_rows(p.shape[len(lead):])
    flat = p.reshape(lead + (n,))
    if rows * PACK_COLS != n:
        flat = jnp.pad(flat, [(0, 0)] * len(lead) + [(0, rows * PACK_COLS - n)])
    return flat.reshape(lead + (rows, PACK_COLS))


def _from_rows(x, r0, shape, lead=()):
    n = math.prod(shape)
    seg = lax.slice_in_dim(x, r0, r0 + _piece_rows(shape), axis=len(lead)).reshape(lead + (-1,))
    return lax.slice_in_dim(seg, 0, n, axis=len(lead)).reshape(lead + tuple(shape))


def _pack_rows(pieces, rows):
    x = jnp.concatenate([_to_rows(p) for p in pieces], axis=0)
    return jnp.pad(x, ((0, rows - x.shape[0]), (0, 0)))


def _unpack(x, specs):
    out, r0 = {}, 0
    for name, shape in specs:
        out[name] = _from_rows(x, r0, shape)
        r0 += _piece_rows(shape)
    return out


def _shards_to_full(x4):
    out, r0 = {}, 0
    for name, shape, axis in SHARDED:
        seg = _from_rows(x4, r0, shape, lead=(N_CHIPS,))
        out[name] = jnp.concatenate([seg[k] for k in range(N_CHIPS)], axis=axis)
        r0 += _piece_rows(shape)
    return out


def _full_to_shards(full):
    blocks = [_to_rows(jnp.stack(jnp.split(full[name], N_CHIPS, axis=axis)), lead=(N_CHIPS,)) for name, _, axis in SHARDED]
    x = jnp.concatenate(blocks, axis=1)
    return jnp.pad(x, ((0, 0), (0, PACK_ROWS - x.shape[1]), (0, 0)))


def _relu2(t):
    r = jnp.maximum(t, 0.0)
    return r * r


def _silu(t):
    return t * _sigmoid(t)


def _pack_even(w):
    return jnp.concatenate([w[:, :2048], w[:, 2064:3600], w[:, 2048:2064], w[:, 3600:3608],
                            jnp.zeros((w.shape[0], EVEN_PAD - EVEN_COLS), w.dtype)], axis=1)


def _unpack_even(wp):
    return jnp.concatenate([wp[:, :2048], wp[:, 3584:3600], wp[:, 2048:3584], wp[:, 3600:3608]], axis=1)


def _mlp_fwd(h, w1, w2, pfx):
    pre = _mm(h, w1, "nn", pfx + "_up", out_dtype=BF16)
    return pre, _mm(pre, w2, "nn", pfx + "_down", a_pro=_relu2)


def _mlp_bwd(dm, h, pre, w1, w2, pfx):
    dpre = _mm(dm, w2, "nt", pfx + "_dpre", epi=lambda acc, p: acc * (2.0 * jnp.maximum(p, 0.0)), extras=[(pre, "mn")], out_dtype=BF16)
    dw2 = _mm(pre, dm, "tn", pfx + "_dw2", a_pro=_relu2)
    dw1 = _mm(h, dpre, "tn", pfx + "_dw1")
    dh = _mm(dpre, w1, "nt", pfx + "_dh")
    return dh, dw1, dw2


def _step(args):
    a = dict(zip(ARGS, args, strict=True))
    x0 = a["x"][0]
    target = a["loss_target"][0]
    mx, my, mc = lax.axis_index("x"), lax.axis_index("y"), lax.axis_index("c")
    chip = 2 * mx + my
    dev = 2 * chip + mc
    bavg = _head_avg_matrix(MIX_HALF)

    c_all = _gather8(jnp.pad(a["c"], ((0, 7), (0, 0))), "c_gather")[:, :, 0, :].reshape(2 * N_CHIPS, D_MODEL)
    ada_b_shard = lax.dynamic_slice_in_dim(a["ada_b"], chip * MOD_SHARD, MOD_SHARD, axis=1)
    mod_sh = [_mm(c_all, a["ada_w"][l], "nn", f"mod{l}", a_pro=_silu, epi=lambda acc, b: acc + b, extras=[(ada_b_shard[l:l + 1], "n")])
              for l in range(2)]
    small3 = jnp.zeros((8, MOD_SHARD), F32)
    for r, n in enumerate(("s5_b_glu", "sgu_ln_gain", "sgu_ln_bias")):
        small3 = small3.at[r, :LANES].set(a[n][0])
    mod_all = _chip_exchange(jnp.concatenate(mod_sh + [small3]), "mod_gather", True)
    mods = []
    for l in range(2):
        full = mod_all[:, 8 * l:8 * l + 8].transpose(1, 0, 2).reshape(8, MOD)
        mods.append(jnp.split(lax.dynamic_slice_in_dim(full, dev, 1, axis=0), 6, axis=1))
    b_glu, ln_gain, ln_bias = (mod_all[:, 16 + r, :LANES].reshape(1, MIX_HALF) for r in range(3))

    local = dict(a, even_w_in=jnp.pad(a["even_w_in"], ((0, 0), (0, 0), (0, PACK_COLS - EVEN_SHARD))))
    shard = _pack_rows([local[n] for n, _, _ in SHARDED], PACK_ROWS).astype(BF16)
    half = lax.dynamic_slice_in_dim(shard, mc * (PACK_ROWS // 2), PACK_ROWS // 2, axis=0)
    collected = _chip_exchange(half, "w_chips", True)
    halves = _by_core(collected, _pair_swap(collected, "w_pair"))
    w = _shards_to_full(halves.transpose(1, 0, 2, 3).reshape(N_CHIPS, PACK_ROWS, PACK_COLS))
    w_even = _pack_even(w["even_w_in"][0].reshape(D_MODEL, N_CHIPS, PACK_COLS)[:, :, :EVEN_SHARD].reshape(D_MODEL, EVEN_COLS))
    w_lr_pad = jnp.zeros((MIX_HALF, MIX_HALF), BF16).at[:GLA_RANK].set(w["gla_w_lr"][0])
    gla_b_lr = a["gla_b_lr"]
    gla_gain, q_gain, k_gain = (a[n].reshape(1, MIX_HALF) for n in ("gla_gain", "fox_q_gain", "fox_k_gain"))
    s5w = dict(lam_re=a["s5_lam_re"][0], lam_im=a["s5_lam_im"][0], log_dt=a["s5_log_dt"][0], b_re=a["s5_b_re"][0], b_im=a["s5_b_im"][0],
               c_re=a["s5_c_re"][0], c_im=a["s5_c_im"][0], d=a["s5_d"][0], w_glu=w["s5_w_glu"][0], b_glu=b_glu)
    sgu_wm, sgu_bt = _sgu_tables(a["sgu_w_s"][0], a["sgu_b_s"][0])

    sh1, sc1, g1, sh2, sc2, g2 = mods[0]
    _, h1_0 = _res_rms(x0, sc1, sh1, "l0_norm1")
    proj0 = _mm(h1_0, w_even, "nn", "l0_proj")
    og, gla_res = _gla_block_fwd(proj0, w_lr_pad, gla_b_lr, gla_gain, bavg, "gla")
    of, fox_res = _fox_block_fwd(proj0, a["fox_b_f"][0], q_gain, k_gain, bavg, "fox")
    mixed0 = jnp.concatenate([og, of], axis=1).astype(BF16)
    y0 = _mm(mixed0, w["even_w_out"][0], "nn", "l0_out")
    x1, h2_0 = _res_rms(x0, sc2, sh2, "l0_norm2", y=y0, g=g1)
    pre0, m0 = _mlp_fwd(h2_0, w["mlp_w1"][0], w["mlp_w2"][0], "l0_mlp")
    sh1b, sc1b, g1b, sh2b, sc2b, g2b = mods[1]
    x2, h1_1 = _res_rms(x1, sc1b, sh1b, "l1_norm1", y=m0, g=g2)
    proj1 = _mm(h1_1, w["odd_w_in"][0], "nn", "l1_proj")
    ys5, s5_res = _s5_block_fwd(proj1[:, :MIX_HALF], s5w, "s5")
    ysgu = _sgu_fwd(proj1, ln_gain, ln_bias, sgu_wm, sgu_bt, "sgu")
    mixed1 = jnp.concatenate([ys5, ysgu], axis=1).astype(BF16)
    y1 = _mm(mixed1, w["odd_w_out"][0], "nn", "l1_out")
    x3, h2_1 = _res_rms(x2, sc2b, sh2b, "l1_norm2", y=y1, g=g1b)
    pre1, m1 = _mlp_fwd(h2_1, w["mlp_w1"][1], w["mlp_w2"][1], "l1_mlp")
    loss_b, dx4, dm1, dg2b = _res_loss(x3, m1, g2b, target, "loss")
    loss = lax.psum(loss_b[0, 0], ("x", "y", "c"))

    full = {}
    dh2_1, dw1_1, dw2_1 = _mlp_bwd(dm1, h2_1, pre1, w["mlp_w1"][1], w["mlp_w2"][1], "l1_mlp")
    dx3, dy1, dg1b, dsc2b, dsh2b = _res_rms_bwd(x3, dh2_1, sc2b, dx4, "l1_norm2_bwd", y=y1, g=g1b)
    dmixed1 = _mm(dy1, w["odd_w_out"][0], "nt", "l1_out_dx")
    full["odd_w_out"] = _mm(mixed1, dy1, "tn", "l1_out_dw")[None]
    du, s5g = _s5_block_bwd(dmixed1[:, :MIX_HALF], s5w, s5_res, "s5")
    dzu, dzv, dws, dbt, dlg, dlb = _sgu_bwd(dmixed1[:, MIX_HALF:], proj1, ln_gain, ln_bias, sgu_wm, sgu_bt, "sgu_bwd")
    g_ws, g_bs = _sgu_grads(dws, dbt)
    dproj1 = jnp.concatenate([du, dzu, dzv], axis=1).astype(BF16)
    full["odd_w_in"] = _mm(h1_1, dproj1, "tn", "l1_proj_dw")[None]
    dh1_1 = _mm(dproj1, w["odd_w_in"][0], "nt", "l1_proj_dx")
    dx2, dm0, dg2, dsc1b, dsh1b = _res_rms_bwd(x2, dh1_1, sc1b, dx3, "l1_norm1_bwd", y=m0, g=g2)
    dh2_0, dw1_0, dw2_0 = _mlp_bwd(dm0, h2_0, pre0, w["mlp_w1"][0], w["mlp_w2"][0], "l0_mlp")
    full["mlp_w1"] = jnp.stack([dw1_0, dw1_1])
    full["mlp_w2"] = jnp.stack([dw2_0, dw2_1])
    dx1, dy0, dg1, dsc2, dsh2 = _res_rms_bwd(x1, dh2_0, sc2, dx2, "l0_norm2_bwd", y=y0, g=g1)
    dmixed0 = _mm(dy0, w["even_w_out"][0], "nt", "l0_out_dx")
    full["even_w_out"] = _mm(mixed0, dy0, "tn", "l0_out_dw")[None]
    (dgq, dgk, dgv, dgg, dsmall), glag = _gla_block_bwd(dmixed0[:, :MIX_HALF], proj0, w_lr_pad, gla_gain, bavg, gla_res, "gla")
    (dfq, dfk, dfv, dff), foxg = _fox_block_bwd(dmixed0[:, MIX_HALF:], proj0, q_gain, k_gain, bavg, fox_res, "fox")
    dsmall = lax.dynamic_update_slice(dsmall, dff, (0, GLA_RANK))
    dproj0 = jnp.concatenate([dgq, dgk, dgv, dgg, dfq, dfk, dfv, dsmall], axis=1).astype(BF16)
    d_even = _unpack_even(_mm(h1_0, dproj0, "tn", "l0_proj_dw")).reshape(D_MODEL, N_CHIPS, EVEN_SHARD)
    full["even_w_in"] = jnp.pad(d_even, ((0, 0), (0, 0), (0, PACK_COLS - EVEN_SHARD))).reshape(1, D_MODEL, N_CHIPS * PACK_COLS)
    dh1_0 = _mm(dproj0, w_even, "nt", "l0_proj_dx")
    grad_x, dsc1, dsh1 = _res_rms_bwd(x0, dh1_0, sc1, dx1, "l0_norm1_bwd")
    full["gla_w_lr"] = glag["w_lr"][None]
    full["s5_w_glu"] = s5g["w_glu"][None]
    full["s5_b_glu"] = s5g["b_glu"][None]
    full["sgu_ln_gain"] = dlg
    full["sgu_ln_bias"] = dlb

    dmod = jnp.concatenate([dsh1, dsc1, dg1, dsh2, dsc2, dg2, dsh1b, dsc1b, dg1b, dsh2b, dsc2b, dg2b], axis=1)
    dmod_all = _gather8(jnp.pad(dmod, ((0, 7), (0, 0))), "dmod_gather")[:, :, 0, :].reshape(2 * N_CHIPS, 2, MOD)
    grads = {}
    grads["ada_w"] = jnp.stack([
        _mm(c_all, lax.dynamic_slice_in_dim(dmod_all[:, l], chip * MOD_SHARD, MOD_SHARD, axis=1), "tn", f"ada_dw{l}", a_pro=_silu)
        for l in range(2)])
    grads["ada_b"] = _sum_slots(dmod_all.reshape(2 * N_CHIPS, 2 * MOD // MIX_HALF, MIX_HALF), "ada_db").reshape(2, MOD)

    packed = _full_to_shards(full)
    hr = PACK_ROWS // 2
    mine = lax.dynamic_slice_in_dim(packed, mc * hr, hr, axis=1)
    other = lax.dynamic_slice_in_dim(packed, (1 - mc) * hr, hr, axis=1)
    theirs = _pair_swap(other, "g_pair")
    pair_sum, = _ew(lambda p, q: p + q, "g_pair_sum", [mine.reshape(N_CHIPS * hr, PACK_COLS), theirs.reshape(N_CHIPS * hr, PACK_COLS)],
                    outs=[(PACK_COLS, BF16)])
    arrived = _chip_exchange(pair_sum.reshape(N_CHIPS, hr, PACK_COLS), "g_chips", False)
    red_half = _sum_slots(arrived, "g_chip_sum")
    reduced = _by_core(red_half, _pair_swap(red_half, "g_pair_out")).reshape(PACK_ROWS, PACK_COLS)
    grads.update(_unpack(reduced, [(n, s) for n, s, _ in SHARDED]))
    grads["even_w_in"] = grads["even_w_in"][:, :, :EVEN_SHARD]

    part = dict(gla_b_lr=glag["b_lr"], gla_gain=glag["gain"], fox_b_f=foxg["b_f"], fox_q_gain=foxg["q_gain"], fox_k_gain=foxg["k_gain"],
                s5_lam_re=s5g["lam_re"], s5_lam_im=s5g["lam_im"], s5_log_dt=s5g["log_dt"], s5_b_re=s5g["b_re"], s5_b_im=s5g["b_im"],
                s5_c_re=s5g["c_re"], s5_c_im=s5g["c_im"], s5_d=s5g["d"], sgu_w_s=g_ws, sgu_b_s=g_bs)
    parts_all = _gather8(_pack_rows([part[n] for n, _ in REPLICATED], SMALL_ROWS).astype(BF16), "rep_gather")
    rep = _sum_slots(parts_all.reshape(2 * N_CHIPS, SMALL_ROWS, PACK_COLS), "rep_sum")
    grads.update(_unpack(rep, REPLICATED))

    delta, new_m, new_v = {}, {}, {}
    for n, shape2 in BIG_ADAM.items():
        d, nm, nv = _adamw(a[n].reshape(shape2), grads[n].reshape(shape2), a["m_" + n].reshape(shape2), a["v_" + n].reshape(shape2), "adamw_" + n)
        delta[n], new_m[n], new_v[n] = (t.reshape(a[n].shape) for t in (d, nm, nv))
    small = [n for n in WEIGHTS if n not in BIG_ADAM]
    spec = [(n, a[n].shape) for n in small]
    packs = [_pack_rows([src[n] for n in small], SMALL_ROWS) for src in
             (a, grads, {n: a["m_" + n] for n in small}, {n: a["v_" + n] for n in small})]
    for tgt, res in zip((delta, new_m, new_v), _adamw(*packs, "adamw_small")):
        tgt.update(_unpack(res, spec))
    outs = [loss, grad_x[None]]
    for group in (grads, delta, new_m, new_v):
        outs += [group[n].reshape(a[n].shape) for n in WEIGHTS]
    return tuple(outs)


def kernel(x, c, ada_w, ada_b, even_w_in, even_w_out, gla_w_lr, gla_b_lr, gla_gain, fox_b_f, fox_q_gain, fox_k_gain, odd_w_in,
           odd_w_out, s5_lam_re, s5_lam_im, s5_log_dt, s5_b_re, s5_b_im, s5_c_re, s5_c_im, s5_d, s5_w_glu, s5_b_glu, sgu_ln_gain,
           sgu_ln_bias, sgu_w_s, sgu_b_s, mlp_w1, mlp_w2, loss_target, m_ada_w, m_ada_b, m_even_w_in, m_even_w_out, m_gla_w_lr,
           m_gla_b_lr, m_gla_gain, m_fox_b_f, m_fox_q_gain, m_fox_k_gain, m_odd_w_in, m_odd_w_out, m_s5_lam_re, m_s5_lam_im,
           m_s5_log_dt, m_s5_b_re, m_s5_b_im, m_s5_c_re, m_s5_c_im, m_s5_d, m_s5_w_glu, m_s5_b_glu, m_sgu_ln_gain, m_sgu_ln_bias,
           m_sgu_w_s, m_sgu_b_s, m_mlp_w1, m_mlp_w2, v_ada_w, v_ada_b, v_even_w_in, v_even_w_out, v_gla_w_lr, v_gla_b_lr,
           v_gla_gain, v_fox_b_f, v_fox_q_gain, v_fox_k_gain, v_odd_w_in, v_odd_w_out, v_s5_lam_re, v_s5_lam_im, v_s5_log_dt,
           v_s5_b_re, v_s5_b_im, v_s5_c_re, v_s5_c_im, v_s5_d, v_s5_w_glu, v_s5_b_glu, v_sgu_ln_gain, v_sgu_ln_bias, v_sgu_w_s,
           v_sgu_b_s, v_mlp_w1, v_mlp_w2):
    return _step((x, c, ada_w, ada_b, even_w_in, even_w_out, gla_w_lr, gla_b_lr, gla_gain, fox_b_f, fox_q_gain, fox_k_gain,
                  odd_w_in, odd_w_out, s5_lam_re, s5_lam_im, s5_log_dt, s5_b_re, s5_b_im, s5_c_re, s5_c_im, s5_d, s5_w_glu,
                  s5_b_glu, sgu_ln_gain, sgu_ln_bias, sgu_w_s, sgu_b_s, mlp_w1, mlp_w2, loss_target, m_ada_w, m_ada_b,
                  m_even_w_in, m_even_w_out, m_gla_w_lr, m_gla_b_lr, m_gla_gain, m_fox_b_f, m_fox_q_gain, m_fox_k_gain,
                  m_odd_w_in, m_odd_w_out, m_s5_lam_re, m_s5_lam_im, m_s5_log_dt, m_s5_b_re, m_s5_b_im, m_s5_c_re, m_s5_c_im,
                  m_s5_d, m_s5_w_glu, m_s5_b_glu, m_sgu_ln_gain, m_sgu_ln_bias, m_sgu_w_s, m_sgu_b_s, m_mlp_w1, m_mlp_w2, v_ada_w,
                  v_ada_b, v_even_w_in, v_even_w_out, v_gla_w_lr, v_gla_b_lr, v_gla_gain, v_fox_b_f, v_fox_q_gain, v_fox_k_gain,
                  v_odd_w_in, v_odd_w_out, v_s5_lam_re, v_s5_lam_im, v_s5_log_dt, v_s5_b_re, v_s5_b_im, v_s5_c_re, v_s5_c_im,
                  v_s5_d, v_s5_w_glu, v_s5_b_glu, v_sgu_ln_gain, v_sgu_ln_bias, v_sgu_w_s, v_sgu_b_s, v_mlp_w1, v_mlp_w2))
```

```python
import functools
import math

import jax
import jax.numpy as jnp
import numpy as np
from jax import lax
from jax.experimental import pallas as pl
from jax.experimental.pallas import tpu as pltpu

F32 = jnp.float32
BF16 = jnp.bfloat16
MESH = pl.DeviceIdType.MESH
ANY = pl.BlockSpec(memory_space=pl.ANY)
DMA_SEM = pltpu.SemaphoreType.DMA

D_MODEL = 1024
HEAD_DIM = 64
MIX_HALF = 512
GLA_RANK = 16
GLA_TAU = 16.0
GLA_CHUNK = 64
S5_GROUPS = 32
S5_GROUP_WIDTH = 16
S5_STATE = 64
S5_N = S5_GROUPS * S5_STATE
SGU_GROUPS = 8
SGU_CHUNK = 128
D_FF = 4096
EPS = 1e-6
N_CHIPS = 4
LANES = 128
VMEM_LIMIT = 48 * 1024 * 1024
PAIR_COPIES = 16

ADAM_LR = 0.001
ADAM_B1 = 0.9
ADAM_B2 = 0.999
ADAM_EPS = 1e-08
ADAM_WD = 0.01
ADAM_STEP = 10


def _cparams(*sem):
    return pltpu.CompilerParams(dimension_semantics=sem, vmem_limit_bytes=VMEM_LIMIT)


def _pair_swap(x, name):
    lead = x.shape[:-2]
    rows = x.shape[-2]
    nsplit = max(1, PAIR_COPIES // max(1, math.prod(lead)))
    while nsplit > 1 and rows % (nsplit * 16):
        nsplit -= 1
    pieces = [idx + (pl.ds(j * (rows // nsplit), rows // nsplit),) for idx in np.ndindex(*lead) for j in range(nsplit)]

    def body(x_ref, o_ref, send_sems, recv_sems):
        mx, my, mc = lax.axis_index("x"), lax.axis_index("y"), lax.axis_index("c")
        copies = [pltpu.make_async_remote_copy(src_ref=x_ref.at[p], dst_ref=o_ref.at[p], send_sem=send_sems.at[j], recv_sem=recv_sems.at[j],
                                               device_id=(mx, my, 1 - mc), device_id_type=MESH) for j, p in enumerate(pieces)]
        for cp in copies:
            cp.start()
        for cp in copies:
            cp.wait_recv()
        for cp in copies:
            cp.wait_send()

    return pl.pallas_call(
        body, name=name, out_shape=jax.ShapeDtypeStruct(x.shape, x.dtype), in_specs=[ANY], out_specs=ANY,
        scratch_shapes=[DMA_SEM((len(pieces),)), DMA_SEM((len(pieces),))])(x)


def _by_core(mine, theirs):
    first = lax.axis_index("c") == 0
    return jnp.stack([jnp.where(first, mine, theirs), jnp.where(first, theirs, mine)])


def _chip_exchange(x, name, bcast):
    blk = x.shape if bcast else x.shape[1:]

    def body(x_ref, o_ref, send_sems, recv_sems, loc_sem):
        mx, my, mc = lax.axis_index("x"), lax.axis_index("y"), lax.axis_index("c")
        me = 2 * mx + my
        peers = [(1 - mx, my), (mx, 1 - my), (1 - mx, 1 - my)]

        def src(k):
            return x_ref if bcast else x_ref.at[k]

        loc = pltpu.make_async_copy(src(me), o_ref.at[me], loc_sem)
        loc.start()
        sends = []
        for j, (px, py) in enumerate(peers):
            cp = pltpu.make_async_remote_copy(src_ref=src(2 * px + py), dst_ref=o_ref.at[me], send_sem=send_sems.at[j],
                                              recv_sem=recv_sems.at[j], device_id=(px, py, mc), device_id_type=MESH)
            cp.start()
            sends.append(cp)
        for j, (px, py) in enumerate(peers):
            pltpu.make_async_remote_copy(src_ref=src(me), dst_ref=o_ref.at[2 * px + py], send_sem=send_sems.at[j],
                                         recv_sem=recv_sems.at[j], device_id=(px, py, mc), device_id_type=MESH).wait_recv()
        for cp in sends:
            cp.wait_send()
        loc.wait()

    return pl.pallas_call(
        body, name=name, out_shape=jax.ShapeDtypeStruct((N_CHIPS,) + tuple(blk), x.dtype), in_specs=[ANY], out_specs=ANY,
        scratch_shapes=[DMA_SEM((3,)), DMA_SEM((3,)), DMA_SEM])(x)


def _gather8(x, name):
    collected = _chip_exchange(x, name + "_chips", True)
    return jnp.swapaxes(_by_core(collected, _pair_swap(collected, name + "_pair")), 0, 1)


def _tile(n, want):
    if n <= want:
        return n
    t = (want // LANES) * LANES
    while t >= LANES:
        if n % t == 0:
            return t
        t -= LANES
    raise ValueError(f"no lane-aligned tile for {n}")


_DIMS = {"nn": (((1,), (0,)), ((), ())), "nt": (((1,), (1,)), ((), ())), "tn": (((0,), (0,)), ((), ()))}


MM_FULL_K = 4096
MM_SLAB_K = 2048
MM_TILES = ((1024, 1024), (512, 1024), (1024, 512), (512, 512), (256, 512), (256, 256))
MM_VMEM_BUDGET = 36 * 1024 * 1024


def _mm(a, b, mode, name, *, a_pro=None, epi=None, extras=(), out_dtype=F32, tm_max=1024, tn_max=1024, tk=None, a_cols=None):
    c0, csize = a_cols if a_cols is not None else (0, a.shape[1])
    if mode == "tn":
        K, M = a.shape[0], csize
    else:
        M, K = a.shape[0], csize
    N = b.shape[0] if mode == "nt" else b.shape[1]
    assert (b.shape[1] if mode == "nt" else b.shape[0]) == K, (a.shape, b.shape, mode)
    if tk is None:
        tk = K if (mode != "tn" and K <= MM_FULL_K) else MM_SLAB_K
    tk = _tile(K, tk)
    nk = K // tk
    n_mn = sum(1 for _, kind in extras if kind == "mn")
    for tm_want, tn_want in MM_TILES:
        tm, tn = _tile(M, min(tm_want, tm_max)), _tile(N, min(tn_want, tn_max))
        need = 2 * (tm * tk * a.dtype.itemsize + tk * tn * b.dtype.itemsize + tm * tn * 4 * (1 + n_mn)) + tm * tn * 4 * (nk > 1)
        if need <= MM_VMEM_BUDGET:
            break
    if mode == "tn":
        assert c0 % tm == 0
        a_spec = pl.BlockSpec((tk, tm), lambda i, j, k: (k, i + c0 // tm))
    else:
        assert c0 % tk == 0
        a_spec = pl.BlockSpec((tm, tk), lambda i, j, k: (i, k + c0 // tk))
    b_spec = pl.BlockSpec((tn, tk), lambda i, j, k: (j, k)) if mode == "nt" else pl.BlockSpec((tk, tn), lambda i, j, k: (k, j))
    ex_specs = []
    for arr, kind in extras:
        if kind == "mn":
            assert arr.shape == (M, N)
            ex_specs.append(pl.BlockSpec((tm, tn), lambda i, j, k: (i, j)))
        else:
            assert arr.shape == (1, N)
            ex_specs.append(pl.BlockSpec((1, tn), lambda i, j, k: (0, j)))
    n_ex = len(extras)

    def body(*refs):
        a_ref, b_ref = refs[:2]
        ex_refs = refs[2:2 + n_ex]
        o_ref = refs[2 + n_ex]
        acc_ref = refs[3 + n_ex] if nk > 1 else None
        k = pl.program_id(2)
        av = a_ref[...]
        if a_pro is not None:
            av = a_pro(av)
        part = lax.dot_general(av.astype(BF16), b_ref[...].astype(BF16), _DIMS[mode], preferred_element_type=F32)
        if nk == 1:
            if epi is not None:
                part = epi(part, *[r[...] for r in ex_refs])
            o_ref[...] = part.astype(o_ref.dtype)
            return

        @pl.when(k == 0)
        def _():
            acc_ref[...] = part

        @pl.when(k > 0)
        def _():
            acc_ref[...] += part

        @pl.when(k == nk - 1)
        def _():
            acc = acc_ref[...]
            if epi is not None:
                acc = epi(acc, *[r[...] for r in ex_refs])
            o_ref[...] = acc.astype(o_ref.dtype)

    return pl.pallas_call(
        body, name=name, grid=(M // tm, N // tn, nk),
        in_specs=[a_spec, b_spec] + ex_specs,
        out_specs=pl.BlockSpec((tm, tn), lambda i, j, k: (i, j)),
        out_shape=jax.ShapeDtypeStruct((M, N), out_dtype),
        scratch_shapes=[pltpu.VMEM((tm, tn), F32)] if nk > 1 else [],
        compiler_params=_cparams("parallel", "parallel", "arbitrary"))(a, b, *[e[0] for e in extras])


ROWS = 256


def _row_spec(w, ts=ROWS):
    return pl.BlockSpec((ts, w), lambda i: (i, 0))


def _vec_spec(w):
    return pl.BlockSpec((1, w), lambda i: (0, 0))


def _res_rms(x, sc, sh, name, y=None, g=None):
    S, D = x.shape
    has_res = y is not None

    def body(*refs):
        if has_res:
            x_ref, y_ref, g_ref, sc_ref, sh_ref, xo_ref, h_ref = refs
            xv = x_ref[...] + g_ref[...] * y_ref[...]
            xo_ref[...] = xv
        else:
            x_ref, sc_ref, sh_ref, h_ref = refs
            xv = x_ref[...]
        r = lax.rsqrt(jnp.mean(xv * xv, axis=-1, keepdims=True) + EPS)
        h_ref[...] = (xv * r * (1.0 + sc_ref[...]) + sh_ref[...]).astype(BF16)

    row, vec = _row_spec(D), _vec_spec(D)
    if has_res:
        return pl.pallas_call(body, name=name, grid=(S // ROWS,), in_specs=[row, row, vec, vec, vec], out_specs=[row, row],
                              out_shape=[jax.ShapeDtypeStruct((S, D), F32), jax.ShapeDtypeStruct((S, D), BF16)],
                              compiler_params=_cparams("parallel"))(x, y, g, sc, sh)
    h = pl.pallas_call(body, name=name, grid=(S // ROWS,), in_specs=[row, vec, vec], out_specs=row,
                       out_shape=jax.ShapeDtypeStruct((S, D), BF16), compiler_params=_cparams("parallel"))(x, sc, sh)
    return x, h


def _res_rms_bwd(x, dh, sc, dres, name, y=None, g=None):
    S, D = x.shape
    has_res = y is not None

    def body(*refs):
        if has_res:
            x_ref, dh_ref, sc_ref, dres_ref, y_ref, g_ref, dx_ref, dy_ref, dg_ref, dsc_ref, dsh_ref = refs
        else:
            x_ref, dh_ref, sc_ref, dres_ref, dx_ref, dsc_ref, dsh_ref = refs
        first = pl.program_id(0) == 0
        xv = x_ref[...]
        dh = dh_ref[...]
        r = lax.rsqrt(jnp.mean(xv * xv, axis=-1, keepdims=True) + EPS)
        xn = xv * r
        dxn = dh * (1.0 + sc_ref[...])
        dx = dres_ref[...] + r * (dxn - xn * jnp.mean(dxn * xn, axis=-1, keepdims=True))
        dx_ref[...] = dx
        parts = [(dsc_ref, jnp.sum(dh * xn, axis=0, keepdims=True)), (dsh_ref, jnp.sum(dh, axis=0, keepdims=True))]
        if has_res:
            dy_ref[...] = (dx * g_ref[...]).astype(BF16)
            parts.append((dg_ref, jnp.sum(dx * y_ref[...], axis=0, keepdims=True)))
        for ref, val in parts:
            @pl.when(first)
            def _(ref=ref, val=val):
                ref[...] = val

            @pl.when(jnp.logical_not(first))
            def _(ref=ref, val=val):
                ref[...] += val

    row, vec = _row_spec(D), _vec_spec(D)
    full = jax.ShapeDtypeStruct((S, D), F32)
    v = jax.ShapeDtypeStruct((1, D), F32)
    if has_res:
        return pl.pallas_call(body, name=name, grid=(S // ROWS,), in_specs=[row, row, vec, row, row, vec],
                              out_specs=[row, row, vec, vec, vec], out_shape=[full, jax.ShapeDtypeStruct((S, D), BF16), v, v, v],
                              compiler_params=_cparams("arbitrary"))(x, dh, sc, dres, y, g)
    return pl.pallas_call(body, name=name, grid=(S // ROWS,), in_specs=[row, row, vec, row],
                          out_specs=[row, vec, vec], out_shape=[full, v, v],
                          compiler_params=_cparams("arbitrary"))(x, dh, sc, dres)


def _res_loss(x, m, g, target, name):
    S, D = x.shape

    def body(x_ref, m_ref, g_ref, t_ref, loss_ref, dx_ref, dm_ref, dg_ref):
        first = pl.program_id(0) == 0
        mv = m_ref[...]
        err = x_ref[...] + g_ref[...] * mv - t_ref[...]
        dx = err * (1.0 / D)
        dx_ref[...] = dx
        dm_ref[...] = (dx * g_ref[...]).astype(BF16)
        part = 0.5 * jnp.sum(jnp.mean(err * err, axis=-1, keepdims=True), axis=0, keepdims=True)
        dg = jnp.sum(dx * mv, axis=0, keepdims=True)

        @pl.when(first)
        def _():
            loss_ref[...] = jnp.broadcast_to(part, loss_ref.shape)
            dg_ref[...] = dg

        @pl.when(jnp.logical_not(first))
        def _():
            loss_ref[...] += jnp.broadcast_to(part, loss_ref.shape)
            dg_ref[...] += dg

    row, vec = _row_spec(D), _vec_spec(D)
    full = jax.ShapeDtypeStruct((S, D), F32)
    return pl.pallas_call(body, name=name, grid=(S // ROWS,), in_specs=[row, row, vec, row],
                          out_specs=[pl.BlockSpec((8, LANES), lambda i: (0, 0)), row, row, vec],
                          out_shape=[jax.ShapeDtypeStruct((8, LANES), F32), full, jax.ShapeDtypeStruct((S, D), BF16), jax.ShapeDtypeStruct((1, D), F32)],
                          compiler_params=_cparams("arbitrary"))(x, m, g, target)


def _adamw(w, g, m, v, name):
    R, C = w.shape
    tr = R if R <= 256 else 256
    assert R % tr == 0

    def body(w_ref, g_ref, m_ref, v_ref, d_ref, nm_ref, nv_ref):
        gv = g_ref[...]
        nm = ADAM_B1 * m_ref[...] + (1.0 - ADAM_B1) * gv
        nv = ADAM_B2 * v_ref[...] + (1.0 - ADAM_B2) * jnp.square(gv)
        m_hat = nm / (1.0 - ADAM_B1 ** ADAM_STEP)
        v_hat = nv / (1.0 - ADAM_B2 ** ADAM_STEP)
        d_ref[...] = -ADAM_LR * (m_hat / (jnp.sqrt(v_hat) + ADAM_EPS) + ADAM_WD * w_ref[...])
        nm_ref[...] = nm
        nv_ref[...] = nv

    spec = pl.BlockSpec((tr, C), lambda i: (i, 0))
    out = jax.ShapeDtypeStruct((R, C), F32)
    return pl.pallas_call(body, name=name, grid=(R // tr,), in_specs=[spec] * 4, out_specs=[spec] * 3,
                          out_shape=[out, out, out], compiler_params=_cparams("parallel"))(w, g, m, v)


def _sum_slots(x, name):
    n, R, C = x.shape
    tr = R if R <= 256 else 256
    assert R % tr == 0

    def body(x_ref, o_ref):
        acc = x_ref[0].astype(F32)
        for j in range(1, n):
            acc = acc + x_ref[j].astype(F32)
        o_ref[...] = acc

    return pl.pallas_call(body, name=name, grid=(R // tr,), in_specs=[pl.BlockSpec((n, tr, C), lambda i: (0, i, 0))],
                          out_specs=pl.BlockSpec((tr, C), lambda i: (i, 0)), out_shape=jax.ShapeDtypeStruct((R, C), F32),
                          compiler_params=_cparams("parallel"))(x)


def _ew(fn, name, tiled, consts=(), outs=(), sums=(), ts=ROWS):
    tiled = [t if isinstance(t, tuple) else (t, t.shape[1], 0) for t in tiled]
    S = tiled[0][0].shape[0]
    n_t, n_c, n_o, n_s = len(tiled), len(consts), len(outs), len(sums)

    def body(*refs):
        ins = [r[...] for r in refs[:n_t + n_c]]
        res = fn(*ins)
        res = res if isinstance(res, (tuple, list)) else (res,)
        assert len(res) == n_o + n_s
        o_refs = refs[n_t + n_c:]
        for r, val in zip(o_refs[:n_o], res[:n_o]):
            r[...] = val.astype(r.dtype)
        first = pl.program_id(0) == 0
        for r, val in zip(o_refs[n_o:], res[n_o:]):
            @pl.when(first)
            def _(r=r, val=val):
                r[...] = val

            @pl.when(jnp.logical_not(first))
            def _(r=r, val=val):
                r[...] += val

    in_specs = [pl.BlockSpec((ts, w), lambda i, cb=cb: (i, cb)) for _, w, cb in tiled]
    in_specs += [pl.BlockSpec(c.shape, lambda i, nd=c.ndim: (0,) * nd) for c in consts]
    out_specs = [_row_spec(w, ts) for w, _ in outs] + [_vec_spec(w) for w in sums]
    out_shape = [jax.ShapeDtypeStruct((S, w), dt) for w, dt in outs] + [jax.ShapeDtypeStruct((1, w), F32) for w in sums]
    res = pl.pallas_call(body, name=name, grid=(S // ts,), in_specs=in_specs, out_specs=out_specs, out_shape=out_shape,
                         compiler_params=_cparams("arbitrary" if sums else "parallel"))(*[t[0] for t in tiled], *consts)
    return res


_GELU_C = math.sqrt(2.0 / math.pi)


def _gelu(x):
    return 0.5 * x * (1.0 + jnp.tanh(_GELU_C * (x + 0.044715 * x * x * x)))


def _dgelu(x):
    t = jnp.tanh(_GELU_C * (x + 0.044715 * x * x * x))
    return 0.5 * (1.0 + t) + 0.5 * x * (1.0 - t * t) * _GELU_C * (1.0 + 3.0 * 0.044715 * x * x)


def _sigmoid(x):
    return 1.0 / (1.0 + jnp.exp(-x))


def _log_sigmoid(x):
    return jnp.minimum(x, 0.0) - jnp.log(1.0 + jnp.exp(-jnp.abs(x)))


SCAN_T = 128
SCAN_TB = 512


def _cmul(ar, ai, br, bi):
    return ar * br - ai * bi, ar * bi + ai * br


def _s5_discretise(lam_re, lam_im, log_dt, b_re, b_im):
    dt = jnp.exp(log_dt)[:, None]
    mag = jnp.exp(lam_re * dt)
    ang = lam_im * dt
    abar_re = mag * jnp.cos(ang)
    abar_im = mag * jnp.sin(ang)
    den = lam_re * lam_re + lam_im * lam_im
    coef_re = ((abar_re - 1.0) * lam_re + abar_im * lam_im) / den
    coef_im = (abar_im * lam_re - (abar_re - 1.0) * lam_im) / den
    bbar_re = coef_re[..., None] * b_re - coef_im[..., None] * b_im
    bbar_im = coef_re[..., None] * b_im + coef_im[..., None] * b_re
    return abar_re, abar_im, bbar_re, bbar_im


def _planes(re, im):
    lead = re.shape[:-1]
    return jnp.stack([re.reshape(lead + (-1, LANES)), im.reshape(lead + (-1, LANES))], axis=-2).reshape(lead + (-1,))


def _unplanes(x):
    lead = x.shape[:-1]
    x4 = x.reshape(lead + (-1, 2, LANES))
    return x4[..., 0, :].reshape(lead + (-1,)), x4[..., 1, :].reshape(lead + (-1,))


def _s5_scan_tables(a_re, a_im, reverse):
    pr, pi = [a_re], [a_im]
    for _ in range(7):
        r, i = _cmul(pr[-1], pi[-1], pr[-1], pi[-1])
        pr.append(r)
        pi.append(i)
    apow = _planes(jnp.stack(pr), jnp.stack(pi))
    n = np.arange(1, SCAN_T + 1)
    if reverse:
        n = n[::-1]
    tr = jnp.ones((SCAN_T, a_re.shape[0]), F32)
    ti = jnp.zeros((SCAN_T, a_re.shape[0]), F32)
    for k in range(8):
        bit = jnp.asarray(((n >> k) & 1).astype(np.float32))[:, None]
        mr = bit * pr[k][None, :] + (1.0 - bit)
        mi = bit * pi[k][None, :]
        tr, ti = _cmul(tr, ti, mr, mi)
    return apow, _planes(tr, ti)


def _s5_scan(bu, apow, ptab, name, reverse, x_fwd=None):
    S, N2 = bu.shape
    T, W = SCAN_T, 2 * LANES
    tb = min(SCAN_TB, S)
    nt, nsub = S // tb, tb // T
    order = list(range(nsub - 1, -1, -1) if reverse else range(nsub))
    with_da = x_fwd is not None

    def tblk(t):
        return (nt - 1 - t) if reverse else t

    def shifted(v, k, rowi):
        s = 1 << k
        if reverse:
            return jnp.where(rowi < T - s, pltpu.roll(v, T - s, 0), 0.0)
        return jnp.where(rowi >= s, pltpu.roll(v, s, 0), 0.0)

    def body(*refs):
        if with_da:
            bu_ref, ap_ref, pt_ref, xf_ref, xp_ref, x_ref, da_ref, carry_ref = refs
        else:
            bu_ref, ap_ref, pt_ref, x_ref, carry_ref = refs
        t = pl.program_id(1)

        @pl.when(t == 0)
        def _():
            carry_ref[...] = jnp.zeros_like(carry_ref)
            if with_da:
                da_ref[...] = jnp.zeros_like(da_ref)

        rowi = lax.broadcasted_iota(jnp.int32, (T, LANES), 0)
        pr, pi = pt_ref[:, :LANES], pt_ref[:, LANES:]
        cr, ci = carry_ref[0:1, :LANES], carry_ref[0:1, LANES:]
        for sb in order:
            rows = pl.ds(sb * T, T)
            xr, xi = bu_ref[rows, :LANES], bu_ref[rows, LANES:]
            for k in range(7):
                ar, ai = ap_ref[k:k + 1, :LANES], ap_ref[k:k + 1, LANES:]
                s = 1 << k
                if s < 8:
                    rr, ri = shifted(xr, k, rowi), shifted(xi, k, rowi)
                    xr, xi = xr + ar * rr - ai * ri, xi + ar * ri + ai * rr
                elif reverse:
                    nr, ni = xr[s:], xi[s:]
                    xr = jnp.concatenate([xr[:T - s] + ar * nr - ai * ni, xr[T - s:]], axis=0)
                    xi = jnp.concatenate([xi[:T - s] + ar * ni + ai * nr, xi[T - s:]], axis=0)
                else:
                    nr, ni = xr[:T - s], xi[:T - s]
                    xr = jnp.concatenate([xr[:s], xr[s:] + ar * nr - ai * ni], axis=0)
                    xi = jnp.concatenate([xi[:s], xi[s:] + ar * ni + ai * nr], axis=0)
            xr, xi = xr + pr * cr - pi * ci, xi + pr * ci + pi * cr
            x_ref[rows, :LANES] = xr
            x_ref[rows, LANES:] = xi
            edge = pl.ds(sb * T + (0 if reverse else T - 1), 1)
            cr, ci = x_ref[edge, :LANES], x_ref[edge, LANES:]
            if with_da:
                if sb > 0:
                    before = pl.ds(sb * T - 1, 1)
                    b_r, b_i = xf_ref[before, :LANES], xf_ref[before, LANES:]
                else:
                    keep = (tblk(t) > 0).astype(F32)
                    b_r, b_i = xp_ref[7:8, :LANES] * keep, xp_ref[7:8, LANES:] * keep
                fr, fi = xf_ref[rows, :LANES], xf_ref[rows, LANES:]
                qr = jnp.where(rowi >= 1, pltpu.roll(fr, 1, 0), b_r)
                qi = jnp.where(rowi >= 1, pltpu.roll(fi, 1, 0), b_i)
                gr, gi = xr * qr + xi * qi, xi * qr - xr * qi
                sr, si = gr[0:8], gi[0:8]
                for j in range(1, T // 8):
                    sr, si = sr + gr[8 * j:8 * j + 8], si + gi[8 * j:8 * j + 8]
                da_ref[:, :LANES] += sr
                da_ref[:, LANES:] += si
        carry_ref[0:1, :LANES] = cr
        carry_ref[0:1, LANES:] = ci

    blk = pl.BlockSpec((tb, W), lambda j, t: (tblk(t), j))
    in_specs = [blk, pl.BlockSpec((8, W), lambda j, t: (0, j)), pl.BlockSpec((T, W), lambda j, t: (0, j))]
    out_specs, out_shape = [blk], [jax.ShapeDtypeStruct((S, N2), F32)]
    args = [bu, apow, ptab]
    if with_da:
        in_specs += [blk, pl.BlockSpec((8, W), lambda j, t: (jnp.maximum(tblk(t) * (tb // 8) - 1, 0), j))]
        out_specs.append(pl.BlockSpec((8, W), lambda j, t: (0, j)))
        out_shape.append(jax.ShapeDtypeStruct((8, N2), F32))
        args += [x_fwd, x_fwd]
    res = pl.pallas_call(body, name=name, grid=(N2 // W, nt), in_specs=in_specs, out_specs=out_specs, out_shape=out_shape,
                         scratch_shapes=[pltpu.VMEM((8, W), F32)], compiler_params=_cparams("parallel", "arbitrary"))(*args)
    return res if with_da else res[0]


S5_BAND = 4


def _mm_band(a, b, name, *, b_t=False, outer=False, epi=None, extras=(), tm=512, tk=2048):
    S = a.shape[0]
    wa = a.shape[1] // S5_BAND
    if outer:
        wb = b.shape[1] // S5_BAND
        tk = _tile(S, tk)
        nk = S // tk

        def obody(a_ref, b_ref, o_ref, acc_ref):
            k = pl.program_id(1)
            part = lax.dot_general(a_ref[...].astype(BF16), b_ref[...].astype(BF16), TN_DIMS, preferred_element_type=F32)

            @pl.when(k == 0)
            def _():
                acc_ref[...] = part

            @pl.when(k > 0)
            def _():
                acc_ref[...] += part

            @pl.when(k == nk - 1)
            def _():
                o_ref[...] = acc_ref[...]

        return pl.pallas_call(
            obody, name=name, grid=(S5_BAND, nk),
            in_specs=[pl.BlockSpec((tk, wa), lambda c, k: (k, c)), pl.BlockSpec((tk, wb), lambda c, k: (k, c))],
            out_specs=pl.BlockSpec((wa, wb), lambda c, k: (c, 0)), out_shape=jax.ShapeDtypeStruct((a.shape[1], wb), F32),
            scratch_shapes=[pltpu.VMEM((wa, wb), F32)], compiler_params=_cparams("parallel", "arbitrary"))(a, b)

    wo = (b.shape[0] if b_t else b.shape[1]) // S5_BAND
    tm = _tile(S, tm)
    ex_specs = [pl.BlockSpec((tm, wo), lambda i, c: (i, c)) if kind == "mn" else pl.BlockSpec((1, wo), lambda i, c: (0, c))
                for _, kind in extras]

    def body(a_ref, b_ref, *refs):
        part = lax.dot_general(a_ref[...].astype(BF16), b_ref[...].astype(BF16), NT_DIMS if b_t else _DIMS["nn"], preferred_element_type=F32)
        if epi is not None:
            part = epi(part, *[r[...] for r in refs[:-1]])
        refs[-1][...] = part

    b_spec = pl.BlockSpec((wo, wa) if b_t else (wa, wo), lambda i, c: (c, c))
    return pl.pallas_call(
        body, name=name, grid=(S // tm, S5_BAND), in_specs=[pl.BlockSpec((tm, wa), lambda i, c: (i, c)), b_spec] + ex_specs,
        out_specs=pl.BlockSpec((tm, wo), lambda i, c: (i, c)), out_shape=jax.ShapeDtypeStruct((S, S5_BAND * wo), F32),
        compiler_params=_cparams("parallel", "parallel"))(a, b, *[e[0] for e in extras])


def _band_to_full(blocks, cols):
    wa, wb = blocks.shape[0] // S5_BAND, blocks.shape[1]
    return jnp.concatenate([jnp.pad(blocks[k * wa:(k + 1) * wa], ((0, 0), (k * wb, cols - (k + 1) * wb))) for k in range(S5_BAND)], axis=0)


def _block_diag(t):
    G, a, b = t.shape
    return (t[:, :, None, :] * jnp.eye(G, dtype=t.dtype)[:, None, :, None]).reshape(G * a, G * b)


def _block_diag_take(m, G):
    a, b = m.shape[0] // G, m.shape[1] // G
    m4 = m.reshape(G, a, G, b)
    return jnp.sum(m4 * jnp.eye(G, dtype=m.dtype)[:, None, :, None], axis=2)


def _s5_block_fwd(u, w, pfx):
    a_re, a_im, bb_re, bb_im = _s5_discretise(w["lam_re"], w["lam_im"], w["log_dt"], w["b_re"], w["b_im"])
    bcat = _planes(_block_diag(bb_re).T, _block_diag(bb_im).T).astype(BF16)
    ccat = _planes(_block_diag(jnp.swapaxes(w["c_re"], 1, 2)).T, -_block_diag(jnp.swapaxes(w["c_im"], 1, 2)).T).T.astype(BF16)
    af_re, af_im = a_re.reshape(-1), a_im.reshape(-1)
    apow, ptab = _s5_scan_tables(af_re, af_im, False)
    bu = _mm_band(u, bcat, pfx + "_bu")
    x = _s5_scan(bu, apow, ptab, pfx + "_scan", False)
    d_row = w["d"].reshape(1, MIX_HALF)
    ys = _mm_band(x, ccat, pfx + "_y", epi=lambda acc, ut, dr: acc + dr * ut, extras=[(u, "mn"), (d_row, "n")])
    z = _mm(ys, w["w_glu"], "nn", pfx + "_glu", a_pro=_gelu, epi=lambda acc, b: acc + b, extras=[(w["b_glu"].reshape(1, -1), "n")])
    y2, = _ew(lambda ysv, zv: _gelu(ysv) * _sigmoid(zv), pfx + "_gate", [ys, z], outs=[(MIX_HALF, F32)])
    return y2, dict(u=u, x=x, ys=ys, z=z, bcat=bcat, ccat=ccat, a=(af_re, af_im), d_row=d_row)


def _s5_block_bwd(dy2, w, res, pfx):
    u, x, ys, z, bcat, ccat = res["u"], res["x"], res["ys"], res["z"], res["bcat"], res["ccat"]

    def gate_bwd(dy, ysv, zv):
        sg = _sigmoid(zv)
        dz = dy * _gelu(ysv) * sg * (1.0 - sg)
        return dz, jnp.sum(dz, axis=0, keepdims=True)

    dz, db_glu = _ew(gate_bwd, pfx + "_gate_bwd", [dy2, ys, z], outs=[(MIX_HALF, F32)], sums=[MIX_HALF])
    dw_glu = _mm(ys, dz, "tn", pfx + "_dwglu", a_pro=_gelu)
    dys = _mm(dz, w["w_glu"], "nt", pfx + "_dys", epi=lambda acc, dy, zv, ysv: (acc + dy * _sigmoid(zv)) * _dgelu(ysv),
              extras=[(dy2, "mn"), (z, "mn"), (ys, "mn")])
    dd, = _ew(lambda a, b: jnp.sum(a * b, axis=0, keepdims=True), pfx + "_dd", [dys, u], sums=[MIX_HALF])
    dccat = _band_to_full(_mm_band(x, dys, pfx + "_dc", outer=True), MIX_HALF)
    dx = _mm_band(dys, ccat, pfx + "_dx", b_t=True)
    af_re, af_im = res["a"]
    apow, ptab = _s5_scan_tables(af_re, -af_im, True)
    lam, da8 = _s5_scan(dx, apow, ptab, pfx + "_scan_bwd", True, x_fwd=x)
    dbcat = _band_to_full(_mm_band(u, lam, pfx + "_db", outer=True), 2 * S5_N)
    du = _mm_band(lam, bcat, pfx + "_du", b_t=True, epi=lambda acc, dyv, dr: acc + dyv * dr, extras=[(dys, "mn"), (res["d_row"], "n")])
    G = S5_GROUPS
    d_abar_re, d_abar_im = (t.reshape(G, S5_STATE) for t in _unplanes(jnp.sum(da8, axis=0)))
    d_bb_re, d_bb_im = (_block_diag_take(t.T, G) for t in _unplanes(dbcat))
    _, vjp = jax.vjp(_s5_discretise, w["lam_re"], w["lam_im"], w["log_dt"], w["b_re"], w["b_im"])
    g_lam_re, g_lam_im, g_log_dt, g_b_re, g_b_im = vjp((d_abar_re, d_abar_im, d_bb_re, d_bb_im))
    dc_re, dc_im = _unplanes(dccat.T)
    g_c_re = jnp.swapaxes(_block_diag_take(dc_re.T, G), 1, 2)
    g_c_im = -jnp.swapaxes(_block_diag_take(dc_im.T, G), 1, 2)
    grads = dict(lam_re=g_lam_re, lam_im=g_lam_im, log_dt=g_log_dt, b_re=g_b_re, b_im=g_b_im, c_re=g_c_re, c_im=g_c_im,
                 d=dd.reshape(G, S5_GROUP_WIDTH), w_glu=dw_glu, b_glu=db_glu.reshape(-1))
    return du, grads


SGU_TS = 512
N_PAIRS = MIX_HALF // LANES


def _half_masks(rows):
    lane = lax.broadcasted_iota(jnp.int32, (rows, LANES), 1)
    left = (lane < HEAD_DIM).astype(F32)
    return left, 1.0 - left


def _sgu_norm(zv, gain, bias):
    v = _gelu(zv)
    mu = jnp.mean(v, axis=-1, keepdims=True)
    vc = v - mu
    rstd = lax.rsqrt(jnp.mean(vc * vc, axis=-1, keepdims=True) + EPS)
    vhat = vc * rstd
    return vhat, rstd, vhat * gain + bias


def _sgu_tables(w_s, b_s):
    mask = jnp.tril(jnp.ones((SGU_CHUNK, SGU_CHUNK), dtype=bool))
    wm = jnp.where(mask[None], w_s, 0.0).astype(BF16)
    bias_tab = jnp.repeat(b_s.T, MIX_HALF // SGU_GROUPS, axis=1)
    return wm, bias_tab


def _sgu_fwd(proj, ln_gain, ln_bias, wm, bias_tab, name):
    S = proj.shape[0]
    nch = SGU_TS // SGU_CHUNK

    def body(zu_ref, zv_ref, g_ref, b_ref, w_ref, bt_ref, o_ref):
        left, right = _half_masks(SGU_CHUNK)
        _, _, vn = _sgu_norm(zv_ref[...], g_ref[...], b_ref[...])
        for ch in range(nch):
            rows = pl.ds(ch * SGU_CHUNK, SGU_CHUNK)
            for p in range(N_PAIRS):
                cols = pl.ds(p * LANES, LANES)
                vp = vn[ch * SGU_CHUNK:(ch + 1) * SGU_CHUNK, p * LANES:(p + 1) * LANES]
                mixed = (jnp.dot(w_ref[2 * p], (vp * left).astype(BF16), preferred_element_type=F32)
                         + jnp.dot(w_ref[2 * p + 1], (vp * right).astype(BF16), preferred_element_type=F32) + bt_ref[:, cols])
                o_ref[rows, cols] = _gelu(zu_ref[rows, cols]) * mixed

    vec = _vec_spec(MIX_HALF)
    return pl.pallas_call(
        body, name=name, grid=(S // SGU_TS,),
        in_specs=[pl.BlockSpec((SGU_TS, MIX_HALF), lambda i: (i, 1)), pl.BlockSpec((SGU_TS, MIX_HALF), lambda i: (i, 2)), vec, vec,
                  pl.BlockSpec((SGU_GROUPS, SGU_CHUNK, SGU_CHUNK), lambda i: (0, 0, 0)), pl.BlockSpec((SGU_CHUNK, MIX_HALF), lambda i: (0, 0))],
        out_specs=_row_spec(MIX_HALF, SGU_TS), out_shape=jax.ShapeDtypeStruct((S, MIX_HALF), F32),
        compiler_params=_cparams("parallel"))(proj, proj, ln_gain, ln_bias, wm, bias_tab)


def _sgu_bwd(dout, proj, ln_gain, ln_bias, wm, bias_tab, name):
    S = proj.shape[0]
    nch = SGU_TS // SGU_CHUNK
    nt_dims = (((1,), (1,)), ((), ()))
    tn_dims = (((0,), (0,)), ((), ()))

    def body(do_ref, zu_ref, zv_ref, g_ref, b_ref, w_ref, bt_ref, dzu_ref, dzv_ref, dw_ref, dbt_ref, dg_ref, db_ref, dvn_ref):
        first = pl.program_id(0) == 0

        @pl.when(first)
        def _():
            dw_ref[...] = jnp.zeros_like(dw_ref)
            dbt_ref[...] = jnp.zeros_like(dbt_ref)
            dg_ref[...] = jnp.zeros_like(dg_ref)
            db_ref[...] = jnp.zeros_like(db_ref)

        left, right = _half_masks(SGU_CHUNK)
        zv = zv_ref[...]
        vhat, rstd, vn = _sgu_norm(zv, g_ref[...], b_ref[...])
        for ch in range(nch):
            rows = pl.ds(ch * SGU_CHUNK, SGU_CHUNK)
            for p in range(N_PAIRS):
                cols = pl.ds(p * LANES, LANES)
                vp = vn[ch * SGU_CHUNK:(ch + 1) * SGU_CHUNK, p * LANES:(p + 1) * LANES]
                vl, vr = (vp * left).astype(BF16), (vp * right).astype(BF16)
                mixed = (jnp.dot(w_ref[2 * p], vl, preferred_element_type=F32)
                         + jnp.dot(w_ref[2 * p + 1], vr, preferred_element_type=F32) + bt_ref[:, cols])
                zu = zu_ref[rows, cols]
                do = do_ref[rows, cols]
                dzu_ref[rows, cols] = do * mixed * _dgelu(zu)
                dmix = do * _gelu(zu)
                dbt_ref[:, cols] += dmix
                dl, dr = (dmix * left).astype(BF16), (dmix * right).astype(BF16)
                dw_ref[2 * p] += lax.dot_general(dl, vl, nt_dims, preferred_element_type=F32)
                dw_ref[2 * p + 1] += lax.dot_general(dr, vr, nt_dims, preferred_element_type=F32)
                dvn_ref[rows, cols] = (lax.dot_general(w_ref[2 * p], dl, tn_dims, preferred_element_type=F32)
                                       + lax.dot_general(w_ref[2 * p + 1], dr, tn_dims, preferred_element_type=F32))
        dvn = dvn_ref[...]
        dg_ref[...] += jnp.sum(dvn * vhat, axis=0, keepdims=True)
        db_ref[...] += jnp.sum(dvn, axis=0, keepdims=True)
        dvh = dvn * g_ref[...]
        dv = rstd * (dvh - jnp.mean(dvh, axis=-1, keepdims=True) - vhat * jnp.mean(dvh * vhat, axis=-1, keepdims=True))
        dzv_ref[...] = dv * _dgelu(zv)

    vec = _vec_spec(MIX_HALF)
    row = _row_spec(MIX_HALF, SGU_TS)
    wspec = pl.BlockSpec((SGU_GROUPS, SGU_CHUNK, SGU_CHUNK), lambda i: (0, 0, 0))
    tspec = pl.BlockSpec((SGU_CHUNK, MIX_HALF), lambda i: (0, 0))
    full = jax.ShapeDtypeStruct((S, MIX_HALF), F32)
    v = jax.ShapeDtypeStruct((1, MIX_HALF), F32)
    return pl.pallas_call(
        body, name=name, grid=(S // SGU_TS,),
        in_specs=[row, pl.BlockSpec((SGU_TS, MIX_HALF), lambda i: (i, 1)), pl.BlockSpec((SGU_TS, MIX_HALF), lambda i: (i, 2)), vec, vec,
                  wspec, tspec],
        out_specs=[row, row, wspec, tspec, vec, vec],
        out_shape=[full, full, jax.ShapeDtypeStruct((SGU_GROUPS, SGU_CHUNK, SGU_CHUNK), F32),
                   jax.ShapeDtypeStruct((SGU_CHUNK, MIX_HALF), F32), v, v],
        scratch_shapes=[pltpu.VMEM((SGU_TS, MIX_HALF), F32)],
        compiler_params=_cparams("arbitrary"))(dout, proj, proj, ln_gain, ln_bias, wm, bias_tab)


def _sgu_grads(dw, dbias_tab):
    mask = jnp.tril(jnp.ones((SGU_CHUNK, SGU_CHUNK), dtype=bool))
    g_w = jnp.where(mask[None], dw, 0.0)
    g_b = dbias_tab.reshape(SGU_CHUNK, SGU_GROUPS, MIX_HALF // SGU_GROUPS).sum(axis=-1).T
    return g_w, g_b


def _head_avg_matrix(w):
    idx = np.arange(w) // HEAD_DIM
    return jnp.asarray((idx[:, None] == idx[None, :]).astype(np.float32) / HEAD_DIM, dtype=BF16)


def _head_mean(t, bavg):
    hi = t.astype(BF16)
    lo = (t - hi.astype(F32)).astype(BF16)
    return jnp.dot(hi, bavg, preferred_element_type=F32) + jnp.dot(lo, bavg, preferred_element_type=F32)


def _head_rms(t, bavg):
    r = lax.rsqrt(_head_mean(t * t, bavg) + EPS)
    return t * r, r


def _head_rms_bwd(dn, n, r, bavg):
    return r * (dn - n * _head_mean(dn * n, bavg))


GLA_TS = 512
C = GLA_CHUNK
NT_DIMS = (((1,), (1,)), ((), ()))
TN_DIMS = (((0,), (0,)), ((), ()))
HI = lax.Precision.HIGHEST


def _bdot(a, b, dims=(((1,), (0,)), ((), ()))):
    return lax.dot_general(a.astype(BF16), b.astype(BF16), dims, preferred_element_type=F32)


def _gla_chunk_terms(q, k, z):
    row = lax.broadcasted_iota(jnp.int32, (C, C), 0)
    col = lax.broadcasted_iota(jnp.int32, (C, C), 1)
    lc = _log_sigmoid(z) * (1.0 / GLA_TAU)
    b = lax.dot_general((row >= col).astype(F32), lc, (((1,), (0,)), ((), ())), precision=HI, preferred_element_type=F32)
    b_last = jnp.sum(lc, axis=0, keepdims=True)
    b_mid = b[C // 2:C // 2 + 1, :]
    scale = HEAD_DIM ** -0.5
    e_b, e_q, e_k, e_l = jnp.exp(b), jnp.exp(b - b_mid), jnp.exp(b_mid - b), jnp.exp(b_last - b)
    qs = q * (scale * e_b)
    qe = q * (scale * e_q)
    ke = k * e_k
    kl = k * e_l
    return dict(e_b=e_b, e_q=e_q, e_k=e_k, e_l=e_l, qs=qs, qe=qe, ke=ke, kl=kl, dec=jnp.exp(b_last), causal=row >= col, scale=scale)


def _pair(x, pp):
    return x[:, pp * LANES:(pp + 1) * LANES]


def _pair_block_diag():
    r = lax.broadcasted_iota(jnp.int32, (LANES, LANES), 0) // HEAD_DIM
    c = lax.broadcasted_iota(jnp.int32, (LANES, LANES), 1) // HEAD_DIM
    return (r == c).astype(F32)


def _gla_fwd(proj, z, name):
    S = proj.shape[0]
    nch = GLA_TS // C

    def body(q_ref, k_ref, v_ref, z_ref, o_ref, st_ref, state_ref):
        @pl.when(pl.program_id(0) == 0)
        def _():
            state_ref[...] = jnp.zeros_like(state_ref)

        left, right = _half_masks(C)
        bd = _pair_block_diag()
        pairs = range(N_PAIRS)
        for ch in range(nch):
            rows = pl.ds(ch * C, C)
            v = v_ref[rows, :]
            t = _gla_chunk_terms(q_ref[rows, :], k_ref[rows, :], z_ref[rows, :])
            sts = [state_ref[pp] for pp in pairs]
            for pp in pairs:
                st_ref[ch, pp] = sts[pp]
            os = [_bdot(_pair(t["qs"], pp), sts[pp], NT_DIMS) for pp in pairs]
            for m in (left, right):
                scores = [jnp.where(t["causal"], _bdot(_pair(t["qe"], pp) * m, _pair(t["ke"], pp), NT_DIMS), 0.0) for pp in pairs]
                os = [os[pp] + m * _bdot(scores[pp], _pair(v, pp)) for pp in pairs]
            o_ref[rows, :] = jnp.concatenate(os, axis=1)
            new = [sts[pp] * _pair(t["dec"], pp) + bd * _bdot(_pair(v, pp), _pair(t["kl"], pp), TN_DIMS) for pp in pairs]
            for pp in pairs:
                state_ref[pp] = new[pp]

    def col(cb):
        return pl.BlockSpec((GLA_TS, MIX_HALF), lambda i: (i, cb))

    return pl.pallas_call(
        body, name=name, grid=(S // GLA_TS,),
        in_specs=[col(0), col(1), col(2), col(0)],
        out_specs=[col(0), pl.BlockSpec((nch, N_PAIRS, LANES, LANES), lambda i: (i, 0, 0, 0))],
        out_shape=[jax.ShapeDtypeStruct((S, MIX_HALF), F32), jax.ShapeDtypeStruct((S // C, N_PAIRS, LANES, LANES), F32)],
        scratch_shapes=[pltpu.VMEM((N_PAIRS, LANES, LANES), F32)], compiler_params=_cparams("arbitrary"))(proj, proj, proj, z)


def _gla_bwd(do, proj, z, states, name):
    S = proj.shape[0]
    nch = GLA_TS // C
    nblk = S // GLA_TS

    def body(do_ref, q_ref, k_ref, v_ref, z_ref, st_ref, dq_ref, dk_ref, dv_ref, dlc_ref, dstate_ref):
        @pl.when(pl.program_id(0) == 0)
        def _():
            dstate_ref[...] = jnp.zeros_like(dstate_ref)

        left, right = _half_masks(C)
        bd = _pair_block_diag()
        rowi = lax.broadcasted_iota(jnp.int32, (C, LANES), 0)
        row = lax.broadcasted_iota(jnp.int32, (C, C), 0)
        colm = lax.broadcasted_iota(jnp.int32, (C, C), 1)
        pairs = range(N_PAIRS)
        rowi = lax.broadcasted_iota(jnp.int32, (C, MIX_HALF), 0)
        for ch in range(nch - 1, -1, -1):
            rows = pl.ds(ch * C, C)
            v, dov = v_ref[rows, :], do_ref[rows, :]
            t = _gla_chunk_terms(q_ref[rows, :], k_ref[rows, :], z_ref[rows, :])
            sts = [st_ref[ch, pp] for pp in pairs]
            nxt = [dstate_ref[pp] for pp in pairs]
            gs = [bd * nxt[pp] for pp in pairs]
            dqs = [_bdot(_pair(dov, pp), sts[pp]) for pp in pairs]
            dv = [_bdot(_pair(t["kl"], pp), gs[pp], NT_DIMS) for pp in pairs]
            dkl = [_bdot(_pair(v, pp), gs[pp]) for pp in pairs]
            dqe = [jnp.zeros((C, LANES), F32) for _ in pairs]
            dke = [jnp.zeros((C, LANES), F32) for _ in pairs]
            for m in (left, right):
                sc = [jnp.where(t["causal"], _bdot(_pair(t["qe"], pp) * m, _pair(t["ke"], pp), NT_DIMS), 0.0) for pp in pairs]
                dsc = [jnp.where(t["causal"], _bdot(_pair(dov, pp) * m, _pair(v, pp), NT_DIMS), 0.0) for pp in pairs]
                dv = [dv[pp] + m * _bdot(sc[pp], _pair(dov, pp), TN_DIMS) for pp in pairs]
                dqe = [dqe[pp] + m * _bdot(dsc[pp], _pair(t["ke"], pp)) for pp in pairs]
                dke = [dke[pp] + m * _bdot(dsc[pp], _pair(t["qe"], pp), TN_DIMS) for pp in pairs]
            for pp in pairs:
                dstate_ref[pp] = bd * (nxt[pp] * _pair(t["dec"], pp) + _bdot(_pair(dov, pp), _pair(t["qs"], pp), TN_DIMS))
            decay_sum = jnp.concatenate([jnp.sum(nxt[pp] * sts[pp], axis=0, keepdims=True) for pp in pairs], axis=1)
            dqs, dv, dkl, dqe, dke = (jnp.concatenate(parts, axis=1) for parts in (dqs, dv, dkl, dqe, dke))
            db_last = decay_sum * t["dec"] + jnp.sum(dkl * t["kl"], axis=0, keepdims=True)
            db = dqs * t["qs"] + dqe * t["qe"] - dke * t["ke"] - dkl * t["kl"]
            db = db + jnp.where(rowi == C - 1, db_last, 0.0)
            dq_ref[rows, :] = (dqs * t["e_b"] + dqe * t["e_q"]) * t["scale"]
            dk_ref[rows, :] = dke * t["e_k"] + dkl * t["e_l"]
            dv_ref[rows, :] = dv
            dlc_ref[rows, :] = lax.dot_general((colm >= row).astype(F32), db, (((1,), (0,)), ((), ())), precision=HI,
                                               preferred_element_type=F32)

    def col(cb):
        return pl.BlockSpec((GLA_TS, MIX_HALF), lambda i: (nblk - 1 - i, cb))

    full = jax.ShapeDtypeStruct((S, MIX_HALF), F32)
    return pl.pallas_call(
        body, name=name, grid=(nblk,),
        in_specs=[col(0), col(0), col(1), col(2), col(0), pl.BlockSpec((nch, N_PAIRS, LANES, LANES), lambda i: (nblk - 1 - i, 0, 0, 0))],
        out_specs=[col(0)] * 4, out_shape=[full, full, full, full],
        scratch_shapes=[pltpu.VMEM((N_PAIRS, LANES, LANES), F32)], compiler_params=_cparams("arbitrary"))(do, proj, proj, proj, z, states)


def _gla_block_fwd(proj, w_lr_pad, b_lr, gain, bavg, pfx):
    z = _mm(proj, w_lr_pad, "nn", pfx + "_z", a_cols=(7 * MIX_HALF, MIX_HALF), epi=lambda acc, b: acc + b, extras=[(b_lr, "n")])
    o, states = _gla_fwd(proj, z, pfx + "_core")

    def out(ov, gg, ba, gn):
        n, _ = _head_rms(ov, ba)
        return n * gn * (gg * _sigmoid(gg))

    og, = _ew(out, pfx + "_out", [o, (proj, MIX_HALF, 3)], consts=[bavg, gain], outs=[(MIX_HALF, F32)])
    return og, dict(z=z, o=o, states=states)


def _gla_block_bwd(dog, proj, w_lr_pad, gain, bavg, res, pfx):
    z, o, states = res["z"], res["o"], res["states"]

    def out_bwd(dy, ov, gg, ba, gn):
        n, r = _head_rms(ov, ba)
        sg = _sigmoid(gg)
        silu = gg * sg
        dn = dy * gn * silu
        do = _head_rms_bwd(dn, n, r, ba)
        dgg = dy * n * gn * (sg * (1.0 + gg * (1.0 - sg)))
        return do, dgg, jnp.sum(dy * n * silu, axis=0, keepdims=True)

    do, dgg, dgain = _ew(out_bwd, pfx + "_out_bwd", [dog, o, (proj, MIX_HALF, 3)], consts=[bavg, gain],
                         outs=[(MIX_HALF, F32), (MIX_HALF, F32)], sums=[MIX_HALF])
    dq, dk, dv, dlc = _gla_bwd(do, proj, z, states, pfx + "_core_bwd")

    def decay_bwd(dl, zv):
        dz = dl * (1.0 / GLA_TAU) * (1.0 - _sigmoid(zv))
        return dz, jnp.sum(dz, axis=0, keepdims=True)

    dz, db_lr = _ew(decay_bwd, pfx + "_decay_bwd", [dlc, z], outs=[(MIX_HALF, F32)], sums=[MIX_HALF])
    dw_lr_pad = _mm(proj, dz, "tn", pfx + "_dwlr", a_cols=(7 * MIX_HALF, MIX_HALF))
    dsmall = _mm(dz, w_lr_pad, "nt", pfx + "_dsmall")
    return (dq, dk, dv, dgg, dsmall), dict(w_lr=dw_lr_pad[:GLA_RANK], b_lr=db_lr.reshape(-1), gain=dgain.reshape(-1, HEAD_DIM))


FOX_T = 512
FOX_HEADS = MIX_HALF // HEAD_DIM
NEG = -1e30
CUM_T = 512


def _cum_lanes(x, name, reverse, pre=None):
    R, S = x.shape
    nb = S // CUM_T

    def body(x_ref, o_ref, carry_ref):
        @pl.when(pl.program_id(0) == 0)
        def _():
            carry_ref[...] = jnp.zeros_like(carry_ref)

        xv = x_ref[...]
        if pre is not None:
            xv = pre(xv)
        i = lax.broadcasted_iota(jnp.int32, (CUM_T, CUM_T), 0)
        j = lax.broadcasted_iota(jnp.int32, (CUM_T, CUM_T), 1)
        tri = ((i >= j) if reverse else (i <= j)).astype(F32)
        c = lax.dot_general(xv, tri, (((1,), (0,)), ((), ())), precision=HI, preferred_element_type=F32)
        carry = carry_ref[...]
        o_ref[...] = c + carry[:, 0:1]
        carry_ref[...] = carry + jnp.sum(xv, axis=1, keepdims=True)

    spec = pl.BlockSpec((R, CUM_T), (lambda i: (0, nb - 1 - i)) if reverse else (lambda i: (0, i)))
    return pl.pallas_call(body, name=name, grid=(nb,), in_specs=[spec], out_specs=spec, out_shape=jax.ShapeDtypeStruct((R, S), F32),
                          scratch_shapes=[pltpu.VMEM((R, LANES), F32)], compiler_params=_cparams("arbitrary"))(x)


def _fox_scores(q, k, cqb, ck_ref, h, m, diag):
    cq = cqb[:, h * HEAD_DIM:h * HEAD_DIM + 1]
    ck = ck_ref[0, h:h + 1, :]
    s = lax.dot_general(q * m.astype(q.dtype), k, NT_DIMS, preferred_element_type=F32) + (cq - ck)
    if not diag:
        return s
    row = lax.broadcasted_iota(jnp.int32, (FOX_T, FOX_T), 0)
    col = lax.broadcasted_iota(jnp.int32, (FOX_T, FOX_T), 1)
    return jnp.where(row < col, NEG, s)


def _on_causal_blocks(q_blk, k_blk, step):
    @pl.when(k_blk < q_blk)
    def _():
        step(False)

    @pl.when(k_blk == q_blk)
    def _():
        step(True)


def _causal_pairs(n, key_major):
    if key_major:
        pairs = [(q, k) for k in range(n) for q in range(k, n)]
    else:
        pairs = [(q, k) for q in range(n) for k in range(q + 1)]
    return jnp.asarray([p[0] for p in pairs], jnp.int32), jnp.asarray([p[1] for p in pairs], jnp.int32)


def _fox_fwd(qn, kn, proj, cum_b, cum_tp, name):
    S = qn.shape[0]
    nq = S // FOX_T
    qidx, kidx = _causal_pairs(nq, False)

    def body(qidx_ref, kidx_ref, q_ref, k_ref, v_ref, cq_ref, ck_ref, o_ref, lse_ref, m_scr, acc_scr):
        t = pl.program_id(1)
        qi, ki = qidx_ref[t], kidx_ref[t]

        @pl.when(ki == 0)
        def _():
            m_scr[...] = jnp.full_like(m_scr, NEG)
            acc_scr[...] = jnp.zeros_like(acc_scr)

        left, right = _half_masks(FOX_T)

        def step(diag):
            q, k, v = q_ref[...], k_ref[...], v_ref[...].astype(BF16)
            cqb = cq_ref[...]
            for h, m in enumerate((left, right)):
                s = _fox_scores(q, k, cqb, ck_ref, h, m, diag)
                m_prev = m_scr[h]
                m_new = jnp.maximum(m_prev, jnp.max(s, axis=1, keepdims=True))
                p = jnp.exp(s - m_new)
                v_h = jnp.where(m > 0, v, jnp.ones_like(v))
                acc_scr[h] = jnp.exp(m_prev - m_new) * acc_scr[h] + jnp.dot(p.astype(BF16), v_h, preferred_element_type=F32)
                m_scr[h] = m_new

        _on_causal_blocks(qi, ki, step)

        @pl.when(ki == qi)
        def _():
            a0, a1 = acc_scr[0], acc_scr[1]
            is_left = left > 0
            num = jnp.where(is_left, a0, a1)
            den = jnp.where(is_left, pltpu.roll(a0, HEAD_DIM, 1), pltpu.roll(a1, HEAD_DIM, 1))
            o_ref[...] = num / den
            lse_ref[...] = jnp.where(is_left, m_scr[0], m_scr[1]) + jnp.log(den)

    qspec = pl.BlockSpec((FOX_T, LANES), lambda p, t, qx, kx: (qx[t], p))
    kspec = pl.BlockSpec((FOX_T, LANES), lambda p, t, qx, kx: (kx[t], p))
    vspec = pl.BlockSpec((FOX_T, LANES), lambda p, t, qx, kx: (kx[t], 6 * N_PAIRS + p))
    ckspec = pl.BlockSpec((1, 8, FOX_T), lambda p, t, qx, kx: (p, 0, kx[t]))
    full = jax.ShapeDtypeStruct((S, MIX_HALF), F32)
    grid_spec = pltpu.PrefetchScalarGridSpec(
        num_scalar_prefetch=2, grid=(N_PAIRS, int(qidx.shape[0])), in_specs=[qspec, kspec, vspec, qspec, ckspec], out_specs=[qspec, qspec],
        scratch_shapes=[pltpu.VMEM((2, FOX_T, 1), F32), pltpu.VMEM((2, FOX_T, LANES), F32)])
    return pl.pallas_call(body, name=name, grid_spec=grid_spec, out_shape=[full, full],
                          compiler_params=_cparams("parallel", "arbitrary"))(qidx, kidx, qn, kn, proj, cum_b, cum_tp)


def _fox_bwd(do, qn, kn, proj, cum_b, cum_tp, lse_b, delta_b, name):
    S = qn.shape[0]
    nq = S // FOX_T
    scale = HEAD_DIM ** -0.5
    qidx, kidx = _causal_pairs(nq, True)
    ntri = int(qidx.shape[0])

    def body(qidx_ref, kidx_ref, do_ref, q_ref, k_ref, v_ref, cq_ref, ck_ref, lse_ref, dl_ref,
             dq_ref, dcq_ref, dk_ref, dv_ref, dck_ref, dq_scr, dk_scr, dv_scr):
        t = pl.program_id(1)
        qi, ki = qidx_ref[t], kidx_ref[t]

        @pl.when(t == 0)
        def _():
            dq_scr[...] = jnp.zeros_like(dq_scr)

        @pl.when(qi == ki)
        def _():
            dk_scr[...] = jnp.zeros_like(dk_scr)
            dv_scr[...] = jnp.zeros_like(dv_scr)

        left, right = _half_masks(FOX_T)
        rows = pl.ds(pl.multiple_of(qi * FOX_T, FOX_T), FOX_T)

        def step(diag):
            q, k, v, dov = q_ref[...], k_ref[...], v_ref[...].astype(BF16), do_ref[...]
            cqb, lseb, dlb = cq_ref[...], lse_ref[...], dl_ref[...]
            dob = dov.astype(BF16)
            heads = (0, 1)
            masks = (left, right)
            col = [slice(h * HEAD_DIM, h * HEAD_DIM + 1) for h in heads]
            ss = [_fox_scores(q, k, cqb, ck_ref, h, masks[h], diag) for h in heads]
            dps = [lax.dot_general((dov * masks[h]).astype(BF16), v, NT_DIMS, preferred_element_type=F32) for h in heads]
            ps = [jnp.exp(ss[h] - lseb[:, col[h]]) for h in heads]
            dss = [(ps[h] * (dps[h] - dlb[:, col[h]])).astype(BF16) for h in heads]
            pvs = [lax.dot_general(ps[h].astype(BF16), dob, TN_DIMS, preferred_element_type=F32) for h in heads]
            dks = [lax.dot_general(dss[h], jnp.where(masks[h] > 0, q, jnp.ones_like(q)), TN_DIMS, preferred_element_type=F32) for h in heads]
            dqs = [jnp.dot(dss[h], jnp.where(masks[h] > 0, k, jnp.ones_like(k)), preferred_element_type=F32) for h in heads]
            dv_scr[...] = dv_scr[...] + left * pvs[0] + right * pvs[1]
            for h in heads:
                dk_scr[h] = dk_scr[h] + dks[h]
                dq_scr[h, rows, :] = dq_scr[h, rows, :] + dqs[h]

        _on_causal_blocks(qi, ki, step)

        @pl.when(qi == nq - 1)
        def _():
            a0, a1 = dk_scr[0], dk_scr[1]
            dk_ref[...] = left * a0 + right * a1
            dv_ref[...] = dv_scr[...]
            dck_ref[...] = left * pltpu.roll(a0, HEAD_DIM, 1) + right * pltpu.roll(a1, HEAD_DIM, 1)

        @pl.when(t == ntri - 1)
        def _():
            for r in range(nq):
                blk = pl.ds(r * FOX_T, FOX_T)
                a0, a1 = dq_scr[0, blk, :], dq_scr[1, blk, :]
                dq_ref[blk, :] = (left * a0 + right * a1) * scale
                dcq_ref[blk, :] = left * pltpu.roll(a0, HEAD_DIM, 1) + right * pltpu.roll(a1, HEAD_DIM, 1)

    qspec = pl.BlockSpec((FOX_T, LANES), lambda p, t, qx, kx: (qx[t], p))
    kspec = pl.BlockSpec((FOX_T, LANES), lambda p, t, qx, kx: (kx[t], p))
    vspec = pl.BlockSpec((FOX_T, LANES), lambda p, t, qx, kx: (kx[t], 6 * N_PAIRS + p))
    ckspec = pl.BlockSpec((1, 8, FOX_T), lambda p, t, qx, kx: (p, 0, kx[t]))
    seq = pl.BlockSpec((S, LANES), lambda p, t, qx, kx: (0, p))
    full = jax.ShapeDtypeStruct((S, MIX_HALF), F32)
    grid_spec = pltpu.PrefetchScalarGridSpec(
        num_scalar_prefetch=2, grid=(N_PAIRS, ntri), in_specs=[qspec, qspec, kspec, vspec, qspec, ckspec, qspec, qspec],
        out_specs=[seq, seq, kspec, kspec, kspec],
        scratch_shapes=[pltpu.VMEM((2, S, LANES), F32), pltpu.VMEM((2, FOX_T, LANES), F32), pltpu.VMEM((FOX_T, LANES), F32)])
    return pl.pallas_call(body, name=name, grid_spec=grid_spec, out_shape=[full] * 5,
                          compiler_params=_cparams("parallel", "arbitrary"))(qidx, kidx, do, qn, kn, proj, cum_b, cum_tp, lse_b, delta_b)


def _ff_bwd(rc, f_t, name):
    def body(rc_ref, f_ref, d_ref, s_ref):
        d = rc_ref[...] * (1.0 - _sigmoid(f_ref[...]))
        d_ref[...] = d
        s_ref[...] = jnp.sum(d, axis=1, keepdims=True)

    return pl.pallas_call(body, name=name, out_shape=[jax.ShapeDtypeStruct(rc.shape, F32), jax.ShapeDtypeStruct((rc.shape[0], 1), F32)])(rc, f_t)


def _fox_block_fwd(proj, b_f, q_gain, k_gain, bavg, pfx):
    S = proj.shape[0]

    def prep(qv, kv, ba, qg, kg):
        return _head_rms(qv, ba)[0] * qg * (HEAD_DIM ** -0.5), _head_rms(kv, ba)[0] * kg

    qn, kn = _ew(prep, pfx + "_prep", [(proj, MIX_HALF, 4), (proj, MIX_HALF, 5)], consts=[bavg, q_gain, k_gain],
                 outs=[(MIX_HALF, BF16), (MIX_HALF, BF16)])
    f0 = 7 * MIX_HALF + GLA_RANK
    f_t = proj[:, f0:f0 + FOX_HEADS].T + b_f.reshape(FOX_HEADS, 1)
    cum = _cum_lanes(f_t, pfx + "_cum", False, pre=_log_sigmoid)
    cum_b = jnp.repeat(cum.T, HEAD_DIM, axis=1)
    cum_tp = jnp.pad(cum.reshape(N_PAIRS, 2, S), ((0, 0), (0, 6), (0, 0)))
    o, lse_b = _fox_fwd(qn, kn, proj, cum_b, cum_tp, pfx + "_attn")
    return o, dict(qn=qn, kn=kn, f_t=f_t, cum_b=cum_b, cum_tp=cum_tp, o=o, lse_b=lse_b)


def _fox_block_bwd(do, proj, q_gain, k_gain, bavg, res, pfx):
    qn, kn, o = res["qn"], res["kn"], res["o"]
    S = proj.shape[0]
    delta_b, = _ew(lambda a, b, ba: _head_mean(a * b, ba) * float(HEAD_DIM), pfx + "_delta", [do, o], consts=[bavg], outs=[(MIX_HALF, F32)])
    args = (do, qn, kn, proj, res["cum_b"], res["cum_tp"], res["lse_b"], delta_b)
    dqn, dcq_b, dkn, dv, dck_b = _fox_bwd(*args, pfx + "_bwd")

    def prep_bwd(dq, dk, qv, kv, ba, qg, kg):
        nq, rq = _head_rms(qv, ba)
        nk, rk = _head_rms(kv, ba)
        return (_head_rms_bwd(dq * qg, nq, rq, ba), _head_rms_bwd(dk * kg, nk, rk, ba),
                jnp.sum(dq * nq, axis=0, keepdims=True), jnp.sum(dk * nk, axis=0, keepdims=True))

    dfq, dfk, dqg, dkg = _ew(prep_bwd, pfx + "_prep_bwd", [dqn, dkn, (proj, MIX_HALF, 4), (proj, MIX_HALF, 5)],
                             consts=[bavg, q_gain, k_gain], outs=[(MIX_HALF, F32), (MIX_HALF, F32)], sums=[MIX_HALF, MIX_HALF])
    dcum = (dcq_b - dck_b)[:, ::HEAD_DIM].T
    rc = _cum_lanes(dcum, pfx + "_rcum", True)
    dff_t, db_f = _ff_bwd(rc, res["f_t"], pfx + "_ff_bwd")
    grads = dict(b_f=db_f.reshape(-1), q_gain=dqg.reshape(-1, HEAD_DIM), k_gain=dkg.reshape(-1, HEAD_DIM))
    return (dfq, dfk, dv, dff_t.T), grads


WEIGHTS = ['ada_w', 'ada_b', 'even_w_in', 'even_w_out', 'gla_w_lr', 'gla_b_lr', 'gla_gain', 'fox_b_f', 'fox_q_gain', 'fox_k_gain',
           'odd_w_in', 'odd_w_out', 's5_lam_re', 's5_lam_im', 's5_log_dt', 's5_b_re', 's5_b_im', 's5_c_re', 's5_c_im', 's5_d',
           's5_w_glu', 's5_b_glu', 'sgu_ln_gain', 'sgu_ln_bias', 'sgu_w_s', 'sgu_b_s', 'mlp_w1', 'mlp_w2']
ARGS = ['x', 'c'] + WEIGHTS + ['loss_target'] + ['m_' + w for w in WEIGHTS] + ['v_' + w for w in WEIGHTS]

EVEN_COLS = 3608
EVEN_PAD = 8 * MIX_HALF
MOD = 6 * D_MODEL
MOD_SHARD = MOD // N_CHIPS

PACK_COLS = 1024
PACK_ROWS = 6144
EVEN_SHARD = EVEN_COLS // N_CHIPS
SHARDED = [("even_w_in", (1, 1024, PACK_COLS), 2), ("even_w_out", (1, 256, 1024), 1), ("odd_w_in", (1, 1024, 384), 2),
           ("odd_w_out", (1, 256, 1024), 1), ("mlp_w1", (2, 1024, 1024), 2), ("mlp_w2", (2, 1024, 1024), 1),
           ("gla_w_lr", (1, 16, 128), 2), ("s5_w_glu", (1, 128, 512), 1), ("s5_b_glu", (1, 128), 1),
           ("sgu_ln_gain", (1, 128), 1), ("sgu_ln_bias", (1, 128), 1)]
REPLICATED = [("gla_b_lr", (1, 512)), ("gla_gain", (1, 8, 64)), ("fox_b_f", (1, 8)), ("fox_q_gain", (1, 8, 64)),
              ("fox_k_gain", (1, 8, 64)), ("s5_lam_re", (1, 32, 64)), ("s5_lam_im", (1, 32, 64)), ("s5_log_dt", (1, 32)),
              ("s5_b_re", (1, 32, 64, 16)), ("s5_b_im", (1, 32, 64, 16)), ("s5_c_re", (1, 32, 16, 64)), ("s5_c_im", (1, 32, 16, 64)),
              ("s5_d", (1, 32, 16)), ("sgu_w_s", (1, 8, 128, 128)), ("sgu_b_s", (1, 8, 128))]
SMALL_ROWS = 512
BIG_ADAM = {"ada_w": (2048, 1536), "even_w_in": (1024, 902), "even_w_out": (256, 1024), "odd_w_in": (1024, 384),
            "odd_w_out": (256, 1024), "mlp_w1": (2048, 1024), "mlp_w2": (2048, 1024), "s5_w_glu": (128, 512)}


PACK_ALIGN = 16


def _piece_rows(shape):
    rows = -(-math.prod(shape) // PACK_COLS)
    return -(-rows // PACK_ALIGN) * PACK_ALIGN


def _to_rows(p, lead=()):
    n = math.prod(p.shape[len(lead):])
    rows = _piece_rows(p.shape[len(lead):])
    flat = p.reshape(lead + (n,))
    if rows * PACK_COLS != n:
        flat = jnp.pad(flat, [(0, 0)] * len(lead) + [(0, rows * PACK_COLS - n)])
    return flat.reshape(lead + (rows, PACK_COLS))


def _from_rows(x, r0, shape, lead=()):
    n = math.prod(shape)
    seg = lax.slice_in_dim(x, r0, r0 + _piece_rows(shape), axis=len(lead)).reshape(lead + (-1,))
    return lax.slice_in_dim(seg, 0, n, axis=len(lead)).reshape(lead + tuple(shape))


def _pack_rows(pieces, rows):
    x = jnp.concatenate([_to_rows(p) for p in pieces], axis=0)
    return jnp.pad(x, ((0, rows - x.shape[0]), (0, 0)))


def _unpack(x, specs):
    out, r0 = {}, 0
    for name, shape in specs:
        out[name] = _from_rows(x, r0, shape)
        r0 += _piece_rows(shape)
    return out


def _shards_to_full(x4):
    out, r0 = {}, 0
    for name, shape, axis in SHARDED:
        seg = _from_rows(x4, r0, shape, lead=(N_CHIPS,))
        out[name] = jnp.concatenate([seg[k] for k in range(N_CHIPS)], axis=axis)
        r0 += _piece_rows(shape)
    return out


def _full_to_shards(full):
    blocks = [_to_rows(jnp.stack(jnp.split(full[name], N_CHIPS, axis=axis)), lead=(N_CHIPS,)) for name, _, axis in SHARDED]
    x = jnp.concatenate(blocks, axis=1)
    return jnp.pad(x, ((0, 0), (0, PACK_ROWS - x.shape[1]), (0, 0)))


def _relu2(t):
    r = jnp.maximum(t, 0.0)
    return r * r


def _silu(t):
    return t * _sigmoid(t)


def _pack_even(w):
    return jnp.concatenate([w[:, :2048], w[:, 2064:3600], w[:, 2048:2064], w[:, 3600:3608],
                            jnp.zeros((w.shape[0], EVEN_PAD - EVEN_COLS), w.dtype)], axis=1)


def _unpack_even(wp):
    return jnp.concatenate([wp[:, :2048], wp[:, 3584:3600], wp[:, 2048:3584], wp[:, 3600:3608]], axis=1)


def _mlp_fwd(h, w1, w2, pfx):
    pre = _mm(h, w1, "nn", pfx + "_up", out_dtype=BF16)
    return pre, _mm(pre, w2, "nn", pfx + "_down", a_pro=_relu2)


def _mlp_bwd(dm, h, pre, w1, w2, pfx):
    dpre = _mm(dm, w2, "nt", pfx + "_dpre", epi=lambda acc, p: acc * (2.0 * jnp.maximum(p, 0.0)), extras=[(pre, "mn")], out_dtype=BF16)
    dw2 = _mm(pre, dm, "tn", pfx + "_dw2", a_pro=_relu2)
    dw1 = _mm(h, dpre, "tn", pfx + "_dw1")
    dh = _mm(dpre, w1, "nt", pfx + "_dh")
    return dh, dw1, dw2


def _step(args):
    a = dict(zip(ARGS, args, strict=True))
    x0 = a["x"][0]
    target = a["loss_target"][0]
    mx, my, mc = lax.axis_index("x"), lax.axis_index("y"), lax.axis_index("c")
    chip = 2 * mx + my
    dev = 2 * chip + mc
    bavg = _head_avg_matrix(MIX_HALF)

    c_all = _gather8(jnp.pad(a["c"], ((0, 7), (0, 0))), "c_gather")[:, :, 0, :].reshape(2 * N_CHIPS, D_MODEL)
    ada_b_shard = lax.dynamic_slice_in_dim(a["ada_b"], chip * MOD_SHARD, MOD_SHARD, axis=1)
    mod_sh = [_mm(c_all, a["ada_w"][l], "nn", f"mod{l}", a_pro=_silu, epi=lambda acc, b: acc + b, extras=[(ada_b_shard[l:l + 1], "n")])
              for l in range(2)]
    small3 = jnp.zeros((8, MOD_SHARD), F32)
    for r, n in enumerate(("s5_b_glu", "sgu_ln_gain", "sgu_ln_bias")):
        small3 = small3.at[r, :LANES].set(a[n][0])
    mod_all = _chip_exchange(jnp.concatenate(mod_sh + [small3]), "mod_gather", True)
    mods = []
    for l in range(2):
        full = mod_all[:, 8 * l:8 * l + 8].transpose(1, 0, 2).reshape(8, MOD)
        mods.append(jnp.split(lax.dynamic_slice_in_dim(full, dev, 1, axis=0), 6, axis=1))
    b_glu, ln_gain, ln_bias = (mod_all[:, 16 + r, :LANES].reshape(1, MIX_HALF) for r in range(3))

    local = dict(a, even_w_in=jnp.pad(a["even_w_in"], ((0, 0), (0, 0), (0, PACK_COLS - EVEN_SHARD))))
    shard = _pack_rows([local[n] for n, _, _ in SHARDED], PACK_ROWS).astype(BF16)
    half = lax.dynamic_slice_in_dim(shard, mc * (PACK_ROWS // 2), PACK_ROWS // 2, axis=0)
    collected = _chip_exchange(half, "w_chips", True)
    halves = _by_core(collected, _pair_swap(collected, "w_pair"))
    w = _shards_to_full(halves.transpose(1, 0, 2, 3).reshape(N_CHIPS, PACK_ROWS, PACK_COLS))
    w_even = _pack_even(w["even_w_in"][0].reshape(D_MODEL, N_CHIPS, PACK_COLS)[:, :, :EVEN_SHARD].reshape(D_MODEL, EVEN_COLS))
    w_lr_pad = jnp.zeros((MIX_HALF, MIX_HALF), BF16).at[:GLA_RANK].set(w["gla_w_lr"][0])
    gla_b_lr = a["gla_b_lr"]
    gla_gain, q_gain, k_gain = (a[n].reshape(1, MIX_HALF) for n in ("gla_gain", "fox_q_gain", "fox_k_gain"))
    s5w = dict(lam_re=a["s5_lam_re"][0], lam_im=a["s5_lam_im"][0], log_dt=a["s5_log_dt"][0], b_re=a["s5_b_re"][0], b_im=a["s5_b_im"][0],
               c_re=a["s5_c_re"][0], c_im=a["s5_c_im"][0], d=a["s5_d"][0], w_glu=w["s5_w_glu"][0], b_glu=b_glu)
    sgu_wm, sgu_bt = _sgu_tables(a["sgu_w_s"][0], a["sgu_b_s"][0])

    sh1, sc1, g1, sh2, sc2, g2 = mods[0]
    _, h1_0 = _res_rms(x0, sc1, sh1, "l0_norm1")
    proj0 = _mm(h1_0, w_even, "nn", "l0_proj")
    og, gla_res = _gla_block_fwd(proj0, w_lr_pad, gla_b_lr, gla_gain, bavg, "gla")
    of, fox_res = _fox_block_fwd(proj0, a["fox_b_f"][0], q_gain, k_gain, bavg, "fox")
    mixed0 = jnp.concatenate([og, of], axis=1).astype(BF16)
    y0 = _mm(mixed0, w["even_w_out"][0], "nn", "l0_out")
    x1, h2_0 = _res_rms(x0, sc2, sh2, "l0_norm2", y=y0, g=g1)
    pre0, m0 = _mlp_fwd(h2_0, w["mlp_w1"][0], w["mlp_w2"][0], "l0_mlp")
    sh1b, sc1b, g1b, sh2b, sc2b, g2b = mods[1]
    x2, h1_1 = _res_rms(x1, sc1b, sh1b, "l1_norm1", y=m0, g=g2)
    proj1 = _mm(h1_1, w["odd_w_in"][0], "nn", "l1_proj")
    ys5, s5_res = _s5_block_fwd(proj1[:, :MIX_HALF], s5w, "s5")
    ysgu = _sgu_fwd(proj1, ln_gain, ln_bias, sgu_wm, sgu_bt, "sgu")
    mixed1 = jnp.concatenate([ys5, ysgu], axis=1).astype(BF16)
    y1 = _mm(mixed1, w["odd_w_out"][0], "nn", "l1_out")
    x3, h2_1 = _res_rms(x2, sc2b, sh2b, "l1_norm2", y=y1, g=g1b)
    pre1, m1 = _mlp_fwd(h2_1, w["mlp_w1"][1], w["mlp_w2"][1], "l1_mlp")
    loss_b, dx4, dm1, dg2b = _res_loss(x3, m1, g2b, target, "loss")
    loss = lax.psum(loss_b[0, 0], ("x", "y", "c"))

    full = {}
    dh2_1, dw1_1, dw2_1 = _mlp_bwd(dm1, h2_1, pre1, w["mlp_w1"][1], w["mlp_w2"][1], "l1_mlp")
    dx3, dy1, dg1b, dsc2b, dsh2b = _res_rms_bwd(x3, dh2_1, sc2b, dx4, "l1_norm2_bwd", y=y1, g=g1b)
    dmixed1 = _mm(dy1, w["odd_w_out"][0], "nt", "l1_out_dx")
    full["odd_w_out"] = _mm(mixed1, dy1, "tn", "l1_out_dw")[None]
    du, s5g = _s5_block_bwd(dmixed1[:, :MIX_HALF], s5w, s5_res, "s5")
    dzu, dzv, dws, dbt, dlg, dlb = _sgu_bwd(dmixed1[:, MIX_HALF:], proj1, ln_gain, ln_bias, sgu_wm, sgu_bt, "sgu_bwd")
    g_ws, g_bs = _sgu_grads(dws, dbt)
    dproj1 = jnp.concatenate([du, dzu, dzv], axis=1).astype(BF16)
    full["odd_w_in"] = _mm(h1_1, dproj1, "tn", "l1_proj_dw")[None]
    dh1_1 = _mm(dproj1, w["odd_w_in"][0], "nt", "l1_proj_dx")
    dx2, dm0, dg2, dsc1b, dsh1b = _res_rms_bwd(x2, dh1_1, sc1b, dx3, "l1_norm1_bwd", y=m0, g=g2)
    dh2_0, dw1_0, dw2_0 = _mlp_bwd(dm0, h2_0, pre0, w["mlp_w1"][0], w["mlp_w2"][0], "l0_mlp")
    full["mlp_w1"] = jnp.stack([dw1_0, dw1_1])
    full["mlp_w2"] = jnp.stack([dw2_0, dw2_1])
    dx1, dy0, dg1, dsc2, dsh2 = _res_rms_bwd(x1, dh2_0, sc2, dx2, "l0_norm2_bwd", y=y0, g=g1)
    dmixed0 = _mm(dy0, w["even_w_out"][0], "nt", "l0_out_dx")
    full["even_w_out"] = _mm(mixed0, dy0, "tn", "l0_out_dw")[None]
    (dgq, dgk, dgv, dgg, dsmall), glag = _gla_block_bwd(dmixed0[:, :MIX_HALF], proj0, w_lr_pad, gla_gain, bavg, gla_res, "gla")
    (dfq, dfk, dfv, dff), foxg = _fox_block_bwd(dmixed0[:, MIX_HALF:], proj0, q_gain, k_gain, bavg, fox_res, "fox")
    dsmall = lax.dynamic_update_slice(dsmall, dff, (0, GLA_RANK))
    dproj0 = jnp.concatenate([dgq, dgk, dgv, dgg, dfq, dfk, dfv, dsmall], axis=1).astype(BF16)
    d_even = _unpack_even(_mm(h1_0, dproj0, "tn", "l0_proj_dw")).reshape(D_MODEL, N_CHIPS, EVEN_SHARD)
    full["even_w_in"] = jnp.pad(d_even, ((0, 0), (0, 0), (0, PACK_COLS - EVEN_SHARD))).reshape(1, D_MODEL, N_CHIPS * PACK_COLS)
    dh1_0 = _mm(dproj0, w_even, "nt", "l0_proj_dx")
    grad_x, dsc1, dsh1 = _res_rms_bwd(x0, dh1_0, sc1, dx1, "l0_norm1_bwd")
    full["gla_w_lr"] = glag["w_lr"][None]
    full["s5_w_glu"] = s5g["w_glu"][None]
    full["s5_b_glu"] = s5g["b_glu"][None]
    full["sgu_ln_gain"] = dlg
    full["sgu_ln_bias"] = dlb

    dmod = jnp.concatenate([dsh1, dsc1, dg1, dsh2, dsc2, dg2, dsh1b, dsc1b, dg1b, dsh2b, dsc2b, dg2b], axis=1)
    dmod_all = _gather8(jnp.pad(dmod, ((0, 7), (0, 0))), "dmod_gather")[:, :, 0, :].reshape(2 * N_CHIPS, 2, MOD)
    grads = {}
    grads["ada_w"] = jnp.stack([
        _mm(c_all, lax.dynamic_slice_in_dim(dmod_all[:, l], chip * MOD_SHARD, MOD_SHARD, axis=1), "tn", f"ada_dw{l}", a_pro=_silu)
        for l in range(2)])
    grads["ada_b"] = _sum_slots(dmod_all.reshape(2 * N_CHIPS, 2 * MOD // MIX_HALF, MIX_HALF), "ada_db").reshape(2, MOD)

    packed = _full_to_shards(full)
    hr = PACK_ROWS // 2
    mine = lax.dynamic_slice_in_dim(packed, mc * hr, hr, axis=1)
    other = lax.dynamic_slice_in_dim(packed, (1 - mc) * hr, hr, axis=1)
    theirs = _pair_swap(other, "g_pair")
    pair_sum, = _ew(lambda p, q: p + q, "g_pair_sum", [mine.reshape(N_CHIPS * hr, PACK_COLS), theirs.reshape(N_CHIPS * hr, PACK_COLS)],
                    outs=[(PACK_COLS, BF16)])
    arrived = _chip_exchange(pair_sum.reshape(N_CHIPS, hr, PACK_COLS), "g_chips", False)
    red_half = _sum_slots(arrived, "g_chip_sum")
    reduced = _by_core(red_half, _pair_swap(red_half, "g_pair_out")).reshape(PACK_ROWS, PACK_COLS)
    grads.update(_unpack(reduced, [(n, s) for n, s, _ in SHARDED]))
    grads["even_w_in"] = grads["even_w_in"][:, :, :EVEN_SHARD]

    part = dict(gla_b_lr=glag["b_lr"], gla_gain=glag["gain"], fox_b_f=foxg["b_f"], fox_q_gain=foxg["q_gain"], fox_k_gain=foxg["k_gain"],
                s5_lam_re=s5g["lam_re"], s5_lam_im=s5g["lam_im"], s5_log_dt=s5g["log_dt"], s5_b_re=s5g["b_re"], s5_b_im=s5g["b_im"],
                s5_c_re=s5g["c_re"], s5_c_im=s5g["c_im"], s5_d=s5g["d"], sgu_w_s=g_ws, sgu_b_s=g_bs)
    parts_all = _gather8(_pack_rows([part[n] for n, _ in REPLICATED], SMALL_ROWS).astype(BF16), "rep_gather")
    rep = _sum_slots(parts_all.reshape(2 * N_CHIPS, SMALL_ROWS, PACK_COLS), "rep_sum")
    grads.update(_unpack(rep, REPLICATED))

    delta, new_m, new_v = {}, {}, {}
    for n, shape2 in BIG_ADAM.items():
        d, nm, nv = _adamw(a[n].reshape(shape2), grads[n].reshape(shape2), a["m_" + n].reshape(shape2), a["v_" + n].reshape(shape2), "adamw_" + n)
        delta[n], new_m[n], new_v[n] = (t.reshape(a[n].shape) for t in (d, nm, nv))
    small = [n for n in WEIGHTS if n not in BIG_ADAM]
    spec = [(n, a[n].shape) for n in small]
    packs = [_pack_rows([src[n] for n in small], SMALL_ROWS) for src in
             (a, grads, {n: a["m_" + n] for n in small}, {n: a["v_" + n] for n in small})]
    for tgt, res in zip((delta, new_m, new_v), _adamw(*packs, "adamw_small")):
        tgt.update(_unpack(res, spec))
    outs = [loss, grad_x[None]]
    for group in (grads, delta, new_m, new_v):
        outs += [group[n].reshape(a[n].shape) for n in WEIGHTS]
    return tuple(outs)


def kernel(x, c, ada_w, ada_b, even_w_in, even_w_out, gla_w_lr, gla_b_lr, gla_gain, fox_b_f, fox_q_gain, fox_k_gain, odd_w_in,
           odd_w_out, s5_lam_re, s5_lam_im, s5_log_dt, s5_b_re, s5_b_im, s5_c_re, s5_c_im, s5_d, s5_w_glu, s5_b_glu, sgu_ln_gain,
           sgu_ln_bias, sgu_w_s, sgu_b_s, mlp_w1, mlp_w2, loss_target, m_ada_w, m_ada_b, m_even_w_in, m_even_w_out, m_gla_w_lr,
           m_gla_b_lr, m_gla_gain, m_fox_b_f, m_fox_q_gain, m_fox_k_gain, m_odd_w_in, m_odd_w_out, m_s5_lam_re, m_s5_lam_im,
           m_s5_log_dt, m_s5_b_re, m_s5_b_im, m_s5_c_re, m_s5_c_im, m_s5_d, m_s5_w_glu, m_s5_b_glu, m_sgu_ln_gain, m_sgu_ln_bias,
           m_sgu_w_s, m_sgu_b_s, m_mlp_w1, m_mlp_w2, v_ada_w, v_ada_b, v_even_w_in, v_even_w_out, v_gla_w_lr, v_gla_b_lr,
           v_gla_gain, v_fox_b_f, v_fox_q_gain, v_fox_k_gain, v_odd_w_in, v_odd_w_out, v_s5_lam_re, v_s5_lam_im, v_s5_log_dt,
           v_s5_b_re, v_s5_b_im, v_s5_c_re, v_s5_c_im, v_s5_d, v_s5_w_glu, v_s5_b_glu, v_sgu_ln_gain, v_sgu_ln_bias, v_sgu_w_s,
           v_sgu_b_s, v_mlp_w1, v_mlp_w2):
    return _step((x, c, ada_w, ada_b, even_w_in, even_w_out, gla_w_lr, gla_b_lr, gla_gain, fox_b_f, fox_q_gain, fox_k_gain,
                  odd_w_in, odd_w_out, s5_lam_re, s5_lam_im, s5_log_dt, s5_b_re, s5_b_im, s5_c_re, s5_c_im, s5_d, s5_w_glu,
                  s5_b_glu, sgu_ln_gain, sgu_ln_bias, sgu_w_s, sgu_b_s, mlp_w1, mlp_w2, loss_target, m_ada_w, m_ada_b,
                  m_even_w_in, m_even_w_out, m_gla_w_lr, m_gla_b_lr, m_gla_gain, m_fox_b_f, m_fox_q_gain, m_fox_k_gain,
                  m_odd_w_in, m_odd_w_out, m_s5_lam_re, m_s5_lam_im, m_s5_log_dt, m_s5_b_re, m_s5_b_im, m_s5_c_re, m_s5_c_im,
                  m_s5_d, m_s5_w_glu, m_s5_b_glu, m_sgu_ln_gain, m_sgu_ln_bias, m_sgu_w_s, m_sgu_b_s, m_mlp_w1, m_mlp_w2, v_ada_w,
                  v_ada_b, v_even_w_in, v_even_w_out, v_gla_w_lr, v_gla_b_lr, v_gla_gain, v_fox_b_f, v_fox_q_gain, v_fox_k_gain,
                  v_odd_w_in, v_odd_w_out, v_s5_lam_re, v_s5_lam_im, v_s5_log_dt, v_s5_b_re, v_s5_b_im, v_s5_c_re, v_s5_c_im,
                  v_s5_d, v_s5_w_glu, v_s5_b_glu, v_sgu_ln_gain, v_sgu_ln_bias, v_sgu_w_s, v_sgu_b_s, v_mlp_w1, v_mlp_w2))
```

```python
import functools
import math

import jax
import jax.numpy as jnp
import numpy as np
from jax import lax
from jax.experimental import pallas as pl
from jax.experimental.pallas import tpu as pltpu

F32 = jnp.float32
BF16 = jnp.bfloat16
MESH = pl.DeviceIdType.MESH
ANY = pl.BlockSpec(memory_space=pl.ANY)
DMA_SEM = pltpu.SemaphoreType.DMA

D_MODEL = 1024
HEAD_DIM = 64
MIX_HALF = 512
GLA_RANK = 16
GLA_TAU = 16.0
GLA_CHUNK = 64
S5_GROUPS = 32
S5_GROUP_WIDTH = 16
S5_STATE = 64
S5_N = S5_GROUPS * S5_STATE
SGU_GROUPS = 8
SGU_CHUNK = 128
D_FF = 4096
EPS = 1e-6
N_CHIPS = 4
LANES = 128
VMEM_LIMIT = 48 * 1024 * 1024
PAIR_COPIES = 16

ADAM_LR = 0.001
ADAM_B1 = 0.9
ADAM_B2 = 0.999
ADAM_EPS = 1e-08
ADAM_WD = 0.01
ADAM_STEP = 10


def _cparams(*sem):
    return pltpu.CompilerParams(dimension_semantics=sem, vmem_limit_bytes=VMEM_LIMIT)


def _pair_swap(x, name):
    lead = x.shape[:-2]
    rows = x.shape[-2]
    nsplit = max(1, PAIR_COPIES // max(1, math.prod(lead)))
    while nsplit > 1 and rows % (nsplit * 16):
        nsplit -= 1
    pieces = [idx + (pl.ds(j * (rows // nsplit), rows // nsplit),) for idx in np.ndindex(*lead) for j in range(nsplit)]

    def body(x_ref, o_ref, send_sems, recv_sems):
        mx, my, mc = lax.axis_index("x"), lax.axis_index("y"), lax.axis_index("c")
        copies = [pltpu.make_async_remote_copy(src_ref=x_ref.at[p], dst_ref=o_ref.at[p], send_sem=send_sems.at[j], recv_sem=recv_sems.at[j],
                                               device_id=(mx, my, 1 - mc), device_id_type=MESH) for j, p in enumerate(pieces)]
        for cp in copies:
            cp.start()
        for cp in copies:
            cp.wait_recv()
        for cp in copies:
            cp.wait_send()

    return pl.pallas_call(
        body, name=name, out_shape=jax.ShapeDtypeStruct(x.shape, x.dtype), in_specs=[ANY], out_specs=ANY,
        scratch_shapes=[DMA_SEM((len(pieces),)), DMA_SEM((len(pieces),))])(x)


def _by_core(mine, theirs):
    first = lax.axis_index("c") == 0
    return jnp.stack([jnp.where(first, mine, theirs), jnp.where(first, theirs, mine)])


def _chip_exchange(x, name, bcast):
    blk = x.shape if bcast else x.shape[1:]

    def body(x_ref, o_ref, send_sems, recv_sems, loc_sem):
        start, finish = _exchange_plan(x_ref, o_ref, send_sems, recv_sems, loc_sem, bcast)
        start()
        finish()

    return pl.pallas_call(
        body, name=name, out_shape=jax.ShapeDtypeStruct((N_CHIPS,) + tuple(blk), x.dtype), in_specs=[ANY], out_specs=ANY,
        scratch_shapes=_EXCHANGE_SEMS)(x)


_EXCHANGE_SEMS = [DMA_SEM((3,)), DMA_SEM((3,)), DMA_SEM]


def _exchange_plan(x_ref, o_ref, send_sems, recv_sems, loc_sem, bcast):
    mx, my, mc = lax.axis_index("x"), lax.axis_index("y"), lax.axis_index("c")
    me = 2 * mx + my
    peers = [(1 - mx, my), (mx, 1 - my), (1 - mx, 1 - my)]

    def src(k):
        return x_ref if bcast else x_ref.at[k]

    def remote(j, source, slot):
        px, py = peers[j]
        return pltpu.make_async_remote_copy(src_ref=source, dst_ref=o_ref.at[slot], send_sem=send_sems.at[j], recv_sem=recv_sems.at[j],
                                            device_id=(px, py, mc), device_id_type=MESH)

    loc = pltpu.make_async_copy(src(me), o_ref.at[me], loc_sem)
    sends = [remote(j, src(2 * px + py), me) for j, (px, py) in enumerate(peers)]
    arrivals = [remote(j, src(me), 2 * px + py) for j, (px, py) in enumerate(peers)]

    def start():
        loc.start()
        for cp in sends:
            cp.start()

    def finish():
        for cp in arrivals:
            cp.wait_recv()
        for cp in sends:
            cp.wait_send()
        loc.wait()

    return start, finish


def _gather8(x, name):
    collected = _chip_exchange(x, name + "_chips", True)
    return jnp.swapaxes(_by_core(collected, _pair_swap(collected, name + "_pair")), 0, 1)


def _tile(n, want):
    if n <= want:
        return n
    t = (want // LANES) * LANES
    while t >= LANES:
        if n % t == 0:
            return t
        t -= LANES
    raise ValueError(f"no lane-aligned tile for {n}")


_DIMS = {"nn": (((1,), (0,)), ((), ())), "nt": (((1,), (1,)), ((), ())), "tn": (((0,), (0,)), ((), ()))}


MM_FULL_K = 4096
MM_SLAB_K = 2048
MM_TILES = ((1024, 1024), (512, 1024), (1024, 512), (512, 512), (256, 512), (256, 256))
MM_VMEM_BUDGET = 36 * 1024 * 1024


def _mm(a, b, mode, name, *, a_pro=None, epi=None, extras=(), out_dtype=F32, tm_max=1024, tn_max=1024, tk=None, a_cols=None):
    c0, csize = a_cols if a_cols is not None else (0, a.shape[1])
    if mode == "tn":
        K, M = a.shape[0], csize
    else:
        M, K = a.shape[0], csize
    N = b.shape[0] if mode == "nt" else b.shape[1]
    assert (b.shape[1] if mode == "nt" else b.shape[0]) == K, (a.shape, b.shape, mode)
    if tk is None:
        tk = K if (mode != "tn" and K <= MM_FULL_K) else MM_SLAB_K
    tk = _tile(K, tk)
    nk = K // tk
    n_mn = sum(1 for _, kind in extras if kind == "mn")
    for tm_want, tn_want in MM_TILES:
        tm, tn = _tile(M, min(tm_want, tm_max)), _tile(N, min(tn_want, tn_max))
        need = 2 * (tm * tk * a.dtype.itemsize + tk * tn * b.dtype.itemsize + tm * tn * 4 * (1 + n_mn)) + tm * tn * 4 * (nk > 1)
        if need <= MM_VMEM_BUDGET:
            break
    if mode == "tn":
        assert c0 % tm == 0
        a_spec = pl.BlockSpec((tk, tm), lambda i, j, k: (k, i + c0 // tm))
    else:
        assert c0 % tk == 0
        a_spec = pl.BlockSpec((tm, tk), lambda i, j, k: (i, k + c0 // tk))
    b_spec = pl.BlockSpec((tn, tk), lambda i, j, k: (j, k)) if mode == "nt" else pl.BlockSpec((tk, tn), lambda i, j, k: (k, j))
    ex_specs = []
    for arr, kind in extras:
        if kind == "mn":
            assert arr.shape == (M, N)
            ex_specs.append(pl.BlockSpec((tm, tn), lambda i, j, k: (i, j)))
        else:
            assert arr.shape == (1, N)
            ex_specs.append(pl.BlockSpec((1, tn), lambda i, j, k: (0, j)))
    n_ex = len(extras)

    def body(*refs):
        a_ref, b_ref = refs[:2]
        ex_refs = refs[2:2 + n_ex]
        o_ref = refs[2 + n_ex]
        acc_ref = refs[3 + n_ex] if nk > 1 else None
        k = pl.program_id(2)
        av = a_ref[...]
        if a_pro is not None:
            av = a_pro(av)
        part = lax.dot_general(av.astype(BF16), b_ref[...].astype(BF16), _DIMS[mode], preferred_element_type=F32)
        if nk == 1:
            if epi is not None:
                part = epi(part, *[r[...] for r in ex_refs])
            o_ref[...] = part.astype(o_ref.dtype)
            return

        @pl.when(k == 0)
        def _():
            acc_ref[...] = part

        @pl.when(k > 0)
        def _():
            acc_ref[...] += part

        @pl.when(k == nk - 1)
        def _():
            acc = acc_ref[...]
            if epi is not None:
                acc = epi(acc, *[r[...] for r in ex_refs])
            o_ref[...] = acc.astype(o_ref.dtype)

    return pl.pallas_call(
        body, name=name, grid=(M // tm, N // tn, nk),
        in_specs=[a_spec, b_spec] + ex_specs,
        out_specs=pl.BlockSpec((tm, tn), lambda i, j, k: (i, j)),
        out_shape=jax.ShapeDtypeStruct((M, N), out_dtype),
        scratch_shapes=[pltpu.VMEM((tm, tn), F32)] if nk > 1 else [],
        compiler_params=_cparams("parallel", "parallel", "arbitrary"))(a, b, *[e[0] for e in extras])


ROWS = 256


def _row_spec(w, ts=ROWS):
    return pl.BlockSpec((ts, w), lambda i: (i, 0))


def _vec_spec(w):
    return pl.BlockSpec((1, w), lambda i: (0, 0))


def _res_rms(x, sc, sh, name, y=None, g=None):
    S, D = x.shape
    has_res = y is not None

    def body(*refs):
        if has_res:
            x_ref, y_ref, g_ref, sc_ref, sh_ref, xo_ref, h_ref = refs
            xv = x_ref[...] + g_ref[...] * y_ref[...]
            xo_ref[...] = xv
        else:
            x_ref, sc_ref, sh_ref, h_ref = refs
            xv = x_ref[...]
        r = lax.rsqrt(jnp.mean(xv * xv, axis=-1, keepdims=True) + EPS)
        h_ref[...] = (xv * r * (1.0 + sc_ref[...]) + sh_ref[...]).astype(BF16)

    row, vec = _row_spec(D), _vec_spec(D)
    if has_res:
        return pl.pallas_call(body, name=name, grid=(S // ROWS,), in_specs=[row, row, vec, vec, vec], out_specs=[row, row],
                              out_shape=[jax.ShapeDtypeStruct((S, D), F32), jax.ShapeDtypeStruct((S, D), BF16)],
                              compiler_params=_cparams("parallel"))(x, y, g, sc, sh)
    h = pl.pallas_call(body, name=name, grid=(S // ROWS,), in_specs=[row, vec, vec], out_specs=row,
                       out_shape=jax.ShapeDtypeStruct((S, D), BF16), compiler_params=_cparams("parallel"))(x, sc, sh)
    return x, h


def _res_rms_bwd(x, dh, sc, dres, name, y=None, g=None):
    S, D = x.shape
    has_res = y is not None

    def body(*refs):
        if has_res:
            x_ref, dh_ref, sc_ref, dres_ref, y_ref, g_ref, dx_ref, dy_ref, dg_ref, dsc_ref, dsh_ref = refs
        else:
            x_ref, dh_ref, sc_ref, dres_ref, dx_ref, dsc_ref, dsh_ref = refs
        first = pl.program_id(0) == 0
        xv = x_ref[...]
        dh = dh_ref[...]
        r = lax.rsqrt(jnp.mean(xv * xv, axis=-1, keepdims=True) + EPS)
        xn = xv * r
        dxn = dh * (1.0 + sc_ref[...])
        dx = dres_ref[...] + r * (dxn - xn * jnp.mean(dxn * xn, axis=-1, keepdims=True))
        dx_ref[...] = dx
        parts = [(dsc_ref, jnp.sum(dh * xn, axis=0, keepdims=True)), (dsh_ref, jnp.sum(dh, axis=0, keepdims=True))]
        if has_res:
            dy_ref[...] = (dx * g_ref[...]).astype(BF16)
            parts.append((dg_ref, jnp.sum(dx * y_ref[...], axis=0, keepdims=True)))
        for ref, val in parts:
            @pl.when(first)
            def _(ref=ref, val=val):
                ref[...] = val

            @pl.when(jnp.logical_not(first))
            def _(ref=ref, val=val):
                ref[...] += val

    row, vec = _row_spec(D), _vec_spec(D)
    full = jax.ShapeDtypeStruct((S, D), F32)
    v = jax.ShapeDtypeStruct((1, D), F32)
    if has_res:
        return pl.pallas_call(body, name=name, grid=(S // ROWS,), in_specs=[row, row, vec, row, row, vec],
                              out_specs=[row, row, vec, vec, vec], out_shape=[full, jax.ShapeDtypeStruct((S, D), BF16), v, v, v],
                              compiler_params=_cparams("arbitrary"))(x, dh, sc, dres, y, g)
    return pl.pallas_call(body, name=name, grid=(S // ROWS,), in_specs=[row, row, vec, row],
                          out_specs=[row, vec, vec], out_shape=[full, v, v],
                          compiler_params=_cparams("arbitrary"))(x, dh, sc, dres)


def _res_loss(x, m, g, target, name):
    S, D = x.shape

    def body(x_ref, m_ref, g_ref, t_ref, loss_ref, dx_ref, dm_ref, dg_ref):
        first = pl.program_id(0) == 0
        mv = m_ref[...]
        err = x_ref[...] + g_ref[...] * mv - t_ref[...]
        dx = err * (1.0 / D)
        dx_ref[...] = dx
        dm_ref[...] = (dx * g_ref[...]).astype(BF16)
        part = 0.5 * jnp.sum(jnp.mean(err * err, axis=-1, keepdims=True), axis=0, keepdims=True)
        dg = jnp.sum(dx * mv, axis=0, keepdims=True)

        @pl.when(first)
        def _():
            loss_ref[...] = jnp.broadcast_to(part, loss_ref.shape)
            dg_ref[...] = dg

        @pl.when(jnp.logical_not(first))
        def _():
            loss_ref[...] += jnp.broadcast_to(part, loss_ref.shape)
            dg_ref[...] += dg

    row, vec = _row_spec(D), _vec_spec(D)
    full = jax.ShapeDtypeStruct((S, D), F32)
    return pl.pallas_call(body, name=name, grid=(S // ROWS,), in_specs=[row, row, vec, row],
                          out_specs=[pl.BlockSpec((8, LANES), lambda i: (0, 0)), row, row, vec],
                          out_shape=[jax.ShapeDtypeStruct((8, LANES), F32), full, jax.ShapeDtypeStruct((S, D), BF16), jax.ShapeDtypeStruct((1, D), F32)],
                          compiler_params=_cparams("arbitrary"))(x, m, g, target)


def _adamw(w, g, m, v, name):
    R, C = w.shape
    tr = R if R <= 256 else 256
    assert R % tr == 0

    def body(w_ref, g_ref, m_ref, v_ref, d_ref, nm_ref, nv_ref):
        gv = g_ref[...]
        nm = ADAM_B1 * m_ref[...] + (1.0 - ADAM_B1) * gv
        nv = ADAM_B2 * v_ref[...] + (1.0 - ADAM_B2) * jnp.square(gv)
        m_hat = nm / (1.0 - ADAM_B1 ** ADAM_STEP)
        v_hat = nv / (1.0 - ADAM_B2 ** ADAM_STEP)
        d_ref[...] = -ADAM_LR * (m_hat / (jnp.sqrt(v_hat) + ADAM_EPS) + ADAM_WD * w_ref[...])
        nm_ref[...] = nm
        nv_ref[...] = nv

    spec = pl.BlockSpec((tr, C), lambda i: (i, 0))
    out = jax.ShapeDtypeStruct((R, C), F32)
    return pl.pallas_call(body, name=name, grid=(R // tr,), in_specs=[spec] * 4, out_specs=[spec] * 3,
                          out_shape=[out, out, out], compiler_params=_cparams("parallel"))(w, g, m, v)


def _sum_slots(x, name):
    n, R, C = x.shape
    tr = R if R <= 256 else 256
    assert R % tr == 0

    def body(x_ref, o_ref):
        acc = x_ref[0].astype(F32)
        for j in range(1, n):
            acc = acc + x_ref[j].astype(F32)
        o_ref[...] = acc

    return pl.pallas_call(body, name=name, grid=(R // tr,), in_specs=[pl.BlockSpec((n, tr, C), lambda i: (0, i, 0))],
                          out_specs=pl.BlockSpec((tr, C), lambda i: (i, 0)), out_shape=jax.ShapeDtypeStruct((R, C), F32),
                          compiler_params=_cparams("parallel"))(x)


def _ew(fn, name, tiled, consts=(), outs=(), sums=(), ts=ROWS):
    tiled = [t if isinstance(t, tuple) else (t, t.shape[1], 0) for t in tiled]
    S = tiled[0][0].shape[0]
    n_t, n_c, n_o, n_s = len(tiled), len(consts), len(outs), len(sums)

    def body(*refs):
        ins = [r[...] for r in refs[:n_t + n_c]]
        res = fn(*ins)
        res = res if isinstance(res, (tuple, list)) else (res,)
        assert len(res) == n_o + n_s
        o_refs = refs[n_t + n_c:]
        for r, val in zip(o_refs[:n_o], res[:n_o]):
            r[...] = val.astype(r.dtype)
        first = pl.program_id(0) == 0
        for r, val in zip(o_refs[n_o:], res[n_o:]):
            @pl.when(first)
            def _(r=r, val=val):
                r[...] = val

            @pl.when(jnp.logical_not(first))
            def _(r=r, val=val):
                r[...] += val

    in_specs = [pl.BlockSpec((ts, w), lambda i, cb=cb: (i, cb)) for _, w, cb in tiled]
    in_specs += [pl.BlockSpec(c.shape, lambda i, nd=c.ndim: (0,) * nd) for c in consts]
    out_specs = [_row_spec(w, ts) for w, _ in outs] + [_vec_spec(w) for w in sums]
    out_shape = [jax.ShapeDtypeStruct((S, w), dt) for w, dt in outs] + [jax.ShapeDtypeStruct((1, w), F32) for w in sums]
    res = pl.pallas_call(body, name=name, grid=(S // ts,), in_specs=in_specs, out_specs=out_specs, out_shape=out_shape,
                         compiler_params=_cparams("arbitrary" if sums else "parallel"))(*[t[0] for t in tiled], *consts)
    return res


_GELU_C = math.sqrt(2.0 / math.pi)


def _gelu(x):
    return 0.5 * x * (1.0 + jnp.tanh(_GELU_C * (x + 0.044715 * x * x * x)))


def _dgelu(x):
    t = jnp.tanh(_GELU_C * (x + 0.044715 * x * x * x))
    return 0.5 * (1.0 + t) + 0.5 * x * (1.0 - t * t) * _GELU_C * (1.0 + 3.0 * 0.044715 * x * x)


def _sigmoid(x):
    return 1.0 / (1.0 + jnp.exp(-x))


def _log_sigmoid(x):
    return jnp.minimum(x, 0.0) - jnp.log(1.0 + jnp.exp(-jnp.abs(x)))


SCAN_T = 128
SCAN_TB = 512


def _cmul(ar, ai, br, bi):
    return ar * br - ai * bi, ar * bi + ai * br


def _s5_discretise(lam_re, lam_im, log_dt, b_re, b_im):
    dt = jnp.exp(log_dt)[:, None]
    mag = jnp.exp(lam_re * dt)
    ang = lam_im * dt
    abar_re = mag * jnp.cos(ang)
    abar_im = mag * jnp.sin(ang)
    den = lam_re * lam_re + lam_im * lam_im
    coef_re = ((abar_re - 1.0) * lam_re + abar_im * lam_im) / den
    coef_im = (abar_im * lam_re - (abar_re - 1.0) * lam_im) / den
    bbar_re = coef_re[..., None] * b_re - coef_im[..., None] * b_im
    bbar_im = coef_re[..., None] * b_im + coef_im[..., None] * b_re
    return abar_re, abar_im, bbar_re, bbar_im


def _planes(re, im):
    lead = re.shape[:-1]
    return jnp.stack([re.reshape(lead + (-1, LANES)), im.reshape(lead + (-1, LANES))], axis=-2).reshape(lead + (-1,))


def _unplanes(x):
    lead = x.shape[:-1]
    x4 = x.reshape(lead + (-1, 2, LANES))
    return x4[..., 0, :].reshape(lead + (-1,)), x4[..., 1, :].reshape(lead + (-1,))


def _s5_scan_tables(a_re, a_im, reverse):
    pr, pi = [a_re], [a_im]
    for _ in range(7):
        r, i = _cmul(pr[-1], pi[-1], pr[-1], pi[-1])
        pr.append(r)
        pi.append(i)
    apow = _planes(jnp.stack(pr), jnp.stack(pi))
    n = np.arange(1, SCAN_T + 1)
    if reverse:
        n = n[::-1]
    tr = jnp.ones((SCAN_T, a_re.shape[0]), F32)
    ti = jnp.zeros((SCAN_T, a_re.shape[0]), F32)
    for k in range(8):
        bit = jnp.asarray(((n >> k) & 1).astype(np.float32))[:, None]
        mr = bit * pr[k][None, :] + (1.0 - bit)
        mi = bit * pi[k][None, :]
        tr, ti = _cmul(tr, ti, mr, mi)
    return apow, _planes(tr, ti)


def _s5_scan(bu, apow, ptab, name, reverse, x_fwd=None):
    S, N2 = bu.shape
    T, W = SCAN_T, 2 * LANES
    tb = min(SCAN_TB, S)
    nt, nsub = S // tb, tb // T
    order = list(range(nsub - 1, -1, -1) if reverse else range(nsub))
    with_da = x_fwd is not None

    def tblk(t):
        return (nt - 1 - t) if reverse else t

    def shifted(v, k, rowi):
        s = 1 << k
        if reverse:
            return jnp.where(rowi < T - s, pltpu.roll(v, T - s, 0), 0.0)
        return jnp.where(rowi >= s, pltpu.roll(v, s, 0), 0.0)

    def body(*refs):
        if with_da:
            bu_ref, ap_ref, pt_ref, xf_ref, xp_ref, x_ref, da_ref, carry_ref = refs
        else:
            bu_ref, ap_ref, pt_ref, x_ref, carry_ref = refs
        t = pl.program_id(1)

        @pl.when(t == 0)
        def _():
            carry_ref[...] = jnp.zeros_like(carry_ref)
            if with_da:
                da_ref[...] = jnp.zeros_like(da_ref)

        rowi = lax.broadcasted_iota(jnp.int32, (T, LANES), 0)
        pr, pi = pt_ref[:, :LANES], pt_ref[:, LANES:]
        cr, ci = carry_ref[0:1, :LANES], carry_ref[0:1, LANES:]
        for sb in order:
            rows = pl.ds(sb * T, T)
            xr, xi = bu_ref[rows, :LANES], bu_ref[rows, LANES:]
            for k in range(7):
                ar, ai = ap_ref[k:k + 1, :LANES], ap_ref[k:k + 1, LANES:]
                s = 1 << k
                if s < 8:
                    rr, ri = shifted(xr, k, rowi), shifted(xi, k, rowi)
                    xr, xi = xr + ar * rr - ai * ri, xi + ar * ri + ai * rr
                elif reverse:
                    nr, ni = xr[s:], xi[s:]
                    xr = jnp.concatenate([xr[:T - s] + ar * nr - ai * ni, xr[T - s:]], axis=0)
                    xi = jnp.concatenate([xi[:T - s] + ar * ni + ai * nr, xi[T - s:]], axis=0)
                else:
                    nr, ni = xr[:T - s], xi[:T - s]
                    xr = jnp.concatenate([xr[:s], xr[s:] + ar * nr - ai * ni], axis=0)
                    xi = jnp.concatenate([xi[:s], xi[s:] + ar * ni + ai * nr], axis=0)
            xr, xi = xr + pr * cr - pi * ci, xi + pr * ci + pi * cr
            x_ref[rows, :LANES] = xr
            x_ref[rows, LANES:] = xi
            edge = pl.ds(sb * T + (0 if reverse else T - 1), 1)
            cr, ci = x_ref[edge, :LANES], x_ref[edge, LANES:]
            if with_da:
                if sb > 0:
                    before = pl.ds(sb * T - 1, 1)
                    b_r, b_i = xf_ref[before, :LANES], xf_ref[before, LANES:]
                else:
                    keep = (tblk(t) > 0).astype(F32)
                    b_r, b_i = xp_ref[7:8, :LANES] * keep, xp_ref[7:8, LANES:] * keep
                fr, fi = xf_ref[rows, :LANES], xf_ref[rows, LANES:]
                qr = jnp.where(rowi >= 1, pltpu.roll(fr, 1, 0), b_r)
                qi = jnp.where(rowi >= 1, pltpu.roll(fi, 1, 0), b_i)
                gr, gi = xr * qr + xi * qi, xi * qr - xr * qi
                sr, si = gr[0:8], gi[0:8]
                for j in range(1, T // 8):
                    sr, si = sr + gr[8 * j:8 * j + 8], si + gi[8 * j:8 * j + 8]
                da_ref[:, :LANES] += sr
                da_ref[:, LANES:] += si
        carry_ref[0:1, :LANES] = cr
        carry_ref[0:1, LANES:] = ci

    blk = pl.BlockSpec((tb, W), lambda j, t: (tblk(t), j))
    in_specs = [blk, pl.BlockSpec((8, W), lambda j, t: (0, j)), pl.BlockSpec((T, W), lambda j, t: (0, j))]
    out_specs, out_shape = [blk], [jax.ShapeDtypeStruct((S, N2), F32)]
    args = [bu, apow, ptab]
    if with_da:
        in_specs += [blk, pl.BlockSpec((8, W), lambda j, t: (jnp.maximum(tblk(t) * (tb // 8) - 1, 0), j))]
        out_specs.append(pl.BlockSpec((8, W), lambda j, t: (0, j)))
        out_shape.append(jax.ShapeDtypeStruct((8, N2), F32))
        args += [x_fwd, x_fwd]
    res = pl.pallas_call(body, name=name, grid=(N2 // W, nt), in_specs=in_specs, out_specs=out_specs, out_shape=out_shape,
                         scratch_shapes=[pltpu.VMEM((8, W), F32)], compiler_params=_cparams("parallel", "arbitrary"))(*args)
    return res if with_da else res[0]


S5_BAND = 4


def _mm_band(a, b, name, *, b_t=False, outer=False, epi=None, extras=(), tm=512, tk=2048):
    S = a.shape[0]
    wa = a.shape[1] // S5_BAND
    if outer:
        wb = b.shape[1] // S5_BAND
        tk = _tile(S, tk)
        nk = S // tk

        def obody(a_ref, b_ref, o_ref, acc_ref):
            k = pl.program_id(1)
            part = lax.dot_general(a_ref[...].astype(BF16), b_ref[...].astype(BF16), TN_DIMS, preferred_element_type=F32)

            @pl.when(k == 0)
            def _():
                acc_ref[...] = part

            @pl.when(k > 0)
            def _():
                acc_ref[...] += part

            @pl.when(k == nk - 1)
            def _():
                o_ref[...] = acc_ref[...]

        return pl.pallas_call(
            obody, name=name, grid=(S5_BAND, nk),
            in_specs=[pl.BlockSpec((tk, wa), lambda c, k: (k, c)), pl.BlockSpec((tk, wb), lambda c, k: (k, c))],
            out_specs=pl.BlockSpec((wa, wb), lambda c, k: (c, 0)), out_shape=jax.ShapeDtypeStruct((a.shape[1], wb), F32),
            scratch_shapes=[pltpu.VMEM((wa, wb), F32)], compiler_params=_cparams("parallel", "arbitrary"))(a, b)

    wo = (b.shape[0] if b_t else b.shape[1]) // S5_BAND
    tm = _tile(S, tm)
    ex_specs = [pl.BlockSpec((tm, wo), lambda i, c: (i, c)) if kind == "mn" else pl.BlockSpec((1, wo), lambda i, c: (0, c))
                for _, kind in extras]

    def body(a_ref, b_ref, *refs):
        part = lax.dot_general(a_ref[...].astype(BF16), b_ref[...].astype(BF16), NT_DIMS if b_t else _DIMS["nn"], preferred_element_type=F32)
        if epi is not None:
            part = epi(part, *[r[...] for r in refs[:-1]])
        refs[-1][...] = part

    b_spec = pl.BlockSpec((wo, wa) if b_t else (wa, wo), lambda i, c: (c, c))
    return pl.pallas_call(
        body, name=name, grid=(S // tm, S5_BAND), in_specs=[pl.BlockSpec((tm, wa), lambda i, c: (i, c)), b_spec] + ex_specs,
        out_specs=pl.BlockSpec((tm, wo), lambda i, c: (i, c)), out_shape=jax.ShapeDtypeStruct((S, S5_BAND * wo), F32),
        compiler_params=_cparams("parallel", "parallel"))(a, b, *[e[0] for e in extras])


def _band_to_full(blocks, cols):
    wa, wb = blocks.shape[0] // S5_BAND, blocks.shape[1]
    return jnp.concatenate([jnp.pad(blocks[k * wa:(k + 1) * wa], ((0, 0), (k * wb, cols - (k + 1) * wb))) for k in range(S5_BAND)], axis=0)


def _block_diag(t):
    G, a, b = t.shape
    return (t[:, :, None, :] * jnp.eye(G, dtype=t.dtype)[:, None, :, None]).reshape(G * a, G * b)


def _block_diag_take(m, G):
    a, b = m.shape[0] // G, m.shape[1] // G
    m4 = m.reshape(G, a, G, b)
    return jnp.sum(m4 * jnp.eye(G, dtype=m.dtype)[:, None, :, None], axis=2)


def _s5_block_fwd(u, w, pfx):
    a_re, a_im, bb_re, bb_im = _s5_discretise(w["lam_re"], w["lam_im"], w["log_dt"], w["b_re"], w["b_im"])
    bcat = _planes(_block_diag(bb_re).T, _block_diag(bb_im).T).astype(BF16)
    ccat = _planes(_block_diag(jnp.swapaxes(w["c_re"], 1, 2)).T, -_block_diag(jnp.swapaxes(w["c_im"], 1, 2)).T).T.astype(BF16)
    af_re, af_im = a_re.reshape(-1), a_im.reshape(-1)
    apow, ptab = _s5_scan_tables(af_re, af_im, False)
    bu = _mm_band(u, bcat, pfx + "_bu")
    x = _s5_scan(bu, apow, ptab, pfx + "_scan", False)
    d_row = w["d"].reshape(1, MIX_HALF)
    ys = _mm_band(x, ccat, pfx + "_y", epi=lambda acc, ut, dr: acc + dr * ut, extras=[(u, "mn"), (d_row, "n")])
    z = _mm(ys, w["w_glu"], "nn", pfx + "_glu", a_pro=_gelu, epi=lambda acc, b: acc + b, extras=[(w["b_glu"].reshape(1, -1), "n")])
    y2, = _ew(lambda ysv, zv: _gelu(ysv) * _sigmoid(zv), pfx + "_gate", [ys, z], outs=[(MIX_HALF, F32)])
    return y2, dict(u=u, x=x, ys=ys, z=z, bcat=bcat, ccat=ccat, a=(af_re, af_im), d_row=d_row)


def _s5_block_bwd(dy2, w, res, pfx):
    u, x, ys, z, bcat, ccat = res["u"], res["x"], res["ys"], res["z"], res["bcat"], res["ccat"]

    def gate_bwd(dy, ysv, zv):
        sg = _sigmoid(zv)
        dz = dy * _gelu(ysv) * sg * (1.0 - sg)
        return dz, jnp.sum(dz, axis=0, keepdims=True)

    dz, db_glu = _ew(gate_bwd, pfx + "_gate_bwd", [dy2, ys, z], outs=[(MIX_HALF, F32)], sums=[MIX_HALF])
    dw_glu = _mm(ys, dz, "tn", pfx + "_dwglu", a_pro=_gelu)
    dys = _mm(dz, w["w_glu"], "nt", pfx + "_dys", epi=lambda acc, dy, zv, ysv: (acc + dy * _sigmoid(zv)) * _dgelu(ysv),
              extras=[(dy2, "mn"), (z, "mn"), (ys, "mn")])
    dd, = _ew(lambda a, b: jnp.sum(a * b, axis=0, keepdims=True), pfx + "_dd", [dys, u], sums=[MIX_HALF])
    dccat = _band_to_full(_mm_band(x, dys, pfx + "_dc", outer=True), MIX_HALF)
    dx = _mm_band(dys, ccat, pfx + "_dx", b_t=True)
    af_re, af_im = res["a"]
    apow, ptab = _s5_scan_tables(af_re, -af_im, True)
    lam, da8 = _s5_scan(dx, apow, ptab, pfx + "_scan_bwd", True, x_fwd=x)
    dbcat = _band_to_full(_mm_band(u, lam, pfx + "_db", outer=True), 2 * S5_N)
    du = _mm_band(lam, bcat, pfx + "_du", b_t=True, epi=lambda acc, dyv, dr: acc + dyv * dr, extras=[(dys, "mn"), (res["d_row"], "n")])
    G = S5_GROUPS
    d_abar_re, d_abar_im = (t.reshape(G, S5_STATE) for t in _unplanes(jnp.sum(da8, axis=0)))
    d_bb_re, d_bb_im = (_block_diag_take(t.T, G) for t in _unplanes(dbcat))
    _, vjp = jax.vjp(_s5_discretise, w["lam_re"], w["lam_im"], w["log_dt"], w["b_re"], w["b_im"])
    g_lam_re, g_lam_im, g_log_dt, g_b_re, g_b_im = vjp((d_abar_re, d_abar_im, d_bb_re, d_bb_im))
    dc_re, dc_im = _unplanes(dccat.T)
    g_c_re = jnp.swapaxes(_block_diag_take(dc_re.T, G), 1, 2)
    g_c_im = -jnp.swapaxes(_block_diag_take(dc_im.T, G), 1, 2)
    grads = dict(lam_re=g_lam_re, lam_im=g_lam_im, log_dt=g_log_dt, b_re=g_b_re, b_im=g_b_im, c_re=g_c_re, c_im=g_c_im,
                 d=dd.reshape(G, S5_GROUP_WIDTH), w_glu=dw_glu, b_glu=db_glu.reshape(-1))
    return du, grads


SGU_TS = 512
N_PAIRS = MIX_HALF // LANES


def _half_masks(rows):
    lane = lax.broadcasted_iota(jnp.int32, (rows, LANES), 1)
    left = (lane < HEAD_DIM).astype(F32)
    return left, 1.0 - left


def _sgu_norm(zv, gain, bias):
    v = _gelu(zv)
    mu = jnp.mean(v, axis=-1, keepdims=True)
    vc = v - mu
    rstd = lax.rsqrt(jnp.mean(vc * vc, axis=-1, keepdims=True) + EPS)
    vhat = vc * rstd
    return vhat, rstd, vhat * gain + bias


def _sgu_tables(w_s, b_s):
    mask = jnp.tril(jnp.ones((SGU_CHUNK, SGU_CHUNK), dtype=bool))
    wm = jnp.where(mask[None], w_s, 0.0).astype(BF16)
    bias_tab = jnp.repeat(b_s.T, MIX_HALF // SGU_GROUPS, axis=1)
    return wm, bias_tab


def _sgu_fwd(proj, ln_gain, ln_bias, wm, bias_tab, name):
    S = proj.shape[0]
    nch = SGU_TS // SGU_CHUNK

    def body(zu_ref, zv_ref, g_ref, b_ref, w_ref, bt_ref, o_ref):
        left, right = _half_masks(SGU_CHUNK)
        _, _, vn = _sgu_norm(zv_ref[...], g_ref[...], b_ref[...])
        for ch in range(nch):
            rows = pl.ds(ch * SGU_CHUNK, SGU_CHUNK)
            for p in range(N_PAIRS):
                cols = pl.ds(p * LANES, LANES)
                vp = vn[ch * SGU_CHUNK:(ch + 1) * SGU_CHUNK, p * LANES:(p + 1) * LANES]
                mixed = (jnp.dot(w_ref[2 * p], (vp * left).astype(BF16), preferred_element_type=F32)
                         + jnp.dot(w_ref[2 * p + 1], (vp * right).astype(BF16), preferred_element_type=F32) + bt_ref[:, cols])
                o_ref[rows, cols] = _gelu(zu_ref[rows, cols]) * mixed

    vec = _vec_spec(MIX_HALF)
    return pl.pallas_call(
        body, name=name, grid=(S // SGU_TS,),
        in_specs=[pl.BlockSpec((SGU_TS, MIX_HALF), lambda i: (i, 1)), pl.BlockSpec((SGU_TS, MIX_HALF), lambda i: (i, 2)), vec, vec,
                  pl.BlockSpec((SGU_GROUPS, SGU_CHUNK, SGU_CHUNK), lambda i: (0, 0, 0)), pl.BlockSpec((SGU_CHUNK, MIX_HALF), lambda i: (0, 0))],
        out_specs=_row_spec(MIX_HALF, SGU_TS), out_shape=jax.ShapeDtypeStruct((S, MIX_HALF), F32),
        compiler_params=_cparams("parallel"))(proj, proj, ln_gain, ln_bias, wm, bias_tab)


def _sgu_bwd(dout, proj, ln_gain, ln_bias, wm, bias_tab, name):
    S = proj.shape[0]
    nch = SGU_TS // SGU_CHUNK
    nt_dims = (((1,), (1,)), ((), ()))
    tn_dims = (((0,), (0,)), ((), ()))

    def body(do_ref, zu_ref, zv_ref, g_ref, b_ref, w_ref, bt_ref, dzu_ref, dzv_ref, dw_ref, dbt_ref, dg_ref, db_ref, dvn_ref):
        first = pl.program_id(0) == 0

        @pl.when(first)
        def _():
            dw_ref[...] = jnp.zeros_like(dw_ref)
            dbt_ref[...] = jnp.zeros_like(dbt_ref)
            dg_ref[...] = jnp.zeros_like(dg_ref)
            db_ref[...] = jnp.zeros_like(db_ref)

        left, right = _half_masks(SGU_CHUNK)
        zv = zv_ref[...]
        vhat, rstd, vn = _sgu_norm(zv, g_ref[...], b_ref[...])
        for ch in range(nch):
            rows = pl.ds(ch * SGU_CHUNK, SGU_CHUNK)
            for p in range(N_PAIRS):
                cols = pl.ds(p * LANES, LANES)
                vp = vn[ch * SGU_CHUNK:(ch + 1) * SGU_CHUNK, p * LANES:(p + 1) * LANES]
                vl, vr = (vp * left).astype(BF16), (vp * right).astype(BF16)
                mixed = (jnp.dot(w_ref[2 * p], vl, preferred_element_type=F32)
                         + jnp.dot(w_ref[2 * p + 1], vr, preferred_element_type=F32) + bt_ref[:, cols])
                zu = zu_ref[rows, cols]
                do = do_ref[rows, cols]
                dzu_ref[rows, cols] = do * mixed * _dgelu(zu)
                dmix = do * _gelu(zu)
                dbt_ref[:, cols] += dmix
                dl, dr = (dmix * left).astype(BF16), (dmix * right).astype(BF16)
                dw_ref[2 * p] += lax.dot_general(dl, vl, nt_dims, preferred_element_type=F32)
                dw_ref[2 * p + 1] += lax.dot_general(dr, vr, nt_dims, preferred_element_type=F32)
                dvn_ref[rows, cols] = (lax.dot_general(w_ref[2 * p], dl, tn_dims, preferred_element_type=F32)
                                       + lax.dot_general(w_ref[2 * p + 1], dr, tn_dims, preferred_element_type=F32))
        dvn = dvn_ref[...]
        dg_ref[...] += jnp.sum(dvn * vhat, axis=0, keepdims=True)
        db_ref[...] += jnp.sum(dvn, axis=0, keepdims=True)
        dvh = dvn * g_ref[...]
        dv = rstd * (dvh - jnp.mean(dvh, axis=-1, keepdims=True) - vhat * jnp.mean(dvh * vhat, axis=-1, keepdims=True))
        dzv_ref[...] = dv * _dgelu(zv)

    vec = _vec_spec(MIX_HALF)
    row = _row_spec(MIX_HALF, SGU_TS)
    wspec = pl.BlockSpec((SGU_GROUPS, SGU_CHUNK, SGU_CHUNK), lambda i: (0, 0, 0))
    tspec = pl.BlockSpec((SGU_CHUNK, MIX_HALF), lambda i: (0, 0))
    full = jax.ShapeDtypeStruct((S, MIX_HALF), F32)
    v = jax.ShapeDtypeStruct((1, MIX_HALF), F32)
    return pl.pallas_call(
        body, name=name, grid=(S // SGU_TS,),
        in_specs=[row, pl.BlockSpec((SGU_TS, MIX_HALF), lambda i: (i, 1)), pl.BlockSpec((SGU_TS, MIX_HALF), lambda i: (i, 2)), vec, vec,
                  wspec, tspec],
        out_specs=[row, row, wspec, tspec, vec, vec],
        out_shape=[full, full, jax.ShapeDtypeStruct((SGU_GROUPS, SGU_CHUNK, SGU_CHUNK), F32),
                   jax.ShapeDtypeStruct((SGU_CHUNK, MIX_HALF), F32), v, v],
        scratch_shapes=[pltpu.VMEM((SGU_TS, MIX_HALF), F32)],
        compiler_params=_cparams("arbitrary"))(dout, proj, proj, ln_gain, ln_bias, wm, bias_tab)


def _sgu_grads(dw, dbias_tab):
    mask = jnp.tril(jnp.ones((SGU_CHUNK, SGU_CHUNK), dtype=bool))
    g_w = jnp.where(mask[None], dw, 0.0)
    g_b = dbias_tab.reshape(SGU_CHUNK, SGU_GROUPS, MIX_HALF // SGU_GROUPS).sum(axis=-1).T
    return g_w, g_b


def _head_avg_matrix(w):
    idx = np.arange(w) // HEAD_DIM
    return jnp.asarray((idx[:, None] == idx[None, :]).astype(np.float32) / HEAD_DIM, dtype=BF16)


def _head_mean(t, bavg):
    hi = t.astype(BF16)
    lo = (t - hi.astype(F32)).astype(BF16)
    return jnp.dot(hi, bavg, preferred_element_type=F32) + jnp.dot(lo, bavg, preferred_element_type=F32)


def _head_rms(t, bavg):
    r = lax.rsqrt(_head_mean(t * t, bavg) + EPS)
    return t * r, r


def _head_rms_bwd(dn, n, r, bavg):
    return r * (dn - n * _head_mean(dn * n, bavg))


GLA_TS = 512
C = GLA_CHUNK
NT_DIMS = (((1,), (1,)), ((), ()))
TN_DIMS = (((0,), (0,)), ((), ()))
HI = lax.Precision.HIGHEST


def _bdot(a, b, dims=(((1,), (0,)), ((), ()))):
    return lax.dot_general(a.astype(BF16), b.astype(BF16), dims, preferred_element_type=F32)


def _gla_chunk_terms(q, k, z):
    row = lax.broadcasted_iota(jnp.int32, (C, C), 0)
    col = lax.broadcasted_iota(jnp.int32, (C, C), 1)
    lc = _log_sigmoid(z) * (1.0 / GLA_TAU)
    b = lax.dot_general((row >= col).astype(F32), lc, (((1,), (0,)), ((), ())), precision=HI, preferred_element_type=F32)
    b_last = jnp.sum(lc, axis=0, keepdims=True)
    b_mid = b[C // 2:C // 2 + 1, :]
    scale = HEAD_DIM ** -0.5
    e_b, e_q, e_k, e_l = jnp.exp(b), jnp.exp(b - b_mid), jnp.exp(b_mid - b), jnp.exp(b_last - b)
    qs = q * (scale * e_b)
    qe = q * (scale * e_q)
    ke = k * e_k
    kl = k * e_l
    return dict(e_b=e_b, e_q=e_q, e_k=e_k, e_l=e_l, qs=qs, qe=qe, ke=ke, kl=kl, dec=jnp.exp(b_last), causal=row >= col, scale=scale)


def _pair(x, pp):
    return x[:, pp * LANES:(pp + 1) * LANES]


def _pair_block_diag():
    r = lax.broadcasted_iota(jnp.int32, (LANES, LANES), 0) // HEAD_DIM
    c = lax.broadcasted_iota(jnp.int32, (LANES, LANES), 1) // HEAD_DIM
    return (r == c).astype(F32)


def _gla_fwd(proj, z, name):
    S = proj.shape[0]
    nch = GLA_TS // C

    def body(q_ref, k_ref, v_ref, z_ref, o_ref, st_ref, state_ref):
        @pl.when(pl.program_id(0) == 0)
        def _():
            state_ref[...] = jnp.zeros_like(state_ref)

        left, right = _half_masks(C)
        bd = _pair_block_diag()
        pairs = range(N_PAIRS)
        for ch in range(nch):
            rows = pl.ds(ch * C, C)
            v = v_ref[rows, :]
            t = _gla_chunk_terms(q_ref[rows, :], k_ref[rows, :], z_ref[rows, :])
            sts = [state_ref[pp] for pp in pairs]
            for pp in pairs:
                st_ref[ch, pp] = sts[pp]
            os = [_bdot(_pair(t["qs"], pp), sts[pp], NT_DIMS) for pp in pairs]
            for m in (left, right):
                scores = [jnp.where(t["causal"], _bdot(_pair(t["qe"], pp) * m, _pair(t["ke"], pp), NT_DIMS), 0.0) for pp in pairs]
                os = [os[pp] + m * _bdot(scores[pp], _pair(v, pp)) for pp in pairs]
            o_ref[rows, :] = jnp.concatenate(os, axis=1)
            new = [sts[pp] * _pair(t["dec"], pp) + bd * _bdot(_pair(v, pp), _pair(t["kl"], pp), TN_DIMS) for pp in pairs]
            for pp in pairs:
                state_ref[pp] = new[pp]

    def col(cb):
        return pl.BlockSpec((GLA_TS, MIX_HALF), lambda i: (i, cb))

    return pl.pallas_call(
        body, name=name, grid=(S // GLA_TS,),
        in_specs=[col(0), col(1), col(2), col(0)],
        out_specs=[col(0), pl.BlockSpec((nch, N_PAIRS, LANES, LANES), lambda i: (i, 0, 0, 0))],
        out_shape=[jax.ShapeDtypeStruct((S, MIX_HALF), F32), jax.ShapeDtypeStruct((S // C, N_PAIRS, LANES, LANES), F32)],
        scratch_shapes=[pltpu.VMEM((N_PAIRS, LANES, LANES), F32)], compiler_params=_cparams("arbitrary"))(proj, proj, proj, z)


def _gla_bwd(do, proj, z, states, name):
    S = proj.shape[0]
    nch = GLA_TS // C
    nblk = S // GLA_TS

    def body(do_ref, q_ref, k_ref, v_ref, z_ref, st_ref, dq_ref, dk_ref, dv_ref, dlc_ref, dstate_ref):
        @pl.when(pl.program_id(0) == 0)
        def _():
            dstate_ref[...] = jnp.zeros_like(dstate_ref)

        left, right = _half_masks(C)
        bd = _pair_block_diag()
        rowi = lax.broadcasted_iota(jnp.int32, (C, LANES), 0)
        row = lax.broadcasted_iota(jnp.int32, (C, C), 0)
        colm = lax.broadcasted_iota(jnp.int32, (C, C), 1)
        pairs = range(N_PAIRS)
        rowi = lax.broadcasted_iota(jnp.int32, (C, MIX_HALF), 0)
        for ch in range(nch - 1, -1, -1):
            rows = pl.ds(ch * C, C)
            v, dov = v_ref[rows, :], do_ref[rows, :]
            t = _gla_chunk_terms(q_ref[rows, :], k_ref[rows, :], z_ref[rows, :])
            sts = [st_ref[ch, pp] for pp in pairs]
            nxt = [dstate_ref[pp] for pp in pairs]
            gs = [bd * nxt[pp] for pp in pairs]
            dqs = [_bdot(_pair(dov, pp), sts[pp]) for pp in pairs]
            dv = [_bdot(_pair(t["kl"], pp), gs[pp], NT_DIMS) for pp in pairs]
            dkl = [_bdot(_pair(v, pp), gs[pp]) for pp in pairs]
            dqe = [jnp.zeros((C, LANES), F32) for _ in pairs]
            dke = [jnp.zeros((C, LANES), F32) for _ in pairs]
            for m in (left, right):
                sc = [jnp.where(t["causal"], _bdot(_pair(t["qe"], pp) * m, _pair(t["ke"], pp), NT_DIMS), 0.0) for pp in pairs]
                dsc = [jnp.where(t["causal"], _bdot(_pair(dov, pp) * m, _pair(v, pp), NT_DIMS), 0.0) for pp in pairs]
                dv = [dv[pp] + m * _bdot(sc[pp], _pair(dov, pp), TN_DIMS) for pp in pairs]
                dqe = [dqe[pp] + m * _bdot(dsc[pp], _pair(t["ke"], pp)) for pp in pairs]
                dke = [dke[pp] + m * _bdot(dsc[pp], _pair(t["qe"], pp), TN_DIMS) for pp in pairs]
            for pp in pairs:
                dstate_ref[pp] = bd * (nxt[pp] * _pair(t["dec"], pp) + _bdot(_pair(dov, pp), _pair(t["qs"], pp), TN_DIMS))
            decay_sum = jnp.concatenate([jnp.sum(nxt[pp] * sts[pp], axis=0, keepdims=True) for pp in pairs], axis=1)
            dqs, dv, dkl, dqe, dke = (jnp.concatenate(parts, axis=1) for parts in (dqs, dv, dkl, dqe, dke))
            db_last = decay_sum * t["dec"] + jnp.sum(dkl * t["kl"], axis=0, keepdims=True)
            db = dqs * t["qs"] + dqe * t["qe"] - dke * t["ke"] - dkl * t["kl"]
            db = db + jnp.where(rowi == C - 1, db_last, 0.0)
            dq_ref[rows, :] = (dqs * t["e_b"] + dqe * t["e_q"]) * t["scale"]
            dk_ref[rows, :] = dke * t["e_k"] + dkl * t["e_l"]
            dv_ref[rows, :] = dv
            dlc_ref[rows, :] = lax.dot_general((colm >= row).astype(F32), db, (((1,), (0,)), ((), ())), precision=HI,
                                               preferred_element_type=F32)

    def col(cb):
        return pl.BlockSpec((GLA_TS, MIX_HALF), lambda i: (nblk - 1 - i, cb))

    full = jax.ShapeDtypeStruct((S, MIX_HALF), F32)
    return pl.pallas_call(
        body, name=name, grid=(nblk,),
        in_specs=[col(0), col(0), col(1), col(2), col(0), pl.BlockSpec((nch, N_PAIRS, LANES, LANES), lambda i: (nblk - 1 - i, 0, 0, 0))],
        out_specs=[col(0)] * 4, out_shape=[full, full, full, full],
        scratch_shapes=[pltpu.VMEM((N_PAIRS, LANES, LANES), F32)], compiler_params=_cparams("arbitrary"))(do, proj, proj, proj, z, states)


def _gla_block_fwd(proj, w_lr_pad, b_lr, gain, bavg, pfx):
    z = _mm(proj, w_lr_pad, "nn", pfx + "_z", a_cols=(7 * MIX_HALF, MIX_HALF), epi=lambda acc, b: acc + b, extras=[(b_lr, "n")])
    o, states = _gla_fwd(proj, z, pfx + "_core")

    def out(ov, gg, ba, gn):
        n, _ = _head_rms(ov, ba)
        return n * gn * (gg * _sigmoid(gg))

    og, = _ew(out, pfx + "_out", [o, (proj, MIX_HALF, 3)], consts=[bavg, gain], outs=[(MIX_HALF, F32)])
    return og, dict(z=z, o=o, states=states)


def _gla_block_bwd(dog, proj, w_lr_pad, gain, bavg, res, pfx):
    z, o, states = res["z"], res["o"], res["states"]

    def out_bwd(dy, ov, gg, ba, gn):
        n, r = _head_rms(ov, ba)
        sg = _sigmoid(gg)
        silu = gg * sg
        dn = dy * gn * silu
        do = _head_rms_bwd(dn, n, r, ba)
        dgg = dy * n * gn * (sg * (1.0 + gg * (1.0 - sg)))
        return do, dgg, jnp.sum(dy * n * silu, axis=0, keepdims=True)

    do, dgg, dgain = _ew(out_bwd, pfx + "_out_bwd", [dog, o, (proj, MIX_HALF, 3)], consts=[bavg, gain],
                         outs=[(MIX_HALF, F32), (MIX_HALF, F32)], sums=[MIX_HALF])
    dq, dk, dv, dlc = _gla_bwd(do, proj, z, states, pfx + "_core_bwd")

    def decay_bwd(dl, zv):
        dz = dl * (1.0 / GLA_TAU) * (1.0 - _sigmoid(zv))
        return dz, jnp.sum(dz, axis=0, keepdims=True)

    dz, db_lr = _ew(decay_bwd, pfx + "_decay_bwd", [dlc, z], outs=[(MIX_HALF, F32)], sums=[MIX_HALF])
    dw_lr_pad = _mm(proj, dz, "tn", pfx + "_dwlr", a_cols=(7 * MIX_HALF, MIX_HALF))
    dsmall = _mm(dz, w_lr_pad, "nt", pfx + "_dsmall")
    return (dq, dk, dv, dgg, dsmall), dict(w_lr=dw_lr_pad[:GLA_RANK], b_lr=db_lr.reshape(-1), gain=dgain.reshape(-1, HEAD_DIM))


FOX_T = 512
FOX_HEADS = MIX_HALF // HEAD_DIM
NEG = -1e30
CUM_T = 512


def _cum_lanes(x, name, reverse, pre=None):
    R, S = x.shape
    nb = S // CUM_T

    def body(x_ref, o_ref, carry_ref):
        @pl.when(pl.program_id(0) == 0)
        def _():
            carry_ref[...] = jnp.zeros_like(carry_ref)

        xv = x_ref[...]
        if pre is not None:
            xv = pre(xv)
        i = lax.broadcasted_iota(jnp.int32, (CUM_T, CUM_T), 0)
        j = lax.broadcasted_iota(jnp.int32, (CUM_T, CUM_T), 1)
        tri = ((i >= j) if reverse else (i <= j)).astype(F32)
        c = lax.dot_general(xv, tri, (((1,), (0,)), ((), ())), precision=HI, preferred_element_type=F32)
        carry = carry_ref[...]
        o_ref[...] = c + carry[:, 0:1]
        carry_ref[...] = carry + jnp.sum(xv, axis=1, keepdims=True)

    spec = pl.BlockSpec((R, CUM_T), (lambda i: (0, nb - 1 - i)) if reverse else (lambda i: (0, i)))
    return pl.pallas_call(body, name=name, grid=(nb,), in_specs=[spec], out_specs=spec, out_shape=jax.ShapeDtypeStruct((R, S), F32),
                          scratch_shapes=[pltpu.VMEM((R, LANES), F32)], compiler_params=_cparams("arbitrary"))(x)


def _fox_scores(q, k, cqb, ck_ref, h, m, diag):
    cq = cqb[:, h * HEAD_DIM:h * HEAD_DIM + 1]
    ck = ck_ref[0, h:h + 1, :]
    s = lax.dot_general(q * m.astype(q.dtype), k, NT_DIMS, preferred_element_type=F32) + (cq - ck)
    if not diag:
        return s
    row = lax.broadcasted_iota(jnp.int32, (FOX_T, FOX_T), 0)
    col = lax.broadcasted_iota(jnp.int32, (FOX_T, FOX_T), 1)
    return jnp.where(row < col, NEG, s)


def _on_causal_blocks(q_blk, k_blk, step):
    @pl.when(k_blk < q_blk)
    def _():
        step(False)

    @pl.when(k_blk == q_blk)
    def _():
        step(True)


def _causal_pairs(n, key_major):
    if key_major:
        pairs = [(q, k) for k in range(n) for q in range(k, n)]
    else:
        pairs = [(q, k) for q in range(n) for k in range(q + 1)]
    return jnp.asarray([p[0] for p in pairs], jnp.int32), jnp.asarray([p[1] for p in pairs], jnp.int32)


def _carried(carry, refs, n_in, n_out, first, last):
    if carry is None:
        return refs
    ins, cx_ref, outs, co_ref = refs[:n_in], refs[n_in], refs[n_in + 1:n_in + 1 + n_out], refs[n_in + 1 + n_out]
    scratch = refs[n_in + 2 + n_out:]
    start, finish = _exchange_plan(cx_ref, co_ref, *scratch[-3:], carry[1])
    pl.when(first)(start)
    pl.when(last)(finish)
    return ins + outs + scratch[:-3]


def _carry_specs(carry):
    if carry is None:
        return [], [], [], [], []
    x, bcast = carry
    blk = x.shape if bcast else x.shape[1:]
    return [ANY], [ANY], [jax.ShapeDtypeStruct((N_CHIPS,) + tuple(blk), x.dtype)], list(_EXCHANGE_SEMS), [x]


def _fox_fwd(qn, kn, proj, cum_b, cum_tp, name, carry=None):
    S = qn.shape[0]
    nq = S // FOX_T
    qidx, kidx = _causal_pairs(nq, False)
    ntri = int(qidx.shape[0])

    def body(qidx_ref, kidx_ref, *refs):
        t = pl.program_id(1)
        first = jnp.logical_and(pl.program_id(0) == 0, t == 0)
        last = jnp.logical_and(pl.program_id(0) == N_PAIRS - 1, t == ntri - 1)
        q_ref, k_ref, v_ref, cq_ref, ck_ref, o_ref, lse_ref, m_scr, acc_scr = _carried(carry, refs, 5, 2, first, last)
        qi, ki = qidx_ref[t], kidx_ref[t]

        @pl.when(ki == 0)
        def _():
            m_scr[...] = jnp.full_like(m_scr, NEG)
            acc_scr[...] = jnp.zeros_like(acc_scr)

        left, right = _half_masks(FOX_T)

        def step(diag):
            q, k, v = q_ref[...], k_ref[...], v_ref[...].astype(BF16)
            cqb = cq_ref[...]
            for h, m in enumerate((left, right)):
                s = _fox_scores(q, k, cqb, ck_ref, h, m, diag)
                m_prev = m_scr[h]
                m_new = jnp.maximum(m_prev, jnp.max(s, axis=1, keepdims=True))
                p = jnp.exp(s - m_new)
                v_h = jnp.where(m > 0, v, jnp.ones_like(v))
                acc_scr[h] = jnp.exp(m_prev - m_new) * acc_scr[h] + jnp.dot(p.astype(BF16), v_h, preferred_element_type=F32)
                m_scr[h] = m_new

        _on_causal_blocks(qi, ki, step)

        @pl.when(ki == qi)
        def _():
            a0, a1 = acc_scr[0], acc_scr[1]
            is_left = left > 0
            num = jnp.where(is_left, a0, a1)
            den = jnp.where(is_left, pltpu.roll(a0, HEAD_DIM, 1), pltpu.roll(a1, HEAD_DIM, 1))
            o_ref[...] = num / den
            lse_ref[...] = jnp.where(is_left, m_scr[0], m_scr[1]) + jnp.log(den)

    qspec = pl.BlockSpec((FOX_T, LANES), lambda p, t, qx, kx: (qx[t], p))
    kspec = pl.BlockSpec((FOX_T, LANES), lambda p, t, qx, kx: (kx[t], p))
    vspec = pl.BlockSpec((FOX_T, LANES), lambda p, t, qx, kx: (kx[t], 6 * N_PAIRS + p))
    ckspec = pl.BlockSpec((1, 8, FOX_T), lambda p, t, qx, kx: (p, 0, kx[t]))
    full = jax.ShapeDtypeStruct((S, MIX_HALF), F32)
    c_in, c_out, c_shape, c_scratch, c_args = _carry_specs(carry)
    grid_spec = pltpu.PrefetchScalarGridSpec(
        num_scalar_prefetch=2, grid=(N_PAIRS, ntri), in_specs=[qspec, kspec, vspec, qspec, ckspec] + c_in, out_specs=[qspec, qspec] + c_out,
        scratch_shapes=[pltpu.VMEM((2, FOX_T, 1), F32), pltpu.VMEM((2, FOX_T, LANES), F32)] + c_scratch)
    return pl.pallas_call(body, name=name, grid_spec=grid_spec, out_shape=[full, full] + c_shape,
                          compiler_params=_cparams("arbitrary", "arbitrary"))(qidx, kidx, qn, kn, proj, cum_b, cum_tp, *c_args)


def _fox_bwd(do, qn, kn, proj, cum_b, cum_tp, lse_b, delta_b, name, carry=None):
    S = qn.shape[0]
    nq = S // FOX_T
    scale = HEAD_DIM ** -0.5
    qidx, kidx = _causal_pairs(nq, True)
    ntri = int(qidx.shape[0])

    def body(qidx_ref, kidx_ref, *refs):
        t = pl.program_id(1)
        first = jnp.logical_and(pl.program_id(0) == 0, t == 0)
        last = jnp.logical_and(pl.program_id(0) == N_PAIRS - 1, t == ntri - 1)
        (do_ref, q_ref, k_ref, v_ref, cq_ref, ck_ref, lse_ref, dl_ref, dq_ref, dcq_ref, dk_ref, dv_ref, dck_ref,
         dq_scr, dk_scr, dv_scr) = _carried(carry, refs, 8, 5, first, last)
        qi, ki = qidx_ref[t], kidx_ref[t]

        @pl.when(t == 0)
        def _():
            dq_scr[...] = jnp.zeros_like(dq_scr)

        @pl.when(qi == ki)
        def _():
            dk_scr[...] = jnp.zeros_like(dk_scr)
            dv_scr[...] = jnp.zeros_like(dv_scr)

        left, right = _half_masks(FOX_T)
        rows = pl.ds(pl.multiple_of(qi * FOX_T, FOX_T), FOX_T)

        def step(diag):
            q, k, v, dov = q_ref[...], k_ref[...], v_ref[...].astype(BF16), do_ref[...]
            cqb, lseb, dlb = cq_ref[...], lse_ref[...], dl_ref[...]
            dob = dov.astype(BF16)
            heads = (0, 1)
            masks = (left, right)
            col = [slice(h * HEAD_DIM, h * HEAD_DIM + 1) for h in heads]
            ss = [_fox_scores(q, k, cqb, ck_ref, h, masks[h], diag) for h in heads]
            dps = [lax.dot_general((dov * masks[h]).astype(BF16), v, NT_DIMS, preferred_element_type=F32) for h in heads]
            ps = [jnp.exp(ss[h] - lseb[:, col[h]]) for h in heads]
            dss = [(ps[h] * (dps[h] - dlb[:, col[h]])).astype(BF16) for h in heads]
            pvs = [lax.dot_general(ps[h].astype(BF16), dob, TN_DIMS, preferred_element_type=F32) for h in heads]
            dks = [lax.dot_general(dss[h], jnp.where(masks[h] > 0, q, jnp.ones_like(q)), TN_DIMS, preferred_element_type=F32) for h in heads]
            dqs = [jnp.dot(dss[h], jnp.where(masks[h] > 0, k, jnp.ones_like(k)), preferred_element_type=F32) for h in heads]
            dv_scr[...] = dv_scr[...] + left * pvs[0] + right * pvs[1]
            for h in heads:
                dk_scr[h] = dk_scr[h] + dks[h]
                dq_scr[h, rows, :] = dq_scr[h, rows, :] + dqs[h]

        _on_causal_blocks(qi, ki, step)

        @pl.when(qi == nq - 1)
        def _():
            a0, a1 = dk_scr[0], dk_scr[1]
            dk_ref[...] = left * a0 + right * a1
            dv_ref[...] = dv_scr[...]
            dck_ref[...] = left * pltpu.roll(a0, HEAD_DIM, 1) + right * pltpu.roll(a1, HEAD_DIM, 1)

        @pl.when(t == ntri - 1)
        def _():
            for r in range(nq):
                blk = pl.ds(r * FOX_T, FOX_T)
                a0, a1 = dq_scr[0, blk, :], dq_scr[1, blk, :]
                dq_ref[blk, :] = (left * a0 + right * a1) * scale
                dcq_ref[blk, :] = left * pltpu.roll(a0, HEAD_DIM, 1) + right * pltpu.roll(a1, HEAD_DIM, 1)

    qspec = pl.BlockSpec((FOX_T, LANES), lambda p, t, qx, kx: (qx[t], p))
    kspec = pl.BlockSpec((FOX_T, LANES), lambda p, t, qx, kx: (kx[t], p))
    vspec = pl.BlockSpec((FOX_T, LANES), lambda p, t, qx, kx: (kx[t], 6 * N_PAIRS + p))
    ckspec = pl.BlockSpec((1, 8, FOX_T), lambda p, t, qx, kx: (p, 0, kx[t]))
    seq = pl.BlockSpec((S, LANES), lambda p, t, qx, kx: (0, p))
    full = jax.ShapeDtypeStruct((S, MIX_HALF), F32)
    c_in, c_out, c_shape, c_scratch, c_args = _carry_specs(carry)
    grid_spec = pltpu.PrefetchScalarGridSpec(
        num_scalar_prefetch=2, grid=(N_PAIRS, ntri), in_specs=[qspec, qspec, kspec, vspec, qspec, ckspec, qspec, qspec] + c_in,
        out_specs=[seq, seq, kspec, kspec, kspec] + c_out,
        scratch_shapes=[pltpu.VMEM((2, S, LANES), F32), pltpu.VMEM((2, FOX_T, LANES), F32), pltpu.VMEM((FOX_T, LANES), F32)] + c_scratch)
    return pl.pallas_call(body, name=name, grid_spec=grid_spec, out_shape=[full] * 5 + c_shape,
                          compiler_params=_cparams("arbitrary", "arbitrary"))(qidx, kidx, do, qn, kn, proj, cum_b, cum_tp, lse_b, delta_b, *c_args)


def _ff_bwd(rc, f_t, name):
    def body(rc_ref, f_ref, d_ref, s_ref):
        d = rc_ref[...] * (1.0 - _sigmoid(f_ref[...]))
        d_ref[...] = d
        s_ref[...] = jnp.sum(d, axis=1, keepdims=True)

    return pl.pallas_call(body, name=name, out_shape=[jax.ShapeDtypeStruct(rc.shape, F32), jax.ShapeDtypeStruct((rc.shape[0], 1), F32)])(rc, f_t)


def _fox_block_fwd(proj, b_f, q_gain, k_gain, bavg, pfx, carry=None):
    S = proj.shape[0]

    def prep(qv, kv, ba, qg, kg):
        return _head_rms(qv, ba)[0] * qg * (HEAD_DIM ** -0.5), _head_rms(kv, ba)[0] * kg

    qn, kn = _ew(prep, pfx + "_prep", [(proj, MIX_HALF, 4), (proj, MIX_HALF, 5)], consts=[bavg, q_gain, k_gain],
                 outs=[(MIX_HALF, BF16), (MIX_HALF, BF16)])
    f0 = 7 * MIX_HALF + GLA_RANK
    f_t = proj[:, f0:f0 + FOX_HEADS].T + b_f.reshape(FOX_HEADS, 1)
    cum = _cum_lanes(f_t, pfx + "_cum", False, pre=_log_sigmoid)
    cum_b = jnp.repeat(cum.T, HEAD_DIM, axis=1)
    cum_tp = jnp.pad(cum.reshape(N_PAIRS, 2, S), ((0, 0), (0, 6), (0, 0)))
    o, lse_b, *carried = _fox_fwd(qn, kn, proj, cum_b, cum_tp, pfx + "_attn", carry=carry)
    return o, dict(qn=qn, kn=kn, f_t=f_t, cum_b=cum_b, cum_tp=cum_tp, o=o, lse_b=lse_b), carried


def _fox_block_bwd(do, proj, q_gain, k_gain, bavg, res, pfx, carry=None):
    qn, kn, o = res["qn"], res["kn"], res["o"]
    S = proj.shape[0]
    delta_b, = _ew(lambda a, b, ba: _head_mean(a * b, ba) * float(HEAD_DIM), pfx + "_delta", [do, o], consts=[bavg], outs=[(MIX_HALF, F32)])
    args = (do, qn, kn, proj, res["cum_b"], res["cum_tp"], res["lse_b"], delta_b)
    dqn, dcq_b, dkn, dv, dck_b, *carried = _fox_bwd(*args, pfx + "_bwd", carry=carry)

    def prep_bwd(dq, dk, qv, kv, ba, qg, kg):
        nq, rq = _head_rms(qv, ba)
        nk, rk = _head_rms(kv, ba)
        return (_head_rms_bwd(dq * qg, nq, rq, ba), _head_rms_bwd(dk * kg, nk, rk, ba),
                jnp.sum(dq * nq, axis=0, keepdims=True), jnp.sum(dk * nk, axis=0, keepdims=True))

    dfq, dfk, dqg, dkg = _ew(prep_bwd, pfx + "_prep_bwd", [dqn, dkn, (proj, MIX_HALF, 4), (proj, MIX_HALF, 5)],
                             consts=[bavg, q_gain, k_gain], outs=[(MIX_HALF, F32), (MIX_HALF, F32)], sums=[MIX_HALF, MIX_HALF])
    dcum = (dcq_b - dck_b)[:, ::HEAD_DIM].T
    rc = _cum_lanes(dcum, pfx + "_rcum", True)
    dff_t, db_f = _ff_bwd(rc, res["f_t"], pfx + "_ff_bwd")
    grads = dict(b_f=db_f.reshape(-1), q_gain=dqg.reshape(-1, HEAD_DIM), k_gain=dkg.reshape(-1, HEAD_DIM))
    return (dfq, dfk, dv, dff_t.T), grads, carried


WEIGHTS = ['ada_w', 'ada_b', 'even_w_in', 'even_w_out', 'gla_w_lr', 'gla_b_lr', 'gla_gain', 'fox_b_f', 'fox_q_gain', 'fox_k_gain',
           'odd_w_in', 'odd_w_out', 's5_lam_re', 's5_lam_im', 's5_log_dt', 's5_b_re', 's5_b_im', 's5_c_re', 's5_c_im', 's5_d',
           's5_w_glu', 's5_b_glu', 'sgu_ln_gain', 'sgu_ln_bias', 'sgu_w_s', 'sgu_b_s', 'mlp_w1', 'mlp_w2']
ARGS = ['x', 'c'] + WEIGHTS + ['loss_target'] + ['m_' + w for w in WEIGHTS] + ['v_' + w for w in WEIGHTS]

EVEN_COLS = 3608
EVEN_PAD = 8 * MIX_HALF
MOD = 6 * D_MODEL
MOD_SHARD = MOD // N_CHIPS

PACK_COLS = 1024
EVEN_SHARD = EVEN_COLS // N_CHIPS
SHARDED = (
    ([("even_w_in", (1, 1024, PACK_COLS), 2), ("even_w_out", (1, 256, 1024), 1), ("mlp_w1_0", (1, 1024, 1024), 2),
      ("mlp_w2_0", (1, 1024, 1024), 1), ("gla_w_lr", (1, 16, 128), 2)], 3584),
    ([("odd_w_in", (1, 1024, 384), 2), ("odd_w_out", (1, 256, 1024), 1), ("mlp_w1_1", (1, 1024, 1024), 2),
      ("mlp_w2_1", (1, 1024, 1024), 1), ("s5_w_glu", (1, 128, 512), 1), ("s5_b_glu", (1, 128), 1), ("sgu_ln_gain", (1, 128), 1),
      ("sgu_ln_bias", (1, 128), 1)], 3072))
REPLICATED = [("gla_b_lr", (1, 512)), ("gla_gain", (1, 8, 64)), ("fox_b_f", (1, 8)), ("fox_q_gain", (1, 8, 64)),
              ("fox_k_gain", (1, 8, 64)), ("s5_lam_re", (1, 32, 64)), ("s5_lam_im", (1, 32, 64)), ("s5_log_dt", (1, 32)),
              ("s5_b_re", (1, 32, 64, 16)), ("s5_b_im", (1, 32, 64, 16)), ("s5_c_re", (1, 32, 16, 64)), ("s5_c_im", (1, 32, 16, 64)),
              ("s5_d", (1, 32, 16)), ("sgu_w_s", (1, 8, 128, 128)), ("sgu_b_s", (1, 8, 128))]
SMALL_ROWS = 512
BIG_ADAM = {"ada_w": (2048, 1536), "even_w_in": (1024, 902), "even_w_out": (256, 1024), "odd_w_in": (1024, 384),
            "odd_w_out": (256, 1024), "mlp_w1": (2048, 1024), "mlp_w2": (2048, 1024), "s5_w_glu": (128, 512)}


PACK_ALIGN = 16


def _piece_rows(shape):
    rows = -(-math.prod(shape) // PACK_COLS)
    return -(-rows // PACK_ALIGN) * PACK_ALIGN


def _to_rows(p, lead=()):
    n = math.prod(p.shape[len(lead):])
    rows = _piece_rows(p.shape[len(lead):])
    flat = p.reshape(lead + (n,))
    if rows * PACK_COLS != n:
        flat = jnp.pad(flat, [(0, 0)] * len(lead) + [(0, rows * PACK_COLS - n)])
    return flat.reshape(lead + (rows, PACK_COLS))


def _from_rows(x, r0, shape, lead=()):
    n = math.prod(shape)
    seg = lax.slice_in_dim(x, r0, r0 + _piece_rows(shape), axis=len(lead)).reshape(lead + (-1,))
    return lax.slice_in_dim(seg, 0, n, axis=len(lead)).reshape(lead + tuple(shape))


def _pack_rows(pieces, rows):
    x = jnp.concatenate([_to_rows(p) for p in pieces], axis=0)
    return jnp.pad(x, ((0, rows - x.shape[0]), (0, 0)))


def _unpack(x, specs):
    out, r0 = {}, 0
    for name, shape in specs:
        out[name] = _from_rows(x, r0, shape)
        r0 += _piece_rows(shape)
    return out


def _shards_to_full(x4, pieces):
    out, r0 = {}, 0
    for name, shape, axis in pieces:
        seg = _from_rows(x4, r0, shape, lead=(N_CHIPS,))
        out[name] = jnp.concatenate([seg[k] for k in range(N_CHIPS)], axis=axis)
        r0 += _piece_rows(shape)
    return out


def _full_to_shards(full, pieces, rows):
    blocks = [_to_rows(jnp.stack(jnp.split(full[name], N_CHIPS, axis=axis)), lead=(N_CHIPS,)) for name, _, axis in pieces]
    x = jnp.concatenate(blocks, axis=1)
    return jnp.pad(x, ((0, 0), (0, rows - x.shape[1]), (0, 0)))


def _gather_prep(local, pieces, rows):
    shard = _pack_rows([local[n] for n, _, _ in pieces], rows).astype(BF16)
    return lax.dynamic_slice_in_dim(shard, lax.axis_index("c") * (rows // 2), rows // 2, axis=0)


def _gather_finish(collected, pieces, rows, tag):
    halves = _by_core(collected, _pair_swap(collected, tag + "_pair"))
    return _shards_to_full(halves.transpose(1, 0, 2, 3).reshape(N_CHIPS, rows, PACK_COLS), pieces)


def _reduce_prep(full, pieces, rows, tag):
    mc = lax.axis_index("c")
    packed = _full_to_shards(full, pieces, rows)
    hr = rows // 2
    mine = lax.dynamic_slice_in_dim(packed, mc * hr, hr, axis=1)
    other = lax.dynamic_slice_in_dim(packed, (1 - mc) * hr, hr, axis=1)
    theirs = _pair_swap(other, tag + "_pair")
    pair_sum, = _ew(lambda p, q: p + q, tag + "_pair_sum", [mine.reshape(N_CHIPS * hr, PACK_COLS), theirs.reshape(N_CHIPS * hr, PACK_COLS)],
                    outs=[(PACK_COLS, BF16)])
    return pair_sum.reshape(N_CHIPS, hr, PACK_COLS)


def _reduce_finish(arrived, pieces, rows, tag):
    red_half = _sum_slots(arrived, tag + "_chip_sum")
    reduced = _by_core(red_half, _pair_swap(red_half, tag + "_pair_out")).reshape(rows, PACK_COLS)
    return _unpack(reduced, [(n, s) for n, s, _ in pieces])


def _relu2(t):
    r = jnp.maximum(t, 0.0)
    return r * r


def _silu(t):
    return t * _sigmoid(t)


def _pack_even(w):
    return jnp.concatenate([w[:, :2048], w[:, 2064:3600], w[:, 2048:2064], w[:, 3600:3608],
                            jnp.zeros((w.shape[0], EVEN_PAD - EVEN_COLS), w.dtype)], axis=1)


def _unpack_even(wp):
    return jnp.concatenate([wp[:, :2048], wp[:, 3584:3600], wp[:, 2048:3584], wp[:, 3600:3608]], axis=1)


def _mlp_fwd(h, w1, w2, pfx):
    pre = _mm(h, w1, "nn", pfx + "_up", out_dtype=BF16)
    return pre, _mm(pre, w2, "nn", pfx + "_down", a_pro=_relu2)


def _mlp_bwd(dm, h, pre, w1, w2, pfx):
    dpre = _mm(dm, w2, "nt", pfx + "_dpre", epi=lambda acc, p: acc * (2.0 * jnp.maximum(p, 0.0)), extras=[(pre, "mn")], out_dtype=BF16)
    dw2 = _mm(pre, dm, "tn", pfx + "_dw2", a_pro=_relu2)
    dw1 = _mm(h, dpre, "tn", pfx + "_dw1")
    dh = _mm(dpre, w1, "nt", pfx + "_dh")
    return dh, dw1, dw2


def _step(args):
    a = dict(zip(ARGS, args, strict=True))
    x0 = a["x"][0]
    target = a["loss_target"][0]
    mx, my, mc = lax.axis_index("x"), lax.axis_index("y"), lax.axis_index("c")
    chip = 2 * mx + my
    dev = 2 * chip + mc
    bavg = _head_avg_matrix(MIX_HALF)

    c_all = _gather8(jnp.pad(a["c"], ((0, 7), (0, 0))), "c_gather")[:, :, 0, :].reshape(2 * N_CHIPS, D_MODEL)
    ada_b_shard = lax.dynamic_slice_in_dim(a["ada_b"], chip * MOD_SHARD, MOD_SHARD, axis=1)
    mod_sh = [_mm(c_all, a["ada_w"][l], "nn", f"mod{l}", a_pro=_silu, epi=lambda acc, b: acc + b, extras=[(ada_b_shard[l:l + 1], "n")])
              for l in range(2)]
    small3 = jnp.zeros((8, MOD_SHARD), F32)
    for r, n in enumerate(("s5_b_glu", "sgu_ln_gain", "sgu_ln_bias")):
        small3 = small3.at[r, :LANES].set(a[n][0])
    mod_all = _chip_exchange(jnp.concatenate(mod_sh + [small3]), "mod_gather", True)
    mods = []
    for l in range(2):
        full = mod_all[:, 8 * l:8 * l + 8].transpose(1, 0, 2).reshape(8, MOD)
        mods.append(jnp.split(lax.dynamic_slice_in_dim(full, dev, 1, axis=0), 6, axis=1))
    b_glu, ln_gain, ln_bias = (mod_all[:, 16 + r, :LANES].reshape(1, MIX_HALF) for r in range(3))

    local = dict(a, even_w_in=jnp.pad(a["even_w_in"], ((0, 0), (0, 0), (0, PACK_COLS - EVEN_SHARD))),
                 mlp_w1_0=a["mlp_w1"][0:1], mlp_w1_1=a["mlp_w1"][1:2], mlp_w2_0=a["mlp_w2"][0:1], mlp_w2_1=a["mlp_w2"][1:2])
    (pieces0, rows0), (pieces1, rows1) = SHARDED
    w = _gather_finish(_chip_exchange(_gather_prep(local, pieces0, rows0), "w0_chips", True), pieces0, rows0, "w0")
    w_even = _pack_even(w["even_w_in"][0].reshape(D_MODEL, N_CHIPS, PACK_COLS)[:, :, :EVEN_SHARD].reshape(D_MODEL, EVEN_COLS))
    w_lr_pad = jnp.zeros((MIX_HALF, MIX_HALF), BF16).at[:GLA_RANK].set(w["gla_w_lr"][0])
    gla_b_lr = a["gla_b_lr"]
    gla_gain, q_gain, k_gain = (a[n].reshape(1, MIX_HALF) for n in ("gla_gain", "fox_q_gain", "fox_k_gain"))
    sgu_wm, sgu_bt = _sgu_tables(a["sgu_w_s"][0], a["sgu_b_s"][0])

    sh1, sc1, g1, sh2, sc2, g2 = mods[0]
    _, h1_0 = _res_rms(x0, sc1, sh1, "l0_norm1")
    proj0 = _mm(h1_0, w_even, "nn", "l0_proj")
    og, gla_res = _gla_block_fwd(proj0, w_lr_pad, gla_b_lr, gla_gain, bavg, "gla")
    of, fox_res, (collected1,) = _fox_block_fwd(proj0, a["fox_b_f"][0], q_gain, k_gain, bavg, "fox",
                                                carry=(_gather_prep(local, pieces1, rows1), True))
    w.update(_gather_finish(collected1, pieces1, rows1, "w1"))
    s5w = dict(lam_re=a["s5_lam_re"][0], lam_im=a["s5_lam_im"][0], log_dt=a["s5_log_dt"][0], b_re=a["s5_b_re"][0], b_im=a["s5_b_im"][0],
               c_re=a["s5_c_re"][0], c_im=a["s5_c_im"][0], d=a["s5_d"][0], w_glu=w["s5_w_glu"][0], b_glu=b_glu)
    mixed0 = jnp.concatenate([og, of], axis=1).astype(BF16)
    y0 = _mm(mixed0, w["even_w_out"][0], "nn", "l0_out")
    x1, h2_0 = _res_rms(x0, sc2, sh2, "l0_norm2", y=y0, g=g1)
    pre0, m0 = _mlp_fwd(h2_0, w["mlp_w1_0"][0], w["mlp_w2_0"][0], "l0_mlp")
    sh1b, sc1b, g1b, sh2b, sc2b, g2b = mods[1]
    x2, h1_1 = _res_rms(x1, sc1b, sh1b, "l1_norm1", y=m0, g=g2)
    proj1 = _mm(h1_1, w["odd_w_in"][0], "nn", "l1_proj")
    ys5, s5_res = _s5_block_fwd(proj1[:, :MIX_HALF], s5w, "s5")
    ysgu = _sgu_fwd(proj1, ln_gain, ln_bias, sgu_wm, sgu_bt, "sgu")
    mixed1 = jnp.concatenate([ys5, ysgu], axis=1).astype(BF16)
    y1 = _mm(mixed1, w["odd_w_out"][0], "nn", "l1_out")
    x3, h2_1 = _res_rms(x2, sc2b, sh2b, "l1_norm2", y=y1, g=g1b)
    pre1, m1 = _mlp_fwd(h2_1, w["mlp_w1_1"][0], w["mlp_w2_1"][0], "l1_mlp")
    loss_b, dx4, dm1, dg2b = _res_loss(x3, m1, g2b, target, "loss")
    loss = lax.psum(loss_b[0, 0], ("x", "y", "c"))

    full = {}
    dh2_1, dw1_1, dw2_1 = _mlp_bwd(dm1, h2_1, pre1, w["mlp_w1_1"][0], w["mlp_w2_1"][0], "l1_mlp")
    dx3, dy1, dg1b, dsc2b, dsh2b = _res_rms_bwd(x3, dh2_1, sc2b, dx4, "l1_norm2_bwd", y=y1, g=g1b)
    dmixed1 = _mm(dy1, w["odd_w_out"][0], "nt", "l1_out_dx")
    full["odd_w_out"] = _mm(mixed1, dy1, "tn", "l1_out_dw")[None]
    du, s5g = _s5_block_bwd(dmixed1[:, :MIX_HALF], s5w, s5_res, "s5")
    dzu, dzv, dws, dbt, dlg, dlb = _sgu_bwd(dmixed1[:, MIX_HALF:], proj1, ln_gain, ln_bias, sgu_wm, sgu_bt, "sgu_bwd")
    g_ws, g_bs = _sgu_grads(dws, dbt)
    dproj1 = jnp.concatenate([du, dzu, dzv], axis=1).astype(BF16)
    full["odd_w_in"] = _mm(h1_1, dproj1, "tn", "l1_proj_dw")[None]
    dh1_1 = _mm(dproj1, w["odd_w_in"][0], "nt", "l1_proj_dx")
    dx2, dm0, dg2, dsc1b, dsh1b = _res_rms_bwd(x2, dh1_1, sc1b, dx3, "l1_norm1_bwd", y=m0, g=g2)
    full.update(mlp_w1_1=dw1_1[None], mlp_w2_1=dw2_1[None], s5_w_glu=s5g["w_glu"][None], s5_b_glu=s5g["b_glu"][None],
                sgu_ln_gain=dlg, sgu_ln_bias=dlb)
    pair_sums1 = _reduce_prep(full, pieces1, rows1, "g1")
    dh2_0, dw1_0, dw2_0 = _mlp_bwd(dm0, h2_0, pre0, w["mlp_w1_0"][0], w["mlp_w2_0"][0], "l0_mlp")
    dx1, dy0, dg1, dsc2, dsh2 = _res_rms_bwd(x1, dh2_0, sc2, dx2, "l0_norm2_bwd", y=y0, g=g1)
    dmixed0 = _mm(dy0, w["even_w_out"][0], "nt", "l0_out_dx")
    full["even_w_out"] = _mm(mixed0, dy0, "tn", "l0_out_dw")[None]
    (dgq, dgk, dgv, dgg, dsmall), glag = _gla_block_bwd(dmixed0[:, :MIX_HALF], proj0, w_lr_pad, gla_gain, bavg, gla_res, "gla")
    (dfq, dfk, dfv, dff), foxg, (arrived1,) = _fox_block_bwd(dmixed0[:, MIX_HALF:], proj0, q_gain, k_gain, bavg, fox_res, "fox",
                                                           carry=(pair_sums1, False))
    dsmall = lax.dynamic_update_slice(dsmall, dff, (0, GLA_RANK))
    dproj0 = jnp.concatenate([dgq, dgk, dgv, dgg, dfq, dfk, dfv, dsmall], axis=1).astype(BF16)
    d_even = _unpack_even(_mm(h1_0, dproj0, "tn", "l0_proj_dw")).reshape(D_MODEL, N_CHIPS, EVEN_SHARD)
    full["even_w_in"] = jnp.pad(d_even, ((0, 0), (0, 0), (0, PACK_COLS - EVEN_SHARD))).reshape(1, D_MODEL, N_CHIPS * PACK_COLS)
    dh1_0 = _mm(dproj0, w_even, "nt", "l0_proj_dx")
    grad_x, dsc1, dsh1 = _res_rms_bwd(x0, dh1_0, sc1, dx1, "l0_norm1_bwd")
    full.update(gla_w_lr=glag["w_lr"][None], mlp_w1_0=dw1_0[None], mlp_w2_0=dw2_0[None])

    dmod = jnp.concatenate([dsh1, dsc1, dg1, dsh2, dsc2, dg2, dsh1b, dsc1b, dg1b, dsh2b, dsc2b, dg2b], axis=1)
    dmod_all = _gather8(jnp.pad(dmod, ((0, 7), (0, 0))), "dmod_gather")[:, :, 0, :].reshape(2 * N_CHIPS, 2, MOD)
    grads = {}
    grads["ada_w"] = jnp.stack([
        _mm(c_all, lax.dynamic_slice_in_dim(dmod_all[:, l], chip * MOD_SHARD, MOD_SHARD, axis=1), "tn", f"ada_dw{l}", a_pro=_silu)
        for l in range(2)])
    grads["ada_b"] = _sum_slots(dmod_all.reshape(2 * N_CHIPS, 2 * MOD // MIX_HALF, MIX_HALF), "ada_db").reshape(2, MOD)

    grads.update(_reduce_finish(arrived1, pieces1, rows1, "g1"))
    grads.update(_reduce_finish(_chip_exchange(_reduce_prep(full, pieces0, rows0, "g0"), "g0_chips", False), pieces0, rows0, "g0"))
    grads["even_w_in"] = grads["even_w_in"][:, :, :EVEN_SHARD]
    grads["mlp_w1"] = jnp.concatenate([grads.pop("mlp_w1_0"), grads.pop("mlp_w1_1")])
    grads["mlp_w2"] = jnp.concatenate([grads.pop("mlp_w2_0"), grads.pop("mlp_w2_1")])

    part = dict(gla_b_lr=glag["b_lr"], gla_gain=glag["gain"], fox_b_f=foxg["b_f"], fox_q_gain=foxg["q_gain"], fox_k_gain=foxg["k_gain"],
                s5_lam_re=s5g["lam_re"], s5_lam_im=s5g["lam_im"], s5_log_dt=s5g["log_dt"], s5_b_re=s5g["b_re"], s5_b_im=s5g["b_im"],
                s5_c_re=s5g["c_re"], s5_c_im=s5g["c_im"], s5_d=s5g["d"], sgu_w_s=g_ws, sgu_b_s=g_bs)
    parts_all = _gather8(_pack_rows([part[n] for n, _ in REPLICATED], SMALL_ROWS).astype(BF16), "rep_gather")
    rep = _sum_slots(parts_all.reshape(2 * N_CHIPS, SMALL_ROWS, PACK_COLS), "rep_sum")
    grads.update(_unpack(rep, REPLICATED))

    delta, new_m, new_v = {}, {}, {}
    for n, shape2 in BIG_ADAM.items():
        d, nm, nv = _adamw(a[n].reshape(shape2), grads[n].reshape(shape2), a["m_" + n].reshape(shape2), a["v_" + n].reshape(shape2), "adamw_" + n)
        delta[n], new_m[n], new_v[n] = (t.reshape(a[n].shape) for t in (d, nm, nv))
    small = [n for n in WEIGHTS if n not in BIG_ADAM]
    spec = [(n, a[n].shape) for n in small]
    packs = [_pack_rows([src[n] for n in small], SMALL_ROWS) for src in
             (a, grads, {n: a["m_" + n] for n in small}, {n: a["v_" + n] for n in small})]
    for tgt, res in zip((delta, new_m, new_v), _adamw(*packs, "adamw_small")):
        tgt.update(_unpack(res, spec))
    outs = [loss, grad_x[None]]
    for group in (grads, delta, new_m, new_v):
        outs += [group[n].reshape(a[n].shape) for n in WEIGHTS]
    return tuple(outs)


def kernel(x, c, ada_w, ada_b, even_w_in, even_w_out, gla_w_lr, gla_b_lr, gla_gain, fox_b_f, fox_q_gain, fox_k_gain, odd_w_in,
           odd_w_out, s5_lam_re, s5_lam_im, s5_log_dt, s5_b_re, s5_b_im, s5_c_re, s5_c_im, s5_d, s5_w_glu, s5_b_glu, sgu_ln_gain,
           sgu_ln_bias, sgu_w_s, sgu_b_s, mlp_w1, mlp_w2, loss_target, m_ada_w, m_ada_b, m_even_w_in, m_even_w_out, m_gla_w_lr,
           m_gla_b_lr, m_gla_gain, m_fox_b_f, m_fox_q_gain, m_fox_k_gain, m_odd_w_in, m_odd_w_out, m_s5_lam_re, m_s5_lam_im,
           m_s5_log_dt, m_s5_b_re, m_s5_b_im, m_s5_c_re, m_s5_c_im, m_s5_d, m_s5_w_glu, m_s5_b_glu, m_sgu_ln_gain, m_sgu_ln_bias,
           m_sgu_w_s, m_sgu_b_s, m_mlp_w1, m_mlp_w2, v_ada_w, v_ada_b, v_even_w_in, v_even_w_out, v_gla_w_lr, v_gla_b_lr,
           v_gla_gain, v_fox_b_f, v_fox_q_gain, v_fox_k_gain, v_odd_w_in, v_odd_w_out, v_s5_lam_re, v_s5_lam_im, v_s5_log_dt,
           v_s5_b_re, v_s5_b_im, v_s5_c_re, v_s5_c_im, v_s5_d, v_s5_w_glu, v_s5_b_glu, v_sgu_ln_gain, v_sgu_ln_bias, v_sgu_w_s,
           v_sgu_b_s, v_mlp_w1, v_mlp_w2):
    return _step((x, c, ada_w, ada_b, even_w_in, even_w_out, gla_w_lr, gla_b_lr, gla_gain, fox_b_f, fox_q_gain, fox_k_gain,
                  odd_w_in, odd_w_out, s5_lam_re, s5_lam_im, s5_log_dt, s5_b_re, s5_b_im, s5_c_re, s5_c_im, s5_d, s5_w_glu,
                  s5_b_glu, sgu_ln_gain, sgu_ln_bias, sgu_w_s, sgu_b_s, mlp_w1, mlp_w2, loss_target, m_ada_w, m_ada_b,
                  m_even_w_in, m_even_w_out, m_gla_w_lr, m_gla_b_lr, m_gla_gain, m_fox_b_f, m_fox_q_gain, m_fox_k_gain,
                  m_odd_w_in, m_odd_w_out, m_s5_lam_re, m_s5_lam_im, m_s5_log_dt, m_s5_b_re, m_s5_b_im, m_s5_c_re, m_s5_c_im,
                  m_s5_d, m_s5_w_glu, m_s5_b_glu, m_sgu_ln_gain, m_sgu_ln_bias, m_sgu_w_s, m_sgu_b_s, m_mlp_w1, m_mlp_w2, v_ada_w,
                  v_ada_b, v_even_w_in, v_even_w_out, v_gla_w_lr, v_gla_b_lr, v_gla_gain, v_fox_b_f, v_fox_q_gain, v_fox_k_gain,
                  v_odd_w_in, v_odd_w_out, v_s5_lam_re, v_s5_lam_im, v_s5_log_dt, v_s5_b_re, v_s5_b_im, v_s5_c_re, v_s5_c_im,
                  v_s5_d, v_s5_w_glu, v_s5_b_glu, v_sgu_ln_gain, v_sgu_ln_bias, v_sgu_w_s, v_sgu_b_s, v_mlp_w1, v_mlp_w2))
```

```python
import functools
import math

import jax
import jax.numpy as jnp
import numpy as np
from jax import lax
from jax.experimental import pallas as pl
from jax.experimental.pallas import tpu as pltpu

F32 = jnp.float32
BF16 = jnp.bfloat16
MESH = pl.DeviceIdType.MESH
ANY = pl.BlockSpec(memory_space=pl.ANY)
DMA_SEM = pltpu.SemaphoreType.DMA

D_MODEL = 1024
HEAD_DIM = 64
MIX_HALF = 512
GLA_RANK = 16
GLA_TAU = 16.0
GLA_CHUNK = 64
S5_GROUPS = 32
S5_GROUP_WIDTH = 16
S5_STATE = 64
S5_N = S5_GROUPS * S5_STATE
SGU_GROUPS = 8
SGU_CHUNK = 128
D_FF = 4096
EPS = 1e-6
N_CHIPS = 4
LANES = 128
VMEM_LIMIT = 48 * 1024 * 1024
PAIR_COPIES = 16

ADAM_LR = 0.001
ADAM_B1 = 0.9
ADAM_B2 = 0.999
ADAM_EPS = 1e-08
ADAM_WD = 0.01
ADAM_STEP = 10


def _cparams(*sem):
    return pltpu.CompilerParams(dimension_semantics=sem, vmem_limit_bytes=VMEM_LIMIT)


def _pair_swap(x, name):
    lead = x.shape[:-2]
    rows = x.shape[-2]
    nsplit = max(1, PAIR_COPIES // max(1, math.prod(lead)))
    while nsplit > 1 and rows % (nsplit * 16):
        nsplit -= 1
    pieces = [idx + (pl.ds(j * (rows // nsplit), rows // nsplit),) for idx in np.ndindex(*lead) for j in range(nsplit)]

    def body(x_ref, o_ref, send_sems, recv_sems):
        mx, my, mc = lax.axis_index("x"), lax.axis_index("y"), lax.axis_index("c")
        copies = [pltpu.make_async_remote_copy(src_ref=x_ref.at[p], dst_ref=o_ref.at[p], send_sem=send_sems.at[j], recv_sem=recv_sems.at[j],
                                               device_id=(mx, my, 1 - mc), device_id_type=MESH) for j, p in enumerate(pieces)]
        for cp in copies:
            cp.start()
        for cp in copies:
            cp.wait_recv()
        for cp in copies:
            cp.wait_send()

    return pl.pallas_call(
        body, name=name, out_shape=jax.ShapeDtypeStruct(x.shape, x.dtype), in_specs=[ANY], out_specs=ANY,
        scratch_shapes=[DMA_SEM((len(pieces),)), DMA_SEM((len(pieces),))])(x)


def _by_core(mine, theirs):
    first = lax.axis_index("c") == 0
    return jnp.stack([jnp.where(first, mine, theirs), jnp.where(first, theirs, mine)])


def _chip_exchange(x, name, bcast):
    blk = x.shape if bcast else x.shape[1:]

    def body(x_ref, o_ref, send_sems, recv_sems, loc_sem):
        start, finish = _exchange_plan(x_ref, o_ref, send_sems, recv_sems, loc_sem, bcast)
        start()
        finish()

    return pl.pallas_call(
        body, name=name, out_shape=jax.ShapeDtypeStruct((N_CHIPS,) + tuple(blk), x.dtype), in_specs=[ANY], out_specs=ANY,
        scratch_shapes=_EXCHANGE_SEMS)(x)


_EXCHANGE_SEMS = [DMA_SEM((3,)), DMA_SEM((3,)), DMA_SEM]


def _exchange_plan(x_ref, o_ref, send_sems, recv_sems, loc_sem, bcast):
    mx, my, mc = lax.axis_index("x"), lax.axis_index("y"), lax.axis_index("c")
    me = 2 * mx + my
    peers = [(1 - mx, my), (mx, 1 - my), (1 - mx, 1 - my)]

    def src(k):
        return x_ref if bcast else x_ref.at[k]

    def remote(j, source, slot):
        px, py = peers[j]
        return pltpu.make_async_remote_copy(src_ref=source, dst_ref=o_ref.at[slot], send_sem=send_sems.at[j], recv_sem=recv_sems.at[j],
                                            device_id=(px, py, mc), device_id_type=MESH)

    loc = pltpu.make_async_copy(src(me), o_ref.at[me], loc_sem)
    sends = [remote(j, src(2 * px + py), me) for j, (px, py) in enumerate(peers)]
    arrivals = [remote(j, src(me), 2 * px + py) for j, (px, py) in enumerate(peers)]

    def start():
        loc.start()
        for cp in sends:
            cp.start()

    def finish():
        for cp in arrivals:
            cp.wait_recv()
        for cp in sends:
            cp.wait_send()
        loc.wait()

    return start, finish


def _gather8(x, name):
    collected = _chip_exchange(x, name + "_chips", True)
    return jnp.swapaxes(_by_core(collected, _pair_swap(collected, name + "_pair")), 0, 1)


def _tile(n, want):
    if n <= want:
        return n
    t = (want // LANES) * LANES
    while t >= LANES:
        if n % t == 0:
            return t
        t -= LANES
    raise ValueError(f"no lane-aligned tile for {n}")


_DIMS = {"nn": (((1,), (0,)), ((), ())), "nt": (((1,), (1,)), ((), ())), "tn": (((0,), (0,)), ((), ()))}


MM_FULL_K = 4096
MM_SLAB_K = 2048
MM_TILES = ((1024, 1024), (512, 1024), (1024, 512), (512, 512), (256, 512), (256, 256))
MM_VMEM_BUDGET = 36 * 1024 * 1024


def _mm(a, b, mode, name, *, a_pro=None, epi=None, extras=(), out_dtype=F32, tm_max=1024, tn_max=1024, tk=None, a_cols=None):
    c0, csize = a_cols if a_cols is not None else (0, a.shape[1])
    if mode == "tn":
        K, M = a.shape[0], csize
    else:
        M, K = a.shape[0], csize
    N = b.shape[0] if mode == "nt" else b.shape[1]
    assert (b.shape[1] if mode == "nt" else b.shape[0]) == K, (a.shape, b.shape, mode)
    if tk is None:
        tk = K if (mode != "tn" and K <= MM_FULL_K) else MM_SLAB_K
    tk = _tile(K, tk)
    nk = K // tk
    n_mn = sum(1 for _, kind in extras if kind == "mn")
    for tm_want, tn_want in MM_TILES:
        tm, tn = _tile(M, min(tm_want, tm_max)), _tile(N, min(tn_want, tn_max))
        need = 2 * (tm * tk * a.dtype.itemsize + tk * tn * b.dtype.itemsize + tm * tn * 4 * (1 + n_mn)) + tm * tn * 4 * (nk > 1)
        if need <= MM_VMEM_BUDGET:
            break
    if mode == "tn":
        assert c0 % tm == 0
        a_spec = pl.BlockSpec((tk, tm), lambda i, j, k: (k, i + c0 // tm))
    else:
        assert c0 % tk == 0
        a_spec = pl.BlockSpec((tm, tk), lambda i, j, k: (i, k + c0 // tk))
    b_spec = pl.BlockSpec((tn, tk), lambda i, j, k: (j, k)) if mode == "nt" else pl.BlockSpec((tk, tn), lambda i, j, k: (k, j))
    ex_specs = []
    for arr, kind in extras:
        if kind == "mn":
            assert arr.shape == (M, N)
            ex_specs.append(pl.BlockSpec((tm, tn), lambda i, j, k: (i, j)))
        else:
            assert arr.shape == (1, N)
            ex_specs.append(pl.BlockSpec((1, tn), lambda i, j, k: (0, j)))
    n_ex = len(extras)

    def body(*refs):
        a_ref, b_ref = refs[:2]
        ex_refs = refs[2:2 + n_ex]
        o_ref = refs[2 + n_ex]
        acc_ref = refs[3 + n_ex] if nk > 1 else None
        k = pl.program_id(2)
        av = a_ref[...]
        if a_pro is not None:
            av = a_pro(av)
        part = lax.dot_general(av.astype(BF16), b_ref[...].astype(BF16), _DIMS[mode], preferred_element_type=F32)
        if nk == 1:
            if epi is not None:
                part = epi(part, *[r[...] for r in ex_refs])
            o_ref[...] = part.astype(o_ref.dtype)
            return

        @pl.when(k == 0)
        def _():
            acc_ref[...] = part

        @pl.when(k > 0)
        def _():
            acc_ref[...] += part

        @pl.when(k == nk - 1)
        def _():
            acc = acc_ref[...]
            if epi is not None:
                acc = epi(acc, *[r[...] for r in ex_refs])
            o_ref[...] = acc.astype(o_ref.dtype)

    return pl.pallas_call(
        body, name=name, grid=(M // tm, N // tn, nk),
        in_specs=[a_spec, b_spec] + ex_specs,
        out_specs=pl.BlockSpec((tm, tn), lambda i, j, k: (i, j)),
        out_shape=jax.ShapeDtypeStruct((M, N), out_dtype),
        scratch_shapes=[pltpu.VMEM((tm, tn), F32)] if nk > 1 else [],
        compiler_params=_cparams("parallel", "parallel", "arbitrary"))(a, b, *[e[0] for e in extras])


ROWS = 256


def _row_spec(w, ts=ROWS):
    return pl.BlockSpec((ts, w), lambda i: (i, 0))


def _vec_spec(w):
    return pl.BlockSpec((1, w), lambda i: (0, 0))


def _res_rms(x, sc, sh, name, y=None, g=None):
    S, D = x.shape
    has_res = y is not None

    def body(*refs):
        if has_res:
            x_ref, y_ref, g_ref, sc_ref, sh_ref, xo_ref, h_ref = refs
            xv = x_ref[...] + g_ref[...] * y_ref[...]
            xo_ref[...] = xv
        else:
            x_ref, sc_ref, sh_ref, h_ref = refs
            xv = x_ref[...]
        r = lax.rsqrt(jnp.mean(xv * xv, axis=-1, keepdims=True) + EPS)
        h_ref[...] = (xv * r * (1.0 + sc_ref[...]) + sh_ref[...]).astype(BF16)

    row, vec = _row_spec(D), _vec_spec(D)
    if has_res:
        return pl.pallas_call(body, name=name, grid=(S // ROWS,), in_specs=[row, row, vec, vec, vec], out_specs=[row, row],
                              out_shape=[jax.ShapeDtypeStruct((S, D), F32), jax.ShapeDtypeStruct((S, D), BF16)],
                              compiler_params=_cparams("parallel"))(x, y, g, sc, sh)
    h = pl.pallas_call(body, name=name, grid=(S // ROWS,), in_specs=[row, vec, vec], out_specs=row,
                       out_shape=jax.ShapeDtypeStruct((S, D), BF16), compiler_params=_cparams("parallel"))(x, sc, sh)
    return x, h


def _res_rms_bwd(x, dh, sc, dres, name, y=None, g=None):
    S, D = x.shape
    has_res = y is not None

    def body(*refs):
        if has_res:
            x_ref, dh_ref, sc_ref, dres_ref, y_ref, g_ref, dx_ref, dy_ref, dg_ref, dsc_ref, dsh_ref = refs
        else:
            x_ref, dh_ref, sc_ref, dres_ref, dx_ref, dsc_ref, dsh_ref = refs
        first = pl.program_id(0) == 0
        xv = x_ref[...]
        dh = dh_ref[...]
        r = lax.rsqrt(jnp.mean(xv * xv, axis=-1, keepdims=True) + EPS)
        xn = xv * r
        dxn = dh * (1.0 + sc_ref[...])
        dx = dres_ref[...] + r * (dxn - xn * jnp.mean(dxn * xn, axis=-1, keepdims=True))
        dx_ref[...] = dx
        parts = [(dsc_ref, jnp.sum(dh * xn, axis=0, keepdims=True)), (dsh_ref, jnp.sum(dh, axis=0, keepdims=True))]
        if has_res:
            dy_ref[...] = (dx * g_ref[...]).astype(BF16)
            parts.append((dg_ref, jnp.sum(dx * y_ref[...], axis=0, keepdims=True)))
        for ref, val in parts:
            @pl.when(first)
            def _(ref=ref, val=val):
                ref[...] = val

            @pl.when(jnp.logical_not(first))
            def _(ref=ref, val=val):
                ref[...] += val

    row, vec = _row_spec(D), _vec_spec(D)
    full = jax.ShapeDtypeStruct((S, D), F32)
    v = jax.ShapeDtypeStruct((1, D), F32)
    if has_res:
        return pl.pallas_call(body, name=name, grid=(S // ROWS,), in_specs=[row, row, vec, row, row, vec],
                              out_specs=[row, row, vec, vec, vec], out_shape=[full, jax.ShapeDtypeStruct((S, D), BF16), v, v, v],
                              compiler_params=_cparams("arbitrary"))(x, dh, sc, dres, y, g)
    return pl.pallas_call(body, name=name, grid=(S // ROWS,), in_specs=[row, row, vec, row],
                          out_specs=[row, vec, vec], out_shape=[full, v, v],
                          compiler_params=_cparams("arbitrary"))(x, dh, sc, dres)


def _res_loss(x, m, g, target, name):
    S, D = x.shape

    def body(x_ref, m_ref, g_ref, t_ref, loss_ref, dx_ref, dm_ref, dg_ref):
        first = pl.program_id(0) == 0
        mv = m_ref[...]
        err = x_ref[...] + g_ref[...] * mv - t_ref[...]
        dx = err * (1.0 / D)
        dx_ref[...] = dx
        dm_ref[...] = (dx * g_ref[...]).astype(BF16)
        part = 0.5 * jnp.sum(jnp.mean(err * err, axis=-1, keepdims=True), axis=0, keepdims=True)
        dg = jnp.sum(dx * mv, axis=0, keepdims=True)

        @pl.when(first)
        def _():
            loss_ref[...] = jnp.broadcast_to(part, loss_ref.shape)
            dg_ref[...] = dg

        @pl.when(jnp.logical_not(first))
        def _():
            loss_ref[...] += jnp.broadcast_to(part, loss_ref.shape)
            dg_ref[...] += dg

    row, vec = _row_spec(D), _vec_spec(D)
    full = jax.ShapeDtypeStruct((S, D), F32)
    return pl.pallas_call(body, name=name, grid=(S // ROWS,), in_specs=[row, row, vec, row],
                          out_specs=[pl.BlockSpec((8, LANES), lambda i: (0, 0)), row, row, vec],
                          out_shape=[jax.ShapeDtypeStruct((8, LANES), F32), full, jax.ShapeDtypeStruct((S, D), BF16), jax.ShapeDtypeStruct((1, D), F32)],
                          compiler_params=_cparams("arbitrary"))(x, m, g, target)


def _adamw(w, g, m, v, name):
    R, C = w.shape
    tr = R if R <= 256 else 256
    assert R % tr == 0

    def body(w_ref, g_ref, m_ref, v_ref, d_ref, nm_ref, nv_ref):
        gv = g_ref[...]
        nm = ADAM_B1 * m_ref[...] + (1.0 - ADAM_B1) * gv
        nv = ADAM_B2 * v_ref[...] + (1.0 - ADAM_B2) * jnp.square(gv)
        m_hat = nm / (1.0 - ADAM_B1 ** ADAM_STEP)
        v_hat = nv / (1.0 - ADAM_B2 ** ADAM_STEP)
        d_ref[...] = -ADAM_LR * (m_hat / (jnp.sqrt(v_hat) + ADAM_EPS) + ADAM_WD * w_ref[...])
        nm_ref[...] = nm
        nv_ref[...] = nv

    spec = pl.BlockSpec((tr, C), lambda i: (i, 0))
    out = jax.ShapeDtypeStruct((R, C), F32)
    return pl.pallas_call(body, name=name, grid=(R // tr,), in_specs=[spec] * 4, out_specs=[spec] * 3,
                          out_shape=[out, out, out], compiler_params=_cparams("parallel"))(w, g, m, v)


def _sum_slots(x, name):
    n, R, C = x.shape
    tr = R if R <= 256 else 256
    assert R % tr == 0

    def body(x_ref, o_ref):
        acc = x_ref[0].astype(F32)
        for j in range(1, n):
            acc = acc + x_ref[j].astype(F32)
        o_ref[...] = acc

    return pl.pallas_call(body, name=name, grid=(R // tr,), in_specs=[pl.BlockSpec((n, tr, C), lambda i: (0, i, 0))],
                          out_specs=pl.BlockSpec((tr, C), lambda i: (i, 0)), out_shape=jax.ShapeDtypeStruct((R, C), F32),
                          compiler_params=_cparams("parallel"))(x)


def _ew(fn, name, tiled, consts=(), outs=(), sums=(), ts=ROWS):
    tiled = [t if isinstance(t, tuple) else (t, t.shape[1], 0) for t in tiled]
    S = tiled[0][0].shape[0]
    n_t, n_c, n_o, n_s = len(tiled), len(consts), len(outs), len(sums)

    def body(*refs):
        ins = [r[...] for r in refs[:n_t + n_c]]
        res = fn(*ins)
        res = res if isinstance(res, (tuple, list)) else (res,)
        assert len(res) == n_o + n_s
        o_refs = refs[n_t + n_c:]
        for r, val in zip(o_refs[:n_o], res[:n_o]):
            r[...] = val.astype(r.dtype)
        first = pl.program_id(0) == 0
        for r, val in zip(o_refs[n_o:], res[n_o:]):
            @pl.when(first)
            def _(r=r, val=val):
                r[...] = val

            @pl.when(jnp.logical_not(first))
            def _(r=r, val=val):
                r[...] += val

    in_specs = [pl.BlockSpec((ts, w), lambda i, cb=cb: (i, cb)) for _, w, cb in tiled]
    in_specs += [pl.BlockSpec(c.shape, lambda i, nd=c.ndim: (0,) * nd) for c in consts]
    out_specs = [_row_spec(w, ts) for w, _ in outs] + [_vec_spec(w) for w in sums]
    out_shape = [jax.ShapeDtypeStruct((S, w), dt) for w, dt in outs] + [jax.ShapeDtypeStruct((1, w), F32) for w in sums]
    res = pl.pallas_call(body, name=name, grid=(S // ts,), in_specs=in_specs, out_specs=out_specs, out_shape=out_shape,
                         compiler_params=_cparams("arbitrary" if sums else "parallel"))(*[t[0] for t in tiled], *consts)
    return res


_GELU_C = math.sqrt(2.0 / math.pi)


def _gelu(x):
    return 0.5 * x * (1.0 + jnp.tanh(_GELU_C * (x + 0.044715 * x * x * x)))


def _dgelu(x):
    t = jnp.tanh(_GELU_C * (x + 0.044715 * x * x * x))
    return 0.5 * (1.0 + t) + 0.5 * x * (1.0 - t * t) * _GELU_C * (1.0 + 3.0 * 0.044715 * x * x)


def _sigmoid(x):
    return 1.0 / (1.0 + jnp.exp(-x))


def _log_sigmoid(x):
    return jnp.minimum(x, 0.0) - jnp.log(1.0 + jnp.exp(-jnp.abs(x)))


SCAN_T = 128
SCAN_TB = 512


def _cmul(ar, ai, br, bi):
    return ar * br - ai * bi, ar * bi + ai * br


def _s5_discretise(lam_re, lam_im, log_dt, b_re, b_im):
    dt = jnp.exp(log_dt)[:, None]
    mag = jnp.exp(lam_re * dt)
    ang = lam_im * dt
    abar_re = mag * jnp.cos(ang)
    abar_im = mag * jnp.sin(ang)
    den = lam_re * lam_re + lam_im * lam_im
    coef_re = ((abar_re - 1.0) * lam_re + abar_im * lam_im) / den
    coef_im = (abar_im * lam_re - (abar_re - 1.0) * lam_im) / den
    bbar_re = coef_re[..., None] * b_re - coef_im[..., None] * b_im
    bbar_im = coef_re[..., None] * b_im + coef_im[..., None] * b_re
    return abar_re, abar_im, bbar_re, bbar_im


def _planes(re, im):
    lead = re.shape[:-1]
    return jnp.stack([re.reshape(lead + (-1, LANES)), im.reshape(lead + (-1, LANES))], axis=-2).reshape(lead + (-1,))


def _unplanes(x):
    lead = x.shape[:-1]
    x4 = x.reshape(lead + (-1, 2, LANES))
    return x4[..., 0, :].reshape(lead + (-1,)), x4[..., 1, :].reshape(lead + (-1,))


def _s5_scan_tables(a_re, a_im, reverse):
    pr, pi = [a_re], [a_im]
    for _ in range(7):
        r, i = _cmul(pr[-1], pi[-1], pr[-1], pi[-1])
        pr.append(r)
        pi.append(i)
    apow = _planes(jnp.stack(pr), jnp.stack(pi))
    n = np.arange(1, SCAN_T + 1)
    if reverse:
        n = n[::-1]
    tr = jnp.ones((SCAN_T, a_re.shape[0]), F32)
    ti = jnp.zeros((SCAN_T, a_re.shape[0]), F32)
    for k in range(8):
        bit = jnp.asarray(((n >> k) & 1).astype(np.float32))[:, None]
        mr = bit * pr[k][None, :] + (1.0 - bit)
        mi = bit * pi[k][None, :]
        tr, ti = _cmul(tr, ti, mr, mi)
    return apow, _planes(tr, ti)


def _s5_scan(bu, apow, ptab, name, reverse, x_fwd=None):
    S, N2 = bu.shape
    T, W = SCAN_T, 2 * LANES
    tb = min(SCAN_TB, S)
    nt, nsub = S // tb, tb // T
    order = list(range(nsub - 1, -1, -1) if reverse else range(nsub))
    with_da = x_fwd is not None

    def tblk(t):
        return (nt - 1 - t) if reverse else t

    def shifted(v, k, rowi):
        s = 1 << k
        if reverse:
            return jnp.where(rowi < T - s, pltpu.roll(v, T - s, 0), 0.0)
        return jnp.where(rowi >= s, pltpu.roll(v, s, 0), 0.0)

    def body(*refs):
        if with_da:
            bu_ref, ap_ref, pt_ref, xf_ref, xp_ref, x_ref, da_ref, carry_ref = refs
        else:
            bu_ref, ap_ref, pt_ref, x_ref, carry_ref = refs
        t = pl.program_id(1)

        @pl.when(t == 0)
        def _():
            carry_ref[...] = jnp.zeros_like(carry_ref)
            if with_da:
                da_ref[...] = jnp.zeros_like(da_ref)

        rowi = lax.broadcasted_iota(jnp.int32, (T, LANES), 0)
        pr, pi = pt_ref[:, :LANES], pt_ref[:, LANES:]
        cr, ci = carry_ref[0:1, :LANES], carry_ref[0:1, LANES:]
        for sb in order:
            rows = pl.ds(sb * T, T)
            xr, xi = bu_ref[rows, :LANES], bu_ref[rows, LANES:]
            for k in range(7):
                ar, ai = ap_ref[k:k + 1, :LANES], ap_ref[k:k + 1, LANES:]
                s = 1 << k
                if s < 8:
                    rr, ri = shifted(xr, k, rowi), shifted(xi, k, rowi)
                    xr, xi = xr + ar * rr - ai * ri, xi + ar * ri + ai * rr
                elif reverse:
                    nr, ni = xr[s:], xi[s:]
                    xr = jnp.concatenate([xr[:T - s] + ar * nr - ai * ni, xr[T - s:]], axis=0)
                    xi = jnp.concatenate([xi[:T - s] + ar * ni + ai * nr, xi[T - s:]], axis=0)
                else:
                    nr, ni = xr[:T - s], xi[:T - s]
                    xr = jnp.concatenate([xr[:s], xr[s:] + ar * nr - ai * ni], axis=0)
                    xi = jnp.concatenate([xi[:s], xi[s:] + ar * ni + ai * nr], axis=0)
            xr, xi = xr + pr * cr - pi * ci, xi + pr * ci + pi * cr
            x_ref[rows, :LANES] = xr
            x_ref[rows, LANES:] = xi
            edge = pl.ds(sb * T + (0 if reverse else T - 1), 1)
            cr, ci = x_ref[edge, :LANES], x_ref[edge, LANES:]
            if with_da:
                if sb > 0:
                    before = pl.ds(sb * T - 1, 1)
                    b_r, b_i = xf_ref[before, :LANES], xf_ref[before, LANES:]
                else:
                    keep = (tblk(t) > 0).astype(F32)
                    b_r, b_i = xp_ref[7:8, :LANES] * keep, xp_ref[7:8, LANES:] * keep
                fr, fi = xf_ref[rows, :LANES], xf_ref[rows, LANES:]
                qr = jnp.where(rowi >= 1, pltpu.roll(fr, 1, 0), b_r)
                qi = jnp.where(rowi >= 1, pltpu.roll(fi, 1, 0), b_i)
                gr, gi = xr * qr + xi * qi, xi * qr - xr * qi
                sr, si = gr[0:8], gi[0:8]
                for j in range(1, T // 8):
                    sr, si = sr + gr[8 * j:8 * j + 8], si + gi[8 * j:8 * j + 8]
                da_ref[:, :LANES] += sr
                da_ref[:, LANES:] += si
        carry_ref[0:1, :LANES] = cr
        carry_ref[0:1, LANES:] = ci

    blk = pl.BlockSpec((tb, W), lambda j, t: (tblk(t), j))
    in_specs = [blk, pl.BlockSpec((8, W), lambda j, t: (0, j)), pl.BlockSpec((T, W), lambda j, t: (0, j))]
    out_specs, out_shape = [blk], [jax.ShapeDtypeStruct((S, N2), F32)]
    args = [bu, apow, ptab]
    if with_da:
        in_specs += [blk, pl.BlockSpec((8, W), lambda j, t: (jnp.maximum(tblk(t) * (tb // 8) - 1, 0), j))]
        out_specs.append(pl.BlockSpec((8, W), lambda j, t: (0, j)))
        out_shape.append(jax.ShapeDtypeStruct((8, N2), F32))
        args += [x_fwd, x_fwd]
    res = pl.pallas_call(body, name=name, grid=(N2 // W, nt), in_specs=in_specs, out_specs=out_specs, out_shape=out_shape,
                         scratch_shapes=[pltpu.VMEM((8, W), F32)], compiler_params=_cparams("parallel", "arbitrary"))(*args)
    return res if with_da else res[0]


S5_BAND = 4


def _mm_band(a, b, name, *, b_t=False, outer=False, epi=None, extras=(), tm=512, tk=2048):
    S = a.shape[0]
    wa = a.shape[1] // S5_BAND
    if outer:
        wb = b.shape[1] // S5_BAND
        tk = _tile(S, tk)
        nk = S // tk

        def obody(a_ref, b_ref, o_ref, acc_ref):
            k = pl.program_id(1)
            part = lax.dot_general(a_ref[...].astype(BF16), b_ref[...].astype(BF16), TN_DIMS, preferred_element_type=F32)

            @pl.when(k == 0)
            def _():
                acc_ref[...] = part

            @pl.when(k > 0)
            def _():
                acc_ref[...] += part

            @pl.when(k == nk - 1)
            def _():
                o_ref[...] = acc_ref[...]

        return pl.pallas_call(
            obody, name=name, grid=(S5_BAND, nk),
            in_specs=[pl.BlockSpec((tk, wa), lambda c, k: (k, c)), pl.BlockSpec((tk, wb), lambda c, k: (k, c))],
            out_specs=pl.BlockSpec((wa, wb), lambda c, k: (c, 0)), out_shape=jax.ShapeDtypeStruct((a.shape[1], wb), F32),
            scratch_shapes=[pltpu.VMEM((wa, wb), F32)], compiler_params=_cparams("parallel", "arbitrary"))(a, b)

    wo = (b.shape[0] if b_t else b.shape[1]) // S5_BAND
    tm = _tile(S, tm)
    ex_specs = [pl.BlockSpec((tm, wo), lambda i, c: (i, c)) if kind == "mn" else pl.BlockSpec((1, wo), lambda i, c: (0, c))
                for _, kind in extras]

    def body(a_ref, b_ref, *refs):
        part = lax.dot_general(a_ref[...].astype(BF16), b_ref[...].astype(BF16), NT_DIMS if b_t else _DIMS["nn"], preferred_element_type=F32)
        if epi is not None:
            part = epi(part, *[r[...] for r in refs[:-1]])
        refs[-1][...] = part

    b_spec = pl.BlockSpec((wo, wa) if b_t else (wa, wo), lambda i, c: (c, c))
    return pl.pallas_call(
        body, name=name, grid=(S // tm, S5_BAND), in_specs=[pl.BlockSpec((tm, wa), lambda i, c: (i, c)), b_spec] + ex_specs,
        out_specs=pl.BlockSpec((tm, wo), lambda i, c: (i, c)), out_shape=jax.ShapeDtypeStruct((S, S5_BAND * wo), F32),
        compiler_params=_cparams("parallel", "parallel"))(a, b, *[e[0] for e in extras])


def _band_to_full(blocks, cols):
    wa, wb = blocks.shape[0] // S5_BAND, blocks.shape[1]
    return jnp.concatenate([jnp.pad(blocks[k * wa:(k + 1) * wa], ((0, 0), (k * wb, cols - (k + 1) * wb))) for k in range(S5_BAND)], axis=0)


def _block_diag(t):
    G, a, b = t.shape
    return (t[:, :, None, :] * jnp.eye(G, dtype=t.dtype)[:, None, :, None]).reshape(G * a, G * b)


def _block_diag_take(m, G):
    a, b = m.shape[0] // G, m.shape[1] // G
    m4 = m.reshape(G, a, G, b)
    return jnp.sum(m4 * jnp.eye(G, dtype=m.dtype)[:, None, :, None], axis=2)


def _s5_block_fwd(u, w, pfx):
    a_re, a_im, bb_re, bb_im = _s5_discretise(w["lam_re"], w["lam_im"], w["log_dt"], w["b_re"], w["b_im"])
    bcat = _planes(_block_diag(bb_re).T, _block_diag(bb_im).T).astype(BF16)
    ccat = _planes(_block_diag(jnp.swapaxes(w["c_re"], 1, 2)).T, -_block_diag(jnp.swapaxes(w["c_im"], 1, 2)).T).T.astype(BF16)
    af_re, af_im = a_re.reshape(-1), a_im.reshape(-1)
    apow, ptab = _s5_scan_tables(af_re, af_im, False)
    bu = _mm_band(u, bcat, pfx + "_bu")
    x = _s5_scan(bu, apow, ptab, pfx + "_scan", False)
    d_row = w["d"].reshape(1, MIX_HALF)
    ys = _mm_band(x, ccat, pfx + "_y", epi=lambda acc, ut, dr: acc + dr * ut, extras=[(u, "mn"), (d_row, "n")])
    z = _mm(ys, w["w_glu"], "nn", pfx + "_glu", a_pro=_gelu, epi=lambda acc, b: acc + b, extras=[(w["b_glu"].reshape(1, -1), "n")])
    y2, = _ew(lambda ysv, zv: _gelu(ysv) * _sigmoid(zv), pfx + "_gate", [ys, z], outs=[(MIX_HALF, F32)])
    return y2, dict(u=u, x=x, ys=ys, z=z, bcat=bcat, ccat=ccat, a=(af_re, af_im), d_row=d_row)


def _s5_block_bwd(dy2, w, res, pfx):
    u, x, ys, z, bcat, ccat = res["u"], res["x"], res["ys"], res["z"], res["bcat"], res["ccat"]

    def gate_bwd(dy, ysv, zv):
        sg = _sigmoid(zv)
        dz = dy * _gelu(ysv) * sg * (1.0 - sg)
        return dz, jnp.sum(dz, axis=0, keepdims=True)

    dz, db_glu = _ew(gate_bwd, pfx + "_gate_bwd", [dy2, ys, z], outs=[(MIX_HALF, F32)], sums=[MIX_HALF])
    dw_glu = _mm(ys, dz, "tn", pfx + "_dwglu", a_pro=_gelu)
    dys = _mm(dz, w["w_glu"], "nt", pfx + "_dys", epi=lambda acc, dy, zv, ysv: (acc + dy * _sigmoid(zv)) * _dgelu(ysv),
              extras=[(dy2, "mn"), (z, "mn"), (ys, "mn")])
    dd, = _ew(lambda a, b: jnp.sum(a * b, axis=0, keepdims=True), pfx + "_dd", [dys, u], sums=[MIX_HALF])
    dccat = _band_to_full(_mm_band(x, dys, pfx + "_dc", outer=True), MIX_HALF)
    dx = _mm_band(dys, ccat, pfx + "_dx", b_t=True)
    af_re, af_im = res["a"]
    apow, ptab = _s5_scan_tables(af_re, -af_im, True)
    lam, da8 = _s5_scan(dx, apow, ptab, pfx + "_scan_bwd", True, x_fwd=x)
    dbcat = _band_to_full(_mm_band(u, lam, pfx + "_db", outer=True), 2 * S5_N)
    du = _mm_band(lam, bcat, pfx + "_du", b_t=True, epi=lambda acc, dyv, dr: acc + dyv * dr, extras=[(dys, "mn"), (res["d_row"], "n")])
    G = S5_GROUPS
    d_abar_re, d_abar_im = (t.reshape(G, S5_STATE) for t in _unplanes(jnp.sum(da8, axis=0)))
    d_bb_re, d_bb_im = (_block_diag_take(t.T, G) for t in _unplanes(dbcat))
    _, vjp = jax.vjp(_s5_discretise, w["lam_re"], w["lam_im"], w["log_dt"], w["b_re"], w["b_im"])
    g_lam_re, g_lam_im, g_log_dt, g_b_re, g_b_im = vjp((d_abar_re, d_abar_im, d_bb_re, d_bb_im))
    dc_re, dc_im = _unplanes(dccat.T)
    g_c_re = jnp.swapaxes(_block_diag_take(dc_re.T, G), 1, 2)
    g_c_im = -jnp.swapaxes(_block_diag_take(dc_im.T, G), 1, 2)
    grads = dict(lam_re=g_lam_re, lam_im=g_lam_im, log_dt=g_log_dt, b_re=g_b_re, b_im=g_b_im, c_re=g_c_re, c_im=g_c_im,
                 d=dd.reshape(G, S5_GROUP_WIDTH), w_glu=dw_glu, b_glu=db_glu.reshape(-1))
    return du, grads


SGU_TS = 512
N_PAIRS = MIX_HALF // LANES


def _half_masks(rows):
    lane = lax.broadcasted_iota(jnp.int32, (rows, LANES), 1)
    left = (lane < HEAD_DIM).astype(F32)
    return left, 1.0 - left


def _sgu_norm(zv, gain, bias):
    v = _gelu(zv)
    mu = jnp.mean(v, axis=-1, keepdims=True)
    vc = v - mu
    rstd = lax.rsqrt(jnp.mean(vc * vc, axis=-1, keepdims=True) + EPS)
    vhat = vc * rstd
    return vhat, rstd, vhat * gain + bias


def _sgu_tables(w_s, b_s):
    mask = jnp.tril(jnp.ones((SGU_CHUNK, SGU_CHUNK), dtype=bool))
    wm = jnp.where(mask[None], w_s, 0.0).astype(BF16)
    bias_tab = jnp.repeat(b_s.T, MIX_HALF // SGU_GROUPS, axis=1)
    return wm, bias_tab


def _sgu_fwd(proj, ln_gain, ln_bias, wm, bias_tab, name):
    S = proj.shape[0]
    nch = SGU_TS // SGU_CHUNK

    def body(zu_ref, zv_ref, g_ref, b_ref, w_ref, bt_ref, o_ref):
        left, right = _half_masks(SGU_CHUNK)
        _, _, vn = _sgu_norm(zv_ref[...], g_ref[...], b_ref[...])
        for ch in range(nch):
            rows = pl.ds(ch * SGU_CHUNK, SGU_CHUNK)
            for p in range(N_PAIRS):
                cols = pl.ds(p * LANES, LANES)
                vp = vn[ch * SGU_CHUNK:(ch + 1) * SGU_CHUNK, p * LANES:(p + 1) * LANES]
                mixed = (jnp.dot(w_ref[2 * p], (vp * left).astype(BF16), preferred_element_type=F32)
                         + jnp.dot(w_ref[2 * p + 1], (vp * right).astype(BF16), preferred_element_type=F32) + bt_ref[:, cols])
                o_ref[rows, cols] = _gelu(zu_ref[rows, cols]) * mixed

    vec = _vec_spec(MIX_HALF)
    return pl.pallas_call(
        body, name=name, grid=(S // SGU_TS,),
        in_specs=[pl.BlockSpec((SGU_TS, MIX_HALF), lambda i: (i, 1)), pl.BlockSpec((SGU_TS, MIX_HALF), lambda i: (i, 2)), vec, vec,
                  pl.BlockSpec((SGU_GROUPS, SGU_CHUNK, SGU_CHUNK), lambda i: (0, 0, 0)), pl.BlockSpec((SGU_CHUNK, MIX_HALF), lambda i: (0, 0))],
        out_specs=_row_spec(MIX_HALF, SGU_TS), out_shape=jax.ShapeDtypeStruct((S, MIX_HALF), F32),
        compiler_params=_cparams("parallel"))(proj, proj, ln_gain, ln_bias, wm, bias_tab)


def _sgu_bwd(dout, proj, ln_gain, ln_bias, wm, bias_tab, name):
    S = proj.shape[0]
    nch = SGU_TS // SGU_CHUNK
    nt_dims = (((1,), (1,)), ((), ()))
    tn_dims = (((0,), (0,)), ((), ()))

    def body(do_ref, zu_ref, zv_ref, g_ref, b_ref, w_ref, bt_ref, dzu_ref, dzv_ref, dw_ref, dbt_ref, dg_ref, db_ref, dvn_ref):
        first = pl.program_id(0) == 0

        @pl.when(first)
        def _():
            dw_ref[...] = jnp.zeros_like(dw_ref)
            dbt_ref[...] = jnp.zeros_like(dbt_ref)
            dg_ref[...] = jnp.zeros_like(dg_ref)
            db_ref[...] = jnp.zeros_like(db_ref)

        left, right = _half_masks(SGU_CHUNK)
        zv = zv_ref[...]
        vhat, rstd, vn = _sgu_norm(zv, g_ref[...], b_ref[...])
        for ch in range(nch):
            rows = pl.ds(ch * SGU_CHUNK, SGU_CHUNK)
            for p in range(N_PAIRS):
                cols = pl.ds(p * LANES, LANES)
                vp = vn[ch * SGU_CHUNK:(ch + 1) * SGU_CHUNK, p * LANES:(p + 1) * LANES]
                vl, vr = (vp * left).astype(BF16), (vp * right).astype(BF16)
                mixed = (jnp.dot(w_ref[2 * p], vl, preferred_element_type=F32)
                         + jnp.dot(w_ref[2 * p + 1], vr, preferred_element_type=F32) + bt_ref[:, cols])
                zu = zu_ref[rows, cols]
                do = do_ref[rows, cols]
                dzu_ref[rows, cols] = do * mixed * _dgelu(zu)
                dmix = do * _gelu(zu)
                dbt_ref[:, cols] += dmix
                dl, dr = (dmix * left).astype(BF16), (dmix * right).astype(BF16)
                dw_ref[2 * p] += lax.dot_general(dl, vl, nt_dims, preferred_element_type=F32)
                dw_ref[2 * p + 1] += lax.dot_general(dr, vr, nt_dims, preferred_element_type=F32)
                dvn_ref[rows, cols] = (lax.dot_general(w_ref[2 * p], dl, tn_dims, preferred_element_type=F32)
                                       + lax.dot_general(w_ref[2 * p + 1], dr, tn_dims, preferred_element_type=F32))
        dvn = dvn_ref[...]
        dg_ref[...] += jnp.sum(dvn * vhat, axis=0, keepdims=True)
        db_ref[...] += jnp.sum(dvn, axis=0, keepdims=True)
        dvh = dvn * g_ref[...]
        dv = rstd * (dvh - jnp.mean(dvh, axis=-1, keepdims=True) - vhat * jnp.mean(dvh * vhat, axis=-1, keepdims=True))
        dzv_ref[...] = dv * _dgelu(zv)

    vec = _vec_spec(MIX_HALF)
    row = _row_spec(MIX_HALF, SGU_TS)
    wspec = pl.BlockSpec((SGU_GROUPS, SGU_CHUNK, SGU_CHUNK), lambda i: (0, 0, 0))
    tspec = pl.BlockSpec((SGU_CHUNK, MIX_HALF), lambda i: (0, 0))
    full = jax.ShapeDtypeStruct((S, MIX_HALF), F32)
    v = jax.ShapeDtypeStruct((1, MIX_HALF), F32)
    return pl.pallas_call(
        body, name=name, grid=(S // SGU_TS,),
        in_specs=[row, pl.BlockSpec((SGU_TS, MIX_HALF), lambda i: (i, 1)), pl.BlockSpec((SGU_TS, MIX_HALF), lambda i: (i, 2)), vec, vec,
                  wspec, tspec],
        out_specs=[row, row, wspec, tspec, vec, vec],
        out_shape=[full, full, jax.ShapeDtypeStruct((SGU_GROUPS, SGU_CHUNK, SGU_CHUNK), F32),
                   jax.ShapeDtypeStruct((SGU_CHUNK, MIX_HALF), F32), v, v],
        scratch_shapes=[pltpu.VMEM((SGU_TS, MIX_HALF), F32)],
        compiler_params=_cparams("arbitrary"))(dout, proj, proj, ln_gain, ln_bias, wm, bias_tab)


def _sgu_grads(dw, dbias_tab):
    mask = jnp.tril(jnp.ones((SGU_CHUNK, SGU_CHUNK), dtype=bool))
    g_w = jnp.where(mask[None], dw, 0.0)
    g_b = dbias_tab.reshape(SGU_CHUNK, SGU_GROUPS, MIX_HALF // SGU_GROUPS).sum(axis=-1).T
    return g_w, g_b


def _head_avg_matrix(w):
    idx = np.arange(w) // HEAD_DIM
    return jnp.asarray((idx[:, None] == idx[None, :]).astype(np.float32) / HEAD_DIM, dtype=BF16)


def _head_mean(t, bavg):
    hi = t.astype(BF16)
    lo = (t - hi.astype(F32)).astype(BF16)
    return jnp.dot(hi, bavg, preferred_element_type=F32) + jnp.dot(lo, bavg, preferred_element_type=F32)


def _head_rms(t, bavg):
    r = lax.rsqrt(_head_mean(t * t, bavg) + EPS)
    return t * r, r


def _head_rms_bwd(dn, n, r, bavg):
    return r * (dn - n * _head_mean(dn * n, bavg))


GLA_TS = 512
C = GLA_CHUNK
NT_DIMS = (((1,), (1,)), ((), ()))
TN_DIMS = (((0,), (0,)), ((), ()))
HI = lax.Precision.HIGHEST


def _bdot(a, b, dims=(((1,), (0,)), ((), ()))):
    return lax.dot_general(a.astype(BF16), b.astype(BF16), dims, preferred_element_type=F32)


def _gla_chunk_terms(q, k, z):
    row = lax.broadcasted_iota(jnp.int32, (C, C), 0)
    col = lax.broadcasted_iota(jnp.int32, (C, C), 1)
    lc = _log_sigmoid(z) * (1.0 / GLA_TAU)
    b = lax.dot_general((row >= col).astype(F32), lc, (((1,), (0,)), ((), ())), precision=HI, preferred_element_type=F32)
    b_last = jnp.sum(lc, axis=0, keepdims=True)
    b_mid = b[C // 2:C // 2 + 1, :]
    scale = HEAD_DIM ** -0.5
    e_b, e_q, e_k, e_l = jnp.exp(b), jnp.exp(b - b_mid), jnp.exp(b_mid - b), jnp.exp(b_last - b)
    qs = q * (scale * e_b)
    qe = q * (scale * e_q)
    ke = k * e_k
    kl = k * e_l
    return dict(e_b=e_b, e_q=e_q, e_k=e_k, e_l=e_l, qs=qs, qe=qe, ke=ke, kl=kl, dec=jnp.exp(b_last), causal=row >= col, scale=scale)


def _pair(x, pp):
    return x[:, pp * LANES:(pp + 1) * LANES]


def _pair_block_diag():
    r = lax.broadcasted_iota(jnp.int32, (LANES, LANES), 0) // HEAD_DIM
    c = lax.broadcasted_iota(jnp.int32, (LANES, LANES), 1) // HEAD_DIM
    return (r == c).astype(F32)


def _gla_fwd(proj, z, name):
    S = proj.shape[0]
    nch = GLA_TS // C

    def body(q_ref, k_ref, v_ref, z_ref, o_ref, st_ref, state_ref):
        @pl.when(pl.program_id(0) == 0)
        def _():
            state_ref[...] = jnp.zeros_like(state_ref)

        left, right = _half_masks(C)
        bd = _pair_block_diag()
        pairs = range(N_PAIRS)
        for ch in range(nch):
            rows = pl.ds(ch * C, C)
            v = v_ref[rows, :]
            t = _gla_chunk_terms(q_ref[rows, :], k_ref[rows, :], z_ref[rows, :])
            sts = [state_ref[pp] for pp in pairs]
            for pp in pairs:
                st_ref[ch, pp] = sts[pp]
            os = [_bdot(_pair(t["qs"], pp), sts[pp], NT_DIMS) for pp in pairs]
            for m in (left, right):
                scores = [jnp.where(t["causal"], _bdot(_pair(t["qe"], pp) * m, _pair(t["ke"], pp), NT_DIMS), 0.0) for pp in pairs]
                os = [os[pp] + m * _bdot(scores[pp], _pair(v, pp)) for pp in pairs]
            o_ref[rows, :] = jnp.concatenate(os, axis=1)
            new = [sts[pp] * _pair(t["dec"], pp) + bd * _bdot(_pair(v, pp), _pair(t["kl"], pp), TN_DIMS) for pp in pairs]
            for pp in pairs:
                state_ref[pp] = new[pp]

    def col(cb):
        return pl.BlockSpec((GLA_TS, MIX_HALF), lambda i: (i, cb))

    return pl.pallas_call(
        body, name=name, grid=(S // GLA_TS,),
        in_specs=[col(0), col(1), col(2), col(0)],
        out_specs=[col(0), pl.BlockSpec((nch, N_PAIRS, LANES, LANES), lambda i: (i, 0, 0, 0))],
        out_shape=[jax.ShapeDtypeStruct((S, MIX_HALF), F32), jax.ShapeDtypeStruct((S // C, N_PAIRS, LANES, LANES), F32)],
        scratch_shapes=[pltpu.VMEM((N_PAIRS, LANES, LANES), F32)], compiler_params=_cparams("arbitrary"))(proj, proj, proj, z)


def _gla_bwd(do, proj, z, states, name):
    S = proj.shape[0]
    nch = GLA_TS // C
    nblk = S // GLA_TS

    def body(do_ref, q_ref, k_ref, v_ref, z_ref, st_ref, dq_ref, dk_ref, dv_ref, dlc_ref, dstate_ref):
        @pl.when(pl.program_id(0) == 0)
        def _():
            dstate_ref[...] = jnp.zeros_like(dstate_ref)

        left, right = _half_masks(C)
        bd = _pair_block_diag()
        rowi = lax.broadcasted_iota(jnp.int32, (C, LANES), 0)
        row = lax.broadcasted_iota(jnp.int32, (C, C), 0)
        colm = lax.broadcasted_iota(jnp.int32, (C, C), 1)
        pairs = range(N_PAIRS)
        rowi = lax.broadcasted_iota(jnp.int32, (C, MIX_HALF), 0)
        for ch in range(nch - 1, -1, -1):
            rows = pl.ds(ch * C, C)
            v, dov = v_ref[rows, :], do_ref[rows, :]
            t = _gla_chunk_terms(q_ref[rows, :], k_ref[rows, :], z_ref[rows, :])
            sts = [st_ref[ch, pp] for pp in pairs]
            nxt = [dstate_ref[pp] for pp in pairs]
            gs = [bd * nxt[pp] for pp in pairs]
            dqs = [_bdot(_pair(dov, pp), sts[pp]) for pp in pairs]
            dv = [_bdot(_pair(t["kl"], pp), gs[pp], NT_DIMS) for pp in pairs]
            dkl = [_bdot(_pair(v, pp), gs[pp]) for pp in pairs]
            dqe = [jnp.zeros((C, LANES), F32) for _ in pairs]
            dke = [jnp.zeros((C, LANES), F32) for _ in pairs]
            for m in (left, right):
                sc = [jnp.where(t["causal"], _bdot(_pair(t["qe"], pp) * m, _pair(t["ke"], pp), NT_DIMS), 0.0) for pp in pairs]
                dsc = [jnp.where(t["causal"], _bdot(_pair(dov, pp) * m, _pair(v, pp), NT_DIMS), 0.0) for pp in pairs]
                dv = [dv[pp] + m * _bdot(sc[pp], _pair(dov, pp), TN_DIMS) for pp in pairs]
                dqe = [dqe[pp] + m * _bdot(dsc[pp], _pair(t["ke"], pp)) for pp in pairs]
                dke = [dke[pp] + m * _bdot(dsc[pp], _pair(t["qe"], pp), TN_DIMS) for pp in pairs]
            for pp in pairs:
                dstate_ref[pp] = bd * (nxt[pp] * _pair(t["dec"], pp) + _bdot(_pair(dov, pp), _pair(t["qs"], pp), TN_DIMS))
            decay_sum = jnp.concatenate([jnp.sum(nxt[pp] * sts[pp], axis=0, keepdims=True) for pp in pairs], axis=1)
            dqs, dv, dkl, dqe, dke = (jnp.concatenate(parts, axis=1) for parts in (dqs, dv, dkl, dqe, dke))
            db_last = decay_sum * t["dec"] + jnp.sum(dkl * t["kl"], axis=0, keepdims=True)
            db = dqs * t["qs"] + dqe * t["qe"] - dke * t["ke"] - dkl * t["kl"]
            db = db + jnp.where(rowi == C - 1, db_last, 0.0)
            dq_ref[rows, :] = (dqs * t["e_b"] + dqe * t["e_q"]) * t["scale"]
            dk_ref[rows, :] = dke * t["e_k"] + dkl * t["e_l"]
            dv_ref[rows, :] = dv
            dlc_ref[rows, :] = lax.dot_general((colm >= row).astype(F32), db, (((1,), (0,)), ((), ())), precision=HI,
                                               preferred_element_type=F32)

    def col(cb):
        return pl.BlockSpec((GLA_TS, MIX_HALF), lambda i: (nblk - 1 - i, cb))

    full = jax.ShapeDtypeStruct((S, MIX_HALF), F32)
    return pl.pallas_call(
        body, name=name, grid=(nblk,),
        in_specs=[col(0), col(0), col(1), col(2), col(0), pl.BlockSpec((nch, N_PAIRS, LANES, LANES), lambda i: (nblk - 1 - i, 0, 0, 0))],
        out_specs=[col(0)] * 4, out_shape=[full, full, full, full],
        scratch_shapes=[pltpu.VMEM((N_PAIRS, LANES, LANES), F32)], compiler_params=_cparams("arbitrary"))(do, proj, proj, proj, z, states)


def _gla_block_fwd(proj, w_lr_pad, b_lr, gain, bavg, pfx):
    z = _mm(proj, w_lr_pad, "nn", pfx + "_z", a_cols=(7 * MIX_HALF, MIX_HALF), epi=lambda acc, b: acc + b, extras=[(b_lr, "n")])
    o, states = _gla_fwd(proj, z, pfx + "_core")

    def out(ov, gg, ba, gn):
        n, _ = _head_rms(ov, ba)
        return n * gn * (gg * _sigmoid(gg))

    og, = _ew(out, pfx + "_out", [o, (proj, MIX_HALF, 3)], consts=[bavg, gain], outs=[(MIX_HALF, F32)])
    return og, dict(z=z, o=o, states=states)


def _gla_block_bwd(dog, proj, w_lr_pad, gain, bavg, res, pfx):
    z, o, states = res["z"], res["o"], res["states"]

    def out_bwd(dy, ov, gg, ba, gn):
        n, r = _head_rms(ov, ba)
        sg = _sigmoid(gg)
        silu = gg * sg
        dn = dy * gn * silu
        do = _head_rms_bwd(dn, n, r, ba)
        dgg = dy * n * gn * (sg * (1.0 + gg * (1.0 - sg)))
        return do, dgg, jnp.sum(dy * n * silu, axis=0, keepdims=True)

    do, dgg, dgain = _ew(out_bwd, pfx + "_out_bwd", [dog, o, (proj, MIX_HALF, 3)], consts=[bavg, gain],
                         outs=[(MIX_HALF, F32), (MIX_HALF, F32)], sums=[MIX_HALF])
    dq, dk, dv, dlc = _gla_bwd(do, proj, z, states, pfx + "_core_bwd")

    def decay_bwd(dl, zv):
        dz = dl * (1.0 / GLA_TAU) * (1.0 - _sigmoid(zv))
        return dz, jnp.sum(dz, axis=0, keepdims=True)

    dz, db_lr = _ew(decay_bwd, pfx + "_decay_bwd", [dlc, z], outs=[(MIX_HALF, F32)], sums=[MIX_HALF])
    dw_lr_pad = _mm(proj, dz, "tn", pfx + "_dwlr", a_cols=(7 * MIX_HALF, MIX_HALF))
    dsmall = _mm(dz, w_lr_pad, "nt", pfx + "_dsmall")
    return (dq, dk, dv, dgg, dsmall), dict(w_lr=dw_lr_pad[:GLA_RANK], b_lr=db_lr.reshape(-1), gain=dgain.reshape(-1, HEAD_DIM))


FOX_T = 512
FOX_HEADS = MIX_HALF // HEAD_DIM
NEG = -1e30
CUM_T = 512


def _cum_lanes(x, name, reverse, pre=None):
    R, S = x.shape
    nb = S // CUM_T

    def body(x_ref, o_ref, carry_ref):
        @pl.when(pl.program_id(0) == 0)
        def _():
            carry_ref[...] = jnp.zeros_like(carry_ref)

        xv = x_ref[...]
        if pre is not None:
            xv = pre(xv)
        i = lax.broadcasted_iota(jnp.int32, (CUM_T, CUM_T), 0)
        j = lax.broadcasted_iota(jnp.int32, (CUM_T, CUM_T), 1)
        tri = ((i >= j) if reverse else (i <= j)).astype(F32)
        c = lax.dot_general(xv, tri, (((1,), (0,)), ((), ())), precision=HI, preferred_element_type=F32)
        carry = carry_ref[...]
        o_ref[...] = c + carry[:, 0:1]
        carry_ref[...] = carry + jnp.sum(xv, axis=1, keepdims=True)

    spec = pl.BlockSpec((R, CUM_T), (lambda i: (0, nb - 1 - i)) if reverse else (lambda i: (0, i)))
    return pl.pallas_call(body, name=name, grid=(nb,), in_specs=[spec], out_specs=spec, out_shape=jax.ShapeDtypeStruct((R, S), F32),
                          scratch_shapes=[pltpu.VMEM((R, LANES), F32)], compiler_params=_cparams("arbitrary"))(x)


def _fox_scores(q, k, cqb, ck_ref, h, m, diag):
    cq = cqb[:, h * HEAD_DIM:h * HEAD_DIM + 1]
    ck = ck_ref[0, h:h + 1, :]
    s = lax.dot_general(q * m.astype(q.dtype), k, NT_DIMS, preferred_element_type=F32) + (cq - ck)
    if not diag:
        return s
    row = lax.broadcasted_iota(jnp.int32, (FOX_T, FOX_T), 0)
    col = lax.broadcasted_iota(jnp.int32, (FOX_T, FOX_T), 1)
    return jnp.where(row < col, NEG, s)


def _on_causal_blocks(q_blk, k_blk, step):
    @pl.when(k_blk < q_blk)
    def _():
        step(False)

    @pl.when(k_blk == q_blk)
    def _():
        step(True)


def _causal_pairs(n, key_major):
    if key_major:
        pairs = [(q, k) for k in range(n) for q in range(k, n)]
    else:
        pairs = [(q, k) for q in range(n) for k in range(q + 1)]
    return jnp.asarray([p[0] for p in pairs], jnp.int32), jnp.asarray([p[1] for p in pairs], jnp.int32)


def _carried(carry, refs, n_in, n_out, first, last):
    if carry is None:
        return refs
    ins, cx_ref, outs, co_ref = refs[:n_in], refs[n_in], refs[n_in + 1:n_in + 1 + n_out], refs[n_in + 1 + n_out]
    scratch = refs[n_in + 2 + n_out:]
    start, finish = _exchange_plan(cx_ref, co_ref, *scratch[-3:], carry[1])
    pl.when(first)(start)
    pl.when(last)(finish)
    return ins + outs + scratch[:-3]


def _carry_specs(carry):
    if carry is None:
        return [], [], [], [], []
    x, bcast = carry
    blk = x.shape if bcast else x.shape[1:]
    return [ANY], [ANY], [jax.ShapeDtypeStruct((N_CHIPS,) + tuple(blk), x.dtype)], list(_EXCHANGE_SEMS), [x]


def _fox_fwd(qn, kn, proj, cum_b, cum_tp, name, carry=None):
    S = qn.shape[0]
    nq = S // FOX_T
    qidx, kidx = _causal_pairs(nq, False)
    ntri = int(qidx.shape[0])

    def body(qidx_ref, kidx_ref, *refs):
        t = pl.program_id(1)
        first = jnp.logical_and(pl.program_id(0) == 0, t == 0)
        last = jnp.logical_and(pl.program_id(0) == N_PAIRS - 1, t == ntri - 1)
        q_ref, k_ref, v_ref, cq_ref, ck_ref, o_ref, lse_ref, m_scr, acc_scr = _carried(carry, refs, 5, 2, first, last)
        qi, ki = qidx_ref[t], kidx_ref[t]

        @pl.when(ki == 0)
        def _():
            m_scr[...] = jnp.full_like(m_scr, NEG)
            acc_scr[...] = jnp.zeros_like(acc_scr)

        left, right = _half_masks(FOX_T)

        def step(diag):
            q, k, v = q_ref[...], k_ref[...], v_ref[...].astype(BF16)
            cqb = cq_ref[...]
            for h, m in enumerate((left, right)):
                s = _fox_scores(q, k, cqb, ck_ref, h, m, diag)
                m_prev = m_scr[h]
                m_new = jnp.maximum(m_prev, jnp.max(s, axis=1, keepdims=True))
                p = jnp.exp(s - m_new)
                v_h = jnp.where(m > 0, v, jnp.ones_like(v))
                acc_scr[h] = jnp.exp(m_prev - m_new) * acc_scr[h] + jnp.dot(p.astype(BF16), v_h, preferred_element_type=F32)
                m_scr[h] = m_new

        _on_causal_blocks(qi, ki, step)

        @pl.when(ki == qi)
        def _():
            a0, a1 = acc_scr[0], acc_scr[1]
            is_left = left > 0
            num = jnp.where(is_left, a0, a1)
            den = jnp.where(is_left, pltpu.roll(a0, HEAD_DIM, 1), pltpu.roll(a1, HEAD_DIM, 1))
            o_ref[...] = num / den
            lse_ref[...] = jnp.where(is_left, m_scr[0], m_scr[1]) + jnp.log(den)

    qspec = pl.BlockSpec((FOX_T, LANES), lambda p, t, qx, kx: (qx[t], p))
    kspec = pl.BlockSpec((FOX_T, LANES), lambda p, t, qx, kx: (kx[t], p))
    vspec = pl.BlockSpec((FOX_T, LANES), lambda p, t, qx, kx: (kx[t], 6 * N_PAIRS + p))
    ckspec = pl.BlockSpec((1, 8, FOX_T), lambda p, t, qx, kx: (p, 0, kx[t]))
    full = jax.ShapeDtypeStruct((S, MIX_HALF), F32)
    c_in, c_out, c_shape, c_scratch, c_args = _carry_specs(carry)
    grid_spec = pltpu.PrefetchScalarGridSpec(
        num_scalar_prefetch=2, grid=(N_PAIRS, ntri), in_specs=[qspec, kspec, vspec, qspec, ckspec] + c_in, out_specs=[qspec, qspec] + c_out,
        scratch_shapes=[pltpu.VMEM((2, FOX_T, 1), F32), pltpu.VMEM((2, FOX_T, LANES), F32)] + c_scratch)
    return pl.pallas_call(body, name=name, grid_spec=grid_spec, out_shape=[full, full] + c_shape,
                          compiler_params=_cparams("arbitrary", "arbitrary"))(qidx, kidx, qn, kn, proj, cum_b, cum_tp, *c_args)


def _fox_bwd(do, qn, kn, proj, cum_b, cum_tp, lse_b, delta_b, name, carry=None):
    S = qn.shape[0]
    nq = S // FOX_T
    scale = HEAD_DIM ** -0.5
    qidx, kidx = _causal_pairs(nq, True)
    ntri = int(qidx.shape[0])

    def body(qidx_ref, kidx_ref, *refs):
        t = pl.program_id(1)
        first = jnp.logical_and(pl.program_id(0) == 0, t == 0)
        last = jnp.logical_and(pl.program_id(0) == N_PAIRS - 1, t == ntri - 1)
        (do_ref, q_ref, k_ref, v_ref, cq_ref, ck_ref, lse_ref, dl_ref, dq_ref, dcq_ref, dk_ref, dv_ref, dck_ref,
         dq_scr, dk_scr, dv_scr) = _carried(carry, refs, 8, 5, first, last)
        qi, ki = qidx_ref[t], kidx_ref[t]

        @pl.when(t == 0)
        def _():
            dq_scr[...] = jnp.zeros_like(dq_scr)

        @pl.when(qi == ki)
        def _():
            dk_scr[...] = jnp.zeros_like(dk_scr)
            dv_scr[...] = jnp.zeros_like(dv_scr)

        left, right = _half_masks(FOX_T)
        rows = pl.ds(pl.multiple_of(qi * FOX_T, FOX_T), FOX_T)

        def step(diag):
            q, k, v, dov = q_ref[...], k_ref[...], v_ref[...].astype(BF16), do_ref[...]
            cqb, lseb, dlb = cq_ref[...], lse_ref[...], dl_ref[...]
            dob = dov.astype(BF16)
            heads = (0, 1)
            masks = (left, right)
            col = [slice(h * HEAD_DIM, h * HEAD_DIM + 1) for h in heads]
            ss = [_fox_scores(q, k, cqb, ck_ref, h, masks[h], diag) for h in heads]
            dps = [lax.dot_general((dov * masks[h]).astype(BF16), v, NT_DIMS, preferred_element_type=F32) for h in heads]
            ps = [jnp.exp(ss[h] - lseb[:, col[h]]) for h in heads]
            dss = [(ps[h] * (dps[h] - dlb[:, col[h]])).astype(BF16) for h in heads]
            pvs = [lax.dot_general(ps[h].astype(BF16), dob, TN_DIMS, preferred_element_type=F32) for h in heads]
            dks = [lax.dot_general(dss[h], jnp.where(masks[h] > 0, q, jnp.ones_like(q)), TN_DIMS, preferred_element_type=F32) for h in heads]
            dqs = [jnp.dot(dss[h], jnp.where(masks[h] > 0, k, jnp.ones_like(k)), preferred_element_type=F32) for h in heads]
            dv_scr[...] = dv_scr[...] + left * pvs[0] + right * pvs[1]
            for h in heads:
                dk_scr[h] = dk_scr[h] + dks[h]
                dq_scr[h, rows, :] = dq_scr[h, rows, :] + dqs[h]

        _on_causal_blocks(qi, ki, step)

        @pl.when(qi == nq - 1)
        def _():
            a0, a1 = dk_scr[0], dk_scr[1]
            dk_ref[...] = left * a0 + right * a1
            dv_ref[...] = dv_scr[...]
            dck_ref[...] = left * pltpu.roll(a0, HEAD_DIM, 1) + right * pltpu.roll(a1, HEAD_DIM, 1)

        @pl.when(t == ntri - 1)
        def _():
            for r in range(nq):
                blk = pl.ds(r * FOX_T, FOX_T)
                a0, a1 = dq_scr[0, blk, :], dq_scr[1, blk, :]
                dq_ref[blk, :] = (left * a0 + right * a1) * scale
                dcq_ref[blk, :] = left * pltpu.roll(a0, HEAD_DIM, 1) + right * pltpu.roll(a1, HEAD_DIM, 1)

    qspec = pl.BlockSpec((FOX_T, LANES), lambda p, t, qx, kx: (qx[t], p))
    kspec = pl.BlockSpec((FOX_T, LANES), lambda p, t, qx, kx: (kx[t], p))
    vspec = pl.BlockSpec((FOX_T, LANES), lambda p, t, qx, kx: (kx[t], 6 * N_PAIRS + p))
    ckspec = pl.BlockSpec((1, 8, FOX_T), lambda p, t, qx, kx: (p, 0, kx[t]))
    seq = pl.BlockSpec((S, LANES), lambda p, t, qx, kx: (0, p))
    full = jax.ShapeDtypeStruct((S, MIX_HALF), F32)
    c_in, c_out, c_shape, c_scratch, c_args = _carry_specs(carry)
    grid_spec = pltpu.PrefetchScalarGridSpec(
        num_scalar_prefetch=2, grid=(N_PAIRS, ntri), in_specs=[qspec, qspec, kspec, vspec, qspec, ckspec, qspec, qspec] + c_in,
        out_specs=[seq, seq, kspec, kspec, kspec] + c_out,
        scratch_shapes=[pltpu.VMEM((2, S, LANES), F32), pltpu.VMEM((2, FOX_T, LANES), F32), pltpu.VMEM((FOX_T, LANES), F32)] + c_scratch)
    return pl.pallas_call(body, name=name, grid_spec=grid_spec, out_shape=[full] * 5 + c_shape,
                          compiler_params=_cparams("arbitrary", "arbitrary"))(qidx, kidx, do, qn, kn, proj, cum_b, cum_tp, lse_b, delta_b, *c_args)


def _ff_bwd(rc, f_t, name):
    def body(rc_ref, f_ref, d_ref, s_ref):
        d = rc_ref[...] * (1.0 - _sigmoid(f_ref[...]))
        d_ref[...] = d
        s_ref[...] = jnp.sum(d, axis=1, keepdims=True)

    return pl.pallas_call(body, name=name, out_shape=[jax.ShapeDtypeStruct(rc.shape, F32), jax.ShapeDtypeStruct((rc.shape[0], 1), F32)])(rc, f_t)


def _fox_block_fwd(proj, b_f, q_gain, k_gain, bavg, pfx, carry=None):
    S = proj.shape[0]

    def prep(qv, kv, ba, qg, kg):
        return _head_rms(qv, ba)[0] * qg * (HEAD_DIM ** -0.5), _head_rms(kv, ba)[0] * kg

    qn, kn = _ew(prep, pfx + "_prep", [(proj, MIX_HALF, 4), (proj, MIX_HALF, 5)], consts=[bavg, q_gain, k_gain],
                 outs=[(MIX_HALF, BF16), (MIX_HALF, BF16)])
    f0 = 7 * MIX_HALF + GLA_RANK
    f_t = proj[:, f0:f0 + FOX_HEADS].T + b_f.reshape(FOX_HEADS, 1)
    cum = _cum_lanes(f_t, pfx + "_cum", False, pre=_log_sigmoid)
    cum_b = jnp.repeat(cum.T, HEAD_DIM, axis=1)
    cum_tp = jnp.pad(cum.reshape(N_PAIRS, 2, S), ((0, 0), (0, 6), (0, 0)))
    o, lse_b, *carried = _fox_fwd(qn, kn, proj, cum_b, cum_tp, pfx + "_attn", carry=carry)
    return o, dict(qn=qn, kn=kn, f_t=f_t, cum_b=cum_b, cum_tp=cum_tp, o=o, lse_b=lse_b), carried


def _fox_block_bwd(do, proj, q_gain, k_gain, bavg, res, pfx, carry=None):
    qn, kn, o = res["qn"], res["kn"], res["o"]
    S = proj.shape[0]
    delta_b, = _ew(lambda a, b, ba: _head_mean(a * b, ba) * float(HEAD_DIM), pfx + "_delta", [do, o], consts=[bavg], outs=[(MIX_HALF, F32)])
    args = (do, qn, kn, proj, res["cum_b"], res["cum_tp"], res["lse_b"], delta_b)
    dqn, dcq_b, dkn, dv, dck_b, *carried = _fox_bwd(*args, pfx + "_bwd", carry=carry)

    def prep_bwd(dq, dk, qv, kv, ba, qg, kg):
        nq, rq = _head_rms(qv, ba)
        nk, rk = _head_rms(kv, ba)
        return (_head_rms_bwd(dq * qg, nq, rq, ba), _head_rms_bwd(dk * kg, nk, rk, ba),
                jnp.sum(dq * nq, axis=0, keepdims=True), jnp.sum(dk * nk, axis=0, keepdims=True))

    dfq, dfk, dqg, dkg = _ew(prep_bwd, pfx + "_prep_bwd", [dqn, dkn, (proj, MIX_HALF, 4), (proj, MIX_HALF, 5)],
                             consts=[bavg, q_gain, k_gain], outs=[(MIX_HALF, F32), (MIX_HALF, F32)], sums=[MIX_HALF, MIX_HALF])
    dcum = (dcq_b - dck_b)[:, ::HEAD_DIM].T
    rc = _cum_lanes(dcum, pfx + "_rcum", True)
    dff_t, db_f = _ff_bwd(rc, res["f_t"], pfx + "_ff_bwd")
    grads = dict(b_f=db_f.reshape(-1), q_gain=dqg.reshape(-1, HEAD_DIM), k_gain=dkg.reshape(-1, HEAD_DIM))
    return (dfq, dfk, dv, dff_t.T), grads, carried


WEIGHTS = ['ada_w', 'ada_b', 'even_w_in', 'even_w_out', 'gla_w_lr', 'gla_b_lr', 'gla_gain', 'fox_b_f', 'fox_q_gain', 'fox_k_gain',
           'odd_w_in', 'odd_w_out', 's5_lam_re', 's5_lam_im', 's5_log_dt', 's5_b_re', 's5_b_im', 's5_c_re', 's5_c_im', 's5_d',
           's5_w_glu', 's5_b_glu', 'sgu_ln_gain', 'sgu_ln_bias', 'sgu_w_s', 'sgu_b_s', 'mlp_w1', 'mlp_w2']
ARGS = ['x', 'c'] + WEIGHTS + ['loss_target'] + ['m_' + w for w in WEIGHTS] + ['v_' + w for w in WEIGHTS]

EVEN_COLS = 3608
EVEN_PAD = 8 * MIX_HALF
MOD = 6 * D_MODEL
MOD_SHARD = MOD // N_CHIPS

PACK_COLS = 1024
EVEN_SHARD = EVEN_COLS // N_CHIPS
SHARDED = (
    ([("even_w_in", (1, 1024, PACK_COLS), 2), ("even_w_out", (1, 256, 1024), 1), ("gla_w_lr", (1, 16, 128), 2)], 1536),
    ([("mlp_w1_0", (1, 1024, 1024), 2), ("mlp_w2_0", (1, 1024, 1024), 1), ("odd_w_in", (1, 1024, 384), 2),
      ("odd_w_out", (1, 256, 1024), 1), ("mlp_w1_1", (1, 1024, 1024), 2), ("mlp_w2_1", (1, 1024, 1024), 1),
      ("s5_w_glu", (1, 128, 512), 1), ("s5_b_glu", (1, 128), 1), ("sgu_ln_gain", (1, 128), 1), ("sgu_ln_bias", (1, 128), 1)], 5120))
REPLICATED = [("gla_b_lr", (1, 512)), ("gla_gain", (1, 8, 64)), ("fox_b_f", (1, 8)), ("fox_q_gain", (1, 8, 64)),
              ("fox_k_gain", (1, 8, 64)), ("s5_lam_re", (1, 32, 64)), ("s5_lam_im", (1, 32, 64)), ("s5_log_dt", (1, 32)),
              ("s5_b_re", (1, 32, 64, 16)), ("s5_b_im", (1, 32, 64, 16)), ("s5_c_re", (1, 32, 16, 64)), ("s5_c_im", (1, 32, 16, 64)),
              ("s5_d", (1, 32, 16)), ("sgu_w_s", (1, 8, 128, 128)), ("sgu_b_s", (1, 8, 128))]
SMALL_ROWS = 512
BIG_ADAM = {"ada_w": (2048, 1536), "even_w_in": (1024, 902), "even_w_out": (256, 1024), "odd_w_in": (1024, 384),
            "odd_w_out": (256, 1024), "mlp_w1": (2048, 1024), "mlp_w2": (2048, 1024), "s5_w_glu": (128, 512)}


PACK_ALIGN = 16


def _piece_rows(shape):
    rows = -(-math.prod(shape) // PACK_COLS)
    return -(-rows // PACK_ALIGN) * PACK_ALIGN


def _to_rows(p, lead=()):
    n = math.prod(p.shape[len(lead):])
    rows = _piece_rows(p.shape[len(lead):])
    flat = p.reshape(lead + (n,))
    if rows * PACK_COLS != n:
        flat = jnp.pad(flat, [(0, 0)] * len(lead) + [(0, rows * PACK_COLS - n)])
    return flat.reshape(lead + (rows, PACK_COLS))


def _from_rows(x, r0, shape, lead=()):
    n = math.prod(shape)
    seg = lax.slice_in_dim(x, r0, r0 + _piece_rows(shape), axis=len(lead)).reshape(lead + (-1,))
    return lax.slice_in_dim(seg, 0, n, axis=len(lead)).reshape(lead + tuple(shape))


def _pack_rows(pieces, rows):
    x = jnp.concatenate([_to_rows(p) for p in pieces], axis=0)
    return jnp.pad(x, ((0, rows - x.shape[0]), (0, 0)))


def _unpack(x, specs):
    out, r0 = {}, 0
    for name, shape in specs:
        out[name] = _from_rows(x, r0, shape)
        r0 += _piece_rows(shape)
    return out


def _shards_to_full(x4, pieces):
    out, r0 = {}, 0
    for name, shape, axis in pieces:
        seg = _from_rows(x4, r0, shape, lead=(N_CHIPS,))
        out[name] = jnp.concatenate([seg[k] for k in range(N_CHIPS)], axis=axis)
        r0 += _piece_rows(shape)
    return out


def _full_to_shards(full, pieces, rows):
    blocks = [_to_rows(jnp.stack(jnp.split(full[name], N_CHIPS, axis=axis)), lead=(N_CHIPS,)) for name, _, axis in pieces]
    x = jnp.concatenate(blocks, axis=1)
    return jnp.pad(x, ((0, 0), (0, rows - x.shape[1]), (0, 0)))


def _gather_prep(local, pieces, rows):
    shard = _pack_rows([local[n] for n, _, _ in pieces], rows).astype(BF16)
    return lax.dynamic_slice_in_dim(shard, lax.axis_index("c") * (rows // 2), rows // 2, axis=0)


def _gather_finish(collected, pieces, rows, tag):
    halves = _by_core(collected, _pair_swap(collected, tag + "_pair"))
    return _shards_to_full(halves.transpose(1, 0, 2, 3).reshape(N_CHIPS, rows, PACK_COLS), pieces)


def _reduce_prep(full, pieces, rows, tag):
    mc = lax.axis_index("c")
    packed = _full_to_shards(full, pieces, rows)
    hr = rows // 2
    mine = lax.dynamic_slice_in_dim(packed, mc * hr, hr, axis=1)
    other = lax.dynamic_slice_in_dim(packed, (1 - mc) * hr, hr, axis=1)
    theirs = _pair_swap(other, tag + "_pair")
    pair_sum, = _ew(lambda p, q: p + q, tag + "_pair_sum", [mine.reshape(N_CHIPS * hr, PACK_COLS), theirs.reshape(N_CHIPS * hr, PACK_COLS)],
                    outs=[(PACK_COLS, BF16)])
    return pair_sum.reshape(N_CHIPS, hr, PACK_COLS)


def _reduce_finish(arrived, pieces, rows, tag):
    red_half = _sum_slots(arrived, tag + "_chip_sum")
    reduced = _by_core(red_half, _pair_swap(red_half, tag + "_pair_out")).reshape(rows, PACK_COLS)
    return _unpack(reduced, [(n, s) for n, s, _ in pieces])


def _relu2(t):
    r = jnp.maximum(t, 0.0)
    return r * r


def _silu(t):
    return t * _sigmoid(t)


def _pack_even(w):
    return jnp.concatenate([w[:, :2048], w[:, 2064:3600], w[:, 2048:2064], w[:, 3600:3608],
                            jnp.zeros((w.shape[0], EVEN_PAD - EVEN_COLS), w.dtype)], axis=1)


def _unpack_even(wp):
    return jnp.concatenate([wp[:, :2048], wp[:, 3584:3600], wp[:, 2048:3584], wp[:, 3600:3608]], axis=1)


def _mlp_fwd(h, w1, w2, pfx):
    pre = _mm(h, w1, "nn", pfx + "_up", out_dtype=BF16)
    return pre, _mm(pre, w2, "nn", pfx + "_down", a_pro=_relu2)


def _mlp_bwd(dm, h, pre, w1, w2, pfx):
    dpre = _mm(dm, w2, "nt", pfx + "_dpre", epi=lambda acc, p: acc * (2.0 * jnp.maximum(p, 0.0)), extras=[(pre, "mn")], out_dtype=BF16)
    dw2 = _mm(pre, dm, "tn", pfx + "_dw2", a_pro=_relu2)
    dw1 = _mm(h, dpre, "tn", pfx + "_dw1")
    dh = _mm(dpre, w1, "nt", pfx + "_dh")
    return dh, dw1, dw2


def _step(args):
    a = dict(zip(ARGS, args, strict=True))
    x0 = a["x"][0]
    target = a["loss_target"][0]
    mx, my, mc = lax.axis_index("x"), lax.axis_index("y"), lax.axis_index("c")
    chip = 2 * mx + my
    dev = 2 * chip + mc
    bavg = _head_avg_matrix(MIX_HALF)

    c_all = _gather8(jnp.pad(a["c"], ((0, 7), (0, 0))), "c_gather")[:, :, 0, :].reshape(2 * N_CHIPS, D_MODEL)
    ada_b_shard = lax.dynamic_slice_in_dim(a["ada_b"], chip * MOD_SHARD, MOD_SHARD, axis=1)
    mod_sh = [_mm(c_all, a["ada_w"][l], "nn", f"mod{l}", a_pro=_silu, epi=lambda acc, b: acc + b, extras=[(ada_b_shard[l:l + 1], "n")])
              for l in range(2)]
    small3 = jnp.zeros((8, MOD_SHARD), F32)
    for r, n in enumerate(("s5_b_glu", "sgu_ln_gain", "sgu_ln_bias")):
        small3 = small3.at[r, :LANES].set(a[n][0])
    mod_all = _chip_exchange(jnp.concatenate(mod_sh + [small3]), "mod_gather", True)
    mods = []
    for l in range(2):
        full = mod_all[:, 8 * l:8 * l + 8].transpose(1, 0, 2).reshape(8, MOD)
        mods.append(jnp.split(lax.dynamic_slice_in_dim(full, dev, 1, axis=0), 6, axis=1))
    b_glu, ln_gain, ln_bias = (mod_all[:, 16 + r, :LANES].reshape(1, MIX_HALF) for r in range(3))

    local = dict(a, even_w_in=jnp.pad(a["even_w_in"], ((0, 0), (0, 0), (0, PACK_COLS - EVEN_SHARD))),
                 mlp_w1_0=a["mlp_w1"][0:1], mlp_w1_1=a["mlp_w1"][1:2], mlp_w2_0=a["mlp_w2"][0:1], mlp_w2_1=a["mlp_w2"][1:2])
    (pieces0, rows0), (pieces1, rows1) = SHARDED
    w = _gather_finish(_chip_exchange(_gather_prep(local, pieces0, rows0), "w0_chips", True), pieces0, rows0, "w0")
    w_even = _pack_even(w["even_w_in"][0].reshape(D_MODEL, N_CHIPS, PACK_COLS)[:, :, :EVEN_SHARD].reshape(D_MODEL, EVEN_COLS))
    w_lr_pad = jnp.zeros((MIX_HALF, MIX_HALF), BF16).at[:GLA_RANK].set(w["gla_w_lr"][0])
    gla_b_lr = a["gla_b_lr"]
    gla_gain, q_gain, k_gain = (a[n].reshape(1, MIX_HALF) for n in ("gla_gain", "fox_q_gain", "fox_k_gain"))
    sgu_wm, sgu_bt = _sgu_tables(a["sgu_w_s"][0], a["sgu_b_s"][0])

    sh1, sc1, g1, sh2, sc2, g2 = mods[0]
    _, h1_0 = _res_rms(x0, sc1, sh1, "l0_norm1")
    proj0 = _mm(h1_0, w_even, "nn", "l0_proj")
    og, gla_res = _gla_block_fwd(proj0, w_lr_pad, gla_b_lr, gla_gain, bavg, "gla")
    of, fox_res, (collected1,) = _fox_block_fwd(proj0, a["fox_b_f"][0], q_gain, k_gain, bavg, "fox",
                                                carry=(_gather_prep(local, pieces1, rows1), True))
    w.update(_gather_finish(collected1, pieces1, rows1, "w1"))
    s5w = dict(lam_re=a["s5_lam_re"][0], lam_im=a["s5_lam_im"][0], log_dt=a["s5_log_dt"][0], b_re=a["s5_b_re"][0], b_im=a["s5_b_im"][0],
               c_re=a["s5_c_re"][0], c_im=a["s5_c_im"][0], d=a["s5_d"][0], w_glu=w["s5_w_glu"][0], b_glu=b_glu)
    mixed0 = jnp.concatenate([og, of], axis=1).astype(BF16)
    y0 = _mm(mixed0, w["even_w_out"][0], "nn", "l0_out")
    x1, h2_0 = _res_rms(x0, sc2, sh2, "l0_norm2", y=y0, g=g1)
    pre0, m0 = _mlp_fwd(h2_0, w["mlp_w1_0"][0], w["mlp_w2_0"][0], "l0_mlp")
    sh1b, sc1b, g1b, sh2b, sc2b, g2b = mods[1]
    x2, h1_1 = _res_rms(x1, sc1b, sh1b, "l1_norm1", y=m0, g=g2)
    proj1 = _mm(h1_1, w["odd_w_in"][0], "nn", "l1_proj")
    ys5, s5_res = _s5_block_fwd(proj1[:, :MIX_HALF], s5w, "s5")
    ysgu = _sgu_fwd(proj1, ln_gain, ln_bias, sgu_wm, sgu_bt, "sgu")
    mixed1 = jnp.concatenate([ys5, ysgu], axis=1).astype(BF16)
    y1 = _mm(mixed1, w["odd_w_out"][0], "nn", "l1_out")
    x3, h2_1 = _res_rms(x2, sc2b, sh2b, "l1_norm2", y=y1, g=g1b)
    pre1, m1 = _mlp_fwd(h2_1, w["mlp_w1_1"][0], w["mlp_w2_1"][0], "l1_mlp")
    loss_b, dx4, dm1, dg2b = _res_loss(x3, m1, g2b, target, "loss")
    loss = lax.psum(loss_b[0, 0], ("x", "y", "c"))

    full = {}
    dh2_1, dw1_1, dw2_1 = _mlp_bwd(dm1, h2_1, pre1, w["mlp_w1_1"][0], w["mlp_w2_1"][0], "l1_mlp")
    dx3, dy1, dg1b, dsc2b, dsh2b = _res_rms_bwd(x3, dh2_1, sc2b, dx4, "l1_norm2_bwd", y=y1, g=g1b)
    dmixed1 = _mm(dy1, w["odd_w_out"][0], "nt", "l1_out_dx")
    full["odd_w_out"] = _mm(mixed1, dy1, "tn", "l1_out_dw")[None]
    du, s5g = _s5_block_bwd(dmixed1[:, :MIX_HALF], s5w, s5_res, "s5")
    dzu, dzv, dws, dbt, dlg, dlb = _sgu_bwd(dmixed1[:, MIX_HALF:], proj1, ln_gain, ln_bias, sgu_wm, sgu_bt, "sgu_bwd")
    g_ws, g_bs = _sgu_grads(dws, dbt)
    dproj1 = jnp.concatenate([du, dzu, dzv], axis=1).astype(BF16)
    full["odd_w_in"] = _mm(h1_1, dproj1, "tn", "l1_proj_dw")[None]
    dh1_1 = _mm(dproj1, w["odd_w_in"][0], "nt", "l1_proj_dx")
    dx2, dm0, dg2, dsc1b, dsh1b = _res_rms_bwd(x2, dh1_1, sc1b, dx3, "l1_norm1_bwd", y=m0, g=g2)
    dh2_0, dw1_0, dw2_0 = _mlp_bwd(dm0, h2_0, pre0, w["mlp_w1_0"][0], w["mlp_w2_0"][0], "l0_mlp")
    full.update(mlp_w1_0=dw1_0[None], mlp_w2_0=dw2_0[None], mlp_w1_1=dw1_1[None], mlp_w2_1=dw2_1[None], s5_w_glu=s5g["w_glu"][None],
                s5_b_glu=s5g["b_glu"][None], sgu_ln_gain=dlg, sgu_ln_bias=dlb)
    pair_sums1 = _reduce_prep(full, pieces1, rows1, "g1")
    dx1, dy0, dg1, dsc2, dsh2 = _res_rms_bwd(x1, dh2_0, sc2, dx2, "l0_norm2_bwd", y=y0, g=g1)
    dmixed0 = _mm(dy0, w["even_w_out"][0], "nt", "l0_out_dx")
    full["even_w_out"] = _mm(mixed0, dy0, "tn", "l0_out_dw")[None]
    (dgq, dgk, dgv, dgg, dsmall), glag = _gla_block_bwd(dmixed0[:, :MIX_HALF], proj0, w_lr_pad, gla_gain, bavg, gla_res, "gla")
    (dfq, dfk, dfv, dff), foxg, (arrived1,) = _fox_block_bwd(dmixed0[:, MIX_HALF:], proj0, q_gain, k_gain, bavg, fox_res, "fox",
                                                           carry=(pair_sums1, False))
    dsmall = lax.dynamic_update_slice(dsmall, dff, (0, GLA_RANK))
    dproj0 = jnp.concatenate([dgq, dgk, dgv, dgg, dfq, dfk, dfv, dsmall], axis=1).astype(BF16)
    d_even = _unpack_even(_mm(h1_0, dproj0, "tn", "l0_proj_dw")).reshape(D_MODEL, N_CHIPS, EVEN_SHARD)
    full["even_w_in"] = jnp.pad(d_even, ((0, 0), (0, 0), (0, PACK_COLS - EVEN_SHARD))).reshape(1, D_MODEL, N_CHIPS * PACK_COLS)
    dh1_0 = _mm(dproj0, w_even, "nt", "l0_proj_dx")
    grad_x, dsc1, dsh1 = _res_rms_bwd(x0, dh1_0, sc1, dx1, "l0_norm1_bwd")
    full["gla_w_lr"] = glag["w_lr"][None]

    dmod = jnp.concatenate([dsh1, dsc1, dg1, dsh2, dsc2, dg2, dsh1b, dsc1b, dg1b, dsh2b, dsc2b, dg2b], axis=1)
    dmod_all = _gather8(jnp.pad(dmod, ((0, 7), (0, 0))), "dmod_gather")[:, :, 0, :].reshape(2 * N_CHIPS, 2, MOD)
    grads = {}
    grads["ada_w"] = jnp.stack([
        _mm(c_all, lax.dynamic_slice_in_dim(dmod_all[:, l], chip * MOD_SHARD, MOD_SHARD, axis=1), "tn", f"ada_dw{l}", a_pro=_silu)
        for l in range(2)])
    grads["ada_b"] = _sum_slots(dmod_all.reshape(2 * N_CHIPS, 2 * MOD // MIX_HALF, MIX_HALF), "ada_db").reshape(2, MOD)

    grads.update(_reduce_finish(arrived1, pieces1, rows1, "g1"))
    grads.update(_reduce_finish(_chip_exchange(_reduce_prep(full, pieces0, rows0, "g0"), "g0_chips", False), pieces0, rows0, "g0"))
    grads["even_w_in"] = grads["even_w_in"][:, :, :EVEN_SHARD]
    grads["mlp_w1"] = jnp.concatenate([grads.pop("mlp_w1_0"), grads.pop("mlp_w1_1")])
    grads["mlp_w2"] = jnp.concatenate([grads.pop("mlp_w2_0"), grads.pop("mlp_w2_1")])

    part = dict(gla_b_lr=glag["b_lr"], gla_gain=glag["gain"], fox_b_f=foxg["b_f"], fox_q_gain=foxg["q_gain"], fox_k_gain=foxg["k_gain"],
                s5_lam_re=s5g["lam_re"], s5_lam_im=s5g["lam_im"], s5_log_dt=s5g["log_dt"], s5_b_re=s5g["b_re"], s5_b_im=s5g["b_im"],
                s5_c_re=s5g["c_re"], s5_c_im=s5g["c_im"], s5_d=s5g["d"], sgu_w_s=g_ws, sgu_b_s=g_bs)
    parts_all = _gather8(_pack_rows([part[n] for n, _ in REPLICATED], SMALL_ROWS).astype(BF16), "rep_gather")
    rep = _sum_slots(parts_all.reshape(2 * N_CHIPS, SMALL_ROWS, PACK_COLS), "rep_sum")
    grads.update(_unpack(rep, REPLICATED))

    delta, new_m, new_v = {}, {}, {}
    for n, shape2 in BIG_ADAM.items():
        d, nm, nv = _adamw(a[n].reshape(shape2), grads[n].reshape(shape2), a["m_" + n].reshape(shape2), a["v_" + n].reshape(shape2), "adamw_" + n)
        delta[n], new_m[n], new_v[n] = (t.reshape(a[n].shape) for t in (d, nm, nv))
    small = [n for n in WEIGHTS if n not in BIG_ADAM]
    spec = [(n, a[n].shape) for n in small]
    packs = [_pack_rows([src[n] for n in small], SMALL_ROWS) for src in
             (a, grads, {n: a["m_" + n] for n in small}, {n: a["v_" + n] for n in small})]
    for tgt, res in zip((delta, new_m, new_v), _adamw(*packs, "adamw_small")):
        tgt.update(_unpack(res, spec))
    outs = [loss, grad_x[None]]
    for group in (grads, delta, new_m, new_v):
        outs += [group[n].reshape(a[n].shape) for n in WEIGHTS]
    return tuple(outs)


def kernel(x, c, ada_w, ada_b, even_w_in, even_w_out, gla_w_lr, gla_b_lr, gla_gain, fox_b_f, fox_q_gain, fox_k_gain, odd_w_in,
           odd_w_out, s5_lam_re, s5_lam_im, s5_log_dt, s5_b_re, s5_b_im, s5_c_re, s5_c_im, s5_d, s5_w_glu, s5_b_glu, sgu_ln_gain,
           sgu_ln_bias, sgu_w_s, sgu_b_s, mlp_w1, mlp_w2, loss_target, m_ada_w, m_ada_b, m_even_w_in, m_even_w_out, m_gla_w_lr,
           m_gla_b_lr, m_gla_gain, m_fox_b_f, m_fox_q_gain, m_fox_k_gain, m_odd_w_in, m_odd_w_out, m_s5_lam_re, m_s5_lam_im,
           m_s5_log_dt, m_s5_b_re, m_s5_b_im, m_s5_c_re, m_s5_c_im, m_s5_d, m_s5_w_glu, m_s5_b_glu, m_sgu_ln_gain, m_sgu_ln_bias,
           m_sgu_w_s, m_sgu_b_s, m_mlp_w1, m_mlp_w2, v_ada_w, v_ada_b, v_even_w_in, v_even_w_out, v_gla_w_lr, v_gla_b_lr,
           v_gla_gain, v_fox_b_f, v_fox_q_gain, v_fox_k_gain, v_odd_w_in, v_odd_w_out, v_s5_lam_re, v_s5_lam_im, v_s5_log_dt,
           v_s5_b_re, v_s5_b_im, v_s5_c_re, v_s5_c_im, v_s5_d, v_s5_w_glu, v_s5_b_glu, v_sgu_ln_gain, v_sgu_ln_bias, v_sgu_w_s,
           v_sgu_b_s, v_mlp_w1, v_mlp_w2):
    return _step((x, c, ada_w, ada_b, even_w_in, even_w_out, gla_w_lr, gla_b_lr, gla_gain, fox_b_f, fox_q_gain, fox_k_gain,
                  odd_w_in, odd_w_out, s5_lam_re, s5_lam_im, s5_log_dt, s5_b_re, s5_b_im, s5_c_re, s5_c_im, s5_d, s5_w_glu,
                  s5_b_glu, sgu_ln_gain, sgu_ln_bias, sgu_w_s, sgu_b_s, mlp_w1, mlp_w2, loss_target, m_ada_w, m_ada_b,
                  m_even_w_in, m_even_w_out, m_gla_w_lr, m_gla_b_lr, m_gla_gain, m_fox_b_f, m_fox_q_gain, m_fox_k_gain,
                  m_odd_w_in, m_odd_w_out, m_s5_lam_re, m_s5_lam_im, m_s5_log_dt, m_s5_b_re, m_s5_b_im, m_s5_c_re, m_s5_c_im,
                  m_s5_d, m_s5_w_glu, m_s5_b_glu, m_sgu_ln_gain, m_sgu_ln_bias, m_sgu_w_s, m_sgu_b_s, m_mlp_w1, m_mlp_w2, v_ada_w,
                  v_ada_b, v_even_w_in, v_even_w_out, v_gla_w_lr, v_gla_b_lr, v_gla_gain, v_fox_b_f, v_fox_q_gain, v_fox_k_gain,
                  v_odd_w_in, v_odd_w_out, v_s5_lam_re, v_s5_lam_im, v_s5_log_dt, v_s5_b_re, v_s5_b_im, v_s5_c_re, v_s5_c_im,
                  v_s5_d, v_s5_w_glu, v_s5_b_glu, v_sgu_ln_gain, v_sgu_ln_bias, v_sgu_w_s, v_sgu_b_s, v_mlp_w1, v_mlp_w2))
```

```python
import functools
import math

import jax
import jax.numpy as jnp
import numpy as np
from jax import lax
from jax.experimental import pallas as pl
from jax.experimental.pallas import tpu as pltpu

F32 = jnp.float32
BF16 = jnp.bfloat16
MESH = pl.DeviceIdType.MESH
ANY = pl.BlockSpec(memory_space=pl.ANY)
DMA_SEM = pltpu.SemaphoreType.DMA

D_MODEL = 1024
HEAD_DIM = 64
MIX_HALF = 512
GLA_RANK = 16
GLA_TAU = 16.0
GLA_CHUNK = 64
S5_GROUPS = 32
S5_GROUP_WIDTH = 16
S5_STATE = 64
S5_N = S5_GROUPS * S5_STATE
SGU_GROUPS = 8
SGU_CHUNK = 128
D_FF = 4096
EPS = 1e-6
N_CHIPS = 4
LANES = 128
VMEM_LIMIT = 48 * 1024 * 1024
PAIR_COPIES = 16

ADAM_LR = 0.001
ADAM_B1 = 0.9
ADAM_B2 = 0.999
ADAM_EPS = 1e-08
ADAM_WD = 0.01
ADAM_STEP = 10


def _cparams(*sem):
    return pltpu.CompilerParams(dimension_semantics=sem, vmem_limit_bytes=VMEM_LIMIT)


def _pair_swap(x, name):
    lead = x.shape[:-2]
    rows = x.shape[-2]
    nsplit = max(1, PAIR_COPIES // max(1, math.prod(lead)))
    while nsplit > 1 and rows % (nsplit * 16):
        nsplit -= 1
    pieces = [idx + (pl.ds(j * (rows // nsplit), rows // nsplit),) for idx in np.ndindex(*lead) for j in range(nsplit)]

    def body(x_ref, o_ref, send_sems, recv_sems):
        mx, my, mc = lax.axis_index("x"), lax.axis_index("y"), lax.axis_index("c")
        copies = [pltpu.make_async_remote_copy(src_ref=x_ref.at[p], dst_ref=o_ref.at[p], send_sem=send_sems.at[j], recv_sem=recv_sems.at[j],
                                               device_id=(mx, my, 1 - mc), device_id_type=MESH) for j, p in enumerate(pieces)]
        for cp in copies:
            cp.start()
        for cp in copies:
            cp.wait_recv()
        for cp in copies:
            cp.wait_send()

    return pl.pallas_call(
        body, name=name, out_shape=jax.ShapeDtypeStruct(x.shape, x.dtype), in_specs=[ANY], out_specs=ANY,
        scratch_shapes=[DMA_SEM((len(pieces),)), DMA_SEM((len(pieces),))])(x)


def _by_core(mine, theirs):
    first = lax.axis_index("c") == 0
    return jnp.stack([jnp.where(first, mine, theirs), jnp.where(first, theirs, mine)])


def _chip_exchange(x, name, bcast):
    blk = x.shape if bcast else x.shape[1:]

    def body(x_ref, o_ref, send_sems, recv_sems, loc_sem):
        start, finish = _exchange_plan(x_ref, o_ref, send_sems, recv_sems, loc_sem, bcast)
        start()
        finish()

    return pl.pallas_call(
        body, name=name, out_shape=jax.ShapeDtypeStruct((N_CHIPS,) + tuple(blk), x.dtype), in_specs=[ANY], out_specs=ANY,
        scratch_shapes=_EXCHANGE_SEMS)(x)


_EXCHANGE_SEMS = [DMA_SEM((3,)), DMA_SEM((3,)), DMA_SEM]


def _exchange_plan(x_ref, o_ref, send_sems, recv_sems, loc_sem, bcast):
    mx, my, mc = lax.axis_index("x"), lax.axis_index("y"), lax.axis_index("c")
    me = 2 * mx + my
    peers = [(1 - mx, my), (mx, 1 - my), (1 - mx, 1 - my)]

    def src(k):
        return x_ref if bcast else x_ref.at[k]

    def remote(j, source, slot):
        px, py = peers[j]
        return pltpu.make_async_remote_copy(src_ref=source, dst_ref=o_ref.at[slot], send_sem=send_sems.at[j], recv_sem=recv_sems.at[j],
                                            device_id=(px, py, mc), device_id_type=MESH)

    loc = pltpu.make_async_copy(src(me), o_ref.at[me], loc_sem)
    sends = [remote(j, src(2 * px + py), me) for j, (px, py) in enumerate(peers)]
    arrivals = [remote(j, src(me), 2 * px + py) for j, (px, py) in enumerate(peers)]

    def start():
        loc.start()
        for cp in sends:
            cp.start()

    def finish():
        for cp in arrivals:
            cp.wait_recv()
        for cp in sends:
            cp.wait_send()
        loc.wait()

    return start, finish


def _gather8(x, name):
    collected = _chip_exchange(x, name + "_chips", True)
    return jnp.swapaxes(_by_core(collected, _pair_swap(collected, name + "_pair")), 0, 1)


def _tile(n, want):
    if n <= want:
        return n
    t = (want // LANES) * LANES
    while t >= LANES:
        if n % t == 0:
            return t
        t -= LANES
    raise ValueError(f"no lane-aligned tile for {n}")


_DIMS = {"nn": (((1,), (0,)), ((), ())), "nt": (((1,), (1,)), ((), ())), "tn": (((0,), (0,)), ((), ()))}


MM_FULL_K = 4096
MM_SLAB_K = 2048
MM_TILES = ((1024, 1024), (512, 1024), (1024, 512), (512, 512), (256, 512), (256, 256))
MM_VMEM_BUDGET = 36 * 1024 * 1024


def _mm(a, b, mode, name, *, a_pro=None, epi=None, extras=(), out_dtype=F32, tm_max=1024, tn_max=1024, tk=None, a_cols=None):
    c0, csize = a_cols if a_cols is not None else (0, a.shape[1])
    if mode == "tn":
        K, M = a.shape[0], csize
    else:
        M, K = a.shape[0], csize
    N = b.shape[0] if mode == "nt" else b.shape[1]
    assert (b.shape[1] if mode == "nt" else b.shape[0]) == K, (a.shape, b.shape, mode)
    if tk is None:
        tk = K if (mode != "tn" and K <= MM_FULL_K) else MM_SLAB_K
    tk = _tile(K, tk)
    nk = K // tk
    n_mn = sum(1 for _, kind in extras if kind == "mn")
    for tm_want, tn_want in MM_TILES:
        tm, tn = _tile(M, min(tm_want, tm_max)), _tile(N, min(tn_want, tn_max))
        need = 2 * (tm * tk * a.dtype.itemsize + tk * tn * b.dtype.itemsize + tm * tn * 4 * (1 + n_mn)) + tm * tn * 4 * (nk > 1)
        if need <= MM_VMEM_BUDGET:
            break
    if mode == "tn":
        assert c0 % tm == 0
        a_spec = pl.BlockSpec((tk, tm), lambda i, j, k: (k, i + c0 // tm))
    else:
        assert c0 % tk == 0
        a_spec = pl.BlockSpec((tm, tk), lambda i, j, k: (i, k + c0 // tk))
    b_spec = pl.BlockSpec((tn, tk), lambda i, j, k: (j, k)) if mode == "nt" else pl.BlockSpec((tk, tn), lambda i, j, k: (k, j))
    ex_specs = []
    for arr, kind in extras:
        if kind == "mn":
            assert arr.shape == (M, N)
            ex_specs.append(pl.BlockSpec((tm, tn), lambda i, j, k: (i, j)))
        else:
            assert arr.shape == (1, N)
            ex_specs.append(pl.BlockSpec((1, tn), lambda i, j, k: (0, j)))
    n_ex = len(extras)

    def body(*refs):
        a_ref, b_ref = refs[:2]
        ex_refs = refs[2:2 + n_ex]
        o_ref = refs[2 + n_ex]
        acc_ref = refs[3 + n_ex] if nk > 1 else None
        k = pl.program_id(2)
        av = a_ref[...]
        if a_pro is not None:
            av = a_pro(av)
        part = lax.dot_general(av.astype(BF16), b_ref[...].astype(BF16), _DIMS[mode], preferred_element_type=F32)
        if nk == 1:
            if epi is not None:
                part = epi(part, *[r[...] for r in ex_refs])
            o_ref[...] = part.astype(o_ref.dtype)
            return

        @pl.when(k == 0)
        def _():
            acc_ref[...] = part

        @pl.when(k > 0)
        def _():
            acc_ref[...] += part

        @pl.when(k == nk - 1)
        def _():
            acc = acc_ref[...]
            if epi is not None:
                acc = epi(acc, *[r[...] for r in ex_refs])
            o_ref[...] = acc.astype(o_ref.dtype)

    return pl.pallas_call(
        body, name=name, grid=(M // tm, N // tn, nk),
        in_specs=[a_spec, b_spec] + ex_specs,
        out_specs=pl.BlockSpec((tm, tn), lambda i, j, k: (i, j)),
        out_shape=jax.ShapeDtypeStruct((M, N), out_dtype),
        scratch_shapes=[pltpu.VMEM((tm, tn), F32)] if nk > 1 else [],
        compiler_params=_cparams("parallel", "parallel", "arbitrary"))(a, b, *[e[0] for e in extras])


ROWS = 256


def _row_spec(w, ts=ROWS):
    return pl.BlockSpec((ts, w), lambda i: (i, 0))


def _vec_spec(w):
    return pl.BlockSpec((1, w), lambda i: (0, 0))


def _res_rms(x, sc, sh, name, y=None, g=None):
    S, D = x.shape
    has_res = y is not None

    def body(*refs):
        if has_res:
            x_ref, y_ref, g_ref, sc_ref, sh_ref, xo_ref, h_ref = refs
            xv = x_ref[...] + g_ref[...] * y_ref[...]
            xo_ref[...] = xv
        else:
            x_ref, sc_ref, sh_ref, h_ref = refs
            xv = x_ref[...]
        r = lax.rsqrt(jnp.mean(xv * xv, axis=-1, keepdims=True) + EPS)
        h_ref[...] = (xv * r * (1.0 + sc_ref[...]) + sh_ref[...]).astype(BF16)

    row, vec = _row_spec(D), _vec_spec(D)
    if has_res:
        return pl.pallas_call(body, name=name, grid=(S // ROWS,), in_specs=[row, row, vec, vec, vec], out_specs=[row, row],
                              out_shape=[jax.ShapeDtypeStruct((S, D), F32), jax.ShapeDtypeStruct((S, D), BF16)],
                              compiler_params=_cparams("parallel"))(x, y, g, sc, sh)
    h = pl.pallas_call(body, name=name, grid=(S // ROWS,), in_specs=[row, vec, vec], out_specs=row,
                       out_shape=jax.ShapeDtypeStruct((S, D), BF16), compiler_params=_cparams("parallel"))(x, sc, sh)
    return x, h


def _res_rms_bwd(x, dh, sc, dres, name, y=None, g=None):
    S, D = x.shape
    has_res = y is not None

    def body(*refs):
        if has_res:
            x_ref, dh_ref, sc_ref, dres_ref, y_ref, g_ref, dx_ref, dy_ref, dg_ref, dsc_ref, dsh_ref = refs
        else:
            x_ref, dh_ref, sc_ref, dres_ref, dx_ref, dsc_ref, dsh_ref = refs
        first = pl.program_id(0) == 0
        xv = x_ref[...]
        dh = dh_ref[...]
        r = lax.rsqrt(jnp.mean(xv * xv, axis=-1, keepdims=True) + EPS)
        xn = xv * r
        dxn = dh * (1.0 + sc_ref[...])
        dx = dres_ref[...] + r * (dxn - xn * jnp.mean(dxn * xn, axis=-1, keepdims=True))
        dx_ref[...] = dx
        parts = [(dsc_ref, jnp.sum(dh * xn, axis=0, keepdims=True)), (dsh_ref, jnp.sum(dh, axis=0, keepdims=True))]
        if has_res:
            dy_ref[...] = (dx * g_ref[...]).astype(BF16)
            parts.append((dg_ref, jnp.sum(dx * y_ref[...], axis=0, keepdims=True)))
        for ref, val in parts:
            @pl.when(first)
            def _(ref=ref, val=val):
                ref[...] = val

            @pl.when(jnp.logical_not(first))
            def _(ref=ref, val=val):
                ref[...] += val

    row, vec = _row_spec(D), _vec_spec(D)
    full = jax.ShapeDtypeStruct((S, D), F32)
    v = jax.ShapeDtypeStruct((1, D), F32)
    if has_res:
        return pl.pallas_call(body, name=name, grid=(S // ROWS,), in_specs=[row, row, vec, row, row, vec],
                              out_specs=[row, row, vec, vec, vec], out_shape=[full, jax.ShapeDtypeStruct((S, D), BF16), v, v, v],
                              compiler_params=_cparams("arbitrary"))(x, dh, sc, dres, y, g)
    return pl.pallas_call(body, name=name, grid=(S // ROWS,), in_specs=[row, row, vec, row],
                          out_specs=[row, vec, vec], out_shape=[full, v, v],
                          compiler_params=_cparams("arbitrary"))(x, dh, sc, dres)


def _res_loss(x, m, g, target, name):
    S, D = x.shape

    def body(x_ref, m_ref, g_ref, t_ref, loss_ref, dx_ref, dm_ref, dg_ref):
        first = pl.program_id(0) == 0
        mv = m_ref[...]
        err = x_ref[...] + g_ref[...] * mv - t_ref[...]
        dx = err * (1.0 / D)
        dx_ref[...] = dx
        dm_ref[...] = (dx * g_ref[...]).astype(BF16)
        part = 0.5 * jnp.sum(jnp.mean(err * err, axis=-1, keepdims=True), axis=0, keepdims=True)
        dg = jnp.sum(dx * mv, axis=0, keepdims=True)

        @pl.when(first)
        def _():
            loss_ref[...] = jnp.broadcast_to(part, loss_ref.shape)
            dg_ref[...] = dg

        @pl.when(jnp.logical_not(first))
        def _():
            loss_ref[...] += jnp.broadcast_to(part, loss_ref.shape)
            dg_ref[...] += dg

    row, vec = _row_spec(D), _vec_spec(D)
    full = jax.ShapeDtypeStruct((S, D), F32)
    return pl.pallas_call(body, name=name, grid=(S // ROWS,), in_specs=[row, row, vec, row],
                          out_specs=[pl.BlockSpec((8, LANES), lambda i: (0, 0)), row, row, vec],
                          out_shape=[jax.ShapeDtypeStruct((8, LANES), F32), full, jax.ShapeDtypeStruct((S, D), BF16), jax.ShapeDtypeStruct((1, D), F32)],
                          compiler_params=_cparams("arbitrary"))(x, m, g, target)


def _adamw(w, g, m, v, name):
    R, C = w.shape
    tr = R if R <= 256 else 256
    assert R % tr == 0

    def body(w_ref, g_ref, m_ref, v_ref, d_ref, nm_ref, nv_ref):
        gv = g_ref[...]
        nm = ADAM_B1 * m_ref[...] + (1.0 - ADAM_B1) * gv
        nv = ADAM_B2 * v_ref[...] + (1.0 - ADAM_B2) * jnp.square(gv)
        m_hat = nm / (1.0 - ADAM_B1 ** ADAM_STEP)
        v_hat = nv / (1.0 - ADAM_B2 ** ADAM_STEP)
        d_ref[...] = -ADAM_LR * (m_hat / (jnp.sqrt(v_hat) + ADAM_EPS) + ADAM_WD * w_ref[...])
        nm_ref[...] = nm
        nv_ref[...] = nv

    spec = pl.BlockSpec((tr, C), lambda i: (i, 0))
    out = jax.ShapeDtypeStruct((R, C), F32)
    return pl.pallas_call(body, name=name, grid=(R // tr,), in_specs=[spec] * 4, out_specs=[spec] * 3,
                          out_shape=[out, out, out], compiler_params=_cparams("parallel"))(w, g, m, v)


def _sum_slots(x, name):
    n, R, C = x.shape
    tr = R if R <= 256 else 256
    assert R % tr == 0

    def body(x_ref, o_ref):
        acc = x_ref[0].astype(F32)
        for j in range(1, n):
            acc = acc + x_ref[j].astype(F32)
        o_ref[...] = acc

    return pl.pallas_call(body, name=name, grid=(R // tr,), in_specs=[pl.BlockSpec((n, tr, C), lambda i: (0, i, 0))],
                          out_specs=pl.BlockSpec((tr, C), lambda i: (i, 0)), out_shape=jax.ShapeDtypeStruct((R, C), F32),
                          compiler_params=_cparams("parallel"))(x)


def _ew(fn, name, tiled, consts=(), outs=(), sums=(), ts=ROWS):
    tiled = [t if isinstance(t, tuple) else (t, t.shape[1], 0) for t in tiled]
    S = tiled[0][0].shape[0]
    n_t, n_c, n_o, n_s = len(tiled), len(consts), len(outs), len(sums)

    def body(*refs):
        ins = [r[...] for r in refs[:n_t + n_c]]
        res = fn(*ins)
        res = res if isinstance(res, (tuple, list)) else (res,)
        assert len(res) == n_o + n_s
        o_refs = refs[n_t + n_c:]
        for r, val in zip(o_refs[:n_o], res[:n_o]):
            r[...] = val.astype(r.dtype)
        first = pl.program_id(0) == 0
        for r, val in zip(o_refs[n_o:], res[n_o:]):
            @pl.when(first)
            def _(r=r, val=val):
                r[...] = val

            @pl.when(jnp.logical_not(first))
            def _(r=r, val=val):
                r[...] += val

    in_specs = [pl.BlockSpec((ts, w), lambda i, cb=cb: (i, cb)) for _, w, cb in tiled]
    in_specs += [pl.BlockSpec(c.shape, lambda i, nd=c.ndim: (0,) * nd) for c in consts]
    out_specs = [_row_spec(w, ts) for w, _ in outs] + [_vec_spec(w) for w in sums]
    out_shape = [jax.ShapeDtypeStruct((S, w), dt) for w, dt in outs] + [jax.ShapeDtypeStruct((1, w), F32) for w in sums]
    res = pl.pallas_call(body, name=name, grid=(S // ts,), in_specs=in_specs, out_specs=out_specs, out_shape=out_shape,
                         compiler_params=_cparams("arbitrary" if sums else "parallel"))(*[t[0] for t in tiled], *consts)
    return res


_GELU_C = math.sqrt(2.0 / math.pi)


def _gelu(x):
    return 0.5 * x * (1.0 + jnp.tanh(_GELU_C * (x + 0.044715 * x * x * x)))


def _dgelu(x):
    t = jnp.tanh(_GELU_C * (x + 0.044715 * x * x * x))
    return 0.5 * (1.0 + t) + 0.5 * x * (1.0 - t * t) * _GELU_C * (1.0 + 3.0 * 0.044715 * x * x)


def _sigmoid(x):
    return 1.0 / (1.0 + jnp.exp(-x))


def _log_sigmoid(x):
    return jnp.minimum(x, 0.0) - jnp.log(1.0 + jnp.exp(-jnp.abs(x)))


SCAN_T = 128
SCAN_TB = 512


def _cmul(ar, ai, br, bi):
    return ar * br - ai * bi, ar * bi + ai * br


def _s5_discretise(lam_re, lam_im, log_dt, b_re, b_im):
    dt = jnp.exp(log_dt)[:, None]
    mag = jnp.exp(lam_re * dt)
    ang = lam_im * dt
    abar_re = mag * jnp.cos(ang)
    abar_im = mag * jnp.sin(ang)
    den = lam_re * lam_re + lam_im * lam_im
    coef_re = ((abar_re - 1.0) * lam_re + abar_im * lam_im) / den
    coef_im = (abar_im * lam_re - (abar_re - 1.0) * lam_im) / den
    bbar_re = coef_re[..., None] * b_re - coef_im[..., None] * b_im
    bbar_im = coef_re[..., None] * b_im + coef_im[..., None] * b_re
    return abar_re, abar_im, bbar_re, bbar_im


def _planes(re, im):
    lead = re.shape[:-1]
    return jnp.stack([re.reshape(lead + (-1, LANES)), im.reshape(lead + (-1, LANES))], axis=-2).reshape(lead + (-1,))


def _unplanes(x):
    lead = x.shape[:-1]
    x4 = x.reshape(lead + (-1, 2, LANES))
    return x4[..., 0, :].reshape(lead + (-1,)), x4[..., 1, :].reshape(lead + (-1,))


def _s5_scan_tables(a_re, a_im, reverse):
    pr, pi = [a_re], [a_im]
    for _ in range(7):
        r, i = _cmul(pr[-1], pi[-1], pr[-1], pi[-1])
        pr.append(r)
        pi.append(i)
    apow = _planes(jnp.stack(pr), jnp.stack(pi))
    n = np.arange(1, SCAN_T + 1)
    if reverse:
        n = n[::-1]
    tr = jnp.ones((SCAN_T, a_re.shape[0]), F32)
    ti = jnp.zeros((SCAN_T, a_re.shape[0]), F32)
    for k in range(8):
        bit = jnp.asarray(((n >> k) & 1).astype(np.float32))[:, None]
        mr = bit * pr[k][None, :] + (1.0 - bit)
        mi = bit * pi[k][None, :]
        tr, ti = _cmul(tr, ti, mr, mi)
    return apow, _planes(tr, ti)


def _s5_scan(bu, apow, ptab, name, reverse, x_fwd=None):
    S, N2 = bu.shape
    T, W = SCAN_T, 2 * LANES
    tb = min(SCAN_TB, S)
    nt, nsub = S // tb, tb // T
    order = list(range(nsub - 1, -1, -1) if reverse else range(nsub))
    with_da = x_fwd is not None

    def tblk(t):
        return (nt - 1 - t) if reverse else t

    def shifted(v, k, rowi):
        s = 1 << k
        if reverse:
            return jnp.where(rowi < T - s, pltpu.roll(v, T - s, 0), 0.0)
        return jnp.where(rowi >= s, pltpu.roll(v, s, 0), 0.0)

    def body(*refs):
        if with_da:
            bu_ref, ap_ref, pt_ref, xf_ref, xp_ref, x_ref, da_ref, carry_ref = refs
        else:
            bu_ref, ap_ref, pt_ref, x_ref, carry_ref = refs
        t = pl.program_id(1)

        @pl.when(t == 0)
        def _():
            carry_ref[...] = jnp.zeros_like(carry_ref)
            if with_da:
                da_ref[...] = jnp.zeros_like(da_ref)

        rowi = lax.broadcasted_iota(jnp.int32, (T, LANES), 0)
        pr, pi = pt_ref[:, :LANES], pt_ref[:, LANES:]
        cr, ci = carry_ref[0:1, :LANES], carry_ref[0:1, LANES:]
        for sb in order:
            rows = pl.ds(sb * T, T)
            xr, xi = bu_ref[rows, :LANES], bu_ref[rows, LANES:]
            for k in range(7):
                ar, ai = ap_ref[k:k + 1, :LANES], ap_ref[k:k + 1, LANES:]
                s = 1 << k
                if s < 8:
                    rr, ri = shifted(xr, k, rowi), shifted(xi, k, rowi)
                    xr, xi = xr + ar * rr - ai * ri, xi + ar * ri + ai * rr
                elif reverse:
                    nr, ni = xr[s:], xi[s:]
                    xr = jnp.concatenate([xr[:T - s] + ar * nr - ai * ni, xr[T - s:]], axis=0)
                    xi = jnp.concatenate([xi[:T - s] + ar * ni + ai * nr, xi[T - s:]], axis=0)
                else:
                    nr, ni = xr[:T - s], xi[:T - s]
                    xr = jnp.concatenate([xr[:s], xr[s:] + ar * nr - ai * ni], axis=0)
                    xi = jnp.concatenate([xi[:s], xi[s:] + ar * ni + ai * nr], axis=0)
            xr, xi = xr + pr * cr - pi * ci, xi + pr * ci + pi * cr
            x_ref[rows, :LANES] = xr
            x_ref[rows, LANES:] = xi
            edge = pl.ds(sb * T + (0 if reverse else T - 1), 1)
            cr, ci = x_ref[edge, :LANES], x_ref[edge, LANES:]
            if with_da:
                if sb > 0:
                    before = pl.ds(sb * T - 1, 1)
                    b_r, b_i = xf_ref[before, :LANES], xf_ref[before, LANES:]
                else:
                    keep = (tblk(t) > 0).astype(F32)
                    b_r, b_i = xp_ref[7:8, :LANES] * keep, xp_ref[7:8, LANES:] * keep
                fr, fi = xf_ref[rows, :LANES], xf_ref[rows, LANES:]
                qr = jnp.where(rowi >= 1, pltpu.roll(fr, 1, 0), b_r)
                qi = jnp.where(rowi >= 1, pltpu.roll(fi, 1, 0), b_i)
                gr, gi = xr * qr + xi * qi, xi * qr - xr * qi
                sr, si = gr[0:8], gi[0:8]
                for j in range(1, T // 8):
                    sr, si = sr + gr[8 * j:8 * j + 8], si + gi[8 * j:8 * j + 8]
                da_ref[:, :LANES] += sr
                da_ref[:, LANES:] += si
        carry_ref[0:1, :LANES] = cr
        carry_ref[0:1, LANES:] = ci

    blk = pl.BlockSpec((tb, W), lambda j, t: (tblk(t), j))
    in_specs = [blk, pl.BlockSpec((8, W), lambda j, t: (0, j)), pl.BlockSpec((T, W), lambda j, t: (0, j))]
    out_specs, out_shape = [blk], [jax.ShapeDtypeStruct((S, N2), F32)]
    args = [bu, apow, ptab]
    if with_da:
        in_specs += [blk, pl.BlockSpec((8, W), lambda j, t: (jnp.maximum(tblk(t) * (tb // 8) - 1, 0), j))]
        out_specs.append(pl.BlockSpec((8, W), lambda j, t: (0, j)))
        out_shape.append(jax.ShapeDtypeStruct((8, N2), F32))
        args += [x_fwd, x_fwd]
    res = pl.pallas_call(body, name=name, grid=(N2 // W, nt), in_specs=in_specs, out_specs=out_specs, out_shape=out_shape,
                         scratch_shapes=[pltpu.VMEM((8, W), F32)], compiler_params=_cparams("parallel", "arbitrary"))(*args)
    return res if with_da else res[0]


S5_BAND = 4


def _mm_band(a, b, name, *, b_t=False, outer=False, epi=None, extras=(), tm=512, tk=2048):
    S = a.shape[0]
    wa = a.shape[1] // S5_BAND
    if outer:
        wb = b.shape[1] // S5_BAND
        tk = _tile(S, tk)
        nk = S // tk

        def obody(a_ref, b_ref, o_ref, acc_ref):
            k = pl.program_id(1)
            part = lax.dot_general(a_ref[...].astype(BF16), b_ref[...].astype(BF16), TN_DIMS, preferred_element_type=F32)

            @pl.when(k == 0)
            def _():
                acc_ref[...] = part

            @pl.when(k > 0)
            def _():
                acc_ref[...] += part

            @pl.when(k == nk - 1)
            def _():
                o_ref[...] = acc_ref[...]

        return pl.pallas_call(
            obody, name=name, grid=(S5_BAND, nk),
            in_specs=[pl.BlockSpec((tk, wa), lambda c, k: (k, c)), pl.BlockSpec((tk, wb), lambda c, k: (k, c))],
            out_specs=pl.BlockSpec((wa, wb), lambda c, k: (c, 0)), out_shape=jax.ShapeDtypeStruct((a.shape[1], wb), F32),
            scratch_shapes=[pltpu.VMEM((wa, wb), F32)], compiler_params=_cparams("parallel", "arbitrary"))(a, b)

    wo = (b.shape[0] if b_t else b.shape[1]) // S5_BAND
    tm = _tile(S, tm)
    ex_specs = [pl.BlockSpec((tm, wo), lambda i, c: (i, c)) if kind == "mn" else pl.BlockSpec((1, wo), lambda i, c: (0, c))
                for _, kind in extras]

    def body(a_ref, b_ref, *refs):
        part = lax.dot_general(a_ref[...].astype(BF16), b_ref[...].astype(BF16), NT_DIMS if b_t else _DIMS["nn"], preferred_element_type=F32)
        if epi is not None:
            part = epi(part, *[r[...] for r in refs[:-1]])
        refs[-1][...] = part

    b_spec = pl.BlockSpec((wo, wa) if b_t else (wa, wo), lambda i, c: (c, c))
    return pl.pallas_call(
        body, name=name, grid=(S // tm, S5_BAND), in_specs=[pl.BlockSpec((tm, wa), lambda i, c: (i, c)), b_spec] + ex_specs,
        out_specs=pl.BlockSpec((tm, wo), lambda i, c: (i, c)), out_shape=jax.ShapeDtypeStruct((S, S5_BAND * wo), F32),
        compiler_params=_cparams("parallel", "parallel"))(a, b, *[e[0] for e in extras])


def _band_to_full(blocks, cols):
    wa, wb = blocks.shape[0] // S5_BAND, blocks.shape[1]
    return jnp.concatenate([jnp.pad(blocks[k * wa:(k + 1) * wa], ((0, 0), (k * wb, cols - (k + 1) * wb))) for k in range(S5_BAND)], axis=0)


def _block_diag(t):
    G, a, b = t.shape
    return (t[:, :, None, :] * jnp.eye(G, dtype=t.dtype)[:, None, :, None]).reshape(G * a, G * b)


def _block_diag_take(m, G):
    a, b = m.shape[0] // G, m.shape[1] // G
    m4 = m.reshape(G, a, G, b)
    return jnp.sum(m4 * jnp.eye(G, dtype=m.dtype)[:, None, :, None], axis=2)


def _s5_block_fwd(u, w, pfx):
    a_re, a_im, bb_re, bb_im = _s5_discretise(w["lam_re"], w["lam_im"], w["log_dt"], w["b_re"], w["b_im"])
    bcat = _planes(_block_diag(bb_re).T, _block_diag(bb_im).T).astype(BF16)
    ccat = _planes(_block_diag(jnp.swapaxes(w["c_re"], 1, 2)).T, -_block_diag(jnp.swapaxes(w["c_im"], 1, 2)).T).T.astype(BF16)
    af_re, af_im = a_re.reshape(-1), a_im.reshape(-1)
    apow, ptab = _s5_scan_tables(af_re, af_im, False)
    bu = _mm_band(u, bcat, pfx + "_bu")
    x = _s5_scan(bu, apow, ptab, pfx + "_scan", False)
    d_row = w["d"].reshape(1, MIX_HALF)
    ys = _mm_band(x, ccat, pfx + "_y", epi=lambda acc, ut, dr: acc + dr * ut, extras=[(u, "mn"), (d_row, "n")])
    z = _mm(ys, w["w_glu"], "nn", pfx + "_glu", a_pro=_gelu, epi=lambda acc, b: acc + b, extras=[(w["b_glu"].reshape(1, -1), "n")])
    y2, = _ew(lambda ysv, zv: _gelu(ysv) * _sigmoid(zv), pfx + "_gate", [ys, z], outs=[(MIX_HALF, F32)])
    return y2, dict(u=u, x=x, ys=ys, z=z, bcat=bcat, ccat=ccat, a=(af_re, af_im), d_row=d_row)


def _s5_block_bwd(dy2, w, res, pfx):
    u, x, ys, z, bcat, ccat = res["u"], res["x"], res["ys"], res["z"], res["bcat"], res["ccat"]

    def gate_bwd(dy, ysv, zv):
        sg = _sigmoid(zv)
        dz = dy * _gelu(ysv) * sg * (1.0 - sg)
        return dz, jnp.sum(dz, axis=0, keepdims=True)

    dz, db_glu = _ew(gate_bwd, pfx + "_gate_bwd", [dy2, ys, z], outs=[(MIX_HALF, F32)], sums=[MIX_HALF])
    dw_glu = _mm(ys, dz, "tn", pfx + "_dwglu", a_pro=_gelu)
    dys = _mm(dz, w["w_glu"], "nt", pfx + "_dys", epi=lambda acc, dy, zv, ysv: (acc + dy * _sigmoid(zv)) * _dgelu(ysv),
              extras=[(dy2, "mn"), (z, "mn"), (ys, "mn")])
    dd, = _ew(lambda a, b: jnp.sum(a * b, axis=0, keepdims=True), pfx + "_dd", [dys, u], sums=[MIX_HALF])
    dccat = _band_to_full(_mm_band(x, dys, pfx + "_dc", outer=True), MIX_HALF)
    dx = _mm_band(dys, ccat, pfx + "_dx", b_t=True)
    af_re, af_im = res["a"]
    apow, ptab = _s5_scan_tables(af_re, -af_im, True)
    lam, da8 = _s5_scan(dx, apow, ptab, pfx + "_scan_bwd", True, x_fwd=x)
    dbcat = _band_to_full(_mm_band(u, lam, pfx + "_db", outer=True), 2 * S5_N)
    du = _mm_band(lam, bcat, pfx + "_du", b_t=True, epi=lambda acc, dyv, dr: acc + dyv * dr, extras=[(dys, "mn"), (res["d_row"], "n")])
    G = S5_GROUPS
    d_abar_re, d_abar_im = (t.reshape(G, S5_STATE) for t in _unplanes(jnp.sum(da8, axis=0)))
    d_bb_re, d_bb_im = (_block_diag_take(t.T, G) for t in _unplanes(dbcat))
    _, vjp = jax.vjp(_s5_discretise, w["lam_re"], w["lam_im"], w["log_dt"], w["b_re"], w["b_im"])
    g_lam_re, g_lam_im, g_log_dt, g_b_re, g_b_im = vjp((d_abar_re, d_abar_im, d_bb_re, d_bb_im))
    dc_re, dc_im = _unplanes(dccat.T)
    g_c_re = jnp.swapaxes(_block_diag_take(dc_re.T, G), 1, 2)
    g_c_im = -jnp.swapaxes(_block_diag_take(dc_im.T, G), 1, 2)
    grads = dict(lam_re=g_lam_re, lam_im=g_lam_im, log_dt=g_log_dt, b_re=g_b_re, b_im=g_b_im, c_re=g_c_re, c_im=g_c_im,
                 d=dd.reshape(G, S5_GROUP_WIDTH), w_glu=dw_glu, b_glu=db_glu.reshape(-1))
    return du, grads


SGU_TS = 512
N_PAIRS = MIX_HALF // LANES


def _half_masks(rows):
    lane = lax.broadcasted_iota(jnp.int32, (rows, LANES), 1)
    left = (lane < HEAD_DIM).astype(F32)
    return left, 1.0 - left


def _sgu_norm(zv, gain, bias):
    v = _gelu(zv)
    mu = jnp.mean(v, axis=-1, keepdims=True)
    vc = v - mu
    rstd = lax.rsqrt(jnp.mean(vc * vc, axis=-1, keepdims=True) + EPS)
    vhat = vc * rstd
    return vhat, rstd, vhat * gain + bias


def _sgu_tables(w_s, b_s):
    mask = jnp.tril(jnp.ones((SGU_CHUNK, SGU_CHUNK), dtype=bool))
    wm = jnp.where(mask[None], w_s, 0.0).astype(BF16)
    bias_tab = jnp.repeat(b_s.T, MIX_HALF // SGU_GROUPS, axis=1)
    return wm, bias_tab


def _sgu_fwd(proj, ln_gain, ln_bias, wm, bias_tab, name):
    S = proj.shape[0]
    nch = SGU_TS // SGU_CHUNK

    def body(zu_ref, zv_ref, g_ref, b_ref, w_ref, bt_ref, o_ref):
        left, right = _half_masks(SGU_CHUNK)
        _, _, vn = _sgu_norm(zv_ref[...], g_ref[...], b_ref[...])
        for ch in range(nch):
            rows = pl.ds(ch * SGU_CHUNK, SGU_CHUNK)
            for p in range(N_PAIRS):
                cols = pl.ds(p * LANES, LANES)
                vp = vn[ch * SGU_CHUNK:(ch + 1) * SGU_CHUNK, p * LANES:(p + 1) * LANES]
                mixed = (jnp.dot(w_ref[2 * p], (vp * left).astype(BF16), preferred_element_type=F32)
                         + jnp.dot(w_ref[2 * p + 1], (vp * right).astype(BF16), preferred_element_type=F32) + bt_ref[:, cols])
                o_ref[rows, cols] = _gelu(zu_ref[rows, cols]) * mixed

    vec = _vec_spec(MIX_HALF)
    return pl.pallas_call(
        body, name=name, grid=(S // SGU_TS,),
        in_specs=[pl.BlockSpec((SGU_TS, MIX_HALF), lambda i: (i, 1)), pl.BlockSpec((SGU_TS, MIX_HALF), lambda i: (i, 2)), vec, vec,
                  pl.BlockSpec((SGU_GROUPS, SGU_CHUNK, SGU_CHUNK), lambda i: (0, 0, 0)), pl.BlockSpec((SGU_CHUNK, MIX_HALF), lambda i: (0, 0))],
        out_specs=_row_spec(MIX_HALF, SGU_TS), out_shape=jax.ShapeDtypeStruct((S, MIX_HALF), F32),
        compiler_params=_cparams("parallel"))(proj, proj, ln_gain, ln_bias, wm, bias_tab)


def _sgu_bwd(dout, proj, ln_gain, ln_bias, wm, bias_tab, name):
    S = proj.shape[0]
    nch = SGU_TS // SGU_CHUNK
    nt_dims = (((1,), (1,)), ((), ()))
    tn_dims = (((0,), (0,)), ((), ()))

    def body(do_ref, zu_ref, zv_ref, g_ref, b_ref, w_ref, bt_ref, dzu_ref, dzv_ref, dw_ref, dbt_ref, dg_ref, db_ref, dvn_ref):
        first = pl.program_id(0) == 0

        @pl.when(first)
        def _():
            dw_ref[...] = jnp.zeros_like(dw_ref)
            dbt_ref[...] = jnp.zeros_like(dbt_ref)
            dg_ref[...] = jnp.zeros_like(dg_ref)
            db_ref[...] = jnp.zeros_like(db_ref)

        left, right = _half_masks(SGU_CHUNK)
        zv = zv_ref[...]
        vhat, rstd, vn = _sgu_norm(zv, g_ref[...], b_ref[...])
        for ch in range(nch):
            rows = pl.ds(ch * SGU_CHUNK, SGU_CHUNK)
            for p in range(N_PAIRS):
                cols = pl.ds(p * LANES, LANES)
                vp = vn[ch * SGU_CHUNK:(ch + 1) * SGU_CHUNK, p * LANES:(p + 1) * LANES]
                vl, vr = (vp * left).astype(BF16), (vp * right).astype(BF16)
                mixed = (jnp.dot(w_ref[2 * p], vl, preferred_element_type=F32)
                         + jnp.dot(w_ref[2 * p + 1], vr, preferred_element_type=F32) + bt_ref[:, cols])
                zu = zu_ref[rows, cols]
                do = do_ref[rows, cols]
                dzu_ref[rows, cols] = do * mixed * _dgelu(zu)
                dmix = do * _gelu(zu)
                dbt_ref[:, cols] += dmix
                dl, dr = (dmix * left).astype(BF16), (dmix * right).astype(BF16)
                dw_ref[2 * p] += lax.dot_general(dl, vl, nt_dims, preferred_element_type=F32)
                dw_ref[2 * p + 1] += lax.dot_general(dr, vr, nt_dims, preferred_element_type=F32)
                dvn_ref[rows, cols] = (lax.dot_general(w_ref[2 * p], dl, tn_dims, preferred_element_type=F32)
                                       + lax.dot_general(w_ref[2 * p + 1], dr, tn_dims, preferred_element_type=F32))
        dvn = dvn_ref[...]
        dg_ref[...] += jnp.sum(dvn * vhat, axis=0, keepdims=True)
        db_ref[...] += jnp.sum(dvn, axis=0, keepdims=True)
        dvh = dvn * g_ref[...]
        dv = rstd * (dvh - jnp.mean(dvh, axis=-1, keepdims=True) - vhat * jnp.mean(dvh * vhat, axis=-1, keepdims=True))
        dzv_ref[...] = dv * _dgelu(zv)

    vec = _vec_spec(MIX_HALF)
    row = _row_spec(MIX_HALF, SGU_TS)
    wspec = pl.BlockSpec((SGU_GROUPS, SGU_CHUNK, SGU_CHUNK), lambda i: (0, 0, 0))
    tspec = pl.BlockSpec((SGU_CHUNK, MIX_HALF), lambda i: (0, 0))
    full = jax.ShapeDtypeStruct((S, MIX_HALF), F32)
    v = jax.ShapeDtypeStruct((1, MIX_HALF), F32)
    return pl.pallas_call(
        body, name=name, grid=(S // SGU_TS,),
        in_specs=[row, pl.BlockSpec((SGU_TS, MIX_HALF), lambda i: (i, 1)), pl.BlockSpec((SGU_TS, MIX_HALF), lambda i: (i, 2)), vec, vec,
                  wspec, tspec],
        out_specs=[row, row, wspec, tspec, vec, vec],
        out_shape=[full, full, jax.ShapeDtypeStruct((SGU_GROUPS, SGU_CHUNK, SGU_CHUNK), F32),
                   jax.ShapeDtypeStruct((SGU_CHUNK, MIX_HALF), F32), v, v],
        scratch_shapes=[pltpu.VMEM((SGU_TS, MIX_HALF), F32)],
        compiler_params=_cparams("arbitrary"))(dout, proj, proj, ln_gain, ln_bias, wm, bias_tab)


def _sgu_grads(dw, dbias_tab):
    mask = jnp.tril(jnp.ones((SGU_CHUNK, SGU_CHUNK), dtype=bool))
    g_w = jnp.where(mask[None], dw, 0.0)
    g_b = dbias_tab.reshape(SGU_CHUNK, SGU_GROUPS, MIX_HALF // SGU_GROUPS).sum(axis=-1).T
    return g_w, g_b


def _head_avg_matrix(w):
    idx = np.arange(w) // HEAD_DIM
    return jnp.asarray((idx[:, None] == idx[None, :]).astype(np.float32) / HEAD_DIM, dtype=BF16)


def _head_mean(t, bavg):
    hi = t.astype(BF16)
    lo = (t - hi.astype(F32)).astype(BF16)
    return jnp.dot(hi, bavg, preferred_element_type=F32) + jnp.dot(lo, bavg, preferred_element_type=F32)


def _head_rms(t, bavg):
    r = lax.rsqrt(_head_mean(t * t, bavg) + EPS)
    return t * r, r


def _head_rms_bwd(dn, n, r, bavg):
    return r * (dn - n * _head_mean(dn * n, bavg))


GLA_TS = 512
C = GLA_CHUNK
NT_DIMS = (((1,), (1,)), ((), ()))
TN_DIMS = (((0,), (0,)), ((), ()))
HI = lax.Precision.HIGHEST


def _bdot(a, b, dims=(((1,), (0,)), ((), ()))):
    return lax.dot_general(a.astype(BF16), b.astype(BF16), dims, preferred_element_type=F32)


def _gla_chunk_terms(q, k, z):
    row = lax.broadcasted_iota(jnp.int32, (C, C), 0)
    col = lax.broadcasted_iota(jnp.int32, (C, C), 1)
    lc = _log_sigmoid(z) * (1.0 / GLA_TAU)
    b = lax.dot_general((row >= col).astype(F32), lc, (((1,), (0,)), ((), ())), precision=HI, preferred_element_type=F32)
    b_last = jnp.sum(lc, axis=0, keepdims=True)
    b_mid = b[C // 2:C // 2 + 1, :]
    scale = HEAD_DIM ** -0.5
    e_b, e_q, e_k, e_l = jnp.exp(b), jnp.exp(b - b_mid), jnp.exp(b_mid - b), jnp.exp(b_last - b)
    qs = q * (scale * e_b)
    qe = q * (scale * e_q)
    ke = k * e_k
    kl = k * e_l
    return dict(e_b=e_b, e_q=e_q, e_k=e_k, e_l=e_l, qs=qs, qe=qe, ke=ke, kl=kl, dec=jnp.exp(b_last), causal=row >= col, scale=scale)


def _pair(x, pp):
    return x[:, pp * LANES:(pp + 1) * LANES]


def _pair_block_diag():
    r = lax.broadcasted_iota(jnp.int32, (LANES, LANES), 0) // HEAD_DIM
    c = lax.broadcasted_iota(jnp.int32, (LANES, LANES), 1) // HEAD_DIM
    return (r == c).astype(F32)


def _gla_fwd(proj, z, name):
    S = proj.shape[0]
    nch = GLA_TS // C

    def body(q_ref, k_ref, v_ref, z_ref, o_ref, st_ref, state_ref):
        @pl.when(pl.program_id(0) == 0)
        def _():
            state_ref[...] = jnp.zeros_like(state_ref)

        left, right = _half_masks(C)
        bd = _pair_block_diag()
        pairs = range(N_PAIRS)
        for ch in range(nch):
            rows = pl.ds(ch * C, C)
            v = v_ref[rows, :]
            t = _gla_chunk_terms(q_ref[rows, :], k_ref[rows, :], z_ref[rows, :])
            sts = [state_ref[pp] for pp in pairs]
            for pp in pairs:
                st_ref[ch, pp] = sts[pp]
            os = [_bdot(_pair(t["qs"], pp), sts[pp], NT_DIMS) for pp in pairs]
            for m in (left, right):
                scores = [jnp.where(t["causal"], _bdot(_pair(t["qe"], pp) * m, _pair(t["ke"], pp), NT_DIMS), 0.0) for pp in pairs]
                os = [os[pp] + m * _bdot(scores[pp], _pair(v, pp)) for pp in pairs]
            o_ref[rows, :] = jnp.concatenate(os, axis=1)
            new = [sts[pp] * _pair(t["dec"], pp) + bd * _bdot(_pair(v, pp), _pair(t["kl"], pp), TN_DIMS) for pp in pairs]
            for pp in pairs:
                state_ref[pp] = new[pp]

    def col(cb):
        return pl.BlockSpec((GLA_TS, MIX_HALF), lambda i: (i, cb))

    return pl.pallas_call(
        body, name=name, grid=(S // GLA_TS,),
        in_specs=[col(0), col(1), col(2), col(0)],
        out_specs=[col(0), pl.BlockSpec((nch, N_PAIRS, LANES, LANES), lambda i: (i, 0, 0, 0))],
        out_shape=[jax.ShapeDtypeStruct((S, MIX_HALF), F32), jax.ShapeDtypeStruct((S // C, N_PAIRS, LANES, LANES), F32)],
        scratch_shapes=[pltpu.VMEM((N_PAIRS, LANES, LANES), F32)], compiler_params=_cparams("arbitrary"))(proj, proj, proj, z)


def _gla_bwd(do, proj, z, states, name):
    S = proj.shape[0]
    nch = GLA_TS // C
    nblk = S // GLA_TS

    def body(do_ref, q_ref, k_ref, v_ref, z_ref, st_ref, dq_ref, dk_ref, dv_ref, dlc_ref, dstate_ref):
        @pl.when(pl.program_id(0) == 0)
        def _():
            dstate_ref[...] = jnp.zeros_like(dstate_ref)

        left, right = _half_masks(C)
        bd = _pair_block_diag()
        rowi = lax.broadcasted_iota(jnp.int32, (C, LANES), 0)
        row = lax.broadcasted_iota(jnp.int32, (C, C), 0)
        colm = lax.broadcasted_iota(jnp.int32, (C, C), 1)
        pairs = range(N_PAIRS)
        rowi = lax.broadcasted_iota(jnp.int32, (C, MIX_HALF), 0)
        for ch in range(nch - 1, -1, -1):
            rows = pl.ds(ch * C, C)
            v, dov = v_ref[rows, :], do_ref[rows, :]
            t = _gla_chunk_terms(q_ref[rows, :], k_ref[rows, :], z_ref[rows, :])
            sts = [st_ref[ch, pp] for pp in pairs]
            nxt = [dstate_ref[pp] for pp in pairs]
            gs = [bd * nxt[pp] for pp in pairs]
            dqs = [_bdot(_pair(dov, pp), sts[pp]) for pp in pairs]
            dv = [_bdot(_pair(t["kl"], pp), gs[pp], NT_DIMS) for pp in pairs]
            dkl = [_bdot(_pair(v, pp), gs[pp]) for pp in pairs]
            dqe = [jnp.zeros((C, LANES), F32) for _ in pairs]
            dke = [jnp.zeros((C, LANES), F32) for _ in pairs]
            for m in (left, right):
                sc = [jnp.where(t["causal"], _bdot(_pair(t["qe"], pp) * m, _pair(t["ke"], pp), NT_DIMS), 0.0) for pp in pairs]
                dsc = [jnp.where(t["causal"], _bdot(_pair(dov, pp) * m, _pair(v, pp), NT_DIMS), 0.0) for pp in pairs]
                dv = [dv[pp] + m * _bdot(sc[pp], _pair(dov, pp), TN_DIMS) for pp in pairs]
                dqe = [dqe[pp] + m * _bdot(dsc[pp], _pair(t["ke"], pp)) for pp in pairs]
                dke = [dke[pp] + m * _bdot(dsc[pp], _pair(t["qe"], pp), TN_DIMS) for pp in pairs]
            for pp in pairs:
                dstate_ref[pp] = bd * (nxt[pp] * _pair(t["dec"], pp) + _bdot(_pair(dov, pp), _pair(t["qs"], pp), TN_DIMS))
            decay_sum = jnp.concatenate([jnp.sum(nxt[pp] * sts[pp], axis=0, keepdims=True) for pp in pairs], axis=1)
            dqs, dv, dkl, dqe, dke = (jnp.concatenate(parts, axis=1) for parts in (dqs, dv, dkl, dqe, dke))
            db_last = decay_sum * t["dec"] + jnp.sum(dkl * t["kl"], axis=0, keepdims=True)
            db = dqs * t["qs"] + dqe * t["qe"] - dke * t["ke"] - dkl * t["kl"]
            db = db + jnp.where(rowi == C - 1, db_last, 0.0)
            dq_ref[rows, :] = (dqs * t["e_b"] + dqe * t["e_q"]) * t["scale"]
            dk_ref[rows, :] = dke * t["e_k"] + dkl * t["e_l"]
            dv_ref[rows, :] = dv
            dlc_ref[rows, :] = lax.dot_general((colm >= row).astype(F32), db, (((1,), (0,)), ((), ())), precision=HI,
                                               preferred_element_type=F32)

    def col(cb):
        return pl.BlockSpec((GLA_TS, MIX_HALF), lambda i: (nblk - 1 - i, cb))

    full = jax.ShapeDtypeStruct((S, MIX_HALF), F32)
    return pl.pallas_call(
        body, name=name, grid=(nblk,),
        in_specs=[col(0), col(0), col(1), col(2), col(0), pl.BlockSpec((nch, N_PAIRS, LANES, LANES), lambda i: (nblk - 1 - i, 0, 0, 0))],
        out_specs=[col(0)] * 4, out_shape=[full, full, full, full],
        scratch_shapes=[pltpu.VMEM((N_PAIRS, LANES, LANES), F32)], compiler_params=_cparams("arbitrary"))(do, proj, proj, proj, z, states)


def _gla_block_fwd(proj, w_lr_pad, b_lr, gain, bavg, pfx):
    z = _mm(proj, w_lr_pad, "nn", pfx + "_z", a_cols=(7 * MIX_HALF, MIX_HALF), epi=lambda acc, b: acc + b, extras=[(b_lr, "n")])
    o, states = _gla_fwd(proj, z, pfx + "_core")

    def out(ov, gg, ba, gn):
        n, _ = _head_rms(ov, ba)
        return n * gn * (gg * _sigmoid(gg))

    og, = _ew(out, pfx + "_out", [o, (proj, MIX_HALF, 3)], consts=[bavg, gain], outs=[(MIX_HALF, F32)])
    return og, dict(z=z, o=o, states=states)


def _gla_block_bwd(dog, proj, w_lr_pad, gain, bavg, res, pfx):
    z, o, states = res["z"], res["o"], res["states"]

    def out_bwd(dy, ov, gg, ba, gn):
        n, r = _head_rms(ov, ba)
        sg = _sigmoid(gg)
        silu = gg * sg
        dn = dy * gn * silu
        do = _head_rms_bwd(dn, n, r, ba)
        dgg = dy * n * gn * (sg * (1.0 + gg * (1.0 - sg)))
        return do, dgg, jnp.sum(dy * n * silu, axis=0, keepdims=True)

    do, dgg, dgain = _ew(out_bwd, pfx + "_out_bwd", [dog, o, (proj, MIX_HALF, 3)], consts=[bavg, gain],
                         outs=[(MIX_HALF, F32), (MIX_HALF, F32)], sums=[MIX_HALF])
    dq, dk, dv, dlc = _gla_bwd(do, proj, z, states, pfx + "_core_bwd")

    def decay_bwd(dl, zv):
        dz = dl * (1.0 / GLA_TAU) * (1.0 - _sigmoid(zv))
        return dz, jnp.sum(dz, axis=0, keepdims=True)

    dz, db_lr = _ew(decay_bwd, pfx + "_decay_bwd", [dlc, z], outs=[(MIX_HALF, F32)], sums=[MIX_HALF])
    dw_lr_pad = _mm(proj, dz, "tn", pfx + "_dwlr", a_cols=(7 * MIX_HALF, MIX_HALF))
    dsmall = _mm(dz, w_lr_pad, "nt", pfx + "_dsmall")
    return (dq, dk, dv, dgg, dsmall), dict(w_lr=dw_lr_pad[:GLA_RANK], b_lr=db_lr.reshape(-1), gain=dgain.reshape(-1, HEAD_DIM))


FOX_T = 512
FOX_HEADS = MIX_HALF // HEAD_DIM
NEG = -1e30
CUM_T = 512


def _cum_lanes(x, name, reverse, pre=None):
    R, S = x.shape
    nb = S // CUM_T

    def body(x_ref, o_ref, carry_ref):
        @pl.when(pl.program_id(0) == 0)
        def _():
            carry_ref[...] = jnp.zeros_like(carry_ref)

        xv = x_ref[...]
        if pre is not None:
            xv = pre(xv)
        i = lax.broadcasted_iota(jnp.int32, (CUM_T, CUM_T), 0)
        j = lax.broadcasted_iota(jnp.int32, (CUM_T, CUM_T), 1)
        tri = ((i >= j) if reverse else (i <= j)).astype(F32)
        c = lax.dot_general(xv, tri, (((1,), (0,)), ((), ())), precision=HI, preferred_element_type=F32)
        carry = carry_ref[...]
        o_ref[...] = c + carry[:, 0:1]
        carry_ref[...] = carry + jnp.sum(xv, axis=1, keepdims=True)

    spec = pl.BlockSpec((R, CUM_T), (lambda i: (0, nb - 1 - i)) if reverse else (lambda i: (0, i)))
    return pl.pallas_call(body, name=name, grid=(nb,), in_specs=[spec], out_specs=spec, out_shape=jax.ShapeDtypeStruct((R, S), F32),
                          scratch_shapes=[pltpu.VMEM((R, LANES), F32)], compiler_params=_cparams("arbitrary"))(x)


def _fox_scores(q, k, cqb, ck_ref, h, m, diag):
    cq = cqb[:, h * HEAD_DIM:h * HEAD_DIM + 1]
    ck = ck_ref[0, h:h + 1, :]
    s = lax.dot_general(q * m.astype(q.dtype), k, NT_DIMS, preferred_element_type=F32) + (cq - ck)
    if not diag:
        return s
    row = lax.broadcasted_iota(jnp.int32, (FOX_T, FOX_T), 0)
    col = lax.broadcasted_iota(jnp.int32, (FOX_T, FOX_T), 1)
    return jnp.where(row < col, NEG, s)


def _on_causal_blocks(q_blk, k_blk, step):
    @pl.when(k_blk < q_blk)
    def _():
        step(False)

    @pl.when(k_blk == q_blk)
    def _():
        step(True)


def _causal_pairs(n, key_major):
    if key_major:
        pairs = [(q, k) for k in range(n) for q in range(k, n)]
    else:
        pairs = [(q, k) for q in range(n) for k in range(q + 1)]
    return jnp.asarray([p[0] for p in pairs], jnp.int32), jnp.asarray([p[1] for p in pairs], jnp.int32)


def _carried(carry, refs, n_in, n_out, first, last):
    if carry is None:
        return refs
    ins, cx_ref, outs, co_ref = refs[:n_in], refs[n_in], refs[n_in + 1:n_in + 1 + n_out], refs[n_in + 1 + n_out]
    scratch = refs[n_in + 2 + n_out:]
    start, finish = _exchange_plan(cx_ref, co_ref, *scratch[-3:], carry[1])
    pl.when(first)(start)
    pl.when(last)(finish)
    return ins + outs + scratch[:-3]


def _carry_specs(carry):
    if carry is None:
        return [], [], [], [], []
    x, bcast = carry
    blk = x.shape if bcast else x.shape[1:]
    return [ANY], [ANY], [jax.ShapeDtypeStruct((N_CHIPS,) + tuple(blk), x.dtype)], list(_EXCHANGE_SEMS), [x]


def _fox_fwd(qn, kn, proj, cum_b, cum_tp, name, carry=None):
    S = qn.shape[0]
    nq = S // FOX_T
    qidx, kidx = _causal_pairs(nq, False)
    ntri = int(qidx.shape[0])

    def body(qidx_ref, kidx_ref, *refs):
        t = pl.program_id(1)
        first = jnp.logical_and(pl.program_id(0) == 0, t == 0)
        last = jnp.logical_and(pl.program_id(0) == N_PAIRS - 1, t == ntri - 1)
        q_ref, k_ref, v_ref, cq_ref, ck_ref, o_ref, lse_ref, m_scr, acc_scr = _carried(carry, refs, 5, 2, first, last)
        qi, ki = qidx_ref[t], kidx_ref[t]

        @pl.when(ki == 0)
        def _():
            m_scr[...] = jnp.full_like(m_scr, NEG)
            acc_scr[...] = jnp.zeros_like(acc_scr)

        left, right = _half_masks(FOX_T)

        def step(diag):
            q, k, v = q_ref[...], k_ref[...], v_ref[...].astype(BF16)
            cqb = cq_ref[...]
            for h, m in enumerate((left, right)):
                s = _fox_scores(q, k, cqb, ck_ref, h, m, diag)
                m_prev = m_scr[h]
                m_new = jnp.maximum(m_prev, jnp.max(s, axis=1, keepdims=True))
                p = jnp.exp(s - m_new)
                v_h = jnp.where(m > 0, v, jnp.ones_like(v))
                acc_scr[h] = jnp.exp(m_prev - m_new) * acc_scr[h] + jnp.dot(p.astype(BF16), v_h, preferred_element_type=F32)
                m_scr[h] = m_new

        _on_causal_blocks(qi, ki, step)

        @pl.when(ki == qi)
        def _():
            a0, a1 = acc_scr[0], acc_scr[1]
            is_left = left > 0
            num = jnp.where(is_left, a0, a1)
            den = jnp.where(is_left, pltpu.roll(a0, HEAD_DIM, 1), pltpu.roll(a1, HEAD_DIM, 1))
            o_ref[...] = num / den
            lse_ref[...] = jnp.where(is_left, m_scr[0], m_scr[1]) + jnp.log(den)

    qspec = pl.BlockSpec((FOX_T, LANES), lambda p, t, qx, kx: (qx[t], p))
    kspec = pl.BlockSpec((FOX_T, LANES), lambda p, t, qx, kx: (kx[t], p))
    vspec = pl.BlockSpec((FOX_T, LANES), lambda p, t, qx, kx: (kx[t], 6 * N_PAIRS + p))
    ckspec = pl.BlockSpec((1, 8, FOX_T), lambda p, t, qx, kx: (p, 0, kx[t]))
    full = jax.ShapeDtypeStruct((S, MIX_HALF), F32)
    c_in, c_out, c_shape, c_scratch, c_args = _carry_specs(carry)
    grid_spec = pltpu.PrefetchScalarGridSpec(
        num_scalar_prefetch=2, grid=(N_PAIRS, ntri), in_specs=[qspec, kspec, vspec, qspec, ckspec] + c_in, out_specs=[qspec, qspec] + c_out,
        scratch_shapes=[pltpu.VMEM((2, FOX_T, 1), F32), pltpu.VMEM((2, FOX_T, LANES), F32)] + c_scratch)
    return pl.pallas_call(body, name=name, grid_spec=grid_spec, out_shape=[full, full] + c_shape,
                          compiler_params=_cparams("arbitrary", "arbitrary"))(qidx, kidx, qn, kn, proj, cum_b, cum_tp, *c_args)


def _fox_bwd(do, qn, kn, proj, cum_b, cum_tp, lse_b, delta_b, name, carry=None, do_pair0=0):
    S = qn.shape[0]
    nq = S // FOX_T
    scale = HEAD_DIM ** -0.5
    qidx, kidx = _causal_pairs(nq, True)
    ntri = int(qidx.shape[0])

    def body(qidx_ref, kidx_ref, *refs):
        t = pl.program_id(1)
        first = jnp.logical_and(pl.program_id(0) == 0, t == 0)
        last = jnp.logical_and(pl.program_id(0) == N_PAIRS - 1, t == ntri - 1)
        (do_ref, q_ref, k_ref, v_ref, cq_ref, ck_ref, lse_ref, dl_ref, dq_ref, dcq_ref, dk_ref, dv_ref, dck_ref,
         dq_scr, dk_scr, dv_scr) = _carried(carry, refs, 8, 5, first, last)
        qi, ki = qidx_ref[t], kidx_ref[t]

        @pl.when(t == 0)
        def _():
            dq_scr[...] = jnp.zeros_like(dq_scr)

        @pl.when(qi == ki)
        def _():
            dk_scr[...] = jnp.zeros_like(dk_scr)
            dv_scr[...] = jnp.zeros_like(dv_scr)

        left, right = _half_masks(FOX_T)
        rows = pl.ds(pl.multiple_of(qi * FOX_T, FOX_T), FOX_T)

        def step(diag):
            q, k, v, dov = q_ref[...], k_ref[...], v_ref[...].astype(BF16), do_ref[...]
            cqb, lseb, dlb = cq_ref[...], lse_ref[...], dl_ref[...]
            dob = dov.astype(BF16)
            heads = (0, 1)
            masks = (left, right)
            col = [slice(h * HEAD_DIM, h * HEAD_DIM + 1) for h in heads]
            ss = [_fox_scores(q, k, cqb, ck_ref, h, masks[h], diag) for h in heads]
            dps = [lax.dot_general((dov * masks[h]).astype(BF16), v, NT_DIMS, preferred_element_type=F32) for h in heads]
            ps = [jnp.exp(ss[h] - lseb[:, col[h]]) for h in heads]
            dss = [(ps[h] * (dps[h] - dlb[:, col[h]])).astype(BF16) for h in heads]
            pvs = [lax.dot_general(ps[h].astype(BF16), dob, TN_DIMS, preferred_element_type=F32) for h in heads]
            dks = [lax.dot_general(dss[h], jnp.where(masks[h] > 0, q, jnp.ones_like(q)), TN_DIMS, preferred_element_type=F32) for h in heads]
            dqs = [jnp.dot(dss[h], jnp.where(masks[h] > 0, k, jnp.ones_like(k)), preferred_element_type=F32) for h in heads]
            dv_scr[...] = dv_scr[...] + left * pvs[0] + right * pvs[1]
            for h in heads:
                dk_scr[h] = dk_scr[h] + dks[h]
                dq_scr[h, rows, :] = dq_scr[h, rows, :] + dqs[h]

        _on_causal_blocks(qi, ki, step)

        @pl.when(qi == nq - 1)
        def _():
            a0, a1 = dk_scr[0], dk_scr[1]
            dk_ref[...] = left * a0 + right * a1
            dv_ref[...] = dv_scr[...]
            dck_ref[...] = left * pltpu.roll(a0, HEAD_DIM, 1) + right * pltpu.roll(a1, HEAD_DIM, 1)

        @pl.when(t == ntri - 1)
        def _():
            for r in range(nq):
                blk = pl.ds(r * FOX_T, FOX_T)
                a0, a1 = dq_scr[0, blk, :], dq_scr[1, blk, :]
                dq_ref[blk, :] = (left * a0 + right * a1) * scale
                dcq_ref[blk, :] = left * pltpu.roll(a0, HEAD_DIM, 1) + right * pltpu.roll(a1, HEAD_DIM, 1)

    qspec = pl.BlockSpec((FOX_T, LANES), lambda p, t, qx, kx: (qx[t], p))
    kspec = pl.BlockSpec((FOX_T, LANES), lambda p, t, qx, kx: (kx[t], p))
    vspec = pl.BlockSpec((FOX_T, LANES), lambda p, t, qx, kx: (kx[t], 6 * N_PAIRS + p))
    ckspec = pl.BlockSpec((1, 8, FOX_T), lambda p, t, qx, kx: (p, 0, kx[t]))
    seq = pl.BlockSpec((S, LANES), lambda p, t, qx, kx: (0, p))
    full = jax.ShapeDtypeStruct((S, MIX_HALF), F32)
    c_in, c_out, c_shape, c_scratch, c_args = _carry_specs(carry)
    grid_spec = pltpu.PrefetchScalarGridSpec(
        num_scalar_prefetch=2, grid=(N_PAIRS, ntri),
        in_specs=[pl.BlockSpec((FOX_T, LANES), lambda p, t, qx, kx: (qx[t], do_pair0 + p)), qspec, kspec, vspec, qspec, ckspec, qspec, qspec] + c_in,
        out_specs=[seq, seq, kspec, kspec, kspec] + c_out,
        scratch_shapes=[pltpu.VMEM((2, S, LANES), F32), pltpu.VMEM((2, FOX_T, LANES), F32), pltpu.VMEM((FOX_T, LANES), F32)] + c_scratch)
    return pl.pallas_call(body, name=name, grid_spec=grid_spec, out_shape=[full] * 5 + c_shape,
                          compiler_params=_cparams("arbitrary", "arbitrary"))(qidx, kidx, do, qn, kn, proj, cum_b, cum_tp, lse_b, delta_b, *c_args)


def _ff_bwd(rc, f_t, name):
    def body(rc_ref, f_ref, d_ref, s_ref):
        d = rc_ref[...] * (1.0 - _sigmoid(f_ref[...]))
        d_ref[...] = d
        s_ref[...] = jnp.sum(d, axis=1, keepdims=True)

    return pl.pallas_call(body, name=name, out_shape=[jax.ShapeDtypeStruct(rc.shape, F32), jax.ShapeDtypeStruct((rc.shape[0], 1), F32)])(rc, f_t)


def _fox_block_fwd(proj, b_f, q_gain, k_gain, bavg, pfx, carry=None):
    S = proj.shape[0]

    def prep(qv, kv, ba, qg, kg):
        return _head_rms(qv, ba)[0] * qg * (HEAD_DIM ** -0.5), _head_rms(kv, ba)[0] * kg

    qn, kn = _ew(prep, pfx + "_prep", [(proj, MIX_HALF, 4), (proj, MIX_HALF, 5)], consts=[bavg, q_gain, k_gain],
                 outs=[(MIX_HALF, BF16), (MIX_HALF, BF16)])
    f0 = 7 * MIX_HALF + GLA_RANK
    f_t = proj[:, f0:f0 + FOX_HEADS].T + b_f.reshape(FOX_HEADS, 1)
    cum = _cum_lanes(f_t, pfx + "_cum", False, pre=_log_sigmoid)
    cum_b = jnp.repeat(cum.T, HEAD_DIM, axis=1)
    cum_tp = jnp.pad(cum.reshape(N_PAIRS, 2, S), ((0, 0), (0, 6), (0, 0)))
    o, lse_b, *carried = _fox_fwd(qn, kn, proj, cum_b, cum_tp, pfx + "_attn", carry=carry)
    return o, dict(qn=qn, kn=kn, f_t=f_t, cum_b=cum_b, cum_tp=cum_tp, o=o, lse_b=lse_b), carried


def _fox_block_bwd(do, proj, q_gain, k_gain, bavg, res, pfx, carry=None):
    qn, kn, o = res["qn"], res["kn"], res["o"]
    S = proj.shape[0]
    delta_b, = _ew(lambda a, b, ba: _head_mean(a * b, ba) * float(HEAD_DIM), pfx + "_delta", [do, o], consts=[bavg], outs=[(MIX_HALF, F32)])
    do_arr, do_blk = (do[0], do[2]) if isinstance(do, tuple) else (do, 0)
    args = (do_arr, qn, kn, proj, res["cum_b"], res["cum_tp"], res["lse_b"], delta_b)
    dqn, dcq_b, dkn, dv, dck_b, *carried = _fox_bwd(*args, pfx + "_bwd", carry=carry, do_pair0=do_blk * N_PAIRS)

    def prep_bwd(dq, dk, qv, kv, ba, qg, kg):
        nq, rq = _head_rms(qv, ba)
        nk, rk = _head_rms(kv, ba)
        return (_head_rms_bwd(dq * qg, nq, rq, ba), _head_rms_bwd(dk * kg, nk, rk, ba),
                jnp.sum(dq * nq, axis=0, keepdims=True), jnp.sum(dk * nk, axis=0, keepdims=True))

    dfq, dfk, dqg, dkg = _ew(prep_bwd, pfx + "_prep_bwd", [dqn, dkn, (proj, MIX_HALF, 4), (proj, MIX_HALF, 5)],
                             consts=[bavg, q_gain, k_gain], outs=[(MIX_HALF, F32), (MIX_HALF, F32)], sums=[MIX_HALF, MIX_HALF])
    dcum = (dcq_b - dck_b)[:, ::HEAD_DIM].T
    rc = _cum_lanes(dcum, pfx + "_rcum", True)
    dff_t, db_f = _ff_bwd(rc, res["f_t"], pfx + "_ff_bwd")
    grads = dict(b_f=db_f.reshape(-1), q_gain=dqg.reshape(-1, HEAD_DIM), k_gain=dkg.reshape(-1, HEAD_DIM))
    return (dfq, dfk, dv, dff_t.T), grads, carried


WEIGHTS = ['ada_w', 'ada_b', 'even_w_in', 'even_w_out', 'gla_w_lr', 'gla_b_lr', 'gla_gain', 'fox_b_f', 'fox_q_gain', 'fox_k_gain',
           'odd_w_in', 'odd_w_out', 's5_lam_re', 's5_lam_im', 's5_log_dt', 's5_b_re', 's5_b_im', 's5_c_re', 's5_c_im', 's5_d',
           's5_w_glu', 's5_b_glu', 'sgu_ln_gain', 'sgu_ln_bias', 'sgu_w_s', 'sgu_b_s', 'mlp_w1', 'mlp_w2']
ARGS = ['x', 'c'] + WEIGHTS + ['loss_target'] + ['m_' + w for w in WEIGHTS] + ['v_' + w for w in WEIGHTS]

EVEN_COLS = 3608
EVEN_PAD = 8 * MIX_HALF
MOD = 6 * D_MODEL
MOD_SHARD = MOD // N_CHIPS

PACK_COLS = 1024
EVEN_SHARD = EVEN_COLS // N_CHIPS
SHARDED = (
    ([("even_w_in", (1, 1024, PACK_COLS), 2), ("even_w_out", (1, 256, 1024), 1), ("gla_w_lr", (1, 16, 128), 2)], 1536),
    ([("mlp_w1_0", (1, 1024, 1024), 2), ("mlp_w2_0", (1, 1024, 1024), 1), ("odd_w_in", (1, 1024, 384), 2),
      ("odd_w_out", (1, 256, 1024), 1), ("mlp_w1_1", (1, 1024, 1024), 2), ("mlp_w2_1", (1, 1024, 1024), 1),
      ("s5_w_glu", (1, 128, 512), 1), ("s5_b_glu", (1, 128), 1), ("sgu_ln_gain", (1, 128), 1), ("sgu_ln_bias", (1, 128), 1)], 5120))
REPLICATED = [("gla_b_lr", (1, 512)), ("gla_gain", (1, 8, 64)), ("fox_b_f", (1, 8)), ("fox_q_gain", (1, 8, 64)),
              ("fox_k_gain", (1, 8, 64)), ("s5_lam_re", (1, 32, 64)), ("s5_lam_im", (1, 32, 64)), ("s5_log_dt", (1, 32)),
              ("s5_b_re", (1, 32, 64, 16)), ("s5_b_im", (1, 32, 64, 16)), ("s5_c_re", (1, 32, 16, 64)), ("s5_c_im", (1, 32, 16, 64)),
              ("s5_d", (1, 32, 16)), ("sgu_w_s", (1, 8, 128, 128)), ("sgu_b_s", (1, 8, 128))]
SMALL_ROWS = 512
BIG_ADAM = {"ada_w": (2048, 1536), "even_w_in": (1024, 902), "even_w_out": (256, 1024), "odd_w_in": (1024, 384),
            "odd_w_out": (256, 1024), "mlp_w1": (2048, 1024), "mlp_w2": (2048, 1024), "s5_w_glu": (128, 512)}


PACK_ALIGN = 16


def _piece_rows(shape):
    rows = -(-math.prod(shape) // PACK_COLS)
    return -(-rows // PACK_ALIGN) * PACK_ALIGN


def _to_rows(p, lead=()):
    n = math.prod(p.shape[len(lead):])
    rows = _piece_rows(p.shape[len(lead):])
    flat = p.reshape(lead + (n,))
    if rows * PACK_COLS != n:
        flat = jnp.pad(flat, [(0, 0)] * len(lead) + [(0, rows * PACK_COLS - n)])
    return flat.reshape(lead + (rows, PACK_COLS))


def _from_rows(x, r0, shape, lead=()):
    n = math.prod(shape)
    seg = lax.slice_in_dim(x, r0, r0 + _piece_rows(shape), axis=len(lead)).reshape(lead + (-1,))
    return lax.slice_in_dim(seg, 0, n, axis=len(lead)).reshape(lead + tuple(shape))


def _pack_rows(pieces, rows):
    x = jnp.concatenate([_to_rows(p) for p in pieces], axis=0)
    return jnp.pad(x, ((0, rows - x.shape[0]), (0, 0)))


def _unpack(x, specs):
    out, r0 = {}, 0
    for name, shape in specs:
        out[name] = _from_rows(x, r0, shape)
        r0 += _piece_rows(shape)
    return out


def _shards_to_full(x4, pieces):
    out, r0 = {}, 0
    for name, shape, axis in pieces:
        seg = _from_rows(x4, r0, shape, lead=(N_CHIPS,))
        out[name] = jnp.concatenate([seg[k] for k in range(N_CHIPS)], axis=axis)
        r0 += _piece_rows(shape)
    return out


def _full_to_shards(full, pieces, rows):
    blocks = [_to_rows(jnp.stack(jnp.split(full[name], N_CHIPS, axis=axis)), lead=(N_CHIPS,)) for name, _, axis in pieces]
    x = jnp.concatenate(blocks, axis=1)
    return jnp.pad(x, ((0, 0), (0, rows - x.shape[1]), (0, 0)))


def _gather_prep(local, pieces, rows):
    shard = _pack_rows([local[n] for n, _, _ in pieces], rows).astype(BF16)
    return lax.dynamic_slice_in_dim(shard, lax.axis_index("c") * (rows // 2), rows // 2, axis=0)


def _gather_finish(collected, pieces, rows, tag):
    halves = _by_core(collected, _pair_swap(collected, tag + "_pair"))
    return _shards_to_full(halves.transpose(1, 0, 2, 3).reshape(N_CHIPS, rows, PACK_COLS), pieces)


def _reduce_prep(full, pieces, rows, tag):
    mc = lax.axis_index("c")
    packed = _full_to_shards(full, pieces, rows)
    hr = rows // 2
    mine = lax.dynamic_slice_in_dim(packed, mc * hr, hr, axis=1)
    other = lax.dynamic_slice_in_dim(packed, (1 - mc) * hr, hr, axis=1)
    theirs = _pair_swap(other.astype(BF16), tag + "_pair")
    pair_sum, = _ew(lambda p, q: p + q, tag + "_pair_sum", [mine.reshape(N_CHIPS * hr, PACK_COLS), theirs.reshape(N_CHIPS * hr, PACK_COLS)],
                    outs=[(PACK_COLS, BF16)])
    return pair_sum.reshape(N_CHIPS, hr, PACK_COLS)


def _reduce_finish(arrived, pieces, rows, tag):
    red_half = _sum_slots(arrived, tag + "_chip_sum")
    reduced = _by_core(red_half, _pair_swap(red_half, tag + "_pair_out")).reshape(rows, PACK_COLS)
    return _unpack(reduced, [(n, s) for n, s, _ in pieces])


def _relu2(t):
    r = jnp.maximum(t, 0.0)
    return r * r


def _silu(t):
    return t * _sigmoid(t)


def _pack_even(w):
    return jnp.concatenate([w[:, :2048], w[:, 2064:3600], w[:, 2048:2064], w[:, 3600:3608],
                            jnp.zeros((w.shape[0], EVEN_PAD - EVEN_COLS), w.dtype)], axis=1)


def _unpack_even(wp):
    return jnp.concatenate([wp[:, :2048], wp[:, 3584:3600], wp[:, 2048:3584], wp[:, 3600:3608]], axis=1)


def _mlp_fwd(h, w1, w2, pfx):
    pre = _mm(h, w1, "nn", pfx + "_up", out_dtype=BF16)
    return pre, _mm(pre, w2, "nn", pfx + "_down", a_pro=_relu2)


def _mlp_bwd(dm, h, pre, w1, w2, pfx):
    dpre = _mm(dm, w2, "nt", pfx + "_dpre", epi=lambda acc, p: acc * (2.0 * jnp.maximum(p, 0.0)), extras=[(pre, "mn")], out_dtype=BF16)
    dw2 = _mm(pre, dm, "tn", pfx + "_dw2", a_pro=_relu2)
    dw1 = _mm(h, dpre, "tn", pfx + "_dw1")
    dh = _mm(dpre, w1, "nt", pfx + "_dh")
    return dh, dw1, dw2


def _step(args):
    a = dict(zip(ARGS, args, strict=True))
    x0 = a["x"][0]
    target = a["loss_target"][0]
    mx, my, mc = lax.axis_index("x"), lax.axis_index("y"), lax.axis_index("c")
    chip = 2 * mx + my
    dev = 2 * chip + mc
    bavg = _head_avg_matrix(MIX_HALF)

    c_all = _gather8(jnp.pad(a["c"], ((0, 7), (0, 0))), "c_gather")[:, :, 0, :].reshape(2 * N_CHIPS, D_MODEL)
    ada_b_shard = lax.dynamic_slice_in_dim(a["ada_b"], chip * MOD_SHARD, MOD_SHARD, axis=1)
    mod_sh = [_mm(c_all, a["ada_w"][l], "nn", f"mod{l}", a_pro=_silu, epi=lambda acc, b: acc + b, extras=[(ada_b_shard[l:l + 1], "n")])
              for l in range(2)]
    small3 = jnp.zeros((8, MOD_SHARD), F32)
    for r, n in enumerate(("s5_b_glu", "sgu_ln_gain", "sgu_ln_bias")):
        small3 = small3.at[r, :LANES].set(a[n][0])
    mod_all = _chip_exchange(jnp.concatenate(mod_sh + [small3]), "mod_gather", True)
    mods = []
    for l in range(2):
        full = mod_all[:, 8 * l:8 * l + 8].transpose(1, 0, 2).reshape(8, MOD)
        mods.append(jnp.split(lax.dynamic_slice_in_dim(full, dev, 1, axis=0), 6, axis=1))
    b_glu, ln_gain, ln_bias = (mod_all[:, 16 + r, :LANES].reshape(1, MIX_HALF) for r in range(3))

    local = dict(a, even_w_in=jnp.pad(a["even_w_in"], ((0, 0), (0, 0), (0, PACK_COLS - EVEN_SHARD))),
                 mlp_w1_0=a["mlp_w1"][0:1], mlp_w1_1=a["mlp_w1"][1:2], mlp_w2_0=a["mlp_w2"][0:1], mlp_w2_1=a["mlp_w2"][1:2])
    (pieces0, rows0), (pieces1, rows1) = SHARDED
    w = _gather_finish(_chip_exchange(_gather_prep(local, pieces0, rows0), "w0_chips", True), pieces0, rows0, "w0")
    w_even = _pack_even(w["even_w_in"][0].reshape(D_MODEL, N_CHIPS, PACK_COLS)[:, :, :EVEN_SHARD].reshape(D_MODEL, EVEN_COLS))
    w_lr_pad = jnp.zeros((MIX_HALF, MIX_HALF), BF16).at[:GLA_RANK].set(w["gla_w_lr"][0])
    gla_b_lr = a["gla_b_lr"]
    gla_gain, q_gain, k_gain = (a[n].reshape(1, MIX_HALF) for n in ("gla_gain", "fox_q_gain", "fox_k_gain"))
    sgu_wm, sgu_bt = _sgu_tables(a["sgu_w_s"][0], a["sgu_b_s"][0])

    sh1, sc1, g1, sh2, sc2, g2 = mods[0]
    _, h1_0 = _res_rms(x0, sc1, sh1, "l0_norm1")
    proj0 = _mm(h1_0, w_even, "nn", "l0_proj")
    og, gla_res = _gla_block_fwd(proj0, w_lr_pad, gla_b_lr, gla_gain, bavg, "gla")
    of, fox_res, (collected1,) = _fox_block_fwd(proj0, a["fox_b_f"][0], q_gain, k_gain, bavg, "fox",
                                                carry=(_gather_prep(local, pieces1, rows1), True))
    w.update(_gather_finish(collected1, pieces1, rows1, "w1"))
    s5w = dict(lam_re=a["s5_lam_re"][0], lam_im=a["s5_lam_im"][0], log_dt=a["s5_log_dt"][0], b_re=a["s5_b_re"][0], b_im=a["s5_b_im"][0],
               c_re=a["s5_c_re"][0], c_im=a["s5_c_im"][0], d=a["s5_d"][0], w_glu=w["s5_w_glu"][0], b_glu=b_glu)
    mixed0 = jnp.concatenate([og, of], axis=1).astype(BF16)
    y0 = _mm(mixed0, w["even_w_out"][0], "nn", "l0_out")
    x1, h2_0 = _res_rms(x0, sc2, sh2, "l0_norm2", y=y0, g=g1)
    pre0, m0 = _mlp_fwd(h2_0, w["mlp_w1_0"][0], w["mlp_w2_0"][0], "l0_mlp")
    sh1b, sc1b, g1b, sh2b, sc2b, g2b = mods[1]
    x2, h1_1 = _res_rms(x1, sc1b, sh1b, "l1_norm1", y=m0, g=g2)
    proj1 = _mm(h1_1, w["odd_w_in"][0], "nn", "l1_proj")
    ys5, s5_res = _s5_block_fwd(proj1[:, :MIX_HALF], s5w, "s5")
    ysgu = _sgu_fwd(proj1, ln_gain, ln_bias, sgu_wm, sgu_bt, "sgu")
    mixed1 = jnp.concatenate([ys5, ysgu], axis=1).astype(BF16)
    y1 = _mm(mixed1, w["odd_w_out"][0], "nn", "l1_out")
    x3, h2_1 = _res_rms(x2, sc2b, sh2b, "l1_norm2", y=y1, g=g1b)
    pre1, m1 = _mlp_fwd(h2_1, w["mlp_w1_1"][0], w["mlp_w2_1"][0], "l1_mlp")
    loss_b, dx4, dm1, dg2b = _res_loss(x3, m1, g2b, target, "loss")
    loss = lax.psum(loss_b[0, 0], ("x", "y", "c"))

    full = {}
    dh2_1, dw1_1, dw2_1 = _mlp_bwd(dm1, h2_1, pre1, w["mlp_w1_1"][0], w["mlp_w2_1"][0], "l1_mlp")
    dx3, dy1, dg1b, dsc2b, dsh2b = _res_rms_bwd(x3, dh2_1, sc2b, dx4, "l1_norm2_bwd", y=y1, g=g1b)
    dmixed1 = _mm(dy1, w["odd_w_out"][0], "nt", "l1_out_dx")
    full["odd_w_out"] = _mm(mixed1, dy1, "tn", "l1_out_dw")[None]
    du, s5g = _s5_block_bwd(dmixed1[:, :MIX_HALF], s5w, s5_res, "s5")
    dzu, dzv, dws, dbt, dlg, dlb = _sgu_bwd(dmixed1[:, MIX_HALF:], proj1, ln_gain, ln_bias, sgu_wm, sgu_bt, "sgu_bwd")
    g_ws, g_bs = _sgu_grads(dws, dbt)
    dproj1 = jnp.concatenate([du, dzu, dzv], axis=1).astype(BF16)
    full["odd_w_in"] = _mm(h1_1, dproj1, "tn", "l1_proj_dw")[None]
    dh1_1 = _mm(dproj1, w["odd_w_in"][0], "nt", "l1_proj_dx")
    dx2, dm0, dg2, dsc1b, dsh1b = _res_rms_bwd(x2, dh1_1, sc1b, dx3, "l1_norm1_bwd", y=m0, g=g2)
    dh2_0, dw1_0, dw2_0 = _mlp_bwd(dm0, h2_0, pre0, w["mlp_w1_0"][0], w["mlp_w2_0"][0], "l0_mlp")
    full.update(mlp_w1_0=dw1_0[None], mlp_w2_0=dw2_0[None], mlp_w1_1=dw1_1[None], mlp_w2_1=dw2_1[None], s5_w_glu=s5g["w_glu"][None],
                s5_b_glu=s5g["b_glu"][None], sgu_ln_gain=dlg, sgu_ln_bias=dlb)
    pair_sums1 = _reduce_prep(full, pieces1, rows1, "g1")
    dx1, dy0, dg1, dsc2, dsh2 = _res_rms_bwd(x1, dh2_0, sc2, dx2, "l0_norm2_bwd", y=y0, g=g1)
    dmixed0 = _mm(dy0, w["even_w_out"][0], "nt", "l0_out_dx")
    full["even_w_out"] = _mm(mixed0, dy0, "tn", "l0_out_dw")[None]
    (dgq, dgk, dgv, dgg, dsmall), glag = _gla_block_bwd((dmixed0, MIX_HALF, 0), proj0, w_lr_pad, gla_gain, bavg, gla_res, "gla")
    (dfq, dfk, dfv, dff), foxg, (arrived1,) = _fox_block_bwd((dmixed0, MIX_HALF, 1), proj0, q_gain, k_gain, bavg, fox_res, "fox",
                                                           carry=(pair_sums1, False))
    dsmall = lax.dynamic_update_slice(dsmall, dff, (0, GLA_RANK))
    dproj0 = jnp.concatenate([dgq, dgk, dgv, dgg, dfq, dfk, dfv, dsmall], axis=1).astype(BF16)
    d_even = _unpack_even(_mm(h1_0, dproj0, "tn", "l0_proj_dw")).reshape(D_MODEL, N_CHIPS, EVEN_SHARD)
    full["even_w_in"] = jnp.pad(d_even, ((0, 0), (0, 0), (0, PACK_COLS - EVEN_SHARD))).reshape(1, D_MODEL, N_CHIPS * PACK_COLS)
    dh1_0 = _mm(dproj0, w_even, "nt", "l0_proj_dx")
    grad_x, dsc1, dsh1 = _res_rms_bwd(x0, dh1_0, sc1, dx1, "l0_norm1_bwd")
    full["gla_w_lr"] = glag["w_lr"][None]

    dmod = jnp.concatenate([dsh1, dsc1, dg1, dsh2, dsc2, dg2, dsh1b, dsc1b, dg1b, dsh2b, dsc2b, dg2b], axis=1)
    dmod_all = _gather8(jnp.pad(dmod, ((0, 7), (0, 0))), "dmod_gather")[:, :, 0, :].reshape(2 * N_CHIPS, 2, MOD)
    grads = {}
    grads["ada_w"] = jnp.stack([
        _mm(c_all, lax.dynamic_slice_in_dim(dmod_all[:, l], chip * MOD_SHARD, MOD_SHARD, axis=1), "tn", f"ada_dw{l}", a_pro=_silu)
        for l in range(2)])
    grads["ada_b"] = _sum_slots(dmod_all.reshape(2 * N_CHIPS, 2 * MOD // MIX_HALF, MIX_HALF), "ada_db").reshape(2, MOD)

    grads.update(_reduce_finish(arrived1, pieces1, rows1, "g1"))
    grads.update(_reduce_finish(_chip_exchange(_reduce_prep(full, pieces0, rows0, "g0"), "g0_chips", False), pieces0, rows0, "g0"))
    grads["even_w_in"] = grads["even_w_in"][:, :, :EVEN_SHARD]
    grads["mlp_w1"] = jnp.concatenate([grads.pop("mlp_w1_0"), grads.pop("mlp_w1_1")])
    grads["mlp_w2"] = jnp.concatenate([grads.pop("mlp_w2_0"), grads.pop("mlp_w2_1")])

    part = dict(gla_b_lr=glag["b_lr"], gla_gain=glag["gain"], fox_b_f=foxg["b_f"], fox_q_gain=foxg["q_gain"], fox_k_gain=foxg["k_gain"],
                s5_lam_re=s5g["lam_re"], s5_lam_im=s5g["lam_im"], s5_log_dt=s5g["log_dt"], s5_b_re=s5g["b_re"], s5_b_im=s5g["b_im"],
                s5_c_re=s5g["c_re"], s5_c_im=s5g["c_im"], s5_d=s5g["d"], sgu_w_s=g_ws, sgu_b_s=g_bs)
    parts_all = _gather8(_pack_rows([part[n] for n, _ in REPLICATED], SMALL_ROWS).astype(BF16), "rep_gather")
    rep = _sum_slots(parts_all.reshape(2 * N_CHIPS, SMALL_ROWS, PACK_COLS), "rep_sum")
    grads.update(_unpack(rep, REPLICATED))

    delta, new_m, new_v = {}, {}, {}
    for n, shape2 in BIG_ADAM.items():
        d, nm, nv = _adamw(a[n].reshape(shape2), grads[n].reshape(shape2), a["m_" + n].reshape(shape2), a["v_" + n].reshape(shape2), "adamw_" + n)
        delta[n], new_m[n], new_v[n] = (t.reshape(a[n].shape) for t in (d, nm, nv))
    small = [n for n in WEIGHTS if n not in BIG_ADAM]
    spec = [(n, a[n].shape) for n in small]
    packs = [_pack_rows([src[n] for n in small], SMALL_ROWS) for src in
             (a, grads, {n: a["m_" + n] for n in small}, {n: a["v_" + n] for n in small})]
    for tgt, res in zip((delta, new_m, new_v), _adamw(*packs, "adamw_small")):
        tgt.update(_unpack(res, spec))
    outs = [loss, grad_x[None]]
    for group in (grads, delta, new_m, new_v):
        outs += [group[n].reshape(a[n].shape) for n in WEIGHTS]
    return tuple(outs)


def kernel(x, c, ada_w, ada_b, even_w_in, even_w_out, gla_w_lr, gla_b_lr, gla_gain, fox_b_f, fox_q_gain, fox_k_gain, odd_w_in,
           odd_w_out, s5_lam_re, s5_lam_im, s5_log_dt, s5_b_re, s5_b_im, s5_c_re, s5_c_im, s5_d, s5_w_glu, s5_b_glu, sgu_ln_gain,
           sgu_ln_bias, sgu_w_s, sgu_b_s, mlp_w1, mlp_w2, loss_target, m_ada_w, m_ada_b, m_even_w_in, m_even_w_out, m_gla_w_lr,
           m_gla_b_lr, m_gla_gain, m_fox_b_f, m_fox_q_gain, m_fox_k_gain, m_odd_w_in, m_odd_w_out, m_s5_lam_re, m_s5_lam_im,
           m_s5_log_dt, m_s5_b_re, m_s5_b_im, m_s5_c_re, m_s5_c_im, m_s5_d, m_s5_w_glu, m_s5_b_glu, m_sgu_ln_gain, m_sgu_ln_bias,
           m_sgu_w_s, m_sgu_b_s, m_mlp_w1, m_mlp_w2, v_ada_w, v_ada_b, v_even_w_in, v_even_w_out, v_gla_w_lr, v_gla_b_lr,
           v_gla_gain, v_fox_b_f, v_fox_q_gain, v_fox_k_gain, v_odd_w_in, v_odd_w_out, v_s5_lam_re, v_s5_lam_im, v_s5_log_dt,
           v_s5_b_re, v_s5_b_im, v_s5_c_re, v_s5_c_im, v_s5_d, v_s5_w_glu, v_s5_b_glu, v_sgu_ln_gain, v_sgu_ln_bias, v_sgu_w_s,
           v_sgu_b_s, v_mlp_w1, v_mlp_w2):
    return _step((x, c, ada_w, ada_b, even_w_in, even_w_out, gla_w_lr, gla_b_lr, gla_gain, fox_b_f, fox_q_gain, fox_k_gain,
                  odd_w_in, odd_w_out, s5_lam_re, s5_lam_im, s5_log_dt, s5_b_re, s5_b_im, s5_c_re, s5_c_im, s5_d, s5_w_glu,
                  s5_b_glu, sgu_ln_gain, sgu_ln_bias, sgu_w_s, sgu_b_s, mlp_w1, mlp_w2, loss_target, m_ada_w, m_ada_b,
                  m_even_w_in, m_even_w_out, m_gla_w_lr, m_gla_b_lr, m_gla_gain, m_fox_b_f, m_fox_q_gain, m_fox_k_gain,
                  m_odd_w_in, m_odd_w_out, m_s5_lam_re, m_s5_lam_im, m_s5_log_dt, m_s5_b_re, m_s5_b_im, m_s5_c_re, m_s5_c_im,
                  m_s5_d, m_s5_w_glu, m_s5_b_glu, m_sgu_ln_gain, m_sgu_ln_bias, m_sgu_w_s, m_sgu_b_s, m_mlp_w1, m_mlp_w2, v_ada_w,
                  v_ada_b, v_even_w_in, v_even_w_out, v_gla_w_lr, v_gla_b_lr, v_gla_gain, v_fox_b_f, v_fox_q_gain, v_fox_k_gain,
                  v_odd_w_in, v_odd_w_out, v_s5_lam_re, v_s5_lam_im, v_s5_log_dt, v_s5_b_re, v_s5_b_im, v_s5_c_re, v_s5_c_im,
                  v_s5_d, v_s5_w_glu, v_s5_b_glu, v_sgu_ln_gain, v_sgu_ln_bias, v_sgu_w_s, v_sgu_b_s, v_mlp_w1, v_mlp_w2))
```

```python
import functools
import math

import jax
import jax.numpy as jnp
import numpy as np
from jax import lax
from jax.experimental import pallas as pl
from jax.experimental.pallas import tpu as pltpu

F32 = jnp.float32
BF16 = jnp.bfloat16
MESH = pl.DeviceIdType.MESH
ANY = pl.BlockSpec(memory_space=pl.ANY)
DMA_SEM = pltpu.SemaphoreType.DMA

D_MODEL = 1024
HEAD_DIM = 64
MIX_HALF = 512
GLA_RANK = 16
GLA_TAU = 16.0
GLA_CHUNK = 64
S5_GROUPS = 32
S5_GROUP_WIDTH = 16
S5_STATE = 64
S5_N = S5_GROUPS * S5_STATE
SGU_GROUPS = 8
SGU_CHUNK = 128
D_FF = 4096
EPS = 1e-6
N_CHIPS = 4
LANES = 128
VMEM_LIMIT = 48 * 1024 * 1024
PAIR_COPIES = 16

ADAM_LR = 0.001
ADAM_B1 = 0.9
ADAM_B2 = 0.999
ADAM_EPS = 1e-08
ADAM_WD = 0.01
ADAM_STEP = 10


def _cparams(*sem):
    return pltpu.CompilerParams(dimension_semantics=sem, vmem_limit_bytes=VMEM_LIMIT)


def _pair_swap(x, name):
    lead = x.shape[:-2]
    rows = x.shape[-2]
    nsplit = max(1, PAIR_COPIES // max(1, math.prod(lead)))
    while nsplit > 1 and rows % (nsplit * 16):
        nsplit -= 1
    pieces = [idx + (pl.ds(j * (rows // nsplit), rows // nsplit),) for idx in np.ndindex(*lead) for j in range(nsplit)]

    def body(x_ref, o_ref, send_sems, recv_sems):
        mx, my, mc = lax.axis_index("x"), lax.axis_index("y"), lax.axis_index("c")
        copies = [pltpu.make_async_remote_copy(src_ref=x_ref.at[p], dst_ref=o_ref.at[p], send_sem=send_sems.at[j], recv_sem=recv_sems.at[j],
                                               device_id=(mx, my, 1 - mc), device_id_type=MESH) for j, p in enumerate(pieces)]
        for cp in copies:
            cp.start()
        for cp in copies:
            cp.wait_recv()
        for cp in copies:
            cp.wait_send()

    return pl.pallas_call(
        body, name=name, out_shape=jax.ShapeDtypeStruct(x.shape, x.dtype), in_specs=[ANY], out_specs=ANY,
        scratch_shapes=[DMA_SEM((len(pieces),)), DMA_SEM((len(pieces),))])(x)


def _by_core(mine, theirs):
    first = lax.axis_index("c") == 0
    return jnp.stack([jnp.where(first, mine, theirs), jnp.where(first, theirs, mine)])


def _chip_exchange(x, name, bcast):
    blk = x.shape if bcast else x.shape[1:]

    def body(x_ref, o_ref, send_sems, recv_sems, loc_sem):
        start, finish = _exchange_plan(x_ref, o_ref, send_sems, recv_sems, loc_sem, bcast)
        start()
        finish()

    return pl.pallas_call(
        body, name=name, out_shape=jax.ShapeDtypeStruct((N_CHIPS,) + tuple(blk), x.dtype), in_specs=[ANY], out_specs=ANY,
        scratch_shapes=_EXCHANGE_SEMS)(x)


_EXCHANGE_SEMS = [DMA_SEM((3,)), DMA_SEM((3,)), DMA_SEM]


def _exchange_plan(x_ref, o_ref, send_sems, recv_sems, loc_sem, bcast):
    mx, my, mc = lax.axis_index("x"), lax.axis_index("y"), lax.axis_index("c")
    me = 2 * mx + my
    peers = [(1 - mx, my), (mx, 1 - my), (1 - mx, 1 - my)]

    def src(k):
        return x_ref if bcast else x_ref.at[k]

    def remote(j, source, slot):
        px, py = peers[j]
        return pltpu.make_async_remote_copy(src_ref=source, dst_ref=o_ref.at[slot], send_sem=send_sems.at[j], recv_sem=recv_sems.at[j],
                                            device_id=(px, py, mc), device_id_type=MESH)

    loc = pltpu.make_async_copy(src(me), o_ref.at[me], loc_sem)
    sends = [remote(j, src(2 * px + py), me) for j, (px, py) in enumerate(peers)]
    arrivals = [remote(j, src(me), 2 * px + py) for j, (px, py) in enumerate(peers)]

    def start():
        loc.start()
        for cp in sends:
            cp.start()

    def finish():
        for cp in arrivals:
            cp.wait_recv()
        for cp in sends:
            cp.wait_send()
        loc.wait()

    return start, finish


def _gather8(x, name):
    collected = _chip_exchange(x, name + "_chips", True)
    return jnp.swapaxes(_by_core(collected, _pair_swap(collected, name + "_pair")), 0, 1)


def _tile(n, want):
    if n <= want:
        return n
    t = (want // LANES) * LANES
    while t >= LANES:
        if n % t == 0:
            return t
        t -= LANES
    raise ValueError(f"no lane-aligned tile for {n}")


_DIMS = {"nn": (((1,), (0,)), ((), ())), "nt": (((1,), (1,)), ((), ())), "tn": (((0,), (0,)), ((), ()))}


MM_FULL_K = 4096
MM_SLAB_K = 2048
MM_TILES = ((1024, 1024), (512, 1024), (1024, 512), (512, 512), (256, 512), (256, 256))
MM_VMEM_BUDGET = 36 * 1024 * 1024


def _mm(a, b, mode, name, *, a_pro=None, epi=None, extras=(), out_dtype=F32, tm_max=1024, tn_max=1024, tk=None, a_cols=None):
    c0, csize = a_cols if a_cols is not None else (0, a.shape[1])
    if mode == "tn":
        K, M = a.shape[0], csize
    else:
        M, K = a.shape[0], csize
    N = b.shape[0] if mode == "nt" else b.shape[1]
    assert (b.shape[1] if mode == "nt" else b.shape[0]) == K, (a.shape, b.shape, mode)
    if tk is None:
        tk = K if (mode != "tn" and K <= MM_FULL_K) else MM_SLAB_K
    tk = _tile(K, tk)
    nk = K // tk
    n_mn = sum(1 for _, kind in extras if kind == "mn")
    for tm_want, tn_want in MM_TILES:
        tm, tn = _tile(M, min(tm_want, tm_max)), _tile(N, min(tn_want, tn_max))
        need = 2 * (tm * tk * a.dtype.itemsize + tk * tn * b.dtype.itemsize + tm * tn * 4 * (1 + n_mn)) + tm * tn * 4 * (nk > 1)
        if need <= MM_VMEM_BUDGET:
            break
    if mode == "tn":
        assert c0 % tm == 0
        a_spec = pl.BlockSpec((tk, tm), lambda i, j, k: (k, i + c0 // tm))
    else:
        assert c0 % tk == 0
        a_spec = pl.BlockSpec((tm, tk), lambda i, j, k: (i, k + c0 // tk))
    b_spec = pl.BlockSpec((tn, tk), lambda i, j, k: (j, k)) if mode == "nt" else pl.BlockSpec((tk, tn), lambda i, j, k: (k, j))
    ex_specs = []
    for arr, kind in extras:
        if kind == "mn":
            assert arr.shape == (M, N)
            ex_specs.append(pl.BlockSpec((tm, tn), lambda i, j, k: (i, j)))
        else:
            assert arr.shape == (1, N)
            ex_specs.append(pl.BlockSpec((1, tn), lambda i, j, k: (0, j)))
    n_ex = len(extras)

    def body(*refs):
        a_ref, b_ref = refs[:2]
        ex_refs = refs[2:2 + n_ex]
        o_ref = refs[2 + n_ex]
        acc_ref = refs[3 + n_ex] if nk > 1 else None
        k = pl.program_id(2)
        av = a_ref[...]
        if a_pro is not None:
            av = a_pro(av)
        part = lax.dot_general(av.astype(BF16), b_ref[...].astype(BF16), _DIMS[mode], preferred_element_type=F32)
        if nk == 1:
            if epi is not None:
                part = epi(part, *[r[...] for r in ex_refs])
            o_ref[...] = part.astype(o_ref.dtype)
            return

        @pl.when(k == 0)
        def _():
            acc_ref[...] = part

        @pl.when(k > 0)
        def _():
            acc_ref[...] += part

        @pl.when(k == nk - 1)
        def _():
            acc = acc_ref[...]
            if epi is not None:
                acc = epi(acc, *[r[...] for r in ex_refs])
            o_ref[...] = acc.astype(o_ref.dtype)

    return pl.pallas_call(
        body, name=name, grid=(M // tm, N // tn, nk),
        in_specs=[a_spec, b_spec] + ex_specs,
        out_specs=pl.BlockSpec((tm, tn), lambda i, j, k: (i, j)),
        out_shape=jax.ShapeDtypeStruct((M, N), out_dtype),
        scratch_shapes=[pltpu.VMEM((tm, tn), F32)] if nk > 1 else [],
        compiler_params=_cparams("parallel", "parallel", "arbitrary"))(a, b, *[e[0] for e in extras])


ROWS = 256


def _row_spec(w, ts=ROWS):
    return pl.BlockSpec((ts, w), lambda i: (i, 0))


def _vec_spec(w):
    return pl.BlockSpec((1, w), lambda i: (0, 0))


def _res_rms(x, sc, sh, name, y=None, g=None):
    S, D = x.shape
    has_res = y is not None

    def body(*refs):
        if has_res:
            x_ref, y_ref, g_ref, sc_ref, sh_ref, xo_ref, h_ref = refs
            xv = x_ref[...] + g_ref[...] * y_ref[...]
            xo_ref[...] = xv
        else:
            x_ref, sc_ref, sh_ref, h_ref = refs
            xv = x_ref[...]
        r = lax.rsqrt(jnp.mean(xv * xv, axis=-1, keepdims=True) + EPS)
        h_ref[...] = (xv * r * (1.0 + sc_ref[...]) + sh_ref[...]).astype(BF16)

    row, vec = _row_spec(D), _vec_spec(D)
    if has_res:
        return pl.pallas_call(body, name=name, grid=(S // ROWS,), in_specs=[row, row, vec, vec, vec], out_specs=[row, row],
                              out_shape=[jax.ShapeDtypeStruct((S, D), F32), jax.ShapeDtypeStruct((S, D), BF16)],
                              compiler_params=_cparams("parallel"))(x, y, g, sc, sh)
    h = pl.pallas_call(body, name=name, grid=(S // ROWS,), in_specs=[row, vec, vec], out_specs=row,
                       out_shape=jax.ShapeDtypeStruct((S, D), BF16), compiler_params=_cparams("parallel"))(x, sc, sh)
    return x, h


def _res_rms_bwd(x, dh, sc, dres, name, y=None, g=None):
    S, D = x.shape
    has_res = y is not None

    def body(*refs):
        if has_res:
            x_ref, dh_ref, sc_ref, dres_ref, y_ref, g_ref, dx_ref, dy_ref, dg_ref, dsc_ref, dsh_ref = refs
        else:
            x_ref, dh_ref, sc_ref, dres_ref, dx_ref, dsc_ref, dsh_ref = refs
        first = pl.program_id(0) == 0
        xv = x_ref[...]
        dh = dh_ref[...].astype(F32)
        r = lax.rsqrt(jnp.mean(xv * xv, axis=-1, keepdims=True) + EPS)
        xn = xv * r
        dxn = dh * (1.0 + sc_ref[...])
        dx = dres_ref[...] + r * (dxn - xn * jnp.mean(dxn * xn, axis=-1, keepdims=True))
        dx_ref[...] = dx
        parts = [(dsc_ref, jnp.sum(dh * xn, axis=0, keepdims=True)), (dsh_ref, jnp.sum(dh, axis=0, keepdims=True))]
        if has_res:
            dy_ref[...] = (dx * g_ref[...]).astype(BF16)
            parts.append((dg_ref, jnp.sum(dx * y_ref[...], axis=0, keepdims=True)))
        for ref, val in parts:
            @pl.when(first)
            def _(ref=ref, val=val):
                ref[...] = val

            @pl.when(jnp.logical_not(first))
            def _(ref=ref, val=val):
                ref[...] += val

    row, vec = _row_spec(D), _vec_spec(D)
    full = jax.ShapeDtypeStruct((S, D), F32)
    v = jax.ShapeDtypeStruct((1, D), F32)
    if has_res:
        return pl.pallas_call(body, name=name, grid=(S // ROWS,), in_specs=[row, row, vec, row, row, vec],
                              out_specs=[row, row, vec, vec, vec], out_shape=[full, jax.ShapeDtypeStruct((S, D), BF16), v, v, v],
                              compiler_params=_cparams("arbitrary"))(x, dh, sc, dres, y, g)
    return pl.pallas_call(body, name=name, grid=(S // ROWS,), in_specs=[row, row, vec, row],
                          out_specs=[row, vec, vec], out_shape=[full, v, v],
                          compiler_params=_cparams("arbitrary"))(x, dh, sc, dres)


def _res_loss(x, m, g, target, name):
    S, D = x.shape

    def body(x_ref, m_ref, g_ref, t_ref, loss_ref, dx_ref, dm_ref, dg_ref):
        first = pl.program_id(0) == 0
        mv = m_ref[...].astype(F32)
        err = x_ref[...] + g_ref[...] * mv - t_ref[...]
        dx = err * (1.0 / D)
        dx_ref[...] = dx
        dm_ref[...] = (dx * g_ref[...]).astype(BF16)
        part = 0.5 * jnp.sum(jnp.mean(err * err, axis=-1, keepdims=True), axis=0, keepdims=True)
        dg = jnp.sum(dx * mv, axis=0, keepdims=True)

        @pl.when(first)
        def _():
            loss_ref[...] = jnp.broadcast_to(part, loss_ref.shape)
            dg_ref[...] = dg

        @pl.when(jnp.logical_not(first))
        def _():
            loss_ref[...] += jnp.broadcast_to(part, loss_ref.shape)
            dg_ref[...] += dg

    row, vec = _row_spec(D), _vec_spec(D)
    full = jax.ShapeDtypeStruct((S, D), F32)
    return pl.pallas_call(body, name=name, grid=(S // ROWS,), in_specs=[row, row, vec, row],
                          out_specs=[pl.BlockSpec((8, LANES), lambda i: (0, 0)), row, row, vec],
                          out_shape=[jax.ShapeDtypeStruct((8, LANES), F32), full, jax.ShapeDtypeStruct((S, D), BF16), jax.ShapeDtypeStruct((1, D), F32)],
                          compiler_params=_cparams("arbitrary"))(x, m, g, target)


def _adamw(w, g, m, v, name):
    R, C = w.shape
    tr = R if R <= 256 else 256
    assert R % tr == 0

    def body(w_ref, g_ref, m_ref, v_ref, d_ref, nm_ref, nv_ref):
        gv = g_ref[...]
        nm = ADAM_B1 * m_ref[...] + (1.0 - ADAM_B1) * gv
        nv = ADAM_B2 * v_ref[...] + (1.0 - ADAM_B2) * jnp.square(gv)
        m_hat = nm / (1.0 - ADAM_B1 ** ADAM_STEP)
        v_hat = nv / (1.0 - ADAM_B2 ** ADAM_STEP)
        d_ref[...] = -ADAM_LR * (m_hat / (jnp.sqrt(v_hat) + ADAM_EPS) + ADAM_WD * w_ref[...])
        nm_ref[...] = nm
        nv_ref[...] = nv

    spec = pl.BlockSpec((tr, C), lambda i: (i, 0))
    out = jax.ShapeDtypeStruct((R, C), F32)
    return pl.pallas_call(body, name=name, grid=(R // tr,), in_specs=[spec] * 4, out_specs=[spec] * 3,
                          out_shape=[out, out, out], compiler_params=_cparams("parallel"))(w, g, m, v)


def _sum_slots(x, name):
    n, R, C = x.shape
    tr = R if R <= 256 else 256
    assert R % tr == 0

    def body(x_ref, o_ref):
        acc = x_ref[0].astype(F32)
        for j in range(1, n):
            acc = acc + x_ref[j].astype(F32)
        o_ref[...] = acc

    return pl.pallas_call(body, name=name, grid=(R // tr,), in_specs=[pl.BlockSpec((n, tr, C), lambda i: (0, i, 0))],
                          out_specs=pl.BlockSpec((tr, C), lambda i: (i, 0)), out_shape=jax.ShapeDtypeStruct((R, C), F32),
                          compiler_params=_cparams("parallel"))(x)


def _ew(fn, name, tiled, consts=(), outs=(), sums=(), ts=ROWS):
    tiled = [t if isinstance(t, tuple) else (t, t.shape[1], 0) for t in tiled]
    S = tiled[0][0].shape[0]
    n_t, n_c, n_o, n_s = len(tiled), len(consts), len(outs), len(sums)

    def body(*refs):
        ins = [r[...] for r in refs[:n_t + n_c]]
        res = fn(*ins)
        res = res if isinstance(res, (tuple, list)) else (res,)
        assert len(res) == n_o + n_s
        o_refs = refs[n_t + n_c:]
        for r, val in zip(o_refs[:n_o], res[:n_o]):
            r[...] = val.astype(r.dtype)
        first = pl.program_id(0) == 0
        for r, val in zip(o_refs[n_o:], res[n_o:]):
            @pl.when(first)
            def _(r=r, val=val):
                r[...] = val

            @pl.when(jnp.logical_not(first))
            def _(r=r, val=val):
                r[...] += val

    in_specs = [pl.BlockSpec((ts, w), lambda i, cb=cb: (i, cb)) for _, w, cb in tiled]
    in_specs += [pl.BlockSpec(c.shape, lambda i, nd=c.ndim: (0,) * nd) for c in consts]
    out_specs = [_row_spec(w, ts) for w, _ in outs] + [_vec_spec(w) for w in sums]
    out_shape = [jax.ShapeDtypeStruct((S, w), dt) for w, dt in outs] + [jax.ShapeDtypeStruct((1, w), F32) for w in sums]
    res = pl.pallas_call(body, name=name, grid=(S // ts,), in_specs=in_specs, out_specs=out_specs, out_shape=out_shape,
                         compiler_params=_cparams("arbitrary" if sums else "parallel"))(*[t[0] for t in tiled], *consts)
    return res


_GELU_C = math.sqrt(2.0 / math.pi)


def _gelu(x):
    return 0.5 * x * (1.0 + jnp.tanh(_GELU_C * (x + 0.044715 * x * x * x)))


def _dgelu(x):
    t = jnp.tanh(_GELU_C * (x + 0.044715 * x * x * x))
    return 0.5 * (1.0 + t) + 0.5 * x * (1.0 - t * t) * _GELU_C * (1.0 + 3.0 * 0.044715 * x * x)


def _sigmoid(x):
    return 1.0 / (1.0 + jnp.exp(-x))


def _log_sigmoid(x):
    return jnp.minimum(x, 0.0) - jnp.log(1.0 + jnp.exp(-jnp.abs(x)))


SCAN_T = 128
SCAN_TB = 512


def _cmul(ar, ai, br, bi):
    return ar * br - ai * bi, ar * bi + ai * br


def _s5_discretise(lam_re, lam_im, log_dt, b_re, b_im):
    dt = jnp.exp(log_dt)[:, None]
    mag = jnp.exp(lam_re * dt)
    ang = lam_im * dt
    abar_re = mag * jnp.cos(ang)
    abar_im = mag * jnp.sin(ang)
    den = lam_re * lam_re + lam_im * lam_im
    coef_re = ((abar_re - 1.0) * lam_re + abar_im * lam_im) / den
    coef_im = (abar_im * lam_re - (abar_re - 1.0) * lam_im) / den
    bbar_re = coef_re[..., None] * b_re - coef_im[..., None] * b_im
    bbar_im = coef_re[..., None] * b_im + coef_im[..., None] * b_re
    return abar_re, abar_im, bbar_re, bbar_im


def _planes(re, im):
    lead = re.shape[:-1]
    return jnp.stack([re.reshape(lead + (-1, LANES)), im.reshape(lead + (-1, LANES))], axis=-2).reshape(lead + (-1,))


def _unplanes(x):
    lead = x.shape[:-1]
    x4 = x.reshape(lead + (-1, 2, LANES))
    return x4[..., 0, :].reshape(lead + (-1,)), x4[..., 1, :].reshape(lead + (-1,))


def _s5_scan_tables(a_re, a_im, reverse):
    pr, pi = [a_re], [a_im]
    for _ in range(7):
        r, i = _cmul(pr[-1], pi[-1], pr[-1], pi[-1])
        pr.append(r)
        pi.append(i)
    apow = _planes(jnp.stack(pr), jnp.stack(pi))
    n = np.arange(1, SCAN_T + 1)
    if reverse:
        n = n[::-1]
    tr = jnp.ones((SCAN_T, a_re.shape[0]), F32)
    ti = jnp.zeros((SCAN_T, a_re.shape[0]), F32)
    for k in range(8):
        bit = jnp.asarray(((n >> k) & 1).astype(np.float32))[:, None]
        mr = bit * pr[k][None, :] + (1.0 - bit)
        mi = bit * pi[k][None, :]
        tr, ti = _cmul(tr, ti, mr, mi)
    return apow, _planes(tr, ti)


def _s5_scan(bu, apow, ptab, name, reverse, x_fwd=None):
    S, N2 = bu.shape
    T, W = SCAN_T, 2 * LANES
    tb = min(SCAN_TB, S)
    nt, nsub = S // tb, tb // T
    order = list(range(nsub - 1, -1, -1) if reverse else range(nsub))
    with_da = x_fwd is not None

    def tblk(t):
        return (nt - 1 - t) if reverse else t

    def shifted(v, k, rowi):
        s = 1 << k
        if reverse:
            return jnp.where(rowi < T - s, pltpu.roll(v, T - s, 0), 0.0)
        return jnp.where(rowi >= s, pltpu.roll(v, s, 0), 0.0)

    def body(*refs):
        if with_da:
            bu_ref, ap_ref, pt_ref, xf_ref, xp_ref, x_ref, da_ref, carry_ref = refs
        else:
            bu_ref, ap_ref, pt_ref, x_ref, carry_ref = refs
        t = pl.program_id(1)

        @pl.when(t == 0)
        def _():
            carry_ref[...] = jnp.zeros_like(carry_ref)
            if with_da:
                da_ref[...] = jnp.zeros_like(da_ref)

        rowi = lax.broadcasted_iota(jnp.int32, (T, LANES), 0)
        pr, pi = pt_ref[:, :LANES], pt_ref[:, LANES:]
        cr, ci = carry_ref[0:1, :LANES], carry_ref[0:1, LANES:]
        for sb in order:
            rows = pl.ds(sb * T, T)
            xr, xi = bu_ref[rows, :LANES], bu_ref[rows, LANES:]
            for k in range(7):
                ar, ai = ap_ref[k:k + 1, :LANES], ap_ref[k:k + 1, LANES:]
                s = 1 << k
                if s < 8:
                    rr, ri = shifted(xr, k, rowi), shifted(xi, k, rowi)
                    xr, xi = xr + ar * rr - ai * ri, xi + ar * ri + ai * rr
                elif reverse:
                    nr, ni = xr[s:], xi[s:]
                    xr = jnp.concatenate([xr[:T - s] + ar * nr - ai * ni, xr[T - s:]], axis=0)
                    xi = jnp.concatenate([xi[:T - s] + ar * ni + ai * nr, xi[T - s:]], axis=0)
                else:
                    nr, ni = xr[:T - s], xi[:T - s]
                    xr = jnp.concatenate([xr[:s], xr[s:] + ar * nr - ai * ni], axis=0)
                    xi = jnp.concatenate([xi[:s], xi[s:] + ar * ni + ai * nr], axis=0)
            xr, xi = xr + pr * cr - pi * ci, xi + pr * ci + pi * cr
            x_ref[rows, :LANES] = xr
            x_ref[rows, LANES:] = xi
            edge = pl.ds(sb * T + (0 if reverse else T - 1), 1)
            cr, ci = x_ref[edge, :LANES], x_ref[edge, LANES:]
            if with_da:
                if sb > 0:
                    before = pl.ds(sb * T - 1, 1)
                    b_r, b_i = xf_ref[before, :LANES], xf_ref[before, LANES:]
                else:
                    keep = (tblk(t) > 0).astype(F32)
                    b_r, b_i = xp_ref[7:8, :LANES] * keep, xp_ref[7:8, LANES:] * keep
                fr, fi = xf_ref[rows, :LANES], xf_ref[rows, LANES:]
                qr = jnp.where(rowi >= 1, pltpu.roll(fr, 1, 0), b_r)
                qi = jnp.where(rowi >= 1, pltpu.roll(fi, 1, 0), b_i)
                gr, gi = xr * qr + xi * qi, xi * qr - xr * qi
                sr, si = gr[0:8], gi[0:8]
                for j in range(1, T // 8):
                    sr, si = sr + gr[8 * j:8 * j + 8], si + gi[8 * j:8 * j + 8]
                da_ref[:, :LANES] += sr
                da_ref[:, LANES:] += si
        carry_ref[0:1, :LANES] = cr
        carry_ref[0:1, LANES:] = ci

    blk = pl.BlockSpec((tb, W), lambda j, t: (tblk(t), j))
    in_specs = [blk, pl.BlockSpec((8, W), lambda j, t: (0, j)), pl.BlockSpec((T, W), lambda j, t: (0, j))]
    out_specs, out_shape = [blk], [jax.ShapeDtypeStruct((S, N2), F32)]
    args = [bu, apow, ptab]
    if with_da:
        in_specs += [blk, pl.BlockSpec((8, W), lambda j, t: (jnp.maximum(tblk(t) * (tb // 8) - 1, 0), j))]
        out_specs.append(pl.BlockSpec((8, W), lambda j, t: (0, j)))
        out_shape.append(jax.ShapeDtypeStruct((8, N2), F32))
        args += [x_fwd, x_fwd]
    res = pl.pallas_call(body, name=name, grid=(N2 // W, nt), in_specs=in_specs, out_specs=out_specs, out_shape=out_shape,
                         scratch_shapes=[pltpu.VMEM((8, W), F32)], compiler_params=_cparams("parallel", "arbitrary"))(*args)
    return res if with_da else res[0]


S5_BAND = 4


def _mm_band(a, b, name, *, b_t=False, outer=False, epi=None, extras=(), tm=512, tk=2048):
    S = a.shape[0]
    wa = a.shape[1] // S5_BAND
    if outer:
        wb = b.shape[1] // S5_BAND
        tk = _tile(S, tk)
        nk = S // tk

        def obody(a_ref, b_ref, o_ref, acc_ref):
            k = pl.program_id(1)
            part = lax.dot_general(a_ref[...].astype(BF16), b_ref[...].astype(BF16), TN_DIMS, preferred_element_type=F32)

            @pl.when(k == 0)
            def _():
                acc_ref[...] = part

            @pl.when(k > 0)
            def _():
                acc_ref[...] += part

            @pl.when(k == nk - 1)
            def _():
                o_ref[...] = acc_ref[...]

        return pl.pallas_call(
            obody, name=name, grid=(S5_BAND, nk),
            in_specs=[pl.BlockSpec((tk, wa), lambda c, k: (k, c)), pl.BlockSpec((tk, wb), lambda c, k: (k, c))],
            out_specs=pl.BlockSpec((wa, wb), lambda c, k: (c, 0)), out_shape=jax.ShapeDtypeStruct((a.shape[1], wb), F32),
            scratch_shapes=[pltpu.VMEM((wa, wb), F32)], compiler_params=_cparams("parallel", "arbitrary"))(a, b)

    wo = (b.shape[0] if b_t else b.shape[1]) // S5_BAND
    tm = _tile(S, tm)
    ex_specs = [pl.BlockSpec((tm, wo), lambda i, c: (i, c)) if kind == "mn" else pl.BlockSpec((1, wo), lambda i, c: (0, c))
                for _, kind in extras]

    def body(a_ref, b_ref, *refs):
        part = lax.dot_general(a_ref[...].astype(BF16), b_ref[...].astype(BF16), NT_DIMS if b_t else _DIMS["nn"], preferred_element_type=F32)
        if epi is not None:
            part = epi(part, *[r[...] for r in refs[:-1]])
        refs[-1][...] = part

    b_spec = pl.BlockSpec((wo, wa) if b_t else (wa, wo), lambda i, c: (c, c))
    return pl.pallas_call(
        body, name=name, grid=(S // tm, S5_BAND), in_specs=[pl.BlockSpec((tm, wa), lambda i, c: (i, c)), b_spec] + ex_specs,
        out_specs=pl.BlockSpec((tm, wo), lambda i, c: (i, c)), out_shape=jax.ShapeDtypeStruct((S, S5_BAND * wo), F32),
        compiler_params=_cparams("parallel", "parallel"))(a, b, *[e[0] for e in extras])


def _band_to_full(blocks, cols):
    wa, wb = blocks.shape[0] // S5_BAND, blocks.shape[1]
    return jnp.concatenate([jnp.pad(blocks[k * wa:(k + 1) * wa], ((0, 0), (k * wb, cols - (k + 1) * wb))) for k in range(S5_BAND)], axis=0)


def _block_diag(t):
    G, a, b = t.shape
    return (t[:, :, None, :] * jnp.eye(G, dtype=t.dtype)[:, None, :, None]).reshape(G * a, G * b)


def _block_diag_take(m, G):
    a, b = m.shape[0] // G, m.shape[1] // G
    m4 = m.reshape(G, a, G, b)
    return jnp.sum(m4 * jnp.eye(G, dtype=m.dtype)[:, None, :, None], axis=2)


def _s5_block_fwd(u, w, pfx):
    a_re, a_im, bb_re, bb_im = _s5_discretise(w["lam_re"], w["lam_im"], w["log_dt"], w["b_re"], w["b_im"])
    bcat = _planes(_block_diag(bb_re).T, _block_diag(bb_im).T).astype(BF16)
    ccat = _planes(_block_diag(jnp.swapaxes(w["c_re"], 1, 2)).T, -_block_diag(jnp.swapaxes(w["c_im"], 1, 2)).T).T.astype(BF16)
    af_re, af_im = a_re.reshape(-1), a_im.reshape(-1)
    apow, ptab = _s5_scan_tables(af_re, af_im, False)
    bu = _mm_band(u, bcat, pfx + "_bu")
    x = _s5_scan(bu, apow, ptab, pfx + "_scan", False)
    d_row = w["d"].reshape(1, MIX_HALF)
    ys = _mm_band(x, ccat, pfx + "_y", epi=lambda acc, ut, dr: acc + dr * ut, extras=[(u, "mn"), (d_row, "n")])
    z = _mm(ys, w["w_glu"], "nn", pfx + "_glu", a_pro=_gelu, epi=lambda acc, b: acc + b, extras=[(w["b_glu"].reshape(1, -1), "n")])
    y2, = _ew(lambda ysv, zv: _gelu(ysv) * _sigmoid(zv), pfx + "_gate", [ys, z], outs=[(MIX_HALF, F32)])
    return y2, dict(u=u, x=x, ys=ys, z=z, bcat=bcat, ccat=ccat, a=(af_re, af_im), d_row=d_row)


def _s5_block_bwd(dy2, w, res, pfx):
    u, x, ys, z, bcat, ccat = res["u"], res["x"], res["ys"], res["z"], res["bcat"], res["ccat"]

    def gate_bwd(dy, ysv, zv):
        sg = _sigmoid(zv)
        dz = dy * _gelu(ysv) * sg * (1.0 - sg)
        return dz, jnp.sum(dz, axis=0, keepdims=True)

    dz, db_glu = _ew(gate_bwd, pfx + "_gate_bwd", [dy2, ys, z], outs=[(MIX_HALF, F32)], sums=[MIX_HALF])
    dw_glu = _mm(ys, dz, "tn", pfx + "_dwglu", a_pro=_gelu)
    dys = _mm(dz, w["w_glu"], "nt", pfx + "_dys", epi=lambda acc, dy, zv, ysv: (acc + dy * _sigmoid(zv)) * _dgelu(ysv),
              extras=[(dy2, "mn"), (z, "mn"), (ys, "mn")])
    dd, = _ew(lambda a, b: jnp.sum(a * b, axis=0, keepdims=True), pfx + "_dd", [dys, u], sums=[MIX_HALF])
    dccat = _band_to_full(_mm_band(x, dys, pfx + "_dc", outer=True), MIX_HALF)
    dx = _mm_band(dys, ccat, pfx + "_dx", b_t=True)
    af_re, af_im = res["a"]
    apow, ptab = _s5_scan_tables(af_re, -af_im, True)
    lam, da8 = _s5_scan(dx, apow, ptab, pfx + "_scan_bwd", True, x_fwd=x)
    dbcat = _band_to_full(_mm_band(u, lam, pfx + "_db", outer=True), 2 * S5_N)
    du = _mm_band(lam, bcat, pfx + "_du", b_t=True, epi=lambda acc, dyv, dr: acc + dyv * dr, extras=[(dys, "mn"), (res["d_row"], "n")])
    G = S5_GROUPS
    d_abar_re, d_abar_im = (t.reshape(G, S5_STATE) for t in _unplanes(jnp.sum(da8, axis=0)))
    d_bb_re, d_bb_im = (_block_diag_take(t.T, G) for t in _unplanes(dbcat))
    _, vjp = jax.vjp(_s5_discretise, w["lam_re"], w["lam_im"], w["log_dt"], w["b_re"], w["b_im"])
    g_lam_re, g_lam_im, g_log_dt, g_b_re, g_b_im = vjp((d_abar_re, d_abar_im, d_bb_re, d_bb_im))
    dc_re, dc_im = _unplanes(dccat.T)
    g_c_re = jnp.swapaxes(_block_diag_take(dc_re.T, G), 1, 2)
    g_c_im = -jnp.swapaxes(_block_diag_take(dc_im.T, G), 1, 2)
    grads = dict(lam_re=g_lam_re, lam_im=g_lam_im, log_dt=g_log_dt, b_re=g_b_re, b_im=g_b_im, c_re=g_c_re, c_im=g_c_im,
                 d=dd.reshape(G, S5_GROUP_WIDTH), w_glu=dw_glu, b_glu=db_glu.reshape(-1))
    return du, grads


SGU_TS = 512
N_PAIRS = MIX_HALF // LANES


def _half_masks(rows):
    lane = lax.broadcasted_iota(jnp.int32, (rows, LANES), 1)
    left = (lane < HEAD_DIM).astype(F32)
    return left, 1.0 - left


def _sgu_norm(zv, gain, bias):
    v = _gelu(zv)
    mu = jnp.mean(v, axis=-1, keepdims=True)
    vc = v - mu
    rstd = lax.rsqrt(jnp.mean(vc * vc, axis=-1, keepdims=True) + EPS)
    vhat = vc * rstd
    return vhat, rstd, vhat * gain + bias


def _sgu_tables(w_s, b_s):
    mask = jnp.tril(jnp.ones((SGU_CHUNK, SGU_CHUNK), dtype=bool))
    wm = jnp.where(mask[None], w_s, 0.0).astype(BF16)
    bias_tab = jnp.repeat(b_s.T, MIX_HALF // SGU_GROUPS, axis=1)
    return wm, bias_tab


def _sgu_fwd(proj, ln_gain, ln_bias, wm, bias_tab, name):
    S = proj.shape[0]
    nch = SGU_TS // SGU_CHUNK

    def body(zu_ref, zv_ref, g_ref, b_ref, w_ref, bt_ref, o_ref):
        left, right = _half_masks(SGU_CHUNK)
        _, _, vn = _sgu_norm(zv_ref[...], g_ref[...], b_ref[...])
        for ch in range(nch):
            rows = pl.ds(ch * SGU_CHUNK, SGU_CHUNK)
            for p in range(N_PAIRS):
                cols = pl.ds(p * LANES, LANES)
                vp = vn[ch * SGU_CHUNK:(ch + 1) * SGU_CHUNK, p * LANES:(p + 1) * LANES]
                mixed = (jnp.dot(w_ref[2 * p], (vp * left).astype(BF16), preferred_element_type=F32)
                         + jnp.dot(w_ref[2 * p + 1], (vp * right).astype(BF16), preferred_element_type=F32) + bt_ref[:, cols])
                o_ref[rows, cols] = _gelu(zu_ref[rows, cols]) * mixed

    vec = _vec_spec(MIX_HALF)
    return pl.pallas_call(
        body, name=name, grid=(S // SGU_TS,),
        in_specs=[pl.BlockSpec((SGU_TS, MIX_HALF), lambda i: (i, 1)), pl.BlockSpec((SGU_TS, MIX_HALF), lambda i: (i, 2)), vec, vec,
                  pl.BlockSpec((SGU_GROUPS, SGU_CHUNK, SGU_CHUNK), lambda i: (0, 0, 0)), pl.BlockSpec((SGU_CHUNK, MIX_HALF), lambda i: (0, 0))],
        out_specs=_row_spec(MIX_HALF, SGU_TS), out_shape=jax.ShapeDtypeStruct((S, MIX_HALF), F32),
        compiler_params=_cparams("parallel"))(proj, proj, ln_gain, ln_bias, wm, bias_tab)


def _sgu_bwd(dout, proj, ln_gain, ln_bias, wm, bias_tab, name):
    S = proj.shape[0]
    nch = SGU_TS // SGU_CHUNK
    nt_dims = (((1,), (1,)), ((), ()))
    tn_dims = (((0,), (0,)), ((), ()))

    def body(do_ref, zu_ref, zv_ref, g_ref, b_ref, w_ref, bt_ref, dzu_ref, dzv_ref, dw_ref, dbt_ref, dg_ref, db_ref, dvn_ref):
        first = pl.program_id(0) == 0

        @pl.when(first)
        def _():
            dw_ref[...] = jnp.zeros_like(dw_ref)
            dbt_ref[...] = jnp.zeros_like(dbt_ref)
            dg_ref[...] = jnp.zeros_like(dg_ref)
            db_ref[...] = jnp.zeros_like(db_ref)

        left, right = _half_masks(SGU_CHUNK)
        zv = zv_ref[...]
        vhat, rstd, vn = _sgu_norm(zv, g_ref[...], b_ref[...])
        for ch in range(nch):
            rows = pl.ds(ch * SGU_CHUNK, SGU_CHUNK)
            for p in range(N_PAIRS):
                cols = pl.ds(p * LANES, LANES)
                vp = vn[ch * SGU_CHUNK:(ch + 1) * SGU_CHUNK, p * LANES:(p + 1) * LANES]
                vl, vr = (vp * left).astype(BF16), (vp * right).astype(BF16)
                mixed = (jnp.dot(w_ref[2 * p], vl, preferred_element_type=F32)
                         + jnp.dot(w_ref[2 * p + 1], vr, preferred_element_type=F32) + bt_ref[:, cols])
                zu = zu_ref[rows, cols]
                do = do_ref[rows, cols]
                dzu_ref[rows, cols] = do * mixed * _dgelu(zu)
                dmix = do * _gelu(zu)
                dbt_ref[:, cols] += dmix
                dl, dr = (dmix * left).astype(BF16), (dmix * right).astype(BF16)
                dw_ref[2 * p] += lax.dot_general(dl, vl, nt_dims, preferred_element_type=F32)
                dw_ref[2 * p + 1] += lax.dot_general(dr, vr, nt_dims, preferred_element_type=F32)
                dvn_ref[rows, cols] = (lax.dot_general(w_ref[2 * p], dl, tn_dims, preferred_element_type=F32)
                                       + lax.dot_general(w_ref[2 * p + 1], dr, tn_dims, preferred_element_type=F32))
        dvn = dvn_ref[...]
        dg_ref[...] += jnp.sum(dvn * vhat, axis=0, keepdims=True)
        db_ref[...] += jnp.sum(dvn, axis=0, keepdims=True)
        dvh = dvn * g_ref[...]
        dv = rstd * (dvh - jnp.mean(dvh, axis=-1, keepdims=True) - vhat * jnp.mean(dvh * vhat, axis=-1, keepdims=True))
        dzv_ref[...] = dv * _dgelu(zv)

    vec = _vec_spec(MIX_HALF)
    row = _row_spec(MIX_HALF, SGU_TS)
    wspec = pl.BlockSpec((SGU_GROUPS, SGU_CHUNK, SGU_CHUNK), lambda i: (0, 0, 0))
    tspec = pl.BlockSpec((SGU_CHUNK, MIX_HALF), lambda i: (0, 0))
    full = jax.ShapeDtypeStruct((S, MIX_HALF), F32)
    v = jax.ShapeDtypeStruct((1, MIX_HALF), F32)
    return pl.pallas_call(
        body, name=name, grid=(S // SGU_TS,),
        in_specs=[row, pl.BlockSpec((SGU_TS, MIX_HALF), lambda i: (i, 1)), pl.BlockSpec((SGU_TS, MIX_HALF), lambda i: (i, 2)), vec, vec,
                  wspec, tspec],
        out_specs=[row, row, wspec, tspec, vec, vec],
        out_shape=[full, full, jax.ShapeDtypeStruct((SGU_GROUPS, SGU_CHUNK, SGU_CHUNK), F32),
                   jax.ShapeDtypeStruct((SGU_CHUNK, MIX_HALF), F32), v, v],
        scratch_shapes=[pltpu.VMEM((SGU_TS, MIX_HALF), F32)],
        compiler_params=_cparams("arbitrary"))(dout, proj, proj, ln_gain, ln_bias, wm, bias_tab)


def _sgu_grads(dw, dbias_tab):
    mask = jnp.tril(jnp.ones((SGU_CHUNK, SGU_CHUNK), dtype=bool))
    g_w = jnp.where(mask[None], dw, 0.0)
    g_b = dbias_tab.reshape(SGU_CHUNK, SGU_GROUPS, MIX_HALF // SGU_GROUPS).sum(axis=-1).T
    return g_w, g_b


def _head_avg_matrix(w):
    idx = np.arange(w) // HEAD_DIM
    return jnp.asarray((idx[:, None] == idx[None, :]).astype(np.float32) / HEAD_DIM, dtype=BF16)


def _head_mean(t, bavg):
    hi = t.astype(BF16)
    lo = (t - hi.astype(F32)).astype(BF16)
    return jnp.dot(hi, bavg, preferred_element_type=F32) + jnp.dot(lo, bavg, preferred_element_type=F32)


def _head_rms(t, bavg):
    r = lax.rsqrt(_head_mean(t * t, bavg) + EPS)
    return t * r, r


def _head_rms_bwd(dn, n, r, bavg):
    return r * (dn - n * _head_mean(dn * n, bavg))


GLA_TS = 512
C = GLA_CHUNK
NT_DIMS = (((1,), (1,)), ((), ()))
TN_DIMS = (((0,), (0,)), ((), ()))
HI = lax.Precision.HIGHEST


def _bdot(a, b, dims=(((1,), (0,)), ((), ()))):
    return lax.dot_general(a.astype(BF16), b.astype(BF16), dims, preferred_element_type=F32)


def _gla_chunk_terms(q, k, z):
    row = lax.broadcasted_iota(jnp.int32, (C, C), 0)
    col = lax.broadcasted_iota(jnp.int32, (C, C), 1)
    lc = _log_sigmoid(z) * (1.0 / GLA_TAU)
    b = lax.dot_general((row >= col).astype(F32), lc, (((1,), (0,)), ((), ())), precision=HI, preferred_element_type=F32)
    b_last = jnp.sum(lc, axis=0, keepdims=True)
    b_mid = b[C // 2:C // 2 + 1, :]
    scale = HEAD_DIM ** -0.5
    e_b, e_q, e_k, e_l = jnp.exp(b), jnp.exp(b - b_mid), jnp.exp(b_mid - b), jnp.exp(b_last - b)
    qs = q * (scale * e_b)
    qe = q * (scale * e_q)
    ke = k * e_k
    kl = k * e_l
    return dict(e_b=e_b, e_q=e_q, e_k=e_k, e_l=e_l, qs=qs, qe=qe, ke=ke, kl=kl, dec=jnp.exp(b_last), causal=row >= col, scale=scale)


def _pair(x, pp):
    return x[:, pp * LANES:(pp + 1) * LANES]


def _pair_block_diag():
    r = lax.broadcasted_iota(jnp.int32, (LANES, LANES), 0) // HEAD_DIM
    c = lax.broadcasted_iota(jnp.int32, (LANES, LANES), 1) // HEAD_DIM
    return (r == c).astype(F32)


def _gla_fwd(proj, z, name):
    S = proj.shape[0]
    nch = GLA_TS // C

    def body(q_ref, k_ref, v_ref, z_ref, o_ref, st_ref, state_ref):
        @pl.when(pl.program_id(0) == 0)
        def _():
            state_ref[...] = jnp.zeros_like(state_ref)

        left, right = _half_masks(C)
        bd = _pair_block_diag()
        pairs = range(N_PAIRS)
        for ch in range(nch):
            rows = pl.ds(ch * C, C)
            v = v_ref[rows, :]
            t = _gla_chunk_terms(q_ref[rows, :], k_ref[rows, :], z_ref[rows, :])
            sts = [state_ref[pp] for pp in pairs]
            for pp in pairs:
                st_ref[ch, pp] = sts[pp]
            os = [_bdot(_pair(t["qs"], pp), sts[pp], NT_DIMS) for pp in pairs]
            for m in (left, right):
                scores = [jnp.where(t["causal"], _bdot(_pair(t["qe"], pp) * m, _pair(t["ke"], pp), NT_DIMS), 0.0) for pp in pairs]
                os = [os[pp] + m * _bdot(scores[pp], _pair(v, pp)) for pp in pairs]
            o_ref[rows, :] = jnp.concatenate(os, axis=1)
            new = [sts[pp] * _pair(t["dec"], pp) + bd * _bdot(_pair(v, pp), _pair(t["kl"], pp), TN_DIMS) for pp in pairs]
            for pp in pairs:
                state_ref[pp] = new[pp]

    def col(cb):
        return pl.BlockSpec((GLA_TS, MIX_HALF), lambda i: (i, cb))

    return pl.pallas_call(
        body, name=name, grid=(S // GLA_TS,),
        in_specs=[col(0), col(1), col(2), col(0)],
        out_specs=[col(0), pl.BlockSpec((nch, N_PAIRS, LANES, LANES), lambda i: (i, 0, 0, 0))],
        out_shape=[jax.ShapeDtypeStruct((S, MIX_HALF), F32), jax.ShapeDtypeStruct((S // C, N_PAIRS, LANES, LANES), F32)],
        scratch_shapes=[pltpu.VMEM((N_PAIRS, LANES, LANES), F32)], compiler_params=_cparams("arbitrary"))(proj, proj, proj, z)


def _gla_bwd(do, proj, z, states, name):
    S = proj.shape[0]
    nch = GLA_TS // C
    nblk = S // GLA_TS

    def body(do_ref, q_ref, k_ref, v_ref, z_ref, st_ref, dq_ref, dk_ref, dv_ref, dlc_ref, dstate_ref):
        @pl.when(pl.program_id(0) == 0)
        def _():
            dstate_ref[...] = jnp.zeros_like(dstate_ref)

        left, right = _half_masks(C)
        bd = _pair_block_diag()
        rowi = lax.broadcasted_iota(jnp.int32, (C, LANES), 0)
        row = lax.broadcasted_iota(jnp.int32, (C, C), 0)
        colm = lax.broadcasted_iota(jnp.int32, (C, C), 1)
        pairs = range(N_PAIRS)
        rowi = lax.broadcasted_iota(jnp.int32, (C, MIX_HALF), 0)
        for ch in range(nch - 1, -1, -1):
            rows = pl.ds(ch * C, C)
            v, dov = v_ref[rows, :], do_ref[rows, :]
            t = _gla_chunk_terms(q_ref[rows, :], k_ref[rows, :], z_ref[rows, :])
            sts = [st_ref[ch, pp] for pp in pairs]
            nxt = [dstate_ref[pp] for pp in pairs]
            gs = [bd * nxt[pp] for pp in pairs]
            dqs = [_bdot(_pair(dov, pp), sts[pp]) for pp in pairs]
            dv = [_bdot(_pair(t["kl"], pp), gs[pp], NT_DIMS) for pp in pairs]
            dkl = [_bdot(_pair(v, pp), gs[pp]) for pp in pairs]
            dqe = [jnp.zeros((C, LANES), F32) for _ in pairs]
            dke = [jnp.zeros((C, LANES), F32) for _ in pairs]
            for m in (left, right):
                sc = [jnp.where(t["causal"], _bdot(_pair(t["qe"], pp) * m, _pair(t["ke"], pp), NT_DIMS), 0.0) for pp in pairs]
                dsc = [jnp.where(t["causal"], _bdot(_pair(dov, pp) * m, _pair(v, pp), NT_DIMS), 0.0) for pp in pairs]
                dv = [dv[pp] + m * _bdot(sc[pp], _pair(dov, pp), TN_DIMS) for pp in pairs]
                dqe = [dqe[pp] + m * _bdot(dsc[pp], _pair(t["ke"], pp)) for pp in pairs]
                dke = [dke[pp] + m * _bdot(dsc[pp], _pair(t["qe"], pp), TN_DIMS) for pp in pairs]
            for pp in pairs:
                dstate_ref[pp] = bd * (nxt[pp] * _pair(t["dec"], pp) + _bdot(_pair(dov, pp), _pair(t["qs"], pp), TN_DIMS))
            decay_sum = jnp.concatenate([jnp.sum(nxt[pp] * sts[pp], axis=0, keepdims=True) for pp in pairs], axis=1)
            dqs, dv, dkl, dqe, dke = (jnp.concatenate(parts, axis=1) for parts in (dqs, dv, dkl, dqe, dke))
            db_last = decay_sum * t["dec"] + jnp.sum(dkl * t["kl"], axis=0, keepdims=True)
            db = dqs * t["qs"] + dqe * t["qe"] - dke * t["ke"] - dkl * t["kl"]
            db = db + jnp.where(rowi == C - 1, db_last, 0.0)
            dq_ref[rows, :] = (dqs * t["e_b"] + dqe * t["e_q"]) * t["scale"]
            dk_ref[rows, :] = dke * t["e_k"] + dkl * t["e_l"]
            dv_ref[rows, :] = dv
            dlc_ref[rows, :] = lax.dot_general((colm >= row).astype(F32), db, (((1,), (0,)), ((), ())), precision=HI,
                                               preferred_element_type=F32)

    def col(cb):
        return pl.BlockSpec((GLA_TS, MIX_HALF), lambda i: (nblk - 1 - i, cb))

    full = jax.ShapeDtypeStruct((S, MIX_HALF), F32)
    return pl.pallas_call(
        body, name=name, grid=(nblk,),
        in_specs=[col(0), col(0), col(1), col(2), col(0), pl.BlockSpec((nch, N_PAIRS, LANES, LANES), lambda i: (nblk - 1 - i, 0, 0, 0))],
        out_specs=[col(0)] * 4, out_shape=[full, full, full, full],
        scratch_shapes=[pltpu.VMEM((N_PAIRS, LANES, LANES), F32)], compiler_params=_cparams("arbitrary"))(do, proj, proj, proj, z, states)


def _gla_block_fwd(proj, w_lr_pad, b_lr, gain, bavg, pfx):
    z = _mm(proj, w_lr_pad, "nn", pfx + "_z", a_cols=(7 * MIX_HALF, MIX_HALF), epi=lambda acc, b: acc + b, extras=[(b_lr, "n")])
    o, states = _gla_fwd(proj, z, pfx + "_core")

    def out(ov, gg, ba, gn):
        n, _ = _head_rms(ov, ba)
        return n * gn * (gg * _sigmoid(gg))

    og, = _ew(out, pfx + "_out", [o, (proj, MIX_HALF, 3)], consts=[bavg, gain], outs=[(MIX_HALF, F32)])
    return og, dict(z=z, o=o, states=states)


def _gla_block_bwd(dog, proj, w_lr_pad, gain, bavg, res, pfx):
    z, o, states = res["z"], res["o"], res["states"]

    def out_bwd(dy, ov, gg, ba, gn):
        n, r = _head_rms(ov, ba)
        sg = _sigmoid(gg)
        silu = gg * sg
        dn = dy * gn * silu
        do = _head_rms_bwd(dn, n, r, ba)
        dgg = dy * n * gn * (sg * (1.0 + gg * (1.0 - sg)))
        return do, dgg, jnp.sum(dy * n * silu, axis=0, keepdims=True)

    do, dgg, dgain = _ew(out_bwd, pfx + "_out_bwd", [dog, o, (proj, MIX_HALF, 3)], consts=[bavg, gain],
                         outs=[(MIX_HALF, F32), (MIX_HALF, F32)], sums=[MIX_HALF])
    dq, dk, dv, dlc = _gla_bwd(do, proj, z, states, pfx + "_core_bwd")

    def decay_bwd(dl, zv):
        dz = dl * (1.0 / GLA_TAU) * (1.0 - _sigmoid(zv))
        return dz, jnp.sum(dz, axis=0, keepdims=True)

    dz, db_lr = _ew(decay_bwd, pfx + "_decay_bwd", [dlc, z], outs=[(MIX_HALF, F32)], sums=[MIX_HALF])
    dw_lr_pad = _mm(proj, dz, "tn", pfx + "_dwlr", a_cols=(7 * MIX_HALF, MIX_HALF))
    dsmall = _mm(dz, w_lr_pad, "nt", pfx + "_dsmall")
    return (dq, dk, dv, dgg, dsmall), dict(w_lr=dw_lr_pad[:GLA_RANK], b_lr=db_lr.reshape(-1), gain=dgain.reshape(-1, HEAD_DIM))


FOX_T = 512
FOX_HEADS = MIX_HALF // HEAD_DIM
NEG = -1e30
CUM_T = 512


def _cum_lanes(x, name, reverse, pre=None):
    R, S = x.shape
    nb = S // CUM_T

    def body(x_ref, o_ref, carry_ref):
        @pl.when(pl.program_id(0) == 0)
        def _():
            carry_ref[...] = jnp.zeros_like(carry_ref)

        xv = x_ref[...]
        if pre is not None:
            xv = pre(xv)
        i = lax.broadcasted_iota(jnp.int32, (CUM_T, CUM_T), 0)
        j = lax.broadcasted_iota(jnp.int32, (CUM_T, CUM_T), 1)
        tri = ((i >= j) if reverse else (i <= j)).astype(F32)
        c = lax.dot_general(xv, tri, (((1,), (0,)), ((), ())), precision=HI, preferred_element_type=F32)
        carry = carry_ref[...]
        o_ref[...] = c + carry[:, 0:1]
        carry_ref[...] = carry + jnp.sum(xv, axis=1, keepdims=True)

    spec = pl.BlockSpec((R, CUM_T), (lambda i: (0, nb - 1 - i)) if reverse else (lambda i: (0, i)))
    return pl.pallas_call(body, name=name, grid=(nb,), in_specs=[spec], out_specs=spec, out_shape=jax.ShapeDtypeStruct((R, S), F32),
                          scratch_shapes=[pltpu.VMEM((R, LANES), F32)], compiler_params=_cparams("arbitrary"))(x)


def _fox_scores(q, k, cqb, ck_ref, h, m, diag):
    cq = cqb[:, h * HEAD_DIM:h * HEAD_DIM + 1]
    ck = ck_ref[0, h:h + 1, :]
    s = lax.dot_general(q * m.astype(q.dtype), k, NT_DIMS, preferred_element_type=F32) + (cq - ck)
    if not diag:
        return s
    row = lax.broadcasted_iota(jnp.int32, (FOX_T, FOX_T), 0)
    col = lax.broadcasted_iota(jnp.int32, (FOX_T, FOX_T), 1)
    return jnp.where(row < col, NEG, s)


def _on_causal_blocks(q_blk, k_blk, step):
    @pl.when(k_blk < q_blk)
    def _():
        step(False)

    @pl.when(k_blk == q_blk)
    def _():
        step(True)


def _causal_pairs(n, key_major):
    if key_major:
        pairs = [(q, k) for k in range(n) for q in range(k, n)]
    else:
        pairs = [(q, k) for q in range(n) for k in range(q + 1)]
    return jnp.asarray([p[0] for p in pairs], jnp.int32), jnp.asarray([p[1] for p in pairs], jnp.int32)


def _carried(carry, refs, n_in, n_out, first, last):
    if carry is None:
        return refs
    ins, cx_ref, outs, co_ref = refs[:n_in], refs[n_in], refs[n_in + 1:n_in + 1 + n_out], refs[n_in + 1 + n_out]
    scratch = refs[n_in + 2 + n_out:]
    start, finish = _exchange_plan(cx_ref, co_ref, *scratch[-3:], carry[1])
    pl.when(first)(start)
    pl.when(last)(finish)
    return ins + outs + scratch[:-3]


def _carry_specs(carry):
    if carry is None:
        return [], [], [], [], []
    x, bcast = carry
    blk = x.shape if bcast else x.shape[1:]
    return [ANY], [ANY], [jax.ShapeDtypeStruct((N_CHIPS,) + tuple(blk), x.dtype)], list(_EXCHANGE_SEMS), [x]


def _fox_fwd(qn, kn, proj, cum_b, cum_tp, name, carry=None):
    S = qn.shape[0]
    nq = S // FOX_T
    qidx, kidx = _causal_pairs(nq, False)
    ntri = int(qidx.shape[0])

    def body(qidx_ref, kidx_ref, *refs):
        t = pl.program_id(1)
        first = jnp.logical_and(pl.program_id(0) == 0, t == 0)
        last = jnp.logical_and(pl.program_id(0) == N_PAIRS - 1, t == ntri - 1)
        q_ref, k_ref, v_ref, cq_ref, ck_ref, o_ref, lse_ref, m_scr, acc_scr = _carried(carry, refs, 5, 2, first, last)
        qi, ki = qidx_ref[t], kidx_ref[t]

        @pl.when(ki == 0)
        def _():
            m_scr[...] = jnp.full_like(m_scr, NEG)
            acc_scr[...] = jnp.zeros_like(acc_scr)

        left, right = _half_masks(FOX_T)

        def step(diag):
            q, k, v = q_ref[...], k_ref[...], v_ref[...].astype(BF16)
            cqb = cq_ref[...]
            for h, m in enumerate((left, right)):
                s = _fox_scores(q, k, cqb, ck_ref, h, m, diag)
                m_prev = m_scr[h]
                m_new = jnp.maximum(m_prev, jnp.max(s, axis=1, keepdims=True))
                p = jnp.exp(s - m_new)
                v_h = jnp.where(m > 0, v, jnp.ones_like(v))
                acc_scr[h] = jnp.exp(m_prev - m_new) * acc_scr[h] + jnp.dot(p.astype(BF16), v_h, preferred_element_type=F32)
                m_scr[h] = m_new

        _on_causal_blocks(qi, ki, step)

        @pl.when(ki == qi)
        def _():
            a0, a1 = acc_scr[0], acc_scr[1]
            is_left = left > 0
            num = jnp.where(is_left, a0, a1)
            den = jnp.where(is_left, pltpu.roll(a0, HEAD_DIM, 1), pltpu.roll(a1, HEAD_DIM, 1))
            o_ref[...] = num / den
            lse_ref[...] = jnp.where(is_left, m_scr[0], m_scr[1]) + jnp.log(den)

    qspec = pl.BlockSpec((FOX_T, LANES), lambda p, t, qx, kx: (qx[t], p))
    kspec = pl.BlockSpec((FOX_T, LANES), lambda p, t, qx, kx: (kx[t], p))
    vspec = pl.BlockSpec((FOX_T, LANES), lambda p, t, qx, kx: (kx[t], 6 * N_PAIRS + p))
    ckspec = pl.BlockSpec((1, 8, FOX_T), lambda p, t, qx, kx: (p, 0, kx[t]))
    full = jax.ShapeDtypeStruct((S, MIX_HALF), F32)
    c_in, c_out, c_shape, c_scratch, c_args = _carry_specs(carry)
    grid_spec = pltpu.PrefetchScalarGridSpec(
        num_scalar_prefetch=2, grid=(N_PAIRS, ntri), in_specs=[qspec, kspec, vspec, qspec, ckspec] + c_in, out_specs=[qspec, qspec] + c_out,
        scratch_shapes=[pltpu.VMEM((2, FOX_T, 1), F32), pltpu.VMEM((2, FOX_T, LANES), F32)] + c_scratch)
    return pl.pallas_call(body, name=name, grid_spec=grid_spec, out_shape=[full, full] + c_shape,
                          compiler_params=_cparams("arbitrary", "arbitrary"))(qidx, kidx, qn, kn, proj, cum_b, cum_tp, *c_args)


def _fox_bwd(do, qn, kn, proj, cum_b, cum_tp, lse_b, delta_b, name, carry=None, do_pair0=0):
    S = qn.shape[0]
    nq = S // FOX_T
    scale = HEAD_DIM ** -0.5
    qidx, kidx = _causal_pairs(nq, True)
    ntri = int(qidx.shape[0])

    def body(qidx_ref, kidx_ref, *refs):
        t = pl.program_id(1)
        first = jnp.logical_and(pl.program_id(0) == 0, t == 0)
        last = jnp.logical_and(pl.program_id(0) == N_PAIRS - 1, t == ntri - 1)
        (do_ref, q_ref, k_ref, v_ref, cq_ref, ck_ref, lse_ref, dl_ref, dq_ref, dcq_ref, dk_ref, dv_ref, dck_ref,
         dq_scr, dk_scr, dv_scr) = _carried(carry, refs, 8, 5, first, last)
        qi, ki = qidx_ref[t], kidx_ref[t]

        @pl.when(t == 0)
        def _():
            dq_scr[...] = jnp.zeros_like(dq_scr)

        @pl.when(qi == ki)
        def _():
            dk_scr[...] = jnp.zeros_like(dk_scr)
            dv_scr[...] = jnp.zeros_like(dv_scr)

        left, right = _half_masks(FOX_T)
        rows = pl.ds(pl.multiple_of(qi * FOX_T, FOX_T), FOX_T)

        def step(diag):
            q, k, v, dov = q_ref[...], k_ref[...], v_ref[...].astype(BF16), do_ref[...]
            cqb, lseb, dlb = cq_ref[...], lse_ref[...], dl_ref[...]
            dob = dov.astype(BF16)
            heads = (0, 1)
            masks = (left, right)
            col = [slice(h * HEAD_DIM, h * HEAD_DIM + 1) for h in heads]
            ss = [_fox_scores(q, k, cqb, ck_ref, h, masks[h], diag) for h in heads]
            dps = [lax.dot_general((dov * masks[h]).astype(BF16), v, NT_DIMS, preferred_element_type=F32) for h in heads]
            ps = [jnp.exp(ss[h] - lseb[:, col[h]]) for h in heads]
            dss = [(ps[h] * (dps[h] - dlb[:, col[h]])).astype(BF16) for h in heads]
            pvs = [lax.dot_general(ps[h].astype(BF16), dob, TN_DIMS, preferred_element_type=F32) for h in heads]
            dks = [lax.dot_general(dss[h], jnp.where(masks[h] > 0, q, jnp.ones_like(q)), TN_DIMS, preferred_element_type=F32) for h in heads]
            dqs = [jnp.dot(dss[h], jnp.where(masks[h] > 0, k, jnp.ones_like(k)), preferred_element_type=F32) for h in heads]
            dv_scr[...] = dv_scr[...] + left * pvs[0] + right * pvs[1]
            for h in heads:
                dk_scr[h] = dk_scr[h] + dks[h]
                dq_scr[h, rows, :] = dq_scr[h, rows, :] + dqs[h]

        _on_causal_blocks(qi, ki, step)

        @pl.when(qi == nq - 1)
        def _():
            a0, a1 = dk_scr[0], dk_scr[1]
            dk_ref[...] = left * a0 + right * a1
            dv_ref[...] = dv_scr[...]
            dck_ref[...] = left * pltpu.roll(a0, HEAD_DIM, 1) + right * pltpu.roll(a1, HEAD_DIM, 1)

        @pl.when(t == ntri - 1)
        def _():
            for r in range(nq):
                blk = pl.ds(r * FOX_T, FOX_T)
                a0, a1 = dq_scr[0, blk, :], dq_scr[1, blk, :]
                dq_ref[blk, :] = (left * a0 + right * a1) * scale
                dcq_ref[blk, :] = left * pltpu.roll(a0, HEAD_DIM, 1) + right * pltpu.roll(a1, HEAD_DIM, 1)

    qspec = pl.BlockSpec((FOX_T, LANES), lambda p, t, qx, kx: (qx[t], p))
    kspec = pl.BlockSpec((FOX_T, LANES), lambda p, t, qx, kx: (kx[t], p))
    vspec = pl.BlockSpec((FOX_T, LANES), lambda p, t, qx, kx: (kx[t], 6 * N_PAIRS + p))
    ckspec = pl.BlockSpec((1, 8, FOX_T), lambda p, t, qx, kx: (p, 0, kx[t]))
    seq = pl.BlockSpec((S, LANES), lambda p, t, qx, kx: (0, p))
    full = jax.ShapeDtypeStruct((S, MIX_HALF), F32)
    c_in, c_out, c_shape, c_scratch, c_args = _carry_specs(carry)
    grid_spec = pltpu.PrefetchScalarGridSpec(
        num_scalar_prefetch=2, grid=(N_PAIRS, ntri),
        in_specs=[pl.BlockSpec((FOX_T, LANES), lambda p, t, qx, kx: (qx[t], do_pair0 + p)), qspec, kspec, vspec, qspec, ckspec, qspec, qspec] + c_in,
        out_specs=[seq, seq, kspec, kspec, kspec] + c_out,
        scratch_shapes=[pltpu.VMEM((2, S, LANES), F32), pltpu.VMEM((2, FOX_T, LANES), F32), pltpu.VMEM((FOX_T, LANES), F32)] + c_scratch)
    return pl.pallas_call(body, name=name, grid_spec=grid_spec, out_shape=[full] * 5 + c_shape,
                          compiler_params=_cparams("arbitrary", "arbitrary"))(qidx, kidx, do, qn, kn, proj, cum_b, cum_tp, lse_b, delta_b, *c_args)


def _ff_bwd(rc, f_t, name):
    def body(rc_ref, f_ref, d_ref, s_ref):
        d = rc_ref[...] * (1.0 - _sigmoid(f_ref[...]))
        d_ref[...] = d
        s_ref[...] = jnp.sum(d, axis=1, keepdims=True)

    return pl.pallas_call(body, name=name, out_shape=[jax.ShapeDtypeStruct(rc.shape, F32), jax.ShapeDtypeStruct((rc.shape[0], 1), F32)])(rc, f_t)


def _fox_block_fwd(proj, b_f, q_gain, k_gain, bavg, pfx, carry=None):
    S = proj.shape[0]

    def prep(qv, kv, ba, qg, kg):
        return _head_rms(qv, ba)[0] * qg * (HEAD_DIM ** -0.5), _head_rms(kv, ba)[0] * kg

    qn, kn = _ew(prep, pfx + "_prep", [(proj, MIX_HALF, 4), (proj, MIX_HALF, 5)], consts=[bavg, q_gain, k_gain],
                 outs=[(MIX_HALF, BF16), (MIX_HALF, BF16)])
    f0 = 7 * MIX_HALF + GLA_RANK
    f_t = proj[:, f0:f0 + FOX_HEADS].T + b_f.reshape(FOX_HEADS, 1)
    cum = _cum_lanes(f_t, pfx + "_cum", False, pre=_log_sigmoid)
    cum_b = jnp.repeat(cum.T, HEAD_DIM, axis=1)
    cum_tp = jnp.pad(cum.reshape(N_PAIRS, 2, S), ((0, 0), (0, 6), (0, 0)))
    o, lse_b, *carried = _fox_fwd(qn, kn, proj, cum_b, cum_tp, pfx + "_attn", carry=carry)
    return o, dict(qn=qn, kn=kn, f_t=f_t, cum_b=cum_b, cum_tp=cum_tp, o=o, lse_b=lse_b), carried


def _fox_block_bwd(do, proj, q_gain, k_gain, bavg, res, pfx, carry=None):
    qn, kn, o = res["qn"], res["kn"], res["o"]
    S = proj.shape[0]
    delta_b, = _ew(lambda a, b, ba: _head_mean(a * b, ba) * float(HEAD_DIM), pfx + "_delta", [do, o], consts=[bavg], outs=[(MIX_HALF, F32)])
    do_arr, do_blk = (do[0], do[2]) if isinstance(do, tuple) else (do, 0)
    args = (do_arr, qn, kn, proj, res["cum_b"], res["cum_tp"], res["lse_b"], delta_b)
    dqn, dcq_b, dkn, dv, dck_b, *carried = _fox_bwd(*args, pfx + "_bwd", carry=carry, do_pair0=do_blk * N_PAIRS)

    def prep_bwd(dq, dk, qv, kv, ba, qg, kg):
        nq, rq = _head_rms(qv, ba)
        nk, rk = _head_rms(kv, ba)
        return (_head_rms_bwd(dq * qg, nq, rq, ba), _head_rms_bwd(dk * kg, nk, rk, ba),
                jnp.sum(dq * nq, axis=0, keepdims=True), jnp.sum(dk * nk, axis=0, keepdims=True))

    dfq, dfk, dqg, dkg = _ew(prep_bwd, pfx + "_prep_bwd", [dqn, dkn, (proj, MIX_HALF, 4), (proj, MIX_HALF, 5)],
                             consts=[bavg, q_gain, k_gain], outs=[(MIX_HALF, F32), (MIX_HALF, F32)], sums=[MIX_HALF, MIX_HALF])
    dcum = (dcq_b - dck_b)[:, ::HEAD_DIM].T
    rc = _cum_lanes(dcum, pfx + "_rcum", True)
    dff_t, db_f = _ff_bwd(rc, res["f_t"], pfx + "_ff_bwd")
    grads = dict(b_f=db_f.reshape(-1), q_gain=dqg.reshape(-1, HEAD_DIM), k_gain=dkg.reshape(-1, HEAD_DIM))
    return (dfq, dfk, dv, dff_t.T), grads, carried


WEIGHTS = ['ada_w', 'ada_b', 'even_w_in', 'even_w_out', 'gla_w_lr', 'gla_b_lr', 'gla_gain', 'fox_b_f', 'fox_q_gain', 'fox_k_gain',
           'odd_w_in', 'odd_w_out', 's5_lam_re', 's5_lam_im', 's5_log_dt', 's5_b_re', 's5_b_im', 's5_c_re', 's5_c_im', 's5_d',
           's5_w_glu', 's5_b_glu', 'sgu_ln_gain', 'sgu_ln_bias', 'sgu_w_s', 'sgu_b_s', 'mlp_w1', 'mlp_w2']
ARGS = ['x', 'c'] + WEIGHTS + ['loss_target'] + ['m_' + w for w in WEIGHTS] + ['v_' + w for w in WEIGHTS]

EVEN_COLS = 3608
EVEN_PAD = 8 * MIX_HALF
MOD = 6 * D_MODEL
MOD_SHARD = MOD // N_CHIPS

PACK_COLS = 1024
EVEN_SHARD = EVEN_COLS // N_CHIPS
SHARDED = (
    ([("even_w_in", (1, 1024, PACK_COLS), 2), ("even_w_out", (1, 256, 1024), 1), ("gla_w_lr", (1, 16, 128), 2)], 1536),
    ([("mlp_w1_0", (1, 1024, 1024), 2), ("mlp_w2_0", (1, 1024, 1024), 1), ("odd_w_in", (1, 1024, 384), 2),
      ("odd_w_out", (1, 256, 1024), 1), ("mlp_w1_1", (1, 1024, 1024), 2), ("mlp_w2_1", (1, 1024, 1024), 1),
      ("s5_w_glu", (1, 128, 512), 1), ("s5_b_glu", (1, 128), 1), ("sgu_ln_gain", (1, 128), 1), ("sgu_ln_bias", (1, 128), 1)], 5120))
REPLICATED = [("gla_b_lr", (1, 512)), ("gla_gain", (1, 8, 64)), ("fox_b_f", (1, 8)), ("fox_q_gain", (1, 8, 64)),
              ("fox_k_gain", (1, 8, 64)), ("s5_lam_re", (1, 32, 64)), ("s5_lam_im", (1, 32, 64)), ("s5_log_dt", (1, 32)),
              ("s5_b_re", (1, 32, 64, 16)), ("s5_b_im", (1, 32, 64, 16)), ("s5_c_re", (1, 32, 16, 64)), ("s5_c_im", (1, 32, 16, 64)),
              ("s5_d", (1, 32, 16)), ("sgu_w_s", (1, 8, 128, 128)), ("sgu_b_s", (1, 8, 128))]
SMALL_ROWS = 512
BIG_ADAM = {"ada_w": (2048, 1536), "even_w_in": (1024, 902), "even_w_out": (256, 1024), "odd_w_in": (1024, 384),
            "odd_w_out": (256, 1024), "mlp_w1": (2048, 1024), "mlp_w2": (2048, 1024), "s5_w_glu": (128, 512)}


PACK_ALIGN = 16


def _piece_rows(shape):
    rows = -(-math.prod(shape) // PACK_COLS)
    return -(-rows // PACK_ALIGN) * PACK_ALIGN


def _to_rows(p, lead=()):
    n = math.prod(p.shape[len(lead):])
    rows = _piece_rows(p.shape[len(lead):])
    flat = p.reshape(lead + (n,))
    if rows * PACK_COLS != n:
        flat = jnp.pad(flat, [(0, 0)] * len(lead) + [(0, rows * PACK_COLS - n)])
    return flat.reshape(lead + (rows, PACK_COLS))


def _from_rows(x, r0, shape, lead=()):
    n = math.prod(shape)
    seg = lax.slice_in_dim(x, r0, r0 + _piece_rows(shape), axis=len(lead)).reshape(lead + (-1,))
    return lax.slice_in_dim(seg, 0, n, axis=len(lead)).reshape(lead + tuple(shape))


def _pack_rows(pieces, rows):
    x = jnp.concatenate([_to_rows(p) for p in pieces], axis=0)
    return jnp.pad(x, ((0, rows - x.shape[0]), (0, 0)))


def _unpack(x, specs):
    out, r0 = {}, 0
    for name, shape in specs:
        out[name] = _from_rows(x, r0, shape)
        r0 += _piece_rows(shape)
    return out


def _shards_to_full(x4, pieces):
    out, r0 = {}, 0
    for name, shape, axis in pieces:
        seg = _from_rows(x4, r0, shape, lead=(N_CHIPS,))
        out[name] = jnp.concatenate([seg[k] for k in range(N_CHIPS)], axis=axis)
        r0 += _piece_rows(shape)
    return out


def _full_to_shards(full, pieces, rows):
    blocks = [_to_rows(jnp.stack(jnp.split(full[name], N_CHIPS, axis=axis)), lead=(N_CHIPS,)) for name, _, axis in pieces]
    x = jnp.concatenate(blocks, axis=1)
    return jnp.pad(x, ((0, 0), (0, rows - x.shape[1]), (0, 0)))


def _gather_prep(local, pieces, rows):
    shard = _pack_rows([local[n] for n, _, _ in pieces], rows).astype(BF16)
    return lax.dynamic_slice_in_dim(shard, lax.axis_index("c") * (rows // 2), rows // 2, axis=0)


def _gather_finish(collected, pieces, rows, tag):
    halves = _by_core(collected, _pair_swap(collected, tag + "_pair"))
    return _shards_to_full(halves.transpose(1, 0, 2, 3).reshape(N_CHIPS, rows, PACK_COLS), pieces)


def _reduce_prep(full, pieces, rows, tag):
    mc = lax.axis_index("c")
    packed = _full_to_shards(full, pieces, rows)
    hr = rows // 2
    mine = lax.dynamic_slice_in_dim(packed, mc * hr, hr, axis=1)
    other = lax.dynamic_slice_in_dim(packed, (1 - mc) * hr, hr, axis=1)
    theirs = _pair_swap(other.astype(BF16), tag + "_pair")
    pair_sum, = _ew(lambda p, q: p + q, tag + "_pair_sum", [mine.reshape(N_CHIPS * hr, PACK_COLS), theirs.reshape(N_CHIPS * hr, PACK_COLS)],
                    outs=[(PACK_COLS, BF16)])
    return pair_sum.reshape(N_CHIPS, hr, PACK_COLS)


def _reduce_finish(arrived, pieces, rows, tag):
    red_half = _sum_slots(arrived, tag + "_chip_sum")
    reduced = _by_core(red_half, _pair_swap(red_half, tag + "_pair_out")).reshape(rows, PACK_COLS)
    return _unpack(reduced, [(n, s) for n, s, _ in pieces])


def _relu2(t):
    r = jnp.maximum(t, 0.0)
    return r * r


def _silu(t):
    return t * _sigmoid(t)


def _pack_even(w):
    return jnp.concatenate([w[:, :2048], w[:, 2064:3600], w[:, 2048:2064], w[:, 3600:3608],
                            jnp.zeros((w.shape[0], EVEN_PAD - EVEN_COLS), w.dtype)], axis=1)


def _unpack_even(wp):
    return jnp.concatenate([wp[:, :2048], wp[:, 3584:3600], wp[:, 2048:3584], wp[:, 3600:3608]], axis=1)


def _mlp_fwd(h, w1, w2, pfx):
    pre = _mm(h, w1, "nn", pfx + "_up", out_dtype=BF16)
    return pre, _mm(pre, w2, "nn", pfx + "_down", a_pro=_relu2, out_dtype=BF16)


def _mlp_bwd(dm, h, pre, w1, w2, pfx):
    dpre = _mm(dm, w2, "nt", pfx + "_dpre", epi=lambda acc, p: acc * (2.0 * jnp.maximum(p, 0.0)), extras=[(pre, "mn")], out_dtype=BF16)
    dw2 = _mm(pre, dm, "tn", pfx + "_dw2", a_pro=_relu2)
    dw1 = _mm(h, dpre, "tn", pfx + "_dw1")
    dh = _mm(dpre, w1, "nt", pfx + "_dh", out_dtype=BF16)
    return dh, dw1, dw2


def _step(args):
    a = dict(zip(ARGS, args, strict=True))
    x0 = a["x"][0]
    target = a["loss_target"][0]
    mx, my, mc = lax.axis_index("x"), lax.axis_index("y"), lax.axis_index("c")
    chip = 2 * mx + my
    dev = 2 * chip + mc
    bavg = _head_avg_matrix(MIX_HALF)

    c_all = _gather8(jnp.pad(a["c"], ((0, 7), (0, 0))), "c_gather")[:, :, 0, :].reshape(2 * N_CHIPS, D_MODEL)
    ada_b_shard = lax.dynamic_slice_in_dim(a["ada_b"], chip * MOD_SHARD, MOD_SHARD, axis=1)
    mod_sh = [_mm(c_all, a["ada_w"][l], "nn", f"mod{l}", a_pro=_silu, epi=lambda acc, b: acc + b, extras=[(ada_b_shard[l:l + 1], "n")])
              for l in range(2)]
    small3 = jnp.zeros((8, MOD_SHARD), F32)
    for r, n in enumerate(("s5_b_glu", "sgu_ln_gain", "sgu_ln_bias")):
        small3 = small3.at[r, :LANES].set(a[n][0])
    mod_all = _chip_exchange(jnp.concatenate(mod_sh + [small3]), "mod_gather", True)
    mods = []
    for l in range(2):
        full = mod_all[:, 8 * l:8 * l + 8].transpose(1, 0, 2).reshape(8, MOD)
        mods.append(jnp.split(lax.dynamic_slice_in_dim(full, dev, 1, axis=0), 6, axis=1))
    b_glu, ln_gain, ln_bias = (mod_all[:, 16 + r, :LANES].reshape(1, MIX_HALF) for r in range(3))

    local = dict(a, even_w_in=jnp.pad(a["even_w_in"], ((0, 0), (0, 0), (0, PACK_COLS - EVEN_SHARD))),
                 mlp_w1_0=a["mlp_w1"][0:1], mlp_w1_1=a["mlp_w1"][1:2], mlp_w2_0=a["mlp_w2"][0:1], mlp_w2_1=a["mlp_w2"][1:2])
    (pieces0, rows0), (pieces1, rows1) = SHARDED
    w = _gather_finish(_chip_exchange(_gather_prep(local, pieces0, rows0), "w0_chips", True), pieces0, rows0, "w0")
    w_even = _pack_even(w["even_w_in"][0].reshape(D_MODEL, N_CHIPS, PACK_COLS)[:, :, :EVEN_SHARD].reshape(D_MODEL, EVEN_COLS))
    w_lr_pad = jnp.zeros((MIX_HALF, MIX_HALF), BF16).at[:GLA_RANK].set(w["gla_w_lr"][0])
    gla_b_lr = a["gla_b_lr"]
    gla_gain, q_gain, k_gain = (a[n].reshape(1, MIX_HALF) for n in ("gla_gain", "fox_q_gain", "fox_k_gain"))
    sgu_wm, sgu_bt = _sgu_tables(a["sgu_w_s"][0], a["sgu_b_s"][0])

    sh1, sc1, g1, sh2, sc2, g2 = mods[0]
    _, h1_0 = _res_rms(x0, sc1, sh1, "l0_norm1")
    proj0 = _mm(h1_0, w_even, "nn", "l0_proj")
    og, gla_res = _gla_block_fwd(proj0, w_lr_pad, gla_b_lr, gla_gain, bavg, "gla")
    of, fox_res, (collected1,) = _fox_block_fwd(proj0, a["fox_b_f"][0], q_gain, k_gain, bavg, "fox",
                                                carry=(_gather_prep(local, pieces1, rows1), True))
    w.update(_gather_finish(collected1, pieces1, rows1, "w1"))
    s5w = dict(lam_re=a["s5_lam_re"][0], lam_im=a["s5_lam_im"][0], log_dt=a["s5_log_dt"][0], b_re=a["s5_b_re"][0], b_im=a["s5_b_im"][0],
               c_re=a["s5_c_re"][0], c_im=a["s5_c_im"][0], d=a["s5_d"][0], w_glu=w["s5_w_glu"][0], b_glu=b_glu)
    mixed0 = jnp.concatenate([og, of], axis=1).astype(BF16)
    y0 = _mm(mixed0, w["even_w_out"][0], "nn", "l0_out", out_dtype=BF16)
    x1, h2_0 = _res_rms(x0, sc2, sh2, "l0_norm2", y=y0, g=g1)
    pre0, m0 = _mlp_fwd(h2_0, w["mlp_w1_0"][0], w["mlp_w2_0"][0], "l0_mlp")
    sh1b, sc1b, g1b, sh2b, sc2b, g2b = mods[1]
    x2, h1_1 = _res_rms(x1, sc1b, sh1b, "l1_norm1", y=m0, g=g2)
    proj1 = _mm(h1_1, w["odd_w_in"][0], "nn", "l1_proj")
    ys5, s5_res = _s5_block_fwd(proj1[:, :MIX_HALF], s5w, "s5")
    ysgu = _sgu_fwd(proj1, ln_gain, ln_bias, sgu_wm, sgu_bt, "sgu")
    mixed1 = jnp.concatenate([ys5, ysgu], axis=1).astype(BF16)
    y1 = _mm(mixed1, w["odd_w_out"][0], "nn", "l1_out", out_dtype=BF16)
    x3, h2_1 = _res_rms(x2, sc2b, sh2b, "l1_norm2", y=y1, g=g1b)
    pre1, m1 = _mlp_fwd(h2_1, w["mlp_w1_1"][0], w["mlp_w2_1"][0], "l1_mlp")
    loss_b, dx4, dm1, dg2b = _res_loss(x3, m1, g2b, target, "loss")
    loss = lax.psum(loss_b[0, 0], ("x", "y", "c"))

    full = {}
    dh2_1, dw1_1, dw2_1 = _mlp_bwd(dm1, h2_1, pre1, w["mlp_w1_1"][0], w["mlp_w2_1"][0], "l1_mlp")
    dx3, dy1, dg1b, dsc2b, dsh2b = _res_rms_bwd(x3, dh2_1, sc2b, dx4, "l1_norm2_bwd", y=y1, g=g1b)
    dmixed1 = _mm(dy1, w["odd_w_out"][0], "nt", "l1_out_dx")
    full["odd_w_out"] = _mm(mixed1, dy1, "tn", "l1_out_dw")[None]
    du, s5g = _s5_block_bwd(dmixed1[:, :MIX_HALF], s5w, s5_res, "s5")
    dzu, dzv, dws, dbt, dlg, dlb = _sgu_bwd(dmixed1[:, MIX_HALF:], proj1, ln_gain, ln_bias, sgu_wm, sgu_bt, "sgu_bwd")
    g_ws, g_bs = _sgu_grads(dws, dbt)
    dproj1 = jnp.concatenate([du, dzu, dzv], axis=1).astype(BF16)
    full["odd_w_in"] = _mm(h1_1, dproj1, "tn", "l1_proj_dw")[None]
    dh1_1 = _mm(dproj1, w["odd_w_in"][0], "nt", "l1_proj_dx", out_dtype=BF16)
    dx2, dm0, dg2, dsc1b, dsh1b = _res_rms_bwd(x2, dh1_1, sc1b, dx3, "l1_norm1_bwd", y=m0, g=g2)
    dh2_0, dw1_0, dw2_0 = _mlp_bwd(dm0, h2_0, pre0, w["mlp_w1_0"][0], w["mlp_w2_0"][0], "l0_mlp")
    full.update(mlp_w1_0=dw1_0[None], mlp_w2_0=dw2_0[None], mlp_w1_1=dw1_1[None], mlp_w2_1=dw2_1[None], s5_w_glu=s5g["w_glu"][None],
                s5_b_glu=s5g["b_glu"][None], sgu_ln_gain=dlg, sgu_ln_bias=dlb)
    pair_sums1 = _reduce_prep(full, pieces1, rows1, "g1")
    dx1, dy0, dg1, dsc2, dsh2 = _res_rms_bwd(x1, dh2_0, sc2, dx2, "l0_norm2_bwd", y=y0, g=g1)
    dmixed0 = _mm(dy0, w["even_w_out"][0], "nt", "l0_out_dx")
    full["even_w_out"] = _mm(mixed0, dy0, "tn", "l0_out_dw")[None]
    (dgq, dgk, dgv, dgg, dsmall), glag = _gla_block_bwd((dmixed0, MIX_HALF, 0), proj0, w_lr_pad, gla_gain, bavg, gla_res, "gla")
    (dfq, dfk, dfv, dff), foxg, (arrived1,) = _fox_block_bwd((dmixed0, MIX_HALF, 1), proj0, q_gain, k_gain, bavg, fox_res, "fox",
                                                           carry=(pair_sums1, False))
    dsmall = lax.dynamic_update_slice(dsmall, dff, (0, GLA_RANK))
    dproj0 = jnp.concatenate([dgq, dgk, dgv, dgg, dfq, dfk, dfv, dsmall], axis=1).astype(BF16)
    d_even = _unpack_even(_mm(h1_0, dproj0, "tn", "l0_proj_dw")).reshape(D_MODEL, N_CHIPS, EVEN_SHARD)
    full["even_w_in"] = jnp.pad(d_even, ((0, 0), (0, 0), (0, PACK_COLS - EVEN_SHARD))).reshape(1, D_MODEL, N_CHIPS * PACK_COLS)
    dh1_0 = _mm(dproj0, w_even, "nt", "l0_proj_dx", out_dtype=BF16)
    grad_x, dsc1, dsh1 = _res_rms_bwd(x0, dh1_0, sc1, dx1, "l0_norm1_bwd")
    full["gla_w_lr"] = glag["w_lr"][None]

    dmod = jnp.concatenate([dsh1, dsc1, dg1, dsh2, dsc2, dg2, dsh1b, dsc1b, dg1b, dsh2b, dsc2b, dg2b], axis=1)
    dmod_all = _gather8(jnp.pad(dmod, ((0, 7), (0, 0))), "dmod_gather")[:, :, 0, :].reshape(2 * N_CHIPS, 2, MOD)
    grads = {}
    grads["ada_w"] = jnp.stack([
        _mm(c_all, lax.dynamic_slice_in_dim(dmod_all[:, l], chip * MOD_SHARD, MOD_SHARD, axis=1), "tn", f"ada_dw{l}", a_pro=_silu)
        for l in range(2)])
    grads["ada_b"] = _sum_slots(dmod_all.reshape(2 * N_CHIPS, 2 * MOD // MIX_HALF, MIX_HALF), "ada_db").reshape(2, MOD)

    grads.update(_reduce_finish(arrived1, pieces1, rows1, "g1"))
    grads.update(_reduce_finish(_chip_exchange(_reduce_prep(full, pieces0, rows0, "g0"), "g0_chips", False), pieces0, rows0, "g0"))
    grads["even_w_in"] = grads["even_w_in"][:, :, :EVEN_SHARD]
    grads["mlp_w1"] = jnp.concatenate([grads.pop("mlp_w1_0"), grads.pop("mlp_w1_1")])
    grads["mlp_w2"] = jnp.concatenate([grads.pop("mlp_w2_0"), grads.pop("mlp_w2_1")])

    part = dict(gla_b_lr=glag["b_lr"], gla_gain=glag["gain"], fox_b_f=foxg["b_f"], fox_q_gain=foxg["q_gain"], fox_k_gain=foxg["k_gain"],
                s5_lam_re=s5g["lam_re"], s5_lam_im=s5g["lam_im"], s5_log_dt=s5g["log_dt"], s5_b_re=s5g["b_re"], s5_b_im=s5g["b_im"],
                s5_c_re=s5g["c_re"], s5_c_im=s5g["c_im"], s5_d=s5g["d"], sgu_w_s=g_ws, sgu_b_s=g_bs)
    parts_all = _gather8(_pack_rows([part[n] for n, _ in REPLICATED], SMALL_ROWS).astype(BF16), "rep_gather")
    rep = _sum_slots(parts_all.reshape(2 * N_CHIPS, SMALL_ROWS, PACK_COLS), "rep_sum")
    grads.update(_unpack(rep, REPLICATED))

    delta, new_m, new_v = {}, {}, {}
    for n, shape2 in BIG_ADAM.items():
        d, nm, nv = _adamw(a[n].reshape(shape2), grads[n].reshape(shape2), a["m_" + n].reshape(shape2), a["v_" + n].reshape(shape2), "adamw_" + n)
        delta[n], new_m[n], new_v[n] = (t.reshape(a[n].shape) for t in (d, nm, nv))
    small = [n for n in WEIGHTS if n not in BIG_ADAM]
    spec = [(n, a[n].shape) for n in small]
    packs = [_pack_rows([src[n] for n in small], SMALL_ROWS) for src in
             (a, grads, {n: a["m_" + n] for n in small}, {n: a["v_" + n] for n in small})]
    for tgt, res in zip((delta, new_m, new_v), _adamw(*packs, "adamw_small")):
        tgt.update(_unpack(res, spec))
    outs = [loss, grad_x[None]]
    for group in (grads, delta, new_m, new_v):
        outs += [group[n].reshape(a[n].shape) for n in WEIGHTS]
    return tuple(outs)


def kernel(x, c, ada_w, ada_b, even_w_in, even_w_out, gla_w_lr, gla_b_lr, gla_gain, fox_b_f, fox_q_gain, fox_k_gain, odd_w_in,
           odd_w_out, s5_lam_re, s5_lam_im, s5_log_dt, s5_b_re, s5_b_im, s5_c_re, s5_c_im, s5_d, s5_w_glu, s5_b_glu, sgu_ln_gain,
           sgu_ln_bias, sgu_w_s, sgu_b_s, mlp_w1, mlp_w2, loss_target, m_ada_w, m_ada_b, m_even_w_in, m_even_w_out, m_gla_w_lr,
           m_gla_b_lr, m_gla_gain, m_fox_b_f, m_fox_q_gain, m_fox_k_gain, m_odd_w_in, m_odd_w_out, m_s5_lam_re, m_s5_lam_im,
           m_s5_log_dt, m_s5_b_re, m_s5_b_im, m_s5_c_re, m_s5_c_im, m_s5_d, m_s5_w_glu, m_s5_b_glu, m_sgu_ln_gain, m_sgu_ln_bias,
           m_sgu_w_s, m_sgu_b_s, m_mlp_w1, m_mlp_w2, v_ada_w, v_ada_b, v_even_w_in, v_even_w_out, v_gla_w_lr, v_gla_b_lr,
           v_gla_gain, v_fox_b_f, v_fox_q_gain, v_fox_k_gain, v_odd_w_in, v_odd_w_out, v_s5_lam_re, v_s5_lam_im, v_s5_log_dt,
           v_s5_b_re, v_s5_b_im, v_s5_c_re, v_s5_c_im, v_s5_d, v_s5_w_glu, v_s5_b_glu, v_sgu_ln_gain, v_sgu_ln_bias, v_sgu_w_s,
           v_sgu_b_s, v_mlp_w1, v_mlp_w2):
    return _step((x, c, ada_w, ada_b, even_w_in, even_w_out, gla_w_lr, gla_b_lr, gla_gain, fox_b_f, fox_q_gain, fox_k_gain,
                  odd_w_in, odd_w_out, s5_lam_re, s5_lam_im, s5_log_dt, s5_b_re, s5_b_im, s5_c_re, s5_c_im, s5_d, s5_w_glu,
                  s5_b_glu, sgu_ln_gain, sgu_ln_bias, sgu_w_s, sgu_b_s, mlp_w1, mlp_w2, loss_target, m_ada_w, m_ada_b,
                  m_even_w_in, m_even_w_out, m_gla_w_lr, m_gla_b_lr, m_gla_gain, m_fox_b_f, m_fox_q_gain, m_fox_k_gain,
                  m_odd_w_in, m_odd_w_out, m_s5_lam_re, m_s5_lam_im, m_s5_log_dt, m_s5_b_re, m_s5_b_im, m_s5_c_re, m_s5_c_im,
                  m_s5_d, m_s5_w_glu, m_s5_b_glu, m_sgu_ln_gain, m_sgu_ln_bias, m_sgu_w_s, m_sgu_b_s, m_mlp_w1, m_mlp_w2, v_ada_w,
                  v_ada_b, v_even_w_in, v_even_w_out, v_gla_w_lr, v_gla_b_lr, v_gla_gain, v_fox_b_f, v_fox_q_gain, v_fox_k_gain,
                  v_odd_w_in, v_odd_w_out, v_s5_lam_re, v_s5_lam_im, v_s5_log_dt, v_s5_b_re, v_s5_b_im, v_s5_c_re, v_s5_c_im,
                  v_s5_d, v_s5_w_glu, v_s5_b_glu, v_sgu_ln_gain, v_sgu_ln_bias, v_sgu_w_s, v_sgu_b_s, v_mlp_w1, v_mlp_w2))
```

```python
import functools
import math

import jax
import jax.numpy as jnp
import numpy as np
from jax import lax
from jax.experimental import pallas as pl
from jax.experimental.pallas import tpu as pltpu

F32 = jnp.float32
BF16 = jnp.bfloat16
MESH = pl.DeviceIdType.MESH
ANY = pl.BlockSpec(memory_space=pl.ANY)
DMA_SEM = pltpu.SemaphoreType.DMA

D_MODEL = 1024
HEAD_DIM = 64
MIX_HALF = 512
GLA_RANK = 16
GLA_TAU = 16.0
GLA_CHUNK = 64
S5_GROUPS = 32
S5_GROUP_WIDTH = 16
S5_STATE = 64
S5_N = S5_GROUPS * S5_STATE
SGU_GROUPS = 8
SGU_CHUNK = 128
D_FF = 4096
EPS = 1e-6
N_CHIPS = 4
LANES = 128
VMEM_LIMIT = 48 * 1024 * 1024
PAIR_COPIES = 16

ADAM_LR = 0.001
ADAM_B1 = 0.9
ADAM_B2 = 0.999
ADAM_EPS = 1e-08
ADAM_WD = 0.01
ADAM_STEP = 10


def _cparams(*sem):
    return pltpu.CompilerParams(dimension_semantics=sem, vmem_limit_bytes=VMEM_LIMIT)


def _pair_swap(x, name):
    lead = x.shape[:-2]
    rows = x.shape[-2]
    nsplit = max(1, PAIR_COPIES // max(1, math.prod(lead)))
    while nsplit > 1 and rows % (nsplit * 16):
        nsplit -= 1
    pieces = [idx + (pl.ds(j * (rows // nsplit), rows // nsplit),) for idx in np.ndindex(*lead) for j in range(nsplit)]

    def body(x_ref, o_ref, send_sems, recv_sems):
        mx, my, mc = lax.axis_index("x"), lax.axis_index("y"), lax.axis_index("c")
        copies = [pltpu.make_async_remote_copy(src_ref=x_ref.at[p], dst_ref=o_ref.at[p], send_sem=send_sems.at[j], recv_sem=recv_sems.at[j],
                                               device_id=(mx, my, 1 - mc), device_id_type=MESH) for j, p in enumerate(pieces)]
        for cp in copies:
            cp.start()
        for cp in copies:
            cp.wait_recv()
        for cp in copies:
            cp.wait_send()

    return pl.pallas_call(
        body, name=name, out_shape=jax.ShapeDtypeStruct(x.shape, x.dtype), in_specs=[ANY], out_specs=ANY,
        scratch_shapes=[DMA_SEM((len(pieces),)), DMA_SEM((len(pieces),))])(x)


def _by_core(mine, theirs):
    first = lax.axis_index("c") == 0
    return jnp.stack([jnp.where(first, mine, theirs), jnp.where(first, theirs, mine)])


def _chip_exchange(x, name, bcast):
    blk = x.shape if bcast else x.shape[1:]

    def body(x_ref, o_ref, send_sems, recv_sems, loc_sem):
        start, finish = _exchange_plan(x_ref, o_ref, send_sems, recv_sems, loc_sem, bcast)
        start()
        finish()

    return pl.pallas_call(
        body, name=name, out_shape=jax.ShapeDtypeStruct((N_CHIPS,) + tuple(blk), x.dtype), in_specs=[ANY], out_specs=ANY,
        scratch_shapes=_EXCHANGE_SEMS)(x)


_EXCHANGE_SEMS = [DMA_SEM((3,)), DMA_SEM((3,)), DMA_SEM]


def _exchange_plan(x_ref, o_ref, send_sems, recv_sems, loc_sem, bcast):
    mx, my, mc = lax.axis_index("x"), lax.axis_index("y"), lax.axis_index("c")
    me = 2 * mx + my
    peers = [(1 - mx, my), (mx, 1 - my), (1 - mx, 1 - my)]

    def src(k):
        return x_ref if bcast else x_ref.at[k]

    def remote(j, source, slot):
        px, py = peers[j]
        return pltpu.make_async_remote_copy(src_ref=source, dst_ref=o_ref.at[slot], send_sem=send_sems.at[j], recv_sem=recv_sems.at[j],
                                            device_id=(px, py, mc), device_id_type=MESH)

    loc = pltpu.make_async_copy(src(me), o_ref.at[me], loc_sem)
    sends = [remote(j, src(2 * px + py), me) for j, (px, py) in enumerate(peers)]
    arrivals = [remote(j, src(me), 2 * px + py) for j, (px, py) in enumerate(peers)]

    def start():
        loc.start()
        for cp in sends:
            cp.start()

    def finish():
        for cp in arrivals:
            cp.wait_recv()
        for cp in sends:
            cp.wait_send()
        loc.wait()

    return start, finish


def _gather8(x, name):
    collected = _chip_exchange(x, name + "_chips", True)
    return jnp.swapaxes(_by_core(collected, _pair_swap(collected, name + "_pair")), 0, 1)


def _tile(n, want):
    if n <= want:
        return n
    t = (want // LANES) * LANES
    while t >= LANES:
        if n % t == 0:
            return t
        t -= LANES
    raise ValueError(f"no lane-aligned tile for {n}")


_DIMS = {"nn": (((1,), (0,)), ((), ())), "nt": (((1,), (1,)), ((), ())), "tn": (((0,), (0,)), ((), ()))}


MM_FULL_K = 4096
MM_SLAB_K = 2048
MM_TILES = ((1024, 1024), (512, 1024), (1024, 512), (512, 512), (256, 512), (256, 256))
MM_VMEM_BUDGET = 36 * 1024 * 1024


def _mm(a, b, mode, name, *, a_pro=None, epi=None, extras=(), out_dtype=F32, tm_max=1024, tn_max=1024, tk=None, a_cols=None):
    c0, csize = a_cols if a_cols is not None else (0, a.shape[1])
    if mode == "tn":
        K, M = a.shape[0], csize
    else:
        M, K = a.shape[0], csize
    N = b.shape[0] if mode == "nt" else b.shape[1]
    assert (b.shape[1] if mode == "nt" else b.shape[0]) == K, (a.shape, b.shape, mode)
    if tk is None:
        tk = K if (mode != "tn" and K <= MM_FULL_K) else MM_SLAB_K
    tk = _tile(K, tk)
    nk = K // tk
    n_mn = sum(1 for _, kind in extras if kind == "mn")
    for tm_want, tn_want in MM_TILES:
        tm, tn = _tile(M, min(tm_want, tm_max)), _tile(N, min(tn_want, tn_max))
        need = 2 * (tm * tk * a.dtype.itemsize + tk * tn * b.dtype.itemsize + tm * tn * 4 * (1 + n_mn)) + tm * tn * 4 * (nk > 1)
        if need <= MM_VMEM_BUDGET:
            break
    if mode == "tn":
        assert c0 % tm == 0
        a_spec = pl.BlockSpec((tk, tm), lambda i, j, k: (k, i + c0 // tm))
    else:
        assert c0 % tk == 0
        a_spec = pl.BlockSpec((tm, tk), lambda i, j, k: (i, k + c0 // tk))
    b_spec = pl.BlockSpec((tn, tk), lambda i, j, k: (j, k)) if mode == "nt" else pl.BlockSpec((tk, tn), lambda i, j, k: (k, j))
    ex_specs = []
    for arr, kind in extras:
        if kind == "mn":
            assert arr.shape == (M, N)
            ex_specs.append(pl.BlockSpec((tm, tn), lambda i, j, k: (i, j)))
        else:
            assert arr.shape == (1, N)
            ex_specs.append(pl.BlockSpec((1, tn), lambda i, j, k: (0, j)))
    n_ex = len(extras)

    def body(*refs):
        a_ref, b_ref = refs[:2]
        ex_refs = refs[2:2 + n_ex]
        o_ref = refs[2 + n_ex]
        acc_ref = refs[3 + n_ex] if nk > 1 else None
        k = pl.program_id(2)
        av = a_ref[...]
        if a_pro is not None:
            av = a_pro(av)
        part = lax.dot_general(av.astype(BF16), b_ref[...].astype(BF16), _DIMS[mode], preferred_element_type=F32)
        if nk == 1:
            if epi is not None:
                part = epi(part, *[r[...] for r in ex_refs])
            o_ref[...] = part.astype(o_ref.dtype)
            return

        @pl.when(k == 0)
        def _():
            acc_ref[...] = part

        @pl.when(k > 0)
        def _():
            acc_ref[...] += part

        @pl.when(k == nk - 1)
        def _():
            acc = acc_ref[...]
            if epi is not None:
                acc = epi(acc, *[r[...] for r in ex_refs])
            o_ref[...] = acc.astype(o_ref.dtype)

    return pl.pallas_call(
        body, name=name, grid=(M // tm, N // tn, nk),
        in_specs=[a_spec, b_spec] + ex_specs,
        out_specs=pl.BlockSpec((tm, tn), lambda i, j, k: (i, j)),
        out_shape=jax.ShapeDtypeStruct((M, N), out_dtype),
        scratch_shapes=[pltpu.VMEM((tm, tn), F32)] if nk > 1 else [],
        compiler_params=_cparams("parallel", "parallel", "arbitrary"))(a, b, *[e[0] for e in extras])


ROWS = 512


def _row_spec(w, ts=ROWS):
    return pl.BlockSpec((ts, w), lambda i: (i, 0))


def _vec_spec(w):
    return pl.BlockSpec((1, w), lambda i: (0, 0))


def _res_rms(x, sc, sh, name, y=None, g=None):
    S, D = x.shape
    has_res = y is not None

    def body(*refs):
        if has_res:
            x_ref, y_ref, g_ref, sc_ref, sh_ref, xo_ref, h_ref = refs
            xv = x_ref[...] + g_ref[...] * y_ref[...]
            xo_ref[...] = xv
        else:
            x_ref, sc_ref, sh_ref, h_ref = refs
            xv = x_ref[...]
        r = lax.rsqrt(jnp.mean(xv * xv, axis=-1, keepdims=True) + EPS)
        h_ref[...] = (xv * r * (1.0 + sc_ref[...]) + sh_ref[...]).astype(BF16)

    row, vec = _row_spec(D), _vec_spec(D)
    if has_res:
        return pl.pallas_call(body, name=name, grid=(S // ROWS,), in_specs=[row, row, vec, vec, vec], out_specs=[row, row],
                              out_shape=[jax.ShapeDtypeStruct((S, D), F32), jax.ShapeDtypeStruct((S, D), BF16)],
                              compiler_params=_cparams("parallel"))(x, y, g, sc, sh)
    h = pl.pallas_call(body, name=name, grid=(S // ROWS,), in_specs=[row, vec, vec], out_specs=row,
                       out_shape=jax.ShapeDtypeStruct((S, D), BF16), compiler_params=_cparams("parallel"))(x, sc, sh)
    return x, h


def _res_rms_bwd(x, dh, sc, dres, name, y=None, g=None):
    S, D = x.shape
    has_res = y is not None

    def body(*refs):
        if has_res:
            x_ref, dh_ref, sc_ref, dres_ref, y_ref, g_ref, dx_ref, dy_ref, dg_ref, dsc_ref, dsh_ref = refs
        else:
            x_ref, dh_ref, sc_ref, dres_ref, dx_ref, dsc_ref, dsh_ref = refs
        first = pl.program_id(0) == 0
        xv = x_ref[...]
        dh = dh_ref[...].astype(F32)
        r = lax.rsqrt(jnp.mean(xv * xv, axis=-1, keepdims=True) + EPS)
        xn = xv * r
        dxn = dh * (1.0 + sc_ref[...])
        dx = dres_ref[...] + r * (dxn - xn * jnp.mean(dxn * xn, axis=-1, keepdims=True))
        dx_ref[...] = dx
        parts = [(dsc_ref, jnp.sum(dh * xn, axis=0, keepdims=True)), (dsh_ref, jnp.sum(dh, axis=0, keepdims=True))]
        if has_res:
            dy_ref[...] = (dx * g_ref[...]).astype(BF16)
            parts.append((dg_ref, jnp.sum(dx * y_ref[...], axis=0, keepdims=True)))
        for ref, val in parts:
            @pl.when(first)
            def _(ref=ref, val=val):
                ref[...] = val

            @pl.when(jnp.logical_not(first))
            def _(ref=ref, val=val):
                ref[...] += val

    row, vec = _row_spec(D), _vec_spec(D)
    full = jax.ShapeDtypeStruct((S, D), F32)
    v = jax.ShapeDtypeStruct((1, D), F32)
    if has_res:
        return pl.pallas_call(body, name=name, grid=(S // ROWS,), in_specs=[row, row, vec, row, row, vec],
                              out_specs=[row, row, vec, vec, vec], out_shape=[full, jax.ShapeDtypeStruct((S, D), BF16), v, v, v],
                              compiler_params=_cparams("arbitrary"))(x, dh, sc, dres, y, g)
    return pl.pallas_call(body, name=name, grid=(S // ROWS,), in_specs=[row, row, vec, row],
                          out_specs=[row, vec, vec], out_shape=[full, v, v],
                          compiler_params=_cparams("arbitrary"))(x, dh, sc, dres)


def _res_loss(x, m, g, target, name):
    S, D = x.shape

    def body(x_ref, m_ref, g_ref, t_ref, loss_ref, dx_ref, dm_ref, dg_ref):
        first = pl.program_id(0) == 0
        mv = m_ref[...].astype(F32)
        err = x_ref[...] + g_ref[...] * mv - t_ref[...]
        dx = err * (1.0 / D)
        dx_ref[...] = dx
        dm_ref[...] = (dx * g_ref[...]).astype(BF16)
        part = 0.5 * jnp.sum(jnp.mean(err * err, axis=-1, keepdims=True), axis=0, keepdims=True)
        dg = jnp.sum(dx * mv, axis=0, keepdims=True)

        @pl.when(first)
        def _():
            loss_ref[...] = jnp.broadcast_to(part, loss_ref.shape)
            dg_ref[...] = dg

        @pl.when(jnp.logical_not(first))
        def _():
            loss_ref[...] += jnp.broadcast_to(part, loss_ref.shape)
            dg_ref[...] += dg

    row, vec = _row_spec(D), _vec_spec(D)
    full = jax.ShapeDtypeStruct((S, D), F32)
    return pl.pallas_call(body, name=name, grid=(S // ROWS,), in_specs=[row, row, vec, row],
                          out_specs=[pl.BlockSpec((8, LANES), lambda i: (0, 0)), row, row, vec],
                          out_shape=[jax.ShapeDtypeStruct((8, LANES), F32), full, jax.ShapeDtypeStruct((S, D), BF16), jax.ShapeDtypeStruct((1, D), F32)],
                          compiler_params=_cparams("arbitrary"))(x, m, g, target)


def _adamw(w, g, m, v, name):
    R, C = w.shape
    tr = R if R <= 256 else 256
    assert R % tr == 0

    def body(w_ref, g_ref, m_ref, v_ref, d_ref, nm_ref, nv_ref):
        gv = g_ref[...]
        nm = ADAM_B1 * m_ref[...] + (1.0 - ADAM_B1) * gv
        nv = ADAM_B2 * v_ref[...] + (1.0 - ADAM_B2) * jnp.square(gv)
        m_hat = nm / (1.0 - ADAM_B1 ** ADAM_STEP)
        v_hat = nv / (1.0 - ADAM_B2 ** ADAM_STEP)
        d_ref[...] = -ADAM_LR * (m_hat / (jnp.sqrt(v_hat) + ADAM_EPS) + ADAM_WD * w_ref[...])
        nm_ref[...] = nm
        nv_ref[...] = nv

    spec = pl.BlockSpec((tr, C), lambda i: (i, 0))
    out = jax.ShapeDtypeStruct((R, C), F32)
    return pl.pallas_call(body, name=name, grid=(R // tr,), in_specs=[spec] * 4, out_specs=[spec] * 3,
                          out_shape=[out, out, out], compiler_params=_cparams("parallel"))(w, g, m, v)


def _sum_slots(x, name):
    n, R, C = x.shape
    tr = R if R <= 256 else 256
    assert R % tr == 0

    def body(x_ref, o_ref):
        acc = x_ref[0].astype(F32)
        for j in range(1, n):
            acc = acc + x_ref[j].astype(F32)
        o_ref[...] = acc

    return pl.pallas_call(body, name=name, grid=(R // tr,), in_specs=[pl.BlockSpec((n, tr, C), lambda i: (0, i, 0))],
                          out_specs=pl.BlockSpec((tr, C), lambda i: (i, 0)), out_shape=jax.ShapeDtypeStruct((R, C), F32),
                          compiler_params=_cparams("parallel"))(x)


def _ew(fn, name, tiled, consts=(), outs=(), sums=(), ts=ROWS):
    tiled = [t if isinstance(t, tuple) else (t, t.shape[1], 0) for t in tiled]
    S = tiled[0][0].shape[0]
    n_t, n_c, n_o, n_s = len(tiled), len(consts), len(outs), len(sums)

    def body(*refs):
        ins = [r[...] for r in refs[:n_t + n_c]]
        res = fn(*ins)
        res = res if isinstance(res, (tuple, list)) else (res,)
        assert len(res) == n_o + n_s
        o_refs = refs[n_t + n_c:]
        for r, val in zip(o_refs[:n_o], res[:n_o]):
            r[...] = val.astype(r.dtype)
        first = pl.program_id(0) == 0
        for r, val in zip(o_refs[n_o:], res[n_o:]):
            @pl.when(first)
            def _(r=r, val=val):
                r[...] = val

            @pl.when(jnp.logical_not(first))
            def _(r=r, val=val):
                r[...] += val

    in_specs = [pl.BlockSpec((ts, w), lambda i, cb=cb: (i, cb)) for _, w, cb in tiled]
    in_specs += [pl.BlockSpec(c.shape, lambda i, nd=c.ndim: (0,) * nd) for c in consts]
    out_specs = [_row_spec(w, ts) for w, _ in outs] + [_vec_spec(w) for w in sums]
    out_shape = [jax.ShapeDtypeStruct((S, w), dt) for w, dt in outs] + [jax.ShapeDtypeStruct((1, w), F32) for w in sums]
    res = pl.pallas_call(body, name=name, grid=(S // ts,), in_specs=in_specs, out_specs=out_specs, out_shape=out_shape,
                         compiler_params=_cparams("arbitrary" if sums else "parallel"))(*[t[0] for t in tiled], *consts)
    return res


_GELU_C = math.sqrt(2.0 / math.pi)


def _gelu(x):
    return 0.5 * x * (1.0 + jnp.tanh(_GELU_C * (x + 0.044715 * x * x * x)))


def _dgelu(x):
    t = jnp.tanh(_GELU_C * (x + 0.044715 * x * x * x))
    return 0.5 * (1.0 + t) + 0.5 * x * (1.0 - t * t) * _GELU_C * (1.0 + 3.0 * 0.044715 * x * x)


def _sigmoid(x):
    return 1.0 / (1.0 + jnp.exp(-x))


def _log_sigmoid(x):
    return jnp.minimum(x, 0.0) - jnp.log(1.0 + jnp.exp(-jnp.abs(x)))


SCAN_T = 128
SCAN_TB = 512


def _cmul(ar, ai, br, bi):
    return ar * br - ai * bi, ar * bi + ai * br


def _s5_discretise(lam_re, lam_im, log_dt, b_re, b_im):
    dt = jnp.exp(log_dt)[:, None]
    mag = jnp.exp(lam_re * dt)
    ang = lam_im * dt
    abar_re = mag * jnp.cos(ang)
    abar_im = mag * jnp.sin(ang)
    den = lam_re * lam_re + lam_im * lam_im
    coef_re = ((abar_re - 1.0) * lam_re + abar_im * lam_im) / den
    coef_im = (abar_im * lam_re - (abar_re - 1.0) * lam_im) / den
    bbar_re = coef_re[..., None] * b_re - coef_im[..., None] * b_im
    bbar_im = coef_re[..., None] * b_im + coef_im[..., None] * b_re
    return abar_re, abar_im, bbar_re, bbar_im


def _planes(re, im):
    lead = re.shape[:-1]
    return jnp.stack([re.reshape(lead + (-1, LANES)), im.reshape(lead + (-1, LANES))], axis=-2).reshape(lead + (-1,))


def _unplanes(x):
    lead = x.shape[:-1]
    x4 = x.reshape(lead + (-1, 2, LANES))
    return x4[..., 0, :].reshape(lead + (-1,)), x4[..., 1, :].reshape(lead + (-1,))


def _s5_scan_tables(a_re, a_im, reverse):
    pr, pi = [a_re], [a_im]
    for _ in range(7):
        r, i = _cmul(pr[-1], pi[-1], pr[-1], pi[-1])
        pr.append(r)
        pi.append(i)
    apow = _planes(jnp.stack(pr), jnp.stack(pi))
    n = np.arange(1, SCAN_T + 1)
    if reverse:
        n = n[::-1]
    tr = jnp.ones((SCAN_T, a_re.shape[0]), F32)
    ti = jnp.zeros((SCAN_T, a_re.shape[0]), F32)
    for k in range(8):
        bit = jnp.asarray(((n >> k) & 1).astype(np.float32))[:, None]
        mr = bit * pr[k][None, :] + (1.0 - bit)
        mi = bit * pi[k][None, :]
        tr, ti = _cmul(tr, ti, mr, mi)
    return apow, _planes(tr, ti)


def _s5_scan(bu, apow, ptab, name, reverse, x_fwd=None):
    S, N2 = bu.shape
    T, W = SCAN_T, 2 * LANES
    tb = min(SCAN_TB, S)
    nt, nsub = S // tb, tb // T
    order = list(range(nsub - 1, -1, -1) if reverse else range(nsub))
    with_da = x_fwd is not None

    def tblk(t):
        return (nt - 1 - t) if reverse else t

    def shifted(v, k, rowi):
        s = 1 << k
        if reverse:
            return jnp.where(rowi < T - s, pltpu.roll(v, T - s, 0), 0.0)
        return jnp.where(rowi >= s, pltpu.roll(v, s, 0), 0.0)

    def body(*refs):
        if with_da:
            bu_ref, ap_ref, pt_ref, xf_ref, xp_ref, x_ref, da_ref, carry_ref = refs
        else:
            bu_ref, ap_ref, pt_ref, x_ref, carry_ref = refs
        t = pl.program_id(1)

        @pl.when(t == 0)
        def _():
            carry_ref[...] = jnp.zeros_like(carry_ref)
            if with_da:
                da_ref[...] = jnp.zeros_like(da_ref)

        rowi = lax.broadcasted_iota(jnp.int32, (T, LANES), 0)
        pr, pi = pt_ref[:, :LANES], pt_ref[:, LANES:]
        cr, ci = carry_ref[0:1, :LANES], carry_ref[0:1, LANES:]
        for sb in order:
            rows = pl.ds(sb * T, T)
            xr, xi = bu_ref[rows, :LANES], bu_ref[rows, LANES:]
            for k in range(7):
                ar, ai = ap_ref[k:k + 1, :LANES], ap_ref[k:k + 1, LANES:]
                s = 1 << k
                if s < 8:
                    rr, ri = shifted(xr, k, rowi), shifted(xi, k, rowi)
                    xr, xi = xr + ar * rr - ai * ri, xi + ar * ri + ai * rr
                elif reverse:
                    nr, ni = xr[s:], xi[s:]
                    xr = jnp.concatenate([xr[:T - s] + ar * nr - ai * ni, xr[T - s:]], axis=0)
                    xi = jnp.concatenate([xi[:T - s] + ar * ni + ai * nr, xi[T - s:]], axis=0)
                else:
                    nr, ni = xr[:T - s], xi[:T - s]
                    xr = jnp.concatenate([xr[:s], xr[s:] + ar * nr - ai * ni], axis=0)
                    xi = jnp.concatenate([xi[:s], xi[s:] + ar * ni + ai * nr], axis=0)
            xr, xi = xr + pr * cr - pi * ci, xi + pr * ci + pi * cr
            x_ref[rows, :LANES] = xr
            x_ref[rows, LANES:] = xi
            edge = pl.ds(sb * T + (0 if reverse else T - 1), 1)
            cr, ci = x_ref[edge, :LANES], x_ref[edge, LANES:]
            if with_da:
                if sb > 0:
                    before = pl.ds(sb * T - 1, 1)
                    b_r, b_i = xf_ref[before, :LANES], xf_ref[before, LANES:]
                else:
                    keep = (tblk(t) > 0).astype(F32)
                    b_r, b_i = xp_ref[7:8, :LANES] * keep, xp_ref[7:8, LANES:] * keep
                fr, fi = xf_ref[rows, :LANES], xf_ref[rows, LANES:]
                qr = jnp.where(rowi >= 1, pltpu.roll(fr, 1, 0), b_r)
                qi = jnp.where(rowi >= 1, pltpu.roll(fi, 1, 0), b_i)
                gr, gi = xr * qr + xi * qi, xi * qr - xr * qi
                sr, si = gr[0:8], gi[0:8]
                for j in range(1, T // 8):
                    sr, si = sr + gr[8 * j:8 * j + 8], si + gi[8 * j:8 * j + 8]
                da_ref[:, :LANES] += sr
                da_ref[:, LANES:] += si
        carry_ref[0:1, :LANES] = cr
        carry_ref[0:1, LANES:] = ci

    blk = pl.BlockSpec((tb, W), lambda j, t: (tblk(t), j))
    in_specs = [blk, pl.BlockSpec((8, W), lambda j, t: (0, j)), pl.BlockSpec((T, W), lambda j, t: (0, j))]
    out_specs, out_shape = [blk], [jax.ShapeDtypeStruct((S, N2), F32)]
    args = [bu, apow, ptab]
    if with_da:
        in_specs += [blk, pl.BlockSpec((8, W), lambda j, t: (jnp.maximum(tblk(t) * (tb // 8) - 1, 0), j))]
        out_specs.append(pl.BlockSpec((8, W), lambda j, t: (0, j)))
        out_shape.append(jax.ShapeDtypeStruct((8, N2), F32))
        args += [x_fwd, x_fwd]
    res = pl.pallas_call(body, name=name, grid=(N2 // W, nt), in_specs=in_specs, out_specs=out_specs, out_shape=out_shape,
                         scratch_shapes=[pltpu.VMEM((8, W), F32)], compiler_params=_cparams("parallel", "arbitrary"))(*args)
    return res if with_da else res[0]


S5_BAND = 4


def _mm_band(a, b, name, *, b_t=False, outer=False, epi=None, extras=(), tm=512, tk=2048):
    S = a.shape[0]
    wa = a.shape[1] // S5_BAND
    if outer:
        wb = b.shape[1] // S5_BAND
        tk = _tile(S, tk)
        nk = S // tk

        def obody(a_ref, b_ref, o_ref, acc_ref):
            k = pl.program_id(1)
            part = lax.dot_general(a_ref[...].astype(BF16), b_ref[...].astype(BF16), TN_DIMS, preferred_element_type=F32)

            @pl.when(k == 0)
            def _():
                acc_ref[...] = part

            @pl.when(k > 0)
            def _():
                acc_ref[...] += part

            @pl.when(k == nk - 1)
            def _():
                o_ref[...] = acc_ref[...]

        return pl.pallas_call(
            obody, name=name, grid=(S5_BAND, nk),
            in_specs=[pl.BlockSpec((tk, wa), lambda c, k: (k, c)), pl.BlockSpec((tk, wb), lambda c, k: (k, c))],
            out_specs=pl.BlockSpec((wa, wb), lambda c, k: (c, 0)), out_shape=jax.ShapeDtypeStruct((a.shape[1], wb), F32),
            scratch_shapes=[pltpu.VMEM((wa, wb), F32)], compiler_params=_cparams("parallel", "arbitrary"))(a, b)

    wo = (b.shape[0] if b_t else b.shape[1]) // S5_BAND
    tm = _tile(S, tm)
    ex_specs = [pl.BlockSpec((tm, wo), lambda i, c: (i, c)) if kind == "mn" else pl.BlockSpec((1, wo), lambda i, c: (0, c))
                for _, kind in extras]

    def body(a_ref, b_ref, *refs):
        part = lax.dot_general(a_ref[...].astype(BF16), b_ref[...].astype(BF16), NT_DIMS if b_t else _DIMS["nn"], preferred_element_type=F32)
        if epi is not None:
            part = epi(part, *[r[...] for r in refs[:-1]])
        refs[-1][...] = part

    b_spec = pl.BlockSpec((wo, wa) if b_t else (wa, wo), lambda i, c: (c, c))
    return pl.pallas_call(
        body, name=name, grid=(S // tm, S5_BAND), in_specs=[pl.BlockSpec((tm, wa), lambda i, c: (i, c)), b_spec] + ex_specs,
        out_specs=pl.BlockSpec((tm, wo), lambda i, c: (i, c)), out_shape=jax.ShapeDtypeStruct((S, S5_BAND * wo), F32),
        compiler_params=_cparams("parallel", "parallel"))(a, b, *[e[0] for e in extras])


def _band_to_full(blocks, cols):
    wa, wb = blocks.shape[0] // S5_BAND, blocks.shape[1]
    return jnp.concatenate([jnp.pad(blocks[k * wa:(k + 1) * wa], ((0, 0), (k * wb, cols - (k + 1) * wb))) for k in range(S5_BAND)], axis=0)


def _block_diag(t):
    G, a, b = t.shape
    return (t[:, :, None, :] * jnp.eye(G, dtype=t.dtype)[:, None, :, None]).reshape(G * a, G * b)


def _block_diag_take(m, G):
    a, b = m.shape[0] // G, m.shape[1] // G
    m4 = m.reshape(G, a, G, b)
    return jnp.sum(m4 * jnp.eye(G, dtype=m.dtype)[:, None, :, None], axis=2)


def _s5_block_fwd(u, w, pfx):
    a_re, a_im, bb_re, bb_im = _s5_discretise(w["lam_re"], w["lam_im"], w["log_dt"], w["b_re"], w["b_im"])
    bcat = _planes(_block_diag(bb_re).T, _block_diag(bb_im).T).astype(BF16)
    ccat = _planes(_block_diag(jnp.swapaxes(w["c_re"], 1, 2)).T, -_block_diag(jnp.swapaxes(w["c_im"], 1, 2)).T).T.astype(BF16)
    af_re, af_im = a_re.reshape(-1), a_im.reshape(-1)
    apow, ptab = _s5_scan_tables(af_re, af_im, False)
    bu = _mm_band(u, bcat, pfx + "_bu")
    x = _s5_scan(bu, apow, ptab, pfx + "_scan", False)
    d_row = w["d"].reshape(1, MIX_HALF)
    ys = _mm_band(x, ccat, pfx + "_y", epi=lambda acc, ut, dr: acc + dr * ut, extras=[(u, "mn"), (d_row, "n")])
    z = _mm(ys, w["w_glu"], "nn", pfx + "_glu", a_pro=_gelu, epi=lambda acc, b: acc + b, extras=[(w["b_glu"].reshape(1, -1), "n")])
    y2, = _ew(lambda ysv, zv: _gelu(ysv) * _sigmoid(zv), pfx + "_gate", [ys, z], outs=[(MIX_HALF, F32)])
    return y2, dict(u=u, x=x, ys=ys, z=z, bcat=bcat, ccat=ccat, a=(af_re, af_im), d_row=d_row)


def _s5_block_bwd(dy2, w, res, pfx):
    u, x, ys, z, bcat, ccat = res["u"], res["x"], res["ys"], res["z"], res["bcat"], res["ccat"]

    def gate_bwd(dy, ysv, zv):
        sg = _sigmoid(zv)
        dz = dy * _gelu(ysv) * sg * (1.0 - sg)
        return dz, jnp.sum(dz, axis=0, keepdims=True)

    dz, db_glu = _ew(gate_bwd, pfx + "_gate_bwd", [dy2, ys, z], outs=[(MIX_HALF, F32)], sums=[MIX_HALF])
    dw_glu = _mm(ys, dz, "tn", pfx + "_dwglu", a_pro=_gelu)
    dys = _mm(dz, w["w_glu"], "nt", pfx + "_dys", epi=lambda acc, dy, zv, ysv: (acc + dy * _sigmoid(zv)) * _dgelu(ysv),
              extras=[(dy2, "mn"), (z, "mn"), (ys, "mn")])
    dd, = _ew(lambda a, b: jnp.sum(a * b, axis=0, keepdims=True), pfx + "_dd", [dys, u], sums=[MIX_HALF])
    dccat = _band_to_full(_mm_band(x, dys, pfx + "_dc", outer=True), MIX_HALF)
    dx = _mm_band(dys, ccat, pfx + "_dx", b_t=True)
    af_re, af_im = res["a"]
    apow, ptab = _s5_scan_tables(af_re, -af_im, True)
    lam, da8 = _s5_scan(dx, apow, ptab, pfx + "_scan_bwd", True, x_fwd=x)
    dbcat = _band_to_full(_mm_band(u, lam, pfx + "_db", outer=True), 2 * S5_N)
    du = _mm_band(lam, bcat, pfx + "_du", b_t=True, epi=lambda acc, dyv, dr: acc + dyv * dr, extras=[(dys, "mn"), (res["d_row"], "n")])
    G = S5_GROUPS
    d_abar_re, d_abar_im = (t.reshape(G, S5_STATE) for t in _unplanes(jnp.sum(da8, axis=0)))
    d_bb_re, d_bb_im = (_block_diag_take(t.T, G) for t in _unplanes(dbcat))
    _, vjp = jax.vjp(_s5_discretise, w["lam_re"], w["lam_im"], w["log_dt"], w["b_re"], w["b_im"])
    g_lam_re, g_lam_im, g_log_dt, g_b_re, g_b_im = vjp((d_abar_re, d_abar_im, d_bb_re, d_bb_im))
    dc_re, dc_im = _unplanes(dccat.T)
    g_c_re = jnp.swapaxes(_block_diag_take(dc_re.T, G), 1, 2)
    g_c_im = -jnp.swapaxes(_block_diag_take(dc_im.T, G), 1, 2)
    grads = dict(lam_re=g_lam_re, lam_im=g_lam_im, log_dt=g_log_dt, b_re=g_b_re, b_im=g_b_im, c_re=g_c_re, c_im=g_c_im,
                 d=dd.reshape(G, S5_GROUP_WIDTH), w_glu=dw_glu, b_glu=db_glu.reshape(-1))
    return du, grads


SGU_TS = 512
N_PAIRS = MIX_HALF // LANES


def _half_masks(rows):
    lane = lax.broadcasted_iota(jnp.int32, (rows, LANES), 1)
    left = (lane < HEAD_DIM).astype(F32)
    return left, 1.0 - left


def _sgu_norm(zv, gain, bias):
    v = _gelu(zv)
    mu = jnp.mean(v, axis=-1, keepdims=True)
    vc = v - mu
    rstd = lax.rsqrt(jnp.mean(vc * vc, axis=-1, keepdims=True) + EPS)
    vhat = vc * rstd
    return vhat, rstd, vhat * gain + bias


def _sgu_tables(w_s, b_s):
    mask = jnp.tril(jnp.ones((SGU_CHUNK, SGU_CHUNK), dtype=bool))
    wm = jnp.where(mask[None], w_s, 0.0).astype(BF16)
    bias_tab = jnp.repeat(b_s.T, MIX_HALF // SGU_GROUPS, axis=1)
    return wm, bias_tab


def _sgu_fwd(proj, ln_gain, ln_bias, wm, bias_tab, name):
    S = proj.shape[0]
    nch = SGU_TS // SGU_CHUNK

    def body(zu_ref, zv_ref, g_ref, b_ref, w_ref, bt_ref, o_ref):
        left, right = _half_masks(SGU_CHUNK)
        _, _, vn = _sgu_norm(zv_ref[...], g_ref[...], b_ref[...])
        for ch in range(nch):
            rows = pl.ds(ch * SGU_CHUNK, SGU_CHUNK)
            for p in range(N_PAIRS):
                cols = pl.ds(p * LANES, LANES)
                vp = vn[ch * SGU_CHUNK:(ch + 1) * SGU_CHUNK, p * LANES:(p + 1) * LANES]
                mixed = (jnp.dot(w_ref[2 * p], (vp * left).astype(BF16), preferred_element_type=F32)
                         + jnp.dot(w_ref[2 * p + 1], (vp * right).astype(BF16), preferred_element_type=F32) + bt_ref[:, cols])
                o_ref[rows, cols] = _gelu(zu_ref[rows, cols]) * mixed

    vec = _vec_spec(MIX_HALF)
    return pl.pallas_call(
        body, name=name, grid=(S // SGU_TS,),
        in_specs=[pl.BlockSpec((SGU_TS, MIX_HALF), lambda i: (i, 1)), pl.BlockSpec((SGU_TS, MIX_HALF), lambda i: (i, 2)), vec, vec,
                  pl.BlockSpec((SGU_GROUPS, SGU_CHUNK, SGU_CHUNK), lambda i: (0, 0, 0)), pl.BlockSpec((SGU_CHUNK, MIX_HALF), lambda i: (0, 0))],
        out_specs=_row_spec(MIX_HALF, SGU_TS), out_shape=jax.ShapeDtypeStruct((S, MIX_HALF), F32),
        compiler_params=_cparams("parallel"))(proj, proj, ln_gain, ln_bias, wm, bias_tab)


def _sgu_bwd(dout, proj, ln_gain, ln_bias, wm, bias_tab, name):
    S = proj.shape[0]
    nch = SGU_TS // SGU_CHUNK
    nt_dims = (((1,), (1,)), ((), ()))
    tn_dims = (((0,), (0,)), ((), ()))

    def body(do_ref, zu_ref, zv_ref, g_ref, b_ref, w_ref, bt_ref, dzu_ref, dzv_ref, dw_ref, dbt_ref, dg_ref, db_ref, dvn_ref):
        first = pl.program_id(0) == 0

        @pl.when(first)
        def _():
            dw_ref[...] = jnp.zeros_like(dw_ref)
            dbt_ref[...] = jnp.zeros_like(dbt_ref)
            dg_ref[...] = jnp.zeros_like(dg_ref)
            db_ref[...] = jnp.zeros_like(db_ref)

        left, right = _half_masks(SGU_CHUNK)
        zv = zv_ref[...]
        vhat, rstd, vn = _sgu_norm(zv, g_ref[...], b_ref[...])
        for ch in range(nch):
            rows = pl.ds(ch * SGU_CHUNK, SGU_CHUNK)
            for p in range(N_PAIRS):
                cols = pl.ds(p * LANES, LANES)
                vp = vn[ch * SGU_CHUNK:(ch + 1) * SGU_CHUNK, p * LANES:(p + 1) * LANES]
                vl, vr = (vp * left).astype(BF16), (vp * right).astype(BF16)
                mixed = (jnp.dot(w_ref[2 * p], vl, preferred_element_type=F32)
                         + jnp.dot(w_ref[2 * p + 1], vr, preferred_element_type=F32) + bt_ref[:, cols])
                zu = zu_ref[rows, cols]
                do = do_ref[rows, cols]
                dzu_ref[rows, cols] = do * mixed * _dgelu(zu)
                dmix = do * _gelu(zu)
                dbt_ref[:, cols] += dmix
                dl, dr = (dmix * left).astype(BF16), (dmix * right).astype(BF16)
                dw_ref[2 * p] += lax.dot_general(dl, vl, nt_dims, preferred_element_type=F32)
                dw_ref[2 * p + 1] += lax.dot_general(dr, vr, nt_dims, preferred_element_type=F32)
                dvn_ref[rows, cols] = (lax.dot_general(w_ref[2 * p], dl, tn_dims, preferred_element_type=F32)
                                       + lax.dot_general(w_ref[2 * p + 1], dr, tn_dims, preferred_element_type=F32))
        dvn = dvn_ref[...]
        dg_ref[...] += jnp.sum(dvn * vhat, axis=0, keepdims=True)
        db_ref[...] += jnp.sum(dvn, axis=0, keepdims=True)
        dvh = dvn * g_ref[...]
        dv = rstd * (dvh - jnp.mean(dvh, axis=-1, keepdims=True) - vhat * jnp.mean(dvh * vhat, axis=-1, keepdims=True))
        dzv_ref[...] = dv * _dgelu(zv)

    vec = _vec_spec(MIX_HALF)
    row = _row_spec(MIX_HALF, SGU_TS)
    wspec = pl.BlockSpec((SGU_GROUPS, SGU_CHUNK, SGU_CHUNK), lambda i: (0, 0, 0))
    tspec = pl.BlockSpec((SGU_CHUNK, MIX_HALF), lambda i: (0, 0))
    full = jax.ShapeDtypeStruct((S, MIX_HALF), F32)
    v = jax.ShapeDtypeStruct((1, MIX_HALF), F32)
    return pl.pallas_call(
        body, name=name, grid=(S // SGU_TS,),
        in_specs=[row, pl.BlockSpec((SGU_TS, MIX_HALF), lambda i: (i, 1)), pl.BlockSpec((SGU_TS, MIX_HALF), lambda i: (i, 2)), vec, vec,
                  wspec, tspec],
        out_specs=[row, row, wspec, tspec, vec, vec],
        out_shape=[full, full, jax.ShapeDtypeStruct((SGU_GROUPS, SGU_CHUNK, SGU_CHUNK), F32),
                   jax.ShapeDtypeStruct((SGU_CHUNK, MIX_HALF), F32), v, v],
        scratch_shapes=[pltpu.VMEM((SGU_TS, MIX_HALF), F32)],
        compiler_params=_cparams("arbitrary"))(dout, proj, proj, ln_gain, ln_bias, wm, bias_tab)


def _sgu_grads(dw, dbias_tab):
    mask = jnp.tril(jnp.ones((SGU_CHUNK, SGU_CHUNK), dtype=bool))
    g_w = jnp.where(mask[None], dw, 0.0)
    g_b = dbias_tab.reshape(SGU_CHUNK, SGU_GROUPS, MIX_HALF // SGU_GROUPS).sum(axis=-1).T
    return g_w, g_b


def _head_avg_matrix(w):
    idx = np.arange(w) // HEAD_DIM
    return jnp.asarray((idx[:, None] == idx[None, :]).astype(np.float32) / HEAD_DIM, dtype=BF16)


def _head_mean(t, bavg):
    hi = t.astype(BF16)
    lo = (t - hi.astype(F32)).astype(BF16)
    return jnp.dot(hi, bavg, preferred_element_type=F32) + jnp.dot(lo, bavg, preferred_element_type=F32)


def _head_rms(t, bavg):
    r = lax.rsqrt(_head_mean(t * t, bavg) + EPS)
    return t * r, r


def _head_rms_bwd(dn, n, r, bavg):
    return r * (dn - n * _head_mean(dn * n, bavg))


GLA_TS = 512
C = GLA_CHUNK
NT_DIMS = (((1,), (1,)), ((), ()))
TN_DIMS = (((0,), (0,)), ((), ()))
HI = lax.Precision.HIGHEST


def _bdot(a, b, dims=(((1,), (0,)), ((), ()))):
    return lax.dot_general(a.astype(BF16), b.astype(BF16), dims, preferred_element_type=F32)


def _gla_chunk_terms(q, k, z):
    row = lax.broadcasted_iota(jnp.int32, (C, C), 0)
    col = lax.broadcasted_iota(jnp.int32, (C, C), 1)
    lc = _log_sigmoid(z) * (1.0 / GLA_TAU)
    b = lax.dot_general((row >= col).astype(F32), lc, (((1,), (0,)), ((), ())), precision=HI, preferred_element_type=F32)
    b_last = jnp.sum(lc, axis=0, keepdims=True)
    b_mid = b[C // 2:C // 2 + 1, :]
    scale = HEAD_DIM ** -0.5
    e_b, e_q, e_k, e_l = jnp.exp(b), jnp.exp(b - b_mid), jnp.exp(b_mid - b), jnp.exp(b_last - b)
    qs = q * (scale * e_b)
    qe = q * (scale * e_q)
    ke = k * e_k
    kl = k * e_l
    return dict(e_b=e_b, e_q=e_q, e_k=e_k, e_l=e_l, qs=qs, qe=qe, ke=ke, kl=kl, dec=jnp.exp(b_last), causal=row >= col, scale=scale)


def _pair(x, pp):
    return x[:, pp * LANES:(pp + 1) * LANES]


def _pair_block_diag():
    r = lax.broadcasted_iota(jnp.int32, (LANES, LANES), 0) // HEAD_DIM
    c = lax.broadcasted_iota(jnp.int32, (LANES, LANES), 1) // HEAD_DIM
    return (r == c).astype(F32)


def _gla_fwd(proj, z, name):
    S = proj.shape[0]
    nch = GLA_TS // C

    def body(q_ref, k_ref, v_ref, z_ref, o_ref, st_ref, state_ref):
        @pl.when(pl.program_id(0) == 0)
        def _():
            state_ref[...] = jnp.zeros_like(state_ref)

        left, right = _half_masks(C)
        bd = _pair_block_diag()
        pairs = range(N_PAIRS)
        for ch in range(nch):
            rows = pl.ds(ch * C, C)
            v = v_ref[rows, :]
            t = _gla_chunk_terms(q_ref[rows, :], k_ref[rows, :], z_ref[rows, :])
            sts = [state_ref[pp] for pp in pairs]
            for pp in pairs:
                st_ref[ch, pp] = sts[pp]
            os = [_bdot(_pair(t["qs"], pp), sts[pp], NT_DIMS) for pp in pairs]
            for m in (left, right):
                scores = [jnp.where(t["causal"], _bdot(_pair(t["qe"], pp) * m, _pair(t["ke"], pp), NT_DIMS), 0.0) for pp in pairs]
                os = [os[pp] + m * _bdot(scores[pp], _pair(v, pp)) for pp in pairs]
            o_ref[rows, :] = jnp.concatenate(os, axis=1)
            new = [sts[pp] * _pair(t["dec"], pp) + bd * _bdot(_pair(v, pp), _pair(t["kl"], pp), TN_DIMS) for pp in pairs]
            for pp in pairs:
                state_ref[pp] = new[pp]

    def col(cb):
        return pl.BlockSpec((GLA_TS, MIX_HALF), lambda i: (i, cb))

    return pl.pallas_call(
        body, name=name, grid=(S // GLA_TS,),
        in_specs=[col(0), col(1), col(2), col(0)],
        out_specs=[col(0), pl.BlockSpec((nch, N_PAIRS, LANES, LANES), lambda i: (i, 0, 0, 0))],
        out_shape=[jax.ShapeDtypeStruct((S, MIX_HALF), F32), jax.ShapeDtypeStruct((S // C, N_PAIRS, LANES, LANES), F32)],
        scratch_shapes=[pltpu.VMEM((N_PAIRS, LANES, LANES), F32)], compiler_params=_cparams("arbitrary"))(proj, proj, proj, z)


def _gla_bwd(do, proj, z, states, name):
    S = proj.shape[0]
    nch = GLA_TS // C
    nblk = S // GLA_TS

    def body(do_ref, q_ref, k_ref, v_ref, z_ref, st_ref, dq_ref, dk_ref, dv_ref, dlc_ref, dstate_ref):
        @pl.when(pl.program_id(0) == 0)
        def _():
            dstate_ref[...] = jnp.zeros_like(dstate_ref)

        left, right = _half_masks(C)
        bd = _pair_block_diag()
        rowi = lax.broadcasted_iota(jnp.int32, (C, LANES), 0)
        row = lax.broadcasted_iota(jnp.int32, (C, C), 0)
        colm = lax.broadcasted_iota(jnp.int32, (C, C), 1)
        pairs = range(N_PAIRS)
        rowi = lax.broadcasted_iota(jnp.int32, (C, MIX_HALF), 0)
        for ch in range(nch - 1, -1, -1):
            rows = pl.ds(ch * C, C)
            v, dov = v_ref[rows, :], do_ref[rows, :]
            t = _gla_chunk_terms(q_ref[rows, :], k_ref[rows, :], z_ref[rows, :])
            sts = [st_ref[ch, pp] for pp in pairs]
            nxt = [dstate_ref[pp] for pp in pairs]
            gs = [bd * nxt[pp] for pp in pairs]
            dqs = [_bdot(_pair(dov, pp), sts[pp]) for pp in pairs]
            dv = [_bdot(_pair(t["kl"], pp), gs[pp], NT_DIMS) for pp in pairs]
            dkl = [_bdot(_pair(v, pp), gs[pp]) for pp in pairs]
            dqe = [jnp.zeros((C, LANES), F32) for _ in pairs]
            dke = [jnp.zeros((C, LANES), F32) for _ in pairs]
            for m in (left, right):
                sc = [jnp.where(t["causal"], _bdot(_pair(t["qe"], pp) * m, _pair(t["ke"], pp), NT_DIMS), 0.0) for pp in pairs]
                dsc = [jnp.where(t["causal"], _bdot(_pair(dov, pp) * m, _pair(v, pp), NT_DIMS), 0.0) for pp in pairs]
                dv = [dv[pp] + m * _bdot(sc[pp], _pair(dov, pp), TN_DIMS) for pp in pairs]
                dqe = [dqe[pp] + m * _bdot(dsc[pp], _pair(t["ke"], pp)) for pp in pairs]
                dke = [dke[pp] + m * _bdot(dsc[pp], _pair(t["qe"], pp), TN_DIMS) for pp in pairs]
            for pp in pairs:
                dstate_ref[pp] = bd * (nxt[pp] * _pair(t["dec"], pp) + _bdot(_pair(dov, pp), _pair(t["qs"], pp), TN_DIMS))
            decay_sum = jnp.concatenate([jnp.sum(nxt[pp] * sts[pp], axis=0, keepdims=True) for pp in pairs], axis=1)
            dqs, dv, dkl, dqe, dke = (jnp.concatenate(parts, axis=1) for parts in (dqs, dv, dkl, dqe, dke))
            db_last = decay_sum * t["dec"] + jnp.sum(dkl * t["kl"], axis=0, keepdims=True)
            db = dqs * t["qs"] + dqe * t["qe"] - dke * t["ke"] - dkl * t["kl"]
            db = db + jnp.where(rowi == C - 1, db_last, 0.0)
            dq_ref[rows, :] = (dqs * t["e_b"] + dqe * t["e_q"]) * t["scale"]
            dk_ref[rows, :] = dke * t["e_k"] + dkl * t["e_l"]
            dv_ref[rows, :] = dv
            dlc_ref[rows, :] = lax.dot_general((colm >= row).astype(F32), db, (((1,), (0,)), ((), ())), precision=HI,
                                               preferred_element_type=F32)

    def col(cb):
        return pl.BlockSpec((GLA_TS, MIX_HALF), lambda i: (nblk - 1 - i, cb))

    full = jax.ShapeDtypeStruct((S, MIX_HALF), F32)
    return pl.pallas_call(
        body, name=name, grid=(nblk,),
        in_specs=[col(0), col(0), col(1), col(2), col(0), pl.BlockSpec((nch, N_PAIRS, LANES, LANES), lambda i: (nblk - 1 - i, 0, 0, 0))],
        out_specs=[col(0)] * 4, out_shape=[full, full, full, full],
        scratch_shapes=[pltpu.VMEM((N_PAIRS, LANES, LANES), F32)], compiler_params=_cparams("arbitrary"))(do, proj, proj, proj, z, states)


def _gla_block_fwd(proj, w_lr_pad, b_lr, gain, bavg, pfx):
    z = _mm(proj, w_lr_pad, "nn", pfx + "_z", a_cols=(7 * MIX_HALF, MIX_HALF), epi=lambda acc, b: acc + b, extras=[(b_lr, "n")])
    o, states = _gla_fwd(proj, z, pfx + "_core")

    def out(ov, gg, ba, gn):
        n, _ = _head_rms(ov, ba)
        return n * gn * (gg * _sigmoid(gg))

    og, = _ew(out, pfx + "_out", [o, (proj, MIX_HALF, 3)], consts=[bavg, gain], outs=[(MIX_HALF, F32)])
    return og, dict(z=z, o=o, states=states)


def _gla_block_bwd(dog, proj, w_lr_pad, gain, bavg, res, pfx):
    z, o, states = res["z"], res["o"], res["states"]

    def out_bwd(dy, ov, gg, ba, gn):
        n, r = _head_rms(ov, ba)
        sg = _sigmoid(gg)
        silu = gg * sg
        dn = dy * gn * silu
        do = _head_rms_bwd(dn, n, r, ba)
        dgg = dy * n * gn * (sg * (1.0 + gg * (1.0 - sg)))
        return do, dgg, jnp.sum(dy * n * silu, axis=0, keepdims=True)

    do, dgg, dgain = _ew(out_bwd, pfx + "_out_bwd", [dog, o, (proj, MIX_HALF, 3)], consts=[bavg, gain],
                         outs=[(MIX_HALF, F32), (MIX_HALF, F32)], sums=[MIX_HALF])
    dq, dk, dv, dlc = _gla_bwd(do, proj, z, states, pfx + "_core_bwd")

    def decay_bwd(dl, zv):
        dz = dl * (1.0 / GLA_TAU) * (1.0 - _sigmoid(zv))
        return dz, jnp.sum(dz, axis=0, keepdims=True)

    dz, db_lr = _ew(decay_bwd, pfx + "_decay_bwd", [dlc, z], outs=[(MIX_HALF, F32)], sums=[MIX_HALF])
    dw_lr_pad = _mm(proj, dz, "tn", pfx + "_dwlr", a_cols=(7 * MIX_HALF, MIX_HALF))
    dsmall = _mm(dz, w_lr_pad, "nt", pfx + "_dsmall")
    return (dq, dk, dv, dgg, dsmall), dict(w_lr=dw_lr_pad[:GLA_RANK], b_lr=db_lr.reshape(-1), gain=dgain.reshape(-1, HEAD_DIM))


FOX_T = 512
FOX_HEADS = MIX_HALF // HEAD_DIM
NEG = -1e30
CUM_T = 512


def _cum_lanes(x, name, reverse, pre=None):
    R, S = x.shape
    nb = S // CUM_T

    def body(x_ref, o_ref, carry_ref):
        @pl.when(pl.program_id(0) == 0)
        def _():
            carry_ref[...] = jnp.zeros_like(carry_ref)

        xv = x_ref[...]
        if pre is not None:
            xv = pre(xv)
        i = lax.broadcasted_iota(jnp.int32, (CUM_T, CUM_T), 0)
        j = lax.broadcasted_iota(jnp.int32, (CUM_T, CUM_T), 1)
        tri = ((i >= j) if reverse else (i <= j)).astype(F32)
        c = lax.dot_general(xv, tri, (((1,), (0,)), ((), ())), precision=HI, preferred_element_type=F32)
        carry = carry_ref[...]
        o_ref[...] = c + carry[:, 0:1]
        carry_ref[...] = carry + jnp.sum(xv, axis=1, keepdims=True)

    spec = pl.BlockSpec((R, CUM_T), (lambda i: (0, nb - 1 - i)) if reverse else (lambda i: (0, i)))
    return pl.pallas_call(body, name=name, grid=(nb,), in_specs=[spec], out_specs=spec, out_shape=jax.ShapeDtypeStruct((R, S), F32),
                          scratch_shapes=[pltpu.VMEM((R, LANES), F32)], compiler_params=_cparams("arbitrary"))(x)


def _fox_scores(q, k, cqb, ck_ref, h, m, diag):
    cq = cqb[:, h * HEAD_DIM:h * HEAD_DIM + 1]
    ck = ck_ref[0, h:h + 1, :]
    s = lax.dot_general(q * m.astype(q.dtype), k, NT_DIMS, preferred_element_type=F32) + (cq - ck)
    if not diag:
        return s
    row = lax.broadcasted_iota(jnp.int32, (FOX_T, FOX_T), 0)
    col = lax.broadcasted_iota(jnp.int32, (FOX_T, FOX_T), 1)
    return jnp.where(row < col, NEG, s)


def _on_causal_blocks(q_blk, k_blk, step):
    @pl.when(k_blk < q_blk)
    def _():
        step(False)

    @pl.when(k_blk == q_blk)
    def _():
        step(True)


def _causal_pairs(n, key_major):
    if key_major:
        pairs = [(q, k) for k in range(n) for q in range(k, n)]
    else:
        pairs = [(q, k) for q in range(n) for k in range(q + 1)]
    return jnp.asarray([p[0] for p in pairs], jnp.int32), jnp.asarray([p[1] for p in pairs], jnp.int32)


def _carried(carry, refs, n_in, n_out, first, last):
    if carry is None:
        return refs
    ins, cx_ref, outs, co_ref = refs[:n_in], refs[n_in], refs[n_in + 1:n_in + 1 + n_out], refs[n_in + 1 + n_out]
    scratch = refs[n_in + 2 + n_out:]
    start, finish = _exchange_plan(cx_ref, co_ref, *scratch[-3:], carry[1])
    pl.when(first)(start)
    pl.when(last)(finish)
    return ins + outs + scratch[:-3]


def _carry_specs(carry):
    if carry is None:
        return [], [], [], [], []
    x, bcast = carry
    blk = x.shape if bcast else x.shape[1:]
    return [ANY], [ANY], [jax.ShapeDtypeStruct((N_CHIPS,) + tuple(blk), x.dtype)], list(_EXCHANGE_SEMS), [x]


def _fox_fwd(qn, kn, proj, cum_b, cum_tp, name, carry=None):
    S = qn.shape[0]
    nq = S // FOX_T
    qidx, kidx = _causal_pairs(nq, False)
    ntri = int(qidx.shape[0])

    def body(qidx_ref, kidx_ref, *refs):
        t = pl.program_id(1)
        first = jnp.logical_and(pl.program_id(0) == 0, t == 0)
        last = jnp.logical_and(pl.program_id(0) == N_PAIRS - 1, t == ntri - 1)
        q_ref, k_ref, v_ref, cq_ref, ck_ref, o_ref, lse_ref, m_scr, acc_scr = _carried(carry, refs, 5, 2, first, last)
        qi, ki = qidx_ref[t], kidx_ref[t]

        @pl.when(ki == 0)
        def _():
            m_scr[...] = jnp.full_like(m_scr, NEG)
            acc_scr[...] = jnp.zeros_like(acc_scr)

        left, right = _half_masks(FOX_T)

        def step(diag):
            q, k, v = q_ref[...], k_ref[...], v_ref[...].astype(BF16)
            cqb = cq_ref[...]
            for h, m in enumerate((left, right)):
                s = _fox_scores(q, k, cqb, ck_ref, h, m, diag)
                m_prev = m_scr[h]
                m_new = jnp.maximum(m_prev, jnp.max(s, axis=1, keepdims=True))
                p = jnp.exp(s - m_new)
                v_h = jnp.where(m > 0, v, jnp.ones_like(v))
                acc_scr[h] = jnp.exp(m_prev - m_new) * acc_scr[h] + jnp.dot(p.astype(BF16), v_h, preferred_element_type=F32)
                m_scr[h] = m_new

        _on_causal_blocks(qi, ki, step)

        @pl.when(ki == qi)
        def _():
            a0, a1 = acc_scr[0], acc_scr[1]
            is_left = left > 0
            num = jnp.where(is_left, a0, a1)
            den = jnp.where(is_left, pltpu.roll(a0, HEAD_DIM, 1), pltpu.roll(a1, HEAD_DIM, 1))
            o_ref[...] = num / den
            lse_ref[...] = jnp.where(is_left, m_scr[0], m_scr[1]) + jnp.log(den)

    qspec = pl.BlockSpec((FOX_T, LANES), lambda p, t, qx, kx: (qx[t], p))
    kspec = pl.BlockSpec((FOX_T, LANES), lambda p, t, qx, kx: (kx[t], p))
    vspec = pl.BlockSpec((FOX_T, LANES), lambda p, t, qx, kx: (kx[t], 6 * N_PAIRS + p))
    ckspec = pl.BlockSpec((1, 8, FOX_T), lambda p, t, qx, kx: (p, 0, kx[t]))
    full = jax.ShapeDtypeStruct((S, MIX_HALF), F32)
    c_in, c_out, c_shape, c_scratch, c_args = _carry_specs(carry)
    grid_spec = pltpu.PrefetchScalarGridSpec(
        num_scalar_prefetch=2, grid=(N_PAIRS, ntri), in_specs=[qspec, kspec, vspec, qspec, ckspec] + c_in, out_specs=[qspec, qspec] + c_out,
        scratch_shapes=[pltpu.VMEM((2, FOX_T, 1), F32), pltpu.VMEM((2, FOX_T, LANES), F32)] + c_scratch)
    return pl.pallas_call(body, name=name, grid_spec=grid_spec, out_shape=[full, full] + c_shape,
                          compiler_params=_cparams("arbitrary", "arbitrary"))(qidx, kidx, qn, kn, proj, cum_b, cum_tp, *c_args)


def _fox_bwd(do, qn, kn, proj, cum_b, cum_tp, lse_b, delta_b, name, carry=None, do_pair0=0):
    S = qn.shape[0]
    nq = S // FOX_T
    scale = HEAD_DIM ** -0.5
    qidx, kidx = _causal_pairs(nq, True)
    ntri = int(qidx.shape[0])

    def body(qidx_ref, kidx_ref, *refs):
        t = pl.program_id(1)
        first = jnp.logical_and(pl.program_id(0) == 0, t == 0)
        last = jnp.logical_and(pl.program_id(0) == N_PAIRS - 1, t == ntri - 1)
        (do_ref, q_ref, k_ref, v_ref, cq_ref, ck_ref, lse_ref, dl_ref, dq_ref, dcq_ref, dk_ref, dv_ref, dck_ref,
         dq_scr, dk_scr, dv_scr) = _carried(carry, refs, 8, 5, first, last)
        qi, ki = qidx_ref[t], kidx_ref[t]

        @pl.when(t == 0)
        def _():
            dq_scr[...] = jnp.zeros_like(dq_scr)

        @pl.when(qi == ki)
        def _():
            dk_scr[...] = jnp.zeros_like(dk_scr)
            dv_scr[...] = jnp.zeros_like(dv_scr)

        left, right = _half_masks(FOX_T)
        rows = pl.ds(pl.multiple_of(qi * FOX_T, FOX_T), FOX_T)

        def step(diag):
            q, k, v, dov = q_ref[...], k_ref[...], v_ref[...].astype(BF16), do_ref[...]
            cqb, lseb, dlb = cq_ref[...], lse_ref[...], dl_ref[...]
            dob = dov.astype(BF16)
            heads = (0, 1)
            masks = (left, right)
            col = [slice(h * HEAD_DIM, h * HEAD_DIM + 1) for h in heads]
            ss = [_fox_scores(q, k, cqb, ck_ref, h, masks[h], diag) for h in heads]
            dps = [lax.dot_general((dov * masks[h]).astype(BF16), v, NT_DIMS, preferred_element_type=F32) for h in heads]
            ps = [jnp.exp(ss[h] - lseb[:, col[h]]) for h in heads]
            dss = [(ps[h] * (dps[h] - dlb[:, col[h]])).astype(BF16) for h in heads]
            pvs = [lax.dot_general(ps[h].astype(BF16), dob, TN_DIMS, preferred_element_type=F32) for h in heads]
            dks = [lax.dot_general(dss[h], jnp.where(masks[h] > 0, q, jnp.ones_like(q)), TN_DIMS, preferred_element_type=F32) for h in heads]
            dqs = [jnp.dot(dss[h], jnp.where(masks[h] > 0, k, jnp.ones_like(k)), preferred_element_type=F32) for h in heads]
            dv_scr[...] = dv_scr[...] + left * pvs[0] + right * pvs[1]
            for h in heads:
                dk_scr[h] = dk_scr[h] + dks[h]
                dq_scr[h, rows, :] = dq_scr[h, rows, :] + dqs[h]

        _on_causal_blocks(qi, ki, step)

        @pl.when(qi == nq - 1)
        def _():
            a0, a1 = dk_scr[0], dk_scr[1]
            dk_ref[...] = left * a0 + right * a1
            dv_ref[...] = dv_scr[...]
            dck_ref[...] = left * pltpu.roll(a0, HEAD_DIM, 1) + right * pltpu.roll(a1, HEAD_DIM, 1)

        @pl.when(t == ntri - 1)
        def _():
            for r in range(nq):
                blk = pl.ds(r * FOX_T, FOX_T)
                a0, a1 = dq_scr[0, blk, :], dq_scr[1, blk, :]
                dq_ref[blk, :] = (left * a0 + right * a1) * scale
                dcq_ref[blk, :] = left * pltpu.roll(a0, HEAD_DIM, 1) + right * pltpu.roll(a1, HEAD_DIM, 1)

    qspec = pl.BlockSpec((FOX_T, LANES), lambda p, t, qx, kx: (qx[t], p))
    kspec = pl.BlockSpec((FOX_T, LANES), lambda p, t, qx, kx: (kx[t], p))
    vspec = pl.BlockSpec((FOX_T, LANES), lambda p, t, qx, kx: (kx[t], 6 * N_PAIRS + p))
    ckspec = pl.BlockSpec((1, 8, FOX_T), lambda p, t, qx, kx: (p, 0, kx[t]))
    seq = pl.BlockSpec((S, LANES), lambda p, t, qx, kx: (0, p))
    full = jax.ShapeDtypeStruct((S, MIX_HALF), F32)
    c_in, c_out, c_shape, c_scratch, c_args = _carry_specs(carry)
    grid_spec = pltpu.PrefetchScalarGridSpec(
        num_scalar_prefetch=2, grid=(N_PAIRS, ntri),
        in_specs=[pl.BlockSpec((FOX_T, LANES), lambda p, t, qx, kx: (qx[t], do_pair0 + p)), qspec, kspec, vspec, qspec, ckspec, qspec, qspec] + c_in,
        out_specs=[seq, seq, kspec, kspec, kspec] + c_out,
        scratch_shapes=[pltpu.VMEM((2, S, LANES), F32), pltpu.VMEM((2, FOX_T, LANES), F32), pltpu.VMEM((FOX_T, LANES), F32)] + c_scratch)
    return pl.pallas_call(body, name=name, grid_spec=grid_spec, out_shape=[full] * 5 + c_shape,
                          compiler_params=_cparams("arbitrary", "arbitrary"))(qidx, kidx, do, qn, kn, proj, cum_b, cum_tp, lse_b, delta_b, *c_args)


def _ff_bwd(rc, f_t, name):
    def body(rc_ref, f_ref, d_ref, s_ref):
        d = rc_ref[...] * (1.0 - _sigmoid(f_ref[...]))
        d_ref[...] = d
        s_ref[...] = jnp.sum(d, axis=1, keepdims=True)

    return pl.pallas_call(body, name=name, out_shape=[jax.ShapeDtypeStruct(rc.shape, F32), jax.ShapeDtypeStruct((rc.shape[0], 1), F32)])(rc, f_t)


def _fox_block_fwd(proj, b_f, q_gain, k_gain, bavg, pfx, carry=None):
    S = proj.shape[0]

    def prep(qv, kv, ba, qg, kg):
        return _head_rms(qv, ba)[0] * qg * (HEAD_DIM ** -0.5), _head_rms(kv, ba)[0] * kg

    qn, kn = _ew(prep, pfx + "_prep", [(proj, MIX_HALF, 4), (proj, MIX_HALF, 5)], consts=[bavg, q_gain, k_gain],
                 outs=[(MIX_HALF, BF16), (MIX_HALF, BF16)])
    f0 = 7 * MIX_HALF + GLA_RANK
    f_t = proj[:, f0:f0 + FOX_HEADS].T + b_f.reshape(FOX_HEADS, 1)
    cum = _cum_lanes(f_t, pfx + "_cum", False, pre=_log_sigmoid)
    cum_b = jnp.repeat(cum.T, HEAD_DIM, axis=1)
    cum_tp = jnp.pad(cum.reshape(N_PAIRS, 2, S), ((0, 0), (0, 6), (0, 0)))
    o, lse_b, *carried = _fox_fwd(qn, kn, proj, cum_b, cum_tp, pfx + "_attn", carry=carry)
    return o, dict(qn=qn, kn=kn, f_t=f_t, cum_b=cum_b, cum_tp=cum_tp, o=o, lse_b=lse_b), carried


def _fox_block_bwd(do, proj, q_gain, k_gain, bavg, res, pfx, carry=None):
    qn, kn, o = res["qn"], res["kn"], res["o"]
    S = proj.shape[0]
    delta_b, = _ew(lambda a, b, ba: _head_mean(a * b, ba) * float(HEAD_DIM), pfx + "_delta", [do, o], consts=[bavg], outs=[(MIX_HALF, F32)])
    do_arr, do_blk = (do[0], do[2]) if isinstance(do, tuple) else (do, 0)
    args = (do_arr, qn, kn, proj, res["cum_b"], res["cum_tp"], res["lse_b"], delta_b)
    dqn, dcq_b, dkn, dv, dck_b, *carried = _fox_bwd(*args, pfx + "_bwd", carry=carry, do_pair0=do_blk * N_PAIRS)

    def prep_bwd(dq, dk, qv, kv, ba, qg, kg):
        nq, rq = _head_rms(qv, ba)
        nk, rk = _head_rms(kv, ba)
        return (_head_rms_bwd(dq * qg, nq, rq, ba), _head_rms_bwd(dk * kg, nk, rk, ba),
                jnp.sum(dq * nq, axis=0, keepdims=True), jnp.sum(dk * nk, axis=0, keepdims=True))

    dfq, dfk, dqg, dkg = _ew(prep_bwd, pfx + "_prep_bwd", [dqn, dkn, (proj, MIX_HALF, 4), (proj, MIX_HALF, 5)],
                             consts=[bavg, q_gain, k_gain], outs=[(MIX_HALF, F32), (MIX_HALF, F32)], sums=[MIX_HALF, MIX_HALF])
    dcum = (dcq_b - dck_b)[:, ::HEAD_DIM].T
    rc = _cum_lanes(dcum, pfx + "_rcum", True)
    dff_t, db_f = _ff_bwd(rc, res["f_t"], pfx + "_ff_bwd")
    grads = dict(b_f=db_f.reshape(-1), q_gain=dqg.reshape(-1, HEAD_DIM), k_gain=dkg.reshape(-1, HEAD_DIM))
    return (dfq, dfk, dv, dff_t.T), grads, carried


WEIGHTS = ['ada_w', 'ada_b', 'even_w_in', 'even_w_out', 'gla_w_lr', 'gla_b_lr', 'gla_gain', 'fox_b_f', 'fox_q_gain', 'fox_k_gain',
           'odd_w_in', 'odd_w_out', 's5_lam_re', 's5_lam_im', 's5_log_dt', 's5_b_re', 's5_b_im', 's5_c_re', 's5_c_im', 's5_d',
           's5_w_glu', 's5_b_glu', 'sgu_ln_gain', 'sgu_ln_bias', 'sgu_w_s', 'sgu_b_s', 'mlp_w1', 'mlp_w2']
ARGS = ['x', 'c'] + WEIGHTS + ['loss_target'] + ['m_' + w for w in WEIGHTS] + ['v_' + w for w in WEIGHTS]

EVEN_COLS = 3608
EVEN_PAD = 8 * MIX_HALF
MOD = 6 * D_MODEL
MOD_SHARD = MOD // N_CHIPS

PACK_COLS = 1024
EVEN_SHARD = EVEN_COLS // N_CHIPS
SHARDED = (
    ([("even_w_in", (1, 1024, PACK_COLS), 2), ("even_w_out", (1, 256, 1024), 1), ("gla_w_lr", (1, 16, 128), 2)], 1536),
    ([("mlp_w1_0", (1, 1024, 1024), 2), ("mlp_w2_0", (1, 1024, 1024), 1), ("odd_w_in", (1, 1024, 384), 2),
      ("odd_w_out", (1, 256, 1024), 1), ("mlp_w1_1", (1, 1024, 1024), 2), ("mlp_w2_1", (1, 1024, 1024), 1),
      ("s5_w_glu", (1, 128, 512), 1), ("s5_b_glu", (1, 128), 1), ("sgu_ln_gain", (1, 128), 1), ("sgu_ln_bias", (1, 128), 1)], 5120))
REPLICATED = [("gla_b_lr", (1, 512)), ("gla_gain", (1, 8, 64)), ("fox_b_f", (1, 8)), ("fox_q_gain", (1, 8, 64)),
              ("fox_k_gain", (1, 8, 64)), ("s5_lam_re", (1, 32, 64)), ("s5_lam_im", (1, 32, 64)), ("s5_log_dt", (1, 32)),
              ("s5_b_re", (1, 32, 64, 16)), ("s5_b_im", (1, 32, 64, 16)), ("s5_c_re", (1, 32, 16, 64)), ("s5_c_im", (1, 32, 16, 64)),
              ("s5_d", (1, 32, 16)), ("sgu_w_s", (1, 8, 128, 128)), ("sgu_b_s", (1, 8, 128))]
SMALL_ROWS = 512
BIG_ADAM = {"ada_w": (2048, 1536), "even_w_in": (1024, 902), "even_w_out": (256, 1024), "odd_w_in": (1024, 384),
            "odd_w_out": (256, 1024), "mlp_w1": (2048, 1024), "mlp_w2": (2048, 1024), "s5_w_glu": (128, 512)}


PACK_ALIGN = 16


def _piece_rows(shape):
    rows = -(-math.prod(shape) // PACK_COLS)
    return -(-rows // PACK_ALIGN) * PACK_ALIGN


def _to_rows(p, lead=()):
    n = math.prod(p.shape[len(lead):])
    rows = _piece_rows(p.shape[len(lead):])
    flat = p.reshape(lead + (n,))
    if rows * PACK_COLS != n:
        flat = jnp.pad(flat, [(0, 0)] * len(lead) + [(0, rows * PACK_COLS - n)])
    return flat.reshape(lead + (rows, PACK_COLS))


def _from_rows(x, r0, shape, lead=()):
    n = math.prod(shape)
    seg = lax.slice_in_dim(x, r0, r0 + _piece_rows(shape), axis=len(lead)).reshape(lead + (-1,))
    return lax.slice_in_dim(seg, 0, n, axis=len(lead)).reshape(lead + tuple(shape))


def _pack_rows(pieces, rows):
    x = jnp.concatenate([_to_rows(p) for p in pieces], axis=0)
    return jnp.pad(x, ((0, rows - x.shape[0]), (0, 0)))


def _unpack(x, specs):
    out, r0 = {}, 0
    for name, shape in specs:
        out[name] = _from_rows(x, r0, shape)
        r0 += _piece_rows(shape)
    return out


def _shards_to_full(x4, pieces):
    out, r0 = {}, 0
    for name, shape, axis in pieces:
        seg = _from_rows(x4, r0, shape, lead=(N_CHIPS,))
        out[name] = jnp.concatenate([seg[k] for k in range(N_CHIPS)], axis=axis)
        r0 += _piece_rows(shape)
    return out


def _full_to_shards(full, pieces, rows):
    blocks = [_to_rows(jnp.stack(jnp.split(full[name], N_CHIPS, axis=axis)), lead=(N_CHIPS,)) for name, _, axis in pieces]
    x = jnp.concatenate(blocks, axis=1)
    return jnp.pad(x, ((0, 0), (0, rows - x.shape[1]), (0, 0)))


def _gather_prep(local, pieces, rows):
    shard = _pack_rows([local[n] for n, _, _ in pieces], rows).astype(BF16)
    return lax.dynamic_slice_in_dim(shard, lax.axis_index("c") * (rows // 2), rows // 2, axis=0)


def _gather_finish(collected, pieces, rows, tag):
    halves = _by_core(collected, _pair_swap(collected, tag + "_pair"))
    return _shards_to_full(halves.transpose(1, 0, 2, 3).reshape(N_CHIPS, rows, PACK_COLS), pieces)


def _reduce_prep(full, pieces, rows, tag):
    mc = lax.axis_index("c")
    packed = _full_to_shards(full, pieces, rows)
    hr = rows // 2
    mine = lax.dynamic_slice_in_dim(packed, mc * hr, hr, axis=1)
    other = lax.dynamic_slice_in_dim(packed, (1 - mc) * hr, hr, axis=1)
    theirs = _pair_swap(other.astype(BF16), tag + "_pair")
    pair_sum, = _ew(lambda p, q: p + q, tag + "_pair_sum", [mine.reshape(N_CHIPS * hr, PACK_COLS), theirs.reshape(N_CHIPS * hr, PACK_COLS)],
                    outs=[(PACK_COLS, BF16)])
    return pair_sum.reshape(N_CHIPS, hr, PACK_COLS)


def _reduce_finish(arrived, pieces, rows, tag):
    red_half = _sum_slots(arrived, tag + "_chip_sum")
    reduced = _by_core(red_half, _pair_swap(red_half, tag + "_pair_out")).reshape(rows, PACK_COLS)
    return _unpack(reduced, [(n, s) for n, s, _ in pieces])


def _relu2(t):
    r = jnp.maximum(t, 0.0)
    return r * r


def _silu(t):
    return t * _sigmoid(t)


def _pack_even(w):
    return jnp.concatenate([w[:, :2048], w[:, 2064:3600], w[:, 2048:2064], w[:, 3600:3608],
                            jnp.zeros((w.shape[0], EVEN_PAD - EVEN_COLS), w.dtype)], axis=1)


def _unpack_even(wp):
    return jnp.concatenate([wp[:, :2048], wp[:, 3584:3600], wp[:, 2048:3584], wp[:, 3600:3608]], axis=1)


def _mlp_fwd(h, w1, w2, pfx):
    pre = _mm(h, w1, "nn", pfx + "_up", out_dtype=BF16)
    return pre, _mm(pre, w2, "nn", pfx + "_down", a_pro=_relu2, out_dtype=BF16)


def _mlp_bwd(dm, h, pre, w1, w2, pfx):
    dpre = _mm(dm, w2, "nt", pfx + "_dpre", epi=lambda acc, p: acc * (2.0 * jnp.maximum(p, 0.0)), extras=[(pre, "mn")], out_dtype=BF16)
    dw2 = _mm(pre, dm, "tn", pfx + "_dw2", a_pro=_relu2)
    dw1 = _mm(h, dpre, "tn", pfx + "_dw1")
    dh = _mm(dpre, w1, "nt", pfx + "_dh", out_dtype=BF16)
    return dh, dw1, dw2


def _step(args):
    a = dict(zip(ARGS, args, strict=True))
    x0 = a["x"][0]
    target = a["loss_target"][0]
    mx, my, mc = lax.axis_index("x"), lax.axis_index("y"), lax.axis_index("c")
    chip = 2 * mx + my
    dev = 2 * chip + mc
    bavg = _head_avg_matrix(MIX_HALF)

    c_all = _gather8(jnp.pad(a["c"], ((0, 7), (0, 0))), "c_gather")[:, :, 0, :].reshape(2 * N_CHIPS, D_MODEL)
    ada_b_shard = lax.dynamic_slice_in_dim(a["ada_b"], chip * MOD_SHARD, MOD_SHARD, axis=1)
    mod_sh = [_mm(c_all, a["ada_w"][l], "nn", f"mod{l}", a_pro=_silu, epi=lambda acc, b: acc + b, extras=[(ada_b_shard[l:l + 1], "n")])
              for l in range(2)]
    small3 = jnp.zeros((8, MOD_SHARD), F32)
    for r, n in enumerate(("s5_b_glu", "sgu_ln_gain", "sgu_ln_bias")):
        small3 = small3.at[r, :LANES].set(a[n][0])
    mod_all = _chip_exchange(jnp.concatenate(mod_sh + [small3]), "mod_gather", True)
    mods = []
    for l in range(2):
        full = mod_all[:, 8 * l:8 * l + 8].transpose(1, 0, 2).reshape(8, MOD)
        mods.append(jnp.split(lax.dynamic_slice_in_dim(full, dev, 1, axis=0), 6, axis=1))
    b_glu, ln_gain, ln_bias = (mod_all[:, 16 + r, :LANES].reshape(1, MIX_HALF) for r in range(3))

    local = dict(a, even_w_in=jnp.pad(a["even_w_in"], ((0, 0), (0, 0), (0, PACK_COLS - EVEN_SHARD))),
                 mlp_w1_0=a["mlp_w1"][0:1], mlp_w1_1=a["mlp_w1"][1:2], mlp_w2_0=a["mlp_w2"][0:1], mlp_w2_1=a["mlp_w2"][1:2])
    (pieces0, rows0), (pieces1, rows1) = SHARDED
    w = _gather_finish(_chip_exchange(_gather_prep(local, pieces0, rows0), "w0_chips", True), pieces0, rows0, "w0")
    w_even = _pack_even(w["even_w_in"][0].reshape(D_MODEL, N_CHIPS, PACK_COLS)[:, :, :EVEN_SHARD].reshape(D_MODEL, EVEN_COLS))
    w_lr_pad = jnp.zeros((MIX_HALF, MIX_HALF), BF16).at[:GLA_RANK].set(w["gla_w_lr"][0])
    gla_b_lr = a["gla_b_lr"]
    gla_gain, q_gain, k_gain = (a[n].reshape(1, MIX_HALF) for n in ("gla_gain", "fox_q_gain", "fox_k_gain"))
    sgu_wm, sgu_bt = _sgu_tables(a["sgu_w_s"][0], a["sgu_b_s"][0])

    sh1, sc1, g1, sh2, sc2, g2 = mods[0]
    _, h1_0 = _res_rms(x0, sc1, sh1, "l0_norm1")
    proj0 = _mm(h1_0, w_even, "nn", "l0_proj")
    og, gla_res = _gla_block_fwd(proj0, w_lr_pad, gla_b_lr, gla_gain, bavg, "gla")
    of, fox_res, (collected1,) = _fox_block_fwd(proj0, a["fox_b_f"][0], q_gain, k_gain, bavg, "fox",
                                                carry=(_gather_prep(local, pieces1, rows1), True))
    w.update(_gather_finish(collected1, pieces1, rows1, "w1"))
    s5w = dict(lam_re=a["s5_lam_re"][0], lam_im=a["s5_lam_im"][0], log_dt=a["s5_log_dt"][0], b_re=a["s5_b_re"][0], b_im=a["s5_b_im"][0],
               c_re=a["s5_c_re"][0], c_im=a["s5_c_im"][0], d=a["s5_d"][0], w_glu=w["s5_w_glu"][0], b_glu=b_glu)
    mixed0 = jnp.concatenate([og, of], axis=1).astype(BF16)
    y0 = _mm(mixed0, w["even_w_out"][0], "nn", "l0_out", out_dtype=BF16)
    x1, h2_0 = _res_rms(x0, sc2, sh2, "l0_norm2", y=y0, g=g1)
    pre0, m0 = _mlp_fwd(h2_0, w["mlp_w1_0"][0], w["mlp_w2_0"][0], "l0_mlp")
    sh1b, sc1b, g1b, sh2b, sc2b, g2b = mods[1]
    x2, h1_1 = _res_rms(x1, sc1b, sh1b, "l1_norm1", y=m0, g=g2)
    proj1 = _mm(h1_1, w["odd_w_in"][0], "nn", "l1_proj")
    ys5, s5_res = _s5_block_fwd(proj1[:, :MIX_HALF], s5w, "s5")
    ysgu = _sgu_fwd(proj1, ln_gain, ln_bias, sgu_wm, sgu_bt, "sgu")
    mixed1 = jnp.concatenate([ys5, ysgu], axis=1).astype(BF16)
    y1 = _mm(mixed1, w["odd_w_out"][0], "nn", "l1_out", out_dtype=BF16)
    x3, h2_1 = _res_rms(x2, sc2b, sh2b, "l1_norm2", y=y1, g=g1b)
    pre1, m1 = _mlp_fwd(h2_1, w["mlp_w1_1"][0], w["mlp_w2_1"][0], "l1_mlp")
    loss_b, dx4, dm1, dg2b = _res_loss(x3, m1, g2b, target, "loss")
    loss = lax.psum(loss_b[0, 0], ("x", "y", "c"))

    full = {}
    dh2_1, dw1_1, dw2_1 = _mlp_bwd(dm1, h2_1, pre1, w["mlp_w1_1"][0], w["mlp_w2_1"][0], "l1_mlp")
    dx3, dy1, dg1b, dsc2b, dsh2b = _res_rms_bwd(x3, dh2_1, sc2b, dx4, "l1_norm2_bwd", y=y1, g=g1b)
    dmixed1 = _mm(dy1, w["odd_w_out"][0], "nt", "l1_out_dx")
    full["odd_w_out"] = _mm(mixed1, dy1, "tn", "l1_out_dw")[None]
    du, s5g = _s5_block_bwd(dmixed1[:, :MIX_HALF], s5w, s5_res, "s5")
    dzu, dzv, dws, dbt, dlg, dlb = _sgu_bwd(dmixed1[:, MIX_HALF:], proj1, ln_gain, ln_bias, sgu_wm, sgu_bt, "sgu_bwd")
    g_ws, g_bs = _sgu_grads(dws, dbt)
    dproj1 = jnp.concatenate([du, dzu, dzv], axis=1).astype(BF16)
    full["odd_w_in"] = _mm(h1_1, dproj1, "tn", "l1_proj_dw")[None]
    dh1_1 = _mm(dproj1, w["odd_w_in"][0], "nt", "l1_proj_dx", out_dtype=BF16)
    dx2, dm0, dg2, dsc1b, dsh1b = _res_rms_bwd(x2, dh1_1, sc1b, dx3, "l1_norm1_bwd", y=m0, g=g2)
    dh2_0, dw1_0, dw2_0 = _mlp_bwd(dm0, h2_0, pre0, w["mlp_w1_0"][0], w["mlp_w2_0"][0], "l0_mlp")
    full.update(mlp_w1_0=dw1_0[None], mlp_w2_0=dw2_0[None], mlp_w1_1=dw1_1[None], mlp_w2_1=dw2_1[None], s5_w_glu=s5g["w_glu"][None],
                s5_b_glu=s5g["b_glu"][None], sgu_ln_gain=dlg, sgu_ln_bias=dlb)
    pair_sums1 = _reduce_prep(full, pieces1, rows1, "g1")
    dx1, dy0, dg1, dsc2, dsh2 = _res_rms_bwd(x1, dh2_0, sc2, dx2, "l0_norm2_bwd", y=y0, g=g1)
    dmixed0 = _mm(dy0, w["even_w_out"][0], "nt", "l0_out_dx")
    full["even_w_out"] = _mm(mixed0, dy0, "tn", "l0_out_dw")[None]
    (dgq, dgk, dgv, dgg, dsmall), glag = _gla_block_bwd((dmixed0, MIX_HALF, 0), proj0, w_lr_pad, gla_gain, bavg, gla_res, "gla")
    (dfq, dfk, dfv, dff), foxg, (arrived1,) = _fox_block_bwd((dmixed0, MIX_HALF, 1), proj0, q_gain, k_gain, bavg, fox_res, "fox",
                                                           carry=(pair_sums1, False))
    dsmall = lax.dynamic_update_slice(dsmall, dff, (0, GLA_RANK))
    dproj0 = jnp.concatenate([dgq, dgk, dgv, dgg, dfq, dfk, dfv, dsmall], axis=1).astype(BF16)
    d_even = _unpack_even(_mm(h1_0, dproj0, "tn", "l0_proj_dw")).reshape(D_MODEL, N_CHIPS, EVEN_SHARD)
    full["even_w_in"] = jnp.pad(d_even, ((0, 0), (0, 0), (0, PACK_COLS - EVEN_SHARD))).reshape(1, D_MODEL, N_CHIPS * PACK_COLS)
    dh1_0 = _mm(dproj0, w_even, "nt", "l0_proj_dx", out_dtype=BF16)
    grad_x, dsc1, dsh1 = _res_rms_bwd(x0, dh1_0, sc1, dx1, "l0_norm1_bwd")
    full["gla_w_lr"] = glag["w_lr"][None]

    dmod = jnp.concatenate([dsh1, dsc1, dg1, dsh2, dsc2, dg2, dsh1b, dsc1b, dg1b, dsh2b, dsc2b, dg2b], axis=1)
    dmod_all = _gather8(jnp.pad(dmod, ((0, 7), (0, 0))), "dmod_gather")[:, :, 0, :].reshape(2 * N_CHIPS, 2, MOD)
    grads = {}
    grads["ada_w"] = jnp.stack([
        _mm(c_all, lax.dynamic_slice_in_dim(dmod_all[:, l], chip * MOD_SHARD, MOD_SHARD, axis=1), "tn", f"ada_dw{l}", a_pro=_silu)
        for l in range(2)])
    grads["ada_b"] = _sum_slots(dmod_all.reshape(2 * N_CHIPS, 2 * MOD // MIX_HALF, MIX_HALF), "ada_db").reshape(2, MOD)

    grads.update(_reduce_finish(arrived1, pieces1, rows1, "g1"))
    grads.update(_reduce_finish(_chip_exchange(_reduce_prep(full, pieces0, rows0, "g0"), "g0_chips", False), pieces0, rows0, "g0"))
    grads["even_w_in"] = grads["even_w_in"][:, :, :EVEN_SHARD]
    grads["mlp_w1"] = jnp.concatenate([grads.pop("mlp_w1_0"), grads.pop("mlp_w1_1")])
    grads["mlp_w2"] = jnp.concatenate([grads.pop("mlp_w2_0"), grads.pop("mlp_w2_1")])

    part = dict(gla_b_lr=glag["b_lr"], gla_gain=glag["gain"], fox_b_f=foxg["b_f"], fox_q_gain=foxg["q_gain"], fox_k_gain=foxg["k_gain"],
                s5_lam_re=s5g["lam_re"], s5_lam_im=s5g["lam_im"], s5_log_dt=s5g["log_dt"], s5_b_re=s5g["b_re"], s5_b_im=s5g["b_im"],
                s5_c_re=s5g["c_re"], s5_c_im=s5g["c_im"], s5_d=s5g["d"], sgu_w_s=g_ws, sgu_b_s=g_bs)
    parts_all = _gather8(_pack_rows([part[n] for n, _ in REPLICATED], SMALL_ROWS).astype(BF16), "rep_gather")
    rep = _sum_slots(parts_all.reshape(2 * N_CHIPS, SMALL_ROWS, PACK_COLS), "rep_sum")
    grads.update(_unpack(rep, REPLICATED))

    delta, new_m, new_v = {}, {}, {}
    for n, shape2 in BIG_ADAM.items():
        d, nm, nv = _adamw(a[n].reshape(shape2), grads[n].reshape(shape2), a["m_" + n].reshape(shape2), a["v_" + n].reshape(shape2), "adamw_" + n)
        delta[n], new_m[n], new_v[n] = (t.reshape(a[n].shape) for t in (d, nm, nv))
    small = [n for n in WEIGHTS if n not in BIG_ADAM]
    spec = [(n, a[n].shape) for n in small]
    packs = [_pack_rows([src[n] for n in small], SMALL_ROWS) for src in
             (a, grads, {n: a["m_" + n] for n in small}, {n: a["v_" + n] for n in small})]
    for tgt, res in zip((delta, new_m, new_v), _adamw(*packs, "adamw_small")):
        tgt.update(_unpack(res, spec))
    outs = [loss, grad_x[None]]
    for group in (grads, delta, new_m, new_v):
        outs += [group[n].reshape(a[n].shape) for n in WEIGHTS]
    return tuple(outs)


def kernel(x, c, ada_w, ada_b, even_w_in, even_w_out, gla_w_lr, gla_b_lr, gla_gain, fox_b_f, fox_q_gain, fox_k_gain, odd_w_in,
           odd_w_out, s5_lam_re, s5_lam_im, s5_log_dt, s5_b_re, s5_b_im, s5_c_re, s5_c_im, s5_d, s5_w_glu, s5_b_glu, sgu_ln_gain,
           sgu_ln_bias, sgu_w_s, sgu_b_s, mlp_w1, mlp_w2, loss_target, m_ada_w, m_ada_b, m_even_w_in, m_even_w_out, m_gla_w_lr,
           m_gla_b_lr, m_gla_gain, m_fox_b_f, m_fox_q_gain, m_fox_k_gain, m_odd_w_in, m_odd_w_out, m_s5_lam_re, m_s5_lam_im,
           m_s5_log_dt, m_s5_b_re, m_s5_b_im, m_s5_c_re, m_s5_c_im, m_s5_d, m_s5_w_glu, m_s5_b_glu, m_sgu_ln_gain, m_sgu_ln_bias,
           m_sgu_w_s, m_sgu_b_s, m_mlp_w1, m_mlp_w2, v_ada_w, v_ada_b, v_even_w_in, v_even_w_out, v_gla_w_lr, v_gla_b_lr,
           v_gla_gain, v_fox_b_f, v_fox_q_gain, v_fox_k_gain, v_odd_w_in, v_odd_w_out, v_s5_lam_re, v_s5_lam_im, v_s5_log_dt,
           v_s5_b_re, v_s5_b_im, v_s5_c_re, v_s5_c_im, v_s5_d, v_s5_w_glu, v_s5_b_glu, v_sgu_ln_gain, v_sgu_ln_bias, v_sgu_w_s,
           v_sgu_b_s, v_mlp_w1, v_mlp_w2):
    return _step((x, c, ada_w, ada_b, even_w_in, even_w_out, gla_w_lr, gla_b_lr, gla_gain, fox_b_f, fox_q_gain, fox_k_gain,
                  odd_w_in, odd_w_out, s5_lam_re, s5_lam_im, s5_log_dt, s5_b_re, s5_b_im, s5_c_re, s5_c_im, s5_d, s5_w_glu,
                  s5_b_glu, sgu_ln_gain, sgu_ln_bias, sgu_w_s, sgu_b_s, mlp_w1, mlp_w2, loss_target, m_ada_w, m_ada_b,
                  m_even_w_in, m_even_w_out, m_gla_w_lr, m_gla_b_lr, m_gla_gain, m_fox_b_f, m_fox_q_gain, m_fox_k_gain,
                  m_odd_w_in, m_odd_w_out, m_s5_lam_re, m_s5_lam_im, m_s5_log_dt, m_s5_b_re, m_s5_b_im, m_s5_c_re, m_s5_c_im,
                  m_s5_d, m_s5_w_glu, m_s5_b_glu, m_sgu_ln_gain, m_sgu_ln_bias, m_sgu_w_s, m_sgu_b_s, m_mlp_w1, m_mlp_w2, v_ada_w,
                  v_ada_b, v_even_w_in, v_even_w_out, v_gla_w_lr, v_gla_b_lr, v_gla_gain, v_fox_b_f, v_fox_q_gain, v_fox_k_gain,
                  v_odd_w_in, v_odd_w_out, v_s5_lam_re, v_s5_lam_im, v_s5_log_dt, v_s5_b_re, v_s5_b_im, v_s5_c_re, v_s5_c_im,
                  v_s5_d, v_s5_w_glu, v_s5_b_glu, v_sgu_ln_gain, v_sgu_ln_bias, v_sgu_w_s, v_sgu_b_s, v_mlp_w1, v_mlp_w2))
```

```python
import functools
import math

import jax
import jax.numpy as jnp
import numpy as np
from jax import lax
from jax.experimental import pallas as pl
from jax.experimental.pallas import tpu as pltpu

F32 = jnp.float32
BF16 = jnp.bfloat16
MESH = pl.DeviceIdType.MESH
ANY = pl.BlockSpec(memory_space=pl.ANY)
DMA_SEM = pltpu.SemaphoreType.DMA

D_MODEL = 1024
HEAD_DIM = 64
MIX_HALF = 512
GLA_RANK = 16
GLA_TAU = 16.0
GLA_CHUNK = 64
S5_GROUPS = 32
S5_GROUP_WIDTH = 16
S5_STATE = 64
S5_N = S5_GROUPS * S5_STATE
SGU_GROUPS = 8
SGU_CHUNK = 128
D_FF = 4096
EPS = 1e-6
N_CHIPS = 4
LANES = 128
VMEM_LIMIT = 48 * 1024 * 1024
PAIR_COPIES = 16

ADAM_LR = 0.001
ADAM_B1 = 0.9
ADAM_B2 = 0.999
ADAM_EPS = 1e-08
ADAM_WD = 0.01
ADAM_STEP = 10


def _cparams(*sem):
    return pltpu.CompilerParams(dimension_semantics=sem, vmem_limit_bytes=VMEM_LIMIT)


def _pair_swap(x, name):
    lead = x.shape[:-2]
    rows = x.shape[-2]
    nsplit = max(1, PAIR_COPIES // max(1, math.prod(lead)))
    while nsplit > 1 and rows % (nsplit * 16):
        nsplit -= 1
    pieces = [idx + (pl.ds(j * (rows // nsplit), rows // nsplit),) for idx in np.ndindex(*lead) for j in range(nsplit)]

    def body(x_ref, o_ref, send_sems, recv_sems):
        mx, my, mc = lax.axis_index("x"), lax.axis_index("y"), lax.axis_index("c")
        copies = [pltpu.make_async_remote_copy(src_ref=x_ref.at[p], dst_ref=o_ref.at[p], send_sem=send_sems.at[j], recv_sem=recv_sems.at[j],
                                               device_id=(mx, my, 1 - mc), device_id_type=MESH) for j, p in enumerate(pieces)]
        for cp in copies:
            cp.start()
        for cp in copies:
            cp.wait_recv()
        for cp in copies:
            cp.wait_send()

    return pl.pallas_call(
        body, name=name, out_shape=jax.ShapeDtypeStruct(x.shape, x.dtype), in_specs=[ANY], out_specs=ANY,
        scratch_shapes=[DMA_SEM((len(pieces),)), DMA_SEM((len(pieces),))])(x)


def _by_core(mine, theirs):
    first = lax.axis_index("c") == 0
    return jnp.stack([jnp.where(first, mine, theirs), jnp.where(first, theirs, mine)])


def _chip_exchange(x, name, bcast):
    blk = x.shape if bcast else x.shape[1:]

    def body(x_ref, o_ref, send_sems, recv_sems, loc_sem):
        start, finish = _exchange_plan(x_ref, o_ref, send_sems, recv_sems, loc_sem, bcast)
        start()
        finish()

    return pl.pallas_call(
        body, name=name, out_shape=jax.ShapeDtypeStruct((N_CHIPS,) + tuple(blk), x.dtype), in_specs=[ANY], out_specs=ANY,
        scratch_shapes=_EXCHANGE_SEMS)(x)


_EXCHANGE_SEMS = [DMA_SEM((3,)), DMA_SEM((3,)), DMA_SEM]


def _exchange_plan(x_ref, o_ref, send_sems, recv_sems, loc_sem, bcast):
    mx, my, mc = lax.axis_index("x"), lax.axis_index("y"), lax.axis_index("c")
    me = 2 * mx + my
    peers = [(1 - mx, my), (mx, 1 - my), (1 - mx, 1 - my)]

    def src(k):
        return x_ref if bcast else x_ref.at[k]

    def remote(j, source, slot):
        px, py = peers[j]
        return pltpu.make_async_remote_copy(src_ref=source, dst_ref=o_ref.at[slot], send_sem=send_sems.at[j], recv_sem=recv_sems.at[j],
                                            device_id=(px, py, mc), device_id_type=MESH)

    loc = pltpu.make_async_copy(src(me), o_ref.at[me], loc_sem)
    sends = [remote(j, src(2 * px + py), me) for j, (px, py) in enumerate(peers)]
    arrivals = [remote(j, src(me), 2 * px + py) for j, (px, py) in enumerate(peers)]

    def start():
        loc.start()
        for cp in sends:
            cp.start()

    def finish():
        for cp in arrivals:
            cp.wait_recv()
        for cp in sends:
            cp.wait_send()
        loc.wait()

    return start, finish


def _gather8(x, name):
    collected = _chip_exchange(x, name + "_chips", True)
    return jnp.swapaxes(_by_core(collected, _pair_swap(collected, name + "_pair")), 0, 1)


def _tile(n, want):
    if n <= want:
        return n
    t = (want // LANES) * LANES
    while t >= LANES:
        if n % t == 0:
            return t
        t -= LANES
    raise ValueError(f"no lane-aligned tile for {n}")


_DIMS = {"nn": (((1,), (0,)), ((), ())), "nt": (((1,), (1,)), ((), ())), "tn": (((0,), (0,)), ((), ()))}


MM_FULL_K = 4096
MM_SLAB_K = 2048
MM_TILES = ((1024, 1024), (512, 1024), (1024, 512), (512, 512), (256, 512), (256, 256))
MM_VMEM_BUDGET = 36 * 1024 * 1024


def _mm(a, b, mode, name, *, a_pro=None, epi=None, extras=(), out_dtype=F32, tm_max=1024, tn_max=1024, tk=None, a_cols=None):
    c0, csize = a_cols if a_cols is not None else (0, a.shape[1])
    if mode == "tn":
        K, M = a.shape[0], csize
    else:
        M, K = a.shape[0], csize
    N = b.shape[0] if mode == "nt" else b.shape[1]
    assert (b.shape[1] if mode == "nt" else b.shape[0]) == K, (a.shape, b.shape, mode)
    if tk is None:
        tk = K if (mode != "tn" and K <= MM_FULL_K) else MM_SLAB_K
    tk = _tile(K, tk)
    nk = K // tk
    n_mn = sum(1 for _, kind in extras if kind == "mn")
    for tm_want, tn_want in MM_TILES:
        tm, tn = _tile(M, min(tm_want, tm_max)), _tile(N, min(tn_want, tn_max))
        need = 2 * (tm * tk * a.dtype.itemsize + tk * tn * b.dtype.itemsize + tm * tn * 4 * (1 + n_mn)) + tm * tn * 4 * (nk > 1)
        if need <= MM_VMEM_BUDGET:
            break
    if mode == "tn":
        assert c0 % tm == 0
        a_spec = pl.BlockSpec((tk, tm), lambda i, j, k: (k, i + c0 // tm))
    else:
        assert c0 % tk == 0
        a_spec = pl.BlockSpec((tm, tk), lambda i, j, k: (i, k + c0 // tk))
    b_spec = pl.BlockSpec((tn, tk), lambda i, j, k: (j, k)) if mode == "nt" else pl.BlockSpec((tk, tn), lambda i, j, k: (k, j))
    ex_specs = []
    for arr, kind in extras:
        if kind == "mn":
            assert arr.shape == (M, N)
            ex_specs.append(pl.BlockSpec((tm, tn), lambda i, j, k: (i, j)))
        else:
            assert arr.shape == (1, N)
            ex_specs.append(pl.BlockSpec((1, tn), lambda i, j, k: (0, j)))
    n_ex = len(extras)

    def body(*refs):
        a_ref, b_ref = refs[:2]
        ex_refs = refs[2:2 + n_ex]
        o_ref = refs[2 + n_ex]
        acc_ref = refs[3 + n_ex] if nk > 1 else None
        k = pl.program_id(2)
        av = a_ref[...]
        if a_pro is not None:
            av = a_pro(av)
        part = lax.dot_general(av.astype(BF16), b_ref[...].astype(BF16), _DIMS[mode], preferred_element_type=F32)
        if nk == 1:
            if epi is not None:
                part = epi(part, *[r[...] for r in ex_refs])
            o_ref[...] = part.astype(o_ref.dtype)
            return

        @pl.when(k == 0)
        def _():
            acc_ref[...] = part

        @pl.when(k > 0)
        def _():
            acc_ref[...] += part

        @pl.when(k == nk - 1)
        def _():
            acc = acc_ref[...]
            if epi is not None:
                acc = epi(acc, *[r[...] for r in ex_refs])
            o_ref[...] = acc.astype(o_ref.dtype)

    return pl.pallas_call(
        body, name=name, grid=(M // tm, N // tn, nk),
        in_specs=[a_spec, b_spec] + ex_specs,
        out_specs=pl.BlockSpec((tm, tn), lambda i, j, k: (i, j)),
        out_shape=jax.ShapeDtypeStruct((M, N), out_dtype),
        scratch_shapes=[pltpu.VMEM((tm, tn), F32)] if nk > 1 else [],
        compiler_params=_cparams("parallel", "parallel", "arbitrary"))(a, b, *[e[0] for e in extras])


ROWS = 512


def _row_spec(w, ts=ROWS):
    return pl.BlockSpec((ts, w), lambda i: (i, 0))


def _vec_spec(w):
    return pl.BlockSpec((1, w), lambda i: (0, 0))


def _res_rms(x, sc, sh, name, y=None, g=None):
    S, D = x.shape
    has_res = y is not None

    def body(*refs):
        if has_res:
            x_ref, y_ref, g_ref, sc_ref, sh_ref, xo_ref, h_ref = refs
            xv = x_ref[...] + g_ref[...] * y_ref[...]
            xo_ref[...] = xv
        else:
            x_ref, sc_ref, sh_ref, h_ref = refs
            xv = x_ref[...]
        r = lax.rsqrt(jnp.mean(xv * xv, axis=-1, keepdims=True) + EPS)
        h_ref[...] = (xv * r * (1.0 + sc_ref[...]) + sh_ref[...]).astype(BF16)

    row, vec = _row_spec(D), _vec_spec(D)
    if has_res:
        return pl.pallas_call(body, name=name, grid=(S // ROWS,), in_specs=[row, row, vec, vec, vec], out_specs=[row, row],
                              out_shape=[jax.ShapeDtypeStruct((S, D), F32), jax.ShapeDtypeStruct((S, D), BF16)],
                              compiler_params=_cparams("parallel"))(x, y, g, sc, sh)
    h = pl.pallas_call(body, name=name, grid=(S // ROWS,), in_specs=[row, vec, vec], out_specs=row,
                       out_shape=jax.ShapeDtypeStruct((S, D), BF16), compiler_params=_cparams("parallel"))(x, sc, sh)
    return x, h


def _res_rms_bwd(x, dh, sc, dres, name, y=None, g=None):
    S, D = x.shape
    has_res = y is not None

    def body(*refs):
        if has_res:
            x_ref, dh_ref, sc_ref, dres_ref, y_ref, g_ref, dx_ref, dy_ref, dg_ref, dsc_ref, dsh_ref = refs
        else:
            x_ref, dh_ref, sc_ref, dres_ref, dx_ref, dsc_ref, dsh_ref = refs
        first = pl.program_id(0) == 0
        xv = x_ref[...]
        dh = dh_ref[...].astype(F32)
        r = lax.rsqrt(jnp.mean(xv * xv, axis=-1, keepdims=True) + EPS)
        xn = xv * r
        dxn = dh * (1.0 + sc_ref[...])
        dx = dres_ref[...] + r * (dxn - xn * jnp.mean(dxn * xn, axis=-1, keepdims=True))
        dx_ref[...] = dx
        parts = [(dsc_ref, jnp.sum(dh * xn, axis=0, keepdims=True)), (dsh_ref, jnp.sum(dh, axis=0, keepdims=True))]
        if has_res:
            dy_ref[...] = (dx * g_ref[...]).astype(BF16)
            parts.append((dg_ref, jnp.sum(dx * y_ref[...], axis=0, keepdims=True)))
        for ref, val in parts:
            @pl.when(first)
            def _(ref=ref, val=val):
                ref[...] = val

            @pl.when(jnp.logical_not(first))
            def _(ref=ref, val=val):
                ref[...] += val

    row, vec = _row_spec(D), _vec_spec(D)
    full = jax.ShapeDtypeStruct((S, D), F32)
    v = jax.ShapeDtypeStruct((1, D), F32)
    if has_res:
        return pl.pallas_call(body, name=name, grid=(S // ROWS,), in_specs=[row, row, vec, row, row, vec],
                              out_specs=[row, row, vec, vec, vec], out_shape=[full, jax.ShapeDtypeStruct((S, D), BF16), v, v, v],
                              compiler_params=_cparams("arbitrary"))(x, dh, sc, dres, y, g)
    return pl.pallas_call(body, name=name, grid=(S // ROWS,), in_specs=[row, row, vec, row],
                          out_specs=[row, vec, vec], out_shape=[full, v, v],
                          compiler_params=_cparams("arbitrary"))(x, dh, sc, dres)


def _res_loss(x, m, g, target, name):
    S, D = x.shape

    def body(x_ref, m_ref, g_ref, t_ref, loss_ref, dx_ref, dm_ref, dg_ref):
        first = pl.program_id(0) == 0
        mv = m_ref[...].astype(F32)
        err = x_ref[...] + g_ref[...] * mv - t_ref[...]
        dx = err * (1.0 / D)
        dx_ref[...] = dx
        dm_ref[...] = (dx * g_ref[...]).astype(BF16)
        part = 0.5 * jnp.sum(jnp.mean(err * err, axis=-1, keepdims=True), axis=0, keepdims=True)
        dg = jnp.sum(dx * mv, axis=0, keepdims=True)

        @pl.when(first)
        def _():
            loss_ref[...] = jnp.broadcast_to(part, loss_ref.shape)
            dg_ref[...] = dg

        @pl.when(jnp.logical_not(first))
        def _():
            loss_ref[...] += jnp.broadcast_to(part, loss_ref.shape)
            dg_ref[...] += dg

    row, vec = _row_spec(D), _vec_spec(D)
    full = jax.ShapeDtypeStruct((S, D), F32)
    return pl.pallas_call(body, name=name, grid=(S // ROWS,), in_specs=[row, row, vec, row],
                          out_specs=[pl.BlockSpec((8, LANES), lambda i: (0, 0)), row, row, vec],
                          out_shape=[jax.ShapeDtypeStruct((8, LANES), F32), full, jax.ShapeDtypeStruct((S, D), BF16), jax.ShapeDtypeStruct((1, D), F32)],
                          compiler_params=_cparams("arbitrary"))(x, m, g, target)


def _adamw(w, g, m, v, name):
    R, C = w.shape
    tr = R if R <= 256 else 256
    assert R % tr == 0

    def body(w_ref, g_ref, m_ref, v_ref, d_ref, nm_ref, nv_ref):
        gv = g_ref[...]
        nm = ADAM_B1 * m_ref[...] + (1.0 - ADAM_B1) * gv
        nv = ADAM_B2 * v_ref[...] + (1.0 - ADAM_B2) * jnp.square(gv)
        m_hat = nm / (1.0 - ADAM_B1 ** ADAM_STEP)
        v_hat = nv / (1.0 - ADAM_B2 ** ADAM_STEP)
        d_ref[...] = -ADAM_LR * (m_hat / (jnp.sqrt(v_hat) + ADAM_EPS) + ADAM_WD * w_ref[...])
        nm_ref[...] = nm
        nv_ref[...] = nv

    spec = pl.BlockSpec((tr, C), lambda i: (i, 0))
    out = jax.ShapeDtypeStruct((R, C), F32)
    return pl.pallas_call(body, name=name, grid=(R // tr,), in_specs=[spec] * 4, out_specs=[spec] * 3,
                          out_shape=[out, out, out], compiler_params=_cparams("parallel"))(w, g, m, v)


def _sum_slots(x, name):
    n, R, C = x.shape
    tr = R if R <= 256 else 256
    assert R % tr == 0

    def body(x_ref, o_ref):
        acc = x_ref[0].astype(F32)
        for j in range(1, n):
            acc = acc + x_ref[j].astype(F32)
        o_ref[...] = acc

    return pl.pallas_call(body, name=name, grid=(R // tr,), in_specs=[pl.BlockSpec((n, tr, C), lambda i: (0, i, 0))],
                          out_specs=pl.BlockSpec((tr, C), lambda i: (i, 0)), out_shape=jax.ShapeDtypeStruct((R, C), F32),
                          compiler_params=_cparams("parallel"))(x)


EW_ROWS = 1024


def _ew(fn, name, tiled, consts=(), outs=(), sums=(), ts=EW_ROWS):
    tiled = [t if isinstance(t, tuple) else (t, t.shape[1], 0) for t in tiled]
    S = tiled[0][0].shape[0]
    ts = min(ts, S)
    assert S % ts == 0, (name, S, ts)
    n_t, n_c, n_o, n_s = len(tiled), len(consts), len(outs), len(sums)

    def body(*refs):
        ins = [r[...] for r in refs[:n_t + n_c]]
        res = fn(*ins)
        res = res if isinstance(res, (tuple, list)) else (res,)
        assert len(res) == n_o + n_s
        o_refs = refs[n_t + n_c:]
        for r, val in zip(o_refs[:n_o], res[:n_o]):
            r[...] = val.astype(r.dtype)
        first = pl.program_id(0) == 0
        for r, val in zip(o_refs[n_o:], res[n_o:]):
            @pl.when(first)
            def _(r=r, val=val):
                r[...] = val

            @pl.when(jnp.logical_not(first))
            def _(r=r, val=val):
                r[...] += val

    in_specs = [pl.BlockSpec((ts, w), lambda i, cb=cb: (i, cb)) for _, w, cb in tiled]
    in_specs += [pl.BlockSpec(c.shape, lambda i, nd=c.ndim: (0,) * nd) for c in consts]
    out_specs = [_row_spec(w, ts) for w, _ in outs] + [_vec_spec(w) for w in sums]
    out_shape = [jax.ShapeDtypeStruct((S, w), dt) for w, dt in outs] + [jax.ShapeDtypeStruct((1, w), F32) for w in sums]
    res = pl.pallas_call(body, name=name, grid=(S // ts,), in_specs=in_specs, out_specs=out_specs, out_shape=out_shape,
                         compiler_params=_cparams("arbitrary" if sums else "parallel"))(*[t[0] for t in tiled], *consts)
    return res


_GELU_C = math.sqrt(2.0 / math.pi)


def _gelu(x):
    return 0.5 * x * (1.0 + jnp.tanh(_GELU_C * (x + 0.044715 * x * x * x)))


def _dgelu(x):
    t = jnp.tanh(_GELU_C * (x + 0.044715 * x * x * x))
    return 0.5 * (1.0 + t) + 0.5 * x * (1.0 - t * t) * _GELU_C * (1.0 + 3.0 * 0.044715 * x * x)


def _sigmoid(x):
    return 1.0 / (1.0 + jnp.exp(-x))


def _log_sigmoid(x):
    return jnp.minimum(x, 0.0) - jnp.log(1.0 + jnp.exp(-jnp.abs(x)))


SCAN_T = 128
SCAN_TB = 512


def _cmul(ar, ai, br, bi):
    return ar * br - ai * bi, ar * bi + ai * br


def _s5_discretise(lam_re, lam_im, log_dt, b_re, b_im):
    dt = jnp.exp(log_dt)[:, None]
    mag = jnp.exp(lam_re * dt)
    ang = lam_im * dt
    abar_re = mag * jnp.cos(ang)
    abar_im = mag * jnp.sin(ang)
    den = lam_re * lam_re + lam_im * lam_im
    coef_re = ((abar_re - 1.0) * lam_re + abar_im * lam_im) / den
    coef_im = (abar_im * lam_re - (abar_re - 1.0) * lam_im) / den
    bbar_re = coef_re[..., None] * b_re - coef_im[..., None] * b_im
    bbar_im = coef_re[..., None] * b_im + coef_im[..., None] * b_re
    return abar_re, abar_im, bbar_re, bbar_im


def _planes(re, im):
    lead = re.shape[:-1]
    return jnp.stack([re.reshape(lead + (-1, LANES)), im.reshape(lead + (-1, LANES))], axis=-2).reshape(lead + (-1,))


def _unplanes(x):
    lead = x.shape[:-1]
    x4 = x.reshape(lead + (-1, 2, LANES))
    return x4[..., 0, :].reshape(lead + (-1,)), x4[..., 1, :].reshape(lead + (-1,))


def _s5_scan_tables(a_re, a_im, reverse):
    pr, pi = [a_re], [a_im]
    for _ in range(7):
        r, i = _cmul(pr[-1], pi[-1], pr[-1], pi[-1])
        pr.append(r)
        pi.append(i)
    apow = _planes(jnp.stack(pr), jnp.stack(pi))
    n = np.arange(1, SCAN_T + 1)
    if reverse:
        n = n[::-1]
    tr = jnp.ones((SCAN_T, a_re.shape[0]), F32)
    ti = jnp.zeros((SCAN_T, a_re.shape[0]), F32)
    for k in range(8):
        bit = jnp.asarray(((n >> k) & 1).astype(np.float32))[:, None]
        mr = bit * pr[k][None, :] + (1.0 - bit)
        mi = bit * pi[k][None, :]
        tr, ti = _cmul(tr, ti, mr, mi)
    return apow, _planes(tr, ti)


def _s5_scan(bu, apow, ptab, name, reverse, x_fwd=None):
    S, N2 = bu.shape
    T, W = SCAN_T, 2 * LANES
    tb = min(SCAN_TB, S)
    nt, nsub = S // tb, tb // T
    order = list(range(nsub - 1, -1, -1) if reverse else range(nsub))
    with_da = x_fwd is not None

    def tblk(t):
        return (nt - 1 - t) if reverse else t

    def shifted(v, k, rowi):
        s = 1 << k
        if reverse:
            return jnp.where(rowi < T - s, pltpu.roll(v, T - s, 0), 0.0)
        return jnp.where(rowi >= s, pltpu.roll(v, s, 0), 0.0)

    def body(*refs):
        if with_da:
            bu_ref, ap_ref, pt_ref, xf_ref, xp_ref, x_ref, da_ref, carry_ref = refs
        else:
            bu_ref, ap_ref, pt_ref, x_ref, carry_ref = refs
        t = pl.program_id(1)

        @pl.when(t == 0)
        def _():
            carry_ref[...] = jnp.zeros_like(carry_ref)
            if with_da:
                da_ref[...] = jnp.zeros_like(da_ref)

        rowi = lax.broadcasted_iota(jnp.int32, (T, LANES), 0)
        pr, pi = pt_ref[:, :LANES], pt_ref[:, LANES:]
        cr, ci = carry_ref[0:1, :LANES], carry_ref[0:1, LANES:]
        for sb in order:
            rows = pl.ds(sb * T, T)
            xr, xi = bu_ref[rows, :LANES], bu_ref[rows, LANES:]
            for k in range(7):
                ar, ai = ap_ref[k:k + 1, :LANES], ap_ref[k:k + 1, LANES:]
                s = 1 << k
                if s < 8:
                    rr, ri = shifted(xr, k, rowi), shifted(xi, k, rowi)
                    xr, xi = xr + ar * rr - ai * ri, xi + ar * ri + ai * rr
                elif reverse:
                    nr, ni = xr[s:], xi[s:]
                    xr = jnp.concatenate([xr[:T - s] + ar * nr - ai * ni, xr[T - s:]], axis=0)
                    xi = jnp.concatenate([xi[:T - s] + ar * ni + ai * nr, xi[T - s:]], axis=0)
                else:
                    nr, ni = xr[:T - s], xi[:T - s]
                    xr = jnp.concatenate([xr[:s], xr[s:] + ar * nr - ai * ni], axis=0)
                    xi = jnp.concatenate([xi[:s], xi[s:] + ar * ni + ai * nr], axis=0)
            xr, xi = xr + pr * cr - pi * ci, xi + pr * ci + pi * cr
            x_ref[rows, :LANES] = xr
            x_ref[rows, LANES:] = xi
            edge = pl.ds(sb * T + (0 if reverse else T - 1), 1)
            cr, ci = x_ref[edge, :LANES], x_ref[edge, LANES:]
            if with_da:
                if sb > 0:
                    before = pl.ds(sb * T - 1, 1)
                    b_r, b_i = xf_ref[before, :LANES], xf_ref[before, LANES:]
                else:
                    keep = (tblk(t) > 0).astype(F32)
                    b_r, b_i = xp_ref[7:8, :LANES] * keep, xp_ref[7:8, LANES:] * keep
                fr, fi = xf_ref[rows, :LANES], xf_ref[rows, LANES:]
                qr = jnp.where(rowi >= 1, pltpu.roll(fr, 1, 0), b_r)
                qi = jnp.where(rowi >= 1, pltpu.roll(fi, 1, 0), b_i)
                gr, gi = xr * qr + xi * qi, xi * qr - xr * qi
                sr, si = gr[0:8], gi[0:8]
                for j in range(1, T // 8):
                    sr, si = sr + gr[8 * j:8 * j + 8], si + gi[8 * j:8 * j + 8]
                da_ref[:, :LANES] += sr
                da_ref[:, LANES:] += si
        carry_ref[0:1, :LANES] = cr
        carry_ref[0:1, LANES:] = ci

    blk = pl.BlockSpec((tb, W), lambda j, t: (tblk(t), j))
    in_specs = [blk, pl.BlockSpec((8, W), lambda j, t: (0, j)), pl.BlockSpec((T, W), lambda j, t: (0, j))]
    out_specs, out_shape = [blk], [jax.ShapeDtypeStruct((S, N2), F32)]
    args = [bu, apow, ptab]
    if with_da:
        in_specs += [blk, pl.BlockSpec((8, W), lambda j, t: (jnp.maximum(tblk(t) * (tb // 8) - 1, 0), j))]
        out_specs.append(pl.BlockSpec((8, W), lambda j, t: (0, j)))
        out_shape.append(jax.ShapeDtypeStruct((8, N2), F32))
        args += [x_fwd, x_fwd]
    res = pl.pallas_call(body, name=name, grid=(N2 // W, nt), in_specs=in_specs, out_specs=out_specs, out_shape=out_shape,
                         scratch_shapes=[pltpu.VMEM((8, W), F32)], compiler_params=_cparams("parallel", "arbitrary"))(*args)
    return res if with_da else res[0]


S5_BAND = 4


def _mm_band(a, b, name, *, b_t=False, outer=False, epi=None, extras=(), tm=512, tk=2048):
    S = a.shape[0]
    wa = a.shape[1] // S5_BAND
    if outer:
        wb = b.shape[1] // S5_BAND
        tk = _tile(S, tk)
        nk = S // tk

        def obody(a_ref, b_ref, o_ref, acc_ref):
            k = pl.program_id(1)
            part = lax.dot_general(a_ref[...].astype(BF16), b_ref[...].astype(BF16), TN_DIMS, preferred_element_type=F32)

            @pl.when(k == 0)
            def _():
                acc_ref[...] = part

            @pl.when(k > 0)
            def _():
                acc_ref[...] += part

            @pl.when(k == nk - 1)
            def _():
                o_ref[...] = acc_ref[...]

        return pl.pallas_call(
            obody, name=name, grid=(S5_BAND, nk),
            in_specs=[pl.BlockSpec((tk, wa), lambda c, k: (k, c)), pl.BlockSpec((tk, wb), lambda c, k: (k, c))],
            out_specs=pl.BlockSpec((wa, wb), lambda c, k: (c, 0)), out_shape=jax.ShapeDtypeStruct((a.shape[1], wb), F32),
            scratch_shapes=[pltpu.VMEM((wa, wb), F32)], compiler_params=_cparams("parallel", "arbitrary"))(a, b)

    wo = (b.shape[0] if b_t else b.shape[1]) // S5_BAND
    tm = _tile(S, tm)
    ex_specs = [pl.BlockSpec((tm, wo), lambda i, c: (i, c)) if kind == "mn" else pl.BlockSpec((1, wo), lambda i, c: (0, c))
                for _, kind in extras]

    def body(a_ref, b_ref, *refs):
        part = lax.dot_general(a_ref[...].astype(BF16), b_ref[...].astype(BF16), NT_DIMS if b_t else _DIMS["nn"], preferred_element_type=F32)
        if epi is not None:
            part = epi(part, *[r[...] for r in refs[:-1]])
        refs[-1][...] = part

    b_spec = pl.BlockSpec((wo, wa) if b_t else (wa, wo), lambda i, c: (c, c))
    return pl.pallas_call(
        body, name=name, grid=(S // tm, S5_BAND), in_specs=[pl.BlockSpec((tm, wa), lambda i, c: (i, c)), b_spec] + ex_specs,
        out_specs=pl.BlockSpec((tm, wo), lambda i, c: (i, c)), out_shape=jax.ShapeDtypeStruct((S, S5_BAND * wo), F32),
        compiler_params=_cparams("parallel", "parallel"))(a, b, *[e[0] for e in extras])


def _band_to_full(blocks, cols):
    wa, wb = blocks.shape[0] // S5_BAND, blocks.shape[1]
    return jnp.concatenate([jnp.pad(blocks[k * wa:(k + 1) * wa], ((0, 0), (k * wb, cols - (k + 1) * wb))) for k in range(S5_BAND)], axis=0)


def _block_diag(t):
    G, a, b = t.shape
    return (t[:, :, None, :] * jnp.eye(G, dtype=t.dtype)[:, None, :, None]).reshape(G * a, G * b)


def _block_diag_take(m, G):
    a, b = m.shape[0] // G, m.shape[1] // G
    m4 = m.reshape(G, a, G, b)
    return jnp.sum(m4 * jnp.eye(G, dtype=m.dtype)[:, None, :, None], axis=2)


def _s5_block_fwd(u, w, pfx):
    a_re, a_im, bb_re, bb_im = _s5_discretise(w["lam_re"], w["lam_im"], w["log_dt"], w["b_re"], w["b_im"])
    bcat = _planes(_block_diag(bb_re).T, _block_diag(bb_im).T).astype(BF16)
    ccat = _planes(_block_diag(jnp.swapaxes(w["c_re"], 1, 2)).T, -_block_diag(jnp.swapaxes(w["c_im"], 1, 2)).T).T.astype(BF16)
    af_re, af_im = a_re.reshape(-1), a_im.reshape(-1)
    apow, ptab = _s5_scan_tables(af_re, af_im, False)
    bu = _mm_band(u, bcat, pfx + "_bu")
    x = _s5_scan(bu, apow, ptab, pfx + "_scan", False)
    d_row = w["d"].reshape(1, MIX_HALF)
    ys = _mm_band(x, ccat, pfx + "_y", epi=lambda acc, ut, dr: acc + dr * ut, extras=[(u, "mn"), (d_row, "n")])
    z = _mm(ys, w["w_glu"], "nn", pfx + "_glu", a_pro=_gelu, epi=lambda acc, b: acc + b, extras=[(w["b_glu"].reshape(1, -1), "n")])
    y2, = _ew(lambda ysv, zv: _gelu(ysv) * _sigmoid(zv), pfx + "_gate", [ys, z], outs=[(MIX_HALF, F32)])
    return y2, dict(u=u, x=x, ys=ys, z=z, bcat=bcat, ccat=ccat, a=(af_re, af_im), d_row=d_row)


def _s5_block_bwd(dy2, w, res, pfx):
    u, x, ys, z, bcat, ccat = res["u"], res["x"], res["ys"], res["z"], res["bcat"], res["ccat"]

    def gate_bwd(dy, ysv, zv):
        sg = _sigmoid(zv)
        dz = dy * _gelu(ysv) * sg * (1.0 - sg)
        return dz, jnp.sum(dz, axis=0, keepdims=True)

    dz, db_glu = _ew(gate_bwd, pfx + "_gate_bwd", [dy2, ys, z], outs=[(MIX_HALF, F32)], sums=[MIX_HALF])
    dw_glu = _mm(ys, dz, "tn", pfx + "_dwglu", a_pro=_gelu)
    dys = _mm(dz, w["w_glu"], "nt", pfx + "_dys", epi=lambda acc, dy, zv, ysv: (acc + dy * _sigmoid(zv)) * _dgelu(ysv),
              extras=[(dy2, "mn"), (z, "mn"), (ys, "mn")])
    dd, = _ew(lambda a, b: jnp.sum(a * b, axis=0, keepdims=True), pfx + "_dd", [dys, u], sums=[MIX_HALF])
    dccat = _band_to_full(_mm_band(x, dys, pfx + "_dc", outer=True), MIX_HALF)
    dx = _mm_band(dys, ccat, pfx + "_dx", b_t=True)
    af_re, af_im = res["a"]
    apow, ptab = _s5_scan_tables(af_re, -af_im, True)
    lam, da8 = _s5_scan(dx, apow, ptab, pfx + "_scan_bwd", True, x_fwd=x)
    dbcat = _band_to_full(_mm_band(u, lam, pfx + "_db", outer=True), 2 * S5_N)
    du = _mm_band(lam, bcat, pfx + "_du", b_t=True, epi=lambda acc, dyv, dr: acc + dyv * dr, extras=[(dys, "mn"), (res["d_row"], "n")])
    G = S5_GROUPS
    d_abar_re, d_abar_im = (t.reshape(G, S5_STATE) for t in _unplanes(jnp.sum(da8, axis=0)))
    d_bb_re, d_bb_im = (_block_diag_take(t.T, G) for t in _unplanes(dbcat))
    _, vjp = jax.vjp(_s5_discretise, w["lam_re"], w["lam_im"], w["log_dt"], w["b_re"], w["b_im"])
    g_lam_re, g_lam_im, g_log_dt, g_b_re, g_b_im = vjp((d_abar_re, d_abar_im, d_bb_re, d_bb_im))
    dc_re, dc_im = _unplanes(dccat.T)
    g_c_re = jnp.swapaxes(_block_diag_take(dc_re.T, G), 1, 2)
    g_c_im = -jnp.swapaxes(_block_diag_take(dc_im.T, G), 1, 2)
    grads = dict(lam_re=g_lam_re, lam_im=g_lam_im, log_dt=g_log_dt, b_re=g_b_re, b_im=g_b_im, c_re=g_c_re, c_im=g_c_im,
                 d=dd.reshape(G, S5_GROUP_WIDTH), w_glu=dw_glu, b_glu=db_glu.reshape(-1))
    return du, grads


SGU_TS = 512
N_PAIRS = MIX_HALF // LANES


def _half_masks(rows):
    lane = lax.broadcasted_iota(jnp.int32, (rows, LANES), 1)
    left = (lane < HEAD_DIM).astype(F32)
    return left, 1.0 - left


def _sgu_norm(zv, gain, bias):
    v = _gelu(zv)
    mu = jnp.mean(v, axis=-1, keepdims=True)
    vc = v - mu
    rstd = lax.rsqrt(jnp.mean(vc * vc, axis=-1, keepdims=True) + EPS)
    vhat = vc * rstd
    return vhat, rstd, vhat * gain + bias


def _sgu_tables(w_s, b_s):
    mask = jnp.tril(jnp.ones((SGU_CHUNK, SGU_CHUNK), dtype=bool))
    wm = jnp.where(mask[None], w_s, 0.0).astype(BF16)
    bias_tab = jnp.repeat(b_s.T, MIX_HALF // SGU_GROUPS, axis=1)
    return wm, bias_tab


def _sgu_fwd(proj, ln_gain, ln_bias, wm, bias_tab, name):
    S = proj.shape[0]
    nch = SGU_TS // SGU_CHUNK

    def body(zu_ref, zv_ref, g_ref, b_ref, w_ref, bt_ref, o_ref):
        left, right = _half_masks(SGU_CHUNK)
        _, _, vn = _sgu_norm(zv_ref[...], g_ref[...], b_ref[...])
        for ch in range(nch):
            rows = pl.ds(ch * SGU_CHUNK, SGU_CHUNK)
            for p in range(N_PAIRS):
                cols = pl.ds(p * LANES, LANES)
                vp = vn[ch * SGU_CHUNK:(ch + 1) * SGU_CHUNK, p * LANES:(p + 1) * LANES]
                mixed = (jnp.dot(w_ref[2 * p], (vp * left).astype(BF16), preferred_element_type=F32)
                         + jnp.dot(w_ref[2 * p + 1], (vp * right).astype(BF16), preferred_element_type=F32) + bt_ref[:, cols])
                o_ref[rows, cols] = _gelu(zu_ref[rows, cols]) * mixed

    vec = _vec_spec(MIX_HALF)
    return pl.pallas_call(
        body, name=name, grid=(S // SGU_TS,),
        in_specs=[pl.BlockSpec((SGU_TS, MIX_HALF), lambda i: (i, 1)), pl.BlockSpec((SGU_TS, MIX_HALF), lambda i: (i, 2)), vec, vec,
                  pl.BlockSpec((SGU_GROUPS, SGU_CHUNK, SGU_CHUNK), lambda i: (0, 0, 0)), pl.BlockSpec((SGU_CHUNK, MIX_HALF), lambda i: (0, 0))],
        out_specs=_row_spec(MIX_HALF, SGU_TS), out_shape=jax.ShapeDtypeStruct((S, MIX_HALF), F32),
        compiler_params=_cparams("parallel"))(proj, proj, ln_gain, ln_bias, wm, bias_tab)


def _sgu_bwd(dout, proj, ln_gain, ln_bias, wm, bias_tab, name):
    S = proj.shape[0]
    nch = SGU_TS // SGU_CHUNK
    nt_dims = (((1,), (1,)), ((), ()))
    tn_dims = (((0,), (0,)), ((), ()))

    def body(do_ref, zu_ref, zv_ref, g_ref, b_ref, w_ref, bt_ref, dzu_ref, dzv_ref, dw_ref, dbt_ref, dg_ref, db_ref, dvn_ref):
        first = pl.program_id(0) == 0

        @pl.when(first)
        def _():
            dw_ref[...] = jnp.zeros_like(dw_ref)
            dbt_ref[...] = jnp.zeros_like(dbt_ref)
            dg_ref[...] = jnp.zeros_like(dg_ref)
            db_ref[...] = jnp.zeros_like(db_ref)

        left, right = _half_masks(SGU_CHUNK)
        zv = zv_ref[...]
        vhat, rstd, vn = _sgu_norm(zv, g_ref[...], b_ref[...])
        for ch in range(nch):
            rows = pl.ds(ch * SGU_CHUNK, SGU_CHUNK)
            for p in range(N_PAIRS):
                cols = pl.ds(p * LANES, LANES)
                vp = vn[ch * SGU_CHUNK:(ch + 1) * SGU_CHUNK, p * LANES:(p + 1) * LANES]
                vl, vr = (vp * left).astype(BF16), (vp * right).astype(BF16)
                mixed = (jnp.dot(w_ref[2 * p], vl, preferred_element_type=F32)
                         + jnp.dot(w_ref[2 * p + 1], vr, preferred_element_type=F32) + bt_ref[:, cols])
                zu = zu_ref[rows, cols]
                do = do_ref[rows, cols]
                dzu_ref[rows, cols] = do * mixed * _dgelu(zu)
                dmix = do * _gelu(zu)
                dbt_ref[:, cols] += dmix
                dl, dr = (dmix * left).astype(BF16), (dmix * right).astype(BF16)
                dw_ref[2 * p] += lax.dot_general(dl, vl, nt_dims, preferred_element_type=F32)
                dw_ref[2 * p + 1] += lax.dot_general(dr, vr, nt_dims, preferred_element_type=F32)
                dvn_ref[rows, cols] = (lax.dot_general(w_ref[2 * p], dl, tn_dims, preferred_element_type=F32)
                                       + lax.dot_general(w_ref[2 * p + 1], dr, tn_dims, preferred_element_type=F32))
        dvn = dvn_ref[...]
        dg_ref[...] += jnp.sum(dvn * vhat, axis=0, keepdims=True)
        db_ref[...] += jnp.sum(dvn, axis=0, keepdims=True)
        dvh = dvn * g_ref[...]
        dv = rstd * (dvh - jnp.mean(dvh, axis=-1, keepdims=True) - vhat * jnp.mean(dvh * vhat, axis=-1, keepdims=True))
        dzv_ref[...] = dv * _dgelu(zv)

    vec = _vec_spec(MIX_HALF)
    row = _row_spec(MIX_HALF, SGU_TS)
    wspec = pl.BlockSpec((SGU_GROUPS, SGU_CHUNK, SGU_CHUNK), lambda i: (0, 0, 0))
    tspec = pl.BlockSpec((SGU_CHUNK, MIX_HALF), lambda i: (0, 0))
    full = jax.ShapeDtypeStruct((S, MIX_HALF), F32)
    v = jax.ShapeDtypeStruct((1, MIX_HALF), F32)
    return pl.pallas_call(
        body, name=name, grid=(S // SGU_TS,),
        in_specs=[row, pl.BlockSpec((SGU_TS, MIX_HALF), lambda i: (i, 1)), pl.BlockSpec((SGU_TS, MIX_HALF), lambda i: (i, 2)), vec, vec,
                  wspec, tspec],
        out_specs=[row, row, wspec, tspec, vec, vec],
        out_shape=[full, full, jax.ShapeDtypeStruct((SGU_GROUPS, SGU_CHUNK, SGU_CHUNK), F32),
                   jax.ShapeDtypeStruct((SGU_CHUNK, MIX_HALF), F32), v, v],
        scratch_shapes=[pltpu.VMEM((SGU_TS, MIX_HALF), F32)],
        compiler_params=_cparams("arbitrary"))(dout, proj, proj, ln_gain, ln_bias, wm, bias_tab)


def _sgu_grads(dw, dbias_tab):
    mask = jnp.tril(jnp.ones((SGU_CHUNK, SGU_CHUNK), dtype=bool))
    g_w = jnp.where(mask[None], dw, 0.0)
    g_b = dbias_tab.reshape(SGU_CHUNK, SGU_GROUPS, MIX_HALF // SGU_GROUPS).sum(axis=-1).T
    return g_w, g_b


def _head_avg_matrix(w):
    idx = np.arange(w) // HEAD_DIM
    return jnp.asarray((idx[:, None] == idx[None, :]).astype(np.float32) / HEAD_DIM, dtype=BF16)


def _head_mean(t, bavg):
    hi = t.astype(BF16)
    lo = (t - hi.astype(F32)).astype(BF16)
    return jnp.dot(hi, bavg, preferred_element_type=F32) + jnp.dot(lo, bavg, preferred_element_type=F32)


def _head_rms(t, bavg):
    r = lax.rsqrt(_head_mean(t * t, bavg) + EPS)
    return t * r, r


def _head_rms_bwd(dn, n, r, bavg):
    return r * (dn - n * _head_mean(dn * n, bavg))


GLA_TS = 512
C = GLA_CHUNK
NT_DIMS = (((1,), (1,)), ((), ()))
TN_DIMS = (((0,), (0,)), ((), ()))
HI = lax.Precision.HIGHEST


def _bdot(a, b, dims=(((1,), (0,)), ((), ()))):
    return lax.dot_general(a.astype(BF16), b.astype(BF16), dims, preferred_element_type=F32)


def _gla_chunk_terms(q, k, z):
    row = lax.broadcasted_iota(jnp.int32, (C, C), 0)
    col = lax.broadcasted_iota(jnp.int32, (C, C), 1)
    lc = _log_sigmoid(z) * (1.0 / GLA_TAU)
    b = lax.dot_general((row >= col).astype(F32), lc, (((1,), (0,)), ((), ())), precision=HI, preferred_element_type=F32)
    b_last = jnp.sum(lc, axis=0, keepdims=True)
    b_mid = b[C // 2:C // 2 + 1, :]
    scale = HEAD_DIM ** -0.5
    e_b, e_q, e_k, e_l = jnp.exp(b), jnp.exp(b - b_mid), jnp.exp(b_mid - b), jnp.exp(b_last - b)
    qs = q * (scale * e_b)
    qe = q * (scale * e_q)
    ke = k * e_k
    kl = k * e_l
    return dict(e_b=e_b, e_q=e_q, e_k=e_k, e_l=e_l, qs=qs, qe=qe, ke=ke, kl=kl, dec=jnp.exp(b_last), causal=row >= col, scale=scale)


def _pair(x, pp):
    return x[:, pp * LANES:(pp + 1) * LANES]


def _pair_block_diag():
    r = lax.broadcasted_iota(jnp.int32, (LANES, LANES), 0) // HEAD_DIM
    c = lax.broadcasted_iota(jnp.int32, (LANES, LANES), 1) // HEAD_DIM
    return (r == c).astype(F32)


def _gla_fwd(proj, z, name):
    S = proj.shape[0]
    nch = GLA_TS // C

    def body(q_ref, k_ref, v_ref, z_ref, o_ref, st_ref, state_ref):
        @pl.when(pl.program_id(0) == 0)
        def _():
            state_ref[...] = jnp.zeros_like(state_ref)

        left, right = _half_masks(C)
        bd = _pair_block_diag()
        pairs = range(N_PAIRS)
        for ch in range(nch):
            rows = pl.ds(ch * C, C)
            v = v_ref[rows, :]
            t = _gla_chunk_terms(q_ref[rows, :], k_ref[rows, :], z_ref[rows, :])
            sts = [state_ref[pp] for pp in pairs]
            for pp in pairs:
                st_ref[ch, pp] = sts[pp]
            os = [_bdot(_pair(t["qs"], pp), sts[pp], NT_DIMS) for pp in pairs]
            for m in (left, right):
                scores = [jnp.where(t["causal"], _bdot(_pair(t["qe"], pp) * m, _pair(t["ke"], pp), NT_DIMS), 0.0) for pp in pairs]
                os = [os[pp] + m * _bdot(scores[pp], _pair(v, pp)) for pp in pairs]
            o_ref[rows, :] = jnp.concatenate(os, axis=1)
            new = [sts[pp] * _pair(t["dec"], pp) + bd * _bdot(_pair(v, pp), _pair(t["kl"], pp), TN_DIMS) for pp in pairs]
            for pp in pairs:
                state_ref[pp] = new[pp]

    def col(cb):
        return pl.BlockSpec((GLA_TS, MIX_HALF), lambda i: (i, cb))

    return pl.pallas_call(
        body, name=name, grid=(S // GLA_TS,),
        in_specs=[col(0), col(1), col(2), col(0)],
        out_specs=[col(0), pl.BlockSpec((nch, N_PAIRS, LANES, LANES), lambda i: (i, 0, 0, 0))],
        out_shape=[jax.ShapeDtypeStruct((S, MIX_HALF), F32), jax.ShapeDtypeStruct((S // C, N_PAIRS, LANES, LANES), F32)],
        scratch_shapes=[pltpu.VMEM((N_PAIRS, LANES, LANES), F32)], compiler_params=_cparams("arbitrary"))(proj, proj, proj, z)


def _gla_bwd(do, proj, z, states, name):
    S = proj.shape[0]
    nch = GLA_TS // C
    nblk = S // GLA_TS

    def body(do_ref, q_ref, k_ref, v_ref, z_ref, st_ref, dq_ref, dk_ref, dv_ref, dlc_ref, dstate_ref):
        @pl.when(pl.program_id(0) == 0)
        def _():
            dstate_ref[...] = jnp.zeros_like(dstate_ref)

        left, right = _half_masks(C)
        bd = _pair_block_diag()
        rowi = lax.broadcasted_iota(jnp.int32, (C, LANES), 0)
        row = lax.broadcasted_iota(jnp.int32, (C, C), 0)
        colm = lax.broadcasted_iota(jnp.int32, (C, C), 1)
        pairs = range(N_PAIRS)
        rowi = lax.broadcasted_iota(jnp.int32, (C, MIX_HALF), 0)
        for ch in range(nch - 1, -1, -1):
            rows = pl.ds(ch * C, C)
            v, dov = v_ref[rows, :], do_ref[rows, :]
            t = _gla_chunk_terms(q_ref[rows, :], k_ref[rows, :], z_ref[rows, :])
            sts = [st_ref[ch, pp] for pp in pairs]
            nxt = [dstate_ref[pp] for pp in pairs]
            gs = [bd * nxt[pp] for pp in pairs]
            dqs = [_bdot(_pair(dov, pp), sts[pp]) for pp in pairs]
            dv = [_bdot(_pair(t["kl"], pp), gs[pp], NT_DIMS) for pp in pairs]
            dkl = [_bdot(_pair(v, pp), gs[pp]) for pp in pairs]
            dqe = [jnp.zeros((C, LANES), F32) for _ in pairs]
            dke = [jnp.zeros((C, LANES), F32) for _ in pairs]
            for m in (left, right):
                sc = [jnp.where(t["causal"], _bdot(_pair(t["qe"], pp) * m, _pair(t["ke"], pp), NT_DIMS), 0.0) for pp in pairs]
                dsc = [jnp.where(t["causal"], _bdot(_pair(dov, pp) * m, _pair(v, pp), NT_DIMS), 0.0) for pp in pairs]
                dv = [dv[pp] + m * _bdot(sc[pp], _pair(dov, pp), TN_DIMS) for pp in pairs]
                dqe = [dqe[pp] + m * _bdot(dsc[pp], _pair(t["ke"], pp)) for pp in pairs]
                dke = [dke[pp] + m * _bdot(dsc[pp], _pair(t["qe"], pp), TN_DIMS) for pp in pairs]
            for pp in pairs:
                dstate_ref[pp] = bd * (nxt[pp] * _pair(t["dec"], pp) + _bdot(_pair(dov, pp), _pair(t["qs"], pp), TN_DIMS))
            decay_sum = jnp.concatenate([jnp.sum(nxt[pp] * sts[pp], axis=0, keepdims=True) for pp in pairs], axis=1)
            dqs, dv, dkl, dqe, dke = (jnp.concatenate(parts, axis=1) for parts in (dqs, dv, dkl, dqe, dke))
            db_last = decay_sum * t["dec"] + jnp.sum(dkl * t["kl"], axis=0, keepdims=True)
            db = dqs * t["qs"] + dqe * t["qe"] - dke * t["ke"] - dkl * t["kl"]
            db = db + jnp.where(rowi == C - 1, db_last, 0.0)
            dq_ref[rows, :] = (dqs * t["e_b"] + dqe * t["e_q"]) * t["scale"]
            dk_ref[rows, :] = dke * t["e_k"] + dkl * t["e_l"]
            dv_ref[rows, :] = dv
            dlc_ref[rows, :] = lax.dot_general((colm >= row).astype(F32), db, (((1,), (0,)), ((), ())), precision=HI,
                                               preferred_element_type=F32)

    def col(cb):
        return pl.BlockSpec((GLA_TS, MIX_HALF), lambda i: (nblk - 1 - i, cb))

    full = jax.ShapeDtypeStruct((S, MIX_HALF), F32)
    return pl.pallas_call(
        body, name=name, grid=(nblk,),
        in_specs=[col(0), col(0), col(1), col(2), col(0), pl.BlockSpec((nch, N_PAIRS, LANES, LANES), lambda i: (nblk - 1 - i, 0, 0, 0))],
        out_specs=[col(0)] * 4, out_shape=[full, full, full, full],
        scratch_shapes=[pltpu.VMEM((N_PAIRS, LANES, LANES), F32)], compiler_params=_cparams("arbitrary"))(do, proj, proj, proj, z, states)


def _gla_block_fwd(proj, w_lr_pad, b_lr, gain, bavg, pfx):
    z = _mm(proj, w_lr_pad, "nn", pfx + "_z", a_cols=(7 * MIX_HALF, MIX_HALF), epi=lambda acc, b: acc + b, extras=[(b_lr, "n")])
    o, states = _gla_fwd(proj, z, pfx + "_core")

    def out(ov, gg, ba, gn):
        n, _ = _head_rms(ov, ba)
        return n * gn * (gg * _sigmoid(gg))

    og, = _ew(out, pfx + "_out", [o, (proj, MIX_HALF, 3)], consts=[bavg, gain], outs=[(MIX_HALF, F32)])
    return og, dict(z=z, o=o, states=states)


def _gla_block_bwd(dog, proj, w_lr_pad, gain, bavg, res, pfx):
    z, o, states = res["z"], res["o"], res["states"]

    def out_bwd(dy, ov, gg, ba, gn):
        n, r = _head_rms(ov, ba)
        sg = _sigmoid(gg)
        silu = gg * sg
        dn = dy * gn * silu
        do = _head_rms_bwd(dn, n, r, ba)
        dgg = dy * n * gn * (sg * (1.0 + gg * (1.0 - sg)))
        return do, dgg, jnp.sum(dy * n * silu, axis=0, keepdims=True)

    do, dgg, dgain = _ew(out_bwd, pfx + "_out_bwd", [dog, o, (proj, MIX_HALF, 3)], consts=[bavg, gain],
                         outs=[(MIX_HALF, F32), (MIX_HALF, F32)], sums=[MIX_HALF])
    dq, dk, dv, dlc = _gla_bwd(do, proj, z, states, pfx + "_core_bwd")

    def decay_bwd(dl, zv):
        dz = dl * (1.0 / GLA_TAU) * (1.0 - _sigmoid(zv))
        return dz, jnp.sum(dz, axis=0, keepdims=True)

    dz, db_lr = _ew(decay_bwd, pfx + "_decay_bwd", [dlc, z], outs=[(MIX_HALF, F32)], sums=[MIX_HALF])
    dw_lr_pad = _mm(proj, dz, "tn", pfx + "_dwlr", a_cols=(7 * MIX_HALF, MIX_HALF))
    dsmall = _mm(dz, w_lr_pad, "nt", pfx + "_dsmall")
    return (dq, dk, dv, dgg, dsmall), dict(w_lr=dw_lr_pad[:GLA_RANK], b_lr=db_lr.reshape(-1), gain=dgain.reshape(-1, HEAD_DIM))


FOX_T = 512
FOX_HEADS = MIX_HALF // HEAD_DIM
NEG = -1e30
CUM_T = 512


def _cum_lanes(x, name, reverse, pre=None):
    R, S = x.shape
    nb = S // CUM_T

    def body(x_ref, o_ref, carry_ref):
        @pl.when(pl.program_id(0) == 0)
        def _():
            carry_ref[...] = jnp.zeros_like(carry_ref)

        xv = x_ref[...]
        if pre is not None:
            xv = pre(xv)
        i = lax.broadcasted_iota(jnp.int32, (CUM_T, CUM_T), 0)
        j = lax.broadcasted_iota(jnp.int32, (CUM_T, CUM_T), 1)
        tri = ((i >= j) if reverse else (i <= j)).astype(F32)
        c = lax.dot_general(xv, tri, (((1,), (0,)), ((), ())), precision=HI, preferred_element_type=F32)
        carry = carry_ref[...]
        o_ref[...] = c + carry[:, 0:1]
        carry_ref[...] = carry + jnp.sum(xv, axis=1, keepdims=True)

    spec = pl.BlockSpec((R, CUM_T), (lambda i: (0, nb - 1 - i)) if reverse else (lambda i: (0, i)))
    return pl.pallas_call(body, name=name, grid=(nb,), in_specs=[spec], out_specs=spec, out_shape=jax.ShapeDtypeStruct((R, S), F32),
                          scratch_shapes=[pltpu.VMEM((R, LANES), F32)], compiler_params=_cparams("arbitrary"))(x)


def _fox_scores(q, k, cqb, ck_ref, h, m, diag):
    cq = cqb[:, h * HEAD_DIM:h * HEAD_DIM + 1]
    ck = ck_ref[0, h:h + 1, :]
    s = lax.dot_general(q * m.astype(q.dtype), k, NT_DIMS, preferred_element_type=F32) + (cq - ck)
    if not diag:
        return s
    row = lax.broadcasted_iota(jnp.int32, (FOX_T, FOX_T), 0)
    col = lax.broadcasted_iota(jnp.int32, (FOX_T, FOX_T), 1)
    return jnp.where(row < col, NEG, s)


def _on_causal_blocks(q_blk, k_blk, step):
    @pl.when(k_blk < q_blk)
    def _():
        step(False)

    @pl.when(k_blk == q_blk)
    def _():
        step(True)


def _causal_pairs(n, key_major):
    if key_major:
        pairs = [(q, k) for k in range(n) for q in range(k, n)]
    else:
        pairs = [(q, k) for q in range(n) for k in range(q + 1)]
    return jnp.asarray([p[0] for p in pairs], jnp.int32), jnp.asarray([p[1] for p in pairs], jnp.int32)


def _carried(carry, refs, n_in, n_out, first, last):
    if carry is None:
        return refs
    ins, cx_ref, outs, co_ref = refs[:n_in], refs[n_in], refs[n_in + 1:n_in + 1 + n_out], refs[n_in + 1 + n_out]
    scratch = refs[n_in + 2 + n_out:]
    start, finish = _exchange_plan(cx_ref, co_ref, *scratch[-3:], carry[1])
    pl.when(first)(start)
    pl.when(last)(finish)
    return ins + outs + scratch[:-3]


def _carry_specs(carry):
    if carry is None:
        return [], [], [], [], []
    x, bcast = carry
    blk = x.shape if bcast else x.shape[1:]
    return [ANY], [ANY], [jax.ShapeDtypeStruct((N_CHIPS,) + tuple(blk), x.dtype)], list(_EXCHANGE_SEMS), [x]


def _fox_fwd(qn, kn, proj, cum_b, cum_tp, name, carry=None):
    S = qn.shape[0]
    nq = S // FOX_T
    qidx, kidx = _causal_pairs(nq, False)
    ntri = int(qidx.shape[0])

    def body(qidx_ref, kidx_ref, *refs):
        t = pl.program_id(1)
        first = jnp.logical_and(pl.program_id(0) == 0, t == 0)
        last = jnp.logical_and(pl.program_id(0) == N_PAIRS - 1, t == ntri - 1)
        q_ref, k_ref, v_ref, cq_ref, ck_ref, o_ref, lse_ref, m_scr, acc_scr = _carried(carry, refs, 5, 2, first, last)
        qi, ki = qidx_ref[t], kidx_ref[t]

        @pl.when(ki == 0)
        def _():
            m_scr[...] = jnp.full_like(m_scr, NEG)
            acc_scr[...] = jnp.zeros_like(acc_scr)

        left, right = _half_masks(FOX_T)

        def step(diag):
            q, k, v = q_ref[...], k_ref[...], v_ref[...].astype(BF16)
            cqb = cq_ref[...]
            for h, m in enumerate((left, right)):
                s = _fox_scores(q, k, cqb, ck_ref, h, m, diag)
                m_prev = m_scr[h]
                m_new = jnp.maximum(m_prev, jnp.max(s, axis=1, keepdims=True))
                p = jnp.exp(s - m_new)
                v_h = jnp.where(m > 0, v, jnp.ones_like(v))
                acc_scr[h] = jnp.exp(m_prev - m_new) * acc_scr[h] + jnp.dot(p.astype(BF16), v_h, preferred_element_type=F32)
                m_scr[h] = m_new

        _on_causal_blocks(qi, ki, step)

        @pl.when(ki == qi)
        def _():
            a0, a1 = acc_scr[0], acc_scr[1]
            is_left = left > 0
            num = jnp.where(is_left, a0, a1)
            den = jnp.where(is_left, pltpu.roll(a0, HEAD_DIM, 1), pltpu.roll(a1, HEAD_DIM, 1))
            o_ref[...] = num / den
            lse_ref[...] = jnp.where(is_left, m_scr[0], m_scr[1]) + jnp.log(den)

    qspec = pl.BlockSpec((FOX_T, LANES), lambda p, t, qx, kx: (qx[t], p))
    kspec = pl.BlockSpec((FOX_T, LANES), lambda p, t, qx, kx: (kx[t], p))
    vspec = pl.BlockSpec((FOX_T, LANES), lambda p, t, qx, kx: (kx[t], 6 * N_PAIRS + p))
    ckspec = pl.BlockSpec((1, 8, FOX_T), lambda p, t, qx, kx: (p, 0, kx[t]))
    full = jax.ShapeDtypeStruct((S, MIX_HALF), F32)
    c_in, c_out, c_shape, c_scratch, c_args = _carry_specs(carry)
    grid_spec = pltpu.PrefetchScalarGridSpec(
        num_scalar_prefetch=2, grid=(N_PAIRS, ntri), in_specs=[qspec, kspec, vspec, qspec, ckspec] + c_in, out_specs=[qspec, qspec] + c_out,
        scratch_shapes=[pltpu.VMEM((2, FOX_T, 1), F32), pltpu.VMEM((2, FOX_T, LANES), F32)] + c_scratch)
    return pl.pallas_call(body, name=name, grid_spec=grid_spec, out_shape=[full, full] + c_shape,
                          compiler_params=_cparams("arbitrary", "arbitrary"))(qidx, kidx, qn, kn, proj, cum_b, cum_tp, *c_args)


def _fox_bwd(do, qn, kn, proj, cum_b, cum_tp, lse_b, delta_b, name, carry=None, do_pair0=0):
    S = qn.shape[0]
    nq = S // FOX_T
    scale = HEAD_DIM ** -0.5
    qidx, kidx = _causal_pairs(nq, True)
    ntri = int(qidx.shape[0])

    def body(qidx_ref, kidx_ref, *refs):
        t = pl.program_id(1)
        first = jnp.logical_and(pl.program_id(0) == 0, t == 0)
        last = jnp.logical_and(pl.program_id(0) == N_PAIRS - 1, t == ntri - 1)
        (do_ref, q_ref, k_ref, v_ref, cq_ref, ck_ref, lse_ref, dl_ref, dq_ref, dcq_ref, dk_ref, dv_ref, dck_ref,
         dq_scr, dk_scr, dv_scr) = _carried(carry, refs, 8, 5, first, last)
        qi, ki = qidx_ref[t], kidx_ref[t]

        @pl.when(t == 0)
        def _():
            dq_scr[...] = jnp.zeros_like(dq_scr)

        @pl.when(qi == ki)
        def _():
            dk_scr[...] = jnp.zeros_like(dk_scr)
            dv_scr[...] = jnp.zeros_like(dv_scr)

        left, right = _half_masks(FOX_T)
        rows = pl.ds(pl.multiple_of(qi * FOX_T, FOX_T), FOX_T)

        def step(diag):
            q, k, v, dov = q_ref[...], k_ref[...], v_ref[...].astype(BF16), do_ref[...]
            cqb, lseb, dlb = cq_ref[...], lse_ref[...], dl_ref[...]
            dob = dov.astype(BF16)
            heads = (0, 1)
            masks = (left, right)
            col = [slice(h * HEAD_DIM, h * HEAD_DIM + 1) for h in heads]
            ss = [_fox_scores(q, k, cqb, ck_ref, h, masks[h], diag) for h in heads]
            dps = [lax.dot_general((dov * masks[h]).astype(BF16), v, NT_DIMS, preferred_element_type=F32) for h in heads]
            ps = [jnp.exp(ss[h] - lseb[:, col[h]]) for h in heads]
            dss = [(ps[h] * (dps[h] - dlb[:, col[h]])).astype(BF16) for h in heads]
            pvs = [lax.dot_general(ps[h].astype(BF16), dob, TN_DIMS, preferred_element_type=F32) for h in heads]
            dks = [lax.dot_general(dss[h], jnp.where(masks[h] > 0, q, jnp.ones_like(q)), TN_DIMS, preferred_element_type=F32) for h in heads]
            dqs = [jnp.dot(dss[h], jnp.where(masks[h] > 0, k, jnp.ones_like(k)), preferred_element_type=F32) for h in heads]
            dv_scr[...] = dv_scr[...] + left * pvs[0] + right * pvs[1]
            for h in heads:
                dk_scr[h] = dk_scr[h] + dks[h]
                dq_scr[h, rows, :] = dq_scr[h, rows, :] + dqs[h]

        _on_causal_blocks(qi, ki, step)

        @pl.when(qi == nq - 1)
        def _():
            a0, a1 = dk_scr[0], dk_scr[1]
            dk_ref[...] = left * a0 + right * a1
            dv_ref[...] = dv_scr[...]
            dck_ref[...] = left * pltpu.roll(a0, HEAD_DIM, 1) + right * pltpu.roll(a1, HEAD_DIM, 1)

        @pl.when(t == ntri - 1)
        def _():
            for r in range(nq):
                blk = pl.ds(r * FOX_T, FOX_T)
                a0, a1 = dq_scr[0, blk, :], dq_scr[1, blk, :]
                dq_ref[blk, :] = (left * a0 + right * a1) * scale
                dcq_ref[blk, :] = left * pltpu.roll(a0, HEAD_DIM, 1) + right * pltpu.roll(a1, HEAD_DIM, 1)

    qspec = pl.BlockSpec((FOX_T, LANES), lambda p, t, qx, kx: (qx[t], p))
    kspec = pl.BlockSpec((FOX_T, LANES), lambda p, t, qx, kx: (kx[t], p))
    vspec = pl.BlockSpec((FOX_T, LANES), lambda p, t, qx, kx: (kx[t], 6 * N_PAIRS + p))
    ckspec = pl.BlockSpec((1, 8, FOX_T), lambda p, t, qx, kx: (p, 0, kx[t]))
    seq = pl.BlockSpec((S, LANES), lambda p, t, qx, kx: (0, p))
    full = jax.ShapeDtypeStruct((S, MIX_HALF), F32)
    c_in, c_out, c_shape, c_scratch, c_args = _carry_specs(carry)
    grid_spec = pltpu.PrefetchScalarGridSpec(
        num_scalar_prefetch=2, grid=(N_PAIRS, ntri),
        in_specs=[pl.BlockSpec((FOX_T, LANES), lambda p, t, qx, kx: (qx[t], do_pair0 + p)), qspec, kspec, vspec, qspec, ckspec, qspec, qspec] + c_in,
        out_specs=[seq, seq, kspec, kspec, kspec] + c_out,
        scratch_shapes=[pltpu.VMEM((2, S, LANES), F32), pltpu.VMEM((2, FOX_T, LANES), F32), pltpu.VMEM((FOX_T, LANES), F32)] + c_scratch)
    return pl.pallas_call(body, name=name, grid_spec=grid_spec, out_shape=[full] * 5 + c_shape,
                          compiler_params=_cparams("arbitrary", "arbitrary"))(qidx, kidx, do, qn, kn, proj, cum_b, cum_tp, lse_b, delta_b, *c_args)


def _ff_bwd(rc, f_t, name):
    def body(rc_ref, f_ref, d_ref, s_ref):
        d = rc_ref[...] * (1.0 - _sigmoid(f_ref[...]))
        d_ref[...] = d
        s_ref[...] = jnp.sum(d, axis=1, keepdims=True)

    return pl.pallas_call(body, name=name, out_shape=[jax.ShapeDtypeStruct(rc.shape, F32), jax.ShapeDtypeStruct((rc.shape[0], 1), F32)])(rc, f_t)


def _fox_block_fwd(proj, b_f, q_gain, k_gain, bavg, pfx, carry=None):
    S = proj.shape[0]

    def prep(qv, kv, ba, qg, kg):
        return _head_rms(qv, ba)[0] * qg * (HEAD_DIM ** -0.5), _head_rms(kv, ba)[0] * kg

    qn, kn = _ew(prep, pfx + "_prep", [(proj, MIX_HALF, 4), (proj, MIX_HALF, 5)], consts=[bavg, q_gain, k_gain],
                 outs=[(MIX_HALF, BF16), (MIX_HALF, BF16)])
    f0 = 7 * MIX_HALF + GLA_RANK
    f_t = proj[:, f0:f0 + FOX_HEADS].T + b_f.reshape(FOX_HEADS, 1)
    cum = _cum_lanes(f_t, pfx + "_cum", False, pre=_log_sigmoid)
    cum_b = jnp.repeat(cum.T, HEAD_DIM, axis=1)
    cum_tp = jnp.pad(cum.reshape(N_PAIRS, 2, S), ((0, 0), (0, 6), (0, 0)))
    o, lse_b, *carried = _fox_fwd(qn, kn, proj, cum_b, cum_tp, pfx + "_attn", carry=carry)
    return o, dict(qn=qn, kn=kn, f_t=f_t, cum_b=cum_b, cum_tp=cum_tp, o=o, lse_b=lse_b), carried


def _fox_block_bwd(do, proj, q_gain, k_gain, bavg, res, pfx, carry=None):
    qn, kn, o = res["qn"], res["kn"], res["o"]
    S = proj.shape[0]
    delta_b, = _ew(lambda a, b, ba: _head_mean(a * b, ba) * float(HEAD_DIM), pfx + "_delta", [do, o], consts=[bavg], outs=[(MIX_HALF, F32)])
    do_arr, do_blk = (do[0], do[2]) if isinstance(do, tuple) else (do, 0)
    args = (do_arr, qn, kn, proj, res["cum_b"], res["cum_tp"], res["lse_b"], delta_b)
    dqn, dcq_b, dkn, dv, dck_b, *carried = _fox_bwd(*args, pfx + "_bwd", carry=carry, do_pair0=do_blk * N_PAIRS)

    def prep_bwd(dq, dk, qv, kv, ba, qg, kg):
        nq, rq = _head_rms(qv, ba)
        nk, rk = _head_rms(kv, ba)
        return (_head_rms_bwd(dq * qg, nq, rq, ba), _head_rms_bwd(dk * kg, nk, rk, ba),
                jnp.sum(dq * nq, axis=0, keepdims=True), jnp.sum(dk * nk, axis=0, keepdims=True))

    dfq, dfk, dqg, dkg = _ew(prep_bwd, pfx + "_prep_bwd", [dqn, dkn, (proj, MIX_HALF, 4), (proj, MIX_HALF, 5)],
                             consts=[bavg, q_gain, k_gain], outs=[(MIX_HALF, F32), (MIX_HALF, F32)], sums=[MIX_HALF, MIX_HALF])
    dcum = (dcq_b - dck_b)[:, ::HEAD_DIM].T
    rc = _cum_lanes(dcum, pfx + "_rcum", True)
    dff_t, db_f = _ff_bwd(rc, res["f_t"], pfx + "_ff_bwd")
    grads = dict(b_f=db_f.reshape(-1), q_gain=dqg.reshape(-1, HEAD_DIM), k_gain=dkg.reshape(-1, HEAD_DIM))
    return (dfq, dfk, dv, dff_t.T), grads, carried


WEIGHTS = ['ada_w', 'ada_b', 'even_w_in', 'even_w_out', 'gla_w_lr', 'gla_b_lr', 'gla_gain', 'fox_b_f', 'fox_q_gain', 'fox_k_gain',
           'odd_w_in', 'odd_w_out', 's5_lam_re', 's5_lam_im', 's5_log_dt', 's5_b_re', 's5_b_im', 's5_c_re', 's5_c_im', 's5_d',
           's5_w_glu', 's5_b_glu', 'sgu_ln_gain', 'sgu_ln_bias', 'sgu_w_s', 'sgu_b_s', 'mlp_w1', 'mlp_w2']
ARGS = ['x', 'c'] + WEIGHTS + ['loss_target'] + ['m_' + w for w in WEIGHTS] + ['v_' + w for w in WEIGHTS]

EVEN_COLS = 3608
EVEN_PAD = 8 * MIX_HALF
MOD = 6 * D_MODEL
MOD_SHARD = MOD // N_CHIPS

PACK_COLS = 1024
EVEN_SHARD = EVEN_COLS // N_CHIPS
SHARDED = (
    ([("even_w_in", (1, 1024, PACK_COLS), 2), ("even_w_out", (1, 256, 1024), 1), ("gla_w_lr", (1, 16, 128), 2)], 1536),
    ([("mlp_w1_0", (1, 1024, 1024), 2), ("mlp_w2_0", (1, 1024, 1024), 1), ("odd_w_in", (1, 1024, 384), 2),
      ("odd_w_out", (1, 256, 1024), 1), ("mlp_w1_1", (1, 1024, 1024), 2), ("mlp_w2_1", (1, 1024, 1024), 1),
      ("s5_w_glu", (1, 128, 512), 1), ("s5_b_glu", (1, 128), 1), ("sgu_ln_gain", (1, 128), 1), ("sgu_ln_bias", (1, 128), 1)], 5120))
REPLICATED = [("gla_b_lr", (1, 512)), ("gla_gain", (1, 8, 64)), ("fox_b_f", (1, 8)), ("fox_q_gain", (1, 8, 64)),
              ("fox_k_gain", (1, 8, 64)), ("s5_lam_re", (1, 32, 64)), ("s5_lam_im", (1, 32, 64)), ("s5_log_dt", (1, 32)),
              ("s5_b_re", (1, 32, 64, 16)), ("s5_b_im", (1, 32, 64, 16)), ("s5_c_re", (1, 32, 16, 64)), ("s5_c_im", (1, 32, 16, 64)),
              ("s5_d", (1, 32, 16)), ("sgu_w_s", (1, 8, 128, 128)), ("sgu_b_s", (1, 8, 128))]
SMALL_ROWS = 512
BIG_ADAM = {"ada_w": (2048, 1536), "even_w_in": (1024, 902), "even_w_out": (256, 1024), "odd_w_in": (1024, 384),
            "odd_w_out": (256, 1024), "mlp_w1": (2048, 1024), "mlp_w2": (2048, 1024), "s5_w_glu": (128, 512)}


PACK_ALIGN = 16


def _piece_rows(shape):
    rows = -(-math.prod(shape) // PACK_COLS)
    return -(-rows // PACK_ALIGN) * PACK_ALIGN


def _to_rows(p, lead=()):
    n = math.prod(p.shape[len(lead):])
    rows = _piece_rows(p.shape[len(lead):])
    flat = p.reshape(lead + (n,))
    if rows * PACK_COLS != n:
        flat = jnp.pad(flat, [(0, 0)] * len(lead) + [(0, rows * PACK_COLS - n)])
    return flat.reshape(lead + (rows, PACK_COLS))


def _from_rows(x, r0, shape, lead=()):
    n = math.prod(shape)
    seg = lax.slice_in_dim(x, r0, r0 + _piece_rows(shape), axis=len(lead)).reshape(lead + (-1,))
    return lax.slice_in_dim(seg, 0, n, axis=len(lead)).reshape(lead + tuple(shape))


def _pack_rows(pieces, rows):
    x = jnp.concatenate([_to_rows(p) for p in pieces], axis=0)
    return jnp.pad(x, ((0, rows - x.shape[0]), (0, 0)))


def _unpack(x, specs):
    out, r0 = {}, 0
    for name, shape in specs:
        out[name] = _from_rows(x, r0, shape)
        r0 += _piece_rows(shape)
    return out


def _shards_to_full(x4, pieces):
    out, r0 = {}, 0
    for name, shape, axis in pieces:
        seg = _from_rows(x4, r0, shape, lead=(N_CHIPS,))
        out[name] = jnp.concatenate([seg[k] for k in range(N_CHIPS)], axis=axis)
        r0 += _piece_rows(shape)
    return out


def _full_to_shards(full, pieces, rows):
    blocks = [_to_rows(jnp.stack(jnp.split(full[name], N_CHIPS, axis=axis)), lead=(N_CHIPS,)) for name, _, axis in pieces]
    x = jnp.concatenate(blocks, axis=1)
    return jnp.pad(x, ((0, 0), (0, rows - x.shape[1]), (0, 0)))


def _gather_prep(local, pieces, rows):
    shard = _pack_rows([local[n] for n, _, _ in pieces], rows).astype(BF16)
    return lax.dynamic_slice_in_dim(shard, lax.axis_index("c") * (rows // 2), rows // 2, axis=0)


def _gather_finish(collected, pieces, rows, tag):
    halves = _by_core(collected, _pair_swap(collected, tag + "_pair"))
    return _shards_to_full(halves.transpose(1, 0, 2, 3).reshape(N_CHIPS, rows, PACK_COLS), pieces)


def _reduce_prep(full, pieces, rows, tag):
    mc = lax.axis_index("c")
    packed = _full_to_shards(full, pieces, rows)
    hr = rows // 2
    mine = lax.dynamic_slice_in_dim(packed, mc * hr, hr, axis=1)
    other = lax.dynamic_slice_in_dim(packed, (1 - mc) * hr, hr, axis=1)
    theirs = _pair_swap(other.astype(BF16), tag + "_pair")
    pair_sum, = _ew(lambda p, q: p + q, tag + "_pair_sum", [mine.reshape(N_CHIPS * hr, PACK_COLS), theirs.reshape(N_CHIPS * hr, PACK_COLS)],
                    outs=[(PACK_COLS, BF16)])
    return pair_sum.reshape(N_CHIPS, hr, PACK_COLS)


def _reduce_finish(arrived, pieces, rows, tag):
    red_half = _sum_slots(arrived, tag + "_chip_sum")
    reduced = _by_core(red_half, _pair_swap(red_half, tag + "_pair_out")).reshape(rows, PACK_COLS)
    return _unpack(reduced, [(n, s) for n, s, _ in pieces])


def _relu2(t):
    r = jnp.maximum(t, 0.0)
    return r * r


def _silu(t):
    return t * _sigmoid(t)


def _pack_even(w):
    return jnp.concatenate([w[:, :2048], w[:, 2064:3600], w[:, 2048:2064], w[:, 3600:3608],
                            jnp.zeros((w.shape[0], EVEN_PAD - EVEN_COLS), w.dtype)], axis=1)


def _unpack_even(wp):
    return jnp.concatenate([wp[:, :2048], wp[:, 3584:3600], wp[:, 2048:3584], wp[:, 3600:3608]], axis=1)


def _mlp_fwd(h, w1, w2, pfx):
    pre = _mm(h, w1, "nn", pfx + "_up", out_dtype=BF16)
    return pre, _mm(pre, w2, "nn", pfx + "_down", a_pro=_relu2, out_dtype=BF16)


def _mlp_bwd(dm, h, pre, w1, w2, pfx):
    dpre = _mm(dm, w2, "nt", pfx + "_dpre", epi=lambda acc, p: acc * (2.0 * jnp.maximum(p, 0.0)), extras=[(pre, "mn")], out_dtype=BF16)
    dw2 = _mm(pre, dm, "tn", pfx + "_dw2", a_pro=_relu2)
    dw1 = _mm(h, dpre, "tn", pfx + "_dw1")
    dh = _mm(dpre, w1, "nt", pfx + "_dh", out_dtype=BF16)
    return dh, dw1, dw2


def _step(args):
    a = dict(zip(ARGS, args, strict=True))
    x0 = a["x"][0]
    target = a["loss_target"][0]
    mx, my, mc = lax.axis_index("x"), lax.axis_index("y"), lax.axis_index("c")
    chip = 2 * mx + my
    dev = 2 * chip + mc
    bavg = _head_avg_matrix(MIX_HALF)

    c_all = _gather8(jnp.pad(a["c"], ((0, 7), (0, 0))), "c_gather")[:, :, 0, :].reshape(2 * N_CHIPS, D_MODEL)
    ada_b_shard = lax.dynamic_slice_in_dim(a["ada_b"], chip * MOD_SHARD, MOD_SHARD, axis=1)
    mod_sh = [_mm(c_all, a["ada_w"][l], "nn", f"mod{l}", a_pro=_silu, epi=lambda acc, b: acc + b, extras=[(ada_b_shard[l:l + 1], "n")])
              for l in range(2)]
    small3 = jnp.zeros((8, MOD_SHARD), F32)
    for r, n in enumerate(("s5_b_glu", "sgu_ln_gain", "sgu_ln_bias")):
        small3 = small3.at[r, :LANES].set(a[n][0])
    mod_all = _chip_exchange(jnp.concatenate(mod_sh + [small3]), "mod_gather", True)
    mods = []
    for l in range(2):
        full = mod_all[:, 8 * l:8 * l + 8].transpose(1, 0, 2).reshape(8, MOD)
        mods.append(jnp.split(lax.dynamic_slice_in_dim(full, dev, 1, axis=0), 6, axis=1))
    b_glu, ln_gain, ln_bias = (mod_all[:, 16 + r, :LANES].reshape(1, MIX_HALF) for r in range(3))

    local = dict(a, even_w_in=jnp.pad(a["even_w_in"], ((0, 0), (0, 0), (0, PACK_COLS - EVEN_SHARD))),
                 mlp_w1_0=a["mlp_w1"][0:1], mlp_w1_1=a["mlp_w1"][1:2], mlp_w2_0=a["mlp_w2"][0:1], mlp_w2_1=a["mlp_w2"][1:2])
    (pieces0, rows0), (pieces1, rows1) = SHARDED
    w = _gather_finish(_chip_exchange(_gather_prep(local, pieces0, rows0), "w0_chips", True), pieces0, rows0, "w0")
    w_even = _pack_even(w["even_w_in"][0].reshape(D_MODEL, N_CHIPS, PACK_COLS)[:, :, :EVEN_SHARD].reshape(D_MODEL, EVEN_COLS))
    w_lr_pad = jnp.zeros((MIX_HALF, MIX_HALF), BF16).at[:GLA_RANK].set(w["gla_w_lr"][0])
    gla_b_lr = a["gla_b_lr"]
    gla_gain, q_gain, k_gain = (a[n].reshape(1, MIX_HALF) for n in ("gla_gain", "fox_q_gain", "fox_k_gain"))
    sgu_wm, sgu_bt = _sgu_tables(a["sgu_w_s"][0], a["sgu_b_s"][0])

    sh1, sc1, g1, sh2, sc2, g2 = mods[0]
    _, h1_0 = _res_rms(x0, sc1, sh1, "l0_norm1")
    proj0 = _mm(h1_0, w_even, "nn", "l0_proj")
    og, gla_res = _gla_block_fwd(proj0, w_lr_pad, gla_b_lr, gla_gain, bavg, "gla")
    of, fox_res, (collected1,) = _fox_block_fwd(proj0, a["fox_b_f"][0], q_gain, k_gain, bavg, "fox",
                                                carry=(_gather_prep(local, pieces1, rows1), True))
    w.update(_gather_finish(collected1, pieces1, rows1, "w1"))
    s5w = dict(lam_re=a["s5_lam_re"][0], lam_im=a["s5_lam_im"][0], log_dt=a["s5_log_dt"][0], b_re=a["s5_b_re"][0], b_im=a["s5_b_im"][0],
               c_re=a["s5_c_re"][0], c_im=a["s5_c_im"][0], d=a["s5_d"][0], w_glu=w["s5_w_glu"][0], b_glu=b_glu)
    mixed0 = jnp.concatenate([og, of], axis=1).astype(BF16)
    y0 = _mm(mixed0, w["even_w_out"][0], "nn", "l0_out", out_dtype=BF16)
    x1, h2_0 = _res_rms(x0, sc2, sh2, "l0_norm2", y=y0, g=g1)
    pre0, m0 = _mlp_fwd(h2_0, w["mlp_w1_0"][0], w["mlp_w2_0"][0], "l0_mlp")
    sh1b, sc1b, g1b, sh2b, sc2b, g2b = mods[1]
    x2, h1_1 = _res_rms(x1, sc1b, sh1b, "l1_norm1", y=m0, g=g2)
    proj1 = _mm(h1_1, w["odd_w_in"][0], "nn", "l1_proj")
    ys5, s5_res = _s5_block_fwd(proj1[:, :MIX_HALF], s5w, "s5")
    ysgu = _sgu_fwd(proj1, ln_gain, ln_bias, sgu_wm, sgu_bt, "sgu")
    mixed1 = jnp.concatenate([ys5, ysgu], axis=1).astype(BF16)
    y1 = _mm(mixed1, w["odd_w_out"][0], "nn", "l1_out", out_dtype=BF16)
    x3, h2_1 = _res_rms(x2, sc2b, sh2b, "l1_norm2", y=y1, g=g1b)
    pre1, m1 = _mlp_fwd(h2_1, w["mlp_w1_1"][0], w["mlp_w2_1"][0], "l1_mlp")
    loss_b, dx4, dm1, dg2b = _res_loss(x3, m1, g2b, target, "loss")
    loss = lax.psum(loss_b[0, 0], ("x", "y", "c"))

    full = {}
    dh2_1, dw1_1, dw2_1 = _mlp_bwd(dm1, h2_1, pre1, w["mlp_w1_1"][0], w["mlp_w2_1"][0], "l1_mlp")
    dx3, dy1, dg1b, dsc2b, dsh2b = _res_rms_bwd(x3, dh2_1, sc2b, dx4, "l1_norm2_bwd", y=y1, g=g1b)
    dmixed1 = _mm(dy1, w["odd_w_out"][0], "nt", "l1_out_dx")
    full["odd_w_out"] = _mm(mixed1, dy1, "tn", "l1_out_dw")[None]
    du, s5g = _s5_block_bwd(dmixed1[:, :MIX_HALF], s5w, s5_res, "s5")
    dzu, dzv, dws, dbt, dlg, dlb = _sgu_bwd(dmixed1[:, MIX_HALF:], proj1, ln_gain, ln_bias, sgu_wm, sgu_bt, "sgu_bwd")
    g_ws, g_bs = _sgu_grads(dws, dbt)
    dproj1 = jnp.concatenate([du, dzu, dzv], axis=1).astype(BF16)
    full["odd_w_in"] = _mm(h1_1, dproj1, "tn", "l1_proj_dw")[None]
    dh1_1 = _mm(dproj1, w["odd_w_in"][0], "nt", "l1_proj_dx", out_dtype=BF16)
    dx2, dm0, dg2, dsc1b, dsh1b = _res_rms_bwd(x2, dh1_1, sc1b, dx3, "l1_norm1_bwd", y=m0, g=g2)
    dh2_0, dw1_0, dw2_0 = _mlp_bwd(dm0, h2_0, pre0, w["mlp_w1_0"][0], w["mlp_w2_0"][0], "l0_mlp")
    full.update(mlp_w1_0=dw1_0[None], mlp_w2_0=dw2_0[None], mlp_w1_1=dw1_1[None], mlp_w2_1=dw2_1[None], s5_w_glu=s5g["w_glu"][None],
                s5_b_glu=s5g["b_glu"][None], sgu_ln_gain=dlg, sgu_ln_bias=dlb)
    pair_sums1 = _reduce_prep(full, pieces1, rows1, "g1")
    dx1, dy0, dg1, dsc2, dsh2 = _res_rms_bwd(x1, dh2_0, sc2, dx2, "l0_norm2_bwd", y=y0, g=g1)
    dmixed0 = _mm(dy0, w["even_w_out"][0], "nt", "l0_out_dx")
    full["even_w_out"] = _mm(mixed0, dy0, "tn", "l0_out_dw")[None]
    (dgq, dgk, dgv, dgg, dsmall), glag = _gla_block_bwd((dmixed0, MIX_HALF, 0), proj0, w_lr_pad, gla_gain, bavg, gla_res, "gla")
    (dfq, dfk, dfv, dff), foxg, (arrived1,) = _fox_block_bwd((dmixed0, MIX_HALF, 1), proj0, q_gain, k_gain, bavg, fox_res, "fox",
                                                           carry=(pair_sums1, False))
    dsmall = lax.dynamic_update_slice(dsmall, dff, (0, GLA_RANK))
    dproj0 = jnp.concatenate([dgq, dgk, dgv, dgg, dfq, dfk, dfv, dsmall], axis=1).astype(BF16)
    d_even = _unpack_even(_mm(h1_0, dproj0, "tn", "l0_proj_dw")).reshape(D_MODEL, N_CHIPS, EVEN_SHARD)
    full["even_w_in"] = jnp.pad(d_even, ((0, 0), (0, 0), (0, PACK_COLS - EVEN_SHARD))).reshape(1, D_MODEL, N_CHIPS * PACK_COLS)
    dh1_0 = _mm(dproj0, w_even, "nt", "l0_proj_dx", out_dtype=BF16)
    grad_x, dsc1, dsh1 = _res_rms_bwd(x0, dh1_0, sc1, dx1, "l0_norm1_bwd")
    full["gla_w_lr"] = glag["w_lr"][None]

    dmod = jnp.concatenate([dsh1, dsc1, dg1, dsh2, dsc2, dg2, dsh1b, dsc1b, dg1b, dsh2b, dsc2b, dg2b], axis=1)
    dmod_all = _gather8(jnp.pad(dmod, ((0, 7), (0, 0))), "dmod_gather")[:, :, 0, :].reshape(2 * N_CHIPS, 2, MOD)
    grads = {}
    grads["ada_w"] = jnp.stack([
        _mm(c_all, lax.dynamic_slice_in_dim(dmod_all[:, l], chip * MOD_SHARD, MOD_SHARD, axis=1), "tn", f"ada_dw{l}", a_pro=_silu)
        for l in range(2)])
    grads["ada_b"] = _sum_slots(dmod_all.reshape(2 * N_CHIPS, 2 * MOD // MIX_HALF, MIX_HALF), "ada_db").reshape(2, MOD)

    grads.update(_reduce_finish(arrived1, pieces1, rows1, "g1"))
    grads.update(_reduce_finish(_chip_exchange(_reduce_prep(full, pieces0, rows0, "g0"), "g0_chips", False), pieces0, rows0, "g0"))
    grads["even_w_in"] = grads["even_w_in"][:, :, :EVEN_SHARD]
    grads["mlp_w1"] = jnp.concatenate([grads.pop("mlp_w1_0"), grads.pop("mlp_w1_1")])
    grads["mlp_w2"] = jnp.concatenate([grads.pop("mlp_w2_0"), grads.pop("mlp_w2_1")])

    part = dict(gla_b_lr=glag["b_lr"], gla_gain=glag["gain"], fox_b_f=foxg["b_f"], fox_q_gain=foxg["q_gain"], fox_k_gain=foxg["k_gain"],
                s5_lam_re=s5g["lam_re"], s5_lam_im=s5g["lam_im"], s5_log_dt=s5g["log_dt"], s5_b_re=s5g["b_re"], s5_b_im=s5g["b_im"],
                s5_c_re=s5g["c_re"], s5_c_im=s5g["c_im"], s5_d=s5g["d"], sgu_w_s=g_ws, sgu_b_s=g_bs)
    parts_all = _gather8(_pack_rows([part[n] for n, _ in REPLICATED], SMALL_ROWS).astype(BF16), "rep_gather")
    rep = _sum_slots(parts_all.reshape(2 * N_CHIPS, SMALL_ROWS, PACK_COLS), "rep_sum")
    grads.update(_unpack(rep, REPLICATED))

    delta, new_m, new_v = {}, {}, {}
    for n, shape2 in BIG_ADAM.items():
        d, nm, nv = _adamw(a[n].reshape(shape2), grads[n].reshape(shape2), a["m_" + n].reshape(shape2), a["v_" + n].reshape(shape2), "adamw_" + n)
        delta[n], new_m[n], new_v[n] = (t.reshape(a[n].shape) for t in (d, nm, nv))
    small = [n for n in WEIGHTS if n not in BIG_ADAM]
    spec = [(n, a[n].shape) for n in small]
    packs = [_pack_rows([src[n] for n in small], SMALL_ROWS) for src in
             (a, grads, {n: a["m_" + n] for n in small}, {n: a["v_" + n] for n in small})]
    for tgt, res in zip((delta, new_m, new_v), _adamw(*packs, "adamw_small")):
        tgt.update(_unpack(res, spec))
    outs = [loss, grad_x[None]]
    for group in (grads, delta, new_m, new_v):
        outs += [group[n].reshape(a[n].shape) for n in WEIGHTS]
    return tuple(outs)


def kernel(x, c, ada_w, ada_b, even_w_in, even_w_out, gla_w_lr, gla_b_lr, gla_gain, fox_b_f, fox_q_gain, fox_k_gain, odd_w_in,
           odd_w_out, s5_lam_re, s5_lam_im, s5_log_dt, s5_b_re, s5_b_im, s5_c_re, s5_c_im, s5_d, s5_w_glu, s5_b_glu, sgu_ln_gain,
           sgu_ln_bias, sgu_w_s, sgu_b_s, mlp_w1, mlp_w2, loss_target, m_ada_w, m_ada_b, m_even_w_in, m_even_w_out, m_gla_w_lr,
           m_gla_b_lr, m_gla_gain, m_fox_b_f, m_fox_q_gain, m_fox_k_gain, m_odd_w_in, m_odd_w_out, m_s5_lam_re, m_s5_lam_im,
           m_s5_log_dt, m_s5_b_re, m_s5_b_im, m_s5_c_re, m_s5_c_im, m_s5_d, m_s5_w_glu, m_s5_b_glu, m_sgu_ln_gain, m_sgu_ln_bias,
           m_sgu_w_s, m_sgu_b_s, m_mlp_w1, m_mlp_w2, v_ada_w, v_ada_b, v_even_w_in, v_even_w_out, v_gla_w_lr, v_gla_b_lr,
           v_gla_gain, v_fox_b_f, v_fox_q_gain, v_fox_k_gain, v_odd_w_in, v_odd_w_out, v_s5_lam_re, v_s5_lam_im, v_s5_log_dt,
           v_s5_b_re, v_s5_b_im, v_s5_c_re, v_s5_c_im, v_s5_d, v_s5_w_glu, v_s5_b_glu, v_sgu_ln_gain, v_sgu_ln_bias, v_sgu_w_s,
           v_sgu_b_s, v_mlp_w1, v_mlp_w2):
    return _step((x, c, ada_w, ada_b, even_w_in, even_w_out, gla_w_lr, gla_b_lr, gla_gain, fox_b_f, fox_q_gain, fox_k_gain,
                  odd_w_in, odd_w_out, s5_lam_re, s5_lam_im, s5_log_dt, s5_b_re, s5_b_im, s5_c_re, s5_c_im, s5_d, s5_w_glu,
                  s5_b_glu, sgu_ln_gain, sgu_ln_bias, sgu_w_s, sgu_b_s, mlp_w1, mlp_w2, loss_target, m_ada_w, m_ada_b,
                  m_even_w_in, m_even_w_out, m_gla_w_lr, m_gla_b_lr, m_gla_gain, m_fox_b_f, m_fox_q_gain, m_fox_k_gain,
                  m_odd_w_in, m_odd_w_out, m_s5_lam_re, m_s5_lam_im, m_s5_log_dt, m_s5_b_re, m_s5_b_im, m_s5_c_re, m_s5_c_im,
                  m_s5_d, m_s5_w_glu, m_s5_b_glu, m_sgu_ln_gain, m_sgu_ln_bias, m_sgu_w_s, m_sgu_b_s, m_mlp_w1, m_mlp_w2, v_ada_w,
                  v_ada_b, v_even_w_in, v_even_w_out, v_gla_w_lr, v_gla_b_lr, v_gla_gain, v_fox_b_f, v_fox_q_gain, v_fox_k_gain,
                  v_odd_w_in, v_odd_w_out, v_s5_lam_re, v_s5_lam_im, v_s5_log_dt, v_s5_b_re, v_s5_b_im, v_s5_c_re, v_s5_c_im,
                  v_s5_d, v_s5_w_glu, v_s5_b_glu, v_sgu_ln_gain, v_sgu_ln_bias, v_sgu_w_s, v_sgu_b_s, v_mlp_w1, v_mlp_w2))
```

```python
import functools
import math

import jax
import jax.numpy as jnp
import numpy as np
from jax import lax
from jax.experimental import pallas as pl
from jax.experimental.pallas import tpu as pltpu

F32 = jnp.float32
BF16 = jnp.bfloat16
MESH = pl.DeviceIdType.MESH
ANY = pl.BlockSpec(memory_space=pl.ANY)
DMA_SEM = pltpu.SemaphoreType.DMA

D_MODEL = 1024
HEAD_DIM = 64
MIX_HALF = 512
GLA_RANK = 16
GLA_TAU = 16.0
GLA_CHUNK = 64
S5_GROUPS = 32
S5_GROUP_WIDTH = 16
S5_STATE = 64
S5_N = S5_GROUPS * S5_STATE
SGU_GROUPS = 8
SGU_CHUNK = 128
D_FF = 4096
EPS = 1e-6
N_CHIPS = 4
LANES = 128
VMEM_LIMIT = 48 * 1024 * 1024
PAIR_COPIES = 16

ADAM_LR = 0.001
ADAM_B1 = 0.9
ADAM_B2 = 0.999
ADAM_EPS = 1e-08
ADAM_WD = 0.01
ADAM_STEP = 10


def _cparams(*sem):
    return pltpu.CompilerParams(dimension_semantics=sem, vmem_limit_bytes=VMEM_LIMIT)


def _pair_swap(x, name):
    lead = x.shape[:-2]
    rows = x.shape[-2]
    nsplit = max(1, PAIR_COPIES // max(1, math.prod(lead)))
    while nsplit > 1 and rows % (nsplit * 16):
        nsplit -= 1
    pieces = [idx + (pl.ds(j * (rows // nsplit), rows // nsplit),) for idx in np.ndindex(*lead) for j in range(nsplit)]

    def body(x_ref, o_ref, send_sems, recv_sems):
        mx, my, mc = lax.axis_index("x"), lax.axis_index("y"), lax.axis_index("c")
        copies = [pltpu.make_async_remote_copy(src_ref=x_ref.at[p], dst_ref=o_ref.at[p], send_sem=send_sems.at[j], recv_sem=recv_sems.at[j],
                                               device_id=(mx, my, 1 - mc), device_id_type=MESH) for j, p in enumerate(pieces)]
        for cp in copies:
            cp.start()
        for cp in copies:
            cp.wait_recv()
        for cp in copies:
            cp.wait_send()

    return pl.pallas_call(
        body, name=name, out_shape=jax.ShapeDtypeStruct(x.shape, x.dtype), in_specs=[ANY], out_specs=ANY,
        scratch_shapes=[DMA_SEM((len(pieces),)), DMA_SEM((len(pieces),))])(x)


def _by_core(mine, theirs):
    first = lax.axis_index("c") == 0
    return jnp.stack([jnp.where(first, mine, theirs), jnp.where(first, theirs, mine)])


def _chip_exchange(x, name, bcast):
    blk = x.shape if bcast else x.shape[1:]

    def body(x_ref, o_ref, send_sems, recv_sems, loc_sem):
        start, finish = _exchange_plan(x_ref, o_ref, send_sems, recv_sems, loc_sem, bcast)
        start()
        finish()

    return pl.pallas_call(
        body, name=name, out_shape=jax.ShapeDtypeStruct((N_CHIPS,) + tuple(blk), x.dtype), in_specs=[ANY], out_specs=ANY,
        scratch_shapes=_EXCHANGE_SEMS)(x)


_EXCHANGE_SEMS = [DMA_SEM((3,)), DMA_SEM((3,)), DMA_SEM]


def _exchange_plan(x_ref, o_ref, send_sems, recv_sems, loc_sem, bcast):
    mx, my, mc = lax.axis_index("x"), lax.axis_index("y"), lax.axis_index("c")
    me = 2 * mx + my
    peers = [(1 - mx, my), (mx, 1 - my), (1 - mx, 1 - my)]

    def src(k):
        return x_ref if bcast else x_ref.at[k]

    def remote(j, source, slot):
        px, py = peers[j]
        return pltpu.make_async_remote_copy(src_ref=source, dst_ref=o_ref.at[slot], send_sem=send_sems.at[j], recv_sem=recv_sems.at[j],
                                            device_id=(px, py, mc), device_id_type=MESH)

    loc = pltpu.make_async_copy(src(me), o_ref.at[me], loc_sem)
    sends = [remote(j, src(2 * px + py), me) for j, (px, py) in enumerate(peers)]
    arrivals = [remote(j, src(me), 2 * px + py) for j, (px, py) in enumerate(peers)]

    def start():
        loc.start()
        for cp in sends:
            cp.start()

    def finish():
        for cp in arrivals:
            cp.wait_recv()
        for cp in sends:
            cp.wait_send()
        loc.wait()

    return start, finish


def _gather8(x, name):
    collected = _chip_exchange(x, name + "_chips", True)
    return jnp.swapaxes(_by_core(collected, _pair_swap(collected, name + "_pair")), 0, 1)


def _tile(n, want):
    if n <= want:
        return n
    t = (want // LANES) * LANES
    while t >= LANES:
        if n % t == 0:
            return t
        t -= LANES
    raise ValueError(f"no lane-aligned tile for {n}")


_DIMS = {"nn": (((1,), (0,)), ((), ())), "nt": (((1,), (1,)), ((), ())), "tn": (((0,), (0,)), ((), ()))}


MM_FULL_K = 4096
MM_SLAB_K = 2048
MM_TILES = ((1024, 1024), (512, 1024), (1024, 512), (512, 512), (256, 512), (256, 256))
MM_VMEM_BUDGET = 36 * 1024 * 1024


def _mm(a, b, mode, name, *, a_pro=None, epi=None, extras=(), out_dtype=F32, tm_max=1024, tn_max=1024, tk=None, a_cols=None):
    c0, csize = a_cols if a_cols is not None else (0, a.shape[1])
    if mode == "tn":
        K, M = a.shape[0], csize
    else:
        M, K = a.shape[0], csize
    N = b.shape[0] if mode == "nt" else b.shape[1]
    assert (b.shape[1] if mode == "nt" else b.shape[0]) == K, (a.shape, b.shape, mode)
    if tk is None:
        tk = K if (mode != "tn" and K <= MM_FULL_K) else MM_SLAB_K
    tk = _tile(K, tk)
    nk = K // tk
    n_mn = sum(1 for _, kind in extras if kind == "mn")
    for tm_want, tn_want in MM_TILES:
        tm, tn = _tile(M, min(tm_want, tm_max)), _tile(N, min(tn_want, tn_max))
        need = 2 * (tm * tk * a.dtype.itemsize + tk * tn * b.dtype.itemsize + tm * tn * 4 * (1 + n_mn)) + tm * tn * 4 * (nk > 1)
        if need <= MM_VMEM_BUDGET:
            break
    if mode == "tn":
        assert c0 % tm == 0
        a_spec = pl.BlockSpec((tk, tm), lambda i, j, k: (k, i + c0 // tm))
    else:
        assert c0 % tk == 0
        a_spec = pl.BlockSpec((tm, tk), lambda i, j, k: (i, k + c0 // tk))
    b_spec = pl.BlockSpec((tn, tk), lambda i, j, k: (j, k)) if mode == "nt" else pl.BlockSpec((tk, tn), lambda i, j, k: (k, j))
    ex_specs = []
    for arr, kind in extras:
        if kind == "mn":
            assert arr.shape == (M, N)
            ex_specs.append(pl.BlockSpec((tm, tn), lambda i, j, k: (i, j)))
        else:
            assert arr.shape == (1, N)
            ex_specs.append(pl.BlockSpec((1, tn), lambda i, j, k: (0, j)))
    n_ex = len(extras)

    def body(*refs):
        a_ref, b_ref = refs[:2]
        ex_refs = refs[2:2 + n_ex]
        o_ref = refs[2 + n_ex]
        acc_ref = refs[3 + n_ex] if nk > 1 else None
        k = pl.program_id(2)
        av = a_ref[...]
        if a_pro is not None:
            av = a_pro(av)
        part = lax.dot_general(av.astype(BF16), b_ref[...].astype(BF16), _DIMS[mode], preferred_element_type=F32)
        if nk == 1:
            if epi is not None:
                part = epi(part, *[r[...] for r in ex_refs])
            o_ref[...] = part.astype(o_ref.dtype)
            return

        @pl.when(k == 0)
        def _():
            acc_ref[...] = part

        @pl.when(k > 0)
        def _():
            acc_ref[...] += part

        @pl.when(k == nk - 1)
        def _():
            acc = acc_ref[...]
            if epi is not None:
                acc = epi(acc, *[r[...] for r in ex_refs])
            o_ref[...] = acc.astype(o_ref.dtype)

    return pl.pallas_call(
        body, name=name, grid=(M // tm, N // tn, nk),
        in_specs=[a_spec, b_spec] + ex_specs,
        out_specs=pl.BlockSpec((tm, tn), lambda i, j, k: (i, j)),
        out_shape=jax.ShapeDtypeStruct((M, N), out_dtype),
        scratch_shapes=[pltpu.VMEM((tm, tn), F32)] if nk > 1 else [],
        compiler_params=_cparams("parallel", "parallel", "arbitrary"))(a, b, *[e[0] for e in extras])


ROWS = 512


def _row_spec(w, ts=ROWS):
    return pl.BlockSpec((ts, w), lambda i: (i, 0))


def _vec_spec(w):
    return pl.BlockSpec((1, w), lambda i: (0, 0))


def _res_rms(x, sc, sh, name, y=None, g=None):
    S, D = x.shape
    has_res = y is not None

    def body(*refs):
        if has_res:
            x_ref, y_ref, g_ref, sc_ref, sh_ref, xo_ref, h_ref = refs
            xv = x_ref[...] + g_ref[...] * y_ref[...]
            xo_ref[...] = xv
        else:
            x_ref, sc_ref, sh_ref, h_ref = refs
            xv = x_ref[...]
        r = lax.rsqrt(jnp.mean(xv * xv, axis=-1, keepdims=True) + EPS)
        h_ref[...] = (xv * r * (1.0 + sc_ref[...]) + sh_ref[...]).astype(BF16)

    row, vec = _row_spec(D), _vec_spec(D)
    if has_res:
        return pl.pallas_call(body, name=name, grid=(S // ROWS,), in_specs=[row, row, vec, vec, vec], out_specs=[row, row],
                              out_shape=[jax.ShapeDtypeStruct((S, D), F32), jax.ShapeDtypeStruct((S, D), BF16)],
                              compiler_params=_cparams("parallel"))(x, y, g, sc, sh)
    h = pl.pallas_call(body, name=name, grid=(S // ROWS,), in_specs=[row, vec, vec], out_specs=row,
                       out_shape=jax.ShapeDtypeStruct((S, D), BF16), compiler_params=_cparams("parallel"))(x, sc, sh)
    return x, h


def _res_rms_bwd(x, dh, sc, dres, name, y=None, g=None):
    S, D = x.shape
    has_res = y is not None

    def body(*refs):
        if has_res:
            x_ref, dh_ref, sc_ref, dres_ref, y_ref, g_ref, dx_ref, dy_ref, dg_ref, dsc_ref, dsh_ref = refs
        else:
            x_ref, dh_ref, sc_ref, dres_ref, dx_ref, dsc_ref, dsh_ref = refs
        first = pl.program_id(0) == 0
        xv = x_ref[...]
        dh = dh_ref[...].astype(F32)
        r = lax.rsqrt(jnp.mean(xv * xv, axis=-1, keepdims=True) + EPS)
        xn = xv * r
        dxn = dh * (1.0 + sc_ref[...])
        dx = dres_ref[...] + r * (dxn - xn * jnp.mean(dxn * xn, axis=-1, keepdims=True))
        dx_ref[...] = dx
        parts = [(dsc_ref, jnp.sum(dh * xn, axis=0, keepdims=True)), (dsh_ref, jnp.sum(dh, axis=0, keepdims=True))]
        if has_res:
            dy_ref[...] = (dx * g_ref[...]).astype(BF16)
            parts.append((dg_ref, jnp.sum(dx * y_ref[...], axis=0, keepdims=True)))
        for ref, val in parts:
            @pl.when(first)
            def _(ref=ref, val=val):
                ref[...] = val

            @pl.when(jnp.logical_not(first))
            def _(ref=ref, val=val):
                ref[...] += val

    row, vec = _row_spec(D), _vec_spec(D)
    full = jax.ShapeDtypeStruct((S, D), F32)
    v = jax.ShapeDtypeStruct((1, D), F32)
    if has_res:
        return pl.pallas_call(body, name=name, grid=(S // ROWS,), in_specs=[row, row, vec, row, row, vec],
                              out_specs=[row, row, vec, vec, vec], out_shape=[full, jax.ShapeDtypeStruct((S, D), BF16), v, v, v],
                              compiler_params=_cparams("arbitrary"))(x, dh, sc, dres, y, g)
    return pl.pallas_call(body, name=name, grid=(S // ROWS,), in_specs=[row, row, vec, row],
                          out_specs=[row, vec, vec], out_shape=[full, v, v],
                          compiler_params=_cparams("arbitrary"))(x, dh, sc, dres)


def _res_loss(x, m, g, target, name):
    S, D = x.shape

    def body(x_ref, m_ref, g_ref, t_ref, loss_ref, dx_ref, dm_ref, dg_ref):
        first = pl.program_id(0) == 0
        mv = m_ref[...].astype(F32)
        err = x_ref[...] + g_ref[...] * mv - t_ref[...]
        dx = err * (1.0 / D)
        dx_ref[...] = dx
        dm_ref[...] = (dx * g_ref[...]).astype(BF16)
        part = 0.5 * jnp.sum(jnp.mean(err * err, axis=-1, keepdims=True), axis=0, keepdims=True)
        dg = jnp.sum(dx * mv, axis=0, keepdims=True)

        @pl.when(first)
        def _():
            loss_ref[...] = jnp.broadcast_to(part, loss_ref.shape)
            dg_ref[...] = dg

        @pl.when(jnp.logical_not(first))
        def _():
            loss_ref[...] += jnp.broadcast_to(part, loss_ref.shape)
            dg_ref[...] += dg

    row, vec = _row_spec(D), _vec_spec(D)
    full = jax.ShapeDtypeStruct((S, D), F32)
    return pl.pallas_call(body, name=name, grid=(S // ROWS,), in_specs=[row, row, vec, row],
                          out_specs=[pl.BlockSpec((8, LANES), lambda i: (0, 0)), row, row, vec],
                          out_shape=[jax.ShapeDtypeStruct((8, LANES), F32), full, jax.ShapeDtypeStruct((S, D), BF16), jax.ShapeDtypeStruct((1, D), F32)],
                          compiler_params=_cparams("arbitrary"))(x, m, g, target)


def _adamw(w, g, m, v, name):
    R, C = w.shape
    tr = R if R <= 256 else 256
    assert R % tr == 0

    def body(w_ref, g_ref, m_ref, v_ref, d_ref, nm_ref, nv_ref):
        gv = g_ref[...]
        nm = ADAM_B1 * m_ref[...] + (1.0 - ADAM_B1) * gv
        nv = ADAM_B2 * v_ref[...] + (1.0 - ADAM_B2) * jnp.square(gv)
        m_hat = nm / (1.0 - ADAM_B1 ** ADAM_STEP)
        v_hat = nv / (1.0 - ADAM_B2 ** ADAM_STEP)
        d_ref[...] = -ADAM_LR * (m_hat / (jnp.sqrt(v_hat) + ADAM_EPS) + ADAM_WD * w_ref[...])
        nm_ref[...] = nm
        nv_ref[...] = nv

    spec = pl.BlockSpec((tr, C), lambda i: (i, 0))
    out = jax.ShapeDtypeStruct((R, C), F32)
    return pl.pallas_call(body, name=name, grid=(R // tr,), in_specs=[spec] * 4, out_specs=[spec] * 3,
                          out_shape=[out, out, out], compiler_params=_cparams("parallel"))(w, g, m, v)


def _sum_slots(x, name):
    n, R, C = x.shape
    tr = R if R <= 256 else 256
    assert R % tr == 0

    def body(x_ref, o_ref):
        acc = x_ref[0].astype(F32)
        for j in range(1, n):
            acc = acc + x_ref[j].astype(F32)
        o_ref[...] = acc

    return pl.pallas_call(body, name=name, grid=(R // tr,), in_specs=[pl.BlockSpec((n, tr, C), lambda i: (0, i, 0))],
                          out_specs=pl.BlockSpec((tr, C), lambda i: (i, 0)), out_shape=jax.ShapeDtypeStruct((R, C), F32),
                          compiler_params=_cparams("parallel"))(x)


EW_ROWS = 1024


def _ew(fn, name, tiled, consts=(), outs=(), sums=(), ts=EW_ROWS):
    tiled = [t if isinstance(t, tuple) else (t, t.shape[1], 0) for t in tiled]
    S = tiled[0][0].shape[0]
    ts = min(ts, S)
    assert S % ts == 0, (name, S, ts)
    n_t, n_c, n_o, n_s = len(tiled), len(consts), len(outs), len(sums)

    def body(*refs):
        ins = [r[...] for r in refs[:n_t + n_c]]
        res = fn(*ins)
        res = res if isinstance(res, (tuple, list)) else (res,)
        assert len(res) == n_o + n_s
        o_refs = refs[n_t + n_c:]
        for r, val in zip(o_refs[:n_o], res[:n_o]):
            r[...] = val.astype(r.dtype)
        first = pl.program_id(0) == 0
        for r, val in zip(o_refs[n_o:], res[n_o:]):
            @pl.when(first)
            def _(r=r, val=val):
                r[...] = val

            @pl.when(jnp.logical_not(first))
            def _(r=r, val=val):
                r[...] += val

    in_specs = [pl.BlockSpec((ts, w), lambda i, cb=cb: (i, cb)) for _, w, cb in tiled]
    in_specs += [pl.BlockSpec(c.shape, lambda i, nd=c.ndim: (0,) * nd) for c in consts]
    out_specs = [_row_spec(w, ts) for w, _ in outs] + [_vec_spec(w) for w in sums]
    out_shape = [jax.ShapeDtypeStruct((S, w), dt) for w, dt in outs] + [jax.ShapeDtypeStruct((1, w), F32) for w in sums]
    res = pl.pallas_call(body, name=name, grid=(S // ts,), in_specs=in_specs, out_specs=out_specs, out_shape=out_shape,
                         compiler_params=_cparams("arbitrary" if sums else "parallel"))(*[t[0] for t in tiled], *consts)
    return res


_GELU_C = math.sqrt(2.0 / math.pi)


def _gelu(x):
    return 0.5 * x * (1.0 + jnp.tanh(_GELU_C * (x + 0.044715 * x * x * x)))


def _dgelu(x):
    t = jnp.tanh(_GELU_C * (x + 0.044715 * x * x * x))
    return 0.5 * (1.0 + t) + 0.5 * x * (1.0 - t * t) * _GELU_C * (1.0 + 3.0 * 0.044715 * x * x)


def _sigmoid(x):
    return 1.0 / (1.0 + jnp.exp(-x))


def _log_sigmoid(x):
    return jnp.minimum(x, 0.0) - jnp.log(1.0 + jnp.exp(-jnp.abs(x)))


SCAN_T = 128
SCAN_TB = 1024


def _cmul(ar, ai, br, bi):
    return ar * br - ai * bi, ar * bi + ai * br


def _s5_discretise(lam_re, lam_im, log_dt, b_re, b_im):
    dt = jnp.exp(log_dt)[:, None]
    mag = jnp.exp(lam_re * dt)
    ang = lam_im * dt
    abar_re = mag * jnp.cos(ang)
    abar_im = mag * jnp.sin(ang)
    den = lam_re * lam_re + lam_im * lam_im
    coef_re = ((abar_re - 1.0) * lam_re + abar_im * lam_im) / den
    coef_im = (abar_im * lam_re - (abar_re - 1.0) * lam_im) / den
    bbar_re = coef_re[..., None] * b_re - coef_im[..., None] * b_im
    bbar_im = coef_re[..., None] * b_im + coef_im[..., None] * b_re
    return abar_re, abar_im, bbar_re, bbar_im


def _planes(re, im):
    lead = re.shape[:-1]
    return jnp.stack([re.reshape(lead + (-1, LANES)), im.reshape(lead + (-1, LANES))], axis=-2).reshape(lead + (-1,))


def _unplanes(x):
    lead = x.shape[:-1]
    x4 = x.reshape(lead + (-1, 2, LANES))
    return x4[..., 0, :].reshape(lead + (-1,)), x4[..., 1, :].reshape(lead + (-1,))


def _s5_scan_tables(a_re, a_im, reverse):
    pr, pi = [a_re], [a_im]
    for _ in range(7):
        r, i = _cmul(pr[-1], pi[-1], pr[-1], pi[-1])
        pr.append(r)
        pi.append(i)
    apow = _planes(jnp.stack(pr), jnp.stack(pi))
    n = np.arange(1, SCAN_T + 1)
    if reverse:
        n = n[::-1]
    tr = jnp.ones((SCAN_T, a_re.shape[0]), F32)
    ti = jnp.zeros((SCAN_T, a_re.shape[0]), F32)
    for k in range(8):
        bit = jnp.asarray(((n >> k) & 1).astype(np.float32))[:, None]
        mr = bit * pr[k][None, :] + (1.0 - bit)
        mi = bit * pi[k][None, :]
        tr, ti = _cmul(tr, ti, mr, mi)
    return apow, _planes(tr, ti)


def _s5_scan(bu, apow, ptab, name, reverse, x_fwd=None):
    S, N2 = bu.shape
    T, W = SCAN_T, 2 * LANES
    tb = min(SCAN_TB, S)
    nt, nsub = S // tb, tb // T
    order = list(range(nsub - 1, -1, -1) if reverse else range(nsub))
    with_da = x_fwd is not None

    def tblk(t):
        return (nt - 1 - t) if reverse else t

    def shifted(v, k, rowi):
        s = 1 << k
        if reverse:
            return jnp.where(rowi < T - s, pltpu.roll(v, T - s, 0), 0.0)
        return jnp.where(rowi >= s, pltpu.roll(v, s, 0), 0.0)

    def body(*refs):
        if with_da:
            bu_ref, ap_ref, pt_ref, xf_ref, xp_ref, x_ref, da_ref, carry_ref = refs
        else:
            bu_ref, ap_ref, pt_ref, x_ref, carry_ref = refs
        t = pl.program_id(1)

        @pl.when(t == 0)
        def _():
            carry_ref[...] = jnp.zeros_like(carry_ref)
            if with_da:
                da_ref[...] = jnp.zeros_like(da_ref)

        rowi = lax.broadcasted_iota(jnp.int32, (T, LANES), 0)
        pr, pi = pt_ref[:, :LANES], pt_ref[:, LANES:]
        cr, ci = carry_ref[0:1, :LANES], carry_ref[0:1, LANES:]
        for sb in order:
            rows = pl.ds(sb * T, T)
            xr, xi = bu_ref[rows, :LANES], bu_ref[rows, LANES:]
            for k in range(7):
                ar, ai = ap_ref[k:k + 1, :LANES], ap_ref[k:k + 1, LANES:]
                s = 1 << k
                if s < 8:
                    rr, ri = shifted(xr, k, rowi), shifted(xi, k, rowi)
                    xr, xi = xr + ar * rr - ai * ri, xi + ar * ri + ai * rr
                elif reverse:
                    nr, ni = xr[s:], xi[s:]
                    xr = jnp.concatenate([xr[:T - s] + ar * nr - ai * ni, xr[T - s:]], axis=0)
                    xi = jnp.concatenate([xi[:T - s] + ar * ni + ai * nr, xi[T - s:]], axis=0)
                else:
                    nr, ni = xr[:T - s], xi[:T - s]
                    xr = jnp.concatenate([xr[:s], xr[s:] + ar * nr - ai * ni], axis=0)
                    xi = jnp.concatenate([xi[:s], xi[s:] + ar * ni + ai * nr], axis=0)
            xr, xi = xr + pr * cr - pi * ci, xi + pr * ci + pi * cr
            x_ref[rows, :LANES] = xr
            x_ref[rows, LANES:] = xi
            edge = pl.ds(sb * T + (0 if reverse else T - 1), 1)
            cr, ci = x_ref[edge, :LANES], x_ref[edge, LANES:]
            if with_da:
                if sb > 0:
                    before = pl.ds(sb * T - 1, 1)
                    b_r, b_i = xf_ref[before, :LANES], xf_ref[before, LANES:]
                else:
                    keep = (tblk(t) > 0).astype(F32)
                    b_r, b_i = xp_ref[7:8, :LANES] * keep, xp_ref[7:8, LANES:] * keep
                fr, fi = xf_ref[rows, :LANES], xf_ref[rows, LANES:]
                qr = jnp.where(rowi >= 1, pltpu.roll(fr, 1, 0), b_r)
                qi = jnp.where(rowi >= 1, pltpu.roll(fi, 1, 0), b_i)
                gr, gi = xr * qr + xi * qi, xi * qr - xr * qi
                sr, si = gr[0:8], gi[0:8]
                for j in range(1, T // 8):
                    sr, si = sr + gr[8 * j:8 * j + 8], si + gi[8 * j:8 * j + 8]
                da_ref[:, :LANES] += sr
                da_ref[:, LANES:] += si
        carry_ref[0:1, :LANES] = cr
        carry_ref[0:1, LANES:] = ci

    blk = pl.BlockSpec((tb, W), lambda j, t: (tblk(t), j))
    in_specs = [blk, pl.BlockSpec((8, W), lambda j, t: (0, j)), pl.BlockSpec((T, W), lambda j, t: (0, j))]
    out_specs, out_shape = [blk], [jax.ShapeDtypeStruct((S, N2), F32)]
    args = [bu, apow, ptab]
    if with_da:
        in_specs += [blk, pl.BlockSpec((8, W), lambda j, t: (jnp.maximum(tblk(t) * (tb // 8) - 1, 0), j))]
        out_specs.append(pl.BlockSpec((8, W), lambda j, t: (0, j)))
        out_shape.append(jax.ShapeDtypeStruct((8, N2), F32))
        args += [x_fwd, x_fwd]
    res = pl.pallas_call(body, name=name, grid=(N2 // W, nt), in_specs=in_specs, out_specs=out_specs, out_shape=out_shape,
                         scratch_shapes=[pltpu.VMEM((8, W), F32)], compiler_params=_cparams("parallel", "arbitrary"))(*args)
    return res if with_da else res[0]


S5_BAND = 4


def _mm_band(a, b, name, *, b_t=False, outer=False, epi=None, extras=(), tm=512, tk=2048):
    S = a.shape[0]
    wa = a.shape[1] // S5_BAND
    if outer:
        wb = b.shape[1] // S5_BAND
        tk = _tile(S, tk)
        nk = S // tk

        def obody(a_ref, b_ref, o_ref, acc_ref):
            k = pl.program_id(1)
            part = lax.dot_general(a_ref[...].astype(BF16), b_ref[...].astype(BF16), TN_DIMS, preferred_element_type=F32)

            @pl.when(k == 0)
            def _():
                acc_ref[...] = part

            @pl.when(k > 0)
            def _():
                acc_ref[...] += part

            @pl.when(k == nk - 1)
            def _():
                o_ref[...] = acc_ref[...]

        return pl.pallas_call(
            obody, name=name, grid=(S5_BAND, nk),
            in_specs=[pl.BlockSpec((tk, wa), lambda c, k: (k, c)), pl.BlockSpec((tk, wb), lambda c, k: (k, c))],
            out_specs=pl.BlockSpec((wa, wb), lambda c, k: (c, 0)), out_shape=jax.ShapeDtypeStruct((a.shape[1], wb), F32),
            scratch_shapes=[pltpu.VMEM((wa, wb), F32)], compiler_params=_cparams("parallel", "arbitrary"))(a, b)

    wo = (b.shape[0] if b_t else b.shape[1]) // S5_BAND
    tm = _tile(S, tm)
    ex_specs = [pl.BlockSpec((tm, wo), lambda i, c: (i, c)) if kind == "mn" else pl.BlockSpec((1, wo), lambda i, c: (0, c))
                for _, kind in extras]

    def body(a_ref, b_ref, *refs):
        part = lax.dot_general(a_ref[...].astype(BF16), b_ref[...].astype(BF16), NT_DIMS if b_t else _DIMS["nn"], preferred_element_type=F32)
        if epi is not None:
            part = epi(part, *[r[...] for r in refs[:-1]])
        refs[-1][...] = part

    b_spec = pl.BlockSpec((wo, wa) if b_t else (wa, wo), lambda i, c: (c, c))
    return pl.pallas_call(
        body, name=name, grid=(S // tm, S5_BAND), in_specs=[pl.BlockSpec((tm, wa), lambda i, c: (i, c)), b_spec] + ex_specs,
        out_specs=pl.BlockSpec((tm, wo), lambda i, c: (i, c)), out_shape=jax.ShapeDtypeStruct((S, S5_BAND * wo), F32),
        compiler_params=_cparams("parallel", "parallel"))(a, b, *[e[0] for e in extras])


def _band_to_full(blocks, cols):
    wa, wb = blocks.shape[0] // S5_BAND, blocks.shape[1]
    return jnp.concatenate([jnp.pad(blocks[k * wa:(k + 1) * wa], ((0, 0), (k * wb, cols - (k + 1) * wb))) for k in range(S5_BAND)], axis=0)


def _block_diag(t):
    G, a, b = t.shape
    return (t[:, :, None, :] * jnp.eye(G, dtype=t.dtype)[:, None, :, None]).reshape(G * a, G * b)


def _block_diag_take(m, G):
    a, b = m.shape[0] // G, m.shape[1] // G
    m4 = m.reshape(G, a, G, b)
    return jnp.sum(m4 * jnp.eye(G, dtype=m.dtype)[:, None, :, None], axis=2)


def _s5_block_fwd(u, w, pfx):
    a_re, a_im, bb_re, bb_im = _s5_discretise(w["lam_re"], w["lam_im"], w["log_dt"], w["b_re"], w["b_im"])
    bcat = _planes(_block_diag(bb_re).T, _block_diag(bb_im).T).astype(BF16)
    ccat = _planes(_block_diag(jnp.swapaxes(w["c_re"], 1, 2)).T, -_block_diag(jnp.swapaxes(w["c_im"], 1, 2)).T).T.astype(BF16)
    af_re, af_im = a_re.reshape(-1), a_im.reshape(-1)
    apow, ptab = _s5_scan_tables(af_re, af_im, False)
    bu = _mm_band(u, bcat, pfx + "_bu")
    x = _s5_scan(bu, apow, ptab, pfx + "_scan", False)
    d_row = w["d"].reshape(1, MIX_HALF)
    ys = _mm_band(x, ccat, pfx + "_y", epi=lambda acc, ut, dr: acc + dr * ut, extras=[(u, "mn"), (d_row, "n")])
    z = _mm(ys, w["w_glu"], "nn", pfx + "_glu", a_pro=_gelu, epi=lambda acc, b: acc + b, extras=[(w["b_glu"].reshape(1, -1), "n")])
    y2, = _ew(lambda ysv, zv: _gelu(ysv) * _sigmoid(zv), pfx + "_gate", [ys, z], outs=[(MIX_HALF, F32)])
    return y2, dict(u=u, x=x, ys=ys, z=z, bcat=bcat, ccat=ccat, a=(af_re, af_im), d_row=d_row)


def _s5_block_bwd(dy2, w, res, pfx):
    u, x, ys, z, bcat, ccat = res["u"], res["x"], res["ys"], res["z"], res["bcat"], res["ccat"]

    def gate_bwd(dy, ysv, zv):
        sg = _sigmoid(zv)
        dz = dy * _gelu(ysv) * sg * (1.0 - sg)
        return dz, jnp.sum(dz, axis=0, keepdims=True)

    dz, db_glu = _ew(gate_bwd, pfx + "_gate_bwd", [dy2, ys, z], outs=[(MIX_HALF, F32)], sums=[MIX_HALF])
    dw_glu = _mm(ys, dz, "tn", pfx + "_dwglu", a_pro=_gelu)
    dys = _mm(dz, w["w_glu"], "nt", pfx + "_dys", epi=lambda acc, dy, zv, ysv: (acc + dy * _sigmoid(zv)) * _dgelu(ysv),
              extras=[(dy2, "mn"), (z, "mn"), (ys, "mn")])
    dd, = _ew(lambda a, b: jnp.sum(a * b, axis=0, keepdims=True), pfx + "_dd", [dys, u], sums=[MIX_HALF])
    dccat = _band_to_full(_mm_band(x, dys, pfx + "_dc", outer=True), MIX_HALF)
    dx = _mm_band(dys, ccat, pfx + "_dx", b_t=True)
    af_re, af_im = res["a"]
    apow, ptab = _s5_scan_tables(af_re, -af_im, True)
    lam, da8 = _s5_scan(dx, apow, ptab, pfx + "_scan_bwd", True, x_fwd=x)
    dbcat = _band_to_full(_mm_band(u, lam, pfx + "_db", outer=True), 2 * S5_N)
    du = _mm_band(lam, bcat, pfx + "_du", b_t=True, epi=lambda acc, dyv, dr: acc + dyv * dr, extras=[(dys, "mn"), (res["d_row"], "n")])
    G = S5_GROUPS
    d_abar_re, d_abar_im = (t.reshape(G, S5_STATE) for t in _unplanes(jnp.sum(da8, axis=0)))
    d_bb_re, d_bb_im = (_block_diag_take(t.T, G) for t in _unplanes(dbcat))
    _, vjp = jax.vjp(_s5_discretise, w["lam_re"], w["lam_im"], w["log_dt"], w["b_re"], w["b_im"])
    g_lam_re, g_lam_im, g_log_dt, g_b_re, g_b_im = vjp((d_abar_re, d_abar_im, d_bb_re, d_bb_im))
    dc_re, dc_im = _unplanes(dccat.T)
    g_c_re = jnp.swapaxes(_block_diag_take(dc_re.T, G), 1, 2)
    g_c_im = -jnp.swapaxes(_block_diag_take(dc_im.T, G), 1, 2)
    grads = dict(lam_re=g_lam_re, lam_im=g_lam_im, log_dt=g_log_dt, b_re=g_b_re, b_im=g_b_im, c_re=g_c_re, c_im=g_c_im,
                 d=dd.reshape(G, S5_GROUP_WIDTH), w_glu=dw_glu, b_glu=db_glu.reshape(-1))
    return du, grads


SGU_TS = 512
N_PAIRS = MIX_HALF // LANES


def _half_masks(rows):
    lane = lax.broadcasted_iota(jnp.int32, (rows, LANES), 1)
    left = (lane < HEAD_DIM).astype(F32)
    return left, 1.0 - left


def _sgu_norm(zv, gain, bias):
    v = _gelu(zv)
    mu = jnp.mean(v, axis=-1, keepdims=True)
    vc = v - mu
    rstd = lax.rsqrt(jnp.mean(vc * vc, axis=-1, keepdims=True) + EPS)
    vhat = vc * rstd
    return vhat, rstd, vhat * gain + bias


def _sgu_tables(w_s, b_s):
    mask = jnp.tril(jnp.ones((SGU_CHUNK, SGU_CHUNK), dtype=bool))
    wm = jnp.where(mask[None], w_s, 0.0).astype(BF16)
    bias_tab = jnp.repeat(b_s.T, MIX_HALF // SGU_GROUPS, axis=1)
    return wm, bias_tab


def _sgu_fwd(proj, ln_gain, ln_bias, wm, bias_tab, name):
    S = proj.shape[0]
    nch = SGU_TS // SGU_CHUNK

    def body(zu_ref, zv_ref, g_ref, b_ref, w_ref, bt_ref, o_ref):
        left, right = _half_masks(SGU_CHUNK)
        _, _, vn = _sgu_norm(zv_ref[...], g_ref[...], b_ref[...])
        for ch in range(nch):
            rows = pl.ds(ch * SGU_CHUNK, SGU_CHUNK)
            for p in range(N_PAIRS):
                cols = pl.ds(p * LANES, LANES)
                vp = vn[ch * SGU_CHUNK:(ch + 1) * SGU_CHUNK, p * LANES:(p + 1) * LANES]
                mixed = (jnp.dot(w_ref[2 * p], (vp * left).astype(BF16), preferred_element_type=F32)
                         + jnp.dot(w_ref[2 * p + 1], (vp * right).astype(BF16), preferred_element_type=F32) + bt_ref[:, cols])
                o_ref[rows, cols] = _gelu(zu_ref[rows, cols]) * mixed

    vec = _vec_spec(MIX_HALF)
    return pl.pallas_call(
        body, name=name, grid=(S // SGU_TS,),
        in_specs=[pl.BlockSpec((SGU_TS, MIX_HALF), lambda i: (i, 1)), pl.BlockSpec((SGU_TS, MIX_HALF), lambda i: (i, 2)), vec, vec,
                  pl.BlockSpec((SGU_GROUPS, SGU_CHUNK, SGU_CHUNK), lambda i: (0, 0, 0)), pl.BlockSpec((SGU_CHUNK, MIX_HALF), lambda i: (0, 0))],
        out_specs=_row_spec(MIX_HALF, SGU_TS), out_shape=jax.ShapeDtypeStruct((S, MIX_HALF), F32),
        compiler_params=_cparams("parallel"))(proj, proj, ln_gain, ln_bias, wm, bias_tab)


def _sgu_bwd(dout, proj, ln_gain, ln_bias, wm, bias_tab, name):
    S = proj.shape[0]
    nch = SGU_TS // SGU_CHUNK
    nt_dims = (((1,), (1,)), ((), ()))
    tn_dims = (((0,), (0,)), ((), ()))

    def body(do_ref, zu_ref, zv_ref, g_ref, b_ref, w_ref, bt_ref, dzu_ref, dzv_ref, dw_ref, dbt_ref, dg_ref, db_ref, dvn_ref):
        first = pl.program_id(0) == 0

        @pl.when(first)
        def _():
            dw_ref[...] = jnp.zeros_like(dw_ref)
            dbt_ref[...] = jnp.zeros_like(dbt_ref)
            dg_ref[...] = jnp.zeros_like(dg_ref)
            db_ref[...] = jnp.zeros_like(db_ref)

        left, right = _half_masks(SGU_CHUNK)
        zv = zv_ref[...]
        vhat, rstd, vn = _sgu_norm(zv, g_ref[...], b_ref[...])
        for ch in range(nch):
            rows = pl.ds(ch * SGU_CHUNK, SGU_CHUNK)
            for p in range(N_PAIRS):
                cols = pl.ds(p * LANES, LANES)
                vp = vn[ch * SGU_CHUNK:(ch + 1) * SGU_CHUNK, p * LANES:(p + 1) * LANES]
                vl, vr = (vp * left).astype(BF16), (vp * right).astype(BF16)
                mixed = (jnp.dot(w_ref[2 * p], vl, preferred_element_type=F32)
                         + jnp.dot(w_ref[2 * p + 1], vr, preferred_element_type=F32) + bt_ref[:, cols])
                zu = zu_ref[rows, cols]
                do = do_ref[rows, cols]
                dzu_ref[rows, cols] = do * mixed * _dgelu(zu)
                dmix = do * _gelu(zu)
                dbt_ref[:, cols] += dmix
                dl, dr = (dmix * left).astype(BF16), (dmix * right).astype(BF16)
                dw_ref[2 * p] += lax.dot_general(dl, vl, nt_dims, preferred_element_type=F32)
                dw_ref[2 * p + 1] += lax.dot_general(dr, vr, nt_dims, preferred_element_type=F32)
                dvn_ref[rows, cols] = (lax.dot_general(w_ref[2 * p], dl, tn_dims, preferred_element_type=F32)
                                       + lax.dot_general(w_ref[2 * p + 1], dr, tn_dims, preferred_element_type=F32))
        dvn = dvn_ref[...]
        dg_ref[...] += jnp.sum(dvn * vhat, axis=0, keepdims=True)
        db_ref[...] += jnp.sum(dvn, axis=0, keepdims=True)
        dvh = dvn * g_ref[...]
        dv = rstd * (dvh - jnp.mean(dvh, axis=-1, keepdims=True) - vhat * jnp.mean(dvh * vhat, axis=-1, keepdims=True))
        dzv_ref[...] = dv * _dgelu(zv)

    vec = _vec_spec(MIX_HALF)
    row = _row_spec(MIX_HALF, SGU_TS)
    wspec = pl.BlockSpec((SGU_GROUPS, SGU_CHUNK, SGU_CHUNK), lambda i: (0, 0, 0))
    tspec = pl.BlockSpec((SGU_CHUNK, MIX_HALF), lambda i: (0, 0))
    full = jax.ShapeDtypeStruct((S, MIX_HALF), F32)
    v = jax.ShapeDtypeStruct((1, MIX_HALF), F32)
    return pl.pallas_call(
        body, name=name, grid=(S // SGU_TS,),
        in_specs=[row, pl.BlockSpec((SGU_TS, MIX_HALF), lambda i: (i, 1)), pl.BlockSpec((SGU_TS, MIX_HALF), lambda i: (i, 2)), vec, vec,
                  wspec, tspec],
        out_specs=[row, row, wspec, tspec, vec, vec],
        out_shape=[full, full, jax.ShapeDtypeStruct((SGU_GROUPS, SGU_CHUNK, SGU_CHUNK), F32),
                   jax.ShapeDtypeStruct((SGU_CHUNK, MIX_HALF), F32), v, v],
        scratch_shapes=[pltpu.VMEM((SGU_TS, MIX_HALF), F32)],
        compiler_params=_cparams("arbitrary"))(dout, proj, proj, ln_gain, ln_bias, wm, bias_tab)


def _sgu_grads(dw, dbias_tab):
    mask = jnp.tril(jnp.ones((SGU_CHUNK, SGU_CHUNK), dtype=bool))
    g_w = jnp.where(mask[None], dw, 0.0)
    g_b = dbias_tab.reshape(SGU_CHUNK, SGU_GROUPS, MIX_HALF // SGU_GROUPS).sum(axis=-1).T
    return g_w, g_b


def _head_avg_matrix(w):
    idx = np.arange(w) // HEAD_DIM
    return jnp.asarray((idx[:, None] == idx[None, :]).astype(np.float32) / HEAD_DIM, dtype=BF16)


def _head_mean(t, bavg):
    hi = t.astype(BF16)
    lo = (t - hi.astype(F32)).astype(BF16)
    return jnp.dot(hi, bavg, preferred_element_type=F32) + jnp.dot(lo, bavg, preferred_element_type=F32)


def _head_rms(t, bavg):
    r = lax.rsqrt(_head_mean(t * t, bavg) + EPS)
    return t * r, r


def _head_rms_bwd(dn, n, r, bavg):
    return r * (dn - n * _head_mean(dn * n, bavg))


GLA_TS = 512
C = GLA_CHUNK
NT_DIMS = (((1,), (1,)), ((), ()))
TN_DIMS = (((0,), (0,)), ((), ()))
HI = lax.Precision.HIGHEST


def _bdot(a, b, dims=(((1,), (0,)), ((), ()))):
    return lax.dot_general(a.astype(BF16), b.astype(BF16), dims, preferred_element_type=F32)


def _gla_chunk_terms(q, k, z):
    row = lax.broadcasted_iota(jnp.int32, (C, C), 0)
    col = lax.broadcasted_iota(jnp.int32, (C, C), 1)
    lc = _log_sigmoid(z) * (1.0 / GLA_TAU)
    b = lax.dot_general((row >= col).astype(F32), lc, (((1,), (0,)), ((), ())), precision=HI, preferred_element_type=F32)
    b_last = jnp.sum(lc, axis=0, keepdims=True)
    b_mid = b[C // 2:C // 2 + 1, :]
    scale = HEAD_DIM ** -0.5
    e_b, e_q, e_k, e_l = jnp.exp(b), jnp.exp(b - b_mid), jnp.exp(b_mid - b), jnp.exp(b_last - b)
    qs = q * (scale * e_b)
    qe = q * (scale * e_q)
    ke = k * e_k
    kl = k * e_l
    return dict(e_b=e_b, e_q=e_q, e_k=e_k, e_l=e_l, qs=qs, qe=qe, ke=ke, kl=kl, dec=jnp.exp(b_last), causal=row >= col, scale=scale)


def _pair(x, pp):
    return x[:, pp * LANES:(pp + 1) * LANES]


def _pair_block_diag():
    r = lax.broadcasted_iota(jnp.int32, (LANES, LANES), 0) // HEAD_DIM
    c = lax.broadcasted_iota(jnp.int32, (LANES, LANES), 1) // HEAD_DIM
    return (r == c).astype(F32)


def _gla_fwd(proj, z, name):
    S = proj.shape[0]
    nch = GLA_TS // C

    def body(q_ref, k_ref, v_ref, z_ref, o_ref, st_ref, state_ref):
        @pl.when(pl.program_id(0) == 0)
        def _():
            state_ref[...] = jnp.zeros_like(state_ref)

        left, right = _half_masks(C)
        bd = _pair_block_diag()
        pairs = range(N_PAIRS)
        for ch in range(nch):
            rows = pl.ds(ch * C, C)
            v = v_ref[rows, :]
            t = _gla_chunk_terms(q_ref[rows, :], k_ref[rows, :], z_ref[rows, :])
            sts = [state_ref[pp] for pp in pairs]
            for pp in pairs:
                st_ref[ch, pp] = sts[pp]
            os = [_bdot(_pair(t["qs"], pp), sts[pp], NT_DIMS) for pp in pairs]
            for m in (left, right):
                scores = [jnp.where(t["causal"], _bdot(_pair(t["qe"], pp) * m, _pair(t["ke"], pp), NT_DIMS), 0.0) for pp in pairs]
                os = [os[pp] + m * _bdot(scores[pp], _pair(v, pp)) for pp in pairs]
            o_ref[rows, :] = jnp.concatenate(os, axis=1)
            new = [sts[pp] * _pair(t["dec"], pp) + bd * _bdot(_pair(v, pp), _pair(t["kl"], pp), TN_DIMS) for pp in pairs]
            for pp in pairs:
                state_ref[pp] = new[pp]

    def col(cb):
        return pl.BlockSpec((GLA_TS, MIX_HALF), lambda i: (i, cb))

    return pl.pallas_call(
        body, name=name, grid=(S // GLA_TS,),
        in_specs=[col(0), col(1), col(2), col(0)],
        out_specs=[col(0), pl.BlockSpec((nch, N_PAIRS, LANES, LANES), lambda i: (i, 0, 0, 0))],
        out_shape=[jax.ShapeDtypeStruct((S, MIX_HALF), F32), jax.ShapeDtypeStruct((S // C, N_PAIRS, LANES, LANES), F32)],
        scratch_shapes=[pltpu.VMEM((N_PAIRS, LANES, LANES), F32)], compiler_params=_cparams("arbitrary"))(proj, proj, proj, z)


def _gla_bwd(do, proj, z, states, name):
    S = proj.shape[0]
    nch = GLA_TS // C
    nblk = S // GLA_TS

    def body(do_ref, q_ref, k_ref, v_ref, z_ref, st_ref, dq_ref, dk_ref, dv_ref, dlc_ref, dstate_ref):
        @pl.when(pl.program_id(0) == 0)
        def _():
            dstate_ref[...] = jnp.zeros_like(dstate_ref)

        left, right = _half_masks(C)
        bd = _pair_block_diag()
        rowi = lax.broadcasted_iota(jnp.int32, (C, LANES), 0)
        row = lax.broadcasted_iota(jnp.int32, (C, C), 0)
        colm = lax.broadcasted_iota(jnp.int32, (C, C), 1)
        pairs = range(N_PAIRS)
        rowi = lax.broadcasted_iota(jnp.int32, (C, MIX_HALF), 0)
        for ch in range(nch - 1, -1, -1):
            rows = pl.ds(ch * C, C)
            v, dov = v_ref[rows, :], do_ref[rows, :]
            t = _gla_chunk_terms(q_ref[rows, :], k_ref[rows, :], z_ref[rows, :])
            sts = [st_ref[ch, pp] for pp in pairs]
            nxt = [dstate_ref[pp] for pp in pairs]
            gs = [bd * nxt[pp] for pp in pairs]
            dqs = [_bdot(_pair(dov, pp), sts[pp]) for pp in pairs]
            dv = [_bdot(_pair(t["kl"], pp), gs[pp], NT_DIMS) for pp in pairs]
            dkl = [_bdot(_pair(v, pp), gs[pp]) for pp in pairs]
            dqe = [jnp.zeros((C, LANES), F32) for _ in pairs]
            dke = [jnp.zeros((C, LANES), F32) for _ in pairs]
            for m in (left, right):
                sc = [jnp.where(t["causal"], _bdot(_pair(t["qe"], pp) * m, _pair(t["ke"], pp), NT_DIMS), 0.0) for pp in pairs]
                dsc = [jnp.where(t["causal"], _bdot(_pair(dov, pp) * m, _pair(v, pp), NT_DIMS), 0.0) for pp in pairs]
                dv = [dv[pp] + m * _bdot(sc[pp], _pair(dov, pp), TN_DIMS) for pp in pairs]
                dqe = [dqe[pp] + m * _bdot(dsc[pp], _pair(t["ke"], pp)) for pp in pairs]
                dke = [dke[pp] + m * _bdot(dsc[pp], _pair(t["qe"], pp), TN_DIMS) for pp in pairs]
            for pp in pairs:
                dstate_ref[pp] = bd * (nxt[pp] * _pair(t["dec"], pp) + _bdot(_pair(dov, pp), _pair(t["qs"], pp), TN_DIMS))
            decay_sum = jnp.concatenate([jnp.sum(nxt[pp] * sts[pp], axis=0, keepdims=True) for pp in pairs], axis=1)
            dqs, dv, dkl, dqe, dke = (jnp.concatenate(parts, axis=1) for parts in (dqs, dv, dkl, dqe, dke))
            db_last = decay_sum * t["dec"] + jnp.sum(dkl * t["kl"], axis=0, keepdims=True)
            db = dqs * t["qs"] + dqe * t["qe"] - dke * t["ke"] - dkl * t["kl"]
            db = db + jnp.where(rowi == C - 1, db_last, 0.0)
            dq_ref[rows, :] = (dqs * t["e_b"] + dqe * t["e_q"]) * t["scale"]
            dk_ref[rows, :] = dke * t["e_k"] + dkl * t["e_l"]
            dv_ref[rows, :] = dv
            dlc_ref[rows, :] = lax.dot_general((colm >= row).astype(F32), db, (((1,), (0,)), ((), ())), precision=HI,
                                               preferred_element_type=F32)

    def col(cb):
        return pl.BlockSpec((GLA_TS, MIX_HALF), lambda i: (nblk - 1 - i, cb))

    full = jax.ShapeDtypeStruct((S, MIX_HALF), F32)
    return pl.pallas_call(
        body, name=name, grid=(nblk,),
        in_specs=[col(0), col(0), col(1), col(2), col(0), pl.BlockSpec((nch, N_PAIRS, LANES, LANES), lambda i: (nblk - 1 - i, 0, 0, 0))],
        out_specs=[col(0)] * 4, out_shape=[full, full, full, full],
        scratch_shapes=[pltpu.VMEM((N_PAIRS, LANES, LANES), F32)], compiler_params=_cparams("arbitrary"))(do, proj, proj, proj, z, states)


def _gla_block_fwd(proj, w_lr_pad, b_lr, gain, bavg, pfx):
    z = _mm(proj, w_lr_pad, "nn", pfx + "_z", a_cols=(7 * MIX_HALF, MIX_HALF), epi=lambda acc, b: acc + b, extras=[(b_lr, "n")])
    o, states = _gla_fwd(proj, z, pfx + "_core")

    def out(ov, gg, ba, gn):
        n, _ = _head_rms(ov, ba)
        return n * gn * (gg * _sigmoid(gg))

    og, = _ew(out, pfx + "_out", [o, (proj, MIX_HALF, 3)], consts=[bavg, gain], outs=[(MIX_HALF, F32)])
    return og, dict(z=z, o=o, states=states)


def _gla_block_bwd(dog, proj, w_lr_pad, gain, bavg, res, pfx):
    z, o, states = res["z"], res["o"], res["states"]

    def out_bwd(dy, ov, gg, ba, gn):
        n, r = _head_rms(ov, ba)
        sg = _sigmoid(gg)
        silu = gg * sg
        dn = dy * gn * silu
        do = _head_rms_bwd(dn, n, r, ba)
        dgg = dy * n * gn * (sg * (1.0 + gg * (1.0 - sg)))
        return do, dgg, jnp.sum(dy * n * silu, axis=0, keepdims=True)

    do, dgg, dgain = _ew(out_bwd, pfx + "_out_bwd", [dog, o, (proj, MIX_HALF, 3)], consts=[bavg, gain],
                         outs=[(MIX_HALF, F32), (MIX_HALF, F32)], sums=[MIX_HALF])
    dq, dk, dv, dlc = _gla_bwd(do, proj, z, states, pfx + "_core_bwd")

    def decay_bwd(dl, zv):
        dz = dl * (1.0 / GLA_TAU) * (1.0 - _sigmoid(zv))
        return dz, jnp.sum(dz, axis=0, keepdims=True)

    dz, db_lr = _ew(decay_bwd, pfx + "_decay_bwd", [dlc, z], outs=[(MIX_HALF, F32)], sums=[MIX_HALF])
    dw_lr_pad = _mm(proj, dz, "tn", pfx + "_dwlr", a_cols=(7 * MIX_HALF, MIX_HALF))
    dsmall = _mm(dz, w_lr_pad, "nt", pfx + "_dsmall")
    return (dq, dk, dv, dgg, dsmall), dict(w_lr=dw_lr_pad[:GLA_RANK], b_lr=db_lr.reshape(-1), gain=dgain.reshape(-1, HEAD_DIM))


FOX_T = 512
FOX_HEADS = MIX_HALF // HEAD_DIM
NEG = -1e30
CUM_T = 512


def _cum_lanes(x, name, reverse, pre=None):
    R, S = x.shape
    nb = S // CUM_T

    def body(x_ref, o_ref, carry_ref):
        @pl.when(pl.program_id(0) == 0)
        def _():
            carry_ref[...] = jnp.zeros_like(carry_ref)

        xv = x_ref[...]
        if pre is not None:
            xv = pre(xv)
        i = lax.broadcasted_iota(jnp.int32, (CUM_T, CUM_T), 0)
        j = lax.broadcasted_iota(jnp.int32, (CUM_T, CUM_T), 1)
        tri = ((i >= j) if reverse else (i <= j)).astype(F32)
        c = lax.dot_general(xv, tri, (((1,), (0,)), ((), ())), precision=HI, preferred_element_type=F32)
        carry = carry_ref[...]
        o_ref[...] = c + carry[:, 0:1]
        carry_ref[...] = carry + jnp.sum(xv, axis=1, keepdims=True)

    spec = pl.BlockSpec((R, CUM_T), (lambda i: (0, nb - 1 - i)) if reverse else (lambda i: (0, i)))
    return pl.pallas_call(body, name=name, grid=(nb,), in_specs=[spec], out_specs=spec, out_shape=jax.ShapeDtypeStruct((R, S), F32),
                          scratch_shapes=[pltpu.VMEM((R, LANES), F32)], compiler_params=_cparams("arbitrary"))(x)


def _fox_scores(q, k, cqb, ck_ref, h, m, diag):
    cq = cqb[:, h * HEAD_DIM:h * HEAD_DIM + 1]
    ck = ck_ref[0, h:h + 1, :]
    s = lax.dot_general(q * m.astype(q.dtype), k, NT_DIMS, preferred_element_type=F32) + (cq - ck)
    if not diag:
        return s
    row = lax.broadcasted_iota(jnp.int32, (FOX_T, FOX_T), 0)
    col = lax.broadcasted_iota(jnp.int32, (FOX_T, FOX_T), 1)
    return jnp.where(row < col, NEG, s)


def _on_causal_blocks(q_blk, k_blk, step):
    @pl.when(k_blk < q_blk)
    def _():
        step(False)

    @pl.when(k_blk == q_blk)
    def _():
        step(True)


def _causal_pairs(n, key_major):
    if key_major:
        pairs = [(q, k) for k in range(n) for q in range(k, n)]
    else:
        pairs = [(q, k) for q in range(n) for k in range(q + 1)]
    return jnp.asarray([p[0] for p in pairs], jnp.int32), jnp.asarray([p[1] for p in pairs], jnp.int32)


def _carried(carry, refs, n_in, n_out, first, last):
    if carry is None:
        return refs
    ins, cx_ref, outs, co_ref = refs[:n_in], refs[n_in], refs[n_in + 1:n_in + 1 + n_out], refs[n_in + 1 + n_out]
    scratch = refs[n_in + 2 + n_out:]
    start, finish = _exchange_plan(cx_ref, co_ref, *scratch[-3:], carry[1])
    pl.when(first)(start)
    pl.when(last)(finish)
    return ins + outs + scratch[:-3]


def _carry_specs(carry):
    if carry is None:
        return [], [], [], [], []
    x, bcast = carry
    blk = x.shape if bcast else x.shape[1:]
    return [ANY], [ANY], [jax.ShapeDtypeStruct((N_CHIPS,) + tuple(blk), x.dtype)], list(_EXCHANGE_SEMS), [x]


def _fox_fwd(qn, kn, proj, cum_b, cum_tp, name, carry=None):
    S = qn.shape[0]
    nq = S // FOX_T
    qidx, kidx = _causal_pairs(nq, False)
    ntri = int(qidx.shape[0])

    def body(qidx_ref, kidx_ref, *refs):
        t = pl.program_id(1)
        first = jnp.logical_and(pl.program_id(0) == 0, t == 0)
        last = jnp.logical_and(pl.program_id(0) == N_PAIRS - 1, t == ntri - 1)
        q_ref, k_ref, v_ref, cq_ref, ck_ref, o_ref, lse_ref, m_scr, acc_scr = _carried(carry, refs, 5, 2, first, last)
        qi, ki = qidx_ref[t], kidx_ref[t]

        @pl.when(ki == 0)
        def _():
            m_scr[...] = jnp.full_like(m_scr, NEG)
            acc_scr[...] = jnp.zeros_like(acc_scr)

        left, right = _half_masks(FOX_T)

        def step(diag):
            q, k, v = q_ref[...], k_ref[...], v_ref[...].astype(BF16)
            cqb = cq_ref[...]
            for h, m in enumerate((left, right)):
                s = _fox_scores(q, k, cqb, ck_ref, h, m, diag)
                m_prev = m_scr[h]
                m_new = jnp.maximum(m_prev, jnp.max(s, axis=1, keepdims=True))
                p = jnp.exp(s - m_new)
                v_h = jnp.where(m > 0, v, jnp.ones_like(v))
                acc_scr[h] = jnp.exp(m_prev - m_new) * acc_scr[h] + jnp.dot(p.astype(BF16), v_h, preferred_element_type=F32)
                m_scr[h] = m_new

        _on_causal_blocks(qi, ki, step)

        @pl.when(ki == qi)
        def _():
            a0, a1 = acc_scr[0], acc_scr[1]
            is_left = left > 0
            num = jnp.where(is_left, a0, a1)
            den = jnp.where(is_left, pltpu.roll(a0, HEAD_DIM, 1), pltpu.roll(a1, HEAD_DIM, 1))
            o_ref[...] = num / den
            lse_ref[...] = jnp.where(is_left, m_scr[0], m_scr[1]) + jnp.log(den)

    qspec = pl.BlockSpec((FOX_T, LANES), lambda p, t, qx, kx: (qx[t], p))
    kspec = pl.BlockSpec((FOX_T, LANES), lambda p, t, qx, kx: (kx[t], p))
    vspec = pl.BlockSpec((FOX_T, LANES), lambda p, t, qx, kx: (kx[t], 6 * N_PAIRS + p))
    ckspec = pl.BlockSpec((1, 8, FOX_T), lambda p, t, qx, kx: (p, 0, kx[t]))
    full = jax.ShapeDtypeStruct((S, MIX_HALF), F32)
    c_in, c_out, c_shape, c_scratch, c_args = _carry_specs(carry)
    grid_spec = pltpu.PrefetchScalarGridSpec(
        num_scalar_prefetch=2, grid=(N_PAIRS, ntri), in_specs=[qspec, kspec, vspec, qspec, ckspec] + c_in, out_specs=[qspec, qspec] + c_out,
        scratch_shapes=[pltpu.VMEM((2, FOX_T, 1), F32), pltpu.VMEM((2, FOX_T, LANES), F32)] + c_scratch)
    return pl.pallas_call(body, name=name, grid_spec=grid_spec, out_shape=[full, full] + c_shape,
                          compiler_params=_cparams("arbitrary", "arbitrary"))(qidx, kidx, qn, kn, proj, cum_b, cum_tp, *c_args)


def _fox_bwd(do, qn, kn, proj, cum_b, cum_tp, lse_b, delta_b, name, carry=None, do_pair0=0):
    S = qn.shape[0]
    nq = S // FOX_T
    scale = HEAD_DIM ** -0.5
    qidx, kidx = _causal_pairs(nq, True)
    ntri = int(qidx.shape[0])

    def body(qidx_ref, kidx_ref, *refs):
        t = pl.program_id(1)
        first = jnp.logical_and(pl.program_id(0) == 0, t == 0)
        last = jnp.logical_and(pl.program_id(0) == N_PAIRS - 1, t == ntri - 1)
        (do_ref, q_ref, k_ref, v_ref, cq_ref, ck_ref, lse_ref, dl_ref, dq_ref, dcq_ref, dk_ref, dv_ref, dck_ref,
         dq_scr, dk_scr, dv_scr) = _carried(carry, refs, 8, 5, first, last)
        qi, ki = qidx_ref[t], kidx_ref[t]

        @pl.when(t == 0)
        def _():
            dq_scr[...] = jnp.zeros_like(dq_scr)

        @pl.when(qi == ki)
        def _():
            dk_scr[...] = jnp.zeros_like(dk_scr)
            dv_scr[...] = jnp.zeros_like(dv_scr)

        left, right = _half_masks(FOX_T)
        rows = pl.ds(pl.multiple_of(qi * FOX_T, FOX_T), FOX_T)

        def step(diag):
            q, k, v, dov = q_ref[...], k_ref[...], v_ref[...].astype(BF16), do_ref[...]
            cqb, lseb, dlb = cq_ref[...], lse_ref[...], dl_ref[...]
            dob = dov.astype(BF16)
            heads = (0, 1)
            masks = (left, right)
            col = [slice(h * HEAD_DIM, h * HEAD_DIM + 1) for h in heads]
            ss = [_fox_scores(q, k, cqb, ck_ref, h, masks[h], diag) for h in heads]
            dps = [lax.dot_general((dov * masks[h]).astype(BF16), v, NT_DIMS, preferred_element_type=F32) for h in heads]
            ps = [jnp.exp(ss[h] - lseb[:, col[h]]) for h in heads]
            dss = [(ps[h] * (dps[h] - dlb[:, col[h]])).astype(BF16) for h in heads]
            pvs = [lax.dot_general(ps[h].astype(BF16), dob, TN_DIMS, preferred_element_type=F32) for h in heads]
            dks = [lax.dot_general(dss[h], jnp.where(masks[h] > 0, q, jnp.ones_like(q)), TN_DIMS, preferred_element_type=F32) for h in heads]
            dqs = [jnp.dot(dss[h], jnp.where(masks[h] > 0, k, jnp.ones_like(k)), preferred_element_type=F32) for h in heads]
            dv_scr[...] = dv_scr[...] + left * pvs[0] + right * pvs[1]
            for h in heads:
                dk_scr[h] = dk_scr[h] + dks[h]
                dq_scr[h, rows, :] = dq_scr[h, rows, :] + dqs[h]

        _on_causal_blocks(qi, ki, step)

        @pl.when(qi == nq - 1)
        def _():
            a0, a1 = dk_scr[0], dk_scr[1]
            dk_ref[...] = left * a0 + right * a1
            dv_ref[...] = dv_scr[...]
            dck_ref[...] = left * pltpu.roll(a0, HEAD_DIM, 1) + right * pltpu.roll(a1, HEAD_DIM, 1)

        @pl.when(t == ntri - 1)
        def _():
            for r in range(nq):
                blk = pl.ds(r * FOX_T, FOX_T)
                a0, a1 = dq_scr[0, blk, :], dq_scr[1, blk, :]
                dq_ref[blk, :] = (left * a0 + right * a1) * scale
                dcq_ref[blk, :] = left * pltpu.roll(a0, HEAD_DIM, 1) + right * pltpu.roll(a1, HEAD_DIM, 1)

    qspec = pl.BlockSpec((FOX_T, LANES), lambda p, t, qx, kx: (qx[t], p))
    kspec = pl.BlockSpec((FOX_T, LANES), lambda p, t, qx, kx: (kx[t], p))
    vspec = pl.BlockSpec((FOX_T, LANES), lambda p, t, qx, kx: (kx[t], 6 * N_PAIRS + p))
    ckspec = pl.BlockSpec((1, 8, FOX_T), lambda p, t, qx, kx: (p, 0, kx[t]))
    seq = pl.BlockSpec((S, LANES), lambda p, t, qx, kx: (0, p))
    full = jax.ShapeDtypeStruct((S, MIX_HALF), F32)
    c_in, c_out, c_shape, c_scratch, c_args = _carry_specs(carry)
    grid_spec = pltpu.PrefetchScalarGridSpec(
        num_scalar_prefetch=2, grid=(N_PAIRS, ntri),
        in_specs=[pl.BlockSpec((FOX_T, LANES), lambda p, t, qx, kx: (qx[t], do_pair0 + p)), qspec, kspec, vspec, qspec, ckspec, qspec, qspec] + c_in,
        out_specs=[seq, seq, kspec, kspec, kspec] + c_out,
        scratch_shapes=[pltpu.VMEM((2, S, LANES), F32), pltpu.VMEM((2, FOX_T, LANES), F32), pltpu.VMEM((FOX_T, LANES), F32)] + c_scratch)
    return pl.pallas_call(body, name=name, grid_spec=grid_spec, out_shape=[full] * 5 + c_shape,
                          compiler_params=_cparams("arbitrary", "arbitrary"))(qidx, kidx, do, qn, kn, proj, cum_b, cum_tp, lse_b, delta_b, *c_args)


def _ff_bwd(rc, f_t, name):
    def body(rc_ref, f_ref, d_ref, s_ref):
        d = rc_ref[...] * (1.0 - _sigmoid(f_ref[...]))
        d_ref[...] = d
        s_ref[...] = jnp.sum(d, axis=1, keepdims=True)

    return pl.pallas_call(body, name=name, out_shape=[jax.ShapeDtypeStruct(rc.shape, F32), jax.ShapeDtypeStruct((rc.shape[0], 1), F32)])(rc, f_t)


def _fox_block_fwd(proj, b_f, q_gain, k_gain, bavg, pfx, carry=None):
    S = proj.shape[0]

    def prep(qv, kv, ba, qg, kg):
        return _head_rms(qv, ba)[0] * qg * (HEAD_DIM ** -0.5), _head_rms(kv, ba)[0] * kg

    qn, kn = _ew(prep, pfx + "_prep", [(proj, MIX_HALF, 4), (proj, MIX_HALF, 5)], consts=[bavg, q_gain, k_gain],
                 outs=[(MIX_HALF, BF16), (MIX_HALF, BF16)])
    f0 = 7 * MIX_HALF + GLA_RANK
    f_t = proj[:, f0:f0 + FOX_HEADS].T + b_f.reshape(FOX_HEADS, 1)
    cum = _cum_lanes(f_t, pfx + "_cum", False, pre=_log_sigmoid)
    cum_b = jnp.repeat(cum.T, HEAD_DIM, axis=1)
    cum_tp = jnp.pad(cum.reshape(N_PAIRS, 2, S), ((0, 0), (0, 6), (0, 0)))
    o, lse_b, *carried = _fox_fwd(qn, kn, proj, cum_b, cum_tp, pfx + "_attn", carry=carry)
    return o, dict(qn=qn, kn=kn, f_t=f_t, cum_b=cum_b, cum_tp=cum_tp, o=o, lse_b=lse_b), carried


def _fox_block_bwd(do, proj, q_gain, k_gain, bavg, res, pfx, carry=None):
    qn, kn, o = res["qn"], res["kn"], res["o"]
    S = proj.shape[0]
    delta_b, = _ew(lambda a, b, ba: _head_mean(a * b, ba) * float(HEAD_DIM), pfx + "_delta", [do, o], consts=[bavg], outs=[(MIX_HALF, F32)])
    do_arr, do_blk = (do[0], do[2]) if isinstance(do, tuple) else (do, 0)
    args = (do_arr, qn, kn, proj, res["cum_b"], res["cum_tp"], res["lse_b"], delta_b)
    dqn, dcq_b, dkn, dv, dck_b, *carried = _fox_bwd(*args, pfx + "_bwd", carry=carry, do_pair0=do_blk * N_PAIRS)

    def prep_bwd(dq, dk, qv, kv, ba, qg, kg):
        nq, rq = _head_rms(qv, ba)
        nk, rk = _head_rms(kv, ba)
        return (_head_rms_bwd(dq * qg, nq, rq, ba), _head_rms_bwd(dk * kg, nk, rk, ba),
                jnp.sum(dq * nq, axis=0, keepdims=True), jnp.sum(dk * nk, axis=0, keepdims=True))

    dfq, dfk, dqg, dkg = _ew(prep_bwd, pfx + "_prep_bwd", [dqn, dkn, (proj, MIX_HALF, 4), (proj, MIX_HALF, 5)],
                             consts=[bavg, q_gain, k_gain], outs=[(MIX_HALF, F32), (MIX_HALF, F32)], sums=[MIX_HALF, MIX_HALF])
    dcum = (dcq_b - dck_b)[:, ::HEAD_DIM].T
    rc = _cum_lanes(dcum, pfx + "_rcum", True)
    dff_t, db_f = _ff_bwd(rc, res["f_t"], pfx + "_ff_bwd")
    grads = dict(b_f=db_f.reshape(-1), q_gain=dqg.reshape(-1, HEAD_DIM), k_gain=dkg.reshape(-1, HEAD_DIM))
    return (dfq, dfk, dv, dff_t.T), grads, carried


WEIGHTS = ['ada_w', 'ada_b', 'even_w_in', 'even_w_out', 'gla_w_lr', 'gla_b_lr', 'gla_gain', 'fox_b_f', 'fox_q_gain', 'fox_k_gain',
           'odd_w_in', 'odd_w_out', 's5_lam_re', 's5_lam_im', 's5_log_dt', 's5_b_re', 's5_b_im', 's5_c_re', 's5_c_im', 's5_d',
           's5_w_glu', 's5_b_glu', 'sgu_ln_gain', 'sgu_ln_bias', 'sgu_w_s', 'sgu_b_s', 'mlp_w1', 'mlp_w2']
ARGS = ['x', 'c'] + WEIGHTS + ['loss_target'] + ['m_' + w for w in WEIGHTS] + ['v_' + w for w in WEIGHTS]

EVEN_COLS = 3608
EVEN_PAD = 8 * MIX_HALF
MOD = 6 * D_MODEL
MOD_SHARD = MOD // N_CHIPS

PACK_COLS = 1024
EVEN_SHARD = EVEN_COLS // N_CHIPS
SHARDED = (
    ([("even_w_in", (1, 1024, PACK_COLS), 2), ("even_w_out", (1, 256, 1024), 1), ("gla_w_lr", (1, 16, 128), 2)], 1536),
    ([("mlp_w1_0", (1, 1024, 1024), 2), ("mlp_w2_0", (1, 1024, 1024), 1), ("odd_w_in", (1, 1024, 384), 2),
      ("odd_w_out", (1, 256, 1024), 1), ("mlp_w1_1", (1, 1024, 1024), 2), ("mlp_w2_1", (1, 1024, 1024), 1),
      ("s5_w_glu", (1, 128, 512), 1), ("s5_b_glu", (1, 128), 1), ("sgu_ln_gain", (1, 128), 1), ("sgu_ln_bias", (1, 128), 1)], 5120))
REPLICATED = [("gla_b_lr", (1, 512)), ("gla_gain", (1, 8, 64)), ("fox_b_f", (1, 8)), ("fox_q_gain", (1, 8, 64)),
              ("fox_k_gain", (1, 8, 64)), ("s5_lam_re", (1, 32, 64)), ("s5_lam_im", (1, 32, 64)), ("s5_log_dt", (1, 32)),
              ("s5_b_re", (1, 32, 64, 16)), ("s5_b_im", (1, 32, 64, 16)), ("s5_c_re", (1, 32, 16, 64)), ("s5_c_im", (1, 32, 16, 64)),
              ("s5_d", (1, 32, 16)), ("sgu_w_s", (1, 8, 128, 128)), ("sgu_b_s", (1, 8, 128))]
SMALL_ROWS = 512
BIG_ADAM = {"ada_w": (2048, 1536), "even_w_in": (1024, 902), "even_w_out": (256, 1024), "odd_w_in": (1024, 384),
            "odd_w_out": (256, 1024), "mlp_w1": (2048, 1024), "mlp_w2": (2048, 1024), "s5_w_glu": (128, 512)}


PACK_ALIGN = 16


def _piece_rows(shape):
    rows = -(-math.prod(shape) // PACK_COLS)
    return -(-rows // PACK_ALIGN) * PACK_ALIGN


def _to_rows(p, lead=()):
    n = math.prod(p.shape[len(lead):])
    rows = _piece_rows(p.shape[len(lead):])
    flat = p.reshape(lead + (n,))
    if rows * PACK_COLS != n:
        flat = jnp.pad(flat, [(0, 0)] * len(lead) + [(0, rows * PACK_COLS - n)])
    return flat.reshape(lead + (rows, PACK_COLS))


def _from_rows(x, r0, shape, lead=()):
    n = math.prod(shape)
    seg = lax.slice_in_dim(x, r0, r0 + _piece_rows(shape), axis=len(lead)).reshape(lead + (-1,))
    return lax.slice_in_dim(seg, 0, n, axis=len(lead)).reshape(lead + tuple(shape))


def _pack_rows(pieces, rows):
    x = jnp.concatenate([_to_rows(p) for p in pieces], axis=0)
    return jnp.pad(x, ((0, rows - x.shape[0]), (0, 0)))


def _unpack(x, specs):
    out, r0 = {}, 0
    for name, shape in specs:
        out[name] = _from_rows(x, r0, shape)
        r0 += _piece_rows(shape)
    return out


def _shards_to_full(x4, pieces):
    out, r0 = {}, 0
    for name, shape, axis in pieces:
        seg = _from_rows(x4, r0, shape, lead=(N_CHIPS,))
        out[name] = jnp.concatenate([seg[k] for k in range(N_CHIPS)], axis=axis)
        r0 += _piece_rows(shape)
    return out


def _full_to_shards(full, pieces, rows):
    blocks = [_to_rows(jnp.stack(jnp.split(full[name], N_CHIPS, axis=axis)), lead=(N_CHIPS,)) for name, _, axis in pieces]
    x = jnp.concatenate(blocks, axis=1)
    return jnp.pad(x, ((0, 0), (0, rows - x.shape[1]), (0, 0)))


def _gather_prep(local, pieces, rows):
    shard = _pack_rows([local[n] for n, _, _ in pieces], rows).astype(BF16)
    return lax.dynamic_slice_in_dim(shard, lax.axis_index("c") * (rows // 2), rows // 2, axis=0)


def _gather_finish(collected, pieces, rows, tag):
    halves = _by_core(collected, _pair_swap(collected, tag + "_pair"))
    return _shards_to_full(halves.transpose(1, 0, 2, 3).reshape(N_CHIPS, rows, PACK_COLS), pieces)


def _reduce_prep(full, pieces, rows, tag):
    mc = lax.axis_index("c")
    packed = _full_to_shards(full, pieces, rows)
    hr = rows // 2
    mine = lax.dynamic_slice_in_dim(packed, mc * hr, hr, axis=1)
    other = lax.dynamic_slice_in_dim(packed, (1 - mc) * hr, hr, axis=1)
    theirs = _pair_swap(other.astype(BF16), tag + "_pair")
    pair_sum, = _ew(lambda p, q: p + q, tag + "_pair_sum", [mine.reshape(N_CHIPS * hr, PACK_COLS), theirs.reshape(N_CHIPS * hr, PACK_COLS)],
                    outs=[(PACK_COLS, BF16)])
    return pair_sum.reshape(N_CHIPS, hr, PACK_COLS)


def _reduce_finish(arrived, pieces, rows, tag):
    red_half = _sum_slots(arrived, tag + "_chip_sum")
    reduced = _by_core(red_half, _pair_swap(red_half, tag + "_pair_out")).reshape(rows, PACK_COLS)
    return _unpack(reduced, [(n, s) for n, s, _ in pieces])


def _relu2(t):
    r = jnp.maximum(t, 0.0)
    return r * r


def _silu(t):
    return t * _sigmoid(t)


def _pack_even(w):
    return jnp.concatenate([w[:, :2048], w[:, 2064:3600], w[:, 2048:2064], w[:, 3600:3608],
                            jnp.zeros((w.shape[0], EVEN_PAD - EVEN_COLS), w.dtype)], axis=1)


def _unpack_even(wp):
    return jnp.concatenate([wp[:, :2048], wp[:, 3584:3600], wp[:, 2048:3584], wp[:, 3600:3608]], axis=1)


def _mlp_fwd(h, w1, w2, pfx):
    pre = _mm(h, w1, "nn", pfx + "_up", out_dtype=BF16)
    return pre, _mm(pre, w2, "nn", pfx + "_down", a_pro=_relu2, out_dtype=BF16)


def _mlp_bwd(dm, h, pre, w1, w2, pfx):
    dpre = _mm(dm, w2, "nt", pfx + "_dpre", epi=lambda acc, p: acc * (2.0 * jnp.maximum(p, 0.0)), extras=[(pre, "mn")], out_dtype=BF16)
    dw2 = _mm(pre, dm, "tn", pfx + "_dw2", a_pro=_relu2)
    dw1 = _mm(h, dpre, "tn", pfx + "_dw1")
    dh = _mm(dpre, w1, "nt", pfx + "_dh", out_dtype=BF16)
    return dh, dw1, dw2


def _step(args):
    a = dict(zip(ARGS, args, strict=True))
    x0 = a["x"][0]
    target = a["loss_target"][0]
    mx, my, mc = lax.axis_index("x"), lax.axis_index("y"), lax.axis_index("c")
    chip = 2 * mx + my
    dev = 2 * chip + mc
    bavg = _head_avg_matrix(MIX_HALF)

    c_all = _gather8(jnp.pad(a["c"], ((0, 7), (0, 0))), "c_gather")[:, :, 0, :].reshape(2 * N_CHIPS, D_MODEL)
    ada_b_shard = lax.dynamic_slice_in_dim(a["ada_b"], chip * MOD_SHARD, MOD_SHARD, axis=1)
    mod_sh = [_mm(c_all, a["ada_w"][l], "nn", f"mod{l}", a_pro=_silu, epi=lambda acc, b: acc + b, extras=[(ada_b_shard[l:l + 1], "n")])
              for l in range(2)]
    small3 = jnp.zeros((8, MOD_SHARD), F32)
    for r, n in enumerate(("s5_b_glu", "sgu_ln_gain", "sgu_ln_bias")):
        small3 = small3.at[r, :LANES].set(a[n][0])
    mod_all = _chip_exchange(jnp.concatenate(mod_sh + [small3]), "mod_gather", True)
    mods = []
    for l in range(2):
        full = mod_all[:, 8 * l:8 * l + 8].transpose(1, 0, 2).reshape(8, MOD)
        mods.append(jnp.split(lax.dynamic_slice_in_dim(full, dev, 1, axis=0), 6, axis=1))
    b_glu, ln_gain, ln_bias = (mod_all[:, 16 + r, :LANES].reshape(1, MIX_HALF) for r in range(3))

    local = dict(a, even_w_in=jnp.pad(a["even_w_in"], ((0, 0), (0, 0), (0, PACK_COLS - EVEN_SHARD))),
                 mlp_w1_0=a["mlp_w1"][0:1], mlp_w1_1=a["mlp_w1"][1:2], mlp_w2_0=a["mlp_w2"][0:1], mlp_w2_1=a["mlp_w2"][1:2])
    (pieces0, rows0), (pieces1, rows1) = SHARDED
    w = _gather_finish(_chip_exchange(_gather_prep(local, pieces0, rows0), "w0_chips", True), pieces0, rows0, "w0")
    w_even = _pack_even(w["even_w_in"][0].reshape(D_MODEL, N_CHIPS, PACK_COLS)[:, :, :EVEN_SHARD].reshape(D_MODEL, EVEN_COLS))
    w_lr_pad = jnp.zeros((MIX_HALF, MIX_HALF), BF16).at[:GLA_RANK].set(w["gla_w_lr"][0])
    gla_b_lr = a["gla_b_lr"]
    gla_gain, q_gain, k_gain = (a[n].reshape(1, MIX_HALF) for n in ("gla_gain", "fox_q_gain", "fox_k_gain"))
    sgu_wm, sgu_bt = _sgu_tables(a["sgu_w_s"][0], a["sgu_b_s"][0])

    sh1, sc1, g1, sh2, sc2, g2 = mods[0]
    _, h1_0 = _res_rms(x0, sc1, sh1, "l0_norm1")
    proj0 = _mm(h1_0, w_even, "nn", "l0_proj")
    og, gla_res = _gla_block_fwd(proj0, w_lr_pad, gla_b_lr, gla_gain, bavg, "gla")
    of, fox_res, (collected1,) = _fox_block_fwd(proj0, a["fox_b_f"][0], q_gain, k_gain, bavg, "fox",
                                                carry=(_gather_prep(local, pieces1, rows1), True))
    w.update(_gather_finish(collected1, pieces1, rows1, "w1"))
    s5w = dict(lam_re=a["s5_lam_re"][0], lam_im=a["s5_lam_im"][0], log_dt=a["s5_log_dt"][0], b_re=a["s5_b_re"][0], b_im=a["s5_b_im"][0],
               c_re=a["s5_c_re"][0], c_im=a["s5_c_im"][0], d=a["s5_d"][0], w_glu=w["s5_w_glu"][0], b_glu=b_glu)
    mixed0 = jnp.concatenate([og, of], axis=1).astype(BF16)
    y0 = _mm(mixed0, w["even_w_out"][0], "nn", "l0_out", out_dtype=BF16)
    x1, h2_0 = _res_rms(x0, sc2, sh2, "l0_norm2", y=y0, g=g1)
    pre0, m0 = _mlp_fwd(h2_0, w["mlp_w1_0"][0], w["mlp_w2_0"][0], "l0_mlp")
    sh1b, sc1b, g1b, sh2b, sc2b, g2b = mods[1]
    x2, h1_1 = _res_rms(x1, sc1b, sh1b, "l1_norm1", y=m0, g=g2)
    proj1 = _mm(h1_1, w["odd_w_in"][0], "nn", "l1_proj")
    ys5, s5_res = _s5_block_fwd(proj1[:, :MIX_HALF], s5w, "s5")
    ysgu = _sgu_fwd(proj1, ln_gain, ln_bias, sgu_wm, sgu_bt, "sgu")
    mixed1 = jnp.concatenate([ys5, ysgu], axis=1).astype(BF16)
    y1 = _mm(mixed1, w["odd_w_out"][0], "nn", "l1_out", out_dtype=BF16)
    x3, h2_1 = _res_rms(x2, sc2b, sh2b, "l1_norm2", y=y1, g=g1b)
    pre1, m1 = _mlp_fwd(h2_1, w["mlp_w1_1"][0], w["mlp_w2_1"][0], "l1_mlp")
    loss_b, dx4, dm1, dg2b = _res_loss(x3, m1, g2b, target, "loss")
    loss = lax.psum(loss_b[0, 0], ("x", "y", "c"))

    full = {}
    dh2_1, dw1_1, dw2_1 = _mlp_bwd(dm1, h2_1, pre1, w["mlp_w1_1"][0], w["mlp_w2_1"][0], "l1_mlp")
    dx3, dy1, dg1b, dsc2b, dsh2b = _res_rms_bwd(x3, dh2_1, sc2b, dx4, "l1_norm2_bwd", y=y1, g=g1b)
    dmixed1 = _mm(dy1, w["odd_w_out"][0], "nt", "l1_out_dx")
    full["odd_w_out"] = _mm(mixed1, dy1, "tn", "l1_out_dw")[None]
    du, s5g = _s5_block_bwd(dmixed1[:, :MIX_HALF], s5w, s5_res, "s5")
    dzu, dzv, dws, dbt, dlg, dlb = _sgu_bwd(dmixed1[:, MIX_HALF:], proj1, ln_gain, ln_bias, sgu_wm, sgu_bt, "sgu_bwd")
    g_ws, g_bs = _sgu_grads(dws, dbt)
    dproj1 = jnp.concatenate([du, dzu, dzv], axis=1).astype(BF16)
    full["odd_w_in"] = _mm(h1_1, dproj1, "tn", "l1_proj_dw")[None]
    dh1_1 = _mm(dproj1, w["odd_w_in"][0], "nt", "l1_proj_dx", out_dtype=BF16)
    dx2, dm0, dg2, dsc1b, dsh1b = _res_rms_bwd(x2, dh1_1, sc1b, dx3, "l1_norm1_bwd", y=m0, g=g2)
    dh2_0, dw1_0, dw2_0 = _mlp_bwd(dm0, h2_0, pre0, w["mlp_w1_0"][0], w["mlp_w2_0"][0], "l0_mlp")
    full.update(mlp_w1_0=dw1_0[None], mlp_w2_0=dw2_0[None], mlp_w1_1=dw1_1[None], mlp_w2_1=dw2_1[None], s5_w_glu=s5g["w_glu"][None],
                s5_b_glu=s5g["b_glu"][None], sgu_ln_gain=dlg, sgu_ln_bias=dlb)
    pair_sums1 = _reduce_prep(full, pieces1, rows1, "g1")
    dx1, dy0, dg1, dsc2, dsh2 = _res_rms_bwd(x1, dh2_0, sc2, dx2, "l0_norm2_bwd", y=y0, g=g1)
    dmixed0 = _mm(dy0, w["even_w_out"][0], "nt", "l0_out_dx")
    full["even_w_out"] = _mm(mixed0, dy0, "tn", "l0_out_dw")[None]
    (dgq, dgk, dgv, dgg, dsmall), glag = _gla_block_bwd((dmixed0, MIX_HALF, 0), proj0, w_lr_pad, gla_gain, bavg, gla_res, "gla")
    (dfq, dfk, dfv, dff), foxg, (arrived1,) = _fox_block_bwd((dmixed0, MIX_HALF, 1), proj0, q_gain, k_gain, bavg, fox_res, "fox",
                                                           carry=(pair_sums1, False))
    dsmall = lax.dynamic_update_slice(dsmall, dff, (0, GLA_RANK))
    dproj0 = jnp.concatenate([dgq, dgk, dgv, dgg, dfq, dfk, dfv, dsmall], axis=1).astype(BF16)
    d_even = _unpack_even(_mm(h1_0, dproj0, "tn", "l0_proj_dw")).reshape(D_MODEL, N_CHIPS, EVEN_SHARD)
    full["even_w_in"] = jnp.pad(d_even, ((0, 0), (0, 0), (0, PACK_COLS - EVEN_SHARD))).reshape(1, D_MODEL, N_CHIPS * PACK_COLS)
    dh1_0 = _mm(dproj0, w_even, "nt", "l0_proj_dx", out_dtype=BF16)
    grad_x, dsc1, dsh1 = _res_rms_bwd(x0, dh1_0, sc1, dx1, "l0_norm1_bwd")
    full["gla_w_lr"] = glag["w_lr"][None]

    dmod = jnp.concatenate([dsh1, dsc1, dg1, dsh2, dsc2, dg2, dsh1b, dsc1b, dg1b, dsh2b, dsc2b, dg2b], axis=1)
    dmod_all = _gather8(jnp.pad(dmod, ((0, 7), (0, 0))), "dmod_gather")[:, :, 0, :].reshape(2 * N_CHIPS, 2, MOD)
    grads = {}
    grads["ada_w"] = jnp.stack([
        _mm(c_all, lax.dynamic_slice_in_dim(dmod_all[:, l], chip * MOD_SHARD, MOD_SHARD, axis=1), "tn", f"ada_dw{l}", a_pro=_silu)
        for l in range(2)])
    grads["ada_b"] = _sum_slots(dmod_all.reshape(2 * N_CHIPS, 2 * MOD // MIX_HALF, MIX_HALF), "ada_db").reshape(2, MOD)

    grads.update(_reduce_finish(arrived1, pieces1, rows1, "g1"))
    grads.update(_reduce_finish(_chip_exchange(_reduce_prep(full, pieces0, rows0, "g0"), "g0_chips", False), pieces0, rows0, "g0"))
    grads["even_w_in"] = grads["even_w_in"][:, :, :EVEN_SHARD]
    grads["mlp_w1"] = jnp.concatenate([grads.pop("mlp_w1_0"), grads.pop("mlp_w1_1")])
    grads["mlp_w2"] = jnp.concatenate([grads.pop("mlp_w2_0"), grads.pop("mlp_w2_1")])

    part = dict(gla_b_lr=glag["b_lr"], gla_gain=glag["gain"], fox_b_f=foxg["b_f"], fox_q_gain=foxg["q_gain"], fox_k_gain=foxg["k_gain"],
                s5_lam_re=s5g["lam_re"], s5_lam_im=s5g["lam_im"], s5_log_dt=s5g["log_dt"], s5_b_re=s5g["b_re"], s5_b_im=s5g["b_im"],
                s5_c_re=s5g["c_re"], s5_c_im=s5g["c_im"], s5_d=s5g["d"], sgu_w_s=g_ws, sgu_b_s=g_bs)
    parts_all = _gather8(_pack_rows([part[n] for n, _ in REPLICATED], SMALL_ROWS).astype(BF16), "rep_gather")
    rep = _sum_slots(parts_all.reshape(2 * N_CHIPS, SMALL_ROWS, PACK_COLS), "rep_sum")
    grads.update(_unpack(rep, REPLICATED))

    delta, new_m, new_v = {}, {}, {}
    for n, shape2 in BIG_ADAM.items():
        d, nm, nv = _adamw(a[n].reshape(shape2), grads[n].reshape(shape2), a["m_" + n].reshape(shape2), a["v_" + n].reshape(shape2), "adamw_" + n)
        delta[n], new_m[n], new_v[n] = (t.reshape(a[n].shape) for t in (d, nm, nv))
    small = [n for n in WEIGHTS if n not in BIG_ADAM]
    spec = [(n, a[n].shape) for n in small]
    packs = [_pack_rows([src[n] for n in small], SMALL_ROWS) for src in
             (a, grads, {n: a["m_" + n] for n in small}, {n: a["v_" + n] for n in small})]
    for tgt, res in zip((delta, new_m, new_v), _adamw(*packs, "adamw_small")):
        tgt.update(_unpack(res, spec))
    outs = [loss, grad_x[None]]
    for group in (grads, delta, new_m, new_v):
        outs += [group[n].reshape(a[n].shape) for n in WEIGHTS]
    return tuple(outs)


def kernel(x, c, ada_w, ada_b, even_w_in, even_w_out, gla_w_lr, gla_b_lr, gla_gain, fox_b_f, fox_q_gain, fox_k_gain, odd_w_in,
           odd_w_out, s5_lam_re, s5_lam_im, s5_log_dt, s5_b_re, s5_b_im, s5_c_re, s5_c_im, s5_d, s5_w_glu, s5_b_glu, sgu_ln_gain,
           sgu_ln_bias, sgu_w_s, sgu_b_s, mlp_w1, mlp_w2, loss_target, m_ada_w, m_ada_b, m_even_w_in, m_even_w_out, m_gla_w_lr,
           m_gla_b_lr, m_gla_gain, m_fox_b_f, m_fox_q_gain, m_fox_k_gain, m_odd_w_in, m_odd_w_out, m_s5_lam_re, m_s5_lam_im,
           m_s5_log_dt, m_s5_b_re, m_s5_b_im, m_s5_c_re, m_s5_c_im, m_s5_d, m_s5_w_glu, m_s5_b_glu, m_sgu_ln_gain, m_sgu_ln_bias,
           m_sgu_w_s, m_sgu_b_s, m_mlp_w1, m_mlp_w2, v_ada_w, v_ada_b, v_even_w_in, v_even_w_out, v_gla_w_lr, v_gla_b_lr,
           v_gla_gain, v_fox_b_f, v_fox_q_gain, v_fox_k_gain, v_odd_w_in, v_odd_w_out, v_s5_lam_re, v_s5_lam_im, v_s5_log_dt,
           v_s5_b_re, v_s5_b_im, v_s5_c_re, v_s5_c_im, v_s5_d, v_s5_w_glu, v_s5_b_glu, v_sgu_ln_gain, v_sgu_ln_bias, v_sgu_w_s,
           v_sgu_b_s, v_mlp_w1, v_mlp_w2):
    return _step((x, c, ada_w, ada_b, even_w_in, even_w_out, gla_w_lr, gla_b_lr, gla_gain, fox_b_f, fox_q_gain, fox_k_gain,
                  odd_w_in, odd_w_out, s5_lam_re, s5_lam_im, s5_log_dt, s5_b_re, s5_b_im, s5_c_re, s5_c_im, s5_d, s5_w_glu,
                  s5_b_glu, sgu_ln_gain, sgu_ln_bias, sgu_w_s, sgu_b_s, mlp_w1, mlp_w2, loss_target, m_ada_w, m_ada_b,
                  m_even_w_in, m_even_w_out, m_gla_w_lr, m_gla_b_lr, m_gla_gain, m_fox_b_f, m_fox_q_gain, m_fox_k_gain,
                  m_odd_w_in, m_odd_w_out, m_s5_lam_re, m_s5_lam_im, m_s5_log_dt, m_s5_b_re, m_s5_b_im, m_s5_c_re, m_s5_c_im,
                  m_s5_d, m_s5_w_glu, m_s5_b_glu, m_sgu_ln_gain, m_sgu_ln_bias, m_sgu_w_s, m_sgu_b_s, m_mlp_w1, m_mlp_w2, v_ada_w,
                  v_ada_b, v_even_w_in, v_even_w_out, v_gla_w_lr, v_gla_b_lr, v_gla_gain, v_fox_b_f, v_fox_q_gain, v_fox_k_gain,
                  v_odd_w_in, v_odd_w_out, v_s5_lam_re, v_s5_lam_im, v_s5_log_dt, v_s5_b_re, v_s5_b_im, v_s5_c_re, v_s5_c_im,
                  v_s5_d, v_s5_w_glu, v_s5_b_glu, v_sgu_ln_gain, v_sgu_ln_bias, v_sgu_w_s, v_sgu_b_s, v_mlp_w1, v_mlp_w2))
```

```python
import functools
import math

import jax
import jax.numpy as jnp
import numpy as np
from jax import lax
from jax.experimental import pallas as pl
from jax.experimental.pallas import tpu as pltpu

F32 = jnp.float32
BF16 = jnp.bfloat16
MESH = pl.DeviceIdType.MESH
ANY = pl.BlockSpec(memory_space=pl.ANY)
DMA_SEM = pltpu.SemaphoreType.DMA

D_MODEL = 1024
HEAD_DIM = 64
MIX_HALF = 512
GLA_RANK = 16
GLA_TAU = 16.0
GLA_CHUNK = 64
S5_GROUPS = 32
S5_GROUP_WIDTH = 16
S5_STATE = 64
S5_N = S5_GROUPS * S5_STATE
SGU_GROUPS = 8
SGU_CHUNK = 128
D_FF = 4096
EPS = 1e-6
N_CHIPS = 4
LANES = 128
VMEM_LIMIT = 48 * 1024 * 1024
PAIR_COPIES = 16

ADAM_LR = 0.001
ADAM_B1 = 0.9
ADAM_B2 = 0.999
ADAM_EPS = 1e-08
ADAM_WD = 0.01
ADAM_STEP = 10


def _cparams(*sem):
    return pltpu.CompilerParams(dimension_semantics=sem, vmem_limit_bytes=VMEM_LIMIT)


def _pair_swap(x, name):
    lead = x.shape[:-2]
    rows = x.shape[-2]
    nsplit = max(1, PAIR_COPIES // max(1, math.prod(lead)))
    while nsplit > 1 and rows % (nsplit * 16):
        nsplit -= 1
    pieces = [idx + (pl.ds(j * (rows // nsplit), rows // nsplit),) for idx in np.ndindex(*lead) for j in range(nsplit)]

    def body(x_ref, o_ref, send_sems, recv_sems):
        mx, my, mc = lax.axis_index("x"), lax.axis_index("y"), lax.axis_index("c")
        copies = [pltpu.make_async_remote_copy(src_ref=x_ref.at[p], dst_ref=o_ref.at[p], send_sem=send_sems.at[j], recv_sem=recv_sems.at[j],
                                               device_id=(mx, my, 1 - mc), device_id_type=MESH) for j, p in enumerate(pieces)]
        for cp in copies:
            cp.start()
        for cp in copies:
            cp.wait_recv()
        for cp in copies:
            cp.wait_send()

    return pl.pallas_call(
        body, name=name, out_shape=jax.ShapeDtypeStruct(x.shape, x.dtype), in_specs=[ANY], out_specs=ANY,
        scratch_shapes=[DMA_SEM((len(pieces),)), DMA_SEM((len(pieces),))])(x)


def _by_core(mine, theirs):
    first = lax.axis_index("c") == 0
    return jnp.stack([jnp.where(first, mine, theirs), jnp.where(first, theirs, mine)])


def _chip_exchange(x, name, bcast):
    blk = x.shape if bcast else x.shape[1:]

    def body(x_ref, o_ref, send_sems, recv_sems, loc_sem):
        start, finish = _exchange_plan(x_ref, o_ref, send_sems, recv_sems, loc_sem, bcast)
        start()
        finish()

    return pl.pallas_call(
        body, name=name, out_shape=jax.ShapeDtypeStruct((N_CHIPS,) + tuple(blk), x.dtype), in_specs=[ANY], out_specs=ANY,
        scratch_shapes=_EXCHANGE_SEMS)(x)


_EXCHANGE_SEMS = [DMA_SEM((3,)), DMA_SEM((3,)), DMA_SEM]


def _exchange_plan(x_ref, o_ref, send_sems, recv_sems, loc_sem, bcast):
    mx, my, mc = lax.axis_index("x"), lax.axis_index("y"), lax.axis_index("c")
    me = 2 * mx + my
    peers = [(1 - mx, my), (mx, 1 - my), (1 - mx, 1 - my)]

    def src(k):
        return x_ref if bcast else x_ref.at[k]

    def remote(j, source, slot):
        px, py = peers[j]
        return pltpu.make_async_remote_copy(src_ref=source, dst_ref=o_ref.at[slot], send_sem=send_sems.at[j], recv_sem=recv_sems.at[j],
                                            device_id=(px, py, mc), device_id_type=MESH)

    loc = pltpu.make_async_copy(src(me), o_ref.at[me], loc_sem)
    sends = [remote(j, src(2 * px + py), me) for j, (px, py) in enumerate(peers)]
    arrivals = [remote(j, src(me), 2 * px + py) for j, (px, py) in enumerate(peers)]

    def start():
        loc.start()
        for cp in sends:
            cp.start()

    def finish():
        for cp in arrivals:
            cp.wait_recv()
        for cp in sends:
            cp.wait_send()
        loc.wait()

    return start, finish


def _gather8(x, name):
    collected = _chip_exchange(x, name + "_chips", True)
    return jnp.swapaxes(_by_core(collected, _pair_swap(collected, name + "_pair")), 0, 1)


def _tile(n, want):
    if n <= want:
        return n
    t = (want // LANES) * LANES
    while t >= LANES:
        if n % t == 0:
            return t
        t -= LANES
    raise ValueError(f"no lane-aligned tile for {n}")


_DIMS = {"nn": (((1,), (0,)), ((), ())), "nt": (((1,), (1,)), ((), ())), "tn": (((0,), (0,)), ((), ()))}


MM_FULL_K = 4096
MM_SLAB_K = 2048
MM_TILES = ((1024, 1024), (512, 1024), (1024, 512), (512, 512), (256, 512), (256, 256))
MM_VMEM_BUDGET = 36 * 1024 * 1024


def _mm(a, b, mode, name, *, a_pro=None, epi=None, extras=(), out_dtype=F32, tm_max=1024, tn_max=1024, tk=None, a_cols=None):
    c0, csize = a_cols if a_cols is not None else (0, a.shape[1])
    if mode == "tn":
        K, M = a.shape[0], csize
    else:
        M, K = a.shape[0], csize
    N = b.shape[0] if mode == "nt" else b.shape[1]
    assert (b.shape[1] if mode == "nt" else b.shape[0]) == K, (a.shape, b.shape, mode)
    if tk is None:
        tk = K if (mode != "tn" and K <= MM_FULL_K) else MM_SLAB_K
    tk = _tile(K, tk)
    nk = K // tk
    n_mn = sum(1 for _, kind in extras if kind == "mn")
    for tm_want, tn_want in MM_TILES:
        tm, tn = _tile(M, min(tm_want, tm_max)), _tile(N, min(tn_want, tn_max))
        need = 2 * (tm * tk * a.dtype.itemsize + tk * tn * b.dtype.itemsize + tm * tn * 4 * (1 + n_mn)) + tm * tn * 4 * (nk > 1)
        if need <= MM_VMEM_BUDGET:
            break
    if mode == "tn":
        assert c0 % tm == 0
        a_spec = pl.BlockSpec((tk, tm), lambda i, j, k: (k, i + c0 // tm))
    else:
        assert c0 % tk == 0
        a_spec = pl.BlockSpec((tm, tk), lambda i, j, k: (i, k + c0 // tk))
    b_spec = pl.BlockSpec((tn, tk), lambda i, j, k: (j, k)) if mode == "nt" else pl.BlockSpec((tk, tn), lambda i, j, k: (k, j))
    ex_specs = []
    for arr, kind in extras:
        if kind == "mn":
            assert arr.shape == (M, N)
            ex_specs.append(pl.BlockSpec((tm, tn), lambda i, j, k: (i, j)))
        else:
            assert arr.shape == (1, N)
            ex_specs.append(pl.BlockSpec((1, tn), lambda i, j, k: (0, j)))
    n_ex = len(extras)

    def body(*refs):
        a_ref, b_ref = refs[:2]
        ex_refs = refs[2:2 + n_ex]
        o_ref = refs[2 + n_ex]
        acc_ref = refs[3 + n_ex] if nk > 1 else None
        k = pl.program_id(2)
        av = a_ref[...]
        if a_pro is not None:
            av = a_pro(av)
        part = lax.dot_general(av.astype(BF16), b_ref[...].astype(BF16), _DIMS[mode], preferred_element_type=F32)
        if nk == 1:
            if epi is not None:
                part = epi(part, *[r[...] for r in ex_refs])
            o_ref[...] = part.astype(o_ref.dtype)
            return

        @pl.when(k == 0)
        def _():
            acc_ref[...] = part

        @pl.when(k > 0)
        def _():
            acc_ref[...] += part

        @pl.when(k == nk - 1)
        def _():
            acc = acc_ref[...]
            if epi is not None:
                acc = epi(acc, *[r[...] for r in ex_refs])
            o_ref[...] = acc.astype(o_ref.dtype)

    return pl.pallas_call(
        body, name=name, grid=(M // tm, N // tn, nk),
        in_specs=[a_spec, b_spec] + ex_specs,
        out_specs=pl.BlockSpec((tm, tn), lambda i, j, k: (i, j)),
        out_shape=jax.ShapeDtypeStruct((M, N), out_dtype),
        scratch_shapes=[pltpu.VMEM((tm, tn), F32)] if nk > 1 else [],
        compiler_params=_cparams("parallel", "parallel", "arbitrary"))(a, b, *[e[0] for e in extras])


ROWS = 512


def _row_spec(w, ts=ROWS):
    return pl.BlockSpec((ts, w), lambda i: (i, 0))


def _vec_spec(w):
    return pl.BlockSpec((1, w), lambda i: (0, 0))


def _res_rms(x, sc, sh, name, y=None, g=None):
    S, D = x.shape
    has_res = y is not None

    def body(*refs):
        if has_res:
            x_ref, y_ref, g_ref, sc_ref, sh_ref, xo_ref, h_ref = refs
            xv = x_ref[...] + g_ref[...] * y_ref[...]
            xo_ref[...] = xv
        else:
            x_ref, sc_ref, sh_ref, h_ref = refs
            xv = x_ref[...]
        r = lax.rsqrt(jnp.mean(xv * xv, axis=-1, keepdims=True) + EPS)
        h_ref[...] = (xv * r * (1.0 + sc_ref[...]) + sh_ref[...]).astype(BF16)

    row, vec = _row_spec(D), _vec_spec(D)
    if has_res:
        return pl.pallas_call(body, name=name, grid=(S // ROWS,), in_specs=[row, row, vec, vec, vec], out_specs=[row, row],
                              out_shape=[jax.ShapeDtypeStruct((S, D), F32), jax.ShapeDtypeStruct((S, D), BF16)],
                              compiler_params=_cparams("parallel"))(x, y, g, sc, sh)
    h = pl.pallas_call(body, name=name, grid=(S // ROWS,), in_specs=[row, vec, vec], out_specs=row,
                       out_shape=jax.ShapeDtypeStruct((S, D), BF16), compiler_params=_cparams("parallel"))(x, sc, sh)
    return x, h


def _res_rms_bwd(x, dh, sc, dres, name, y=None, g=None):
    S, D = x.shape
    has_res = y is not None

    def body(*refs):
        if has_res:
            x_ref, dh_ref, sc_ref, dres_ref, y_ref, g_ref, dx_ref, dy_ref, dg_ref, dsc_ref, dsh_ref = refs
        else:
            x_ref, dh_ref, sc_ref, dres_ref, dx_ref, dsc_ref, dsh_ref = refs
        first = pl.program_id(0) == 0
        xv = x_ref[...]
        dh = dh_ref[...].astype(F32)
        r = lax.rsqrt(jnp.mean(xv * xv, axis=-1, keepdims=True) + EPS)
        xn = xv * r
        dxn = dh * (1.0 + sc_ref[...])
        dx = dres_ref[...] + r * (dxn - xn * jnp.mean(dxn * xn, axis=-1, keepdims=True))
        dx_ref[...] = dx
        parts = [(dsc_ref, jnp.sum(dh * xn, axis=0, keepdims=True)), (dsh_ref, jnp.sum(dh, axis=0, keepdims=True))]
        if has_res:
            dy_ref[...] = (dx * g_ref[...]).astype(BF16)
            parts.append((dg_ref, jnp.sum(dx * y_ref[...], axis=0, keepdims=True)))
        for ref, val in parts:
            @pl.when(first)
            def _(ref=ref, val=val):
                ref[...] = val

            @pl.when(jnp.logical_not(first))
            def _(ref=ref, val=val):
                ref[...] += val

    row, vec = _row_spec(D), _vec_spec(D)
    full = jax.ShapeDtypeStruct((S, D), F32)
    v = jax.ShapeDtypeStruct((1, D), F32)
    if has_res:
        return pl.pallas_call(body, name=name, grid=(S // ROWS,), in_specs=[row, row, vec, row, row, vec],
                              out_specs=[row, row, vec, vec, vec], out_shape=[full, jax.ShapeDtypeStruct((S, D), BF16), v, v, v],
                              compiler_params=_cparams("arbitrary"))(x, dh, sc, dres, y, g)
    return pl.pallas_call(body, name=name, grid=(S // ROWS,), in_specs=[row, row, vec, row],
                          out_specs=[row, vec, vec], out_shape=[full, v, v],
                          compiler_params=_cparams("arbitrary"))(x, dh, sc, dres)


def _res_loss(x, m, g, target, name):
    S, D = x.shape

    def body(x_ref, m_ref, g_ref, t_ref, loss_ref, dx_ref, dm_ref, dg_ref):
        first = pl.program_id(0) == 0
        mv = m_ref[...].astype(F32)
        err = x_ref[...] + g_ref[...] * mv - t_ref[...]
        dx = err * (1.0 / D)
        dx_ref[...] = dx
        dm_ref[...] = (dx * g_ref[...]).astype(BF16)
        part = 0.5 * jnp.sum(jnp.mean(err * err, axis=-1, keepdims=True), axis=0, keepdims=True)
        dg = jnp.sum(dx * mv, axis=0, keepdims=True)

        @pl.when(first)
        def _():
            loss_ref[...] = jnp.broadcast_to(part, loss_ref.shape)
            dg_ref[...] = dg

        @pl.when(jnp.logical_not(first))
        def _():
            loss_ref[...] += jnp.broadcast_to(part, loss_ref.shape)
            dg_ref[...] += dg

    row, vec = _row_spec(D), _vec_spec(D)
    full = jax.ShapeDtypeStruct((S, D), F32)
    return pl.pallas_call(body, name=name, grid=(S // ROWS,), in_specs=[row, row, vec, row],
                          out_specs=[pl.BlockSpec((8, LANES), lambda i: (0, 0)), row, row, vec],
                          out_shape=[jax.ShapeDtypeStruct((8, LANES), F32), full, jax.ShapeDtypeStruct((S, D), BF16), jax.ShapeDtypeStruct((1, D), F32)],
                          compiler_params=_cparams("arbitrary"))(x, m, g, target)


def _adamw(w, g, m, v, name):
    R, C = w.shape
    tr = R if R <= 256 else 256
    assert R % tr == 0

    def body(w_ref, g_ref, m_ref, v_ref, d_ref, nm_ref, nv_ref):
        gv = g_ref[...]
        nm = ADAM_B1 * m_ref[...] + (1.0 - ADAM_B1) * gv
        nv = ADAM_B2 * v_ref[...] + (1.0 - ADAM_B2) * jnp.square(gv)
        m_hat = nm / (1.0 - ADAM_B1 ** ADAM_STEP)
        v_hat = nv / (1.0 - ADAM_B2 ** ADAM_STEP)
        d_ref[...] = -ADAM_LR * (m_hat / (jnp.sqrt(v_hat) + ADAM_EPS) + ADAM_WD * w_ref[...])
        nm_ref[...] = nm
        nv_ref[...] = nv

    spec = pl.BlockSpec((tr, C), lambda i: (i, 0))
    out = jax.ShapeDtypeStruct((R, C), F32)
    return pl.pallas_call(body, name=name, grid=(R // tr,), in_specs=[spec] * 4, out_specs=[spec] * 3,
                          out_shape=[out, out, out], compiler_params=_cparams("parallel"))(w, g, m, v)


def _sum_slots(x, name):
    n, R, C = x.shape
    tr = R if R <= 256 else 256
    assert R % tr == 0

    def body(x_ref, o_ref):
        acc = x_ref[0].astype(F32)
        for j in range(1, n):
            acc = acc + x_ref[j].astype(F32)
        o_ref[...] = acc

    return pl.pallas_call(body, name=name, grid=(R // tr,), in_specs=[pl.BlockSpec((n, tr, C), lambda i: (0, i, 0))],
                          out_specs=pl.BlockSpec((tr, C), lambda i: (i, 0)), out_shape=jax.ShapeDtypeStruct((R, C), F32),
                          compiler_params=_cparams("parallel"))(x)


EW_ROWS = 1024


def _ew(fn, name, tiled, consts=(), outs=(), sums=(), ts=EW_ROWS):
    tiled = [t if isinstance(t, tuple) else (t, t.shape[1], 0) for t in tiled]
    S = tiled[0][0].shape[0]
    ts = min(ts, S)
    assert S % ts == 0, (name, S, ts)
    n_t, n_c, n_o, n_s = len(tiled), len(consts), len(outs), len(sums)

    def body(*refs):
        ins = [r[...] for r in refs[:n_t + n_c]]
        res = fn(*ins)
        res = res if isinstance(res, (tuple, list)) else (res,)
        assert len(res) == n_o + n_s
        o_refs = refs[n_t + n_c:]
        for r, val in zip(o_refs[:n_o], res[:n_o]):
            r[...] = val.astype(r.dtype)
        first = pl.program_id(0) == 0
        for r, val in zip(o_refs[n_o:], res[n_o:]):
            @pl.when(first)
            def _(r=r, val=val):
                r[...] = val

            @pl.when(jnp.logical_not(first))
            def _(r=r, val=val):
                r[...] += val

    in_specs = [pl.BlockSpec((ts, w), lambda i, cb=cb: (i, cb)) for _, w, cb in tiled]
    in_specs += [pl.BlockSpec(c.shape, lambda i, nd=c.ndim: (0,) * nd) for c in consts]
    out_specs = [_row_spec(w, ts) for w, _ in outs] + [_vec_spec(w) for w in sums]
    out_shape = [jax.ShapeDtypeStruct((S, w), dt) for w, dt in outs] + [jax.ShapeDtypeStruct((1, w), F32) for w in sums]
    res = pl.pallas_call(body, name=name, grid=(S // ts,), in_specs=in_specs, out_specs=out_specs, out_shape=out_shape,
                         compiler_params=_cparams("arbitrary" if sums else "parallel"))(*[t[0] for t in tiled], *consts)
    return res


_GELU_C = math.sqrt(2.0 / math.pi)


def _gelu(x):
    return 0.5 * x * (1.0 + jnp.tanh(_GELU_C * (x + 0.044715 * x * x * x)))


def _dgelu(x):
    t = jnp.tanh(_GELU_C * (x + 0.044715 * x * x * x))
    return 0.5 * (1.0 + t) + 0.5 * x * (1.0 - t * t) * _GELU_C * (1.0 + 3.0 * 0.044715 * x * x)


def _sigmoid(x):
    return 1.0 / (1.0 + jnp.exp(-x))


def _log_sigmoid(x):
    return jnp.minimum(x, 0.0) - jnp.log(1.0 + jnp.exp(-jnp.abs(x)))


SCAN_T = 128
SCAN_TB = 2048


def _cmul(ar, ai, br, bi):
    return ar * br - ai * bi, ar * bi + ai * br


def _s5_discretise(lam_re, lam_im, log_dt, b_re, b_im):
    dt = jnp.exp(log_dt)[:, None]
    mag = jnp.exp(lam_re * dt)
    ang = lam_im * dt
    abar_re = mag * jnp.cos(ang)
    abar_im = mag * jnp.sin(ang)
    den = lam_re * lam_re + lam_im * lam_im
    coef_re = ((abar_re - 1.0) * lam_re + abar_im * lam_im) / den
    coef_im = (abar_im * lam_re - (abar_re - 1.0) * lam_im) / den
    bbar_re = coef_re[..., None] * b_re - coef_im[..., None] * b_im
    bbar_im = coef_re[..., None] * b_im + coef_im[..., None] * b_re
    return abar_re, abar_im, bbar_re, bbar_im


def _planes(re, im):
    lead = re.shape[:-1]
    return jnp.stack([re.reshape(lead + (-1, LANES)), im.reshape(lead + (-1, LANES))], axis=-2).reshape(lead + (-1,))


def _unplanes(x):
    lead = x.shape[:-1]
    x4 = x.reshape(lead + (-1, 2, LANES))
    return x4[..., 0, :].reshape(lead + (-1,)), x4[..., 1, :].reshape(lead + (-1,))


def _s5_scan_tables(a_re, a_im, reverse):
    pr, pi = [a_re], [a_im]
    for _ in range(7):
        r, i = _cmul(pr[-1], pi[-1], pr[-1], pi[-1])
        pr.append(r)
        pi.append(i)
    apow = _planes(jnp.stack(pr), jnp.stack(pi))
    n = np.arange(1, SCAN_T + 1)
    if reverse:
        n = n[::-1]
    tr = jnp.ones((SCAN_T, a_re.shape[0]), F32)
    ti = jnp.zeros((SCAN_T, a_re.shape[0]), F32)
    for k in range(8):
        bit = jnp.asarray(((n >> k) & 1).astype(np.float32))[:, None]
        mr = bit * pr[k][None, :] + (1.0 - bit)
        mi = bit * pi[k][None, :]
        tr, ti = _cmul(tr, ti, mr, mi)
    return apow, _planes(tr, ti)


def _s5_scan(bu, apow, ptab, name, reverse, x_fwd=None):
    S, N2 = bu.shape
    T, W = SCAN_T, 2 * LANES
    tb = min(SCAN_TB, S)
    nt, nsub = S // tb, tb // T
    order = list(range(nsub - 1, -1, -1) if reverse else range(nsub))
    with_da = x_fwd is not None

    def tblk(t):
        return (nt - 1 - t) if reverse else t

    def shifted(v, k, rowi):
        s = 1 << k
        if reverse:
            return jnp.where(rowi < T - s, pltpu.roll(v, T - s, 0), 0.0)
        return jnp.where(rowi >= s, pltpu.roll(v, s, 0), 0.0)

    def body(*refs):
        if with_da:
            bu_ref, ap_ref, pt_ref, xf_ref, xp_ref, x_ref, da_ref, carry_ref = refs
        else:
            bu_ref, ap_ref, pt_ref, x_ref, carry_ref = refs
        t = pl.program_id(1)

        @pl.when(t == 0)
        def _():
            carry_ref[...] = jnp.zeros_like(carry_ref)
            if with_da:
                da_ref[...] = jnp.zeros_like(da_ref)

        rowi = lax.broadcasted_iota(jnp.int32, (T, LANES), 0)
        pr, pi = pt_ref[:, :LANES], pt_ref[:, LANES:]
        cr, ci = carry_ref[0:1, :LANES], carry_ref[0:1, LANES:]
        for sb in order:
            rows = pl.ds(sb * T, T)
            xr, xi = bu_ref[rows, :LANES], bu_ref[rows, LANES:]
            for k in range(7):
                ar, ai = ap_ref[k:k + 1, :LANES], ap_ref[k:k + 1, LANES:]
                s = 1 << k
                if s < 8:
                    rr, ri = shifted(xr, k, rowi), shifted(xi, k, rowi)
                    xr, xi = xr + ar * rr - ai * ri, xi + ar * ri + ai * rr
                elif reverse:
                    nr, ni = xr[s:], xi[s:]
                    xr = jnp.concatenate([xr[:T - s] + ar * nr - ai * ni, xr[T - s:]], axis=0)
                    xi = jnp.concatenate([xi[:T - s] + ar * ni + ai * nr, xi[T - s:]], axis=0)
                else:
                    nr, ni = xr[:T - s], xi[:T - s]
                    xr = jnp.concatenate([xr[:s], xr[s:] + ar * nr - ai * ni], axis=0)
                    xi = jnp.concatenate([xi[:s], xi[s:] + ar * ni + ai * nr], axis=0)
            xr, xi = xr + pr * cr - pi * ci, xi + pr * ci + pi * cr
            x_ref[rows, :LANES] = xr
            x_ref[rows, LANES:] = xi
            edge = pl.ds(sb * T + (0 if reverse else T - 1), 1)
            cr, ci = x_ref[edge, :LANES], x_ref[edge, LANES:]
            if with_da:
                if sb > 0:
                    before = pl.ds(sb * T - 1, 1)
                    b_r, b_i = xf_ref[before, :LANES], xf_ref[before, LANES:]
                else:
                    keep = (tblk(t) > 0).astype(F32)
                    b_r, b_i = xp_ref[7:8, :LANES] * keep, xp_ref[7:8, LANES:] * keep
                fr, fi = xf_ref[rows, :LANES], xf_ref[rows, LANES:]
                qr = jnp.where(rowi >= 1, pltpu.roll(fr, 1, 0), b_r)
                qi = jnp.where(rowi >= 1, pltpu.roll(fi, 1, 0), b_i)
                gr, gi = xr * qr + xi * qi, xi * qr - xr * qi
                sr, si = gr[0:8], gi[0:8]
                for j in range(1, T // 8):
                    sr, si = sr + gr[8 * j:8 * j + 8], si + gi[8 * j:8 * j + 8]
                da_ref[:, :LANES] += sr
                da_ref[:, LANES:] += si
        carry_ref[0:1, :LANES] = cr
        carry_ref[0:1, LANES:] = ci

    blk = pl.BlockSpec((tb, W), lambda j, t: (tblk(t), j))
    in_specs = [blk, pl.BlockSpec((8, W), lambda j, t: (0, j)), pl.BlockSpec((T, W), lambda j, t: (0, j))]
    out_specs, out_shape = [blk], [jax.ShapeDtypeStruct((S, N2), F32)]
    args = [bu, apow, ptab]
    if with_da:
        in_specs += [blk, pl.BlockSpec((8, W), lambda j, t: (jnp.maximum(tblk(t) * (tb // 8) - 1, 0), j))]
        out_specs.append(pl.BlockSpec((8, W), lambda j, t: (0, j)))
        out_shape.append(jax.ShapeDtypeStruct((8, N2), F32))
        args += [x_fwd, x_fwd]
    res = pl.pallas_call(body, name=name, grid=(N2 // W, nt), in_specs=in_specs, out_specs=out_specs, out_shape=out_shape,
                         scratch_shapes=[pltpu.VMEM((8, W), F32)], compiler_params=_cparams("parallel", "arbitrary"))(*args)
    return res if with_da else res[0]


S5_BAND = 4


def _mm_band(a, b, name, *, b_t=False, outer=False, epi=None, extras=(), tm=512, tk=2048):
    S = a.shape[0]
    wa = a.shape[1] // S5_BAND
    if outer:
        wb = b.shape[1] // S5_BAND
        tk = _tile(S, tk)
        nk = S // tk

        def obody(a_ref, b_ref, o_ref, acc_ref):
            k = pl.program_id(1)
            part = lax.dot_general(a_ref[...].astype(BF16), b_ref[...].astype(BF16), TN_DIMS, preferred_element_type=F32)

            @pl.when(k == 0)
            def _():
                acc_ref[...] = part

            @pl.when(k > 0)
            def _():
                acc_ref[...] += part

            @pl.when(k == nk - 1)
            def _():
                o_ref[...] = acc_ref[...]

        return pl.pallas_call(
            obody, name=name, grid=(S5_BAND, nk),
            in_specs=[pl.BlockSpec((tk, wa), lambda c, k: (k, c)), pl.BlockSpec((tk, wb), lambda c, k: (k, c))],
            out_specs=pl.BlockSpec((wa, wb), lambda c, k: (c, 0)), out_shape=jax.ShapeDtypeStruct((a.shape[1], wb), F32),
            scratch_shapes=[pltpu.VMEM((wa, wb), F32)], compiler_params=_cparams("parallel", "arbitrary"))(a, b)

    wo = (b.shape[0] if b_t else b.shape[1]) // S5_BAND
    tm = _tile(S, tm)
    ex_specs = [pl.BlockSpec((tm, wo), lambda i, c: (i, c)) if kind == "mn" else pl.BlockSpec((1, wo), lambda i, c: (0, c))
                for _, kind in extras]

    def body(a_ref, b_ref, *refs):
        part = lax.dot_general(a_ref[...].astype(BF16), b_ref[...].astype(BF16), NT_DIMS if b_t else _DIMS["nn"], preferred_element_type=F32)
        if epi is not None:
            part = epi(part, *[r[...] for r in refs[:-1]])
        refs[-1][...] = part

    b_spec = pl.BlockSpec((wo, wa) if b_t else (wa, wo), lambda i, c: (c, c))
    return pl.pallas_call(
        body, name=name, grid=(S // tm, S5_BAND), in_specs=[pl.BlockSpec((tm, wa), lambda i, c: (i, c)), b_spec] + ex_specs,
        out_specs=pl.BlockSpec((tm, wo), lambda i, c: (i, c)), out_shape=jax.ShapeDtypeStruct((S, S5_BAND * wo), F32),
        compiler_params=_cparams("parallel", "parallel"))(a, b, *[e[0] for e in extras])


def _band_to_full(blocks, cols):
    wa, wb = blocks.shape[0] // S5_BAND, blocks.shape[1]
    return jnp.concatenate([jnp.pad(blocks[k * wa:(k + 1) * wa], ((0, 0), (k * wb, cols - (k + 1) * wb))) for k in range(S5_BAND)], axis=0)


def _block_diag(t):
    G, a, b = t.shape
    return (t[:, :, None, :] * jnp.eye(G, dtype=t.dtype)[:, None, :, None]).reshape(G * a, G * b)


def _block_diag_take(m, G):
    a, b = m.shape[0] // G, m.shape[1] // G
    m4 = m.reshape(G, a, G, b)
    return jnp.sum(m4 * jnp.eye(G, dtype=m.dtype)[:, None, :, None], axis=2)


def _s5_block_fwd(u, w, pfx):
    a_re, a_im, bb_re, bb_im = _s5_discretise(w["lam_re"], w["lam_im"], w["log_dt"], w["b_re"], w["b_im"])
    bcat = _planes(_block_diag(bb_re).T, _block_diag(bb_im).T).astype(BF16)
    ccat = _planes(_block_diag(jnp.swapaxes(w["c_re"], 1, 2)).T, -_block_diag(jnp.swapaxes(w["c_im"], 1, 2)).T).T.astype(BF16)
    af_re, af_im = a_re.reshape(-1), a_im.reshape(-1)
    apow, ptab = _s5_scan_tables(af_re, af_im, False)
    bu = _mm_band(u, bcat, pfx + "_bu")
    x = _s5_scan(bu, apow, ptab, pfx + "_scan", False)
    d_row = w["d"].reshape(1, MIX_HALF)
    ys = _mm_band(x, ccat, pfx + "_y", epi=lambda acc, ut, dr: acc + dr * ut, extras=[(u, "mn"), (d_row, "n")])
    z = _mm(ys, w["w_glu"], "nn", pfx + "_glu", a_pro=_gelu, epi=lambda acc, b: acc + b, extras=[(w["b_glu"].reshape(1, -1), "n")])
    y2, = _ew(lambda ysv, zv: _gelu(ysv) * _sigmoid(zv), pfx + "_gate", [ys, z], outs=[(MIX_HALF, F32)])
    return y2, dict(u=u, x=x, ys=ys, z=z, bcat=bcat, ccat=ccat, a=(af_re, af_im), d_row=d_row)


def _s5_block_bwd(dy2, w, res, pfx):
    u, x, ys, z, bcat, ccat = res["u"], res["x"], res["ys"], res["z"], res["bcat"], res["ccat"]

    def gate_bwd(dy, ysv, zv):
        sg = _sigmoid(zv)
        dz = dy * _gelu(ysv) * sg * (1.0 - sg)
        return dz, jnp.sum(dz, axis=0, keepdims=True)

    dz, db_glu = _ew(gate_bwd, pfx + "_gate_bwd", [dy2, ys, z], outs=[(MIX_HALF, F32)], sums=[MIX_HALF])
    dw_glu = _mm(ys, dz, "tn", pfx + "_dwglu", a_pro=_gelu)
    dys = _mm(dz, w["w_glu"], "nt", pfx + "_dys", epi=lambda acc, dy, zv, ysv: (acc + dy * _sigmoid(zv)) * _dgelu(ysv),
              extras=[(dy2, "mn"), (z, "mn"), (ys, "mn")])
    dd, = _ew(lambda a, b: jnp.sum(a * b, axis=0, keepdims=True), pfx + "_dd", [dys, u], sums=[MIX_HALF])
    dccat = _band_to_full(_mm_band(x, dys, pfx + "_dc", outer=True), MIX_HALF)
    dx = _mm_band(dys, ccat, pfx + "_dx", b_t=True)
    af_re, af_im = res["a"]
    apow, ptab = _s5_scan_tables(af_re, -af_im, True)
    lam, da8 = _s5_scan(dx, apow, ptab, pfx + "_scan_bwd", True, x_fwd=x)
    dbcat = _band_to_full(_mm_band(u, lam, pfx + "_db", outer=True), 2 * S5_N)
    du = _mm_band(lam, bcat, pfx + "_du", b_t=True, epi=lambda acc, dyv, dr: acc + dyv * dr, extras=[(dys, "mn"), (res["d_row"], "n")])
    G = S5_GROUPS
    d_abar_re, d_abar_im = (t.reshape(G, S5_STATE) for t in _unplanes(jnp.sum(da8, axis=0)))
    d_bb_re, d_bb_im = (_block_diag_take(t.T, G) for t in _unplanes(dbcat))
    _, vjp = jax.vjp(_s5_discretise, w["lam_re"], w["lam_im"], w["log_dt"], w["b_re"], w["b_im"])
    g_lam_re, g_lam_im, g_log_dt, g_b_re, g_b_im = vjp((d_abar_re, d_abar_im, d_bb_re, d_bb_im))
    dc_re, dc_im = _unplanes(dccat.T)
    g_c_re = jnp.swapaxes(_block_diag_take(dc_re.T, G), 1, 2)
    g_c_im = -jnp.swapaxes(_block_diag_take(dc_im.T, G), 1, 2)
    grads = dict(lam_re=g_lam_re, lam_im=g_lam_im, log_dt=g_log_dt, b_re=g_b_re, b_im=g_b_im, c_re=g_c_re, c_im=g_c_im,
                 d=dd.reshape(G, S5_GROUP_WIDTH), w_glu=dw_glu, b_glu=db_glu.reshape(-1))
    return du, grads


SGU_TS = 512
N_PAIRS = MIX_HALF // LANES


def _half_masks(rows):
    lane = lax.broadcasted_iota(jnp.int32, (rows, LANES), 1)
    left = (lane < HEAD_DIM).astype(F32)
    return left, 1.0 - left


def _sgu_norm(zv, gain, bias):
    v = _gelu(zv)
    mu = jnp.mean(v, axis=-1, keepdims=True)
    vc = v - mu
    rstd = lax.rsqrt(jnp.mean(vc * vc, axis=-1, keepdims=True) + EPS)
    vhat = vc * rstd
    return vhat, rstd, vhat * gain + bias


def _sgu_tables(w_s, b_s):
    mask = jnp.tril(jnp.ones((SGU_CHUNK, SGU_CHUNK), dtype=bool))
    wm = jnp.where(mask[None], w_s, 0.0).astype(BF16)
    bias_tab = jnp.repeat(b_s.T, MIX_HALF // SGU_GROUPS, axis=1)
    return wm, bias_tab


def _sgu_fwd(proj, ln_gain, ln_bias, wm, bias_tab, name):
    S = proj.shape[0]
    nch = SGU_TS // SGU_CHUNK

    def body(zu_ref, zv_ref, g_ref, b_ref, w_ref, bt_ref, o_ref):
        left, right = _half_masks(SGU_CHUNK)
        _, _, vn = _sgu_norm(zv_ref[...], g_ref[...], b_ref[...])
        for ch in range(nch):
            rows = pl.ds(ch * SGU_CHUNK, SGU_CHUNK)
            for p in range(N_PAIRS):
                cols = pl.ds(p * LANES, LANES)
                vp = vn[ch * SGU_CHUNK:(ch + 1) * SGU_CHUNK, p * LANES:(p + 1) * LANES]
                mixed = (jnp.dot(w_ref[2 * p], (vp * left).astype(BF16), preferred_element_type=F32)
                         + jnp.dot(w_ref[2 * p + 1], (vp * right).astype(BF16), preferred_element_type=F32) + bt_ref[:, cols])
                o_ref[rows, cols] = _gelu(zu_ref[rows, cols]) * mixed

    vec = _vec_spec(MIX_HALF)
    return pl.pallas_call(
        body, name=name, grid=(S // SGU_TS,),
        in_specs=[pl.BlockSpec((SGU_TS, MIX_HALF), lambda i: (i, 1)), pl.BlockSpec((SGU_TS, MIX_HALF), lambda i: (i, 2)), vec, vec,
                  pl.BlockSpec((SGU_GROUPS, SGU_CHUNK, SGU_CHUNK), lambda i: (0, 0, 0)), pl.BlockSpec((SGU_CHUNK, MIX_HALF), lambda i: (0, 0))],
        out_specs=_row_spec(MIX_HALF, SGU_TS), out_shape=jax.ShapeDtypeStruct((S, MIX_HALF), F32),
        compiler_params=_cparams("parallel"))(proj, proj, ln_gain, ln_bias, wm, bias_tab)


def _sgu_bwd(dout, proj, ln_gain, ln_bias, wm, bias_tab, name):
    S = proj.shape[0]
    nch = SGU_TS // SGU_CHUNK
    nt_dims = (((1,), (1,)), ((), ()))
    tn_dims = (((0,), (0,)), ((), ()))

    def body(do_ref, zu_ref, zv_ref, g_ref, b_ref, w_ref, bt_ref, dzu_ref, dzv_ref, dw_ref, dbt_ref, dg_ref, db_ref, dvn_ref):
        first = pl.program_id(0) == 0

        @pl.when(first)
        def _():
            dw_ref[...] = jnp.zeros_like(dw_ref)
            dbt_ref[...] = jnp.zeros_like(dbt_ref)
            dg_ref[...] = jnp.zeros_like(dg_ref)
            db_ref[...] = jnp.zeros_like(db_ref)

        left, right = _half_masks(SGU_CHUNK)
        zv = zv_ref[...]
        vhat, rstd, vn = _sgu_norm(zv, g_ref[...], b_ref[...])
        for ch in range(nch):
            rows = pl.ds(ch * SGU_CHUNK, SGU_CHUNK)
            for p in range(N_PAIRS):
                cols = pl.ds(p * LANES, LANES)
                vp = vn[ch * SGU_CHUNK:(ch + 1) * SGU_CHUNK, p * LANES:(p + 1) * LANES]
                vl, vr = (vp * left).astype(BF16), (vp * right).astype(BF16)
                mixed = (jnp.dot(w_ref[2 * p], vl, preferred_element_type=F32)
                         + jnp.dot(w_ref[2 * p + 1], vr, preferred_element_type=F32) + bt_ref[:, cols])
                zu = zu_ref[rows, cols]
                do = do_ref[rows, cols]
                dzu_ref[rows, cols] = do * mixed * _dgelu(zu)
                dmix = do * _gelu(zu)
                dbt_ref[:, cols] += dmix
                dl, dr = (dmix * left).astype(BF16), (dmix * right).astype(BF16)
                dw_ref[2 * p] += lax.dot_general(dl, vl, nt_dims, preferred_element_type=F32)
                dw_ref[2 * p + 1] += lax.dot_general(dr, vr, nt_dims, preferred_element_type=F32)
                dvn_ref[rows, cols] = (lax.dot_general(w_ref[2 * p], dl, tn_dims, preferred_element_type=F32)
                                       + lax.dot_general(w_ref[2 * p + 1], dr, tn_dims, preferred_element_type=F32))
        dvn = dvn_ref[...]
        dg_ref[...] += jnp.sum(dvn * vhat, axis=0, keepdims=True)
        db_ref[...] += jnp.sum(dvn, axis=0, keepdims=True)
        dvh = dvn * g_ref[...]
        dv = rstd * (dvh - jnp.mean(dvh, axis=-1, keepdims=True) - vhat * jnp.mean(dvh * vhat, axis=-1, keepdims=True))
        dzv_ref[...] = dv * _dgelu(zv)

    vec = _vec_spec(MIX_HALF)
    row = _row_spec(MIX_HALF, SGU_TS)
    wspec = pl.BlockSpec((SGU_GROUPS, SGU_CHUNK, SGU_CHUNK), lambda i: (0, 0, 0))
    tspec = pl.BlockSpec((SGU_CHUNK, MIX_HALF), lambda i: (0, 0))
    full = jax.ShapeDtypeStruct((S, MIX_HALF), F32)
    v = jax.ShapeDtypeStruct((1, MIX_HALF), F32)
    return pl.pallas_call(
        body, name=name, grid=(S // SGU_TS,),
        in_specs=[row, pl.BlockSpec((SGU_TS, MIX_HALF), lambda i: (i, 1)), pl.BlockSpec((SGU_TS, MIX_HALF), lambda i: (i, 2)), vec, vec,
                  wspec, tspec],
        out_specs=[row, row, wspec, tspec, vec, vec],
        out_shape=[full, full, jax.ShapeDtypeStruct((SGU_GROUPS, SGU_CHUNK, SGU_CHUNK), F32),
                   jax.ShapeDtypeStruct((SGU_CHUNK, MIX_HALF), F32), v, v],
        scratch_shapes=[pltpu.VMEM((SGU_TS, MIX_HALF), F32)],
        compiler_params=_cparams("arbitrary"))(dout, proj, proj, ln_gain, ln_bias, wm, bias_tab)


def _sgu_grads(dw, dbias_tab):
    mask = jnp.tril(jnp.ones((SGU_CHUNK, SGU_CHUNK), dtype=bool))
    g_w = jnp.where(mask[None], dw, 0.0)
    g_b = dbias_tab.reshape(SGU_CHUNK, SGU_GROUPS, MIX_HALF // SGU_GROUPS).sum(axis=-1).T
    return g_w, g_b


def _head_avg_matrix(w):
    idx = np.arange(w) // HEAD_DIM
    return jnp.asarray((idx[:, None] == idx[None, :]).astype(np.float32) / HEAD_DIM, dtype=BF16)


def _head_mean(t, bavg):
    hi = t.astype(BF16)
    lo = (t - hi.astype(F32)).astype(BF16)
    return jnp.dot(hi, bavg, preferred_element_type=F32) + jnp.dot(lo, bavg, preferred_element_type=F32)


def _head_rms(t, bavg):
    r = lax.rsqrt(_head_mean(t * t, bavg) + EPS)
    return t * r, r


def _head_rms_bwd(dn, n, r, bavg):
    return r * (dn - n * _head_mean(dn * n, bavg))


GLA_TS = 512
C = GLA_CHUNK
NT_DIMS = (((1,), (1,)), ((), ()))
TN_DIMS = (((0,), (0,)), ((), ()))
HI = lax.Precision.HIGHEST


def _bdot(a, b, dims=(((1,), (0,)), ((), ()))):
    return lax.dot_general(a.astype(BF16), b.astype(BF16), dims, preferred_element_type=F32)


def _gla_chunk_terms(q, k, z):
    row = lax.broadcasted_iota(jnp.int32, (C, C), 0)
    col = lax.broadcasted_iota(jnp.int32, (C, C), 1)
    lc = _log_sigmoid(z) * (1.0 / GLA_TAU)
    b = lax.dot_general((row >= col).astype(F32), lc, (((1,), (0,)), ((), ())), precision=HI, preferred_element_type=F32)
    b_last = jnp.sum(lc, axis=0, keepdims=True)
    b_mid = b[C // 2:C // 2 + 1, :]
    scale = HEAD_DIM ** -0.5
    e_b, e_q, e_k, e_l = jnp.exp(b), jnp.exp(b - b_mid), jnp.exp(b_mid - b), jnp.exp(b_last - b)
    qs = q * (scale * e_b)
    qe = q * (scale * e_q)
    ke = k * e_k
    kl = k * e_l
    return dict(e_b=e_b, e_q=e_q, e_k=e_k, e_l=e_l, qs=qs, qe=qe, ke=ke, kl=kl, dec=jnp.exp(b_last), causal=row >= col, scale=scale)


def _pair(x, pp):
    return x[:, pp * LANES:(pp + 1) * LANES]


def _pair_block_diag():
    r = lax.broadcasted_iota(jnp.int32, (LANES, LANES), 0) // HEAD_DIM
    c = lax.broadcasted_iota(jnp.int32, (LANES, LANES), 1) // HEAD_DIM
    return (r == c).astype(F32)


def _gla_fwd(proj, z, name):
    S = proj.shape[0]
    nch = GLA_TS // C

    def body(q_ref, k_ref, v_ref, z_ref, o_ref, st_ref, state_ref):
        @pl.when(pl.program_id(0) == 0)
        def _():
            state_ref[...] = jnp.zeros_like(state_ref)

        left, right = _half_masks(C)
        bd = _pair_block_diag()
        pairs = range(N_PAIRS)
        for ch in range(nch):
            rows = pl.ds(ch * C, C)
            v = v_ref[rows, :]
            t = _gla_chunk_terms(q_ref[rows, :], k_ref[rows, :], z_ref[rows, :])
            sts = [state_ref[pp] for pp in pairs]
            for pp in pairs:
                st_ref[ch, pp] = sts[pp]
            os = [_bdot(_pair(t["qs"], pp), sts[pp], NT_DIMS) for pp in pairs]
            for m in (left, right):
                scores = [jnp.where(t["causal"], _bdot(_pair(t["qe"], pp) * m, _pair(t["ke"], pp), NT_DIMS), 0.0) for pp in pairs]
                os = [os[pp] + m * _bdot(scores[pp], _pair(v, pp)) for pp in pairs]
            o_ref[rows, :] = jnp.concatenate(os, axis=1)
            new = [sts[pp] * _pair(t["dec"], pp) + bd * _bdot(_pair(v, pp), _pair(t["kl"], pp), TN_DIMS) for pp in pairs]
            for pp in pairs:
                state_ref[pp] = new[pp]

    def col(cb):
        return pl.BlockSpec((GLA_TS, MIX_HALF), lambda i: (i, cb))

    return pl.pallas_call(
        body, name=name, grid=(S // GLA_TS,),
        in_specs=[col(0), col(1), col(2), col(0)],
        out_specs=[col(0), pl.BlockSpec((nch, N_PAIRS, LANES, LANES), lambda i: (i, 0, 0, 0))],
        out_shape=[jax.ShapeDtypeStruct((S, MIX_HALF), F32), jax.ShapeDtypeStruct((S // C, N_PAIRS, LANES, LANES), F32)],
        scratch_shapes=[pltpu.VMEM((N_PAIRS, LANES, LANES), F32)], compiler_params=_cparams("arbitrary"))(proj, proj, proj, z)


def _gla_bwd(do, proj, z, states, name):
    S = proj.shape[0]
    nch = GLA_TS // C
    nblk = S // GLA_TS

    def body(do_ref, q_ref, k_ref, v_ref, z_ref, st_ref, dq_ref, dk_ref, dv_ref, dlc_ref, dstate_ref):
        @pl.when(pl.program_id(0) == 0)
        def _():
            dstate_ref[...] = jnp.zeros_like(dstate_ref)

        left, right = _half_masks(C)
        bd = _pair_block_diag()
        rowi = lax.broadcasted_iota(jnp.int32, (C, LANES), 0)
        row = lax.broadcasted_iota(jnp.int32, (C, C), 0)
        colm = lax.broadcasted_iota(jnp.int32, (C, C), 1)
        pairs = range(N_PAIRS)
        rowi = lax.broadcasted_iota(jnp.int32, (C, MIX_HALF), 0)
        for ch in range(nch - 1, -1, -1):
            rows = pl.ds(ch * C, C)
            v, dov = v_ref[rows, :], do_ref[rows, :]
            t = _gla_chunk_terms(q_ref[rows, :], k_ref[rows, :], z_ref[rows, :])
            sts = [st_ref[ch, pp] for pp in pairs]
            nxt = [dstate_ref[pp] for pp in pairs]
            gs = [bd * nxt[pp] for pp in pairs]
            dqs = [_bdot(_pair(dov, pp), sts[pp]) for pp in pairs]
            dv = [_bdot(_pair(t["kl"], pp), gs[pp], NT_DIMS) for pp in pairs]
            dkl = [_bdot(_pair(v, pp), gs[pp]) for pp in pairs]
            dqe = [jnp.zeros((C, LANES), F32) for _ in pairs]
            dke = [jnp.zeros((C, LANES), F32) for _ in pairs]
            for m in (left, right):
                sc = [jnp.where(t["causal"], _bdot(_pair(t["qe"], pp) * m, _pair(t["ke"], pp), NT_DIMS), 0.0) for pp in pairs]
                dsc = [jnp.where(t["causal"], _bdot(_pair(dov, pp) * m, _pair(v, pp), NT_DIMS), 0.0) for pp in pairs]
                dv = [dv[pp] + m * _bdot(sc[pp], _pair(dov, pp), TN_DIMS) for pp in pairs]
                dqe = [dqe[pp] + m * _bdot(dsc[pp], _pair(t["ke"], pp)) for pp in pairs]
                dke = [dke[pp] + m * _bdot(dsc[pp], _pair(t["qe"], pp), TN_DIMS) for pp in pairs]
            for pp in pairs:
                dstate_ref[pp] = bd * (nxt[pp] * _pair(t["dec"], pp) + _bdot(_pair(dov, pp), _pair(t["qs"], pp), TN_DIMS))
            decay_sum = jnp.concatenate([jnp.sum(nxt[pp] * sts[pp], axis=0, keepdims=True) for pp in pairs], axis=1)
            dqs, dv, dkl, dqe, dke = (jnp.concatenate(parts, axis=1) for parts in (dqs, dv, dkl, dqe, dke))
            db_last = decay_sum * t["dec"] + jnp.sum(dkl * t["kl"], axis=0, keepdims=True)
            db = dqs * t["qs"] + dqe * t["qe"] - dke * t["ke"] - dkl * t["kl"]
            db = db + jnp.where(rowi == C - 1, db_last, 0.0)
            dq_ref[rows, :] = (dqs * t["e_b"] + dqe * t["e_q"]) * t["scale"]
            dk_ref[rows, :] = dke * t["e_k"] + dkl * t["e_l"]
            dv_ref[rows, :] = dv
            dlc_ref[rows, :] = lax.dot_general((colm >= row).astype(F32), db, (((1,), (0,)), ((), ())), precision=HI,
                                               preferred_element_type=F32)

    def col(cb):
        return pl.BlockSpec((GLA_TS, MIX_HALF), lambda i: (nblk - 1 - i, cb))

    full = jax.ShapeDtypeStruct((S, MIX_HALF), F32)
    return pl.pallas_call(
        body, name=name, grid=(nblk,),
        in_specs=[col(0), col(0), col(1), col(2), col(0), pl.BlockSpec((nch, N_PAIRS, LANES, LANES), lambda i: (nblk - 1 - i, 0, 0, 0))],
        out_specs=[col(0)] * 4, out_shape=[full, full, full, full],
        scratch_shapes=[pltpu.VMEM((N_PAIRS, LANES, LANES), F32)], compiler_params=_cparams("arbitrary"))(do, proj, proj, proj, z, states)


def _gla_block_fwd(proj, w_lr_pad, b_lr, gain, bavg, pfx):
    z = _mm(proj, w_lr_pad, "nn", pfx + "_z", a_cols=(7 * MIX_HALF, MIX_HALF), epi=lambda acc, b: acc + b, extras=[(b_lr, "n")])
    o, states = _gla_fwd(proj, z, pfx + "_core")

    def out(ov, gg, ba, gn):
        n, _ = _head_rms(ov, ba)
        return n * gn * (gg * _sigmoid(gg))

    og, = _ew(out, pfx + "_out", [o, (proj, MIX_HALF, 3)], consts=[bavg, gain], outs=[(MIX_HALF, F32)])
    return og, dict(z=z, o=o, states=states)


def _gla_block_bwd(dog, proj, w_lr_pad, gain, bavg, res, pfx):
    z, o, states = res["z"], res["o"], res["states"]

    def out_bwd(dy, ov, gg, ba, gn):
        n, r = _head_rms(ov, ba)
        sg = _sigmoid(gg)
        silu = gg * sg
        dn = dy * gn * silu
        do = _head_rms_bwd(dn, n, r, ba)
        dgg = dy * n * gn * (sg * (1.0 + gg * (1.0 - sg)))
        return do, dgg, jnp.sum(dy * n * silu, axis=0, keepdims=True)

    do, dgg, dgain = _ew(out_bwd, pfx + "_out_bwd", [dog, o, (proj, MIX_HALF, 3)], consts=[bavg, gain],
                         outs=[(MIX_HALF, F32), (MIX_HALF, F32)], sums=[MIX_HALF])
    dq, dk, dv, dlc = _gla_bwd(do, proj, z, states, pfx + "_core_bwd")

    def decay_bwd(dl, zv):
        dz = dl * (1.0 / GLA_TAU) * (1.0 - _sigmoid(zv))
        return dz, jnp.sum(dz, axis=0, keepdims=True)

    dz, db_lr = _ew(decay_bwd, pfx + "_decay_bwd", [dlc, z], outs=[(MIX_HALF, F32)], sums=[MIX_HALF])
    dw_lr_pad = _mm(proj, dz, "tn", pfx + "_dwlr", a_cols=(7 * MIX_HALF, MIX_HALF))
    dsmall = _mm(dz, w_lr_pad, "nt", pfx + "_dsmall")
    return (dq, dk, dv, dgg, dsmall), dict(w_lr=dw_lr_pad[:GLA_RANK], b_lr=db_lr.reshape(-1), gain=dgain.reshape(-1, HEAD_DIM))


FOX_T = 512
FOX_HEADS = MIX_HALF // HEAD_DIM
NEG = -1e30
CUM_T = 512


def _cum_lanes(x, name, reverse, pre=None):
    R, S = x.shape
    nb = S // CUM_T

    def body(x_ref, o_ref, carry_ref):
        @pl.when(pl.program_id(0) == 0)
        def _():
            carry_ref[...] = jnp.zeros_like(carry_ref)

        xv = x_ref[...]
        if pre is not None:
            xv = pre(xv)
        i = lax.broadcasted_iota(jnp.int32, (CUM_T, CUM_T), 0)
        j = lax.broadcasted_iota(jnp.int32, (CUM_T, CUM_T), 1)
        tri = ((i >= j) if reverse else (i <= j)).astype(F32)
        c = lax.dot_general(xv, tri, (((1,), (0,)), ((), ())), precision=HI, preferred_element_type=F32)
        carry = carry_ref[...]
        o_ref[...] = c + carry[:, 0:1]
        carry_ref[...] = carry + jnp.sum(xv, axis=1, keepdims=True)

    spec = pl.BlockSpec((R, CUM_T), (lambda i: (0, nb - 1 - i)) if reverse else (lambda i: (0, i)))
    return pl.pallas_call(body, name=name, grid=(nb,), in_specs=[spec], out_specs=spec, out_shape=jax.ShapeDtypeStruct((R, S), F32),
                          scratch_shapes=[pltpu.VMEM((R, LANES), F32)], compiler_params=_cparams("arbitrary"))(x)


def _fox_scores(q, k, cqb, ck_ref, h, m, diag):
    cq = cqb[:, h * HEAD_DIM:h * HEAD_DIM + 1]
    ck = ck_ref[0, h:h + 1, :]
    s = lax.dot_general(q * m.astype(q.dtype), k, NT_DIMS, preferred_element_type=F32) + (cq - ck)
    if not diag:
        return s
    row = lax.broadcasted_iota(jnp.int32, (FOX_T, FOX_T), 0)
    col = lax.broadcasted_iota(jnp.int32, (FOX_T, FOX_T), 1)
    return jnp.where(row < col, NEG, s)


def _on_causal_blocks(q_blk, k_blk, step):
    @pl.when(k_blk < q_blk)
    def _():
        step(False)

    @pl.when(k_blk == q_blk)
    def _():
        step(True)


def _causal_pairs(n, key_major):
    if key_major:
        pairs = [(q, k) for k in range(n) for q in range(k, n)]
    else:
        pairs = [(q, k) for q in range(n) for k in range(q + 1)]
    return jnp.asarray([p[0] for p in pairs], jnp.int32), jnp.asarray([p[1] for p in pairs], jnp.int32)


def _carried(carry, refs, n_in, n_out, first, last):
    if carry is None:
        return refs
    ins, cx_ref, outs, co_ref = refs[:n_in], refs[n_in], refs[n_in + 1:n_in + 1 + n_out], refs[n_in + 1 + n_out]
    scratch = refs[n_in + 2 + n_out:]
    start, finish = _exchange_plan(cx_ref, co_ref, *scratch[-3:], carry[1])
    pl.when(first)(start)
    pl.when(last)(finish)
    return ins + outs + scratch[:-3]


def _carry_specs(carry):
    if carry is None:
        return [], [], [], [], []
    x, bcast = carry
    blk = x.shape if bcast else x.shape[1:]
    return [ANY], [ANY], [jax.ShapeDtypeStruct((N_CHIPS,) + tuple(blk), x.dtype)], list(_EXCHANGE_SEMS), [x]


def _fox_fwd(qn, kn, proj, cum_b, cum_tp, name, carry=None):
    S = qn.shape[0]
    nq = S // FOX_T
    qidx, kidx = _causal_pairs(nq, False)
    ntri = int(qidx.shape[0])

    def body(qidx_ref, kidx_ref, *refs):
        t = pl.program_id(1)
        first = jnp.logical_and(pl.program_id(0) == 0, t == 0)
        last = jnp.logical_and(pl.program_id(0) == N_PAIRS - 1, t == ntri - 1)
        q_ref, k_ref, v_ref, cq_ref, ck_ref, o_ref, lse_ref, m_scr, acc_scr = _carried(carry, refs, 5, 2, first, last)
        qi, ki = qidx_ref[t], kidx_ref[t]

        @pl.when(ki == 0)
        def _():
            m_scr[...] = jnp.full_like(m_scr, NEG)
            acc_scr[...] = jnp.zeros_like(acc_scr)

        left, right = _half_masks(FOX_T)

        def step(diag):
            q, k, v = q_ref[...], k_ref[...], v_ref[...].astype(BF16)
            cqb = cq_ref[...]
            for h, m in enumerate((left, right)):
                s = _fox_scores(q, k, cqb, ck_ref, h, m, diag)
                m_prev = m_scr[h]
                m_new = jnp.maximum(m_prev, jnp.max(s, axis=1, keepdims=True))
                p = jnp.exp(s - m_new)
                v_h = jnp.where(m > 0, v, jnp.ones_like(v))
                acc_scr[h] = jnp.exp(m_prev - m_new) * acc_scr[h] + jnp.dot(p.astype(BF16), v_h, preferred_element_type=F32)
                m_scr[h] = m_new

        _on_causal_blocks(qi, ki, step)

        @pl.when(ki == qi)
        def _():
            a0, a1 = acc_scr[0], acc_scr[1]
            is_left = left > 0
            num = jnp.where(is_left, a0, a1)
            den = jnp.where(is_left, pltpu.roll(a0, HEAD_DIM, 1), pltpu.roll(a1, HEAD_DIM, 1))
            o_ref[...] = num / den
            lse_ref[...] = jnp.where(is_left, m_scr[0], m_scr[1]) + jnp.log(den)

    qspec = pl.BlockSpec((FOX_T, LANES), lambda p, t, qx, kx: (qx[t], p))
    kspec = pl.BlockSpec((FOX_T, LANES), lambda p, t, qx, kx: (kx[t], p))
    vspec = pl.BlockSpec((FOX_T, LANES), lambda p, t, qx, kx: (kx[t], 6 * N_PAIRS + p))
    ckspec = pl.BlockSpec((1, 8, FOX_T), lambda p, t, qx, kx: (p, 0, kx[t]))
    full = jax.ShapeDtypeStruct((S, MIX_HALF), F32)
    c_in, c_out, c_shape, c_scratch, c_args = _carry_specs(carry)
    grid_spec = pltpu.PrefetchScalarGridSpec(
        num_scalar_prefetch=2, grid=(N_PAIRS, ntri), in_specs=[qspec, kspec, vspec, qspec, ckspec] + c_in, out_specs=[qspec, qspec] + c_out,
        scratch_shapes=[pltpu.VMEM((2, FOX_T, 1), F32), pltpu.VMEM((2, FOX_T, LANES), F32)] + c_scratch)
    return pl.pallas_call(body, name=name, grid_spec=grid_spec, out_shape=[full, full] + c_shape,
                          compiler_params=_cparams("arbitrary", "arbitrary"))(qidx, kidx, qn, kn, proj, cum_b, cum_tp, *c_args)


def _fox_bwd(do, qn, kn, proj, cum_b, cum_tp, lse_b, delta_b, name, carry=None, do_pair0=0):
    S = qn.shape[0]
    nq = S // FOX_T
    scale = HEAD_DIM ** -0.5
    qidx, kidx = _causal_pairs(nq, True)
    ntri = int(qidx.shape[0])

    def body(qidx_ref, kidx_ref, *refs):
        t = pl.program_id(1)
        first = jnp.logical_and(pl.program_id(0) == 0, t == 0)
        last = jnp.logical_and(pl.program_id(0) == N_PAIRS - 1, t == ntri - 1)
        (do_ref, q_ref, k_ref, v_ref, cq_ref, ck_ref, lse_ref, dl_ref, dq_ref, dcq_ref, dk_ref, dv_ref, dck_ref,
         dq_scr, dk_scr, dv_scr) = _carried(carry, refs, 8, 5, first, last)
        qi, ki = qidx_ref[t], kidx_ref[t]

        @pl.when(t == 0)
        def _():
            dq_scr[...] = jnp.zeros_like(dq_scr)

        @pl.when(qi == ki)
        def _():
            dk_scr[...] = jnp.zeros_like(dk_scr)
            dv_scr[...] = jnp.zeros_like(dv_scr)

        left, right = _half_masks(FOX_T)
        rows = pl.ds(pl.multiple_of(qi * FOX_T, FOX_T), FOX_T)

        def step(diag):
            q, k, v, dov = q_ref[...], k_ref[...], v_ref[...].astype(BF16), do_ref[...]
            cqb, lseb, dlb = cq_ref[...], lse_ref[...], dl_ref[...]
            dob = dov.astype(BF16)
            heads = (0, 1)
            masks = (left, right)
            col = [slice(h * HEAD_DIM, h * HEAD_DIM + 1) for h in heads]
            ss = [_fox_scores(q, k, cqb, ck_ref, h, masks[h], diag) for h in heads]
            dps = [lax.dot_general((dov * masks[h]).astype(BF16), v, NT_DIMS, preferred_element_type=F32) for h in heads]
            ps = [jnp.exp(ss[h] - lseb[:, col[h]]) for h in heads]
            dss = [(ps[h] * (dps[h] - dlb[:, col[h]])).astype(BF16) for h in heads]
            pvs = [lax.dot_general(ps[h].astype(BF16), dob, TN_DIMS, preferred_element_type=F32) for h in heads]
            dks = [lax.dot_general(dss[h], jnp.where(masks[h] > 0, q, jnp.ones_like(q)), TN_DIMS, preferred_element_type=F32) for h in heads]
            dqs = [jnp.dot(dss[h], jnp.where(masks[h] > 0, k, jnp.ones_like(k)), preferred_element_type=F32) for h in heads]
            dv_scr[...] = dv_scr[...] + left * pvs[0] + right * pvs[1]
            for h in heads:
                dk_scr[h] = dk_scr[h] + dks[h]
                dq_scr[h, rows, :] = dq_scr[h, rows, :] + dqs[h]

        _on_causal_blocks(qi, ki, step)

        @pl.when(qi == nq - 1)
        def _():
            a0, a1 = dk_scr[0], dk_scr[1]
            dk_ref[...] = left * a0 + right * a1
            dv_ref[...] = dv_scr[...]
            dck_ref[...] = left * pltpu.roll(a0, HEAD_DIM, 1) + right * pltpu.roll(a1, HEAD_DIM, 1)

        @pl.when(t == ntri - 1)
        def _():
            for r in range(nq):
                blk = pl.ds(r * FOX_T, FOX_T)
                a0, a1 = dq_scr[0, blk, :], dq_scr[1, blk, :]
                dq_ref[blk, :] = (left * a0 + right * a1) * scale
                dcq_ref[blk, :] = left * pltpu.roll(a0, HEAD_DIM, 1) + right * pltpu.roll(a1, HEAD_DIM, 1)

    qspec = pl.BlockSpec((FOX_T, LANES), lambda p, t, qx, kx: (qx[t], p))
    kspec = pl.BlockSpec((FOX_T, LANES), lambda p, t, qx, kx: (kx[t], p))
    vspec = pl.BlockSpec((FOX_T, LANES), lambda p, t, qx, kx: (kx[t], 6 * N_PAIRS + p))
    ckspec = pl.BlockSpec((1, 8, FOX_T), lambda p, t, qx, kx: (p, 0, kx[t]))
    seq = pl.BlockSpec((S, LANES), lambda p, t, qx, kx: (0, p))
    full = jax.ShapeDtypeStruct((S, MIX_HALF), F32)
    c_in, c_out, c_shape, c_scratch, c_args = _carry_specs(carry)
    grid_spec = pltpu.PrefetchScalarGridSpec(
        num_scalar_prefetch=2, grid=(N_PAIRS, ntri),
        in_specs=[pl.BlockSpec((FOX_T, LANES), lambda p, t, qx, kx: (qx[t], do_pair0 + p)), qspec, kspec, vspec, qspec, ckspec, qspec, qspec] + c_in,
        out_specs=[seq, seq, kspec, kspec, kspec] + c_out,
        scratch_shapes=[pltpu.VMEM((2, S, LANES), F32), pltpu.VMEM((2, FOX_T, LANES), F32), pltpu.VMEM((FOX_T, LANES), F32)] + c_scratch)
    return pl.pallas_call(body, name=name, grid_spec=grid_spec, out_shape=[full] * 5 + c_shape,
                          compiler_params=_cparams("arbitrary", "arbitrary"))(qidx, kidx, do, qn, kn, proj, cum_b, cum_tp, lse_b, delta_b, *c_args)


def _ff_bwd(rc, f_t, name):
    def body(rc_ref, f_ref, d_ref, s_ref):
        d = rc_ref[...] * (1.0 - _sigmoid(f_ref[...]))
        d_ref[...] = d
        s_ref[...] = jnp.sum(d, axis=1, keepdims=True)

    return pl.pallas_call(body, name=name, out_shape=[jax.ShapeDtypeStruct(rc.shape, F32), jax.ShapeDtypeStruct((rc.shape[0], 1), F32)])(rc, f_t)


def _fox_block_fwd(proj, b_f, q_gain, k_gain, bavg, pfx, carry=None):
    S = proj.shape[0]

    def prep(qv, kv, ba, qg, kg):
        return _head_rms(qv, ba)[0] * qg * (HEAD_DIM ** -0.5), _head_rms(kv, ba)[0] * kg

    qn, kn = _ew(prep, pfx + "_prep", [(proj, MIX_HALF, 4), (proj, MIX_HALF, 5)], consts=[bavg, q_gain, k_gain],
                 outs=[(MIX_HALF, BF16), (MIX_HALF, BF16)])
    f0 = 7 * MIX_HALF + GLA_RANK
    f_t = proj[:, f0:f0 + FOX_HEADS].T + b_f.reshape(FOX_HEADS, 1)
    cum = _cum_lanes(f_t, pfx + "_cum", False, pre=_log_sigmoid)
    cum_b = jnp.repeat(cum.T, HEAD_DIM, axis=1)
    cum_tp = jnp.pad(cum.reshape(N_PAIRS, 2, S), ((0, 0), (0, 6), (0, 0)))
    o, lse_b, *carried = _fox_fwd(qn, kn, proj, cum_b, cum_tp, pfx + "_attn", carry=carry)
    return o, dict(qn=qn, kn=kn, f_t=f_t, cum_b=cum_b, cum_tp=cum_tp, o=o, lse_b=lse_b), carried


def _fox_block_bwd(do, proj, q_gain, k_gain, bavg, res, pfx, carry=None):
    qn, kn, o = res["qn"], res["kn"], res["o"]
    S = proj.shape[0]
    delta_b, = _ew(lambda a, b, ba: _head_mean(a * b, ba) * float(HEAD_DIM), pfx + "_delta", [do, o], consts=[bavg], outs=[(MIX_HALF, F32)])
    do_arr, do_blk = (do[0], do[2]) if isinstance(do, tuple) else (do, 0)
    args = (do_arr, qn, kn, proj, res["cum_b"], res["cum_tp"], res["lse_b"], delta_b)
    dqn, dcq_b, dkn, dv, dck_b, *carried = _fox_bwd(*args, pfx + "_bwd", carry=carry, do_pair0=do_blk * N_PAIRS)

    def prep_bwd(dq, dk, qv, kv, ba, qg, kg):
        nq, rq = _head_rms(qv, ba)
        nk, rk = _head_rms(kv, ba)
        return (_head_rms_bwd(dq * qg, nq, rq, ba), _head_rms_bwd(dk * kg, nk, rk, ba),
                jnp.sum(dq * nq, axis=0, keepdims=True), jnp.sum(dk * nk, axis=0, keepdims=True))

    dfq, dfk, dqg, dkg = _ew(prep_bwd, pfx + "_prep_bwd", [dqn, dkn, (proj, MIX_HALF, 4), (proj, MIX_HALF, 5)],
                             consts=[bavg, q_gain, k_gain], outs=[(MIX_HALF, F32), (MIX_HALF, F32)], sums=[MIX_HALF, MIX_HALF])
    dcum = (dcq_b - dck_b)[:, ::HEAD_DIM].T
    rc = _cum_lanes(dcum, pfx + "_rcum", True)
    dff_t, db_f = _ff_bwd(rc, res["f_t"], pfx + "_ff_bwd")
    grads = dict(b_f=db_f.reshape(-1), q_gain=dqg.reshape(-1, HEAD_DIM), k_gain=dkg.reshape(-1, HEAD_DIM))
    return (dfq, dfk, dv, dff_t.T), grads, carried


WEIGHTS = ['ada_w', 'ada_b', 'even_w_in', 'even_w_out', 'gla_w_lr', 'gla_b_lr', 'gla_gain', 'fox_b_f', 'fox_q_gain', 'fox_k_gain',
           'odd_w_in', 'odd_w_out', 's5_lam_re', 's5_lam_im', 's5_log_dt', 's5_b_re', 's5_b_im', 's5_c_re', 's5_c_im', 's5_d',
           's5_w_glu', 's5_b_glu', 'sgu_ln_gain', 'sgu_ln_bias', 'sgu_w_s', 'sgu_b_s', 'mlp_w1', 'mlp_w2']
ARGS = ['x', 'c'] + WEIGHTS + ['loss_target'] + ['m_' + w for w in WEIGHTS] + ['v_' + w for w in WEIGHTS]

EVEN_COLS = 3608
EVEN_PAD = 8 * MIX_HALF
MOD = 6 * D_MODEL
MOD_SHARD = MOD // N_CHIPS

PACK_COLS = 1024
EVEN_SHARD = EVEN_COLS // N_CHIPS
SHARDED = (
    ([("even_w_in", (1, 1024, PACK_COLS), 2), ("even_w_out", (1, 256, 1024), 1), ("gla_w_lr", (1, 16, 128), 2)], 1536),
    ([("mlp_w1_0", (1, 1024, 1024), 2), ("mlp_w2_0", (1, 1024, 1024), 1), ("odd_w_in", (1, 1024, 384), 2),
      ("odd_w_out", (1, 256, 1024), 1), ("mlp_w1_1", (1, 1024, 1024), 2), ("mlp_w2_1", (1, 1024, 1024), 1),
      ("s5_w_glu", (1, 128, 512), 1), ("s5_b_glu", (1, 128), 1), ("sgu_ln_gain", (1, 128), 1), ("sgu_ln_bias", (1, 128), 1)], 5120))
REPLICATED = [("gla_b_lr", (1, 512)), ("gla_gain", (1, 8, 64)), ("fox_b_f", (1, 8)), ("fox_q_gain", (1, 8, 64)),
              ("fox_k_gain", (1, 8, 64)), ("s5_lam_re", (1, 32, 64)), ("s5_lam_im", (1, 32, 64)), ("s5_log_dt", (1, 32)),
              ("s5_b_re", (1, 32, 64, 16)), ("s5_b_im", (1, 32, 64, 16)), ("s5_c_re", (1, 32, 16, 64)), ("s5_c_im", (1, 32, 16, 64)),
              ("s5_d", (1, 32, 16)), ("sgu_w_s", (1, 8, 128, 128)), ("sgu_b_s", (1, 8, 128))]
SMALL_ROWS = 512
BIG_ADAM = {"ada_w": (2048, 1536), "even_w_in": (1024, 902), "even_w_out": (256, 1024), "odd_w_in": (1024, 384),
            "odd_w_out": (256, 1024), "mlp_w1": (2048, 1024), "mlp_w2": (2048, 1024), "s5_w_glu": (128, 512)}


PACK_ALIGN = 16


def _piece_rows(shape):
    rows = -(-math.prod(shape) // PACK_COLS)
    return -(-rows // PACK_ALIGN) * PACK_ALIGN


def _to_rows(p, lead=()):
    n = math.prod(p.shape[len(lead):])
    rows = _piece_rows(p.shape[len(lead):])
    flat = p.reshape(lead + (n,))
    if rows * PACK_COLS != n:
        flat = jnp.pad(flat, [(0, 0)] * len(lead) + [(0, rows * PACK_COLS - n)])
    return flat.reshape(lead + (rows, PACK_COLS))


def _from_rows(x, r0, shape, lead=()):
    n = math.prod(shape)
    seg = lax.slice_in_dim(x, r0, r0 + _piece_rows(shape), axis=len(lead)).reshape(lead + (-1,))
    return lax.slice_in_dim(seg, 0, n, axis=len(lead)).reshape(lead + tuple(shape))


def _pack_rows(pieces, rows):
    x = jnp.concatenate([_to_rows(p) for p in pieces], axis=0)
    return jnp.pad(x, ((0, rows - x.shape[0]), (0, 0)))


def _unpack(x, specs):
    out, r0 = {}, 0
    for name, shape in specs:
        out[name] = _from_rows(x, r0, shape)
        r0 += _piece_rows(shape)
    return out


def _shards_to_full(x4, pieces):
    out, r0 = {}, 0
    for name, shape, axis in pieces:
        seg = _from_rows(x4, r0, shape, lead=(N_CHIPS,))
        out[name] = jnp.concatenate([seg[k] for k in range(N_CHIPS)], axis=axis)
        r0 += _piece_rows(shape)
    return out


def _full_to_shards(full, pieces, rows):
    blocks = [_to_rows(jnp.stack(jnp.split(full[name], N_CHIPS, axis=axis)), lead=(N_CHIPS,)) for name, _, axis in pieces]
    x = jnp.concatenate(blocks, axis=1)
    return jnp.pad(x, ((0, 0), (0, rows - x.shape[1]), (0, 0)))


def _gather_prep(local, pieces, rows):
    shard = _pack_rows([local[n] for n, _, _ in pieces], rows).astype(BF16)
    return lax.dynamic_slice_in_dim(shard, lax.axis_index("c") * (rows // 2), rows // 2, axis=0)


def _gather_finish(collected, pieces, rows, tag):
    halves = _by_core(collected, _pair_swap(collected, tag + "_pair"))
    return _shards_to_full(halves.transpose(1, 0, 2, 3).reshape(N_CHIPS, rows, PACK_COLS), pieces)


def _reduce_prep(full, pieces, rows, tag):
    mc = lax.axis_index("c")
    packed = _full_to_shards(full, pieces, rows)
    hr = rows // 2
    mine = lax.dynamic_slice_in_dim(packed, mc * hr, hr, axis=1)
    other = lax.dynamic_slice_in_dim(packed, (1 - mc) * hr, hr, axis=1)
    theirs = _pair_swap(other.astype(BF16), tag + "_pair")
    pair_sum, = _ew(lambda p, q: p + q, tag + "_pair_sum", [mine.reshape(N_CHIPS * hr, PACK_COLS), theirs.reshape(N_CHIPS * hr, PACK_COLS)],
                    outs=[(PACK_COLS, BF16)])
    return pair_sum.reshape(N_CHIPS, hr, PACK_COLS)


def _reduce_finish(arrived, pieces, rows, tag):
    red_half = _sum_slots(arrived, tag + "_chip_sum")
    reduced = _by_core(red_half, _pair_swap(red_half, tag + "_pair_out")).reshape(rows, PACK_COLS)
    return _unpack(reduced, [(n, s) for n, s, _ in pieces])


def _relu2(t):
    r = jnp.maximum(t, 0.0)
    return r * r


def _silu(t):
    return t * _sigmoid(t)


def _pack_even(w):
    return jnp.concatenate([w[:, :2048], w[:, 2064:3600], w[:, 2048:2064], w[:, 3600:3608],
                            jnp.zeros((w.shape[0], EVEN_PAD - EVEN_COLS), w.dtype)], axis=1)


def _unpack_even(wp):
    return jnp.concatenate([wp[:, :2048], wp[:, 3584:3600], wp[:, 2048:3584], wp[:, 3600:3608]], axis=1)


def _mlp_fwd(h, w1, w2, pfx):
    pre = _mm(h, w1, "nn", pfx + "_up", out_dtype=BF16)
    return pre, _mm(pre, w2, "nn", pfx + "_down", a_pro=_relu2, out_dtype=BF16)


def _mlp_bwd(dm, h, pre, w1, w2, pfx):
    dpre = _mm(dm, w2, "nt", pfx + "_dpre", epi=lambda acc, p: acc * (2.0 * jnp.maximum(p, 0.0)), extras=[(pre, "mn")], out_dtype=BF16)
    dw2 = _mm(pre, dm, "tn", pfx + "_dw2", a_pro=_relu2)
    dw1 = _mm(h, dpre, "tn", pfx + "_dw1")
    dh = _mm(dpre, w1, "nt", pfx + "_dh", out_dtype=BF16)
    return dh, dw1, dw2


def _step(args):
    a = dict(zip(ARGS, args, strict=True))
    x0 = a["x"][0]
    target = a["loss_target"][0]
    mx, my, mc = lax.axis_index("x"), lax.axis_index("y"), lax.axis_index("c")
    chip = 2 * mx + my
    dev = 2 * chip + mc
    bavg = _head_avg_matrix(MIX_HALF)

    c_all = _gather8(jnp.pad(a["c"], ((0, 7), (0, 0))), "c_gather")[:, :, 0, :].reshape(2 * N_CHIPS, D_MODEL)
    ada_b_shard = lax.dynamic_slice_in_dim(a["ada_b"], chip * MOD_SHARD, MOD_SHARD, axis=1)
    mod_sh = [_mm(c_all, a["ada_w"][l], "nn", f"mod{l}", a_pro=_silu, epi=lambda acc, b: acc + b, extras=[(ada_b_shard[l:l + 1], "n")])
              for l in range(2)]
    small3 = jnp.zeros((8, MOD_SHARD), F32)
    for r, n in enumerate(("s5_b_glu", "sgu_ln_gain", "sgu_ln_bias")):
        small3 = small3.at[r, :LANES].set(a[n][0])
    mod_all = _chip_exchange(jnp.concatenate(mod_sh + [small3]), "mod_gather", True)
    mods = []
    for l in range(2):
        full = mod_all[:, 8 * l:8 * l + 8].transpose(1, 0, 2).reshape(8, MOD)
        mods.append(jnp.split(lax.dynamic_slice_in_dim(full, dev, 1, axis=0), 6, axis=1))
    b_glu, ln_gain, ln_bias = (mod_all[:, 16 + r, :LANES].reshape(1, MIX_HALF) for r in range(3))

    local = dict(a, even_w_in=jnp.pad(a["even_w_in"], ((0, 0), (0, 0), (0, PACK_COLS - EVEN_SHARD))),
                 mlp_w1_0=a["mlp_w1"][0:1], mlp_w1_1=a["mlp_w1"][1:2], mlp_w2_0=a["mlp_w2"][0:1], mlp_w2_1=a["mlp_w2"][1:2])
    (pieces0, rows0), (pieces1, rows1) = SHARDED
    w = _gather_finish(_chip_exchange(_gather_prep(local, pieces0, rows0), "w0_chips", True), pieces0, rows0, "w0")
    w_even = _pack_even(w["even_w_in"][0].reshape(D_MODEL, N_CHIPS, PACK_COLS)[:, :, :EVEN_SHARD].reshape(D_MODEL, EVEN_COLS))
    w_lr_pad = jnp.zeros((MIX_HALF, MIX_HALF), BF16).at[:GLA_RANK].set(w["gla_w_lr"][0])
    gla_b_lr = a["gla_b_lr"]
    gla_gain, q_gain, k_gain = (a[n].reshape(1, MIX_HALF) for n in ("gla_gain", "fox_q_gain", "fox_k_gain"))
    sgu_wm, sgu_bt = _sgu_tables(a["sgu_w_s"][0], a["sgu_b_s"][0])

    sh1, sc1, g1, sh2, sc2, g2 = mods[0]
    _, h1_0 = _res_rms(x0, sc1, sh1, "l0_norm1")
    proj0 = _mm(h1_0, w_even, "nn", "l0_proj")
    og, gla_res = _gla_block_fwd(proj0, w_lr_pad, gla_b_lr, gla_gain, bavg, "gla")
    of, fox_res, (collected1,) = _fox_block_fwd(proj0, a["fox_b_f"][0], q_gain, k_gain, bavg, "fox",
                                                carry=(_gather_prep(local, pieces1, rows1), True))
    w.update(_gather_finish(collected1, pieces1, rows1, "w1"))
    s5w = dict(lam_re=a["s5_lam_re"][0], lam_im=a["s5_lam_im"][0], log_dt=a["s5_log_dt"][0], b_re=a["s5_b_re"][0], b_im=a["s5_b_im"][0],
               c_re=a["s5_c_re"][0], c_im=a["s5_c_im"][0], d=a["s5_d"][0], w_glu=w["s5_w_glu"][0], b_glu=b_glu)
    mixed0 = jnp.concatenate([og, of], axis=1).astype(BF16)
    y0 = _mm(mixed0, w["even_w_out"][0], "nn", "l0_out", out_dtype=BF16)
    x1, h2_0 = _res_rms(x0, sc2, sh2, "l0_norm2", y=y0, g=g1)
    pre0, m0 = _mlp_fwd(h2_0, w["mlp_w1_0"][0], w["mlp_w2_0"][0], "l0_mlp")
    sh1b, sc1b, g1b, sh2b, sc2b, g2b = mods[1]
    x2, h1_1 = _res_rms(x1, sc1b, sh1b, "l1_norm1", y=m0, g=g2)
    proj1 = _mm(h1_1, w["odd_w_in"][0], "nn", "l1_proj")
    ys5, s5_res = _s5_block_fwd(proj1[:, :MIX_HALF], s5w, "s5")
    ysgu = _sgu_fwd(proj1, ln_gain, ln_bias, sgu_wm, sgu_bt, "sgu")
    mixed1 = jnp.concatenate([ys5, ysgu], axis=1).astype(BF16)
    y1 = _mm(mixed1, w["odd_w_out"][0], "nn", "l1_out", out_dtype=BF16)
    x3, h2_1 = _res_rms(x2, sc2b, sh2b, "l1_norm2", y=y1, g=g1b)
    pre1, m1 = _mlp_fwd(h2_1, w["mlp_w1_1"][0], w["mlp_w2_1"][0], "l1_mlp")
    loss_b, dx4, dm1, dg2b = _res_loss(x3, m1, g2b, target, "loss")
    loss = lax.psum(loss_b[0, 0], ("x", "y", "c"))

    full = {}
    dh2_1, dw1_1, dw2_1 = _mlp_bwd(dm1, h2_1, pre1, w["mlp_w1_1"][0], w["mlp_w2_1"][0], "l1_mlp")
    dx3, dy1, dg1b, dsc2b, dsh2b = _res_rms_bwd(x3, dh2_1, sc2b, dx4, "l1_norm2_bwd", y=y1, g=g1b)
    dmixed1 = _mm(dy1, w["odd_w_out"][0], "nt", "l1_out_dx")
    full["odd_w_out"] = _mm(mixed1, dy1, "tn", "l1_out_dw")[None]
    du, s5g = _s5_block_bwd(dmixed1[:, :MIX_HALF], s5w, s5_res, "s5")
    dzu, dzv, dws, dbt, dlg, dlb = _sgu_bwd(dmixed1[:, MIX_HALF:], proj1, ln_gain, ln_bias, sgu_wm, sgu_bt, "sgu_bwd")
    g_ws, g_bs = _sgu_grads(dws, dbt)
    dproj1 = jnp.concatenate([du, dzu, dzv], axis=1).astype(BF16)
    full["odd_w_in"] = _mm(h1_1, dproj1, "tn", "l1_proj_dw")[None]
    dh1_1 = _mm(dproj1, w["odd_w_in"][0], "nt", "l1_proj_dx", out_dtype=BF16)
    dx2, dm0, dg2, dsc1b, dsh1b = _res_rms_bwd(x2, dh1_1, sc1b, dx3, "l1_norm1_bwd", y=m0, g=g2)
    dh2_0, dw1_0, dw2_0 = _mlp_bwd(dm0, h2_0, pre0, w["mlp_w1_0"][0], w["mlp_w2_0"][0], "l0_mlp")
    full.update(mlp_w1_0=dw1_0[None], mlp_w2_0=dw2_0[None], mlp_w1_1=dw1_1[None], mlp_w2_1=dw2_1[None], s5_w_glu=s5g["w_glu"][None],
                s5_b_glu=s5g["b_glu"][None], sgu_ln_gain=dlg, sgu_ln_bias=dlb)
    pair_sums1 = _reduce_prep(full, pieces1, rows1, "g1")
    dx1, dy0, dg1, dsc2, dsh2 = _res_rms_bwd(x1, dh2_0, sc2, dx2, "l0_norm2_bwd", y=y0, g=g1)
    dmixed0 = _mm(dy0, w["even_w_out"][0], "nt", "l0_out_dx")
    full["even_w_out"] = _mm(mixed0, dy0, "tn", "l0_out_dw")[None]
    (dgq, dgk, dgv, dgg, dsmall), glag = _gla_block_bwd((dmixed0, MIX_HALF, 0), proj0, w_lr_pad, gla_gain, bavg, gla_res, "gla")
    (dfq, dfk, dfv, dff), foxg, (arrived1,) = _fox_block_bwd((dmixed0, MIX_HALF, 1), proj0, q_gain, k_gain, bavg, fox_res, "fox",
                                                           carry=(pair_sums1, False))
    dsmall = lax.dynamic_update_slice(dsmall, dff, (0, GLA_RANK))
    dproj0 = jnp.concatenate([dgq, dgk, dgv, dgg, dfq, dfk, dfv, dsmall], axis=1).astype(BF16)
    d_even = _unpack_even(_mm(h1_0, dproj0, "tn", "l0_proj_dw")).reshape(D_MODEL, N_CHIPS, EVEN_SHARD)
    full["even_w_in"] = jnp.pad(d_even, ((0, 0), (0, 0), (0, PACK_COLS - EVEN_SHARD))).reshape(1, D_MODEL, N_CHIPS * PACK_COLS)
    dh1_0 = _mm(dproj0, w_even, "nt", "l0_proj_dx", out_dtype=BF16)
    grad_x, dsc1, dsh1 = _res_rms_bwd(x0, dh1_0, sc1, dx1, "l0_norm1_bwd")
    full["gla_w_lr"] = glag["w_lr"][None]

    dmod = jnp.concatenate([dsh1, dsc1, dg1, dsh2, dsc2, dg2, dsh1b, dsc1b, dg1b, dsh2b, dsc2b, dg2b], axis=1)
    dmod_all = _gather8(jnp.pad(dmod, ((0, 7), (0, 0))), "dmod_gather")[:, :, 0, :].reshape(2 * N_CHIPS, 2, MOD)
    grads = {}
    grads["ada_w"] = jnp.stack([
        _mm(c_all, lax.dynamic_slice_in_dim(dmod_all[:, l], chip * MOD_SHARD, MOD_SHARD, axis=1), "tn", f"ada_dw{l}", a_pro=_silu)
        for l in range(2)])
    grads["ada_b"] = _sum_slots(dmod_all.reshape(2 * N_CHIPS, 2 * MOD // MIX_HALF, MIX_HALF), "ada_db").reshape(2, MOD)

    grads.update(_reduce_finish(arrived1, pieces1, rows1, "g1"))
    grads.update(_reduce_finish(_chip_exchange(_reduce_prep(full, pieces0, rows0, "g0"), "g0_chips", False), pieces0, rows0, "g0"))
    grads["even_w_in"] = grads["even_w_in"][:, :, :EVEN_SHARD]
    grads["mlp_w1"] = jnp.concatenate([grads.pop("mlp_w1_0"), grads.pop("mlp_w1_1")])
    grads["mlp_w2"] = jnp.concatenate([grads.pop("mlp_w2_0"), grads.pop("mlp_w2_1")])

    part = dict(gla_b_lr=glag["b_lr"], gla_gain=glag["gain"], fox_b_f=foxg["b_f"], fox_q_gain=foxg["q_gain"], fox_k_gain=foxg["k_gain"],
                s5_lam_re=s5g["lam_re"], s5_lam_im=s5g["lam_im"], s5_log_dt=s5g["log_dt"], s5_b_re=s5g["b_re"], s5_b_im=s5g["b_im"],
                s5_c_re=s5g["c_re"], s5_c_im=s5g["c_im"], s5_d=s5g["d"], sgu_w_s=g_ws, sgu_b_s=g_bs)
    parts_all = _gather8(_pack_rows([part[n] for n, _ in REPLICATED], SMALL_ROWS).astype(BF16), "rep_gather")
    rep = _sum_slots(parts_all.reshape(2 * N_CHIPS, SMALL_ROWS, PACK_COLS), "rep_sum")
    grads.update(_unpack(rep, REPLICATED))

    delta, new_m, new_v = {}, {}, {}
    for n, shape2 in BIG_ADAM.items():
        d, nm, nv = _adamw(a[n].reshape(shape2), grads[n].reshape(shape2), a["m_" + n].reshape(shape2), a["v_" + n].reshape(shape2), "adamw_" + n)
        delta[n], new_m[n], new_v[n] = (t.reshape(a[n].shape) for t in (d, nm, nv))
    small = [n for n in WEIGHTS if n not in BIG_ADAM]
    spec = [(n, a[n].shape) for n in small]
    packs = [_pack_rows([src[n] for n in small], SMALL_ROWS) for src in
             (a, grads, {n: a["m_" + n] for n in small}, {n: a["v_" + n] for n in small})]
    for tgt, res in zip((delta, new_m, new_v), _adamw(*packs, "adamw_small")):
        tgt.update(_unpack(res, spec))
    outs = [loss, grad_x[None]]
    for group in (grads, delta, new_m, new_v):
        outs += [group[n].reshape(a[n].shape) for n in WEIGHTS]
    return tuple(outs)


def kernel(x, c, ada_w, ada_b, even_w_in, even_w_out, gla_w_lr, gla_b_lr, gla_gain, fox_b_f, fox_q_gain, fox_k_gain, odd_w_in,
           odd_w_out, s5_lam_re, s5_lam_im, s5_log_dt, s5_b_re, s5_b_im, s5_c_re, s5_c_im, s5_d, s5_w_glu, s5_b_glu, sgu_ln_gain,
           sgu_ln_bias, sgu_w_s, sgu_b_s, mlp_w1, mlp_w2, loss_target, m_ada_w, m_ada_b, m_even_w_in, m_even_w_out, m_gla_w_lr,
           m_gla_b_lr, m_gla_gain, m_fox_b_f, m_fox_q_gain, m_fox_k_gain, m_odd_w_in, m_odd_w_out, m_s5_lam_re, m_s5_lam_im,
           m_s5_log_dt, m_s5_b_re, m_s5_b_im, m_s5_c_re, m_s5_c_im, m_s5_d, m_s5_w_glu, m_s5_b_glu, m_sgu_ln_gain, m_sgu_ln_bias,
           m_sgu_w_s, m_sgu_b_s, m_mlp_w1, m_mlp_w2, v_ada_w, v_ada_b, v_even_w_in, v_even_w_out, v_gla_w_lr, v_gla_b_lr,
           v_gla_gain, v_fox_b_f, v_fox_q_gain, v_fox_k_gain, v_odd_w_in, v_odd_w_out, v_s5_lam_re, v_s5_lam_im, v_s5_log_dt,
           v_s5_b_re, v_s5_b_im, v_s5_c_re, v_s5_c_im, v_s5_d, v_s5_w_glu, v_s5_b_glu, v_sgu_ln_gain, v_sgu_ln_bias, v_sgu_w_s,
           v_sgu_b_s, v_mlp_w1, v_mlp_w2):
    return _step((x, c, ada_w, ada_b, even_w_in, even_w_out, gla_w_lr, gla_b_lr, gla_gain, fox_b_f, fox_q_gain, fox_k_gain,
                  odd_w_in, odd_w_out, s5_lam_re, s5_lam_im, s5_log_dt, s5_b_re, s5_b_im, s5_c_re, s5_c_im, s5_d, s5_w_glu,
                  s5_b_glu, sgu_ln_gain, sgu_ln_bias, sgu_w_s, sgu_b_s, mlp_w1, mlp_w2, loss_target, m_ada_w, m_ada_b,
                  m_even_w_in, m_even_w_out, m_gla_w_lr, m_gla_b_lr, m_gla_gain, m_fox_b_f, m_fox_q_gain, m_fox_k_gain,
                  m_odd_w_in, m_odd_w_out, m_s5_lam_re, m_s5_lam_im, m_s5_log_dt, m_s5_b_re, m_s5_b_im, m_s5_c_re, m_s5_c_im,
                  m_s5_d, m_s5_w_glu, m_s5_b_glu, m_sgu_ln_gain, m_sgu_ln_bias, m_sgu_w_s, m_sgu_b_s, m_mlp_w1, m_mlp_w2, v_ada_w,
                  v_ada_b, v_even_w_in, v_even_w_out, v_gla_w_lr, v_gla_b_lr, v_gla_gain, v_fox_b_f, v_fox_q_gain, v_fox_k_gain,
                  v_odd_w_in, v_odd_w_out, v_s5_lam_re, v_s5_lam_im, v_s5_log_dt, v_s5_b_re, v_s5_b_im, v_s5_c_re, v_s5_c_im,
                  v_s5_d, v_s5_w_glu, v_s5_b_glu, v_sgu_ln_gain, v_sgu_ln_bias, v_sgu_w_s, v_sgu_b_s, v_mlp_w1, v_mlp_w2))
```
